```python
import math
import jax, jax.numpy as jnp
from jax import lax
import numpy as np


D_MODEL = 1024
BATCH = 4
SEQ = 4096
DEPTH = 1
DEC_BATCH = 128
DEC_SEQ = 4
PAST_LEN = 2048
PAGE_SIZE = 128

HEAD_DIM = 64
MIX_DIM = D_MODEL
CONV_DIM = MIX_DIM // 2
CONV_K = 3
ATTN_DIM = MIX_DIM - CONV_DIM
N_HEADS = ATTN_DIM // HEAD_DIM
N_KV_HEADS = 2
GROUP = N_HEADS // N_KV_HEADS
KV_DIM = N_KV_HEADS * HEAD_DIM
N_BRANCH = 3
CMP_BLOCK = 32
CMP_STRIDE = 16
CMP_HIDDEN = 256
SLC_BLOCK = 64
N_SEL = 16
WINDOW = 512
Q_BLOCK = 128
FORCE_SCORE = 1e4
NUM_BUCKETS = 32
MAX_DISTANCE = 128
N_GROUPS = 4
EXPERTS_PER_GROUP = 8
N_EXPERTS = N_GROUPS * EXPERTS_PER_GROUP
TOP_K = 2
D_EXPERT = 512
EXPERT_BLOCK = 128
PLE_DIM = 256
EPS = 1e-6
W_IN_COLS = 3 * CONV_DIM + ATTN_DIM + 6 * KV_DIM + N_BRANCH * N_HEADS

kernel_name = 'hymba_conv_nsa_hmoe_step'


def rms_norm(x, g):
    x32 = x.astype(jnp.float32)
    y = x32 * lax.rsqrt(jnp.mean(x32 * x32, axis=-1, keepdims=True) + EPS)
    return (y * g.astype(jnp.float32)).astype(x.dtype)


def rel_bucket(dist):
    n = jnp.maximum(dist, 0)
    max_exact = NUM_BUCKETS // 2
    nf = jnp.maximum(n, 1).astype(jnp.float32)
    large = max_exact + (jnp.log(nf / max_exact) / math.log(MAX_DISTANCE / max_exact) * (NUM_BUCKETS - max_exact)).astype(jnp.int32)
    large = jnp.minimum(large, NUM_BUCKETS - 1)
    return jnp.where(n < max_exact, n, large)


def masked_softmax(s, mask):
    s = jnp.where(mask, s, -jnp.inf)
    m = jnp.max(s, axis=-1, keepdims=True)
    m = jnp.where(jnp.isfinite(m), m, 0.0)
    p = jnp.exp(s - m)
    return p / jnp.maximum(jnp.sum(p, axis=-1, keepdims=True), 1e-30)


def _split_points():
    sizes = [CONV_DIM] * 3 + [ATTN_DIM] + [KV_DIM] * 6 + [N_BRANCH * N_HEADS]
    return [int(v) for v in np.cumsum(sizes)[:-1]]


def _project(x, w_in_l, norm_l, qn_l, kn_l):
    B, T, _ = x.shape
    z = rms_norm(x, norm_l) @ w_in_l
    xh, bg, cg, q, kc, vc, ks, vs, kw, vw, gt = jnp.split(z, _split_points(), axis=-1)
    kv = lambda t: t.reshape(B, T, N_KV_HEADS, HEAD_DIM)
    q = rms_norm(q.reshape(B, T, N_HEADS, HEAD_DIM), qn_l)
    gates = jax.nn.sigmoid(gt.astype(jnp.float32)).reshape(B, T, N_HEADS, N_BRANCH)
    return (cg * xh, bg, q, kv(kc), kv(vc), rms_norm(kv(ks), kn_l[1]), kv(vs),
            rms_norm(kv(kw), kn_l[2]), kv(vw), gates)


def short_conv(u_prev, u, w):
    T = u.shape[1]
    up = jnp.concatenate([u_prev, u], axis=1)
    y = w[0] * up[:, 0:T]
    for k in range(1, CONV_K):
        y = y + w[k] * up[:, k:k + T]
    return y, up[:, T:]


def compress(rows, pe, w1, w2):
    B, Tk = rows.shape[:2]
    n_chunk = Tk // CMP_STRIDE
    r = CMP_BLOCK // CMP_STRIDE
    n_cmp = n_chunk - r + 1
    chunks = rows[:, :n_chunk * CMP_STRIDE].reshape(B, n_chunk, CMP_STRIDE, N_KV_HEADS, HEAD_DIM)
    blocks = jnp.concatenate([chunks[:, j:j + n_cmp] for j in range(r)], axis=2) + pe[:, None, :]
    flat = blocks.transpose(0, 1, 3, 2, 4).reshape(B, n_cmp, N_KV_HEADS, CMP_BLOCK * HEAD_DIM)
    return jax.nn.gelu(flat @ w1) @ w2


def nsa_attend(q, gates, pos_q, k_cmp, v_cmp, k_slc, v_slc, k_win, v_win, pos_win, rel_bias):
    B, Tq = q.shape[:2]
    scale = HEAD_DIM ** -0.5
    qg = q.reshape(B, Tq, N_KV_HEADS, GROUP, HEAD_DIM)
    rb = rel_bias.astype(jnp.float32).reshape(NUM_BUCKETS, N_KV_HEADS, GROUP)

    def dense_bias(dist):
        return rb[rel_bucket(dist)].transpose(0, 2, 3, 1)[None]

    n_cmp = k_cmp.shape[1]
    c_start = jnp.arange(n_cmp, dtype=jnp.int32) * CMP_STRIDE
    d_c = pos_q[:, None] - (c_start + CMP_BLOCK - 1)[None, :]
    s_c = jnp.einsum('bqhgd,bnhd->bqhgn', qg, k_cmp).astype(jnp.float32) * scale + dense_bias(d_c)
    p_c = masked_softmax(s_c, (d_c >= 0)[None, :, None, None, :])
    o_c = jnp.einsum('bqhgn,bnhd->bqhgd', p_c.astype(v_cmp.dtype), v_cmp)

    Tk = k_slc.shape[1]
    n_slc = -(-Tk // SLC_BLOCK)
    pad = n_slc * SLC_BLOCK - Tk
    s_start = jnp.arange(n_slc, dtype=jnp.int32) * SLC_BLOCK
    overlap = ((c_start[:, None] < s_start[None, :] + SLC_BLOCK) &
               (c_start[:, None] + CMP_BLOCK > s_start[None, :])).astype(jnp.float32)
    imp = jnp.einsum('bqhgn,ns->bqhs', p_c, overlap)
    q_blk = pos_q // SLC_BLOCK
    j = jnp.arange(n_slc, dtype=jnp.int32)[None, :]
    forced = (j == 0) | (j == q_blk[:, None]) | (j == q_blk[:, None] - 1)
    valid = s_start[None, :] <= pos_q[:, None]
    imp = jnp.where(valid[None, :, None, :], imp + jnp.where(forced, FORCE_SCORE, 0.0)[None, :, None, :], -jnp.inf)
    n_sel = min(N_SEL, n_slc)
    _, idx = lax.top_k(imp, n_sel)
    idx = idx.transpose(0, 2, 1, 3)

    def blocks_of(t):
        t = jnp.pad(t, ((0, 0), (0, pad), (0, 0), (0, 0)))
        return t.reshape(B, n_slc, SLC_BLOCK, N_KV_HEADS, HEAD_DIM).transpose(0, 3, 1, 2, 4)

    bi = jnp.arange(B)[:, None, None, None]
    hi = jnp.arange(N_KV_HEADS)[None, :, None, None]
    nk = n_sel * SLC_BLOCK
    k_sel = blocks_of(k_slc)[bi, hi, idx].reshape(B, N_KV_HEADS, Tq, nk, HEAD_DIM)
    v_sel = blocks_of(v_slc)[bi, hi, idx].reshape(B, N_KV_HEADS, Tq, nk, HEAD_DIM)
    pos_sel = (idx[..., None] * SLC_BLOCK + jnp.arange(SLC_BLOCK, dtype=jnp.int32)).reshape(B, N_KV_HEADS, Tq, nk)
    d_s = pos_q[None, None, :, None] - pos_sel
    bias_s = rb[rel_bucket(d_s), hi].transpose(0, 2, 1, 4, 3)
    s_s = jnp.einsum('bqhgd,bhqkd->bqhgk', qg, k_sel).astype(jnp.float32) * scale + bias_s
    p_s = masked_softmax(s_s, (d_s >= 0).transpose(0, 2, 1, 3)[:, :, :, None, :])
    o_s = jnp.einsum('bqhgk,bhqkd->bqhgd', p_s.astype(v_sel.dtype), v_sel)

    d_w = pos_q[:, None] - pos_win[None, :]
    mask_w = (d_w >= 0) & (d_w < WINDOW) & (pos_win[None, :] >= 0)
    s_w = jnp.einsum('bqhgd,bkhd->bqhgk', qg, k_win).astype(jnp.float32) * scale + dense_bias(d_w)
    p_w = masked_softmax(s_w, mask_w[None, :, None, None, :])
    o_w = jnp.einsum('bqhgk,bkhd->bqhgd', p_w.astype(v_win.dtype), v_win)

    g = gates.reshape(B, Tq, N_KV_HEADS, GROUP, N_BRANCH)
    o = g[..., 0:1] * o_c + g[..., 1:2] * o_s + g[..., 2:3] * o_w
    return o.reshape(B, Tq, ATTN_DIM).astype(q.dtype)


def nsa_prompt(q, gates, k_cmp, v_cmp, k_slc, v_slc, k_win, v_win, rel_bias):
    B, T = q.shape[:2]
    n_blk = T // Q_BLOCK
    band = WINDOW + Q_BLOCK
    pad = lambda t: jnp.pad(t, ((0, 0), (WINDOW, 0), (0, 0), (0, 0)))
    kw_p, vw_p = pad(k_win), pad(v_win)

    def one(i):
        s = i * Q_BLOCK
        sl = lambda t, n: lax.dynamic_slice_in_dim(t, s, n, axis=1)
        pos_q = s + jnp.arange(Q_BLOCK, dtype=jnp.int32)
        pos_w = s - WINDOW + jnp.arange(band, dtype=jnp.int32)
        return nsa_attend(sl(q, Q_BLOCK), sl(gates, Q_BLOCK), pos_q, k_cmp, v_cmp, k_slc, v_slc,
                          sl(kw_p, band), sl(vw_p, band), pos_w, rel_bias)

    o = lax.map(one, jnp.arange(n_blk, dtype=jnp.int32))
    return o.transpose(1, 0, 2, 3).reshape(B, T, ATTN_DIM)


def moe_ffn(h, w_rg, b_rg, w_re, b_re, w_g, w_u, w_d):
    N, D = h.shape
    p_grp = jax.nn.softmax((h @ w_rg).astype(jnp.float32) + b_rg.astype(jnp.float32), axis=-1)
    grp = jnp.argmax(p_grp, axis=-1)
    gw = jnp.take_along_axis(p_grp, grp[:, None], axis=1)[:, 0]
    le = ((h @ w_re).astype(jnp.float32) + b_re.astype(jnp.float32)).reshape(N, N_GROUPS, EXPERTS_PER_GROUP)
    le_g = jnp.take_along_axis(le, grp[:, None, None], axis=1)[:, 0]
    w2, e_loc = lax.top_k(jax.nn.softmax(le_g, axis=-1), TOP_K)
    w2 = w2 / jnp.sum(w2, axis=-1, keepdims=True) * gw[:, None]
    e_idx = grp[:, None] * EXPERTS_PER_GROUP + e_loc
    A = N * TOP_K
    flat_e = e_idx.reshape(-1).astype(jnp.int32)
    flat_w = w2.reshape(-1)
    flat_tok = jnp.arange(A, dtype=jnp.int32) // TOP_K
    counts = jnp.bincount(flat_e, length=N_EXPERTS)
    padded = (counts + EXPERT_BLOCK - 1) // EXPERT_BLOCK * EXPERT_BLOCK
    pad_end = jnp.cumsum(padded)
    pad_start = pad_end - padded
    seg_start = jnp.cumsum(counts) - counts
    order = jnp.argsort(flat_e)
    e_sorted = flat_e[order]
    tok_sorted = flat_tok[order]
    dest = pad_start[e_sorted] + jnp.arange(A, dtype=jnp.int32) - seg_start[e_sorted]
    n_blk = (A + N_EXPERTS * (EXPERT_BLOCK - 1) + EXPERT_BLOCK - 1) // EXPERT_BLOCK
    P = n_blk * EXPERT_BLOCK
    slot_tok = jnp.full((P,), N, jnp.int32).at[dest].set(tok_sorted)
    h_pad = jnp.concatenate([h, jnp.zeros((1, D), h.dtype)], axis=0)
    xb = h_pad[slot_tok].reshape(n_blk, EXPERT_BLOCK, D)
    blk_expert = jnp.clip(jnp.searchsorted(pad_end, jnp.arange(n_blk, dtype=jnp.int32) * EXPERT_BLOCK, side='right'), 0, N_EXPERTS - 1)

    def run(args):
        xe, e = args
        return (jax.nn.silu(xe @ w_g[e]) * (xe @ w_u[e])) @ w_d[e]

    yb = lax.map(run, (xb, blk_expert)).reshape(P, D)
    y_assign = yb[dest].astype(jnp.float32) * flat_w[order][:, None]
    return jax.ops.segment_sum(y_assign, tok_sorted, num_segments=N).astype(h.dtype)


def _post(x, conv_out, attn_out, p, w_out_l, norm_ffn_l, moe_w, norm_ple_l, w_ple_l, w_pg_l, b_pg_l):
    h = x + jnp.concatenate([conv_out, attn_out], axis=-1) @ w_out_l
    hn = rms_norm(h, norm_ffn_l)
    h = h + moe_ffn(hn.reshape(-1, D_MODEL), *moe_w).reshape(h.shape)
    gate = jax.nn.sigmoid((rms_norm(h, norm_ple_l) @ w_pg_l).astype(jnp.float32) + b_pg_l.astype(jnp.float32))
    return h + (gate * (p @ w_ple_l).astype(jnp.float32)).astype(h.dtype)


def setup_inputs(seed: int = 0) -> dict:
    key = jax.random.key(seed)
    keys = iter(jax.random.split(key, 48))

    def nrm(shape, scale):
        return jax.random.normal(next(keys), shape, jnp.float32) * scale

    def gain(shape):
        return 1.0 + nrm(shape, 0.1)

    n_pages = PAST_LEN // PAGE_SIZE
    n_used = DEC_BATCH * n_pages
    n_phys = n_used + max(1, n_used // 4)
    w_buf = min(WINDOW, PAST_LEN)
    page_shape = (DEPTH, n_phys, PAGE_SIZE, N_KV_HEADS, HEAD_DIM)
    win_shape = (DEPTH, DEC_BATCH, w_buf, N_KV_HEADS, HEAD_DIM)
    return {
        'x_prompt': nrm((BATCH, SEQ, D_MODEL), 1.0),
        'x_sample': nrm((DEC_BATCH, DEC_SEQ, D_MODEL), 1.0),
        'p_prompt': nrm((DEPTH, BATCH, SEQ, PLE_DIM), 1.0),
        'p_sample': nrm((DEPTH, DEC_BATCH, DEC_SEQ, PLE_DIM), 1.0),
        'cache_k_cmp': nrm(page_shape, 1.0),
        'cache_v_cmp': nrm(page_shape, 1.0),
        'cache_k_slc': nrm(page_shape, 1.0),
        'cache_v_slc': nrm(page_shape, 1.0),
        'cache_k_win': nrm(win_shape, 1.0),
        'cache_v_win': nrm(win_shape, 1.0),
        'state_conv': nrm((DEPTH, DEC_BATCH, CONV_K - 1, CONV_DIM), 1.0),
        'page_table': jax.random.permutation(next(keys), n_phys)[:n_used].reshape(DEC_BATCH, n_pages).astype(jnp.int32),
        'w_in': nrm((DEPTH, D_MODEL, W_IN_COLS), D_MODEL ** -0.5),
        'w_out': nrm((DEPTH, MIX_DIM, D_MODEL), MIX_DIM ** -0.5),
        'conv_w': nrm((DEPTH, CONV_K, CONV_DIM), CONV_K ** -0.5),
        'norm_mix': gain((DEPTH, D_MODEL)),
        'norm_ffn': gain((DEPTH, D_MODEL)),
        'norm_ple': gain((DEPTH, D_MODEL)),
        'q_norm': gain((DEPTH, HEAD_DIM)),
        'k_norm': gain((DEPTH, N_BRANCH, HEAD_DIM)),
        'cmp_pe_k': nrm((DEPTH, CMP_BLOCK, HEAD_DIM), 0.1),
        'cmp_w1_k': nrm((DEPTH, CMP_BLOCK * HEAD_DIM, CMP_HIDDEN), (CMP_BLOCK * HEAD_DIM) ** -0.5),
        'cmp_w2_k': nrm((DEPTH, CMP_HIDDEN, HEAD_DIM), CMP_HIDDEN ** -0.5),
        'cmp_pe_v': nrm((DEPTH, CMP_BLOCK, HEAD_DIM), 0.1),
        'cmp_w1_v': nrm((DEPTH, CMP_BLOCK * HEAD_DIM, CMP_HIDDEN), (CMP_BLOCK * HEAD_DIM) ** -0.5),
        'cmp_w2_v': nrm((DEPTH, CMP_HIDDEN, HEAD_DIM), CMP_HIDDEN ** -0.5),
        'rel_bias': nrm((NUM_BUCKETS, N_HEADS), 0.5),
        'w_router_group': nrm((DEPTH, D_MODEL, N_GROUPS), D_MODEL ** -0.5),
        'b_router_group': nrm((DEPTH, N_GROUPS), 0.01),
        'w_router_expert': nrm((DEPTH, D_MODEL, N_EXPERTS), D_MODEL ** -0.5),
        'b_router_expert': nrm((DEPTH, N_EXPERTS), 0.01),
        'w_exp_gate': nrm((DEPTH, N_EXPERTS, D_MODEL, D_EXPERT), D_MODEL ** -0.5),
        'w_exp_up': nrm((DEPTH, N_EXPERTS, D_MODEL, D_EXPERT), D_MODEL ** -0.5),
        'w_exp_down': nrm((DEPTH, N_EXPERTS, D_EXPERT, D_MODEL), D_EXPERT ** -0.5),
        'w_ple': nrm((DEPTH, PLE_DIM, D_MODEL), PLE_DIM ** -0.5),
        'w_ple_gate': nrm((DEPTH, D_MODEL, D_MODEL), D_MODEL ** -0.5),
        'b_ple_gate': nrm((DEPTH, D_MODEL), 0.1),
    }


def reference(x_prompt, x_sample, p_prompt, p_sample, cache_k_cmp, cache_v_cmp, cache_k_slc, cache_v_slc,
              cache_k_win, cache_v_win, state_conv, page_table, w_in, w_out, conv_w, norm_mix, norm_ffn, norm_ple,
              q_norm, k_norm, cmp_pe_k, cmp_w1_k, cmp_w2_k, cmp_pe_v, cmp_w1_v, cmp_w2_v, rel_bias,
              w_router_group, b_router_group, w_router_expert, b_router_expert, w_exp_gate, w_exp_up, w_exp_down,
              w_ple, w_ple_gate, b_ple_gate):
    y_p, y_s = x_prompt, x_sample
    Bp, T = x_prompt.shape[:2]
    Bs, Ts = x_sample.shape[:2]
    past_len = page_table.shape[1] * cache_k_cmp.shape[2]
    w_buf = cache_k_win.shape[2]
    new_p = [[] for _ in range(7)]
    new_s = [[] for _ in range(7)]
    for i in range(DEPTH):
        moe_w = (w_router_group[i], b_router_group[i], w_router_expert[i], b_router_expert[i],
                 w_exp_gate[i], w_exp_up[i], w_exp_down[i])
        post_w = (w_out[i], norm_ffn[i], moe_w, norm_ple[i], w_ple[i], w_ple_gate[i], b_ple_gate[i])

        u, bg, q, kc, vc, ks, vs, kw, vw, gts = _project(y_p, w_in[i], norm_mix[i], q_norm[i], k_norm[i])
        conv_y, conv_st = short_conv(jnp.zeros((Bp, CONV_K - 1, CONV_DIM), u.dtype), u, conv_w[i])
        kc_blk = rms_norm(compress(kc, cmp_pe_k[i], cmp_w1_k[i], cmp_w2_k[i]), k_norm[i, 0])
        vc_blk = compress(vc, cmp_pe_v[i], cmp_w1_v[i], cmp_w2_v[i])
        attn = nsa_prompt(q, gts, kc_blk, vc_blk, ks, vs, kw, vw, rel_bias)
        y_p = _post(y_p, bg * conv_y, attn, p_prompt[i], *post_w)
        wp = min(WINDOW, T)
        for lst, val in zip(new_p, (kc, vc, ks, vs, kw[:, T - wp:], vw[:, T - wp:], conv_st)):
            lst.append(val)

        u, bg, q, kc, vc, ks, vs, kw, vw, gts = _project(y_s, w_in[i], norm_mix[i], q_norm[i], k_norm[i])
        conv_y, conv_st = short_conv(state_conv[i], u, conv_w[i])
        gather = lambda c: c[page_table].reshape(Bs, past_len, N_KV_HEADS, HEAD_DIM)
        kc_all = jnp.concatenate([gather(cache_k_cmp[i]), kc], axis=1)
        vc_all = jnp.concatenate([gather(cache_v_cmp[i]), vc], axis=1)
        ks_all = jnp.concatenate([gather(cache_k_slc[i]), ks], axis=1)
        vs_all = jnp.concatenate([gather(cache_v_slc[i]), vs], axis=1)
        kw_all = jnp.concatenate([cache_k_win[i], kw], axis=1)
        vw_all = jnp.concatenate([cache_v_win[i], vw], axis=1)
        kc_blk = rms_norm(compress(kc_all, cmp_pe_k[i], cmp_w1_k[i], cmp_w2_k[i]), k_norm[i, 0])
        vc_blk = compress(vc_all, cmp_pe_v[i], cmp_w1_v[i], cmp_w2_v[i])
        pos_q = past_len + jnp.arange(Ts, dtype=jnp.int32)
        pos_w = past_len - w_buf + jnp.arange(w_buf + Ts, dtype=jnp.int32)
        attn = nsa_attend(q, gts, pos_q, kc_blk, vc_blk, ks_all, vs_all, kw_all, vw_all, pos_w, rel_bias)
        y_s = _post(y_s, bg * conv_y, attn, p_sample[i], *post_w)
        for lst, val in zip(new_s, (kc, vc, ks, vs, kw_all[:, Ts:], vw_all[:, Ts:], conv_st)):
            lst.append(val)

    sp = [jnp.stack(v, axis=0) for v in new_p]
    ss = [jnp.stack(v, axis=0) for v in new_s]
    return (y_p, y_s, sp[0], sp[1], sp[2], sp[3], sp[4], sp[5], sp[6],
            ss[0], ss[1], ss[2], ss[3], ss[4], ss[5], ss[6])
```

```python
import functools
import math

import numpy as np
import jax
import jax.numpy as jnp
from jax import lax
from jax.experimental import pallas as pl
from jax.experimental.pallas import tpu as pltpu

F32 = jnp.float32
BF16 = jnp.bfloat16
NEG_INF = float("-inf")

HEAD_DIM = 64
N_HEADS = 8
N_KV_HEADS = 2
GROUP = N_HEADS // N_KV_HEADS
CONV_DIM = 512
ATTN_DIM = 512
KV_DIM = N_KV_HEADS * HEAD_DIM
N_BRANCH = 3
CONV_K = 3
CMP_BLOCK = 32
CMP_STRIDE = 16
CMP_HIDDEN = 256
SLC_BLOCK = 64
N_SEL = 16
WINDOW = 512
Q_BLOCK = 128
FORCE_SCORE = 1e4
NUM_BUCKETS = 32
MAX_DISTANCE = 128
N_GROUPS = 4
EXPERTS_PER_GROUP = 8
N_EXPERTS = N_GROUPS * EXPERTS_PER_GROUP
D_EXPERT = 512
EPS = 1e-6

LANES = 128
Z_COLS = 3 * CONV_DIM + ATTN_DIM + 6 * KV_DIM + LANES
BIAS_DMAX = 768
EXPERT_ROWS = 256
ROUTER_GROUP_LANE = 32
DMA_WINDOW = 256
VMEM_LIMIT = 56 * 1024 * 1024


def _cparams(*sem):
    return pltpu.CompilerParams(dimension_semantics=sem, vmem_limit_bytes=VMEM_LIMIT)


def _dot(a, b):
    return jnp.dot(a, b, preferred_element_type=F32)


def _dot_nt(a, b):
    return lax.dot_general(a, b, (((1,), (1,)), ((), ())), preferred_element_type=F32)


def _split3(x):
    hi = x.astype(BF16)
    r = x - hi.astype(F32)
    mid = r.astype(BF16)
    lo = (r - mid.astype(F32)).astype(BF16)
    return hi, mid, lo


def _rms(x, g):
    return x * lax.rsqrt(jnp.mean(x * x, axis=-1, keepdims=True) + EPS) * g


def _head_rms(x, bd, g):
    hi, mid, lo = _split3(x * x)
    ss = _dot(hi, bd) + _dot(mid, bd) + _dot(lo, bd)
    return x * lax.rsqrt(ss * (1.0 / HEAD_DIM) + EPS) * g


def _sigmoid(x):
    return 1.0 / (1.0 + jnp.exp(-x))


def _softmax_parts(s):
    m = jnp.max(s, axis=-1, keepdims=True)
    m = jnp.where(m == NEG_INF, 0.0, m)
    p = jnp.exp(s - m)
    l = jnp.sum(p, axis=-1, keepdims=True)
    return p, l


def _proj_body(sample, tm, *refs):
    if sample:
        (x_ref, nm_ref, w_ref, qn_ref, kn1_ref, kn2_ref, cw_ref, bd_ref, s0_ref, s1_ref,
         co_ref, q_ref, kc_ref, vc_ref, ks_ref, vs_ref, kw_ref, vw_ref, gt_ref, u_ref) = refs
    else:
        (x_ref, nm_ref, w_ref, qn_ref, kn1_ref, kn2_ref, cw_ref, bd_ref,
         co_ref, q_ref, kc_ref, vc_ref, ks_ref, vs_ref, kw_ref, vw_ref, gt_ref, cs_ref,
         ksb_ref, vsb_ref, kwb_ref, vwb_ref, carry_ref) = refs

    xn = _rms(x_ref[...], nm_ref[...]).astype(BF16)

    def seg(a, b):
        return _dot(xn, w_ref[:, a:b])

    c3 = 3 * CONV_DIM
    u = seg(2 * CONV_DIM, c3) * seg(0, CONV_DIM)
    bg = seg(CONV_DIM, 2 * CONV_DIM)
    row = lax.broadcasted_iota(jnp.int32, (tm, 1), 0)
    um1 = pltpu.roll(u, 1, axis=0)
    um2 = pltpu.roll(u, 2, axis=0)
    if sample:
        r = row & 3
        s0 = s0_ref[...]
        s1 = s1_ref[...]
        prev1 = jnp.where(r == 0, s1, um1)
        prev2 = jnp.where(r == 0, s0, jnp.where(r == 1, s1, um2))
        u_ref[...] = u
    else:
        @pl.when(pl.program_id(1) == 0)
        def _():
            carry_ref[...] = jnp.zeros_like(carry_ref)
        c = carry_ref[...]
        prev1 = jnp.where(row == 0, c[7:8], um1)
        prev2 = jnp.where(row == 0, c[6:7], jnp.where(row == 1, c[7:8], um2))
        carry_ref[...] = u[tm - 8:tm]
        cs_ref[0] = u[tm - 8:tm]
    cw = cw_ref[...]
    y = cw[0:1] * prev2 + cw[1:2] * prev1 + cw[2:3] * u
    co_ref[...] = (bg * y).astype(BF16)

    bd = bd_ref[...]
    q = _head_rms(seg(c3, c3 + ATTN_DIM), bd, qn_ref[...]) * (HEAD_DIM ** -0.5)
    if sample:
        q_ref[...] = q
    else:
        lane = lax.broadcasted_iota(jnp.int32, (tm, LANES), 1)
        for hd in range(N_HEADS):
            pair = q[:, (hd // 2) * LANES:(hd // 2 + 1) * LANES]
            if (hd % 2) != (hd // GROUP):
                pair = pltpu.roll(pair, HEAD_DIM, axis=1)
            keep = (lane >= HEAD_DIM) if (hd // GROUP) else (lane < HEAD_DIM)
            q_ref[0, hd] = jnp.where(keep, pair, 0.0).astype(BF16)

    k0 = c3 + ATTN_DIM
    bdk = bd[:KV_DIM, :KV_DIM]
    kc_ref[...] = seg(k0, k0 + KV_DIM)
    vc_ref[...] = seg(k0 + KV_DIM, k0 + 2 * KV_DIM)
    ks = _head_rms(seg(k0 + 2 * KV_DIM, k0 + 3 * KV_DIM), bdk, kn1_ref[...])
    vs = seg(k0 + 3 * KV_DIM, k0 + 4 * KV_DIM)
    kw = _head_rms(seg(k0 + 4 * KV_DIM, k0 + 5 * KV_DIM), bdk, kn2_ref[...])
    vw = seg(k0 + 5 * KV_DIM, k0 + 6 * KV_DIM)
    ks_ref[...] = ks
    vs_ref[...] = vs
    kw_ref[...] = kw
    vw_ref[...] = vw
    if not sample:
        ksb_ref[...] = ks.astype(BF16)
        vsb_ref[...] = vs.astype(BF16)
        kwb_ref[...] = kw.astype(BF16)
        vwb_ref[...] = vw.astype(BF16)
    gt_ref[...] = _sigmoid(seg(k0 + 6 * KV_DIM, k0 + 6 * KV_DIM + LANES))


def _project(x2d, batch, seq, tm, weights, state=None):
    n, d = x2d.shape
    sample = state is not None
    nt = seq // tm if not sample else 1
    const = lambda shape: pl.BlockSpec(shape, lambda b, t: (0,) * len(shape))
    rows = lambda w: pl.BlockSpec((tm, w), lambda b, t: (b * nt + t, 0))
    nm, w_in, qn, kn1, kn2, cw, bd = weights
    in_specs = [rows(d), const(nm.shape), const(w_in.shape), const(qn.shape), const(kn1.shape),
                const(kn2.shape), const(cw.shape), const(bd.shape)]
    args = [x2d, nm, w_in, qn, kn1, kn2, cw, bd]
    kv_f32 = [jax.ShapeDtypeStruct((n, KV_DIM), F32)] * 6
    if sample:
        in_specs += [rows(CONV_DIM), rows(CONV_DIM)]
        args += list(state)
        out_shape = ([jax.ShapeDtypeStruct((n, CONV_DIM), BF16), jax.ShapeDtypeStruct((n, ATTN_DIM), F32)]
                     + kv_f32 + [jax.ShapeDtypeStruct((n, LANES), F32), jax.ShapeDtypeStruct((n, CONV_DIM), F32)])
        out_specs = [rows(CONV_DIM), rows(ATTN_DIM)] + [rows(KV_DIM)] * 6 + [rows(LANES), rows(CONV_DIM)]
        scratch = []
        grid = (1, 1)
    else:
        out_shape = ([jax.ShapeDtypeStruct((n, CONV_DIM), BF16),
                      jax.ShapeDtypeStruct((batch, N_HEADS, seq, LANES), BF16)]
                     + kv_f32 + [jax.ShapeDtypeStruct((n, LANES), F32),
                                 jax.ShapeDtypeStruct((batch, 8, CONV_DIM), F32)]
                     + [jax.ShapeDtypeStruct((n, KV_DIM), BF16)] * 4)
        out_specs = ([rows(CONV_DIM), pl.BlockSpec((1, N_HEADS, tm, LANES), lambda b, t: (b, 0, t, 0))]
                     + [rows(KV_DIM)] * 6 + [rows(LANES), pl.BlockSpec((1, 8, CONV_DIM), lambda b, t: (b, 0, 0))]
                     + [rows(KV_DIM)] * 4)
        scratch = [pltpu.VMEM((8, CONV_DIM), F32)]
        grid = (batch, nt)
    return pl.pallas_call(
        functools.partial(_proj_body, sample, tm),
        grid=grid, in_specs=in_specs, out_specs=out_specs, out_shape=out_shape, scratch_shapes=scratch,
        compiler_params=_cparams("arbitrary", "arbitrary"),
        name="proj_sample" if sample else "proj_prompt",
    )(*args)


def _gelu_tanh(x):
    cdf = 0.5 * (1.0 + jnp.tanh(math.sqrt(2.0 / math.pi) * (x + 0.044715 * (x * x * x))))
    return x * cdf


def _compress_core(norm, nch, x, pe_ref, we_ref, w2_ref, g_ref, o_ref):
    a0 = _dot((x + pe_ref[0:1]).astype(BF16), we_ref[0])
    a1 = _dot((x + pe_ref[1:2]).astype(BF16), we_ref[1])
    hid = a0 + pltpu.roll(a1, nch - 1, axis=0)
    w2 = w2_ref[...]
    outs = []
    for h in range(N_KV_HEADS):
        act = _gelu_tanh(hid[:, h * CMP_HIDDEN:(h + 1) * CMP_HIDDEN])
        o = _dot(act.astype(BF16), w2)
        if norm:
            o = _rms(o, g_ref[...])
        outs.append(o)
    o_ref[0] = jnp.concatenate(outs, axis=-1)


def _compress_rows_body(norm, nch, x_ref, pe_ref, we_ref, w2_ref, g_ref, o_ref):
    _compress_core(norm, nch, x_ref[0], pe_ref, we_ref, w2_ref, g_ref, o_ref)


def _compress_pages_body(norm, nch, n_pages, pt_ref, *refs):
    pages = refs[:n_pages]
    pe_ref, we_ref, w2_ref, g_ref, o_ref = refs[n_pages:]
    x = jnp.concatenate([p[0] for p in pages], axis=0)
    _compress_core(norm, nch, x, pe_ref, we_ref, w2_ref, g_ref, o_ref)


def _compress_weights(pe, w1, w2):
    w1r = w1.reshape(2, CMP_STRIDE, HEAD_DIM, CMP_HIDDEN)
    eye = jnp.eye(N_KV_HEADS, dtype=w1.dtype)
    we = jnp.einsum("jrdc,hk->jrhdkc", w1r, eye).reshape(2, CMP_STRIDE * KV_DIM, N_KV_HEADS * CMP_HIDDEN)
    per = pe.reshape(2, CMP_STRIDE, 1, HEAD_DIM)
    pex = jnp.broadcast_to(per, (2, CMP_STRIDE, N_KV_HEADS, HEAD_DIM)).reshape(2, CMP_STRIDE * KV_DIM)
    return pex.astype(F32), we.astype(BF16), w2.astype(BF16)


def _compress_rows(rows3, cw, gain, norm):
    b, nch, width = rows3.shape
    pex, we, w2 = cw
    const = lambda a: pl.BlockSpec(a.shape, lambda i: (0,) * a.ndim)
    return pl.pallas_call(
        functools.partial(_compress_rows_body, norm, nch),
        grid=(b,),
        in_specs=[pl.BlockSpec((1, nch, width), lambda i: (i, 0, 0)), const(pex), const(we), const(w2), const(gain)],
        out_specs=pl.BlockSpec((1, nch, KV_DIM), lambda i: (i, 0, 0)),
        out_shape=jax.ShapeDtypeStruct((b, nch, KV_DIM), F32),
        compiler_params=_cparams("arbitrary"),
        name="compress_rows",
    )(rows3, pex, we, w2, gain)


def _compress_pages(cache3, pt_flat, n_batch, n_pages, cw, gain, norm):
    _, pch, width = cache3.shape
    nch = n_pages * pch
    pex, we, w2 = cw
    const = lambda a: pl.BlockSpec(a.shape, lambda i, pt: (0,) * a.ndim)
    page_spec = lambda j: pl.BlockSpec((1, pch, width), lambda i, pt: (pt[i * n_pages + j], 0, 0))
    grid_spec = pltpu.PrefetchScalarGridSpec(
        num_scalar_prefetch=1, grid=(n_batch,),
        in_specs=[page_spec(j) for j in range(n_pages)] + [const(pex), const(we), const(w2), const(gain)],
        out_specs=pl.BlockSpec((1, nch, KV_DIM), lambda i, pt: (i, 0, 0)))
    return pl.pallas_call(
        functools.partial(_compress_pages_body, norm, nch, n_pages),
        grid_spec=grid_spec,
        out_shape=jax.ShapeDtypeStruct((n_batch, nch, KV_DIM), F32),
        compiler_params=_cparams("arbitrary"),
        name="compress_pages",
    )(pt_flat, *([cache3] * n_pages), pex, we, w2, gain)


def _rel_bucket(dist):
    n = jnp.maximum(dist, 0)
    max_exact = NUM_BUCKETS // 2
    nf = jnp.maximum(n, 1).astype(F32)
    large = max_exact + (jnp.log(nf / max_exact) / math.log(MAX_DISTANCE / max_exact)
                         * (NUM_BUCKETS - max_exact)).astype(jnp.int32)
    large = jnp.minimum(large, NUM_BUCKETS - 1)
    return jnp.where(n < max_exact, n, large)


def _bias_by_distance(rel_bias):
    d = jnp.arange(BIAS_DMAX, dtype=jnp.int32)
    f = rel_bias.astype(F32)[_rel_bucket(d)]
    f = (f - f[BIAS_DMAX - 1:BIAS_DMAX]).T
    return jnp.concatenate([f, jnp.full((N_HEADS, 1), NEG_INF, F32)], axis=1)


def _bias_table(fext, d, valid):
    idx = np.where(valid, np.clip(d, 0, BIAS_DMAX - 1), BIAS_DMAX).astype(np.int32)
    return jnp.take(fext, jnp.asarray(idx), axis=1)


def _select_blocks(imp_t, srow, qpos, n_rank):
    qblk = qpos >> 6
    forced = (srow == 0) | (srow == qblk) | (srow == qblk - 1)
    valid = (srow << 6) <= qpos
    imp_t = jnp.where(valid, imp_t + jnp.where(forced, FORCE_SCORE, 0.0), NEG_INF)
    cnt = jnp.zeros(imp_t.shape, jnp.int32)
    for s in range(n_rank):
        r = imp_t[s:s + 1, :]
        beats = (r > imp_t) | ((r == imp_t) & (srow > s))
        cnt = cnt + jnp.where(beats, 1, 0)
    return jnp.where((cnt < N_SEL) & valid, 1.0, 0.0)


def _attn_prompt_body(kt, n_slc, q_ref, gt_ref, kcmp_ref, vcmp_ref, ks_ref, vs_ref, kw_ref, vw_ref,
                      ctab_ref, ntab_ref, wtab_ref, efar_ref, enear_ref, ovl_ref, o_ref):
    i = pl.program_id(1)
    qb = Q_BLOCK
    rows = GROUP * qb
    n_cmp_pad = kcmp_ref.shape[1]
    near_start = pl.multiple_of(jnp.maximum(i - 1, 0) * qb, qb)
    win_start = pl.multiple_of(jnp.maximum(i * qb - WINDOW, 0), qb)
    band = WINDOW + qb
    gates = gt_ref[...]
    kcmp = kcmp_ref[0].astype(BF16)
    vcmp = vcmp_ref[0].astype(BF16)
    lane = lax.broadcasted_iota(jnp.int32, (qb, LANES), 1)
    heads_out = []
    for h in range(N_KV_HEADS):
        qh = q_ref[0, h * GROUP:(h + 1) * GROUP].reshape(rows, LANES)

        s = _dot_nt(qh, kcmp) + ctab_ref[h * GROUP:(h + 1) * GROUP, 0].reshape(rows, n_cmp_pad)
        p, l = _softmax_parts(s)
        pn = p / jnp.maximum(l, 1e-30)
        o_c = _dot(pn.astype(BF16), vcmp)

        psum = pn[0:qb] + pn[qb:2 * qb] + pn[2 * qb:3 * qb] + pn[3 * qb:4 * qb]
        hi, mid, lo = _split3(psum)
        ovl = ovl_ref[...]
        imp_t = _dot_nt(ovl, hi) + _dot_nt(ovl, mid) + _dot_nt(ovl, lo)
        srow = lax.broadcasted_iota(jnp.int32, (n_slc, qb), 0)
        qpos_t = i * qb + lax.broadcasted_iota(jnp.int32, (n_slc, qb), 1)
        sel_t = _select_blocks(imp_t, srow, qpos_t, n_slc)
        if n_slc < LANES:
            sel_t = jnp.concatenate([sel_t, jnp.zeros((LANES - n_slc, qb), F32)], axis=0)
        sel = sel_t.T.astype(BF16)

        kk = ks_ref[pl.ds(near_start, 2 * qb), :]
        vv = vs_ref[pl.ds(near_start, 2 * qb), :]
        mexp = _dot(sel, enear_ref[0])
        mb = jnp.where(mexp > 0.5, ntab_ref[0, h * GROUP:(h + 1) * GROUP], NEG_INF)
        s = _dot_nt(qh, kk).reshape(GROUP, qb, 2 * qb) + mb
        s = s.reshape(rows, 2 * qb)
        m0 = jnp.max(s, axis=-1, keepdims=True)
        ms = jnp.where(m0 == NEG_INF, 0.0, m0)
        p = jnp.exp(s - ms)
        l0 = jnp.sum(p, axis=-1, keepdims=True)
        a0 = _dot(p.astype(BF16), vv)

        def far_tile(t, carry):
            m_old, l_old, acc = carry
            k0 = pl.multiple_of(t * kt, kt)
            kk = ks_ref[pl.ds(k0, kt), :]
            vv = vs_ref[pl.ds(k0, kt), :]
            mexp = _dot(sel, efar_ref[t])
            kpos = k0 + lax.broadcasted_iota(jnp.int32, (qb, kt), 1)
            mb = jnp.where((mexp > 0.5) & (kpos < near_start), 0.0, NEG_INF)
            s = (_dot_nt(qh, kk).reshape(GROUP, qb, kt) + mb[None]).reshape(rows, kt)
            m_new = jnp.maximum(m_old, jnp.max(s, axis=-1, keepdims=True))
            ms = jnp.where(m_new == NEG_INF, 0.0, m_new)
            alpha = jnp.exp(m_old - ms)
            p = jnp.exp(s - ms)
            l_new = alpha * l_old + jnp.sum(p, axis=-1, keepdims=True)
            acc = alpha * acc + _dot(p.astype(BF16), vv)
            return m_new, l_new, acc

        n_far = (near_start + kt - 1) // kt
        _, l_s, acc_s = lax.fori_loop(0, n_far, far_tile, (m0, l0, a0))
        o_s = acc_s / jnp.maximum(l_s, 1e-30)

        kk = kw_ref[pl.ds(win_start, band), :]
        vv = vw_ref[pl.ds(win_start, band), :]
        s = _dot_nt(qh, kk) + wtab_ref[0, h * GROUP:(h + 1) * GROUP].reshape(rows, band)
        p, l = _softmax_parts(s)
        o_w = _dot(p.astype(BF16), vv) / jnp.maximum(l, 1e-30)

        for g in range(GROUP):
            c = (h * GROUP + g) * N_BRANCH
            sl = slice(g * qb, (g + 1) * qb)
            heads_out.append(gates[:, c:c + 1] * o_c[sl] + gates[:, c + 1:c + 2] * o_s[sl]
                             + gates[:, c + 2:c + 3] * o_w[sl])

    tiles = []
    for j in range(N_HEADS // 2):
        a, b = heads_out[2 * j], heads_out[2 * j + 1]
        if (2 * j) // GROUP == 1:
            a = pltpu.roll(a, HEAD_DIM, axis=1)
        else:
            b = pltpu.roll(b, HEAD_DIM, axis=1)
        tiles.append(jnp.where(lane < HEAD_DIM, a, b))
    o_ref[...] = jnp.concatenate(tiles, axis=-1).astype(BF16)


def _attn_prompt(qp, gates, kcmp, vcmp, ksb, vsb, kwb, vwb, fext, batch, seq):
    qb = Q_BLOCK
    nqb = seq // qb
    n_slc = seq // SLC_BLOCK
    n_cmp_pad = kcmp.shape[1]
    n_cmp = n_cmp_pad - 1
    kt = min(512, seq)
    n_kt = seq // kt
    band = WINDOW + qb
    iq = np.arange(qb)

    d = (np.arange(nqb)[:, None, None] * qb + iq[None, :, None]
         - (np.arange(n_cmp_pad)[None, None, :] * CMP_STRIDE + CMP_BLOCK - 1))
    ctab = _bias_table(fext, d, (d >= 0) & (np.arange(n_cmp_pad) < n_cmp)[None, None, :])
    d = np.arange(2)[:, None, None] * qb + iq[None, :, None] - np.arange(2 * qb)[None, None, :]
    ntab = jnp.swapaxes(_bias_table(fext, d, d >= 0), 0, 1)
    nv = WINDOW // qb + 1
    d = np.arange(nv)[:, None, None] * qb + iq[None, :, None] - np.arange(band)[None, None, :]
    wtab = jnp.swapaxes(_bias_table(fext, d, (d >= 0) & (d < WINDOW)), 0, 1)

    srow = np.arange(LANES)[None, :, None]
    efar = ((np.arange(n_kt)[:, None, None] * kt + np.arange(kt)[None, None, :]) // SLC_BLOCK == srow)
    nstart = np.maximum(np.arange(nqb) - 1, 0) * qb
    enear = ((nstart[:, None, None] + np.arange(2 * qb)[None, None, :]) // SLC_BLOCK == srow)
    c_start = np.arange(n_cmp_pad) * CMP_STRIDE
    s_start = np.arange(n_slc) * SLC_BLOCK
    ovl = ((c_start[None, :] < s_start[:, None] + SLC_BLOCK) & (c_start[None, :] + CMP_BLOCK > s_start[:, None])
           & (np.arange(n_cmp_pad) < n_cmp)[None, :])
    efar = jnp.asarray(efar, BF16)
    enear = jnp.asarray(enear, BF16)
    ovl = jnp.asarray(ovl, BF16)

    per_batch = lambda a: pl.BlockSpec((seq, a.shape[1]), lambda b, i: (b, 0))
    const = lambda a: pl.BlockSpec(a.shape, lambda b, i: (0,) * a.ndim)
    return pl.pallas_call(
        functools.partial(_attn_prompt_body, kt, n_slc),
        grid=(batch, nqb),
        in_specs=[pl.BlockSpec((1, N_HEADS, qb, LANES), lambda b, i: (b, 0, i, 0)),
                  pl.BlockSpec((qb, LANES), lambda b, i: (b * nqb + i, 0)),
                  pl.BlockSpec((1, n_cmp_pad, KV_DIM), lambda b, i: (b, 0, 0)),
                  pl.BlockSpec((1, n_cmp_pad, KV_DIM), lambda b, i: (b, 0, 0)),
                  per_batch(ksb), per_batch(vsb), per_batch(kwb), per_batch(vwb),
                  pl.BlockSpec((N_HEADS, 1, qb, n_cmp_pad), lambda b, i: (0, i, 0, 0)),
                  pl.BlockSpec((1, N_HEADS, qb, 2 * qb), lambda b, i: (jnp.minimum(i, 1), 0, 0, 0)),
                  pl.BlockSpec((1, N_HEADS, qb, band), lambda b, i: (jnp.minimum(i, nv - 1), 0, 0, 0)),
                  const(efar),
                  pl.BlockSpec((1, LANES, 2 * qb), lambda b, i: (i, 0, 0)),
                  const(ovl)],
        out_specs=pl.BlockSpec((qb, ATTN_DIM), lambda b, i: (b * nqb + i, 0)),
        out_shape=jax.ShapeDtypeStruct((batch * seq, ATTN_DIM), BF16),
        compiler_params=_cparams("arbitrary", "arbitrary"),
        name="attn_prompt",
    )(qp, gates, kcmp, vcmp, ksb, vsb, kwb, vwb, ctab, ntab, wtab, efar, enear, ovl)


def _attn_sample_body(n_pages, ts, past_len, pt_ref, *refs):
    kpages = refs[:n_pages]
    vpages = refs[n_pages:2 * n_pages]
    (q_ref, gt_ref, kcmp_ref, vcmp_ref, ksn_ref, vsn_ref, kwc_ref, vwc_ref, kwn_ref, vwn_ref,
     ctab_ref, stab_ref, sntab_ref, wtab_ref, wntab_ref, eexp_ref, ovl_ref, o_ref) = refs[2 * n_pages:]
    rows = GROUP * N_KV_HEADS * ts
    rq = N_KV_HEADS * ts
    q = q_ref[0]
    gates = gt_ref[0]

    s = _dot_nt(q, kcmp_ref[0].astype(BF16)) + ctab_ref[...]
    p, l = _softmax_parts(s)
    pn = p / jnp.maximum(l, 1e-30)
    o_c = _dot(pn.astype(BF16), vcmp_ref[0].astype(BF16))

    psum = pn[0:rq]
    for g in range(1, GROUP):
        psum = psum + pn[g * rq:(g + 1) * rq]
    hi, mid, lo = _split3(psum)
    ovl = ovl_ref[...]
    imp = _dot(hi, ovl) + _dot(mid, ovl) + _dot(lo, ovl)
    n_slc = -(-(past_len + ts) // SLC_BLOCK)
    blk = lax.broadcasted_iota(jnp.int32, (rq, LANES), 1)
    qpos = past_len + (lax.broadcasted_iota(jnp.int32, (rq, LANES), 0) & (ts - 1))
    qblk = qpos >> 6
    forced = (blk == 0) | (blk == qblk) | (blk == qblk - 1)
    valid = ((blk << 6) <= qpos) & (blk < n_slc)
    imp = jnp.where(valid, imp + jnp.where(forced, FORCE_SCORE, 0.0), NEG_INF)
    cnt = jnp.zeros((rq, LANES), jnp.int32)
    for sidx in range(n_slc):
        r = imp[:, sidx:sidx + 1]
        beats = (r > imp) | ((r == imp) & (blk > sidx))
        cnt = cnt + jnp.where(beats, 1, 0)
    sel = jnp.where((cnt < N_SEL) & valid, 1.0, 0.0)
    sel = jnp.concatenate([sel] * GROUP, axis=0).astype(BF16)

    kc = jnp.concatenate([p_[0] for p_ in kpages], axis=0).astype(BF16)
    vc = jnp.concatenate([p_[0] for p_ in vpages], axis=0).astype(BF16)
    mexp = _dot(sel, eexp_ref[...])
    s1 = jnp.where(mexp > 0.5, _dot_nt(q, kc) + stab_ref[...], NEG_INF)
    last = sel[:, n_slc - 1:n_slc].astype(F32)
    s2 = jnp.where(last > 0.5, _dot_nt(q, ksn_ref[0].astype(BF16)) + sntab_ref[...], NEG_INF)
    m = jnp.maximum(jnp.max(s1, axis=-1, keepdims=True), jnp.max(s2, axis=-1, keepdims=True))
    m = jnp.where(m == NEG_INF, 0.0, m)
    p1 = jnp.exp(s1 - m)
    p2 = jnp.exp(s2 - m)
    l = jnp.sum(p1, axis=-1, keepdims=True) + jnp.sum(p2, axis=-1, keepdims=True)
    o_s = (_dot(p1.astype(BF16), vc) + _dot(p2.astype(BF16), vsn_ref[0].astype(BF16))) / jnp.maximum(l, 1e-30)

    s1 = _dot_nt(q, kwc_ref[0].astype(BF16)) + wtab_ref[...]
    s2 = _dot_nt(q, kwn_ref[0].astype(BF16)) + wntab_ref[...]
    m = jnp.maximum(jnp.max(s1, axis=-1, keepdims=True), jnp.max(s2, axis=-1, keepdims=True))
    m = jnp.where(m == NEG_INF, 0.0, m)
    p1 = jnp.exp(s1 - m)
    p2 = jnp.exp(s2 - m)
    l = jnp.sum(p1, axis=-1, keepdims=True) + jnp.sum(p2, axis=-1, keepdims=True)
    o_w = (_dot(p1.astype(BF16), vwc_ref[0].astype(BF16))
           + _dot(p2.astype(BF16), vwn_ref[0].astype(BF16))) / jnp.maximum(l, 1e-30)

    o_ref[0] = gates[:, 0:1] * o_c + gates[:, 1:2] * o_s + gates[:, 2:3] * o_w


def _attn_sample(q_s, gates_s, kcmp, vcmp, ks_new, vs_new, kw_new, vw_new, cache_ks, cache_vs,
                 cache_kw, cache_vw, pt_flat, fext, n_batch, ts, n_pages, page):
    past_len = n_pages * page
    w_buf = cache_kw.shape[1]
    rows = GROUP * N_KV_HEADS * ts
    n_new = 8
    n_cmp_pad = kcmp.shape[1]
    n_cmp = n_cmp_pad - 1
    n_slc = -(-(past_len + ts) // SLC_BLOCK)

    q5 = q_s.reshape(n_batch, ts, N_KV_HEADS, GROUP, HEAD_DIM).transpose(0, 3, 2, 1, 4)
    eye = jnp.eye(N_KV_HEADS, dtype=q_s.dtype)
    qr = jnp.einsum("bghtd,hk->bghtkd", q5, eye).reshape(n_batch, rows, LANES).astype(BF16)
    g5 = gates_s[:, :N_HEADS * N_BRANCH].reshape(n_batch, ts, N_KV_HEADS, GROUP, N_BRANCH).transpose(0, 3, 2, 1, 4)
    gr = jnp.pad(g5.reshape(n_batch, rows, N_BRANCH), ((0, 0), (0, 0), (0, LANES - N_BRANCH)))
    pad_new = lambda a: jnp.pad(a.reshape(n_batch, ts, KV_DIM), ((0, 0), (0, n_new - ts), (0, 0)))
    ks_new, vs_new, kw_new, vw_new = map(pad_new, (ks_new, vs_new, kw_new, vw_new))

    g_i, h_i, t_i = np.meshgrid(np.arange(GROUP), np.arange(N_KV_HEADS), np.arange(ts), indexing="ij")
    head = (h_i * GROUP + g_i).reshape(rows)
    tq = t_i.reshape(rows)
    pos_q = past_len + tq

    def table(d, valid):
        full = _bias_table(fext, d, valid)
        return full[jnp.asarray(head), jnp.arange(rows)]

    nn = np.arange(n_cmp_pad)
    d = pos_q[:, None] - (nn[None, :] * CMP_STRIDE + CMP_BLOCK - 1)
    ctab = table(d, (d >= 0) & (nn < n_cmp)[None, :])
    d = pos_q[:, None] - np.arange(past_len)[None, :]
    stab = table(d, d >= 0)
    jn = np.arange(n_new)
    d = tq[:, None] - jn[None, :]
    sntab = table(d, (d >= 0) & (jn < ts)[None, :])
    pos_w = past_len - w_buf + np.arange(w_buf)
    d = pos_q[:, None] - pos_w[None, :]
    wtab = table(d, (d >= 0) & (d < WINDOW) & (pos_w >= 0)[None, :])
    wntab = table(tq[:, None] - jn[None, :], (tq[:, None] >= jn[None, :]) & (jn < ts)[None, :])

    eexp = jnp.asarray(np.arange(past_len)[None, :] // SLC_BLOCK == np.arange(LANES)[:, None], BF16)
    c_start = nn * CMP_STRIDE
    s_start = np.arange(LANES) * SLC_BLOCK
    ovl = jnp.asarray((c_start[:, None] < s_start[None, :] + SLC_BLOCK) & (c_start[:, None] + CMP_BLOCK > s_start[None, :])
                      & (nn < n_cmp)[:, None] & (np.arange(LANES) < n_slc)[None, :], BF16)

    const = lambda a: pl.BlockSpec(a.shape, lambda b, pt: (0,) * a.ndim)
    per_b = lambda a: pl.BlockSpec((1,) + a.shape[1:], lambda b, pt: (b,) + (0,) * (a.ndim - 1))
    page_spec = lambda j: pl.BlockSpec((1, page, KV_DIM), lambda b, pt: (pt[b * n_pages + j], 0, 0))
    small = [qr, gr, kcmp, vcmp, ks_new, vs_new, cache_kw, cache_vw, kw_new, vw_new]
    consts = [ctab, stab, sntab, wtab, wntab, eexp, ovl]
    grid_spec = pltpu.PrefetchScalarGridSpec(
        num_scalar_prefetch=1, grid=(n_batch,),
        in_specs=[page_spec(j) for j in range(n_pages)] * 2 + [per_b(a) for a in small] + [const(a) for a in consts],
        out_specs=pl.BlockSpec((1, rows, LANES), lambda b, pt: (b, 0, 0)))
    o = pl.pallas_call(
        functools.partial(_attn_sample_body, n_pages, ts, past_len),
        grid_spec=grid_spec,
        out_shape=jax.ShapeDtypeStruct((n_batch, rows, LANES), F32),
        compiler_params=_cparams("arbitrary"),
        name="attn_sample",
    )(pt_flat, *([cache_ks] * n_pages), *([cache_vs] * n_pages), *small, *consts)
    o6 = o.reshape(n_batch, GROUP, N_KV_HEADS, ts, N_KV_HEADS, HEAD_DIM)
    o5 = jnp.stack([o6[:, :, h, :, h] for h in range(N_KV_HEADS)], axis=2)
    return o5.transpose(0, 3, 2, 1, 4).reshape(n_batch * ts, ATTN_DIM).astype(BF16)


def _post1_body(tm, x_ref, co_ref, at_ref, wo_ref, nf_ref, wr_ref, br_ref, tri_ref, run0_ref,
                h_ref, hn_ref, rt_ref, cnt_ref, run_ref):
    @pl.when(pl.program_id(0) == 0)
    def _():
        run_ref[...] = run0_ref[...]

    h = x_ref[...] + _dot(co_ref[...], wo_ref[0:CONV_DIM]) + _dot(at_ref[...], wo_ref[CONV_DIM:CONV_DIM + ATTN_DIM])
    hn = _rms(h, nf_ref[...])
    h_ref[...] = h
    hn_ref[...] = hn

    hi = hn.astype(BF16)
    lo = (hn - hi.astype(F32)).astype(BF16)
    wr = wr_ref[...]
    whi = wr.astype(BF16)
    wlo = (wr - whi.astype(F32)).astype(BF16)
    logits = _dot(hi, whi) + _dot(lo, whi) + _dot(hi, wlo) + br_ref[...]

    lane_i = lax.broadcasted_iota(jnp.int32, (tm, LANES), 1)
    lane = lane_i.astype(F32)
    big = float(LANES)
    gmask = (lane_i >= ROUTER_GROUP_LANE) & (lane_i < ROUTER_GROUP_LANE + N_GROUPS)
    lg = jnp.where(gmask, logits, NEG_INF)
    eg = jnp.exp(lg - jnp.max(lg, axis=-1, keepdims=True))
    pg = eg / jnp.sum(eg, axis=-1, keepdims=True)
    gw = jnp.max(pg, axis=-1, keepdims=True)
    grp = jnp.min(jnp.where(gmask & (pg == gw), lane, big), axis=-1, keepdims=True) - ROUTER_GROUP_LANE

    group_of_lane = (lane_i >> 3).astype(F32)
    emask = (lane_i < N_EXPERTS) & (group_of_lane == grp)
    le = jnp.where(emask, logits, NEG_INF)
    ee = jnp.exp(le - jnp.max(le, axis=-1, keepdims=True))
    pe = jnp.where(emask, ee / jnp.sum(ee, axis=-1, keepdims=True), -1.0)
    v1 = jnp.max(pe, axis=-1, keepdims=True)
    i1 = jnp.min(jnp.where(pe == v1, lane, big), axis=-1, keepdims=True)
    pe2 = jnp.where(lane == i1, -1.0, pe)
    v2 = jnp.max(pe2, axis=-1, keepdims=True)
    i2 = jnp.min(jnp.where(pe2 == v2, lane, big), axis=-1, keepdims=True)
    tot = v1 + v2
    w1 = v1 / tot * gw
    w2 = v2 / tot * gw

    oh1 = jnp.where(lane == i1, 1.0, 0.0)
    oh2 = jnp.where(lane == i2, 1.0, 0.0)
    both = oh1 + oh2
    before = _dot(tri_ref[...], both.astype(BF16)) + run_ref[0:1]
    r1 = jnp.sum(oh1 * before, axis=-1, keepdims=True)
    r2 = jnp.sum(oh2 * before, axis=-1, keepdims=True)
    run = run_ref[0:1] + jnp.sum(both, axis=0, keepdims=True)
    run_ref[...] = jnp.broadcast_to(run, run_ref.shape)
    cnt_ref[...] = jnp.broadcast_to(run, cnt_ref.shape)

    rt = jnp.where(lane_i == 0, i1, 0.0)
    rt = jnp.where(lane_i == 1, i2, rt)
    rt = jnp.where(lane_i == 2, r1, rt)
    rt = jnp.where(lane_i == 3, r2, rt)
    rt = jnp.where(lane_i == 4, w1, rt)
    rt = jnp.where(lane_i == 5, w2, rt)
    rt_ref[...] = rt


def _post1(x2d, co, at, wo, nf, wr, br, run0, tm):
    n, d = x2d.shape
    tri = jnp.asarray(np.tril(np.ones((tm, tm), np.float32), -1), BF16)
    rows = lambda w: pl.BlockSpec((tm, w), lambda i: (i, 0))
    const = lambda a: pl.BlockSpec(a.shape, lambda i: (0,) * a.ndim)
    return pl.pallas_call(
        functools.partial(_post1_body, tm),
        grid=(n // tm,),
        in_specs=[rows(d), rows(CONV_DIM), rows(ATTN_DIM), const(wo), const(nf), const(wr), const(br),
                  const(tri), const(run0)],
        out_specs=[rows(d), rows(d), rows(LANES), pl.BlockSpec((8, LANES), lambda i: (0, 0))],
        out_shape=[jax.ShapeDtypeStruct((n, d), F32), jax.ShapeDtypeStruct((n, d), F32),
                   jax.ShapeDtypeStruct((n, LANES), F32), jax.ShapeDtypeStruct((8, LANES), F32)],
        scratch_shapes=[pltpu.VMEM((8, LANES), F32)],
        compiler_params=_cparams("arbitrary"),
        name="post1",
    )(x2d, co, at, wo, nf, wr, br, tri, run0)


def _row_copy(src_ref, dst_ref, s, d, sem):
    return pltpu.make_async_copy(src_ref.at[pl.ds(s, 1)], dst_ref.at[pl.ds(d, 1)], sem)


def _permute_rows(n_assign, src_ref, out_ref, sem, src_dst_of):
    window = min(DMA_WINDOW, n_assign)
    assert n_assign % window == 0

    def chunk(c, _):
        base = c * window

        def issue(j, _):
            s, d = src_dst_of(base + j)
            _row_copy(src_ref, out_ref, s, d, sem).start()
            return 0

        def drain(j, _):
            _row_copy(src_ref, out_ref, 0, 0, sem).wait()
            return 0

        lax.fori_loop(0, window, issue, 0)
        lax.fori_loop(0, window, drain, 0)
        return 0

    lax.fori_loop(0, n_assign // window, chunk, 0)


def _scatter_rows_body(n_assign, dest_ref, src_ref, init_ref, out_ref, sem):
    del init_ref
    _permute_rows(n_assign, src_ref, out_ref, sem, lambda a: (a >> 1, dest_ref[a]))


def _gather_rows_body(n_assign, dest_ref, src_ref, out_ref, sem):
    _permute_rows(n_assign, src_ref, out_ref, sem, lambda a: (dest_ref[a], a))


def _scatter_rows(dest, src, slots):
    n_assign = dest.shape[0]
    any_spec = pl.BlockSpec(memory_space=pl.ANY)
    return pl.pallas_call(
        functools.partial(_scatter_rows_body, n_assign),
        grid_spec=pltpu.PrefetchScalarGridSpec(num_scalar_prefetch=1, grid=(1,), in_specs=[any_spec, any_spec],
                                               out_specs=any_spec, scratch_shapes=[pltpu.SemaphoreType.DMA(())]),
        out_shape=jax.ShapeDtypeStruct(slots.shape, slots.dtype),
        input_output_aliases={2: 0},
        compiler_params=pltpu.CompilerParams(dimension_semantics=("arbitrary",)),
        name="scatter_rows",
    )(dest, src, slots)


def _gather_rows(dest, src):
    n_assign = dest.shape[0]
    any_spec = pl.BlockSpec(memory_space=pl.ANY)
    return pl.pallas_call(
        functools.partial(_gather_rows_body, n_assign),
        grid_spec=pltpu.PrefetchScalarGridSpec(num_scalar_prefetch=1, grid=(1,), in_specs=[any_spec],
                                               out_specs=any_spec, scratch_shapes=[pltpu.SemaphoreType.DMA(())]),
        out_shape=jax.ShapeDtypeStruct((n_assign, src.shape[1]), src.dtype),
        compiler_params=pltpu.CompilerParams(dimension_semantics=("arbitrary",)),
        name="gather_rows",
    )(dest, src)


def _experts_body(be_ref, nu_ref, x_ref, wg_ref, wu_ref, wd_ref, o_ref, wg_s, wu_s, wd_s):
    i = pl.program_id(0)

    @pl.when(i < nu_ref[0])
    def _():
        prev = be_ref[jnp.maximum(i - 1, 0)]

        @pl.when((i == 0) | (be_ref[i] != prev))
        def _():
            wg_s[...] = wg_ref[0].astype(BF16)
            wu_s[...] = wu_ref[0].astype(BF16)
            wd_s[...] = wd_ref[0].astype(BF16)

        x = x_ref[...].astype(BF16)
        g = _dot(x, wg_s[...])
        u = _dot(x, wu_s[...])
        a = g * _sigmoid(g) * u
        o_ref[...] = _dot(a.astype(BF16), wd_s[...])

    @pl.when(i >= nu_ref[0])
    def _():
        o_ref[...] = jnp.zeros_like(o_ref)


def _experts(blk_expert, n_used, xs, wg, wu, wd):
    p, d = xs.shape
    n_blk = p // EXPERT_ROWS
    de = wg.shape[2]
    xmap = lambda i, be, nu: (jnp.minimum(i, jnp.maximum(nu[0] - 1, 0)), 0)
    wmap = lambda i, be, nu: (be[jnp.minimum(i, jnp.maximum(nu[0] - 1, 0))], 0, 0)
    grid_spec = pltpu.PrefetchScalarGridSpec(
        num_scalar_prefetch=2, grid=(n_blk,),
        in_specs=[pl.BlockSpec((EXPERT_ROWS, d), xmap), pl.BlockSpec((1, d, de), wmap),
                  pl.BlockSpec((1, d, de), wmap), pl.BlockSpec((1, de, d), wmap)],
        out_specs=pl.BlockSpec((EXPERT_ROWS, d), lambda i, be, nu: (i, 0)),
        scratch_shapes=[pltpu.VMEM((d, de), BF16), pltpu.VMEM((d, de), BF16), pltpu.VMEM((de, d), BF16)])
    return pl.pallas_call(
        _experts_body, grid_spec=grid_spec,
        out_shape=jax.ShapeDtypeStruct((p, d), F32),
        compiler_params=_cparams("arbitrary"),
        name="experts",
    )(blk_expert, n_used, xs, wg, wu, wd)


def _post2_body(d, h_ref, ya_ref, rt_ref, p_ref, wple_ref, wpg_ref, bpg_ref, np_ref, o_ref):
    rt = rt_ref[...]
    moe = ya_ref[:, 0:d] * rt[:, 4:5] + ya_ref[:, d:2 * d] * rt[:, 5:6]
    h = h_ref[...] + moe
    gate = _sigmoid(_dot(_rms(h, np_ref[...]).astype(BF16), wpg_ref[...]) + bpg_ref[...])
    o_ref[...] = h + gate * _dot(p_ref[...].astype(BF16), wple_ref[...])


def _post2(h, ya2, rt, p2d, wple, wpg, bpg, npl, tm):
    n, d = h.shape
    rows = lambda w: pl.BlockSpec((tm, w), lambda i: (i, 0))
    const = lambda a: pl.BlockSpec(a.shape, lambda i: (0,) * a.ndim)
    return pl.pallas_call(
        functools.partial(_post2_body, d),
        grid=(n // tm,),
        in_specs=[rows(d), rows(2 * d), rows(LANES), rows(p2d.shape[1]), const(wple), const(wpg), const(bpg),
                  const(npl)],
        out_specs=rows(d),
        out_shape=jax.ShapeDtypeStruct((n, d), F32),
        compiler_params=_cparams("arbitrary"),
        name="post2",
    )(h, ya2, rt, p2d, wple, wpg, bpg, npl)


def _row_tile(n, cap=512):
    t = min(cap, n)
    assert n % t == 0 and t % 8 == 0
    return t


def kernel(x_prompt, x_sample, p_prompt, p_sample, cache_k_cmp, cache_v_cmp, cache_k_slc, cache_v_slc, cache_k_win, cache_v_win, state_conv, page_table, w_in, w_out, conv_w, norm_mix, norm_ffn, norm_ple, q_norm, k_norm, cmp_pe_k, cmp_w1_k, cmp_w2_k, cmp_pe_v, cmp_w1_v, cmp_w2_v, rel_bias, w_router_group, b_router_group, w_router_expert, b_router_expert, w_exp_gate, w_exp_up, w_exp_down, w_ple, w_ple_gate, b_ple_gate):
    assert w_in.shape[0] == 1, "single-layer step"
    bp, t, d = x_prompt.shape
    bs, ts, _ = x_sample.shape
    n_pages = page_table.shape[1]
    page = cache_k_cmp.shape[2]
    past_len = n_pages * page
    w_buf = cache_k_win.shape[2]
    n_phys = cache_k_cmp.shape[1]
    assert t % Q_BLOCK == 0 and t >= WINDOW + Q_BLOCK and page % CMP_STRIDE == 0 and ts == 4
    np_rows, ns_rows = bp * t, bs * ts

    row = lambda v: v.reshape(1, -1).astype(F32)
    w_in_b = jnp.pad(w_in[0], ((0, 0), (0, Z_COLS - w_in.shape[2]))).astype(BF16)
    qn = row(jnp.tile(q_norm[0], N_HEADS))
    kn1 = row(jnp.tile(k_norm[0, 1], N_KV_HEADS))
    kn2 = row(jnp.tile(k_norm[0, 2], N_KV_HEADS))
    bd = jnp.asarray(np.kron(np.eye(N_HEADS), np.ones((HEAD_DIM, HEAD_DIM))), BF16)
    pw = (row(norm_mix[0]), w_in_b, qn, kn1, kn2, conv_w[0].astype(F32), bd)
    cw_k = _compress_weights(cmp_pe_k[0], cmp_w1_k[0], cmp_w2_k[0])
    cw_v = _compress_weights(cmp_pe_v[0], cmp_w1_v[0], cmp_w2_v[0])
    kn0 = row(k_norm[0, 0])
    fext = _bias_by_distance(rel_bias)
    pt_flat = page_table.reshape(-1).astype(jnp.int32)
    wr = jnp.zeros((d, LANES), F32).at[:, :N_EXPERTS].set(w_router_expert[0])
    wr = wr.at[:, ROUTER_GROUP_LANE:ROUTER_GROUP_LANE + N_GROUPS].set(w_router_group[0])
    br = jnp.zeros((1, LANES), F32).at[0, :N_EXPERTS].set(b_router_expert[0])
    br = br.at[0, ROUTER_GROUP_LANE:ROUTER_GROUP_LANE + N_GROUPS].set(b_router_group[0])
    wo_b = w_out[0].astype(BF16)
    wple_b = w_ple[0].astype(BF16)
    wpg_b = w_ple_gate[0].astype(BF16)

    tm_p = _row_tile(t)
    (co_p, q_p, kc_p, vc_p, ks_p, vs_p, kw_p, vw_p, gt_p, cs_p, ksb, vsb, kwb, vwb) = _project(
        x_prompt.reshape(np_rows, d), bp, t, tm_p, pw)
    chunk_w = CMP_STRIDE * KV_DIM
    kcmp_p = _compress_rows(kc_p.reshape(bp, t // CMP_STRIDE, chunk_w), cw_k, kn0, True)
    vcmp_p = _compress_rows(vc_p.reshape(bp, t // CMP_STRIDE, chunk_w), cw_v, kn0, False)
    at_p = _attn_prompt(q_p, gt_p, kcmp_p, vcmp_p, ksb, vsb, kwb, vwb, fext, bp, t)

    st = state_conv[0].astype(F32)
    s0 = jnp.repeat(st[:, 0], ts, axis=0)
    s1 = jnp.repeat(st[:, 1], ts, axis=0)
    (co_s, q_s, kc_s, vc_s, ks_s, vs_s, kw_s, vw_s, gt_s, u_s) = _project(
        x_sample.reshape(ns_rows, d), bs, ts, ns_rows, pw, state=(s0, s1))
    pch = page // CMP_STRIDE
    kcmp_s = _compress_pages(cache_k_cmp[0].reshape(n_phys, pch, chunk_w), pt_flat, bs, n_pages, cw_k, kn0, True)
    vcmp_s = _compress_pages(cache_v_cmp[0].reshape(n_phys, pch, chunk_w), pt_flat, bs, n_pages, cw_v, kn0, False)
    kwc = cache_k_win[0].reshape(bs, w_buf, KV_DIM)
    vwc = cache_v_win[0].reshape(bs, w_buf, KV_DIM)
    at_s = _attn_sample(q_s, gt_s, kcmp_s, vcmp_s, ks_s, vs_s, kw_s, vw_s,
                        cache_k_slc[0].reshape(n_phys, page, KV_DIM), cache_v_slc[0].reshape(n_phys, page, KV_DIM),
                        kwc, vwc, pt_flat, fext, bs, ts, n_pages, page)

    tp1 = _row_tile(np_rows)
    ts1 = _row_tile(ns_rows)
    nf = row(norm_ffn[0])
    h_p, hn_p, rt_p, cnt_p = _post1(x_prompt.reshape(np_rows, d), co_p, at_p, wo_b, nf, wr, br,
                                    jnp.zeros((8, LANES), F32), tp1)
    h_s, hn_s, rt_s, cnt_s = _post1(x_sample.reshape(ns_rows, d), co_s, at_s, wo_b, nf, wr, br, cnt_p, ts1)

    counts = cnt_s[0, :N_EXPERTS].astype(jnp.int32)
    padded = (counts + EXPERT_ROWS - 1) // EXPERT_ROWS * EXPERT_ROWS
    pad_end = jnp.cumsum(padded)
    pad_start = pad_end - padded
    n_assign = 2 * (np_rows + ns_rows)
    n_blk = (n_assign + N_EXPERTS * (EXPERT_ROWS - 1) + EXPERT_ROWS - 1) // EXPERT_ROWS
    blk_expert = jnp.clip(jnp.searchsorted(pad_end, jnp.arange(n_blk, dtype=jnp.int32) * EXPERT_ROWS, side="right"),
                          0, N_EXPERTS - 1).astype(jnp.int32)
    n_used = (pad_end[-1:] // EXPERT_ROWS).astype(jnp.int32)

    def dest_of(rt):
        e = rt[:, 0:2].astype(jnp.int32).reshape(-1)
        return pad_start[e] + rt[:, 2:4].astype(jnp.int32).reshape(-1)

    dest_p = dest_of(rt_p)
    dest_s = dest_of(rt_s)

    xs = jnp.zeros((n_blk * EXPERT_ROWS, d), F32)
    xs = _scatter_rows(dest_p, hn_p, xs)
    xs = _scatter_rows(dest_s, hn_s, xs)
    yb = _experts(blk_expert, n_used, xs, w_exp_gate[0], w_exp_up[0], w_exp_down[0])
    ya_p = _gather_rows(dest_p, yb).reshape(np_rows, 2 * d)
    ya_s = _gather_rows(dest_s, yb).reshape(ns_rows, 2 * d)

    bpg = row(b_ple_gate[0])
    npl = row(norm_ple[0])
    y_p = _post2(h_p, ya_p, rt_p, p_prompt[0].reshape(np_rows, -1), wple_b, wpg_b, bpg, npl, tp1)
    y_s = _post2(h_s, ya_s, rt_s, p_sample[0].reshape(ns_rows, -1), wple_b, wpg_b, bpg, npl, ts1)

    kv5 = lambda a, b, s: a.reshape(1, b, s, N_KV_HEADS, HEAD_DIM)
    wp = min(WINDOW, t)
    win_p = lambda a: kv5(a, bp, t)[:, :, t - wp:]
    win_s = lambda c, new: jnp.concatenate([c[0], new.reshape(bs, ts, N_KV_HEADS, HEAD_DIM)], axis=1)[None, :, ts:]
    conv_p = cs_p[:, 8 - (CONV_K - 1):][None]
    conv_s = u_s.reshape(bs, ts, CONV_DIM)[:, ts - (CONV_K - 1):][None]
    return (y_p.reshape(bp, t, d), y_s.reshape(bs, ts, d),
            kv5(kc_p, bp, t), kv5(vc_p, bp, t), kv5(ks_p, bp, t), kv5(vs_p, bp, t), win_p(kw_p), win_p(vw_p), conv_p,
            kv5(kc_s, bs, ts), kv5(vc_s, bs, ts), kv5(ks_s, bs, ts), kv5(vs_s, bs, ts),
            win_s(cache_k_win, kw_s), win_s(cache_v_win, vw_s), conv_s)
```

```python
import functools
import math

import numpy as np
import jax
import jax.numpy as jnp
from jax import lax
from jax.experimental import pallas as pl
from jax.experimental.pallas import tpu as pltpu

F32 = jnp.float32
BF16 = jnp.bfloat16
NEG_INF = float("-inf")

HEAD_DIM = 64
N_HEADS = 8
N_KV_HEADS = 2
GROUP = N_HEADS // N_KV_HEADS
CONV_DIM = 512
ATTN_DIM = 512
KV_DIM = N_KV_HEADS * HEAD_DIM
N_BRANCH = 3
CONV_K = 3
CMP_BLOCK = 32
CMP_STRIDE = 16
CMP_HIDDEN = 256
SLC_BLOCK = 64
N_SEL = 16
WINDOW = 512
Q_BLOCK = 128
FORCE_SCORE = 1e4
NUM_BUCKETS = 32
MAX_DISTANCE = 128
N_GROUPS = 4
EXPERTS_PER_GROUP = 8
N_EXPERTS = N_GROUPS * EXPERTS_PER_GROUP
D_EXPERT = 512
EPS = 1e-6

D_MODEL = 1024
LANES = 128
TOKEN_TILE_ROWS = D_MODEL // LANES
Z_COLS = 3 * CONV_DIM + ATTN_DIM + 6 * KV_DIM + LANES
BIAS_DMAX = 768
EXPERT_ROWS = 256
ROUTER_GROUP_LANE = 32
DMA_WINDOW = 256
PERMUTE_STEP = 2048
VMEM_LIMIT = 56 * 1024 * 1024


def _cparams(*sem):
    return pltpu.CompilerParams(dimension_semantics=sem, vmem_limit_bytes=VMEM_LIMIT)


def _dot(a, b):
    return jnp.dot(a, b, preferred_element_type=F32)


def _dot_nt(a, b):
    return lax.dot_general(a, b, (((1,), (1,)), ((), ())), preferred_element_type=F32)


def _split3(x):
    hi = x.astype(BF16)
    r = x - hi.astype(F32)
    mid = r.astype(BF16)
    lo = (r - mid.astype(F32)).astype(BF16)
    return hi, mid, lo


def _rms(x, g):
    return x * lax.rsqrt(jnp.mean(x * x, axis=-1, keepdims=True) + EPS) * g


def _head_rms(x, bd, g):
    hi, mid, lo = _split3(x * x)
    ss = _dot(hi, bd) + _dot(mid, bd) + _dot(lo, bd)
    return x * lax.rsqrt(ss * (1.0 / HEAD_DIM) + EPS) * g


def _sigmoid(x):
    return 1.0 / (1.0 + jnp.exp(-x))


def _store_token_tiles(ref, x, n):
    r = x.shape[1] // LANES
    for j in range(r):
        ref[pl.ds(j, n, stride=r), :] = x[:, j * LANES:(j + 1) * LANES]


def _load_token_tiles(ref, n, r):
    return jnp.concatenate([ref[pl.ds(j, n, stride=r), :] for j in range(r)], axis=-1)


def _softmax_parts(s):
    m = jnp.max(s, axis=-1, keepdims=True)
    m = jnp.where(m == NEG_INF, 0.0, m)
    p = jnp.exp(s - m)
    l = jnp.sum(p, axis=-1, keepdims=True)
    return p, l


def _proj_body(sample, tm, *refs):
    if sample:
        (x_ref, nm_ref, w_ref, qn_ref, kn1_ref, kn2_ref, cw_ref, bd_ref, s0_ref, s1_ref,
         co_ref, q_ref, kc_ref, vc_ref, ks_ref, vs_ref, kw_ref, vw_ref, gt_ref, u_ref) = refs
    else:
        (x_ref, nm_ref, w_ref, qn_ref, kn1_ref, kn2_ref, cw_ref, bd_ref,
         co_ref, q_ref, kc_ref, vc_ref, ks_ref, vs_ref, kw_ref, vw_ref, gt_ref, cs_ref,
         ksb_ref, vsb_ref, kwb_ref, vwb_ref, carry_ref) = refs

    xn = _rms(x_ref[...], nm_ref[...]).astype(BF16)

    def seg(a, b):
        return _dot(xn, w_ref[:, a:b])

    c3 = 3 * CONV_DIM
    u = seg(2 * CONV_DIM, c3) * seg(0, CONV_DIM)
    bg = seg(CONV_DIM, 2 * CONV_DIM)
    row = lax.broadcasted_iota(jnp.int32, (tm, 1), 0)
    um1 = pltpu.roll(u, 1, axis=0)
    um2 = pltpu.roll(u, 2, axis=0)
    if sample:
        r = row & 3
        s0 = s0_ref[...]
        s1 = s1_ref[...]
        prev1 = jnp.where(r == 0, s1, um1)
        prev2 = jnp.where(r == 0, s0, jnp.where(r == 1, s1, um2))
        u_ref[...] = u
    else:
        @pl.when(pl.program_id(1) == 0)
        def _():
            carry_ref[...] = jnp.zeros_like(carry_ref)
        c = carry_ref[...]
        prev1 = jnp.where(row == 0, c[7:8], um1)
        prev2 = jnp.where(row == 0, c[6:7], jnp.where(row == 1, c[7:8], um2))
        carry_ref[...] = u[tm - 8:tm]
        cs_ref[0] = u[tm - 8:tm]
    cw = cw_ref[...]
    y = cw[0:1] * prev2 + cw[1:2] * prev1 + cw[2:3] * u
    co_ref[...] = (bg * y).astype(BF16)

    bd = bd_ref[...]
    q = _head_rms(seg(c3, c3 + ATTN_DIM), bd, qn_ref[...]) * (HEAD_DIM ** -0.5)
    if sample:
        q_ref[...] = q
    else:
        lane = lax.broadcasted_iota(jnp.int32, (tm, LANES), 1)
        for hd in range(N_HEADS):
            pair = q[:, (hd // 2) * LANES:(hd // 2 + 1) * LANES]
            if (hd % 2) != (hd // GROUP):
                pair = pltpu.roll(pair, HEAD_DIM, axis=1)
            keep = (lane >= HEAD_DIM) if (hd // GROUP) else (lane < HEAD_DIM)
            q_ref[0, hd] = jnp.where(keep, pair, 0.0).astype(BF16)

    k0 = c3 + ATTN_DIM
    bdk = bd[:KV_DIM, :KV_DIM]
    kc_ref[...] = seg(k0, k0 + KV_DIM)
    vc_ref[...] = seg(k0 + KV_DIM, k0 + 2 * KV_DIM)
    ks = _head_rms(seg(k0 + 2 * KV_DIM, k0 + 3 * KV_DIM), bdk, kn1_ref[...])
    vs = seg(k0 + 3 * KV_DIM, k0 + 4 * KV_DIM)
    kw = _head_rms(seg(k0 + 4 * KV_DIM, k0 + 5 * KV_DIM), bdk, kn2_ref[...])
    vw = seg(k0 + 5 * KV_DIM, k0 + 6 * KV_DIM)
    ks_ref[...] = ks
    vs_ref[...] = vs
    kw_ref[...] = kw
    vw_ref[...] = vw
    if not sample:
        ksb_ref[...] = ks.astype(BF16)
        vsb_ref[...] = vs.astype(BF16)
        kwb_ref[...] = kw.astype(BF16)
        vwb_ref[...] = vw.astype(BF16)
    gt_ref[...] = _sigmoid(seg(k0 + 6 * KV_DIM, k0 + 6 * KV_DIM + LANES))


def _project(x2d, batch, seq, tm, weights, state=None):
    n, d = x2d.shape
    sample = state is not None
    nt = seq // tm if not sample else 1
    const = lambda shape: pl.BlockSpec(shape, lambda b, t: (0,) * len(shape))
    rows = lambda w: pl.BlockSpec((tm, w), lambda b, t: (b * nt + t, 0))
    nm, w_in, qn, kn1, kn2, cw, bd = weights
    in_specs = [rows(d), const(nm.shape), const(w_in.shape), const(qn.shape), const(kn1.shape),
                const(kn2.shape), const(cw.shape), const(bd.shape)]
    args = [x2d, nm, w_in, qn, kn1, kn2, cw, bd]
    kv_f32 = [jax.ShapeDtypeStruct((n, KV_DIM), F32)] * 6
    if sample:
        in_specs += [rows(CONV_DIM), rows(CONV_DIM)]
        args += list(state)
        out_shape = ([jax.ShapeDtypeStruct((n, CONV_DIM), BF16), jax.ShapeDtypeStruct((n, ATTN_DIM), F32)]
                     + kv_f32 + [jax.ShapeDtypeStruct((n, LANES), F32), jax.ShapeDtypeStruct((n, CONV_DIM), F32)])
        out_specs = [rows(CONV_DIM), rows(ATTN_DIM)] + [rows(KV_DIM)] * 6 + [rows(LANES), rows(CONV_DIM)]
        scratch = []
        grid = (1, 1)
    else:
        out_shape = ([jax.ShapeDtypeStruct((n, CONV_DIM), BF16),
                      jax.ShapeDtypeStruct((batch, N_HEADS, seq, LANES), BF16)]
                     + kv_f32 + [jax.ShapeDtypeStruct((n, LANES), F32),
                                 jax.ShapeDtypeStruct((batch, 8, CONV_DIM), F32)]
                     + [jax.ShapeDtypeStruct((n, KV_DIM), BF16)] * 4)
        out_specs = ([rows(CONV_DIM), pl.BlockSpec((1, N_HEADS, tm, LANES), lambda b, t: (b, 0, t, 0))]
                     + [rows(KV_DIM)] * 6 + [rows(LANES), pl.BlockSpec((1, 8, CONV_DIM), lambda b, t: (b, 0, 0))]
                     + [rows(KV_DIM)] * 4)
        scratch = [pltpu.VMEM((8, CONV_DIM), F32)]
        grid = (batch, nt)
    return pl.pallas_call(
        functools.partial(_proj_body, sample, tm),
        grid=grid, in_specs=in_specs, out_specs=out_specs, out_shape=out_shape, scratch_shapes=scratch,
        compiler_params=_cparams("arbitrary", "arbitrary"),
        name="proj_sample" if sample else "proj_prompt",
    )(*args)


def _gelu_tanh(x):
    cdf = 0.5 * (1.0 + jnp.tanh(math.sqrt(2.0 / math.pi) * (x + 0.044715 * (x * x * x))))
    return x * cdf


def _compress_core(norm, nch, x, pe_ref, we_ref, w2_ref, g_ref, o_ref):
    a0 = _dot((x + pe_ref[0:1]).astype(BF16), we_ref[0])
    a1 = _dot((x + pe_ref[1:2]).astype(BF16), we_ref[1])
    hid = a0 + pltpu.roll(a1, nch - 1, axis=0)
    w2 = w2_ref[...]
    outs = []
    for h in range(N_KV_HEADS):
        act = _gelu_tanh(hid[:, h * CMP_HIDDEN:(h + 1) * CMP_HIDDEN])
        o = _dot(act.astype(BF16), w2)
        if norm:
            o = _rms(o, g_ref[...])
        outs.append(o)
    o_ref[0] = jnp.concatenate(outs, axis=-1)


def _compress_rows_body(norm, nch, x_ref, pe_ref, we_ref, w2_ref, g_ref, o_ref):
    _compress_core(norm, nch, x_ref[0], pe_ref, we_ref, w2_ref, g_ref, o_ref)


def _compress_pages_body(norm, nch, n_pages, pt_ref, *refs):
    pages = refs[:n_pages]
    pe_ref, we_ref, w2_ref, g_ref, o_ref = refs[n_pages:]
    x = jnp.concatenate([p[0] for p in pages], axis=0)
    _compress_core(norm, nch, x, pe_ref, we_ref, w2_ref, g_ref, o_ref)


def _compress_weights(pe, w1, w2):
    w1r = w1.reshape(2, CMP_STRIDE, HEAD_DIM, CMP_HIDDEN)
    eye = jnp.eye(N_KV_HEADS, dtype=w1.dtype)
    we = jnp.einsum("jrdc,hk->jrhdkc", w1r, eye).reshape(2, CMP_STRIDE * KV_DIM, N_KV_HEADS * CMP_HIDDEN)
    per = pe.reshape(2, CMP_STRIDE, 1, HEAD_DIM)
    pex = jnp.broadcast_to(per, (2, CMP_STRIDE, N_KV_HEADS, HEAD_DIM)).reshape(2, CMP_STRIDE * KV_DIM)
    return pex.astype(F32), we.astype(BF16), w2.astype(BF16)


def _compress_rows(rows3, cw, gain, norm):
    b, nch, width = rows3.shape
    pex, we, w2 = cw
    const = lambda a: pl.BlockSpec(a.shape, lambda i: (0,) * a.ndim)
    return pl.pallas_call(
        functools.partial(_compress_rows_body, norm, nch),
        grid=(b,),
        in_specs=[pl.BlockSpec((1, nch, width), lambda i: (i, 0, 0)), const(pex), const(we), const(w2), const(gain)],
        out_specs=pl.BlockSpec((1, nch, KV_DIM), lambda i: (i, 0, 0)),
        out_shape=jax.ShapeDtypeStruct((b, nch, KV_DIM), F32),
        compiler_params=_cparams("arbitrary"),
        name="compress_rows",
    )(rows3, pex, we, w2, gain)


def _compress_pages(cache3, pt_flat, n_batch, n_pages, cw, gain, norm):
    _, pch, width = cache3.shape
    nch = n_pages * pch
    pex, we, w2 = cw
    const = lambda a: pl.BlockSpec(a.shape, lambda i, pt: (0,) * a.ndim)
    page_spec = lambda j: pl.BlockSpec((1, pch, width), lambda i, pt: (pt[i * n_pages + j], 0, 0))
    grid_spec = pltpu.PrefetchScalarGridSpec(
        num_scalar_prefetch=1, grid=(n_batch,),
        in_specs=[page_spec(j) for j in range(n_pages)] + [const(pex), const(we), const(w2), const(gain)],
        out_specs=pl.BlockSpec((1, nch, KV_DIM), lambda i, pt: (i, 0, 0)))
    return pl.pallas_call(
        functools.partial(_compress_pages_body, norm, nch, n_pages),
        grid_spec=grid_spec,
        out_shape=jax.ShapeDtypeStruct((n_batch, nch, KV_DIM), F32),
        compiler_params=_cparams("arbitrary"),
        name="compress_pages",
    )(pt_flat, *([cache3] * n_pages), pex, we, w2, gain)


def _rel_bucket(dist):
    n = jnp.maximum(dist, 0)
    max_exact = NUM_BUCKETS // 2
    nf = jnp.maximum(n, 1).astype(F32)
    large = max_exact + (jnp.log(nf / max_exact) / math.log(MAX_DISTANCE / max_exact)
                         * (NUM_BUCKETS - max_exact)).astype(jnp.int32)
    large = jnp.minimum(large, NUM_BUCKETS - 1)
    return jnp.where(n < max_exact, n, large)


def _bias_by_distance(rel_bias):
    d = jnp.arange(BIAS_DMAX, dtype=jnp.int32)
    f = rel_bias.astype(F32)[_rel_bucket(d)]
    f = (f - f[BIAS_DMAX - 1:BIAS_DMAX]).T
    return jnp.concatenate([f, jnp.full((N_HEADS, 1), NEG_INF, F32)], axis=1)


def _bias_index(d, valid):
    return np.where(valid, np.clip(d, 0, BIAS_DMAX - 1), BIAS_DMAX).astype(np.int32)


def _bias_table(fext, d, valid):
    return jnp.take(fext, jnp.asarray(_bias_index(d, valid)), axis=1)


def _toeplitz(g, n_rows, k):
    h = g.shape[0]
    w = n_rows + k
    v = jnp.pad(g, ((0, 0), (0, w - g.shape[1])))
    t = jnp.tile(v, (1, n_rows + 1))[:, :n_rows * (w + 1)].reshape(h, n_rows, w + 1)[:, :, :k]
    return t[:, :, ::-1]


def _select_blocks(imp_t, srow, qpos, n_rank):
    qblk = qpos >> 6
    forced = (srow == 0) | (srow == qblk) | (srow == qblk - 1)
    valid = (srow << 6) <= qpos
    imp_t = jnp.where(valid, imp_t + jnp.where(forced, FORCE_SCORE, 0.0), NEG_INF)
    cnt = jnp.zeros(imp_t.shape, jnp.int32)
    for s in range(n_rank):
        r = imp_t[s:s + 1, :]
        beats = (r > imp_t) | ((r == imp_t) & (srow > s))
        cnt = cnt + jnp.where(beats, 1, 0)
    return jnp.where((cnt < N_SEL) & valid, 1.0, 0.0)


def _attn_prompt_body(kt, n_slc, q_ref, gt_ref, kcmp_ref, vcmp_ref, ks_ref, vs_ref, kw_ref, vw_ref,
                      ctab_ref, ntab_ref, wtab_ref, efar_ref, enear_ref, ovl_ref, o_ref):
    i = pl.program_id(1)
    qb = Q_BLOCK
    rows = GROUP * qb
    n_cmp_pad = kcmp_ref.shape[1]
    near_start = pl.multiple_of(jnp.maximum(i - 1, 0) * qb, qb)
    win_start = pl.multiple_of(jnp.maximum(i * qb - WINDOW, 0), qb)
    band = WINDOW + qb
    gates = gt_ref[...]
    kcmp = kcmp_ref[0].astype(BF16)
    vcmp = vcmp_ref[0].astype(BF16)
    lane = lax.broadcasted_iota(jnp.int32, (qb, LANES), 1)
    heads_out = []
    for h in range(N_KV_HEADS):
        qh = q_ref[0, h * GROUP:(h + 1) * GROUP].reshape(rows, LANES)

        s = _dot_nt(qh, kcmp) + ctab_ref[h * GROUP:(h + 1) * GROUP, 0].reshape(rows, n_cmp_pad)
        p, l = _softmax_parts(s)
        pn = p / jnp.maximum(l, 1e-30)
        o_c = _dot(pn.astype(BF16), vcmp)

        psum = pn[0:qb] + pn[qb:2 * qb] + pn[2 * qb:3 * qb] + pn[3 * qb:4 * qb]
        hi, mid, lo = _split3(psum)
        ovl = ovl_ref[...]
        imp_t = _dot_nt(ovl, hi) + _dot_nt(ovl, mid) + _dot_nt(ovl, lo)
        srow = lax.broadcasted_iota(jnp.int32, (n_slc, qb), 0)
        qpos_t = i * qb + lax.broadcasted_iota(jnp.int32, (n_slc, qb), 1)
        sel_t = _select_blocks(imp_t, srow, qpos_t, n_slc)
        if n_slc < LANES:
            sel_t = jnp.concatenate([sel_t, jnp.zeros((LANES - n_slc, qb), F32)], axis=0)
        sel = sel_t.T.astype(BF16)

        kk = ks_ref[pl.ds(near_start, 2 * qb), :]
        vv = vs_ref[pl.ds(near_start, 2 * qb), :]
        mexp = _dot(sel, enear_ref[0])
        mb = jnp.where(mexp > 0.5, ntab_ref[0, h * GROUP:(h + 1) * GROUP], NEG_INF)
        s = _dot_nt(qh, kk).reshape(GROUP, qb, 2 * qb) + mb
        s = s.reshape(rows, 2 * qb)
        m0 = jnp.max(s, axis=-1, keepdims=True)
        ms = jnp.where(m0 == NEG_INF, 0.0, m0)
        p = jnp.exp(s - ms)
        l0 = jnp.sum(p, axis=-1, keepdims=True)
        a0 = _dot(p.astype(BF16), vv)

        def far_tile(t, carry):
            m_old, l_old, acc = carry
            k0 = pl.multiple_of(t * kt, kt)
            kk = ks_ref[pl.ds(k0, kt), :]
            vv = vs_ref[pl.ds(k0, kt), :]
            mexp = _dot(sel, efar_ref[t])
            kpos = k0 + lax.broadcasted_iota(jnp.int32, (qb, kt), 1)
            mb = jnp.where((mexp > 0.5) & (kpos < near_start), 0.0, NEG_INF)
            s = (_dot_nt(qh, kk).reshape(GROUP, qb, kt) + mb[None]).reshape(rows, kt)
            m_new = jnp.maximum(m_old, jnp.max(s, axis=-1, keepdims=True))
            ms = jnp.where(m_new == NEG_INF, 0.0, m_new)
            alpha = jnp.exp(m_old - ms)
            p = jnp.exp(s - ms)
            l_new = alpha * l_old + jnp.sum(p, axis=-1, keepdims=True)
            acc = alpha * acc + _dot(p.astype(BF16), vv)
            return m_new, l_new, acc

        n_far = (near_start + kt - 1) // kt
        _, l_s, acc_s = lax.fori_loop(0, n_far, far_tile, (m0, l0, a0))
        o_s = acc_s / jnp.maximum(l_s, 1e-30)

        kk = kw_ref[pl.ds(win_start, band), :]
        vv = vw_ref[pl.ds(win_start, band), :]
        s = _dot_nt(qh, kk) + wtab_ref[0, h * GROUP:(h + 1) * GROUP].reshape(rows, band)
        p, l = _softmax_parts(s)
        o_w = _dot(p.astype(BF16), vv) / jnp.maximum(l, 1e-30)

        for g in range(GROUP):
            c = (h * GROUP + g) * N_BRANCH
            sl = slice(g * qb, (g + 1) * qb)
            heads_out.append(gates[:, c:c + 1] * o_c[sl] + gates[:, c + 1:c + 2] * o_s[sl]
                             + gates[:, c + 2:c + 3] * o_w[sl])

    tiles = []
    for j in range(N_HEADS // 2):
        a, b = heads_out[2 * j], heads_out[2 * j + 1]
        if (2 * j) // GROUP == 1:
            a = pltpu.roll(a, HEAD_DIM, axis=1)
        else:
            b = pltpu.roll(b, HEAD_DIM, axis=1)
        tiles.append(jnp.where(lane < HEAD_DIM, a, b))
    o_ref[...] = jnp.concatenate(tiles, axis=-1).astype(BF16)


def _attn_prompt(qp, gates, kcmp, vcmp, ksb, vsb, kwb, vwb, fext, batch, seq):
    qb = Q_BLOCK
    nqb = seq // qb
    n_slc = seq // SLC_BLOCK
    n_cmp_pad = kcmp.shape[1]
    n_cmp = n_cmp_pad - 1
    kt = min(512, seq)
    n_kt = seq // kt
    band = WINDOW + qb
    iq = np.arange(qb)

    per_qb = qb // CMP_STRIDE
    half = 2 * per_qb
    m = np.arange(-half, half)
    d = iq[:, None] - (m[None, :] * CMP_STRIDE + CMP_BLOCK - 1)
    assert d[:, 0].min() >= MAX_DISTANCE and d[:, -1].max() < 0
    gwide = jnp.concatenate([jnp.zeros((N_HEADS, qb, n_cmp_pad - half), F32), _bias_table(fext, d, d >= 0),
                             jnp.full((N_HEADS, qb, n_cmp_pad - half), NEG_INF, F32)], axis=2)
    ctab = jnp.stack([gwide[:, :, n_cmp_pad - per_qb * i:2 * n_cmp_pad - per_qb * i] for i in range(nqb)], axis=1)
    assert (n_cmp_pad - 1) * CMP_STRIDE + CMP_BLOCK - 1 >= seq and n_cmp == n_cmp_pad - 1

    def toeplitz_variants(n_var, k, valid):
        j = np.arange(qb + k - 1)
        tabs = []
        for v in range(n_var):
            dj = qb * v + j - (k - 1)
            tabs.append(_toeplitz(_bias_table(fext, dj, valid(dj)), qb, k))
        return jnp.stack(tabs, axis=0)

    ntab = toeplitz_variants(2, 2 * qb, lambda dj: dj >= 0)
    nv = WINDOW // qb + 1
    wtab = toeplitz_variants(nv, band, lambda dj: (dj >= 0) & (dj < WINDOW))

    srow = np.arange(LANES)[None, :, None]
    efar = ((np.arange(n_kt)[:, None, None] * kt + np.arange(kt)[None, None, :]) // SLC_BLOCK == srow)
    nstart = np.maximum(np.arange(nqb) - 1, 0) * qb
    enear = ((nstart[:, None, None] + np.arange(2 * qb)[None, None, :]) // SLC_BLOCK == srow)
    c_start = np.arange(n_cmp_pad) * CMP_STRIDE
    s_start = np.arange(n_slc) * SLC_BLOCK
    ovl = ((c_start[None, :] < s_start[:, None] + SLC_BLOCK) & (c_start[None, :] + CMP_BLOCK > s_start[:, None])
           & (np.arange(n_cmp_pad) < n_cmp)[None, :])
    efar = jnp.asarray(efar, BF16)
    enear = jnp.asarray(enear, BF16)
    ovl = jnp.asarray(ovl, BF16)

    per_batch = lambda a: pl.BlockSpec((seq, a.shape[1]), lambda b, i: (b, 0))
    const = lambda a: pl.BlockSpec(a.shape, lambda b, i: (0,) * a.ndim)
    return pl.pallas_call(
        functools.partial(_attn_prompt_body, kt, n_slc),
        grid=(batch, nqb),
        in_specs=[pl.BlockSpec((1, N_HEADS, qb, LANES), lambda b, i: (b, 0, i, 0)),
                  pl.BlockSpec((qb, LANES), lambda b, i: (b * nqb + i, 0)),
                  pl.BlockSpec((1, n_cmp_pad, KV_DIM), lambda b, i: (b, 0, 0)),
                  pl.BlockSpec((1, n_cmp_pad, KV_DIM), lambda b, i: (b, 0, 0)),
                  per_batch(ksb), per_batch(vsb), per_batch(kwb), per_batch(vwb),
                  pl.BlockSpec((N_HEADS, 1, qb, n_cmp_pad), lambda b, i: (0, i, 0, 0)),
                  pl.BlockSpec((1, N_HEADS, qb, 2 * qb), lambda b, i: (jnp.minimum(i, 1), 0, 0, 0)),
                  pl.BlockSpec((1, N_HEADS, qb, band), lambda b, i: (jnp.minimum(i, nv - 1), 0, 0, 0)),
                  const(efar),
                  pl.BlockSpec((1, LANES, 2 * qb), lambda b, i: (i, 0, 0)),
                  const(ovl)],
        out_specs=pl.BlockSpec((qb, ATTN_DIM), lambda b, i: (b * nqb + i, 0)),
        out_shape=jax.ShapeDtypeStruct((batch * seq, ATTN_DIM), BF16),
        compiler_params=_cparams("arbitrary", "arbitrary"),
        name="attn_prompt",
    )(qp, gates, kcmp, vcmp, ksb, vsb, kwb, vwb, ctab, ntab, wtab, efar, enear, ovl)


def _attn_sample_body(n_pages, ts, past_len, pt_ref, *refs):
    kpages = refs[:n_pages]
    vpages = refs[n_pages:2 * n_pages]
    (q_ref, gt_ref, kcmp_ref, vcmp_ref, ksn_ref, vsn_ref, kwc_ref, vwc_ref, kwn_ref, vwn_ref,
     ctab_ref, stab_ref, sntab_ref, wtab_ref, wntab_ref, eexp_ref, ovl_ref, o_ref) = refs[2 * n_pages:]
    rows = GROUP * N_KV_HEADS * ts
    rq = N_KV_HEADS * ts
    q = q_ref[0]
    gates = gt_ref[0]

    s = _dot_nt(q, kcmp_ref[0].astype(BF16)) + ctab_ref[...]
    p, l = _softmax_parts(s)
    pn = p / jnp.maximum(l, 1e-30)
    o_c = _dot(pn.astype(BF16), vcmp_ref[0].astype(BF16))

    psum = pn[0:rq]
    for g in range(1, GROUP):
        psum = psum + pn[g * rq:(g + 1) * rq]
    hi, mid, lo = _split3(psum)
    ovl = ovl_ref[...]
    imp = _dot(hi, ovl) + _dot(mid, ovl) + _dot(lo, ovl)
    n_slc = -(-(past_len + ts) // SLC_BLOCK)
    blk = lax.broadcasted_iota(jnp.int32, (rq, LANES), 1)
    qpos = past_len + (lax.broadcasted_iota(jnp.int32, (rq, LANES), 0) & (ts - 1))
    qblk = qpos >> 6
    forced = (blk == 0) | (blk == qblk) | (blk == qblk - 1)
    valid = ((blk << 6) <= qpos) & (blk < n_slc)
    imp = jnp.where(valid, imp + jnp.where(forced, FORCE_SCORE, 0.0), NEG_INF)
    cnt = jnp.zeros((rq, LANES), jnp.int32)
    for sidx in range(n_slc):
        r = imp[:, sidx:sidx + 1]
        beats = (r > imp) | ((r == imp) & (blk > sidx))
        cnt = cnt + jnp.where(beats, 1, 0)
    sel = jnp.where((cnt < N_SEL) & valid, 1.0, 0.0)
    sel = jnp.concatenate([sel] * GROUP, axis=0).astype(BF16)

    kc = jnp.concatenate([p_[0] for p_ in kpages], axis=0).astype(BF16)
    vc = jnp.concatenate([p_[0] for p_ in vpages], axis=0).astype(BF16)
    mexp = _dot(sel, eexp_ref[...])
    s1 = jnp.where(mexp > 0.5, _dot_nt(q, kc) + stab_ref[...], NEG_INF)
    last = sel[:, n_slc - 1:n_slc].astype(F32)
    s2 = jnp.where(last > 0.5, _dot_nt(q, ksn_ref[0].astype(BF16)) + sntab_ref[...], NEG_INF)
    m = jnp.maximum(jnp.max(s1, axis=-1, keepdims=True), jnp.max(s2, axis=-1, keepdims=True))
    m = jnp.where(m == NEG_INF, 0.0, m)
    p1 = jnp.exp(s1 - m)
    p2 = jnp.exp(s2 - m)
    l = jnp.sum(p1, axis=-1, keepdims=True) + jnp.sum(p2, axis=-1, keepdims=True)
    o_s = (_dot(p1.astype(BF16), vc) + _dot(p2.astype(BF16), vsn_ref[0].astype(BF16))) / jnp.maximum(l, 1e-30)

    s1 = _dot_nt(q, kwc_ref[0].astype(BF16)) + wtab_ref[...]
    s2 = _dot_nt(q, kwn_ref[0].astype(BF16)) + wntab_ref[...]
    m = jnp.maximum(jnp.max(s1, axis=-1, keepdims=True), jnp.max(s2, axis=-1, keepdims=True))
    m = jnp.where(m == NEG_INF, 0.0, m)
    p1 = jnp.exp(s1 - m)
    p2 = jnp.exp(s2 - m)
    l = jnp.sum(p1, axis=-1, keepdims=True) + jnp.sum(p2, axis=-1, keepdims=True)
    o_w = (_dot(p1.astype(BF16), vwc_ref[0].astype(BF16))
           + _dot(p2.astype(BF16), vwn_ref[0].astype(BF16))) / jnp.maximum(l, 1e-30)

    o_ref[0] = gates[:, 0:1] * o_c + gates[:, 1:2] * o_s + gates[:, 2:3] * o_w


def _attn_sample(q_s, gates_s, kcmp, vcmp, ks_new, vs_new, kw_new, vw_new, cache_ks, cache_vs,
                 cache_kw, cache_vw, pt_flat, fext, n_batch, ts, n_pages, page):
    past_len = n_pages * page
    w_buf = cache_kw.shape[1]
    rows = GROUP * N_KV_HEADS * ts
    n_new = 8
    n_cmp_pad = kcmp.shape[1]
    n_cmp = n_cmp_pad - 1
    n_slc = -(-(past_len + ts) // SLC_BLOCK)

    q5 = q_s.reshape(n_batch, ts, N_KV_HEADS, GROUP, HEAD_DIM).transpose(0, 3, 2, 1, 4)
    eye = jnp.eye(N_KV_HEADS, dtype=q_s.dtype)
    qr = jnp.einsum("bghtd,hk->bghtkd", q5, eye).reshape(n_batch, rows, LANES).astype(BF16)
    g5 = gates_s[:, :N_HEADS * N_BRANCH].reshape(n_batch, ts, N_KV_HEADS, GROUP, N_BRANCH).transpose(0, 3, 2, 1, 4)
    gr = jnp.pad(g5.reshape(n_batch, rows, N_BRANCH), ((0, 0), (0, 0), (0, LANES - N_BRANCH)))
    pad_new = lambda a: jnp.pad(a.reshape(n_batch, ts, KV_DIM), ((0, 0), (0, n_new - ts), (0, 0)))
    ks_new, vs_new, kw_new, vw_new = map(pad_new, (ks_new, vs_new, kw_new, vw_new))

    g_i, h_i, t_i = np.meshgrid(np.arange(GROUP), np.arange(N_KV_HEADS), np.arange(ts), indexing="ij")
    head = (h_i * GROUP + g_i).reshape(rows)
    tq = t_i.reshape(rows)
    pos_q = past_len + tq

    f_rows = fext[jnp.asarray(head)]

    def table(d, valid):
        return jnp.take_along_axis(f_rows, jnp.asarray(_bias_index(d, valid)), axis=1)

    nn = np.arange(n_cmp_pad)
    d = pos_q[:, None] - (nn[None, :] * CMP_STRIDE + CMP_BLOCK - 1)
    ctab = table(d, (d >= 0) & (nn < n_cmp)[None, :])
    near = np.arange(past_len - MAX_DISTANCE, past_len)
    d = pos_q[:, None] - near[None, :]
    assert past_len >= MAX_DISTANCE and d.min() >= 0
    stab = jnp.concatenate([jnp.zeros((rows, past_len - MAX_DISTANCE), F32), table(d, d >= 0)], axis=1)
    jn = np.arange(n_new)
    d = tq[:, None] - jn[None, :]
    sntab = table(d, (d >= 0) & (jn < ts)[None, :])
    pos_w = past_len - w_buf + np.arange(w_buf)
    d = pos_q[:, None] - pos_w[None, :]
    wtab = table(d, (d >= 0) & (d < WINDOW) & (pos_w >= 0)[None, :])
    wntab = table(tq[:, None] - jn[None, :], (tq[:, None] >= jn[None, :]) & (jn < ts)[None, :])

    eexp = jnp.asarray(np.arange(past_len)[None, :] // SLC_BLOCK == np.arange(LANES)[:, None], BF16)
    c_start = nn * CMP_STRIDE
    s_start = np.arange(LANES) * SLC_BLOCK
    ovl = jnp.asarray((c_start[:, None] < s_start[None, :] + SLC_BLOCK) & (c_start[:, None] + CMP_BLOCK > s_start[None, :])
                      & (nn < n_cmp)[:, None] & (np.arange(LANES) < n_slc)[None, :], BF16)

    const = lambda a: pl.BlockSpec(a.shape, lambda b, pt: (0,) * a.ndim)
    per_b = lambda a: pl.BlockSpec((1,) + a.shape[1:], lambda b, pt: (b,) + (0,) * (a.ndim - 1))
    page_spec = lambda j: pl.BlockSpec((1, page, KV_DIM), lambda b, pt: (pt[b * n_pages + j], 0, 0))
    small = [qr, gr, kcmp, vcmp, ks_new, vs_new, cache_kw, cache_vw, kw_new, vw_new]
    consts = [ctab, stab, sntab, wtab, wntab, eexp, ovl]
    grid_spec = pltpu.PrefetchScalarGridSpec(
        num_scalar_prefetch=1, grid=(n_batch,),
        in_specs=[page_spec(j) for j in range(n_pages)] * 2 + [per_b(a) for a in small] + [const(a) for a in consts],
        out_specs=pl.BlockSpec((1, rows, LANES), lambda b, pt: (b, 0, 0)))
    o = pl.pallas_call(
        functools.partial(_attn_sample_body, n_pages, ts, past_len),
        grid_spec=grid_spec,
        out_shape=jax.ShapeDtypeStruct((n_batch, rows, LANES), F32),
        compiler_params=_cparams("arbitrary"),
        name="attn_sample",
    )(pt_flat, *([cache_ks] * n_pages), *([cache_vs] * n_pages), *small, *consts)
    o6 = o.reshape(n_batch, GROUP, N_KV_HEADS, ts, N_KV_HEADS, HEAD_DIM)
    o5 = jnp.stack([o6[:, :, h, :, h] for h in range(N_KV_HEADS)], axis=2)
    return o5.transpose(0, 3, 2, 1, 4).reshape(n_batch * ts, ATTN_DIM).astype(BF16)


def _post1_body(tm, x_ref, co_ref, at_ref, wo_ref, nf_ref, wr_ref, br_ref, tri_ref, run0_ref,
                h_ref, hn_ref, rt_ref, cnt_ref, run_ref):
    @pl.when(pl.program_id(0) == 0)
    def _():
        run_ref[...] = run0_ref[...]

    h = x_ref[...] + _dot(co_ref[...], wo_ref[0:CONV_DIM]) + _dot(at_ref[...], wo_ref[CONV_DIM:CONV_DIM + ATTN_DIM])
    hn = _rms(h, nf_ref[...])
    h_ref[...] = h
    _store_token_tiles(hn_ref, hn, tm)

    hi = hn.astype(BF16)
    lo = (hn - hi.astype(F32)).astype(BF16)
    wr = wr_ref[...]
    whi = wr.astype(BF16)
    wlo = (wr - whi.astype(F32)).astype(BF16)
    logits = _dot(hi, whi) + _dot(lo, whi) + _dot(hi, wlo) + br_ref[...]

    lane_i = lax.broadcasted_iota(jnp.int32, (tm, LANES), 1)
    lane = lane_i.astype(F32)
    big = float(LANES)
    gmask = (lane_i >= ROUTER_GROUP_LANE) & (lane_i < ROUTER_GROUP_LANE + N_GROUPS)
    lg = jnp.where(gmask, logits, NEG_INF)
    eg = jnp.exp(lg - jnp.max(lg, axis=-1, keepdims=True))
    pg = eg / jnp.sum(eg, axis=-1, keepdims=True)
    gw = jnp.max(pg, axis=-1, keepdims=True)
    grp = jnp.min(jnp.where(gmask & (pg == gw), lane, big), axis=-1, keepdims=True) - ROUTER_GROUP_LANE

    group_of_lane = (lane_i >> 3).astype(F32)
    emask = (lane_i < N_EXPERTS) & (group_of_lane == grp)
    le = jnp.where(emask, logits, NEG_INF)
    ee = jnp.exp(le - jnp.max(le, axis=-1, keepdims=True))
    pe = jnp.where(emask, ee / jnp.sum(ee, axis=-1, keepdims=True), -1.0)
    v1 = jnp.max(pe, axis=-1, keepdims=True)
    i1 = jnp.min(jnp.where(pe == v1, lane, big), axis=-1, keepdims=True)
    pe2 = jnp.where(lane == i1, -1.0, pe)
    v2 = jnp.max(pe2, axis=-1, keepdims=True)
    i2 = jnp.min(jnp.where(pe2 == v2, lane, big), axis=-1, keepdims=True)
    tot = v1 + v2
    w1 = v1 / tot * gw
    w2 = v2 / tot * gw

    oh1 = jnp.where(lane == i1, 1.0, 0.0)
    oh2 = jnp.where(lane == i2, 1.0, 0.0)
    both = oh1 + oh2
    before = _dot(tri_ref[...], both.astype(BF16)) + run_ref[0:1]
    r1 = jnp.sum(oh1 * before, axis=-1, keepdims=True)
    r2 = jnp.sum(oh2 * before, axis=-1, keepdims=True)
    run = run_ref[0:1] + jnp.sum(both, axis=0, keepdims=True)
    run_ref[...] = jnp.broadcast_to(run, run_ref.shape)
    cnt_ref[...] = jnp.broadcast_to(run, cnt_ref.shape)

    rt = jnp.where(lane_i == 0, i1, 0.0)
    rt = jnp.where(lane_i == 1, i2, rt)
    rt = jnp.where(lane_i == 2, r1, rt)
    rt = jnp.where(lane_i == 3, r2, rt)
    rt = jnp.where(lane_i == 4, w1, rt)
    rt = jnp.where(lane_i == 5, w2, rt)
    rt_ref[...] = rt


def _post1(x2d, co, at, wo, nf, wr, br, run0, tm):
    n, d = x2d.shape
    tri = jnp.asarray(np.tril(np.ones((tm, tm), np.float32), -1), BF16)
    rows = lambda w: pl.BlockSpec((tm, w), lambda i: (i, 0))
    const = lambda a: pl.BlockSpec(a.shape, lambda i: (0,) * a.ndim)
    return pl.pallas_call(
        functools.partial(_post1_body, tm),
        grid=(n // tm,),
        in_specs=[rows(d), rows(CONV_DIM), rows(ATTN_DIM), const(wo), const(nf), const(wr), const(br),
                  const(tri), const(run0)],
        out_specs=[rows(d), pl.BlockSpec((tm * TOKEN_TILE_ROWS, LANES), lambda i: (i, 0)), rows(LANES),
                   pl.BlockSpec((8, LANES), lambda i: (0, 0))],
        out_shape=[jax.ShapeDtypeStruct((n, d), F32), jax.ShapeDtypeStruct((n * TOKEN_TILE_ROWS, LANES), F32),
                   jax.ShapeDtypeStruct((n, LANES), F32), jax.ShapeDtypeStruct((8, LANES), F32)],
        scratch_shapes=[pltpu.VMEM((8, LANES), F32)],
        compiler_params=_cparams("arbitrary"),
        name="post1",
    )(x2d, co, at, wo, nf, wr, br, tri, run0)


def _token_copy(src_ref, dst_ref, s, d, sem):
    r = TOKEN_TILE_ROWS
    return pltpu.make_async_copy(src_ref.at[pl.ds(pl.multiple_of(s * r, r), r)],
                                 dst_ref.at[pl.ds(pl.multiple_of(d * r, r), r)], sem)


def _permute_tokens(per_step, src_ref, out_ref, sem, src_dst_of):
    window = min(DMA_WINDOW, per_step)
    base = pl.program_id(0) * per_step

    def issue(j, _):
        s, d = src_dst_of(base + j)
        _token_copy(src_ref, out_ref, s, d, sem).start()
        return 0

    def retire(j, _):
        _token_copy(src_ref, out_ref, 0, 0, sem).wait()
        return 0

    def retire_and_issue(j, _):
        retire(j, 0)
        return issue(j, 0)

    lax.fori_loop(0, window, issue, 0)
    lax.fori_loop(window, per_step, retire_and_issue, 0)
    lax.fori_loop(0, window, retire, 0)


def _scatter_rows_body(per_step, dest_ref, src_ref, init_ref, out_ref, sem):
    del init_ref
    _permute_tokens(per_step, src_ref, out_ref, sem, lambda a: (a >> 1, dest_ref[a]))


def _gather_rows_body(per_step, n_tok, dest_ref, src_ref, out_ref, sem):
    _permute_tokens(per_step, src_ref, out_ref, sem, lambda a: (dest_ref[a], (a & 1) * n_tok + (a >> 1)))


def _permute_steps(n_assign):
    per_step = min(PERMUTE_STEP, n_assign)
    assert n_assign % per_step == 0
    return per_step, n_assign // per_step


def _scatter_rows(dest, src, slots):
    per_step, steps = _permute_steps(dest.shape[0])
    any_spec = pl.BlockSpec(memory_space=pl.ANY)
    return pl.pallas_call(
        functools.partial(_scatter_rows_body, per_step),
        grid_spec=pltpu.PrefetchScalarGridSpec(num_scalar_prefetch=1, grid=(steps,), in_specs=[any_spec, any_spec],
                                               out_specs=any_spec, scratch_shapes=[pltpu.SemaphoreType.DMA(())]),
        out_shape=jax.ShapeDtypeStruct(slots.shape, slots.dtype),
        input_output_aliases={2: 0},
        compiler_params=pltpu.CompilerParams(dimension_semantics=("arbitrary",)),
        name="scatter_rows",
    )(dest, src, slots)


def _gather_rows(dest, src):
    n_assign = dest.shape[0]
    per_step, steps = _permute_steps(n_assign)
    any_spec = pl.BlockSpec(memory_space=pl.ANY)
    return pl.pallas_call(
        functools.partial(_gather_rows_body, per_step, n_assign // 2),
        grid_spec=pltpu.PrefetchScalarGridSpec(num_scalar_prefetch=1, grid=(steps,), in_specs=[any_spec],
                                               out_specs=any_spec, scratch_shapes=[pltpu.SemaphoreType.DMA(())]),
        out_shape=jax.ShapeDtypeStruct((n_assign * TOKEN_TILE_ROWS, LANES), src.dtype),
        compiler_params=pltpu.CompilerParams(dimension_semantics=("arbitrary",)),
        name="gather_rows",
    )(dest, src)


def _experts_body(be_ref, nu_ref, x_ref, wg_ref, wu_ref, wd_ref, o_ref, wg_s, wu_s, wd_s):
    i = pl.program_id(0)

    @pl.when(i < nu_ref[0])
    def _():
        prev = be_ref[jnp.maximum(i - 1, 0)]

        @pl.when((i == 0) | (be_ref[i] != prev))
        def _():
            wg_s[...] = wg_ref[0].astype(BF16)
            wu_s[...] = wu_ref[0].astype(BF16)
            wd_s[...] = wd_ref[0].astype(BF16)

        x = _load_token_tiles(x_ref, EXPERT_ROWS, TOKEN_TILE_ROWS).astype(BF16)
        g = _dot(x, wg_s[...])
        u = _dot(x, wu_s[...])
        a = g * _sigmoid(g) * u
        _store_token_tiles(o_ref, _dot(a.astype(BF16), wd_s[...]), EXPERT_ROWS)

    @pl.when(i >= nu_ref[0])
    def _():
        o_ref[...] = jnp.zeros_like(o_ref)


def _experts(blk_expert, n_used, xs, wg, wu, wd):
    blk_rows = EXPERT_ROWS * TOKEN_TILE_ROWS
    n_blk = xs.shape[0] // blk_rows
    _, d, de = wg.shape
    xmap = lambda i, be, nu: (jnp.minimum(i, jnp.maximum(nu[0] - 1, 0)), 0)
    wmap = lambda i, be, nu: (be[jnp.minimum(i, jnp.maximum(nu[0] - 1, 0))], 0, 0)
    grid_spec = pltpu.PrefetchScalarGridSpec(
        num_scalar_prefetch=2, grid=(n_blk,),
        in_specs=[pl.BlockSpec((blk_rows, LANES), xmap), pl.BlockSpec((1, d, de), wmap),
                  pl.BlockSpec((1, d, de), wmap), pl.BlockSpec((1, de, d), wmap)],
        out_specs=pl.BlockSpec((blk_rows, LANES), lambda i, be, nu: (i, 0)),
        scratch_shapes=[pltpu.VMEM((d, de), BF16), pltpu.VMEM((d, de), BF16), pltpu.VMEM((de, d), BF16)])
    return pl.pallas_call(
        _experts_body, grid_spec=grid_spec,
        out_shape=jax.ShapeDtypeStruct(xs.shape, F32),
        compiler_params=_cparams("arbitrary"),
        name="experts",
    )(blk_expert, n_used, xs, wg, wu, wd)


def _post2_body(tm, h_ref, y0_ref, y1_ref, rt_ref, p_ref, wple_ref, wpg_ref, bpg_ref, np_ref, o_ref):
    rt = rt_ref[...]
    moe = (_load_token_tiles(y0_ref, tm, TOKEN_TILE_ROWS) * rt[:, 4:5]
           + _load_token_tiles(y1_ref, tm, TOKEN_TILE_ROWS) * rt[:, 5:6])
    h = h_ref[...] + moe
    gate = _sigmoid(_dot(_rms(h, np_ref[...]).astype(BF16), wpg_ref[...]) + bpg_ref[...])
    o_ref[...] = h + gate * _dot(p_ref[...].astype(BF16), wple_ref[...])


def _post2(h, ya, rt, p2d, wple, wpg, bpg, npl, tm):
    n, d = h.shape
    nt = n // tm
    rows = lambda w: pl.BlockSpec((tm, w), lambda i: (i, 0))
    const = lambda a: pl.BlockSpec(a.shape, lambda i: (0,) * a.ndim)
    tiles = lambda k: pl.BlockSpec((tm * TOKEN_TILE_ROWS, LANES), lambda i: (i + k * nt, 0))
    return pl.pallas_call(
        functools.partial(_post2_body, tm),
        grid=(nt,),
        in_specs=[rows(d), tiles(0), tiles(1), rows(LANES), rows(p2d.shape[1]), const(wple), const(wpg), const(bpg),
                  const(npl)],
        out_specs=rows(d),
        out_shape=jax.ShapeDtypeStruct((n, d), F32),
        compiler_params=_cparams("arbitrary"),
        name="post2",
    )(h, ya, ya, rt, p2d, wple, wpg, bpg, npl)


def _row_tile(n, cap=512):
    t = min(cap, n)
    assert n % t == 0 and t % 8 == 0
    return t


def kernel(x_prompt, x_sample, p_prompt, p_sample, cache_k_cmp, cache_v_cmp, cache_k_slc, cache_v_slc, cache_k_win, cache_v_win, state_conv, page_table, w_in, w_out, conv_w, norm_mix, norm_ffn, norm_ple, q_norm, k_norm, cmp_pe_k, cmp_w1_k, cmp_w2_k, cmp_pe_v, cmp_w1_v, cmp_w2_v, rel_bias, w_router_group, b_router_group, w_router_expert, b_router_expert, w_exp_gate, w_exp_up, w_exp_down, w_ple, w_ple_gate, b_ple_gate):
    assert w_in.shape[0] == 1, "single-layer step"
    bp, t, d = x_prompt.shape
    bs, ts, _ = x_sample.shape
    n_pages = page_table.shape[1]
    page = cache_k_cmp.shape[2]
    past_len = n_pages * page
    w_buf = cache_k_win.shape[2]
    n_phys = cache_k_cmp.shape[1]
    assert t % Q_BLOCK == 0 and t >= WINDOW + Q_BLOCK and page % CMP_STRIDE == 0 and ts == 4 and d == D_MODEL
    assert past_len % SLC_BLOCK == 0
    np_rows, ns_rows = bp * t, bs * ts

    row = lambda v: v.reshape(1, -1).astype(F32)
    w_in_b = jnp.pad(w_in[0], ((0, 0), (0, Z_COLS - w_in.shape[2]))).astype(BF16)
    qn = row(jnp.tile(q_norm[0], N_HEADS))
    kn1 = row(jnp.tile(k_norm[0, 1], N_KV_HEADS))
    kn2 = row(jnp.tile(k_norm[0, 2], N_KV_HEADS))
    bd = jnp.asarray(np.kron(np.eye(N_HEADS), np.ones((HEAD_DIM, HEAD_DIM))), BF16)
    pw = (row(norm_mix[0]), w_in_b, qn, kn1, kn2, conv_w[0].astype(F32), bd)
    cw_k = _compress_weights(cmp_pe_k[0], cmp_w1_k[0], cmp_w2_k[0])
    cw_v = _compress_weights(cmp_pe_v[0], cmp_w1_v[0], cmp_w2_v[0])
    kn0 = row(k_norm[0, 0])
    fext = _bias_by_distance(rel_bias)
    pt_flat = page_table.reshape(-1).astype(jnp.int32)
    wr = jnp.zeros((d, LANES), F32).at[:, :N_EXPERTS].set(w_router_expert[0])
    wr = wr.at[:, ROUTER_GROUP_LANE:ROUTER_GROUP_LANE + N_GROUPS].set(w_router_group[0])
    br = jnp.zeros((1, LANES), F32).at[0, :N_EXPERTS].set(b_router_expert[0])
    br = br.at[0, ROUTER_GROUP_LANE:ROUTER_GROUP_LANE + N_GROUPS].set(b_router_group[0])
    wo_b = w_out[0].astype(BF16)
    wple_b = w_ple[0].astype(BF16)
    wpg_b = w_ple_gate[0].astype(BF16)

    tm_p = _row_tile(t)
    (co_p, q_p, kc_p, vc_p, ks_p, vs_p, kw_p, vw_p, gt_p, cs_p, ksb, vsb, kwb, vwb) = _project(
        x_prompt.reshape(np_rows, d), bp, t, tm_p, pw)
    chunk_w = CMP_STRIDE * KV_DIM
    kcmp_p = _compress_rows(kc_p.reshape(bp, t // CMP_STRIDE, chunk_w), cw_k, kn0, True)
    vcmp_p = _compress_rows(vc_p.reshape(bp, t // CMP_STRIDE, chunk_w), cw_v, kn0, False)
    at_p = _attn_prompt(q_p, gt_p, kcmp_p, vcmp_p, ksb, vsb, kwb, vwb, fext, bp, t)

    st = state_conv[0].astype(F32)
    s0 = jnp.repeat(st[:, 0], ts, axis=0)
    s1 = jnp.repeat(st[:, 1], ts, axis=0)
    (co_s, q_s, kc_s, vc_s, ks_s, vs_s, kw_s, vw_s, gt_s, u_s) = _project(
        x_sample.reshape(ns_rows, d), bs, ts, ns_rows, pw, state=(s0, s1))
    pch = page // CMP_STRIDE
    kcmp_s = _compress_pages(cache_k_cmp[0].reshape(n_phys, pch, chunk_w), pt_flat, bs, n_pages, cw_k, kn0, True)
    vcmp_s = _compress_pages(cache_v_cmp[0].reshape(n_phys, pch, chunk_w), pt_flat, bs, n_pages, cw_v, kn0, False)
    kwc = cache_k_win[0].reshape(bs, w_buf, KV_DIM)
    vwc = cache_v_win[0].reshape(bs, w_buf, KV_DIM)
    at_s = _attn_sample(q_s, gt_s, kcmp_s, vcmp_s, ks_s, vs_s, kw_s, vw_s,
                        cache_k_slc[0].reshape(n_phys, page, KV_DIM), cache_v_slc[0].reshape(n_phys, page, KV_DIM),
                        kwc, vwc, pt_flat, fext, bs, ts, n_pages, page)

    tp1 = _row_tile(np_rows)
    ts1 = _row_tile(ns_rows)
    nf = row(norm_ffn[0])
    h_p, hn_p, rt_p, cnt_p = _post1(x_prompt.reshape(np_rows, d), co_p, at_p, wo_b, nf, wr, br,
                                    jnp.zeros((8, LANES), F32), tp1)
    h_s, hn_s, rt_s, cnt_s = _post1(x_sample.reshape(ns_rows, d), co_s, at_s, wo_b, nf, wr, br, cnt_p, ts1)

    counts = cnt_s[0, :N_EXPERTS].astype(jnp.int32)
    padded = (counts + EXPERT_ROWS - 1) // EXPERT_ROWS * EXPERT_ROWS
    pad_end = jnp.cumsum(padded)
    pad_start = pad_end - padded
    n_assign = 2 * (np_rows + ns_rows)
    n_blk = (n_assign + N_EXPERTS * (EXPERT_ROWS - 1) + EXPERT_ROWS - 1) // EXPERT_ROWS
    blk_expert = jnp.clip(jnp.searchsorted(pad_end, jnp.arange(n_blk, dtype=jnp.int32) * EXPERT_ROWS, side="right"),
                          0, N_EXPERTS - 1).astype(jnp.int32)
    n_used = (pad_end[-1:] // EXPERT_ROWS).astype(jnp.int32)

    def dest_of(rt):
        e = rt[:, 0:2].astype(jnp.int32).reshape(-1)
        return pad_start[e] + rt[:, 2:4].astype(jnp.int32).reshape(-1)

    dest_p = dest_of(rt_p)
    dest_s = dest_of(rt_s)

    xs = jnp.zeros((n_blk * EXPERT_ROWS * TOKEN_TILE_ROWS, LANES), F32)
    xs = _scatter_rows(dest_p, hn_p, xs)
    xs = _scatter_rows(dest_s, hn_s, xs)
    yb = _experts(blk_expert, n_used, xs, w_exp_gate[0], w_exp_up[0], w_exp_down[0])
    ya_p = _gather_rows(dest_p, yb)
    ya_s = _gather_rows(dest_s, yb)

    bpg = row(b_ple_gate[0])
    npl = row(norm_ple[0])
    y_p = _post2(h_p, ya_p, rt_p, p_prompt[0].reshape(np_rows, -1), wple_b, wpg_b, bpg, npl, tp1)
    y_s = _post2(h_s, ya_s, rt_s, p_sample[0].reshape(ns_rows, -1), wple_b, wpg_b, bpg, npl, ts1)

    kv5 = lambda a, b, s: a.reshape(1, b, s, N_KV_HEADS, HEAD_DIM)
    wp = min(WINDOW, t)
    win_p = lambda a: kv5(a, bp, t)[:, :, t - wp:]
    win_s = lambda c, new: jnp.concatenate([c[0], new.reshape(bs, ts, N_KV_HEADS, HEAD_DIM)], axis=1)[None, :, ts:]
    conv_p = cs_p[:, 8 - (CONV_K - 1):][None]
    conv_s = u_s.reshape(bs, ts, CONV_DIM)[:, ts - (CONV_K - 1):][None]
    return (y_p.reshape(bp, t, d), y_s.reshape(bs, ts, d),
            kv5(kc_p, bp, t), kv5(vc_p, bp, t), kv5(ks_p, bp, t), kv5(vs_p, bp, t), win_p(kw_p), win_p(vw_p), conv_p,
            kv5(kc_s, bs, ts), kv5(vc_s, bs, ts), kv5(ks_s, bs, ts), kv5(vs_s, bs, ts),
            win_s(cache_k_win, kw_s), win_s(cache_v_win, vw_s), conv_s)
```

```python
import functools
import math

import numpy as np
import jax
import jax.numpy as jnp
from jax import lax
from jax.experimental import pallas as pl
from jax.experimental.pallas import tpu as pltpu

F32 = jnp.float32
BF16 = jnp.bfloat16
NEG_INF = float("-inf")

HEAD_DIM = 64
N_HEADS = 8
N_KV_HEADS = 2
GROUP = N_HEADS // N_KV_HEADS
CONV_DIM = 512
ATTN_DIM = 512
KV_DIM = N_KV_HEADS * HEAD_DIM
N_BRANCH = 3
CONV_K = 3
CMP_BLOCK = 32
CMP_STRIDE = 16
CMP_HIDDEN = 256
SLC_BLOCK = 64
N_SEL = 16
WINDOW = 512
Q_BLOCK = 128
FORCE_SCORE = 1e4
NUM_BUCKETS = 32
MAX_DISTANCE = 128
N_GROUPS = 4
EXPERTS_PER_GROUP = 8
N_EXPERTS = N_GROUPS * EXPERTS_PER_GROUP
D_EXPERT = 512
EPS = 1e-6

D_MODEL = 1024
LANES = 128
TOKEN_TILE_ROWS = D_MODEL // LANES
Z_COLS = 3 * CONV_DIM + ATTN_DIM + 6 * KV_DIM + LANES
BIAS_DMAX = 768
EXPERT_ROWS = 256
ROUTER_GROUP_LANE = 32
SCATTER_TOKENS = 256
VMEM_LIMIT = 56 * 1024 * 1024


def _cparams(*sem):
    return pltpu.CompilerParams(dimension_semantics=sem, vmem_limit_bytes=VMEM_LIMIT)


def _dot(a, b):
    return jnp.dot(a, b, preferred_element_type=F32)


def _dot_nt(a, b):
    return lax.dot_general(a, b, (((1,), (1,)), ((), ())), preferred_element_type=F32)


def _split3(x):
    hi = x.astype(BF16)
    r = x - hi.astype(F32)
    mid = r.astype(BF16)
    lo = (r - mid.astype(F32)).astype(BF16)
    return hi, mid, lo


def _rms(x, g):
    return x * lax.rsqrt(jnp.mean(x * x, axis=-1, keepdims=True) + EPS) * g


def _head_rms(x, bd, g):
    hi, mid, lo = _split3(x * x)
    ss = _dot(hi, bd) + _dot(mid, bd) + _dot(lo, bd)
    return x * lax.rsqrt(ss * (1.0 / HEAD_DIM) + EPS) * g


def _sigmoid(x):
    return 1.0 / (1.0 + jnp.exp(-x))


def _store_token_tiles(ref, x, n):
    r = x.shape[1] // LANES
    for j in range(r):
        ref[pl.ds(j, n, stride=r), :] = x[:, j * LANES:(j + 1) * LANES]


def _load_token_tiles(ref, n, r, first=0):
    return jnp.concatenate([ref[pl.ds(first + j, n, stride=r), :] for j in range(r)], axis=-1)


def _softmax_parts(s):
    m = jnp.max(s, axis=-1, keepdims=True)
    m = jnp.where(m == NEG_INF, 0.0, m)
    p = jnp.exp(s - m)
    l = jnp.sum(p, axis=-1, keepdims=True)
    return p, l


def _proj_body(sample, tm, *refs):
    if sample:
        (x_ref, nm_ref, w_ref, qn_ref, kn1_ref, kn2_ref, cw_ref, bd_ref, s0_ref, s1_ref,
         co_ref, q_ref, kc_ref, vc_ref, ks_ref, vs_ref, kw_ref, vw_ref, gt_ref, u_ref) = refs
    else:
        (x_ref, nm_ref, w_ref, qn_ref, kn1_ref, kn2_ref, cw_ref, bd_ref,
         co_ref, q_ref, kc_ref, vc_ref, ks_ref, vs_ref, kw_ref, vw_ref, gt_ref, cs_ref,
         ksb_ref, vsb_ref, kwb_ref, vwb_ref, carry_ref) = refs

    xn = _rms(x_ref[...], nm_ref[...]).astype(BF16)

    def seg(a, b):
        return _dot(xn, w_ref[:, a:b])

    c3 = 3 * CONV_DIM
    u = seg(2 * CONV_DIM, c3) * seg(0, CONV_DIM)
    bg = seg(CONV_DIM, 2 * CONV_DIM)
    row = lax.broadcasted_iota(jnp.int32, (tm, 1), 0)
    um1 = pltpu.roll(u, 1, axis=0)
    um2 = pltpu.roll(u, 2, axis=0)
    if sample:
        r = row & 3
        s0 = s0_ref[...]
        s1 = s1_ref[...]
        prev1 = jnp.where(r == 0, s1, um1)
        prev2 = jnp.where(r == 0, s0, jnp.where(r == 1, s1, um2))
        u_ref[...] = u
    else:
        @pl.when(pl.program_id(1) == 0)
        def _():
            carry_ref[...] = jnp.zeros_like(carry_ref)
        c = carry_ref[...]
        prev1 = jnp.where(row == 0, c[7:8], um1)
        prev2 = jnp.where(row == 0, c[6:7], jnp.where(row == 1, c[7:8], um2))
        carry_ref[...] = u[tm - 8:tm]
        cs_ref[0] = u[tm - 8:tm]
    cw = cw_ref[...]
    y = cw[0:1] * prev2 + cw[1:2] * prev1 + cw[2:3] * u
    co_ref[...] = (bg * y).astype(BF16)

    bd = bd_ref[...]
    q = _head_rms(seg(c3, c3 + ATTN_DIM), bd, qn_ref[...]) * (HEAD_DIM ** -0.5)
    if sample:
        q_ref[...] = q
    else:
        lane = lax.broadcasted_iota(jnp.int32, (tm, LANES), 1)
        for hd in range(N_HEADS):
            pair = q[:, (hd // 2) * LANES:(hd // 2 + 1) * LANES]
            if (hd % 2) != (hd // GROUP):
                pair = pltpu.roll(pair, HEAD_DIM, axis=1)
            keep = (lane >= HEAD_DIM) if (hd // GROUP) else (lane < HEAD_DIM)
            q_ref[0, hd] = jnp.where(keep, pair, 0.0).astype(BF16)

    k0 = c3 + ATTN_DIM
    bdk = bd[:KV_DIM, :KV_DIM]
    kc_ref[...] = seg(k0, k0 + KV_DIM)
    vc_ref[...] = seg(k0 + KV_DIM, k0 + 2 * KV_DIM)
    ks = _head_rms(seg(k0 + 2 * KV_DIM, k0 + 3 * KV_DIM), bdk, kn1_ref[...])
    vs = seg(k0 + 3 * KV_DIM, k0 + 4 * KV_DIM)
    kw = _head_rms(seg(k0 + 4 * KV_DIM, k0 + 5 * KV_DIM), bdk, kn2_ref[...])
    vw = seg(k0 + 5 * KV_DIM, k0 + 6 * KV_DIM)
    ks_ref[...] = ks
    vs_ref[...] = vs
    kw_ref[...] = kw
    vw_ref[...] = vw
    if not sample:
        ksb_ref[...] = ks.astype(BF16)
        vsb_ref[...] = vs.astype(BF16)
        kwb_ref[...] = kw.astype(BF16)
        vwb_ref[...] = vw.astype(BF16)
    gt_ref[...] = _sigmoid(seg(k0 + 6 * KV_DIM, k0 + 6 * KV_DIM + LANES))


def _project(x2d, batch, seq, tm, weights, state=None):
    n, d = x2d.shape
    sample = state is not None
    nt = seq // tm if not sample else 1
    const = lambda shape: pl.BlockSpec(shape, lambda b, t: (0,) * len(shape))
    rows = lambda w: pl.BlockSpec((tm, w), lambda b, t: (b * nt + t, 0))
    nm, w_in, qn, kn1, kn2, cw, bd = weights
    in_specs = [rows(d), const(nm.shape), const(w_in.shape), const(qn.shape), const(kn1.shape),
                const(kn2.shape), const(cw.shape), const(bd.shape)]
    args = [x2d, nm, w_in, qn, kn1, kn2, cw, bd]
    kv_f32 = [jax.ShapeDtypeStruct((n, KV_DIM), F32)] * 6
    if sample:
        in_specs += [rows(CONV_DIM), rows(CONV_DIM)]
        args += list(state)
        out_shape = ([jax.ShapeDtypeStruct((n, CONV_DIM), BF16), jax.ShapeDtypeStruct((n, ATTN_DIM), F32)]
                     + kv_f32 + [jax.ShapeDtypeStruct((n, LANES), F32), jax.ShapeDtypeStruct((n, CONV_DIM), F32)])
        out_specs = [rows(CONV_DIM), rows(ATTN_DIM)] + [rows(KV_DIM)] * 6 + [rows(LANES), rows(CONV_DIM)]
        scratch = []
        grid = (1, 1)
    else:
        out_shape = ([jax.ShapeDtypeStruct((n, CONV_DIM), BF16),
                      jax.ShapeDtypeStruct((batch, N_HEADS, seq, LANES), BF16)]
                     + kv_f32 + [jax.ShapeDtypeStruct((n, LANES), F32),
                                 jax.ShapeDtypeStruct((batch, 8, CONV_DIM), F32)]
                     + [jax.ShapeDtypeStruct((n, KV_DIM), BF16)] * 4)
        out_specs = ([rows(CONV_DIM), pl.BlockSpec((1, N_HEADS, tm, LANES), lambda b, t: (b, 0, t, 0))]
                     + [rows(KV_DIM)] * 6 + [rows(LANES), pl.BlockSpec((1, 8, CONV_DIM), lambda b, t: (b, 0, 0))]
                     + [rows(KV_DIM)] * 4)
        scratch = [pltpu.VMEM((8, CONV_DIM), F32)]
        grid = (batch, nt)
    return pl.pallas_call(
        functools.partial(_proj_body, sample, tm),
        grid=grid, in_specs=in_specs, out_specs=out_specs, out_shape=out_shape, scratch_shapes=scratch,
        compiler_params=_cparams("arbitrary", "arbitrary"),
        name="proj_sample" if sample else "proj_prompt",
    )(*args)


def _gelu_tanh(x):
    cdf = 0.5 * (1.0 + jnp.tanh(math.sqrt(2.0 / math.pi) * (x + 0.044715 * (x * x * x))))
    return x * cdf


def _compress_core(norm, nch, x, pe_ref, we_ref, w2_ref, g_ref, o_ref):
    a0 = _dot((x + pe_ref[0:1]).astype(BF16), we_ref[0])
    a1 = _dot((x + pe_ref[1:2]).astype(BF16), we_ref[1])
    hid = a0 + pltpu.roll(a1, nch - 1, axis=0)
    w2 = w2_ref[...]
    outs = []
    for h in range(N_KV_HEADS):
        act = _gelu_tanh(hid[:, h * CMP_HIDDEN:(h + 1) * CMP_HIDDEN])
        o = _dot(act.astype(BF16), w2)
        if norm:
            o = _rms(o, g_ref[...])
        outs.append(o)
    o_ref[0] = jnp.concatenate(outs, axis=-1)


def _compress_rows_body(norm, nch, x_ref, pe_ref, we_ref, w2_ref, g_ref, o_ref):
    _compress_core(norm, nch, x_ref[0], pe_ref, we_ref, w2_ref, g_ref, o_ref)


def _compress_pages_body(norm, nch, n_pages, pt_ref, *refs):
    pages = refs[:n_pages]
    pe_ref, we_ref, w2_ref, g_ref, o_ref = refs[n_pages:]
    x = jnp.concatenate([p[0] for p in pages], axis=0)
    _compress_core(norm, nch, x, pe_ref, we_ref, w2_ref, g_ref, o_ref)


def _compress_weights(pe, w1, w2):
    w1r = w1.reshape(2, CMP_STRIDE, HEAD_DIM, CMP_HIDDEN)
    eye = jnp.eye(N_KV_HEADS, dtype=w1.dtype)
    we = jnp.einsum("jrdc,hk->jrhdkc", w1r, eye).reshape(2, CMP_STRIDE * KV_DIM, N_KV_HEADS * CMP_HIDDEN)
    per = pe.reshape(2, CMP_STRIDE, 1, HEAD_DIM)
    pex = jnp.broadcast_to(per, (2, CMP_STRIDE, N_KV_HEADS, HEAD_DIM)).reshape(2, CMP_STRIDE * KV_DIM)
    return pex.astype(F32), we.astype(BF16), w2.astype(BF16)


def _compress_rows(rows3, cw, gain, norm):
    b, nch, width = rows3.shape
    pex, we, w2 = cw
    const = lambda a: pl.BlockSpec(a.shape, lambda i: (0,) * a.ndim)
    return pl.pallas_call(
        functools.partial(_compress_rows_body, norm, nch),
        grid=(b,),
        in_specs=[pl.BlockSpec((1, nch, width), lambda i: (i, 0, 0)), const(pex), const(we), const(w2), const(gain)],
        out_specs=pl.BlockSpec((1, nch, KV_DIM), lambda i: (i, 0, 0)),
        out_shape=jax.ShapeDtypeStruct((b, nch, KV_DIM), F32),
        compiler_params=_cparams("arbitrary"),
        name="compress_rows",
    )(rows3, pex, we, w2, gain)


def _compress_pages(cache3, pt_flat, n_batch, n_pages, cw, gain, norm):
    _, pch, width = cache3.shape
    nch = n_pages * pch
    pex, we, w2 = cw
    const = lambda a: pl.BlockSpec(a.shape, lambda i, pt: (0,) * a.ndim)
    page_spec = lambda j: pl.BlockSpec((1, pch, width), lambda i, pt: (pt[i * n_pages + j], 0, 0))
    grid_spec = pltpu.PrefetchScalarGridSpec(
        num_scalar_prefetch=1, grid=(n_batch,),
        in_specs=[page_spec(j) for j in range(n_pages)] + [const(pex), const(we), const(w2), const(gain)],
        out_specs=pl.BlockSpec((1, nch, KV_DIM), lambda i, pt: (i, 0, 0)))
    return pl.pallas_call(
        functools.partial(_compress_pages_body, norm, nch, n_pages),
        grid_spec=grid_spec,
        out_shape=jax.ShapeDtypeStruct((n_batch, nch, KV_DIM), F32),
        compiler_params=_cparams("arbitrary"),
        name="compress_pages",
    )(pt_flat, *([cache3] * n_pages), pex, we, w2, gain)


def _rel_bucket(dist):
    n = jnp.maximum(dist, 0)
    max_exact = NUM_BUCKETS // 2
    nf = jnp.maximum(n, 1).astype(F32)
    large = max_exact + (jnp.log(nf / max_exact) / math.log(MAX_DISTANCE / max_exact)
                         * (NUM_BUCKETS - max_exact)).astype(jnp.int32)
    large = jnp.minimum(large, NUM_BUCKETS - 1)
    return jnp.where(n < max_exact, n, large)


def _bias_by_distance(rel_bias):
    d = jnp.arange(BIAS_DMAX, dtype=jnp.int32)
    f = rel_bias.astype(F32)[_rel_bucket(d)]
    f = (f - f[BIAS_DMAX - 1:BIAS_DMAX]).T
    return jnp.concatenate([f, jnp.full((N_HEADS, 1), NEG_INF, F32)], axis=1)


def _bias_index(d, valid):
    return np.where(valid, np.clip(d, 0, BIAS_DMAX - 1), BIAS_DMAX).astype(np.int32)


def _bias_table(fext, d, valid):
    return jnp.take(fext, jnp.asarray(_bias_index(d, valid)), axis=1)


def _toeplitz(g, n_rows, k):
    h = g.shape[0]
    w = n_rows + k
    v = jnp.pad(g, ((0, 0), (0, w - g.shape[1])))
    t = jnp.tile(v, (1, n_rows + 1))[:, :n_rows * (w + 1)].reshape(h, n_rows, w + 1)[:, :, :k]
    return t[:, :, ::-1]


def _select_blocks(imp_t, srow, qpos, n_rank):
    qblk = qpos >> 6
    forced = (srow == 0) | (srow == qblk) | (srow == qblk - 1)
    valid = (srow << 6) <= qpos
    imp_t = jnp.where(valid, imp_t + jnp.where(forced, FORCE_SCORE, 0.0), NEG_INF)
    cnt = jnp.zeros(imp_t.shape, jnp.int32)
    for s in range(n_rank):
        r = imp_t[s:s + 1, :]
        beats = (r > imp_t) | ((r == imp_t) & (srow > s))
        cnt = cnt + jnp.where(beats, 1, 0)
    return jnp.where((cnt < N_SEL) & valid, 1.0, 0.0)


def _attn_prompt_body(kt, n_slc, q_ref, gt_ref, kcmp_ref, vcmp_ref, ks_ref, vs_ref, kw_ref, vw_ref,
                      ctab_ref, ntab_ref, wtab_ref, efar_ref, enear_ref, ovl_ref, o_ref):
    i = pl.program_id(1)
    qb = Q_BLOCK
    rows = GROUP * qb
    n_cmp_pad = kcmp_ref.shape[1]
    near_start = pl.multiple_of(jnp.maximum(i - 1, 0) * qb, qb)
    win_start = pl.multiple_of(jnp.maximum(i * qb - WINDOW, 0), qb)
    band = WINDOW + qb
    gates = gt_ref[...]
    kcmp = kcmp_ref[0].astype(BF16)
    vcmp = vcmp_ref[0].astype(BF16)
    lane = lax.broadcasted_iota(jnp.int32, (qb, LANES), 1)
    heads_out = []
    for h in range(N_KV_HEADS):
        qh = q_ref[0, h * GROUP:(h + 1) * GROUP].reshape(rows, LANES)

        s = _dot_nt(qh, kcmp) + ctab_ref[h * GROUP:(h + 1) * GROUP, 0].reshape(rows, n_cmp_pad)
        p, l = _softmax_parts(s)
        pn = p / jnp.maximum(l, 1e-30)
        o_c = _dot(pn.astype(BF16), vcmp)

        psum = pn[0:qb] + pn[qb:2 * qb] + pn[2 * qb:3 * qb] + pn[3 * qb:4 * qb]
        hi, mid, lo = _split3(psum)
        ovl = ovl_ref[...]
        imp_t = _dot_nt(ovl, hi) + _dot_nt(ovl, mid) + _dot_nt(ovl, lo)
        srow = lax.broadcasted_iota(jnp.int32, (n_slc, qb), 0)
        qpos_t = i * qb + lax.broadcasted_iota(jnp.int32, (n_slc, qb), 1)
        sel_t = _select_blocks(imp_t, srow, qpos_t, n_slc)
        if n_slc < LANES:
            sel_t = jnp.concatenate([sel_t, jnp.zeros((LANES - n_slc, qb), F32)], axis=0)
        sel = sel_t.T.astype(BF16)

        kk = ks_ref[pl.ds(near_start, 2 * qb), :]
        vv = vs_ref[pl.ds(near_start, 2 * qb), :]
        mexp = _dot(sel, enear_ref[0])
        mb = jnp.where(mexp > 0.5, ntab_ref[0, h * GROUP:(h + 1) * GROUP], NEG_INF)
        s = _dot_nt(qh, kk).reshape(GROUP, qb, 2 * qb) + mb
        s = s.reshape(rows, 2 * qb)
        m0 = jnp.max(s, axis=-1, keepdims=True)
        ms = jnp.where(m0 == NEG_INF, 0.0, m0)
        p = jnp.exp(s - ms)
        l0 = jnp.sum(p, axis=-1, keepdims=True)
        a0 = _dot(p.astype(BF16), vv)

        def far_tile(t, carry):
            m_old, l_old, acc = carry
            k0 = pl.multiple_of(t * kt, kt)
            kk = ks_ref[pl.ds(k0, kt), :]
            vv = vs_ref[pl.ds(k0, kt), :]
            mexp = _dot(sel, efar_ref[t])
            kpos = k0 + lax.broadcasted_iota(jnp.int32, (qb, kt), 1)
            mb = jnp.where((mexp > 0.5) & (kpos < near_start), 0.0, NEG_INF)
            s = (_dot_nt(qh, kk).reshape(GROUP, qb, kt) + mb[None]).reshape(rows, kt)
            m_new = jnp.maximum(m_old, jnp.max(s, axis=-1, keepdims=True))
            ms = jnp.where(m_new == NEG_INF, 0.0, m_new)
            alpha = jnp.exp(m_old - ms)
            p = jnp.exp(s - ms)
            l_new = alpha * l_old + jnp.sum(p, axis=-1, keepdims=True)
            acc = alpha * acc + _dot(p.astype(BF16), vv)
            return m_new, l_new, acc

        n_far = (near_start + kt - 1) // kt
        _, l_s, acc_s = lax.fori_loop(0, n_far, far_tile, (m0, l0, a0))
        o_s = acc_s / jnp.maximum(l_s, 1e-30)

        kk = kw_ref[pl.ds(win_start, band), :]
        vv = vw_ref[pl.ds(win_start, band), :]
        s = _dot_nt(qh, kk) + wtab_ref[0, h * GROUP:(h + 1) * GROUP].reshape(rows, band)
        p, l = _softmax_parts(s)
        o_w = _dot(p.astype(BF16), vv) / jnp.maximum(l, 1e-30)

        for g in range(GROUP):
            c = (h * GROUP + g) * N_BRANCH
            sl = slice(g * qb, (g + 1) * qb)
            heads_out.append(gates[:, c:c + 1] * o_c[sl] + gates[:, c + 1:c + 2] * o_s[sl]
                             + gates[:, c + 2:c + 3] * o_w[sl])

    tiles = []
    for j in range(N_HEADS // 2):
        a, b = heads_out[2 * j], heads_out[2 * j + 1]
        if (2 * j) // GROUP == 1:
            a = pltpu.roll(a, HEAD_DIM, axis=1)
        else:
            b = pltpu.roll(b, HEAD_DIM, axis=1)
        tiles.append(jnp.where(lane < HEAD_DIM, a, b))
    o_ref[...] = jnp.concatenate(tiles, axis=-1).astype(BF16)


def _attn_prompt(qp, gates, kcmp, vcmp, ksb, vsb, kwb, vwb, fext, batch, seq):
    qb = Q_BLOCK
    nqb = seq // qb
    n_slc = seq // SLC_BLOCK
    n_cmp_pad = kcmp.shape[1]
    n_cmp = n_cmp_pad - 1
    kt = min(512, seq)
    n_kt = seq // kt
    band = WINDOW + qb
    iq = np.arange(qb)

    per_qb = qb // CMP_STRIDE
    half = 2 * per_qb
    m = np.arange(-half, half)
    d = iq[:, None] - (m[None, :] * CMP_STRIDE + CMP_BLOCK - 1)
    assert d[:, 0].min() >= MAX_DISTANCE and d[:, -1].max() < 0
    gwide = jnp.concatenate([jnp.zeros((N_HEADS, qb, n_cmp_pad - half), F32), _bias_table(fext, d, d >= 0),
                             jnp.full((N_HEADS, qb, n_cmp_pad - half), NEG_INF, F32)], axis=2)
    ctab = jnp.stack([gwide[:, :, n_cmp_pad - per_qb * i:2 * n_cmp_pad - per_qb * i] for i in range(nqb)], axis=1)
    assert (n_cmp_pad - 1) * CMP_STRIDE + CMP_BLOCK - 1 >= seq and n_cmp == n_cmp_pad - 1

    def toeplitz_variants(n_var, k, valid):
        j = np.arange(qb + k - 1)
        tabs = []
        for v in range(n_var):
            dj = qb * v + j - (k - 1)
            tabs.append(_toeplitz(_bias_table(fext, dj, valid(dj)), qb, k))
        return jnp.stack(tabs, axis=0)

    ntab = toeplitz_variants(2, 2 * qb, lambda dj: dj >= 0)
    nv = WINDOW // qb + 1
    wtab = toeplitz_variants(nv, band, lambda dj: (dj >= 0) & (dj < WINDOW))

    srow = np.arange(LANES)[None, :, None]
    efar = ((np.arange(n_kt)[:, None, None] * kt + np.arange(kt)[None, None, :]) // SLC_BLOCK == srow)
    nstart = np.maximum(np.arange(nqb) - 1, 0) * qb
    enear = ((nstart[:, None, None] + np.arange(2 * qb)[None, None, :]) // SLC_BLOCK == srow)
    c_start = np.arange(n_cmp_pad) * CMP_STRIDE
    s_start = np.arange(n_slc) * SLC_BLOCK
    ovl = ((c_start[None, :] < s_start[:, None] + SLC_BLOCK) & (c_start[None, :] + CMP_BLOCK > s_start[:, None])
           & (np.arange(n_cmp_pad) < n_cmp)[None, :])
    efar = jnp.asarray(efar, BF16)
    enear = jnp.asarray(enear, BF16)
    ovl = jnp.asarray(ovl, BF16)

    per_batch = lambda a: pl.BlockSpec((seq, a.shape[1]), lambda b, i: (b, 0))
    const = lambda a: pl.BlockSpec(a.shape, lambda b, i: (0,) * a.ndim)
    return pl.pallas_call(
        functools.partial(_attn_prompt_body, kt, n_slc),
        grid=(batch, nqb),
        in_specs=[pl.BlockSpec((1, N_HEADS, qb, LANES), lambda b, i: (b, 0, i, 0)),
                  pl.BlockSpec((qb, LANES), lambda b, i: (b * nqb + i, 0)),
                  pl.BlockSpec((1, n_cmp_pad, KV_DIM), lambda b, i: (b, 0, 0)),
                  pl.BlockSpec((1, n_cmp_pad, KV_DIM), lambda b, i: (b, 0, 0)),
                  per_batch(ksb), per_batch(vsb), per_batch(kwb), per_batch(vwb),
                  pl.BlockSpec((N_HEADS, 1, qb, n_cmp_pad), lambda b, i: (0, i, 0, 0)),
                  pl.BlockSpec((1, N_HEADS, qb, 2 * qb), lambda b, i: (jnp.minimum(i, 1), 0, 0, 0)),
                  pl.BlockSpec((1, N_HEADS, qb, band), lambda b, i: (jnp.minimum(i, nv - 1), 0, 0, 0)),
                  const(efar),
                  pl.BlockSpec((1, LANES, 2 * qb), lambda b, i: (i, 0, 0)),
                  const(ovl)],
        out_specs=pl.BlockSpec((qb, ATTN_DIM), lambda b, i: (b * nqb + i, 0)),
        out_shape=jax.ShapeDtypeStruct((batch * seq, ATTN_DIM), BF16),
        compiler_params=_cparams("arbitrary", "arbitrary"),
        name="attn_prompt",
    )(qp, gates, kcmp, vcmp, ksb, vsb, kwb, vwb, ctab, ntab, wtab, efar, enear, ovl)


def _attn_sample_body(n_pages, ts, past_len, pt_ref, *refs):
    kpages = refs[:n_pages]
    vpages = refs[n_pages:2 * n_pages]
    (q_ref, gt_ref, kcmp_ref, vcmp_ref, ksn_ref, vsn_ref, kwc_ref, vwc_ref, kwn_ref, vwn_ref,
     ctab_ref, stab_ref, sntab_ref, wtab_ref, wntab_ref, eexp_ref, ovl_ref, o_ref) = refs[2 * n_pages:]
    rows = GROUP * N_KV_HEADS * ts
    rq = N_KV_HEADS * ts
    q = q_ref[0]
    gates = gt_ref[0]

    s = _dot_nt(q, kcmp_ref[0].astype(BF16)) + ctab_ref[...]
    p, l = _softmax_parts(s)
    pn = p / jnp.maximum(l, 1e-30)
    o_c = _dot(pn.astype(BF16), vcmp_ref[0].astype(BF16))

    psum = pn[0:rq]
    for g in range(1, GROUP):
        psum = psum + pn[g * rq:(g + 1) * rq]
    hi, mid, lo = _split3(psum)
    ovl = ovl_ref[...]
    imp = _dot(hi, ovl) + _dot(mid, ovl) + _dot(lo, ovl)
    n_slc = -(-(past_len + ts) // SLC_BLOCK)
    blk = lax.broadcasted_iota(jnp.int32, (rq, LANES), 1)
    qpos = past_len + (lax.broadcasted_iota(jnp.int32, (rq, LANES), 0) & (ts - 1))
    qblk = qpos >> 6
    forced = (blk == 0) | (blk == qblk) | (blk == qblk - 1)
    valid = ((blk << 6) <= qpos) & (blk < n_slc)
    imp = jnp.where(valid, imp + jnp.where(forced, FORCE_SCORE, 0.0), NEG_INF)
    cnt = jnp.zeros((rq, LANES), jnp.int32)
    for sidx in range(n_slc):
        r = imp[:, sidx:sidx + 1]
        beats = (r > imp) | ((r == imp) & (blk > sidx))
        cnt = cnt + jnp.where(beats, 1, 0)
    sel = jnp.where((cnt < N_SEL) & valid, 1.0, 0.0)
    sel = jnp.concatenate([sel] * GROUP, axis=0).astype(BF16)

    kc = jnp.concatenate([p_[0] for p_ in kpages], axis=0).astype(BF16)
    vc = jnp.concatenate([p_[0] for p_ in vpages], axis=0).astype(BF16)
    mexp = _dot(sel, eexp_ref[...])
    s1 = jnp.where(mexp > 0.5, _dot_nt(q, kc) + stab_ref[...], NEG_INF)
    last = sel[:, n_slc - 1:n_slc].astype(F32)
    s2 = jnp.where(last > 0.5, _dot_nt(q, ksn_ref[0].astype(BF16)) + sntab_ref[...], NEG_INF)
    m = jnp.maximum(jnp.max(s1, axis=-1, keepdims=True), jnp.max(s2, axis=-1, keepdims=True))
    m = jnp.where(m == NEG_INF, 0.0, m)
    p1 = jnp.exp(s1 - m)
    p2 = jnp.exp(s2 - m)
    l = jnp.sum(p1, axis=-1, keepdims=True) + jnp.sum(p2, axis=-1, keepdims=True)
    o_s = (_dot(p1.astype(BF16), vc) + _dot(p2.astype(BF16), vsn_ref[0].astype(BF16))) / jnp.maximum(l, 1e-30)

    s1 = _dot_nt(q, kwc_ref[0].astype(BF16)) + wtab_ref[...]
    s2 = _dot_nt(q, kwn_ref[0].astype(BF16)) + wntab_ref[...]
    m = jnp.maximum(jnp.max(s1, axis=-1, keepdims=True), jnp.max(s2, axis=-1, keepdims=True))
    m = jnp.where(m == NEG_INF, 0.0, m)
    p1 = jnp.exp(s1 - m)
    p2 = jnp.exp(s2 - m)
    l = jnp.sum(p1, axis=-1, keepdims=True) + jnp.sum(p2, axis=-1, keepdims=True)
    o_w = (_dot(p1.astype(BF16), vwc_ref[0].astype(BF16))
           + _dot(p2.astype(BF16), vwn_ref[0].astype(BF16))) / jnp.maximum(l, 1e-30)

    o_ref[0] = gates[:, 0:1] * o_c + gates[:, 1:2] * o_s + gates[:, 2:3] * o_w


def _attn_sample(q_s, gates_s, kcmp, vcmp, ks_new, vs_new, kw_new, vw_new, cache_ks, cache_vs,
                 cache_kw, cache_vw, pt_flat, fext, n_batch, ts, n_pages, page):
    past_len = n_pages * page
    w_buf = cache_kw.shape[1]
    rows = GROUP * N_KV_HEADS * ts
    n_new = 8
    n_cmp_pad = kcmp.shape[1]
    n_cmp = n_cmp_pad - 1
    n_slc = -(-(past_len + ts) // SLC_BLOCK)

    q5 = q_s.reshape(n_batch, ts, N_KV_HEADS, GROUP, HEAD_DIM).transpose(0, 3, 2, 1, 4)
    eye = jnp.eye(N_KV_HEADS, dtype=q_s.dtype)
    qr = jnp.einsum("bghtd,hk->bghtkd", q5, eye).reshape(n_batch, rows, LANES).astype(BF16)
    g5 = gates_s[:, :N_HEADS * N_BRANCH].reshape(n_batch, ts, N_KV_HEADS, GROUP, N_BRANCH).transpose(0, 3, 2, 1, 4)
    gr = jnp.pad(g5.reshape(n_batch, rows, N_BRANCH), ((0, 0), (0, 0), (0, LANES - N_BRANCH)))
    pad_new = lambda a: jnp.pad(a.reshape(n_batch, ts, KV_DIM), ((0, 0), (0, n_new - ts), (0, 0)))
    ks_new, vs_new, kw_new, vw_new = map(pad_new, (ks_new, vs_new, kw_new, vw_new))

    g_i, h_i, t_i = np.meshgrid(np.arange(GROUP), np.arange(N_KV_HEADS), np.arange(ts), indexing="ij")
    head = (h_i * GROUP + g_i).reshape(rows)
    tq = t_i.reshape(rows)
    pos_q = past_len + tq

    f_rows = fext[jnp.asarray(head)]

    def table(d, valid):
        return jnp.take_along_axis(f_rows, jnp.asarray(_bias_index(d, valid)), axis=1)

    nn = np.arange(n_cmp_pad)
    d = pos_q[:, None] - (nn[None, :] * CMP_STRIDE + CMP_BLOCK - 1)
    ctab = table(d, (d >= 0) & (nn < n_cmp)[None, :])
    near = np.arange(past_len - MAX_DISTANCE, past_len)
    d = pos_q[:, None] - near[None, :]
    assert past_len >= MAX_DISTANCE and d.min() >= 0
    stab = jnp.concatenate([jnp.zeros((rows, past_len - MAX_DISTANCE), F32), table(d, d >= 0)], axis=1)
    jn = np.arange(n_new)
    d = tq[:, None] - jn[None, :]
    sntab = table(d, (d >= 0) & (jn < ts)[None, :])
    pos_w = past_len - w_buf + np.arange(w_buf)
    d = pos_q[:, None] - pos_w[None, :]
    wtab = table(d, (d >= 0) & (d < WINDOW) & (pos_w >= 0)[None, :])
    wntab = table(tq[:, None] - jn[None, :], (tq[:, None] >= jn[None, :]) & (jn < ts)[None, :])

    eexp = jnp.asarray(np.arange(past_len)[None, :] // SLC_BLOCK == np.arange(LANES)[:, None], BF16)
    c_start = nn * CMP_STRIDE
    s_start = np.arange(LANES) * SLC_BLOCK
    ovl = jnp.asarray((c_start[:, None] < s_start[None, :] + SLC_BLOCK) & (c_start[:, None] + CMP_BLOCK > s_start[None, :])
                      & (nn < n_cmp)[:, None] & (np.arange(LANES) < n_slc)[None, :], BF16)

    const = lambda a: pl.BlockSpec(a.shape, lambda b, pt: (0,) * a.ndim)
    per_b = lambda a: pl.BlockSpec((1,) + a.shape[1:], lambda b, pt: (b,) + (0,) * (a.ndim - 1))
    page_spec = lambda j: pl.BlockSpec((1, page, KV_DIM), lambda b, pt: (pt[b * n_pages + j], 0, 0))
    small = [qr, gr, kcmp, vcmp, ks_new, vs_new, cache_kw, cache_vw, kw_new, vw_new]
    consts = [ctab, stab, sntab, wtab, wntab, eexp, ovl]
    grid_spec = pltpu.PrefetchScalarGridSpec(
        num_scalar_prefetch=1, grid=(n_batch,),
        in_specs=[page_spec(j) for j in range(n_pages)] * 2 + [per_b(a) for a in small] + [const(a) for a in consts],
        out_specs=pl.BlockSpec((1, rows, LANES), lambda b, pt: (b, 0, 0)))
    o = pl.pallas_call(
        functools.partial(_attn_sample_body, n_pages, ts, past_len),
        grid_spec=grid_spec,
        out_shape=jax.ShapeDtypeStruct((n_batch, rows, LANES), F32),
        compiler_params=_cparams("arbitrary"),
        name="attn_sample",
    )(pt_flat, *([cache_ks] * n_pages), *([cache_vs] * n_pages), *small, *consts)
    o6 = o.reshape(n_batch, GROUP, N_KV_HEADS, ts, N_KV_HEADS, HEAD_DIM)
    o5 = jnp.stack([o6[:, :, h, :, h] for h in range(N_KV_HEADS)], axis=2)
    return o5.transpose(0, 3, 2, 1, 4).reshape(n_batch * ts, ATTN_DIM).astype(BF16)


def _post1_body(tm, x_ref, co_ref, at_ref, wo_ref, nf_ref, wr_ref, br_ref, tri_ref, run0_ref,
                h_ref, hn_ref, rt_ref, cnt_ref, run_ref):
    @pl.when(pl.program_id(0) == 0)
    def _():
        run_ref[...] = run0_ref[...]

    h = x_ref[...] + _dot(co_ref[...], wo_ref[0:CONV_DIM]) + _dot(at_ref[...], wo_ref[CONV_DIM:CONV_DIM + ATTN_DIM])
    hn = _rms(h, nf_ref[...])
    h_ref[...] = h
    _store_token_tiles(hn_ref, hn, tm)

    hi = hn.astype(BF16)
    lo = (hn - hi.astype(F32)).astype(BF16)
    wr = wr_ref[...]
    whi = wr.astype(BF16)
    wlo = (wr - whi.astype(F32)).astype(BF16)
    logits = _dot(hi, whi) + _dot(lo, whi) + _dot(hi, wlo) + br_ref[...]

    lane_i = lax.broadcasted_iota(jnp.int32, (tm, LANES), 1)
    lane = lane_i.astype(F32)
    big = float(LANES)
    gmask = (lane_i >= ROUTER_GROUP_LANE) & (lane_i < ROUTER_GROUP_LANE + N_GROUPS)
    lg = jnp.where(gmask, logits, NEG_INF)
    eg = jnp.exp(lg - jnp.max(lg, axis=-1, keepdims=True))
    pg = eg / jnp.sum(eg, axis=-1, keepdims=True)
    gw = jnp.max(pg, axis=-1, keepdims=True)
    grp = jnp.min(jnp.where(gmask & (pg == gw), lane, big), axis=-1, keepdims=True) - ROUTER_GROUP_LANE

    group_of_lane = (lane_i >> 3).astype(F32)
    emask = (lane_i < N_EXPERTS) & (group_of_lane == grp)
    le = jnp.where(emask, logits, NEG_INF)
    ee = jnp.exp(le - jnp.max(le, axis=-1, keepdims=True))
    pe = jnp.where(emask, ee / jnp.sum(ee, axis=-1, keepdims=True), -1.0)
    v1 = jnp.max(pe, axis=-1, keepdims=True)
    i1 = jnp.min(jnp.where(pe == v1, lane, big), axis=-1, keepdims=True)
    pe2 = jnp.where(lane == i1, -1.0, pe)
    v2 = jnp.max(pe2, axis=-1, keepdims=True)
    i2 = jnp.min(jnp.where(pe2 == v2, lane, big), axis=-1, keepdims=True)
    tot = v1 + v2
    w1 = v1 / tot * gw
    w2 = v2 / tot * gw

    oh1 = jnp.where(lane == i1, 1.0, 0.0)
    oh2 = jnp.where(lane == i2, 1.0, 0.0)
    both = oh1 + oh2
    before = _dot(tri_ref[...], both.astype(BF16)) + run_ref[0:1]
    r1 = jnp.sum(oh1 * before, axis=-1, keepdims=True)
    r2 = jnp.sum(oh2 * before, axis=-1, keepdims=True)
    run = run_ref[0:1] + jnp.sum(both, axis=0, keepdims=True)
    run_ref[...] = jnp.broadcast_to(run, run_ref.shape)
    cnt_ref[...] = jnp.broadcast_to(run, cnt_ref.shape)

    rt = jnp.where(lane_i == 0, i1, 0.0)
    rt = jnp.where(lane_i == 1, i2, rt)
    rt = jnp.where(lane_i == 2, r1, rt)
    rt = jnp.where(lane_i == 3, r2, rt)
    rt = jnp.where(lane_i == 4, w1, rt)
    rt = jnp.where(lane_i == 5, w2, rt)
    rt_ref[...] = rt


def _post1(x2d, co, at, wo, nf, wr, br, run0, tm):
    n, d = x2d.shape
    tri = jnp.asarray(np.tril(np.ones((tm, tm), np.float32), -1), BF16)
    rows = lambda w: pl.BlockSpec((tm, w), lambda i: (i, 0))
    const = lambda a: pl.BlockSpec(a.shape, lambda i: (0,) * a.ndim)
    return pl.pallas_call(
        functools.partial(_post1_body, tm),
        grid=(n // tm,),
        in_specs=[rows(d), rows(CONV_DIM), rows(ATTN_DIM), const(wo), const(nf), const(wr), const(br),
                  const(tri), const(run0)],
        out_specs=[rows(d), pl.BlockSpec((tm * TOKEN_TILE_ROWS, LANES), lambda i: (i, 0)), rows(LANES),
                   pl.BlockSpec((8, LANES), lambda i: (0, 0))],
        out_shape=[jax.ShapeDtypeStruct((n, d), F32), jax.ShapeDtypeStruct((n * TOKEN_TILE_ROWS, LANES), F32),
                   jax.ShapeDtypeStruct((n, LANES), F32), jax.ShapeDtypeStruct((8, LANES), F32)],
        scratch_shapes=[pltpu.VMEM((8, LANES), F32)],
        compiler_params=_cparams("arbitrary"),
        name="post1",
    )(x2d, co, at, wo, nf, wr, br, tri, run0)


def _token_copy(src_ref, dst_ref, s, d, sem):
    r = TOKEN_TILE_ROWS
    return pltpu.make_async_copy(src_ref.at[pl.ds(pl.multiple_of(s * r, r), r)],
                                 dst_ref.at[pl.ds(pl.multiple_of(d * r, r), r)], sem)


def _scatter_rows_body(ts, dest_ref, src_ref, init_ref, out_ref, sem):
    del init_ref
    base = pl.program_id(0) * (2 * ts)

    def issue(j, _):
        _token_copy(src_ref, out_ref, j >> 1, dest_ref[base + j], sem).start()
        return 0

    def retire(j, _):
        _token_copy(src_ref, out_ref, 0, 0, sem).wait()
        return 0

    lax.fori_loop(0, 2 * ts, issue, 0)
    lax.fori_loop(0, 2 * ts, retire, 0)


def _scatter_rows(dest, src, slots):
    n_tok = dest.shape[0] // 2
    ts = min(SCATTER_TOKENS, n_tok)
    assert n_tok % ts == 0
    any_spec = pl.BlockSpec(memory_space=pl.ANY)
    return pl.pallas_call(
        functools.partial(_scatter_rows_body, ts),
        grid_spec=pltpu.PrefetchScalarGridSpec(
            num_scalar_prefetch=1, grid=(n_tok // ts,),
            in_specs=[pl.BlockSpec((ts * TOKEN_TILE_ROWS, LANES), lambda i, dest: (i, 0)), any_spec],
            out_specs=any_spec, scratch_shapes=[pltpu.SemaphoreType.DMA(())]),
        out_shape=jax.ShapeDtypeStruct(slots.shape, slots.dtype),
        input_output_aliases={2: 0},
        compiler_params=pltpu.CompilerParams(dimension_semantics=("arbitrary",)),
        name="scatter_rows",
    )(dest, src, slots)


def _experts_body(be_ref, nu_ref, x_ref, wg_ref, wu_ref, wd_ref, o_ref, wg_s, wu_s, wd_s):
    i = pl.program_id(0)

    @pl.when(i < nu_ref[0])
    def _():
        prev = be_ref[jnp.maximum(i - 1, 0)]

        @pl.when((i == 0) | (be_ref[i] != prev))
        def _():
            wg_s[...] = wg_ref[0].astype(BF16)
            wu_s[...] = wu_ref[0].astype(BF16)
            wd_s[...] = wd_ref[0].astype(BF16)

        x = _load_token_tiles(x_ref, EXPERT_ROWS, TOKEN_TILE_ROWS).astype(BF16)
        g = _dot(x, wg_s[...])
        u = _dot(x, wu_s[...])
        a = g * _sigmoid(g) * u
        _store_token_tiles(o_ref, _dot(a.astype(BF16), wd_s[...]), EXPERT_ROWS)

    @pl.when(i >= nu_ref[0])
    def _():
        o_ref[...] = jnp.zeros_like(o_ref)


def _experts(blk_expert, n_used, xs, wg, wu, wd):
    blk_rows = EXPERT_ROWS * TOKEN_TILE_ROWS
    n_blk = xs.shape[0] // blk_rows
    _, d, de = wg.shape
    xmap = lambda i, be, nu: (jnp.minimum(i, jnp.maximum(nu[0] - 1, 0)), 0)
    wmap = lambda i, be, nu: (be[jnp.minimum(i, jnp.maximum(nu[0] - 1, 0))], 0, 0)
    grid_spec = pltpu.PrefetchScalarGridSpec(
        num_scalar_prefetch=2, grid=(n_blk,),
        in_specs=[pl.BlockSpec((blk_rows, LANES), xmap), pl.BlockSpec((1, d, de), wmap),
                  pl.BlockSpec((1, d, de), wmap), pl.BlockSpec((1, de, d), wmap)],
        out_specs=pl.BlockSpec((blk_rows, LANES), lambda i, be, nu: (i, 0)),
        scratch_shapes=[pltpu.VMEM((d, de), BF16), pltpu.VMEM((d, de), BF16), pltpu.VMEM((de, d), BF16)])
    return pl.pallas_call(
        _experts_body, grid_spec=grid_spec,
        out_shape=jax.ShapeDtypeStruct(xs.shape, F32),
        compiler_params=_cparams("arbitrary"),
        name="experts",
    )(blk_expert, n_used, xs, wg, wu, wd)


def _post2_body(tm, dest_ref, h_ref, rt_ref, p_ref, yb_ref, wple_ref, wpg_ref, bpg_ref, np_ref, o_ref, buf, sem):
    i = pl.program_id(0)
    n = pl.num_programs(0)

    def fetch(step, slot):
        base = step * (2 * tm)

        def issue(j, _):
            row = (j & 1) * tm + (j >> 1)
            _token_copy(yb_ref, buf.at[slot], dest_ref[base + j], row, sem.at[slot]).start()
            return 0

        lax.fori_loop(0, 2 * tm, issue, 0)

    @pl.when(i == 0)
    def _():
        fetch(0, 0)

    @pl.when(i + 1 < n)
    def _():
        fetch(i + 1, (i + 1) & 1)

    slot = i & 1

    def retire(j, _):
        _token_copy(yb_ref, buf.at[slot], 0, 0, sem.at[slot]).wait()
        return 0

    lax.fori_loop(0, 2 * tm, retire, 0)
    rt = rt_ref[...]
    y0 = _load_token_tiles(buf.at[slot], tm, TOKEN_TILE_ROWS)
    y1 = _load_token_tiles(buf.at[slot], tm, TOKEN_TILE_ROWS, first=tm * TOKEN_TILE_ROWS)
    h = h_ref[...] + (y0 * rt[:, 4:5] + y1 * rt[:, 5:6])
    gate = _sigmoid(_dot(_rms(h, np_ref[...]).astype(BF16), wpg_ref[...]) + bpg_ref[...])
    o_ref[...] = h + gate * _dot(p_ref[...].astype(BF16), wple_ref[...])


def _post2(dest, h, rt, p2d, yb, wple, wpg, bpg, npl, tm):
    n, d = h.shape
    rows = lambda w: pl.BlockSpec((tm, w), lambda i, dest: (i, 0))
    const = lambda a: pl.BlockSpec(a.shape, lambda i, dest: (0,) * a.ndim)
    grid_spec = pltpu.PrefetchScalarGridSpec(
        num_scalar_prefetch=1, grid=(n // tm,),
        in_specs=[rows(d), rows(LANES), rows(p2d.shape[1]), pl.BlockSpec(memory_space=pl.ANY), const(wple),
                  const(wpg), const(bpg), const(npl)],
        out_specs=rows(d),
        scratch_shapes=[pltpu.VMEM((2, 2 * tm * TOKEN_TILE_ROWS, LANES), F32), pltpu.SemaphoreType.DMA((2,))])
    return pl.pallas_call(
        functools.partial(_post2_body, tm),
        grid_spec=grid_spec,
        out_shape=jax.ShapeDtypeStruct((n, d), F32),
        compiler_params=_cparams("arbitrary"),
        name="post2",
    )(dest, h, rt, p2d, yb, wple, wpg, bpg, npl)


def _row_tile(n, cap=512):
    t = min(cap, n)
    assert n % t == 0 and t % 8 == 0
    return t


def kernel(x_prompt, x_sample, p_prompt, p_sample, cache_k_cmp, cache_v_cmp, cache_k_slc, cache_v_slc, cache_k_win, cache_v_win, state_conv, page_table, w_in, w_out, conv_w, norm_mix, norm_ffn, norm_ple, q_norm, k_norm, cmp_pe_k, cmp_w1_k, cmp_w2_k, cmp_pe_v, cmp_w1_v, cmp_w2_v, rel_bias, w_router_group, b_router_group, w_router_expert, b_router_expert, w_exp_gate, w_exp_up, w_exp_down, w_ple, w_ple_gate, b_ple_gate):
    assert w_in.shape[0] == 1, "single-layer step"
    bp, t, d = x_prompt.shape
    bs, ts, _ = x_sample.shape
    n_pages = page_table.shape[1]
    page = cache_k_cmp.shape[2]
    past_len = n_pages * page
    w_buf = cache_k_win.shape[2]
    n_phys = cache_k_cmp.shape[1]
    assert t % Q_BLOCK == 0 and t >= WINDOW + Q_BLOCK and page % CMP_STRIDE == 0 and ts == 4 and d == D_MODEL
    assert past_len % SLC_BLOCK == 0
    np_rows, ns_rows = bp * t, bs * ts

    row = lambda v: v.reshape(1, -1).astype(F32)
    w_in_b = jnp.pad(w_in[0], ((0, 0), (0, Z_COLS - w_in.shape[2]))).astype(BF16)
    qn = row(jnp.tile(q_norm[0], N_HEADS))
    kn1 = row(jnp.tile(k_norm[0, 1], N_KV_HEADS))
    kn2 = row(jnp.tile(k_norm[0, 2], N_KV_HEADS))
    bd = jnp.asarray(np.kron(np.eye(N_HEADS), np.ones((HEAD_DIM, HEAD_DIM))), BF16)
    pw = (row(norm_mix[0]), w_in_b, qn, kn1, kn2, conv_w[0].astype(F32), bd)
    cw_k = _compress_weights(cmp_pe_k[0], cmp_w1_k[0], cmp_w2_k[0])
    cw_v = _compress_weights(cmp_pe_v[0], cmp_w1_v[0], cmp_w2_v[0])
    kn0 = row(k_norm[0, 0])
    fext = _bias_by_distance(rel_bias)
    pt_flat = page_table.reshape(-1).astype(jnp.int32)
    wr = jnp.zeros((d, LANES), F32).at[:, :N_EXPERTS].set(w_router_expert[0])
    wr = wr.at[:, ROUTER_GROUP_LANE:ROUTER_GROUP_LANE + N_GROUPS].set(w_router_group[0])
    br = jnp.zeros((1, LANES), F32).at[0, :N_EXPERTS].set(b_router_expert[0])
    br = br.at[0, ROUTER_GROUP_LANE:ROUTER_GROUP_LANE + N_GROUPS].set(b_router_group[0])
    wo_b = w_out[0].astype(BF16)
    wple_b = w_ple[0].astype(BF16)
    wpg_b = w_ple_gate[0].astype(BF16)

    tm_p = _row_tile(t)
    (co_p, q_p, kc_p, vc_p, ks_p, vs_p, kw_p, vw_p, gt_p, cs_p, ksb, vsb, kwb, vwb) = _project(
        x_prompt.reshape(np_rows, d), bp, t, tm_p, pw)
    chunk_w = CMP_STRIDE * KV_DIM
    kcmp_p = _compress_rows(kc_p.reshape(bp, t // CMP_STRIDE, chunk_w), cw_k, kn0, True)
    vcmp_p = _compress_rows(vc_p.reshape(bp, t // CMP_STRIDE, chunk_w), cw_v, kn0, False)
    at_p = _attn_prompt(q_p, gt_p, kcmp_p, vcmp_p, ksb, vsb, kwb, vwb, fext, bp, t)

    st = state_conv[0].astype(F32)
    s0 = jnp.repeat(st[:, 0], ts, axis=0)
    s1 = jnp.repeat(st[:, 1], ts, axis=0)
    (co_s, q_s, kc_s, vc_s, ks_s, vs_s, kw_s, vw_s, gt_s, u_s) = _project(
        x_sample.reshape(ns_rows, d), bs, ts, ns_rows, pw, state=(s0, s1))
    pch = page // CMP_STRIDE
    kcmp_s = _compress_pages(cache_k_cmp[0].reshape(n_phys, pch, chunk_w), pt_flat, bs, n_pages, cw_k, kn0, True)
    vcmp_s = _compress_pages(cache_v_cmp[0].reshape(n_phys, pch, chunk_w), pt_flat, bs, n_pages, cw_v, kn0, False)
    kwc = cache_k_win[0].reshape(bs, w_buf, KV_DIM)
    vwc = cache_v_win[0].reshape(bs, w_buf, KV_DIM)
    at_s = _attn_sample(q_s, gt_s, kcmp_s, vcmp_s, ks_s, vs_s, kw_s, vw_s,
                        cache_k_slc[0].reshape(n_phys, page, KV_DIM), cache_v_slc[0].reshape(n_phys, page, KV_DIM),
                        kwc, vwc, pt_flat, fext, bs, ts, n_pages, page)

    tp1 = _row_tile(np_rows)
    ts1 = _row_tile(ns_rows)
    nf = row(norm_ffn[0])
    h_p, hn_p, rt_p, cnt_p = _post1(x_prompt.reshape(np_rows, d), co_p, at_p, wo_b, nf, wr, br,
                                    jnp.zeros((8, LANES), F32), tp1)
    h_s, hn_s, rt_s, cnt_s = _post1(x_sample.reshape(ns_rows, d), co_s, at_s, wo_b, nf, wr, br, cnt_p, ts1)

    counts = cnt_s[0, :N_EXPERTS].astype(jnp.int32)
    padded = (counts + EXPERT_ROWS - 1) // EXPERT_ROWS * EXPERT_ROWS
    pad_end = jnp.cumsum(padded)
    pad_start = pad_end - padded
    n_assign = 2 * (np_rows + ns_rows)
    n_blk = (n_assign + N_EXPERTS * (EXPERT_ROWS - 1) + EXPERT_ROWS - 1) // EXPERT_ROWS
    blk_first = jnp.arange(n_blk, dtype=jnp.int32) * EXPERT_ROWS
    blk_expert = jnp.minimum(jnp.sum((pad_end[None, :] <= blk_first[:, None]).astype(jnp.int32), axis=1),
                             N_EXPERTS - 1)
    n_used = (pad_end[-1:] // EXPERT_ROWS).astype(jnp.int32)

    def dest_of(rt):
        e = rt[:, 0:2].astype(jnp.int32).reshape(-1)
        return pad_start[e] + rt[:, 2:4].astype(jnp.int32).reshape(-1)

    dest_p = dest_of(rt_p)
    dest_s = dest_of(rt_s)

    xs = jnp.zeros((n_blk * EXPERT_ROWS * TOKEN_TILE_ROWS, LANES), F32)
    xs = _scatter_rows(dest_p, hn_p, xs)
    xs = _scatter_rows(dest_s, hn_s, xs)
    yb = _experts(blk_expert, n_used, xs, w_exp_gate[0], w_exp_up[0], w_exp_down[0])

    bpg = row(b_ple_gate[0])
    npl = row(norm_ple[0])
    y_p = _post2(dest_p, h_p, rt_p, p_prompt[0].reshape(np_rows, -1), yb, wple_b, wpg_b, bpg, npl, tp1)
    y_s = _post2(dest_s, h_s, rt_s, p_sample[0].reshape(ns_rows, -1), yb, wple_b, wpg_b, bpg, npl, ts1)

    kv5 = lambda a, b, s: a.reshape(1, b, s, N_KV_HEADS, HEAD_DIM)
    wp = min(WINDOW, t)
    win_p = lambda a: kv5(a, bp, t)[:, :, t - wp:]
    win_s = lambda c, new: jnp.concatenate([c[0], new.reshape(bs, ts, N_KV_HEADS, HEAD_DIM)], axis=1)[None, :, ts:]
    conv_p = cs_p[:, 8 - (CONV_K - 1):][None]
    conv_s = u_s.reshape(bs, ts, CONV_DIM)[:, ts - (CONV_K - 1):][None]
    return (y_p.reshape(bp, t, d), y_s.reshape(bs, ts, d),
            kv5(kc_p, bp, t), kv5(vc_p, bp, t), kv5(ks_p, bp, t), kv5(vs_p, bp, t), win_p(kw_p), win_p(vw_p), conv_p,
            kv5(kc_s, bs, ts), kv5(vc_s, bs, ts), kv5(ks_s, bs, ts), kv5(vs_s, bs, ts),
            win_s(cache_k_win, kw_s), win_s(cache_v_win, vw_s), conv_s)
```

```python
import functools
import math

import numpy as np
import jax
import jax.numpy as jnp
from jax import lax
from jax.experimental import pallas as pl
from jax.experimental.pallas import tpu as pltpu

F32 = jnp.float32
BF16 = jnp.bfloat16
NEG_INF = float("-inf")

HEAD_DIM = 64
N_HEADS = 8
N_KV_HEADS = 2
GROUP = N_HEADS // N_KV_HEADS
CONV_DIM = 512
ATTN_DIM = 512
KV_DIM = N_KV_HEADS * HEAD_DIM
N_BRANCH = 3
CONV_K = 3
CMP_BLOCK = 32
CMP_STRIDE = 16
CMP_HIDDEN = 256
SLC_BLOCK = 64
N_SEL = 16
WINDOW = 512
Q_BLOCK = 128
FORCE_SCORE = 1e4
NUM_BUCKETS = 32
MAX_DISTANCE = 128
N_GROUPS = 4
EXPERTS_PER_GROUP = 8
N_EXPERTS = N_GROUPS * EXPERTS_PER_GROUP
D_EXPERT = 512
EPS = 1e-6

D_MODEL = 1024
LANES = 128
TOKEN_TILE_ROWS = D_MODEL // LANES
Z_COLS = 3 * CONV_DIM + ATTN_DIM + 6 * KV_DIM + LANES
BIAS_DMAX = 768
EXPERT_ROWS = 256
ROUTER_GROUP_LANE = 32
SCATTER_TOKENS = 256
VMEM_LIMIT = 56 * 1024 * 1024


def _cparams(*sem):
    return pltpu.CompilerParams(dimension_semantics=sem, vmem_limit_bytes=VMEM_LIMIT)


def _dot(a, b):
    return jnp.dot(a, b, preferred_element_type=F32)


def _dot_nt(a, b):
    return lax.dot_general(a, b, (((1,), (1,)), ((), ())), preferred_element_type=F32)


def _split3(x):
    hi = x.astype(BF16)
    r = x - hi.astype(F32)
    mid = r.astype(BF16)
    lo = (r - mid.astype(F32)).astype(BF16)
    return hi, mid, lo


def _rms(x, g):
    return x * lax.rsqrt(jnp.mean(x * x, axis=-1, keepdims=True) + EPS) * g


def _head_rms(x, bd, g):
    hi, mid, lo = _split3(x * x)
    ss = _dot(hi, bd) + _dot(mid, bd) + _dot(lo, bd)
    return x * lax.rsqrt(ss * (1.0 / HEAD_DIM) + EPS) * g


def _sigmoid(x):
    return 1.0 / (1.0 + jnp.exp(-x))


def _store_token_tiles(ref, x, n):
    r = x.shape[1] // LANES
    for j in range(r):
        ref[pl.ds(j, n, stride=r), :] = x[:, j * LANES:(j + 1) * LANES]


def _load_token_tiles(ref, n, r, first=0):
    return jnp.concatenate([ref[pl.ds(first + j, n, stride=r), :] for j in range(r)], axis=-1)


def _softmax_parts(s):
    m = jnp.max(s, axis=-1, keepdims=True)
    m = jnp.where(m == NEG_INF, 0.0, m)
    p = jnp.exp(s - m)
    l = jnp.sum(p, axis=-1, keepdims=True)
    return p, l


def _proj_body(sample, tm, *refs):
    if sample:
        (x_ref, nm_ref, w_ref, qn_ref, kn1_ref, kn2_ref, cw_ref, bd_ref, s0_ref, s1_ref,
         co_ref, q_ref, kc_ref, vc_ref, ks_ref, vs_ref, kw_ref, vw_ref, gt_ref, u_ref) = refs
    else:
        (x_ref, nm_ref, w_ref, qn_ref, kn1_ref, kn2_ref, cw_ref, bd_ref,
         co_ref, q_ref, kc_ref, vc_ref, ks_ref, vs_ref, kw_ref, vw_ref, gt_ref, cs_ref,
         ksb_ref, vsb_ref, kwb_ref, vwb_ref, carry_ref) = refs

    xn = _rms(x_ref[...], nm_ref[...]).astype(BF16)

    def seg(a, b):
        return _dot(xn, w_ref[:, a:b])

    c3 = 3 * CONV_DIM
    u = seg(2 * CONV_DIM, c3) * seg(0, CONV_DIM)
    bg = seg(CONV_DIM, 2 * CONV_DIM)
    row = lax.broadcasted_iota(jnp.int32, (tm, 1), 0)
    um1 = pltpu.roll(u, 1, axis=0)
    um2 = pltpu.roll(u, 2, axis=0)
    if sample:
        r = row & 3
        s0 = s0_ref[...]
        s1 = s1_ref[...]
        prev1 = jnp.where(r == 0, s1, um1)
        prev2 = jnp.where(r == 0, s0, jnp.where(r == 1, s1, um2))
        u_ref[...] = u
    else:
        @pl.when(pl.program_id(1) == 0)
        def _():
            carry_ref[...] = jnp.zeros_like(carry_ref)
        c = carry_ref[...]
        prev1 = jnp.where(row == 0, c[7:8], um1)
        prev2 = jnp.where(row == 0, c[6:7], jnp.where(row == 1, c[7:8], um2))
        carry_ref[...] = u[tm - 8:tm]
        cs_ref[0] = u[tm - 8:tm]
    cw = cw_ref[...]
    y = cw[0:1] * prev2 + cw[1:2] * prev1 + cw[2:3] * u
    co_ref[...] = (bg * y).astype(BF16)

    bd = bd_ref[...]
    q = _head_rms(seg(c3, c3 + ATTN_DIM), bd, qn_ref[...]) * (HEAD_DIM ** -0.5)
    if sample:
        q_ref[...] = q
    else:
        lane = lax.broadcasted_iota(jnp.int32, (tm, LANES), 1)
        for hd in range(N_HEADS):
            pair = q[:, (hd // 2) * LANES:(hd // 2 + 1) * LANES]
            if (hd % 2) != (hd // GROUP):
                pair = pltpu.roll(pair, HEAD_DIM, axis=1)
            keep = (lane >= HEAD_DIM) if (hd // GROUP) else (lane < HEAD_DIM)
            q_ref[0, hd] = jnp.where(keep, pair, 0.0).astype(BF16)

    k0 = c3 + ATTN_DIM
    bdk = bd[:KV_DIM, :KV_DIM]
    kc_ref[...] = seg(k0, k0 + KV_DIM)
    vc_ref[...] = seg(k0 + KV_DIM, k0 + 2 * KV_DIM)
    ks = _head_rms(seg(k0 + 2 * KV_DIM, k0 + 3 * KV_DIM), bdk, kn1_ref[...])
    vs = seg(k0 + 3 * KV_DIM, k0 + 4 * KV_DIM)
    kw = _head_rms(seg(k0 + 4 * KV_DIM, k0 + 5 * KV_DIM), bdk, kn2_ref[...])
    vw = seg(k0 + 5 * KV_DIM, k0 + 6 * KV_DIM)
    ks_ref[...] = ks
    vs_ref[...] = vs
    kw_ref[...] = kw
    vw_ref[...] = vw
    if not sample:
        ksb_ref[...] = ks.astype(BF16)
        vsb_ref[...] = vs.astype(BF16)
        kwb_ref[...] = kw.astype(BF16)
        vwb_ref[...] = vw.astype(BF16)
    gt_ref[...] = _sigmoid(seg(k0 + 6 * KV_DIM, k0 + 6 * KV_DIM + LANES))


def _project(x2d, batch, seq, tm, weights, state=None):
    n, d = x2d.shape
    sample = state is not None
    nt = seq // tm if not sample else 1
    const = lambda shape: pl.BlockSpec(shape, lambda b, t: (0,) * len(shape))
    rows = lambda w: pl.BlockSpec((tm, w), lambda b, t: (b * nt + t, 0))
    nm, w_in, qn, kn1, kn2, cw, bd = weights
    in_specs = [rows(d), const(nm.shape), const(w_in.shape), const(qn.shape), const(kn1.shape),
                const(kn2.shape), const(cw.shape), const(bd.shape)]
    args = [x2d, nm, w_in, qn, kn1, kn2, cw, bd]
    kv_f32 = [jax.ShapeDtypeStruct((n, KV_DIM), F32)] * 6
    if sample:
        in_specs += [rows(CONV_DIM), rows(CONV_DIM)]
        args += list(state)
        out_shape = ([jax.ShapeDtypeStruct((n, CONV_DIM), BF16), jax.ShapeDtypeStruct((n, ATTN_DIM), F32)]
                     + kv_f32 + [jax.ShapeDtypeStruct((n, LANES), F32), jax.ShapeDtypeStruct((n, CONV_DIM), F32)])
        out_specs = [rows(CONV_DIM), rows(ATTN_DIM)] + [rows(KV_DIM)] * 6 + [rows(LANES), rows(CONV_DIM)]
        scratch = []
        grid = (1, 1)
    else:
        out_shape = ([jax.ShapeDtypeStruct((n, CONV_DIM), BF16),
                      jax.ShapeDtypeStruct((batch, N_HEADS, seq, LANES), BF16)]
                     + kv_f32 + [jax.ShapeDtypeStruct((n, LANES), F32),
                                 jax.ShapeDtypeStruct((batch, 8, CONV_DIM), F32)]
                     + [jax.ShapeDtypeStruct((n, KV_DIM), BF16)] * 4)
        out_specs = ([rows(CONV_DIM), pl.BlockSpec((1, N_HEADS, tm, LANES), lambda b, t: (b, 0, t, 0))]
                     + [rows(KV_DIM)] * 6 + [rows(LANES), pl.BlockSpec((1, 8, CONV_DIM), lambda b, t: (b, 0, 0))]
                     + [rows(KV_DIM)] * 4)
        scratch = [pltpu.VMEM((8, CONV_DIM), F32)]
        grid = (batch, nt)
    return pl.pallas_call(
        functools.partial(_proj_body, sample, tm),
        grid=grid, in_specs=in_specs, out_specs=out_specs, out_shape=out_shape, scratch_shapes=scratch,
        compiler_params=_cparams("arbitrary", "arbitrary"),
        name="proj_sample" if sample else "proj_prompt",
    )(*args)


def _gelu_tanh(x):
    cdf = 0.5 * (1.0 + jnp.tanh(math.sqrt(2.0 / math.pi) * (x + 0.044715 * (x * x * x))))
    return x * cdf


def _compress_core(norm, nch, x, pe_ref, we_ref, w2_ref, g_ref, o_ref):
    a0 = _dot((x + pe_ref[0:1]).astype(BF16), we_ref[0])
    a1 = _dot((x + pe_ref[1:2]).astype(BF16), we_ref[1])
    hid = a0 + pltpu.roll(a1, nch - 1, axis=0)
    w2 = w2_ref[...]
    outs = []
    for h in range(N_KV_HEADS):
        act = _gelu_tanh(hid[:, h * CMP_HIDDEN:(h + 1) * CMP_HIDDEN])
        o = _dot(act.astype(BF16), w2)
        if norm:
            o = _rms(o, g_ref[...])
        outs.append(o)
    o_ref[0] = jnp.concatenate(outs, axis=-1)


def _compress_rows_body(norm, nch, x_ref, pe_ref, we_ref, w2_ref, g_ref, o_ref):
    _compress_core(norm, nch, x_ref[0], pe_ref, we_ref, w2_ref, g_ref, o_ref)


def _compress_pages_body(norm, nch, n_pages, pt_ref, *refs):
    pages = refs[:n_pages]
    pe_ref, we_ref, w2_ref, g_ref, o_ref = refs[n_pages:]
    x = jnp.concatenate([p[0] for p in pages], axis=0)
    _compress_core(norm, nch, x, pe_ref, we_ref, w2_ref, g_ref, o_ref)


def _compress_weights(pe, w1, w2):
    w1r = w1.reshape(2, CMP_STRIDE, HEAD_DIM, CMP_HIDDEN)
    eye = jnp.eye(N_KV_HEADS, dtype=w1.dtype)
    we = jnp.einsum("jrdc,hk->jrhdkc", w1r, eye).reshape(2, CMP_STRIDE * KV_DIM, N_KV_HEADS * CMP_HIDDEN)
    per = pe.reshape(2, CMP_STRIDE, 1, HEAD_DIM)
    pex = jnp.broadcast_to(per, (2, CMP_STRIDE, N_KV_HEADS, HEAD_DIM)).reshape(2, CMP_STRIDE * KV_DIM)
    return pex.astype(F32), we.astype(BF16), w2.astype(BF16)


def _compress_rows(rows3, cw, gain, norm):
    b, nch, width = rows3.shape
    pex, we, w2 = cw
    const = lambda a: pl.BlockSpec(a.shape, lambda i: (0,) * a.ndim)
    return pl.pallas_call(
        functools.partial(_compress_rows_body, norm, nch),
        grid=(b,),
        in_specs=[pl.BlockSpec((1, nch, width), lambda i: (i, 0, 0)), const(pex), const(we), const(w2), const(gain)],
        out_specs=pl.BlockSpec((1, nch, KV_DIM), lambda i: (i, 0, 0)),
        out_shape=jax.ShapeDtypeStruct((b, nch, KV_DIM), F32),
        compiler_params=_cparams("arbitrary"),
        name="compress_rows",
    )(rows3, pex, we, w2, gain)


def _compress_pages(cache3, pt_flat, n_batch, n_pages, cw, gain, norm):
    _, pch, width = cache3.shape
    nch = n_pages * pch
    pex, we, w2 = cw
    const = lambda a: pl.BlockSpec(a.shape, lambda i, pt: (0,) * a.ndim)
    page_spec = lambda j: pl.BlockSpec((1, pch, width), lambda i, pt: (pt[i * n_pages + j], 0, 0))
    grid_spec = pltpu.PrefetchScalarGridSpec(
        num_scalar_prefetch=1, grid=(n_batch,),
        in_specs=[page_spec(j) for j in range(n_pages)] + [const(pex), const(we), const(w2), const(gain)],
        out_specs=pl.BlockSpec((1, nch, KV_DIM), lambda i, pt: (i, 0, 0)))
    return pl.pallas_call(
        functools.partial(_compress_pages_body, norm, nch, n_pages),
        grid_spec=grid_spec,
        out_shape=jax.ShapeDtypeStruct((n_batch, nch, KV_DIM), F32),
        compiler_params=_cparams("arbitrary"),
        name="compress_pages",
    )(pt_flat, *([cache3] * n_pages), pex, we, w2, gain)


def _rel_bucket(dist):
    n = jnp.maximum(dist, 0)
    max_exact = NUM_BUCKETS // 2
    nf = jnp.maximum(n, 1).astype(F32)
    large = max_exact + (jnp.log(nf / max_exact) / math.log(MAX_DISTANCE / max_exact)
                         * (NUM_BUCKETS - max_exact)).astype(jnp.int32)
    large = jnp.minimum(large, NUM_BUCKETS - 1)
    return jnp.where(n < max_exact, n, large)


def _bias_by_distance(rel_bias):
    d = jnp.arange(BIAS_DMAX, dtype=jnp.int32)
    f = rel_bias.astype(F32)[_rel_bucket(d)]
    f = (f - f[BIAS_DMAX - 1:BIAS_DMAX]).T
    return jnp.concatenate([f, jnp.full((N_HEADS, 1), NEG_INF, F32)], axis=1)


def _bias_index(d, valid):
    return np.where(valid, np.clip(d, 0, BIAS_DMAX - 1), BIAS_DMAX).astype(np.int32)


def _bias_table(fext, d, valid):
    return jnp.take(fext, jnp.asarray(_bias_index(d, valid)), axis=1)


def _toeplitz(g, n_rows, k):
    h = g.shape[0]
    w = n_rows + k
    v = jnp.pad(g, ((0, 0), (0, w - g.shape[1])))
    t = jnp.tile(v, (1, n_rows + 1))[:, :n_rows * (w + 1)].reshape(h, n_rows, w + 1)[:, :, :k]
    return t[:, :, ::-1]


def _select_blocks(imp_t, srow, qpos, n_rank):
    qblk = qpos >> 6
    forced = (srow == 0) | (srow == qblk) | (srow == qblk - 1)
    valid = (srow << 6) <= qpos
    imp_t = jnp.where(valid, imp_t + jnp.where(forced, FORCE_SCORE, 0.0), NEG_INF)
    cnt = jnp.zeros(imp_t.shape, jnp.int32)
    for s in range(n_rank):
        r = imp_t[s:s + 1, :]
        beats = (r > imp_t) | ((r == imp_t) & (srow > s))
        cnt = cnt + jnp.where(beats, 1, 0)
    return jnp.where((cnt < N_SEL) & valid, 1.0, 0.0)


def _attn_prompt_body(kt, n_slc, q_ref, gt_ref, kcmp_ref, vcmp_ref, ks_ref, vs_ref, kw_ref, vw_ref,
                      ctab_ref, ntab_ref, wtab_ref, efar_ref, enear_ref, ovl_ref, o_ref):
    i = pl.program_id(1)
    qb = Q_BLOCK
    rows = GROUP * qb
    n_cmp_pad = kcmp_ref.shape[1]
    near_start = pl.multiple_of(jnp.maximum(i - 1, 0) * qb, qb)
    win_start = pl.multiple_of(jnp.maximum(i * qb - WINDOW, 0), qb)
    band = WINDOW + qb
    gates = gt_ref[...]
    kcmp = kcmp_ref[0].astype(BF16)
    vcmp = vcmp_ref[0].astype(BF16)
    lane = lax.broadcasted_iota(jnp.int32, (qb, LANES), 1)
    heads_out = []
    for h in range(N_KV_HEADS):
        qh = q_ref[0, h * GROUP:(h + 1) * GROUP].reshape(rows, LANES)

        s = _dot_nt(qh, kcmp) + ctab_ref[h * GROUP:(h + 1) * GROUP, 0].reshape(rows, n_cmp_pad)
        p, l = _softmax_parts(s)
        pn = p / jnp.maximum(l, 1e-30)
        o_c = _dot(pn.astype(BF16), vcmp)

        psum = pn[0:qb] + pn[qb:2 * qb] + pn[2 * qb:3 * qb] + pn[3 * qb:4 * qb]
        hi, mid, lo = _split3(psum)
        ovl = ovl_ref[...]
        imp_t = _dot_nt(ovl, hi) + _dot_nt(ovl, mid) + _dot_nt(ovl, lo)
        srow = lax.broadcasted_iota(jnp.int32, (n_slc, qb), 0)
        qpos_t = i * qb + lax.broadcasted_iota(jnp.int32, (n_slc, qb), 1)
        sel_t = _select_blocks(imp_t, srow, qpos_t, n_slc)
        if n_slc < LANES:
            sel_t = jnp.concatenate([sel_t, jnp.zeros((LANES - n_slc, qb), F32)], axis=0)
        sel = sel_t.T.astype(BF16)

        kk = ks_ref[pl.ds(near_start, 2 * qb), :]
        vv = vs_ref[pl.ds(near_start, 2 * qb), :]
        mexp = _dot(sel, enear_ref[0])
        mb = jnp.where(mexp > 0.5, ntab_ref[0, h * GROUP:(h + 1) * GROUP], NEG_INF)
        s = _dot_nt(qh, kk).reshape(GROUP, qb, 2 * qb) + mb
        s = s.reshape(rows, 2 * qb)
        m0 = jnp.max(s, axis=-1, keepdims=True)
        ms = jnp.where(m0 == NEG_INF, 0.0, m0)
        p = jnp.exp(s - ms)
        l0 = jnp.sum(p, axis=-1, keepdims=True)
        a0 = _dot(p.astype(BF16), vv)

        def far_tile(t, carry):
            m_old, l_old, acc = carry
            k0 = pl.multiple_of(t * kt, kt)
            kk = ks_ref[pl.ds(k0, kt), :]
            vv = vs_ref[pl.ds(k0, kt), :]
            mexp = _dot(sel, efar_ref[t])
            kpos = k0 + lax.broadcasted_iota(jnp.int32, (qb, kt), 1)
            mb = jnp.where((mexp > 0.5) & (kpos < near_start), 0.0, NEG_INF)
            s = (_dot_nt(qh, kk).reshape(GROUP, qb, kt) + mb[None]).reshape(rows, kt)
            m_new = jnp.maximum(m_old, jnp.max(s, axis=-1, keepdims=True))
            ms = jnp.where(m_new == NEG_INF, 0.0, m_new)
            alpha = jnp.exp(m_old - ms)
            p = jnp.exp(s - ms)
            l_new = alpha * l_old + jnp.sum(p, axis=-1, keepdims=True)
            acc = alpha * acc + _dot(p.astype(BF16), vv)
            return m_new, l_new, acc

        n_far = (near_start + kt - 1) // kt
        _, l_s, acc_s = lax.fori_loop(0, n_far, far_tile, (m0, l0, a0))
        o_s = acc_s / jnp.maximum(l_s, 1e-30)

        kk = kw_ref[pl.ds(win_start, band), :]
        vv = vw_ref[pl.ds(win_start, band), :]
        s = _dot_nt(qh, kk) + wtab_ref[0, h * GROUP:(h + 1) * GROUP].reshape(rows, band)
        p, l = _softmax_parts(s)
        o_w = _dot(p.astype(BF16), vv) / jnp.maximum(l, 1e-30)

        for g in range(GROUP):
            c = (h * GROUP + g) * N_BRANCH
            sl = slice(g * qb, (g + 1) * qb)
            heads_out.append(gates[:, c:c + 1] * o_c[sl] + gates[:, c + 1:c + 2] * o_s[sl]
                             + gates[:, c + 2:c + 3] * o_w[sl])

    tiles = []
    for j in range(N_HEADS // 2):
        a, b = heads_out[2 * j], heads_out[2 * j + 1]
        if (2 * j) // GROUP == 1:
            a = pltpu.roll(a, HEAD_DIM, axis=1)
        else:
            b = pltpu.roll(b, HEAD_DIM, axis=1)
        tiles.append(jnp.where(lane < HEAD_DIM, a, b))
    o_ref[...] = jnp.concatenate(tiles, axis=-1).astype(BF16)


def _attn_prompt(qp, gates, kcmp, vcmp, ksb, vsb, kwb, vwb, fext, batch, seq):
    qb = Q_BLOCK
    nqb = seq // qb
    n_slc = seq // SLC_BLOCK
    n_cmp_pad = kcmp.shape[1]
    n_cmp = n_cmp_pad - 1
    kt = min(512, seq)
    n_kt = seq // kt
    band = WINDOW + qb
    iq = np.arange(qb)

    per_qb = qb // CMP_STRIDE
    half = 2 * per_qb
    m = np.arange(-half, half)
    d = iq[:, None] - (m[None, :] * CMP_STRIDE + CMP_BLOCK - 1)
    assert d[:, 0].min() >= MAX_DISTANCE and d[:, -1].max() < 0
    gwide = jnp.concatenate([jnp.zeros((N_HEADS, qb, n_cmp_pad - half), F32), _bias_table(fext, d, d >= 0),
                             jnp.full((N_HEADS, qb, n_cmp_pad - half), NEG_INF, F32)], axis=2)
    ctab = jnp.stack([gwide[:, :, n_cmp_pad - per_qb * i:2 * n_cmp_pad - per_qb * i] for i in range(nqb)], axis=1)
    assert (n_cmp_pad - 1) * CMP_STRIDE + CMP_BLOCK - 1 >= seq and n_cmp == n_cmp_pad - 1

    def toeplitz_variants(n_var, k, valid):
        j = np.arange(qb + k - 1)
        tabs = []
        for v in range(n_var):
            dj = qb * v + j - (k - 1)
            tabs.append(_toeplitz(_bias_table(fext, dj, valid(dj)), qb, k))
        return jnp.stack(tabs, axis=0)

    ntab = toeplitz_variants(2, 2 * qb, lambda dj: dj >= 0)
    nv = WINDOW // qb + 1
    wtab = toeplitz_variants(nv, band, lambda dj: (dj >= 0) & (dj < WINDOW))

    srow = np.arange(LANES)[None, :, None]
    efar = ((np.arange(n_kt)[:, None, None] * kt + np.arange(kt)[None, None, :]) // SLC_BLOCK == srow)
    nstart = np.maximum(np.arange(nqb) - 1, 0) * qb
    enear = ((nstart[:, None, None] + np.arange(2 * qb)[None, None, :]) // SLC_BLOCK == srow)
    c_start = np.arange(n_cmp_pad) * CMP_STRIDE
    s_start = np.arange(n_slc) * SLC_BLOCK
    ovl = ((c_start[None, :] < s_start[:, None] + SLC_BLOCK) & (c_start[None, :] + CMP_BLOCK > s_start[:, None])
           & (np.arange(n_cmp_pad) < n_cmp)[None, :])
    efar = jnp.asarray(efar, BF16)
    enear = jnp.asarray(enear, BF16)
    ovl = jnp.asarray(ovl, BF16)

    per_batch = lambda a: pl.BlockSpec((seq, a.shape[1]), lambda b, i: (b, 0))
    const = lambda a: pl.BlockSpec(a.shape, lambda b, i: (0,) * a.ndim)
    return pl.pallas_call(
        functools.partial(_attn_prompt_body, kt, n_slc),
        grid=(batch, nqb),
        in_specs=[pl.BlockSpec((1, N_HEADS, qb, LANES), lambda b, i: (b, 0, i, 0)),
                  pl.BlockSpec((qb, LANES), lambda b, i: (b * nqb + i, 0)),
                  pl.BlockSpec((1, n_cmp_pad, KV_DIM), lambda b, i: (b, 0, 0)),
                  pl.BlockSpec((1, n_cmp_pad, KV_DIM), lambda b, i: (b, 0, 0)),
                  per_batch(ksb), per_batch(vsb), per_batch(kwb), per_batch(vwb),
                  pl.BlockSpec((N_HEADS, 1, qb, n_cmp_pad), lambda b, i: (0, i, 0, 0)),
                  pl.BlockSpec((1, N_HEADS, qb, 2 * qb), lambda b, i: (jnp.minimum(i, 1), 0, 0, 0)),
                  pl.BlockSpec((1, N_HEADS, qb, band), lambda b, i: (jnp.minimum(i, nv - 1), 0, 0, 0)),
                  const(efar),
                  pl.BlockSpec((1, LANES, 2 * qb), lambda b, i: (i, 0, 0)),
                  const(ovl)],
        out_specs=pl.BlockSpec((qb, ATTN_DIM), lambda b, i: (b * nqb + i, 0)),
        out_shape=jax.ShapeDtypeStruct((batch * seq, ATTN_DIM), BF16),
        compiler_params=_cparams("arbitrary", "arbitrary"),
        name="attn_prompt",
    )(qp, gates, kcmp, vcmp, ksb, vsb, kwb, vwb, ctab, ntab, wtab, efar, enear, ovl)


def _attn_sample_body(n_pages, ts, past_len, pt_ref, *refs):
    kpages = refs[:n_pages]
    vpages = refs[n_pages:2 * n_pages]
    (q_ref, gt_ref, kcmp_ref, vcmp_ref, ksn_ref, vsn_ref, kwc_ref, vwc_ref, kwn_ref, vwn_ref,
     ctab_ref, stab_ref, sntab_ref, wtab_ref, wntab_ref, eexp_ref, ovl_ref, o_ref) = refs[2 * n_pages:]
    rows = GROUP * N_KV_HEADS * ts
    rq = N_KV_HEADS * ts
    q = q_ref[0]
    gates = gt_ref[0]

    s = _dot_nt(q, kcmp_ref[0].astype(BF16)) + ctab_ref[...]
    p, l = _softmax_parts(s)
    pn = p / jnp.maximum(l, 1e-30)
    o_c = _dot(pn.astype(BF16), vcmp_ref[0].astype(BF16))

    psum = pn[0:rq]
    for g in range(1, GROUP):
        psum = psum + pn[g * rq:(g + 1) * rq]
    hi, mid, lo = _split3(psum)
    ovl = ovl_ref[...]
    imp = _dot(hi, ovl) + _dot(mid, ovl) + _dot(lo, ovl)
    n_slc = -(-(past_len + ts) // SLC_BLOCK)
    blk = lax.broadcasted_iota(jnp.int32, (rq, LANES), 1)
    qpos = past_len + (lax.broadcasted_iota(jnp.int32, (rq, LANES), 0) & (ts - 1))
    qblk = qpos >> 6
    forced = (blk == 0) | (blk == qblk) | (blk == qblk - 1)
    valid = ((blk << 6) <= qpos) & (blk < n_slc)
    imp = jnp.where(valid, imp + jnp.where(forced, FORCE_SCORE, 0.0), NEG_INF)
    cnt = jnp.zeros((rq, LANES), jnp.int32)
    for sidx in range(n_slc):
        r = imp[:, sidx:sidx + 1]
        beats = (r > imp) | ((r == imp) & (blk > sidx))
        cnt = cnt + jnp.where(beats, 1, 0)
    sel = jnp.where((cnt < N_SEL) & valid, 1.0, 0.0)
    sel = jnp.concatenate([sel] * GROUP, axis=0).astype(BF16)

    kc_t = jnp.concatenate([p_[0] for p_ in kpages], axis=1).astype(BF16)
    vc_t = jnp.concatenate([p_[0] for p_ in vpages], axis=1).astype(BF16)
    mexp = _dot(sel, eexp_ref[...])
    s1 = jnp.where(mexp > 0.5, _dot(q, kc_t) + stab_ref[...], NEG_INF)
    last = sel[:, n_slc - 1:n_slc].astype(F32)
    s2 = jnp.where(last > 0.5, _dot_nt(q, ksn_ref[0].astype(BF16)) + sntab_ref[...], NEG_INF)
    m = jnp.maximum(jnp.max(s1, axis=-1, keepdims=True), jnp.max(s2, axis=-1, keepdims=True))
    m = jnp.where(m == NEG_INF, 0.0, m)
    p1 = jnp.exp(s1 - m)
    p2 = jnp.exp(s2 - m)
    l = jnp.sum(p1, axis=-1, keepdims=True) + jnp.sum(p2, axis=-1, keepdims=True)
    o_s = (_dot_nt(p1.astype(BF16), vc_t) + _dot(p2.astype(BF16), vsn_ref[0].astype(BF16))) / jnp.maximum(l, 1e-30)

    s1 = _dot(q, kwc_ref[0].astype(BF16)) + wtab_ref[...]
    s2 = _dot_nt(q, kwn_ref[0].astype(BF16)) + wntab_ref[...]
    m = jnp.maximum(jnp.max(s1, axis=-1, keepdims=True), jnp.max(s2, axis=-1, keepdims=True))
    m = jnp.where(m == NEG_INF, 0.0, m)
    p1 = jnp.exp(s1 - m)
    p2 = jnp.exp(s2 - m)
    l = jnp.sum(p1, axis=-1, keepdims=True) + jnp.sum(p2, axis=-1, keepdims=True)
    o_w = (_dot_nt(p1.astype(BF16), vwc_ref[0].astype(BF16))
           + _dot(p2.astype(BF16), vwn_ref[0].astype(BF16))) / jnp.maximum(l, 1e-30)

    o_ref[0] = gates[:, 0:1] * o_c + gates[:, 1:2] * o_s + gates[:, 2:3] * o_w


def _attn_sample(q_s, gates_s, kcmp, vcmp, ks_new, vs_new, kw_new, vw_new, cache_ks, cache_vs,
                 cache_kw, cache_vw, pt_flat, fext, n_batch, ts, n_pages, page):
    past_len = n_pages * page
    w_buf = cache_kw.shape[2]
    rows = GROUP * N_KV_HEADS * ts
    n_new = 8
    n_cmp_pad = kcmp.shape[1]
    n_cmp = n_cmp_pad - 1
    n_slc = -(-(past_len + ts) // SLC_BLOCK)

    q5 = q_s.reshape(n_batch, ts, N_KV_HEADS, GROUP, HEAD_DIM).transpose(0, 3, 2, 1, 4)
    eye = jnp.eye(N_KV_HEADS, dtype=q_s.dtype)
    qr = jnp.einsum("bghtd,hk->bghtkd", q5, eye).reshape(n_batch, rows, LANES).astype(BF16)
    g5 = gates_s[:, :N_HEADS * N_BRANCH].reshape(n_batch, ts, N_KV_HEADS, GROUP, N_BRANCH).transpose(0, 3, 2, 1, 4)
    gr = jnp.pad(g5.reshape(n_batch, rows, N_BRANCH), ((0, 0), (0, 0), (0, LANES - N_BRANCH)))
    pad_new = lambda a: jnp.pad(a.reshape(n_batch, ts, KV_DIM), ((0, 0), (0, n_new - ts), (0, 0)))
    ks_new, vs_new, kw_new, vw_new = map(pad_new, (ks_new, vs_new, kw_new, vw_new))

    g_i, h_i, t_i = np.meshgrid(np.arange(GROUP), np.arange(N_KV_HEADS), np.arange(ts), indexing="ij")
    head = (h_i * GROUP + g_i).reshape(rows)
    tq = t_i.reshape(rows)
    pos_q = past_len + tq

    f_rows = fext[jnp.asarray(head)]

    def table(d, valid):
        return jnp.take_along_axis(f_rows, jnp.asarray(_bias_index(d, valid)), axis=1)

    nn = np.arange(n_cmp_pad)
    d = pos_q[:, None] - (nn[None, :] * CMP_STRIDE + CMP_BLOCK - 1)
    ctab = table(d, (d >= 0) & (nn < n_cmp)[None, :])
    near = np.arange(past_len - MAX_DISTANCE, past_len)
    d = pos_q[:, None] - near[None, :]
    assert past_len >= MAX_DISTANCE and d.min() >= 0
    stab = jnp.concatenate([jnp.zeros((rows, past_len - MAX_DISTANCE), F32), table(d, d >= 0)], axis=1)
    jn = np.arange(n_new)
    d = tq[:, None] - jn[None, :]
    sntab = table(d, (d >= 0) & (jn < ts)[None, :])
    pos_w = past_len - w_buf + np.arange(w_buf)
    d = pos_q[:, None] - pos_w[None, :]
    wtab = table(d, (d >= 0) & (d < WINDOW) & (pos_w >= 0)[None, :])
    wntab = table(tq[:, None] - jn[None, :], (tq[:, None] >= jn[None, :]) & (jn < ts)[None, :])

    eexp = jnp.asarray(np.arange(past_len)[None, :] // SLC_BLOCK == np.arange(LANES)[:, None], BF16)
    c_start = nn * CMP_STRIDE
    s_start = np.arange(LANES) * SLC_BLOCK
    ovl = jnp.asarray((c_start[:, None] < s_start[None, :] + SLC_BLOCK) & (c_start[:, None] + CMP_BLOCK > s_start[None, :])
                      & (nn < n_cmp)[:, None] & (np.arange(LANES) < n_slc)[None, :], BF16)

    const = lambda a: pl.BlockSpec(a.shape, lambda b, pt: (0,) * a.ndim)
    per_b = lambda a: pl.BlockSpec((1,) + a.shape[1:], lambda b, pt: (b,) + (0,) * (a.ndim - 1))
    page_spec = lambda j: pl.BlockSpec((1, KV_DIM, page), lambda b, pt: (pt[b * n_pages + j], 0, 0))
    small = [qr, gr, kcmp, vcmp, ks_new, vs_new, cache_kw, cache_vw, kw_new, vw_new]
    consts = [ctab, stab, sntab, wtab, wntab, eexp, ovl]
    grid_spec = pltpu.PrefetchScalarGridSpec(
        num_scalar_prefetch=1, grid=(n_batch,),
        in_specs=[page_spec(j) for j in range(n_pages)] * 2 + [per_b(a) for a in small] + [const(a) for a in consts],
        out_specs=pl.BlockSpec((1, rows, LANES), lambda b, pt: (b, 0, 0)))
    o = pl.pallas_call(
        functools.partial(_attn_sample_body, n_pages, ts, past_len),
        grid_spec=grid_spec,
        out_shape=jax.ShapeDtypeStruct((n_batch, rows, LANES), F32),
        compiler_params=_cparams("arbitrary"),
        name="attn_sample",
    )(pt_flat, *([cache_ks] * n_pages), *([cache_vs] * n_pages), *small, *consts)
    o6 = o.reshape(n_batch, GROUP, N_KV_HEADS, ts, N_KV_HEADS, HEAD_DIM)
    o5 = jnp.stack([o6[:, :, h, :, h] for h in range(N_KV_HEADS)], axis=2)
    return o5.transpose(0, 3, 2, 1, 4).reshape(n_batch * ts, ATTN_DIM).astype(BF16)


def _post1_body(tm, x_ref, co_ref, at_ref, wo_ref, nf_ref, wr_ref, br_ref, tri_ref, run0_ref,
                h_ref, hn_ref, rt_ref, cnt_ref, run_ref):
    @pl.when(pl.program_id(0) == 0)
    def _():
        run_ref[...] = run0_ref[...]

    h = x_ref[...] + _dot(co_ref[...], wo_ref[0:CONV_DIM]) + _dot(at_ref[...], wo_ref[CONV_DIM:CONV_DIM + ATTN_DIM])
    hn = _rms(h, nf_ref[...])
    h_ref[...] = h
    _store_token_tiles(hn_ref, hn, tm)

    hi = hn.astype(BF16)
    lo = (hn - hi.astype(F32)).astype(BF16)
    wr = wr_ref[...]
    whi = wr.astype(BF16)
    wlo = (wr - whi.astype(F32)).astype(BF16)
    logits = _dot(hi, whi) + _dot(lo, whi) + _dot(hi, wlo) + br_ref[...]

    lane_i = lax.broadcasted_iota(jnp.int32, (tm, LANES), 1)
    lane = lane_i.astype(F32)
    big = float(LANES)
    gmask = (lane_i >= ROUTER_GROUP_LANE) & (lane_i < ROUTER_GROUP_LANE + N_GROUPS)
    lg = jnp.where(gmask, logits, NEG_INF)
    eg = jnp.exp(lg - jnp.max(lg, axis=-1, keepdims=True))
    pg = eg / jnp.sum(eg, axis=-1, keepdims=True)
    gw = jnp.max(pg, axis=-1, keepdims=True)
    grp = jnp.min(jnp.where(gmask & (pg == gw), lane, big), axis=-1, keepdims=True) - ROUTER_GROUP_LANE

    group_of_lane = (lane_i >> 3).astype(F32)
    emask = (lane_i < N_EXPERTS) & (group_of_lane == grp)
    le = jnp.where(emask, logits, NEG_INF)
    ee = jnp.exp(le - jnp.max(le, axis=-1, keepdims=True))
    pe = jnp.where(emask, ee / jnp.sum(ee, axis=-1, keepdims=True), -1.0)
    v1 = jnp.max(pe, axis=-1, keepdims=True)
    i1 = jnp.min(jnp.where(pe == v1, lane, big), axis=-1, keepdims=True)
    pe2 = jnp.where(lane == i1, -1.0, pe)
    v2 = jnp.max(pe2, axis=-1, keepdims=True)
    i2 = jnp.min(jnp.where(pe2 == v2, lane, big), axis=-1, keepdims=True)
    tot = v1 + v2
    w1 = v1 / tot * gw
    w2 = v2 / tot * gw

    oh1 = jnp.where(lane == i1, 1.0, 0.0)
    oh2 = jnp.where(lane == i2, 1.0, 0.0)
    both = oh1 + oh2
    before = _dot(tri_ref[...], both.astype(BF16)) + run_ref[0:1]
    r1 = jnp.sum(oh1 * before, axis=-1, keepdims=True)
    r2 = jnp.sum(oh2 * before, axis=-1, keepdims=True)
    run = run_ref[0:1] + jnp.sum(both, axis=0, keepdims=True)
    run_ref[...] = jnp.broadcast_to(run, run_ref.shape)
    cnt_ref[...] = jnp.broadcast_to(run, cnt_ref.shape)

    rt = jnp.where(lane_i == 0, i1, 0.0)
    rt = jnp.where(lane_i == 1, i2, rt)
    rt = jnp.where(lane_i == 2, r1, rt)
    rt = jnp.where(lane_i == 3, r2, rt)
    rt = jnp.where(lane_i == 4, w1, rt)
    rt = jnp.where(lane_i == 5, w2, rt)
    rt_ref[...] = rt


def _post1(x2d, co, at, wo, nf, wr, br, run0, tm):
    n, d = x2d.shape
    tri = jnp.asarray(np.tril(np.ones((tm, tm), np.float32), -1), BF16)
    rows = lambda w: pl.BlockSpec((tm, w), lambda i: (i, 0))
    const = lambda a: pl.BlockSpec(a.shape, lambda i: (0,) * a.ndim)
    return pl.pallas_call(
        functools.partial(_post1_body, tm),
        grid=(n // tm,),
        in_specs=[rows(d), rows(CONV_DIM), rows(ATTN_DIM), const(wo), const(nf), const(wr), const(br),
                  const(tri), const(run0)],
        out_specs=[rows(d), pl.BlockSpec((tm * TOKEN_TILE_ROWS, LANES), lambda i: (i, 0)), rows(LANES),
                   pl.BlockSpec((8, LANES), lambda i: (0, 0))],
        out_shape=[jax.ShapeDtypeStruct((n, d), F32), jax.ShapeDtypeStruct((n * TOKEN_TILE_ROWS, LANES), F32),
                   jax.ShapeDtypeStruct((n, LANES), F32), jax.ShapeDtypeStruct((8, LANES), F32)],
        scratch_shapes=[pltpu.VMEM((8, LANES), F32)],
        compiler_params=_cparams("arbitrary"),
        name="post1",
    )(x2d, co, at, wo, nf, wr, br, tri, run0)


def _token_copy(src_ref, dst_ref, s, d, sem):
    r = TOKEN_TILE_ROWS
    return pltpu.make_async_copy(src_ref.at[pl.ds(pl.multiple_of(s * r, r), r)],
                                 dst_ref.at[pl.ds(pl.multiple_of(d * r, r), r)], sem)


def _scatter_rows_body(ts, dest_ref, src_ref, init_ref, out_ref, sem):
    del init_ref
    base = pl.program_id(0) * (2 * ts)

    def issue(t, _):
        for k in range(2):
            _token_copy(src_ref, out_ref, t, dest_ref[base + 2 * t + k], sem).start(priority=k)
        return 0

    lax.fori_loop(0, ts, issue, 0)
    for _ in range(2):
        pltpu.make_async_copy(src_ref, out_ref.at[pl.ds(0, ts * TOKEN_TILE_ROWS)], sem).wait()


def _scatter_rows(dest, src, slots):
    n_tok = dest.shape[0] // 2
    ts = min(SCATTER_TOKENS, n_tok)
    assert n_tok % ts == 0
    any_spec = pl.BlockSpec(memory_space=pl.ANY)
    return pl.pallas_call(
        functools.partial(_scatter_rows_body, ts),
        grid_spec=pltpu.PrefetchScalarGridSpec(
            num_scalar_prefetch=1, grid=(n_tok // ts,),
            in_specs=[pl.BlockSpec((ts * TOKEN_TILE_ROWS, LANES), lambda i, dest: (i, 0)), any_spec],
            out_specs=any_spec, scratch_shapes=[pltpu.SemaphoreType.DMA(())]),
        out_shape=jax.ShapeDtypeStruct(slots.shape, slots.dtype),
        input_output_aliases={2: 0},
        compiler_params=pltpu.CompilerParams(dimension_semantics=("arbitrary",)),
        name="scatter_rows",
    )(dest, src, slots)


def _experts_body(be_ref, nu_ref, x_ref, wg_ref, wu_ref, wd_ref, o_ref, wg_s, wu_s, wd_s):
    i = pl.program_id(0)

    @pl.when(i < nu_ref[0])
    def _():
        prev = be_ref[jnp.maximum(i - 1, 0)]

        @pl.when((i == 0) | (be_ref[i] != prev))
        def _():
            wg_s[...] = wg_ref[0].astype(BF16)
            wu_s[...] = wu_ref[0].astype(BF16)
            wd_s[...] = wd_ref[0].astype(BF16)

        x = _load_token_tiles(x_ref, EXPERT_ROWS, TOKEN_TILE_ROWS).astype(BF16)
        g = _dot(x, wg_s[...])
        u = _dot(x, wu_s[...])
        a = g * _sigmoid(g) * u
        _store_token_tiles(o_ref, _dot(a.astype(BF16), wd_s[...]), EXPERT_ROWS)

    @pl.when(i >= nu_ref[0])
    def _():
        o_ref[...] = jnp.zeros_like(o_ref)


def _experts(blk_expert, n_used, xs, wg, wu, wd):
    blk_rows = EXPERT_ROWS * TOKEN_TILE_ROWS
    n_blk = xs.shape[0] // blk_rows
    _, d, de = wg.shape
    xmap = lambda i, be, nu: (jnp.minimum(i, jnp.maximum(nu[0] - 1, 0)), 0)
    wmap = lambda i, be, nu: (be[jnp.minimum(i, jnp.maximum(nu[0] - 1, 0))], 0, 0)
    grid_spec = pltpu.PrefetchScalarGridSpec(
        num_scalar_prefetch=2, grid=(n_blk,),
        in_specs=[pl.BlockSpec((blk_rows, LANES), xmap), pl.BlockSpec((1, d, de), wmap),
                  pl.BlockSpec((1, d, de), wmap), pl.BlockSpec((1, de, d), wmap)],
        out_specs=pl.BlockSpec((blk_rows, LANES), lambda i, be, nu: (i, 0)),
        scratch_shapes=[pltpu.VMEM((d, de), BF16), pltpu.VMEM((d, de), BF16), pltpu.VMEM((de, d), BF16)])
    return pl.pallas_call(
        _experts_body, grid_spec=grid_spec,
        out_shape=jax.ShapeDtypeStruct(xs.shape, F32),
        compiler_params=_cparams("arbitrary"),
        name="experts",
    )(blk_expert, n_used, xs, wg, wu, wd)


def _post2_body(tm, dest_ref, h_ref, rt_ref, p_ref, yb_ref, wple_ref, wpg_ref, bpg_ref, np_ref, o_ref, buf, sem):
    i = pl.program_id(0)
    n = pl.num_programs(0)

    def fetch(step, slot):
        base = step * (2 * tm)

        def issue(t, _):
            for k in range(2):
                _token_copy(yb_ref, buf.at[slot], dest_ref[base + 2 * t + k], k * tm + t, sem.at[slot]).start(priority=k)
            return 0

        lax.fori_loop(0, tm, issue, 0)

    @pl.when(i == 0)
    def _():
        fetch(0, 0)

    @pl.when(i + 1 < n)
    def _():
        fetch(i + 1, (i + 1) & 1)

    slot = i & 1

    pltpu.make_async_copy(yb_ref.at[pl.ds(0, 2 * tm * TOKEN_TILE_ROWS)], buf.at[slot], sem.at[slot]).wait()
    rt = rt_ref[...]
    y0 = _load_token_tiles(buf.at[slot], tm, TOKEN_TILE_ROWS)
    y1 = _load_token_tiles(buf.at[slot], tm, TOKEN_TILE_ROWS, first=tm * TOKEN_TILE_ROWS)
    h = h_ref[...] + (y0 * rt[:, 4:5] + y1 * rt[:, 5:6])
    gate = _sigmoid(_dot(_rms(h, np_ref[...]).astype(BF16), wpg_ref[...]) + bpg_ref[...])
    o_ref[...] = h + gate * _dot(p_ref[...].astype(BF16), wple_ref[...])


def _post2(dest, h, rt, p2d, yb, wple, wpg, bpg, npl, tm):
    n, d = h.shape
    rows = lambda w: pl.BlockSpec((tm, w), lambda i, dest: (i, 0))
    const = lambda a: pl.BlockSpec(a.shape, lambda i, dest: (0,) * a.ndim)
    grid_spec = pltpu.PrefetchScalarGridSpec(
        num_scalar_prefetch=1, grid=(n // tm,),
        in_specs=[rows(d), rows(LANES), rows(p2d.shape[1]), pl.BlockSpec(memory_space=pl.ANY), const(wple),
                  const(wpg), const(bpg), const(npl)],
        out_specs=rows(d),
        scratch_shapes=[pltpu.VMEM((2, 2 * tm * TOKEN_TILE_ROWS, LANES), F32), pltpu.SemaphoreType.DMA((2,))])
    return pl.pallas_call(
        functools.partial(_post2_body, tm),
        grid_spec=grid_spec,
        out_shape=jax.ShapeDtypeStruct((n, d), F32),
        compiler_params=_cparams("arbitrary"),
        name="post2",
    )(dest, h, rt, p2d, yb, wple, wpg, bpg, npl)


def _row_tile(n, cap=512):
    t = min(cap, n)
    assert n % t == 0 and t % 8 == 0
    return t


def kernel(x_prompt, x_sample, p_prompt, p_sample, cache_k_cmp, cache_v_cmp, cache_k_slc, cache_v_slc, cache_k_win, cache_v_win, state_conv, page_table, w_in, w_out, conv_w, norm_mix, norm_ffn, norm_ple, q_norm, k_norm, cmp_pe_k, cmp_w1_k, cmp_w2_k, cmp_pe_v, cmp_w1_v, cmp_w2_v, rel_bias, w_router_group, b_router_group, w_router_expert, b_router_expert, w_exp_gate, w_exp_up, w_exp_down, w_ple, w_ple_gate, b_ple_gate):
    assert w_in.shape[0] == 1, "single-layer step"
    bp, t, d = x_prompt.shape
    bs, ts, _ = x_sample.shape
    n_pages = page_table.shape[1]
    page = cache_k_cmp.shape[2]
    past_len = n_pages * page
    w_buf = cache_k_win.shape[2]
    n_phys = cache_k_cmp.shape[1]
    assert t % Q_BLOCK == 0 and t >= WINDOW + Q_BLOCK and page % CMP_STRIDE == 0 and ts == 4 and d == D_MODEL
    assert past_len % SLC_BLOCK == 0
    np_rows, ns_rows = bp * t, bs * ts

    row = lambda v: v.reshape(1, -1).astype(F32)
    w_in_b = jnp.pad(w_in[0], ((0, 0), (0, Z_COLS - w_in.shape[2]))).astype(BF16)
    qn = row(jnp.tile(q_norm[0], N_HEADS))
    kn1 = row(jnp.tile(k_norm[0, 1], N_KV_HEADS))
    kn2 = row(jnp.tile(k_norm[0, 2], N_KV_HEADS))
    bd = jnp.asarray(np.kron(np.eye(N_HEADS), np.ones((HEAD_DIM, HEAD_DIM))), BF16)
    pw = (row(norm_mix[0]), w_in_b, qn, kn1, kn2, conv_w[0].astype(F32), bd)
    cw_k = _compress_weights(cmp_pe_k[0], cmp_w1_k[0], cmp_w2_k[0])
    cw_v = _compress_weights(cmp_pe_v[0], cmp_w1_v[0], cmp_w2_v[0])
    kn0 = row(k_norm[0, 0])
    fext = _bias_by_distance(rel_bias)
    pt_flat = page_table.reshape(-1).astype(jnp.int32)
    wr = jnp.zeros((d, LANES), F32).at[:, :N_EXPERTS].set(w_router_expert[0])
    wr = wr.at[:, ROUTER_GROUP_LANE:ROUTER_GROUP_LANE + N_GROUPS].set(w_router_group[0])
    br = jnp.zeros((1, LANES), F32).at[0, :N_EXPERTS].set(b_router_expert[0])
    br = br.at[0, ROUTER_GROUP_LANE:ROUTER_GROUP_LANE + N_GROUPS].set(b_router_group[0])
    wo_b = w_out[0].astype(BF16)
    wple_b = w_ple[0].astype(BF16)
    wpg_b = w_ple_gate[0].astype(BF16)

    tm_p = _row_tile(t)
    (co_p, q_p, kc_p, vc_p, ks_p, vs_p, kw_p, vw_p, gt_p, cs_p, ksb, vsb, kwb, vwb) = _project(
        x_prompt.reshape(np_rows, d), bp, t, tm_p, pw)
    chunk_w = CMP_STRIDE * KV_DIM
    kcmp_p = _compress_rows(kc_p.reshape(bp, t // CMP_STRIDE, chunk_w), cw_k, kn0, True)
    vcmp_p = _compress_rows(vc_p.reshape(bp, t // CMP_STRIDE, chunk_w), cw_v, kn0, False)
    at_p = _attn_prompt(q_p, gt_p, kcmp_p, vcmp_p, ksb, vsb, kwb, vwb, fext, bp, t)

    st = state_conv[0].astype(F32)
    s0 = jnp.repeat(st[:, 0], ts, axis=0)
    s1 = jnp.repeat(st[:, 1], ts, axis=0)
    (co_s, q_s, kc_s, vc_s, ks_s, vs_s, kw_s, vw_s, gt_s, u_s) = _project(
        x_sample.reshape(ns_rows, d), bs, ts, ns_rows, pw, state=(s0, s1))
    pch = page // CMP_STRIDE
    kcmp_s = _compress_pages(cache_k_cmp[0].reshape(n_phys, pch, chunk_w), pt_flat, bs, n_pages, cw_k, kn0, True)
    vcmp_s = _compress_pages(cache_v_cmp[0].reshape(n_phys, pch, chunk_w), pt_flat, bs, n_pages, cw_v, kn0, False)
    feature_major = lambda c, n, rows_: jnp.transpose(c[0], (0, 2, 3, 1)).reshape(n, KV_DIM, rows_)
    at_s = _attn_sample(q_s, gt_s, kcmp_s, vcmp_s, ks_s, vs_s, kw_s, vw_s,
                        feature_major(cache_k_slc, n_phys, page), feature_major(cache_v_slc, n_phys, page),
                        feature_major(cache_k_win, bs, w_buf), feature_major(cache_v_win, bs, w_buf),
                        pt_flat, fext, bs, ts, n_pages, page)

    tp1 = _row_tile(np_rows)
    ts1 = _row_tile(ns_rows)
    nf = row(norm_ffn[0])
    h_p, hn_p, rt_p, cnt_p = _post1(x_prompt.reshape(np_rows, d), co_p, at_p, wo_b, nf, wr, br,
                                    jnp.zeros((8, LANES), F32), tp1)
    h_s, hn_s, rt_s, cnt_s = _post1(x_sample.reshape(ns_rows, d), co_s, at_s, wo_b, nf, wr, br, cnt_p, ts1)

    counts = cnt_s[0, :N_EXPERTS].astype(jnp.int32)
    padded = (counts + EXPERT_ROWS - 1) // EXPERT_ROWS * EXPERT_ROWS
    pad_end = jnp.cumsum(padded)
    pad_start = pad_end - padded
    n_assign = 2 * (np_rows + ns_rows)
    n_blk = (n_assign + N_EXPERTS * (EXPERT_ROWS - 1) + EXPERT_ROWS - 1) // EXPERT_ROWS
    blk_first = jnp.arange(n_blk, dtype=jnp.int32) * EXPERT_ROWS
    blk_expert = jnp.minimum(jnp.sum((pad_end[None, :] <= blk_first[:, None]).astype(jnp.int32), axis=1),
                             N_EXPERTS - 1)
    n_used = (pad_end[-1:] // EXPERT_ROWS).astype(jnp.int32)

    def dest_of(rt):
        e = rt[:, 0:2].astype(jnp.int32).reshape(-1)
        return pad_start[e] + rt[:, 2:4].astype(jnp.int32).reshape(-1)

    dest_p = dest_of(rt_p)
    dest_s = dest_of(rt_s)

    xs = jnp.zeros((n_blk * EXPERT_ROWS * TOKEN_TILE_ROWS, LANES), F32)
    xs = _scatter_rows(dest_p, hn_p, xs)
    xs = _scatter_rows(dest_s, hn_s, xs)
    yb = _experts(blk_expert, n_used, xs, w_exp_gate[0], w_exp_up[0], w_exp_down[0])

    bpg = row(b_ple_gate[0])
    npl = row(norm_ple[0])
    y_p = _post2(dest_p, h_p, rt_p, p_prompt[0].reshape(np_rows, -1), yb, wple_b, wpg_b, bpg, npl, tp1)
    y_s = _post2(dest_s, h_s, rt_s, p_sample[0].reshape(ns_rows, -1), yb, wple_b, wpg_b, bpg, npl, ts1)

    kv5 = lambda a, b, s: a.reshape(1, b, s, N_KV_HEADS, HEAD_DIM)
    wp = min(WINDOW, t)
    win_p = lambda a: kv5(a, bp, t)[:, :, t - wp:]
    win_s = lambda c, new: jnp.concatenate([c[0], new.reshape(bs, ts, N_KV_HEADS, HEAD_DIM)], axis=1)[None, :, ts:]
    conv_p = cs_p[:, 8 - (CONV_K - 1):][None]
    conv_s = u_s.reshape(bs, ts, CONV_DIM)[:, ts - (CONV_K - 1):][None]
    return (y_p.reshape(bp, t, d), y_s.reshape(bs, ts, d),
            kv5(kc_p, bp, t), kv5(vc_p, bp, t), kv5(ks_p, bp, t), kv5(vs_p, bp, t), win_p(kw_p), win_p(vw_p), conv_p,
            kv5(kc_s, bs, ts), kv5(vc_s, bs, ts), kv5(ks_s, bs, ts), kv5(vs_s, bs, ts),
            win_s(cache_k_win, kw_s), win_s(cache_v_win, vw_s), conv_s)
```

```python
import functools
import math

import numpy as np
import jax
import jax.numpy as jnp
from jax import lax
from jax.experimental import pallas as pl
from jax.experimental.pallas import tpu as pltpu

F32 = jnp.float32
BF16 = jnp.bfloat16
NEG_INF = float("-inf")
MASKED = -1e30

HEAD_DIM = 64
N_HEADS = 8
N_KV_HEADS = 2
GROUP = N_HEADS // N_KV_HEADS
CONV_DIM = 512
ATTN_DIM = 512
KV_DIM = N_KV_HEADS * HEAD_DIM
N_BRANCH = 3
CONV_K = 3
CMP_BLOCK = 32
CMP_STRIDE = 16
CMP_HIDDEN = 256
SLC_BLOCK = 64
N_SEL = 16
WINDOW = 512
Q_BLOCK = 128
FORCE_SCORE = 1e4
NUM_BUCKETS = 32
MAX_DISTANCE = 128
N_GROUPS = 4
EXPERTS_PER_GROUP = 8
N_EXPERTS = N_GROUPS * EXPERTS_PER_GROUP
D_EXPERT = 512
EPS = 1e-6

D_MODEL = 1024
LANES = 128
TOKEN_TILE_ROWS = D_MODEL // LANES
Z_COLS = 3 * CONV_DIM + ATTN_DIM + 6 * KV_DIM + LANES
BIAS_DMAX = 768
EXPERT_ROWS = 256
ROUTER_GROUP_LANE = 32
SCATTER_TOKENS = 256
VMEM_LIMIT = 56 * 1024 * 1024


def _cparams(*sem):
    return pltpu.CompilerParams(dimension_semantics=sem, vmem_limit_bytes=VMEM_LIMIT)


def _dot(a, b):
    return jnp.dot(a, b, preferred_element_type=F32)


def _dot_nt(a, b):
    return lax.dot_general(a, b, (((1,), (1,)), ((), ())), preferred_element_type=F32)


def _split3(x):
    hi = x.astype(BF16)
    r = x - hi.astype(F32)
    mid = r.astype(BF16)
    lo = (r - mid.astype(F32)).astype(BF16)
    return hi, mid, lo


def _rms(x, g):
    return x * lax.rsqrt(jnp.mean(x * x, axis=-1, keepdims=True) + EPS) * g


def _head_rms(x, bd, g):
    hi, mid, lo = _split3(x * x)
    ss = _dot(hi, bd) + _dot(mid, bd) + _dot(lo, bd)
    return x * lax.rsqrt(ss * (1.0 / HEAD_DIM) + EPS) * g


def _sigmoid(x):
    return 1.0 / (1.0 + jnp.exp(-x))


def _store_token_tiles(ref, x, n):
    r = x.shape[1] // LANES
    for j in range(r):
        ref[pl.ds(j, n, stride=r), :] = x[:, j * LANES:(j + 1) * LANES]


def _load_token_tiles(ref, n, r, first=0):
    return jnp.concatenate([ref[pl.ds(first + j, n, stride=r), :] for j in range(r)], axis=-1)


def _softmax_parts(s):
    m = jnp.max(s, axis=-1, keepdims=True)
    m = jnp.where(m == NEG_INF, 0.0, m)
    p = jnp.exp(s - m)
    l = jnp.sum(p, axis=-1, keepdims=True)
    return p, l


def _proj_body(sample, tm, *refs):
    if sample:
        (x_ref, nm_ref, w_ref, qn_ref, kn1_ref, kn2_ref, cw_ref, bd_ref, s0_ref, s1_ref,
         co_ref, q_ref, kc_ref, vc_ref, ks_ref, vs_ref, kw_ref, vw_ref, gt_ref, u_ref) = refs
    else:
        (x_ref, nm_ref, w_ref, qn_ref, kn1_ref, kn2_ref, cw_ref, bd_ref,
         co_ref, q_ref, kc_ref, vc_ref, ks_ref, vs_ref, kw_ref, vw_ref, gt_ref, cs_ref,
         ksb_ref, vsb_ref, kwb_ref, vwb_ref, carry_ref) = refs

    xn = _rms(x_ref[...], nm_ref[...]).astype(BF16)

    def seg(a, b):
        return _dot(xn, w_ref[:, a:b])

    c3 = 3 * CONV_DIM
    u = seg(2 * CONV_DIM, c3) * seg(0, CONV_DIM)
    bg = seg(CONV_DIM, 2 * CONV_DIM)
    row = lax.broadcasted_iota(jnp.int32, (tm, 1), 0)
    um1 = pltpu.roll(u, 1, axis=0)
    um2 = pltpu.roll(u, 2, axis=0)
    if sample:
        r = row & 3
        s0 = s0_ref[...]
        s1 = s1_ref[...]
        prev1 = jnp.where(r == 0, s1, um1)
        prev2 = jnp.where(r == 0, s0, jnp.where(r == 1, s1, um2))
        u_ref[...] = u
    else:
        @pl.when(pl.program_id(1) == 0)
        def _():
            carry_ref[...] = jnp.zeros_like(carry_ref)
        c = carry_ref[...]
        prev1 = jnp.where(row == 0, c[7:8], um1)
        prev2 = jnp.where(row == 0, c[6:7], jnp.where(row == 1, c[7:8], um2))
        carry_ref[...] = u[tm - 8:tm]
        cs_ref[0] = u[tm - 8:tm]
    cw = cw_ref[...]
    y = cw[0:1] * prev2 + cw[1:2] * prev1 + cw[2:3] * u
    co_ref[...] = (bg * y).astype(BF16)

    bd = bd_ref[...]
    q = _head_rms(seg(c3, c3 + ATTN_DIM), bd, qn_ref[...]) * (HEAD_DIM ** -0.5)
    lane = lax.broadcasted_iota(jnp.int32, (tm, LANES), 1)
    low = lane < HEAD_DIM

    def head_planes(x, fill):
        return [jnp.where(low, x if h == 0 else pltpu.roll(x, HEAD_DIM, axis=1), fill) for h in range(N_KV_HEADS)]

    if sample:
        q_ref[...] = q
    else:
        for hd in range(N_HEADS):
            pair = q[:, (hd // 2) * LANES:(hd // 2 + 1) * LANES]
            if hd % 2:
                pair = pltpu.roll(pair, HEAD_DIM, axis=1)
            q_ref[0, hd] = jnp.where(low, pair, 0.0).astype(BF16)

    k0 = c3 + ATTN_DIM
    bdk = bd[:KV_DIM, :KV_DIM]
    kc_ref[...] = seg(k0, k0 + KV_DIM)
    vc_ref[...] = seg(k0 + KV_DIM, k0 + 2 * KV_DIM)
    ks = _head_rms(seg(k0 + 2 * KV_DIM, k0 + 3 * KV_DIM), bdk, kn1_ref[...])
    vs = seg(k0 + 3 * KV_DIM, k0 + 4 * KV_DIM)
    kw = _head_rms(seg(k0 + 4 * KV_DIM, k0 + 5 * KV_DIM), bdk, kn2_ref[...])
    vw = seg(k0 + 5 * KV_DIM, k0 + 6 * KV_DIM)
    ks_ref[...] = ks
    vs_ref[...] = vs
    kw_ref[...] = kw
    vw_ref[...] = vw
    gates = _sigmoid(seg(k0 + 6 * KV_DIM, k0 + 6 * KV_DIM + LANES))
    if sample:
        gt_ref[...] = gates
    else:
        pos = pl.program_id(1) * tm + lax.broadcasted_iota(jnp.int32, (tm, LANES), 0)
        block_onehot = jnp.where(lane - HEAD_DIM == (pos >> 6), 1.0, 0.0)
        for h, (a, b, c, e) in enumerate(zip(head_planes(ks, block_onehot), head_planes(vs, 1.0),
                                             head_planes(kw, 0.0), head_planes(vw, 1.0))):
            ksb_ref[h] = a.astype(BF16)
            vsb_ref[h] = b.astype(BF16)
            kwb_ref[h] = c.astype(BF16)
            vwb_ref[h] = e.astype(BF16)
        gt_ref[0] = gates
        gt_ref[1] = pltpu.roll(gates, LANES - GROUP * N_BRANCH, axis=1)


def _project(x2d, batch, seq, tm, weights, state=None):
    n, d = x2d.shape
    sample = state is not None
    nt = seq // tm if not sample else 1
    const = lambda shape: pl.BlockSpec(shape, lambda b, t: (0,) * len(shape))
    rows = lambda w: pl.BlockSpec((tm, w), lambda b, t: (b * nt + t, 0))
    nm, w_in, qn, kn1, kn2, cw, bd = weights
    in_specs = [rows(d), const(nm.shape), const(w_in.shape), const(qn.shape), const(kn1.shape),
                const(kn2.shape), const(cw.shape), const(bd.shape)]
    args = [x2d, nm, w_in, qn, kn1, kn2, cw, bd]
    kv_f32 = [jax.ShapeDtypeStruct((n, KV_DIM), F32)] * 6
    if sample:
        in_specs += [rows(CONV_DIM), rows(CONV_DIM)]
        args += list(state)
        out_shape = ([jax.ShapeDtypeStruct((n, CONV_DIM), BF16), jax.ShapeDtypeStruct((n, ATTN_DIM), F32)]
                     + kv_f32 + [jax.ShapeDtypeStruct((n, LANES), F32), jax.ShapeDtypeStruct((n, CONV_DIM), F32)])
        out_specs = [rows(CONV_DIM), rows(ATTN_DIM)] + [rows(KV_DIM)] * 6 + [rows(LANES), rows(CONV_DIM)]
        scratch = []
        grid = (1, 1)
    else:
        planes = pl.BlockSpec((N_KV_HEADS, tm, LANES), lambda b, t: (0, b * nt + t, 0))
        out_shape = ([jax.ShapeDtypeStruct((n, CONV_DIM), BF16),
                      jax.ShapeDtypeStruct((batch, N_HEADS, seq, LANES), BF16)]
                     + kv_f32 + [jax.ShapeDtypeStruct((N_KV_HEADS, n, LANES), F32),
                                 jax.ShapeDtypeStruct((batch, 8, CONV_DIM), F32)]
                     + [jax.ShapeDtypeStruct((N_KV_HEADS, n, LANES), BF16)] * 4)
        out_specs = ([rows(CONV_DIM), pl.BlockSpec((1, N_HEADS, tm, LANES), lambda b, t: (b, 0, t, 0))]
                     + [rows(KV_DIM)] * 6 + [planes, pl.BlockSpec((1, 8, CONV_DIM), lambda b, t: (b, 0, 0))]
                     + [planes] * 4)
        scratch = [pltpu.VMEM((8, CONV_DIM), F32)]
        grid = (batch, nt)
    return pl.pallas_call(
        functools.partial(_proj_body, sample, tm),
        grid=grid, in_specs=in_specs, out_specs=out_specs, out_shape=out_shape, scratch_shapes=scratch,
        compiler_params=_cparams("arbitrary", "arbitrary"),
        name="proj_sample" if sample else "proj_prompt",
    )(*args)


def _gelu_tanh(x):
    cdf = 0.5 * (1.0 + jnp.tanh(math.sqrt(2.0 / math.pi) * (x + 0.044715 * (x * x * x))))
    return x * cdf


def _compress_core(norm, nch, x, pe_ref, we_ref, w2_ref, g_ref, o_ref, per_head=False):
    a0 = _dot((x + pe_ref[0:1]).astype(BF16), we_ref[0])
    a1 = _dot((x + pe_ref[1:2]).astype(BF16), we_ref[1])
    hid = a0 + pltpu.roll(a1, nch - 1, axis=0)
    w2 = w2_ref[...]
    outs = []
    for h in range(N_KV_HEADS):
        act = _gelu_tanh(hid[:, h * CMP_HIDDEN:(h + 1) * CMP_HIDDEN])
        o = _dot(act.astype(BF16), w2)
        if norm:
            o = _rms(o, g_ref[...])
        outs.append(o)
    if per_head:
        for h in range(N_KV_HEADS):
            o_ref[0, h] = jnp.concatenate([outs[h], jnp.zeros_like(outs[h])], axis=-1).astype(BF16)
    else:
        o_ref[0] = jnp.concatenate(outs, axis=-1)


def _compress_rows_body(norm, nch, x_ref, pe_ref, we_ref, w2_ref, g_ref, o_ref):
    _compress_core(norm, nch, x_ref[0], pe_ref, we_ref, w2_ref, g_ref, o_ref, per_head=True)


def _compress_pages_body(norm, nch, n_pages, pt_ref, *refs):
    pages = refs[:n_pages]
    pe_ref, we_ref, w2_ref, g_ref, eye_ref, o_ref, rows_ref = refs[n_pages:]
    eye = eye_ref[...]
    page = pages[0].shape[2]
    for j, p in enumerate(pages):
        hi, mid, lo = _split3(p[0])
        rows_ref[j * page:(j + 1) * page, :] = _dot_nt(eye, hi) + _dot_nt(eye, mid) + _dot_nt(eye, lo)
    x = jnp.concatenate([rows_ref[pl.ds(r, nch, stride=CMP_STRIDE), :] for r in range(CMP_STRIDE)], axis=-1)
    _compress_core(norm, nch, x, pe_ref, we_ref, w2_ref, g_ref, o_ref)


def _compress_weights(pe, w1, w2):
    w1r = w1.reshape(2, CMP_STRIDE, HEAD_DIM, CMP_HIDDEN)
    eye = jnp.eye(N_KV_HEADS, dtype=w1.dtype)
    we = jnp.einsum("jrdc,hk->jrhdkc", w1r, eye).reshape(2, CMP_STRIDE * KV_DIM, N_KV_HEADS * CMP_HIDDEN)
    per = pe.reshape(2, CMP_STRIDE, 1, HEAD_DIM)
    pex = jnp.broadcast_to(per, (2, CMP_STRIDE, N_KV_HEADS, HEAD_DIM)).reshape(2, CMP_STRIDE * KV_DIM)
    return pex.astype(F32), we.astype(BF16), w2.astype(BF16)


def _compress_rows(rows3, cw, gain, norm):
    b, nch, width = rows3.shape
    pex, we, w2 = cw
    const = lambda a: pl.BlockSpec(a.shape, lambda i: (0,) * a.ndim)
    return pl.pallas_call(
        functools.partial(_compress_rows_body, norm, nch),
        grid=(b,),
        in_specs=[pl.BlockSpec((1, nch, width), lambda i: (i, 0, 0)), const(pex), const(we), const(w2), const(gain)],
        out_specs=pl.BlockSpec((1, N_KV_HEADS, nch, LANES), lambda i: (i, 0, 0, 0)),
        out_shape=jax.ShapeDtypeStruct((b, N_KV_HEADS, nch, LANES), BF16),
        compiler_params=_cparams("arbitrary"),
        name="compress_rows",
    )(rows3, pex, we, w2, gain)


def _compress_pages(cache_t, pt_flat, n_batch, n_pages, cw, gain, norm):
    _, _, page = cache_t.shape
    nch = n_pages * page // CMP_STRIDE
    pex, we, w2 = cw
    eye = jnp.asarray(np.eye(page), BF16)
    const = lambda a: pl.BlockSpec(a.shape, lambda i, pt: (0,) * a.ndim)
    page_spec = lambda j: pl.BlockSpec((1, KV_DIM, page), lambda i, pt: (pt[i * n_pages + j], 0, 0))
    grid_spec = pltpu.PrefetchScalarGridSpec(
        num_scalar_prefetch=1, grid=(n_batch,),
        in_specs=[page_spec(j) for j in range(n_pages)] + [const(pex), const(we), const(w2), const(gain), const(eye)],
        out_specs=pl.BlockSpec((1, nch, KV_DIM), lambda i, pt: (i, 0, 0)),
        scratch_shapes=[pltpu.VMEM((n_pages * page, KV_DIM), F32)])
    return pl.pallas_call(
        functools.partial(_compress_pages_body, norm, nch, n_pages),
        grid_spec=grid_spec,
        out_shape=jax.ShapeDtypeStruct((n_batch, nch, KV_DIM), F32),
        compiler_params=_cparams("arbitrary"),
        name="compress_pages",
    )(pt_flat, *([cache_t] * n_pages), pex, we, w2, gain, eye)


def _rel_bucket(dist):
    n = jnp.maximum(dist, 0)
    max_exact = NUM_BUCKETS // 2
    nf = jnp.maximum(n, 1).astype(F32)
    large = max_exact + (jnp.log(nf / max_exact) / math.log(MAX_DISTANCE / max_exact)
                         * (NUM_BUCKETS - max_exact)).astype(jnp.int32)
    large = jnp.minimum(large, NUM_BUCKETS - 1)
    return jnp.where(n < max_exact, n, large)


def _bias_by_distance(rel_bias):
    d = jnp.arange(BIAS_DMAX, dtype=jnp.int32)
    f = rel_bias.astype(F32)[_rel_bucket(d)]
    f = (f - f[BIAS_DMAX - 1:BIAS_DMAX]).T
    return jnp.concatenate([f, jnp.full((N_HEADS, 1), NEG_INF, F32)], axis=1)


def _bias_index(d, valid):
    return np.where(valid, np.clip(d, 0, BIAS_DMAX - 1), BIAS_DMAX).astype(np.int32)


def _bias_table(fext, d, valid):
    return jnp.take(fext, jnp.asarray(_bias_index(d, valid)), axis=1)


def _toeplitz(g, n_rows, k):
    h = g.shape[0]
    w = n_rows + k
    v = jnp.pad(g, ((0, 0), (0, w - g.shape[1])))
    t = jnp.tile(v, (1, n_rows + 1))[:, :n_rows * (w + 1)].reshape(h, n_rows, w + 1)[:, :, :k]
    return t[:, :, ::-1]


def _select_blocks(imp_t, srow, qpos, n_rank):
    qblk = qpos >> 6
    forced = (srow == 0) | (srow == qblk) | (srow == qblk - 1)
    valid = (srow << 6) <= qpos
    imp_t = jnp.where(valid, imp_t + jnp.where(forced, FORCE_SCORE, 0.0), NEG_INF)
    n_rows = imp_t.shape[0]
    assert n_rows % 8 == 0
    slabs = [imp_t[a:a + 8] for a in range(0, n_rows, 8)]
    rows8 = [srow[a:a + 8] for a in range(0, n_rows, 8)]
    cnts = [jnp.zeros(x.shape, jnp.int32) for x in slabs]
    for s in range(n_rank):
        r = imp_t[s:s + 1, :]
        for j, x in enumerate(slabs):
            if 8 * j > s:
                beats = r >= x
            elif 8 * j + 7 <= s:
                beats = r > x
            else:
                beats = (r > x) | ((r == x) & (rows8[j] > s))
            cnts[j] = cnts[j] + jnp.where(beats, 1, 0)
    cnt = jnp.concatenate(cnts, axis=0)
    return jnp.where((cnt < N_SEL) & valid, 1.0, 0.0)


def _exp_pv(s, m, v):
    return _dot(jnp.exp((s - m).astype(BF16)), v)


def _normalize_pv(pv):
    return pv / jnp.maximum(pv[:, HEAD_DIM:HEAD_DIM + 1], 1e-30)


def _attn_prompt_body(kt, n_slc, q_ref, gt_ref, kcmp_ref, vcmp_ref, ks_ref, vs_ref, kw_ref, vw_ref,
                      ctab_ref, ntab_ref, wtab_ref, ovl_ref, o_ref):
    i = pl.program_id(2)
    qb = Q_BLOCK
    rows = GROUP * qb
    n_cmp_pad = kcmp_ref.shape[2]
    first_near_block = 2 * jnp.maximum(i - 1, 0)
    near_start = pl.multiple_of(jnp.maximum(i - 1, 0) * qb, qb)
    win_start = pl.multiple_of(jnp.maximum(i * qb - WINDOW, 0), qb)
    band = WINDOW + qb
    gates = gt_ref[0]
    lane = lax.broadcasted_iota(jnp.int32, (qb, LANES), 1)
    q0 = q_ref[0].reshape(rows, LANES)

    s = _dot_nt(q0, kw_ref[0, pl.ds(win_start, band), :]) + wtab_ref[0].reshape(rows, band)
    m = jnp.max(s, axis=-1, keepdims=True)
    m = jnp.where(m == NEG_INF, 0.0, m)
    o_w = _normalize_pv(_exp_pv(s, m, vw_ref[0, pl.ds(win_start, band), :]))

    s = _dot_nt(q0, kcmp_ref[0, 0]) + ctab_ref[:, 0].reshape(rows, n_cmp_pad)
    p, l = _softmax_parts(s)
    pn = p / jnp.maximum(l, 1e-30)
    o_c = _dot(pn.astype(BF16), vcmp_ref[0, 0])

    psum = pn[0:qb] + pn[qb:2 * qb] + pn[2 * qb:3 * qb] + pn[3 * qb:4 * qb]
    hi, mid, lo = _split3(psum)
    ovl = ovl_ref[...]
    imp_t = _dot_nt(ovl, hi) + _dot_nt(ovl, mid) + _dot_nt(ovl, lo)
    srow = lax.broadcasted_iota(jnp.int32, (n_slc, qb), 0)
    qpos_t = i * qb + lax.broadcasted_iota(jnp.int32, (n_slc, qb), 1)
    sel_t = _select_blocks(imp_t, srow, qpos_t, n_slc)
    sel_t = jnp.concatenate([sel_t, jnp.zeros((LANES - n_slc, qb), F32)], axis=0)
    sel = sel_t.T

    def query_with_mask(keep):
        m = pltpu.roll(jnp.where(keep, 0.0, MASKED), HEAD_DIM, axis=1).astype(BF16)
        m = jnp.concatenate([jnp.where(lane < HEAD_DIM, q0[g * qb:(g + 1) * qb], m) for g in range(GROUP)], axis=0)
        return m

    q_near = query_with_mask(sel > 0.5)
    q_far = query_with_mask((sel > 0.5) & (lane < first_near_block))

    s = _dot_nt(q_near, ks_ref[0, pl.ds(near_start, 2 * qb), :]) + ntab_ref[0].reshape(rows, 2 * qb)
    m0 = jnp.max(s, axis=-1, keepdims=True)
    m0 = jnp.where(m0 == NEG_INF, 0.0, m0)
    a0 = _exp_pv(s, m0, vs_ref[0, pl.ds(near_start, 2 * qb), :])

    def far_tile(t, carry):
        m_old, acc = carry
        k0 = pl.multiple_of(t * kt, kt)
        s = _dot_nt(q_far, ks_ref[0, pl.ds(k0, kt), :])
        m_new = jnp.maximum(m_old, jnp.max(s, axis=-1, keepdims=True))
        acc = jnp.exp(m_old - m_new) * acc + _exp_pv(s, m_new, vs_ref[0, pl.ds(k0, kt), :])
        return m_new, acc

    n_far = (near_start + kt - 1) // kt
    _, acc_s = lax.fori_loop(0, n_far, far_tile, (m0, a0))
    o_s = _normalize_pv(acc_s)

    heads_out = []
    for g in range(GROUP):
        c = g * N_BRANCH
        sl = slice(g * qb, (g + 1) * qb)
        heads_out.append(gates[:, c:c + 1] * o_c[sl] + gates[:, c + 1:c + 2] * o_s[sl] + gates[:, c + 2:c + 3] * o_w[sl])
    tiles = [jnp.where(lane < HEAD_DIM, heads_out[2 * j], pltpu.roll(heads_out[2 * j + 1], HEAD_DIM, axis=1))
             for j in range(GROUP // 2)]
    o_ref[...] = jnp.concatenate(tiles, axis=-1).astype(BF16)


def _attn_prompt(qp, gates, kcmp, vcmp, ksb, vsb, kwb, vwb, fext, batch, seq):
    qb = Q_BLOCK
    nqb = seq // qb
    n_slc = seq // SLC_BLOCK
    assert n_slc <= LANES - HEAD_DIM
    n_cmp_pad = kcmp.shape[2]
    n_cmp = n_cmp_pad - 1
    kt = min(512, seq)
    band = WINDOW + qb
    iq = np.arange(qb)

    per_qb = qb // CMP_STRIDE
    half = 2 * per_qb
    m = np.arange(-half, half)
    d = iq[:, None] - (m[None, :] * CMP_STRIDE + CMP_BLOCK - 1)
    assert d[:, 0].min() >= MAX_DISTANCE and d[:, -1].max() < 0
    gwide = jnp.concatenate([jnp.zeros((N_HEADS, qb, n_cmp_pad - half), F32), _bias_table(fext, d, d >= 0),
                             jnp.full((N_HEADS, qb, n_cmp_pad - half), NEG_INF, F32)], axis=2)
    ctab = jnp.stack([gwide[:, :, n_cmp_pad - per_qb * i:2 * n_cmp_pad - per_qb * i] for i in range(nqb)], axis=1)
    assert (n_cmp_pad - 1) * CMP_STRIDE + CMP_BLOCK - 1 >= seq and n_cmp == n_cmp_pad - 1

    def toeplitz_variants(n_var, k, valid):
        j = np.arange(qb + k - 1)
        tabs = []
        for v in range(n_var):
            dj = qb * v + j - (k - 1)
            tabs.append(_toeplitz(_bias_table(fext, dj, valid(dj)), qb, k))
        return jnp.stack(tabs, axis=0)

    ntab = toeplitz_variants(2, 2 * qb, lambda dj: dj >= 0)
    nv = WINDOW // qb + 1
    wtab = toeplitz_variants(nv, band, lambda dj: (dj >= 0) & (dj < WINDOW))

    c_start = np.arange(n_cmp_pad) * CMP_STRIDE
    s_start = np.arange(n_slc) * SLC_BLOCK
    ovl = ((c_start[None, :] < s_start[:, None] + SLC_BLOCK) & (c_start[None, :] + CMP_BLOCK > s_start[:, None])
           & (np.arange(n_cmp_pad) < n_cmp)[None, :])
    ovl = jnp.asarray(ovl, BF16)

    per_head = lambda: pl.BlockSpec((1, seq, LANES), lambda b, h, i: (h, b, 0))
    return pl.pallas_call(
        functools.partial(_attn_prompt_body, kt, n_slc),
        grid=(batch, N_KV_HEADS, nqb),
        in_specs=[pl.BlockSpec((1, GROUP, qb, LANES), lambda b, h, i: (b, h, i, 0)),
                  pl.BlockSpec((1, qb, LANES), lambda b, h, i: (h, b * nqb + i, 0)),
                  pl.BlockSpec((1, 1, n_cmp_pad, LANES), lambda b, h, i: (b, h, 0, 0)),
                  pl.BlockSpec((1, 1, n_cmp_pad, LANES), lambda b, h, i: (b, h, 0, 0)),
                  per_head(), per_head(), per_head(), per_head(),
                  pl.BlockSpec((GROUP, 1, qb, n_cmp_pad), lambda b, h, i: (h, i, 0, 0)),
                  pl.BlockSpec((1, GROUP, qb, 2 * qb), lambda b, h, i: (jnp.minimum(i, 1), h, 0, 0)),
                  pl.BlockSpec((1, GROUP, qb, band), lambda b, h, i: (jnp.minimum(i, nv - 1), h, 0, 0)),
                  pl.BlockSpec(ovl.shape, lambda b, h, i: (0, 0))],
        out_specs=pl.BlockSpec((qb, GROUP * HEAD_DIM), lambda b, h, i: (b * nqb + i, h)),
        out_shape=jax.ShapeDtypeStruct((batch * seq, ATTN_DIM), BF16),
        compiler_params=_cparams("arbitrary", "arbitrary", "arbitrary"),
        name="attn_prompt",
    )(qp, gates, kcmp, vcmp, ksb, vsb, kwb, vwb, ctab, ntab, wtab, ovl)


def _attn_sample_body(n_pages, ts, past_len, pt_ref, *refs):
    kpages = refs[:n_pages]
    vpages = refs[n_pages:2 * n_pages]
    (q_ref, gt_ref, kcmp_ref, vcmp_ref, ksn_ref, vsn_ref, kwc_ref, vwc_ref, kwn_ref, vwn_ref,
     ctab_ref, stab_ref, sntab_ref, wtab_ref, wntab_ref, eexp_ref, ovl_ref, o_ref) = refs[2 * n_pages:]
    rows = GROUP * N_KV_HEADS * ts
    rq = N_KV_HEADS * ts
    q = q_ref[0]
    gates = gt_ref[0]

    s = _dot_nt(q, kcmp_ref[0].astype(BF16)) + ctab_ref[...]
    p, l = _softmax_parts(s)
    pn = p / jnp.maximum(l, 1e-30)
    o_c = _dot(pn.astype(BF16), vcmp_ref[0].astype(BF16))

    psum = pn[0:rq]
    for g in range(1, GROUP):
        psum = psum + pn[g * rq:(g + 1) * rq]
    hi, mid, lo = _split3(psum)
    ovl = ovl_ref[...]
    imp = _dot(hi, ovl) + _dot(mid, ovl) + _dot(lo, ovl)
    n_slc = -(-(past_len + ts) // SLC_BLOCK)
    blk = lax.broadcasted_iota(jnp.int32, (rq, LANES), 1)
    qpos = past_len + (lax.broadcasted_iota(jnp.int32, (rq, LANES), 0) & (ts - 1))
    qblk = qpos >> 6
    forced = (blk == 0) | (blk == qblk) | (blk == qblk - 1)
    valid = ((blk << 6) <= qpos) & (blk < n_slc)
    imp = jnp.where(valid, imp + jnp.where(forced, FORCE_SCORE, 0.0), NEG_INF)
    cnt = jnp.zeros((rq, LANES), jnp.int32)
    for sidx in range(n_slc):
        r = imp[:, sidx:sidx + 1]
        beats = (r > imp) | ((r == imp) & (blk > sidx))
        cnt = cnt + jnp.where(beats, 1, 0)
    sel = jnp.where((cnt < N_SEL) & valid, 1.0, 0.0)
    sel = jnp.concatenate([sel] * GROUP, axis=0).astype(BF16)

    kc_t = jnp.concatenate([p_[0] for p_ in kpages], axis=1).astype(BF16)
    vc_t = jnp.concatenate([p_[0] for p_ in vpages], axis=1).astype(BF16)
    mexp = _dot(sel, eexp_ref[...])
    s1 = jnp.where(mexp > 0.5, _dot(q, kc_t) + stab_ref[...], NEG_INF)
    last = sel[:, n_slc - 1:n_slc].astype(F32)
    s2 = jnp.where(last > 0.5, _dot_nt(q, ksn_ref[0].astype(BF16)) + sntab_ref[...], NEG_INF)
    m = jnp.maximum(jnp.max(s1, axis=-1, keepdims=True), jnp.max(s2, axis=-1, keepdims=True))
    m = jnp.where(m == NEG_INF, 0.0, m)
    p1 = jnp.exp(s1 - m)
    p2 = jnp.exp(s2 - m)
    l = jnp.sum(p1, axis=-1, keepdims=True) + jnp.sum(p2, axis=-1, keepdims=True)
    o_s = (_dot_nt(p1.astype(BF16), vc_t) + _dot(p2.astype(BF16), vsn_ref[0].astype(BF16))) / jnp.maximum(l, 1e-30)

    s1 = _dot(q, kwc_ref[0].astype(BF16)) + wtab_ref[...]
    s2 = _dot_nt(q, kwn_ref[0].astype(BF16)) + wntab_ref[...]
    m = jnp.maximum(jnp.max(s1, axis=-1, keepdims=True), jnp.max(s2, axis=-1, keepdims=True))
    m = jnp.where(m == NEG_INF, 0.0, m)
    p1 = jnp.exp(s1 - m)
    p2 = jnp.exp(s2 - m)
    l = jnp.sum(p1, axis=-1, keepdims=True) + jnp.sum(p2, axis=-1, keepdims=True)
    o_w = (_dot_nt(p1.astype(BF16), vwc_ref[0].astype(BF16))
           + _dot(p2.astype(BF16), vwn_ref[0].astype(BF16))) / jnp.maximum(l, 1e-30)

    o_ref[0] = gates[:, 0:1] * o_c + gates[:, 1:2] * o_s + gates[:, 2:3] * o_w


def _attn_sample(q_s, gates_s, kcmp, vcmp, ks_new, vs_new, kw_new, vw_new, cache_ks, cache_vs,
                 cache_kw, cache_vw, pt_flat, fext, n_batch, ts, n_pages, page):
    past_len = n_pages * page
    w_buf = cache_kw.shape[2]
    rows = GROUP * N_KV_HEADS * ts
    n_new = 8
    n_cmp_pad = kcmp.shape[1]
    n_cmp = n_cmp_pad - 1
    n_slc = -(-(past_len + ts) // SLC_BLOCK)

    q5 = q_s.reshape(n_batch, ts, N_KV_HEADS, GROUP, HEAD_DIM).transpose(0, 3, 2, 1, 4)
    eye = jnp.eye(N_KV_HEADS, dtype=q_s.dtype)
    qr = jnp.einsum("bghtd,hk->bghtkd", q5, eye).reshape(n_batch, rows, LANES).astype(BF16)
    g5 = gates_s[:, :N_HEADS * N_BRANCH].reshape(n_batch, ts, N_KV_HEADS, GROUP, N_BRANCH).transpose(0, 3, 2, 1, 4)
    gr = jnp.pad(g5.reshape(n_batch, rows, N_BRANCH), ((0, 0), (0, 0), (0, LANES - N_BRANCH)))
    pad_new = lambda a: jnp.pad(a.reshape(n_batch, ts, KV_DIM), ((0, 0), (0, n_new - ts), (0, 0)))
    ks_new, vs_new, kw_new, vw_new = map(pad_new, (ks_new, vs_new, kw_new, vw_new))

    g_i, h_i, t_i = np.meshgrid(np.arange(GROUP), np.arange(N_KV_HEADS), np.arange(ts), indexing="ij")
    head = (h_i * GROUP + g_i).reshape(rows)
    tq = t_i.reshape(rows)
    pos_q = past_len + tq

    f_rows = fext[jnp.asarray(head)]

    def table(d, valid):
        return jnp.take_along_axis(f_rows, jnp.asarray(_bias_index(d, valid)), axis=1)

    nn = np.arange(n_cmp_pad)
    d = pos_q[:, None] - (nn[None, :] * CMP_STRIDE + CMP_BLOCK - 1)
    ctab = table(d, (d >= 0) & (nn < n_cmp)[None, :])
    near = np.arange(past_len - MAX_DISTANCE, past_len)
    d = pos_q[:, None] - near[None, :]
    assert past_len >= MAX_DISTANCE and d.min() >= 0
    stab = jnp.concatenate([jnp.zeros((rows, past_len - MAX_DISTANCE), F32), table(d, d >= 0)], axis=1)
    jn = np.arange(n_new)
    d = tq[:, None] - jn[None, :]
    sntab = table(d, (d >= 0) & (jn < ts)[None, :])
    pos_w = past_len - w_buf + np.arange(w_buf)
    d = pos_q[:, None] - pos_w[None, :]
    wtab = table(d, (d >= 0) & (d < WINDOW) & (pos_w >= 0)[None, :])
    wntab = table(tq[:, None] - jn[None, :], (tq[:, None] >= jn[None, :]) & (jn < ts)[None, :])

    eexp = jnp.asarray(np.arange(past_len)[None, :] // SLC_BLOCK == np.arange(LANES)[:, None], BF16)
    c_start = nn * CMP_STRIDE
    s_start = np.arange(LANES) * SLC_BLOCK
    ovl = jnp.asarray((c_start[:, None] < s_start[None, :] + SLC_BLOCK) & (c_start[:, None] + CMP_BLOCK > s_start[None, :])
                      & (nn < n_cmp)[:, None] & (np.arange(LANES) < n_slc)[None, :], BF16)

    const = lambda a: pl.BlockSpec(a.shape, lambda b, pt: (0,) * a.ndim)
    per_b = lambda a: pl.BlockSpec((1,) + a.shape[1:], lambda b, pt: (b,) + (0,) * (a.ndim - 1))
    page_spec = lambda j: pl.BlockSpec((1, KV_DIM, page), lambda b, pt: (pt[b * n_pages + j], 0, 0))
    small = [qr, gr, kcmp, vcmp, ks_new, vs_new, cache_kw, cache_vw, kw_new, vw_new]
    consts = [ctab, stab, sntab, wtab, wntab, eexp, ovl]
    grid_spec = pltpu.PrefetchScalarGridSpec(
        num_scalar_prefetch=1, grid=(n_batch,),
        in_specs=[page_spec(j) for j in range(n_pages)] * 2 + [per_b(a) for a in small] + [const(a) for a in consts],
        out_specs=pl.BlockSpec((1, rows, LANES), lambda b, pt: (b, 0, 0)))
    o = pl.pallas_call(
        functools.partial(_attn_sample_body, n_pages, ts, past_len),
        grid_spec=grid_spec,
        out_shape=jax.ShapeDtypeStruct((n_batch, rows, LANES), F32),
        compiler_params=_cparams("arbitrary"),
        name="attn_sample",
    )(pt_flat, *([cache_ks] * n_pages), *([cache_vs] * n_pages), *small, *consts)
    o6 = o.reshape(n_batch, GROUP, N_KV_HEADS, ts, N_KV_HEADS, HEAD_DIM)
    o5 = jnp.stack([o6[:, :, h, :, h] for h in range(N_KV_HEADS)], axis=2)
    return o5.transpose(0, 3, 2, 1, 4).reshape(n_batch * ts, ATTN_DIM).astype(BF16)


def _post1_body(tm, x_ref, co_ref, at_ref, wo_ref, nf_ref, wr_ref, br_ref, tri_ref, run0_ref,
                h_ref, hn_ref, rt_ref, cnt_ref, run_ref):
    @pl.when(pl.program_id(0) == 0)
    def _():
        run_ref[...] = run0_ref[...]

    h = x_ref[...] + _dot(co_ref[...], wo_ref[0:CONV_DIM]) + _dot(at_ref[...], wo_ref[CONV_DIM:CONV_DIM + ATTN_DIM])
    hn = _rms(h, nf_ref[...])
    h_ref[...] = h
    _store_token_tiles(hn_ref, hn, tm)

    hi = hn.astype(BF16)
    lo = (hn - hi.astype(F32)).astype(BF16)
    wr = wr_ref[...]
    whi = wr.astype(BF16)
    wlo = (wr - whi.astype(F32)).astype(BF16)
    logits = _dot(hi, whi) + _dot(lo, whi) + _dot(hi, wlo) + br_ref[...]

    lane_i = lax.broadcasted_iota(jnp.int32, (tm, LANES), 1)
    lane = lane_i.astype(F32)
    big = float(LANES)
    gmask = (lane_i >= ROUTER_GROUP_LANE) & (lane_i < ROUTER_GROUP_LANE + N_GROUPS)
    lg = jnp.where(gmask, logits, NEG_INF)
    eg = jnp.exp(lg - jnp.max(lg, axis=-1, keepdims=True))
    pg = eg / jnp.sum(eg, axis=-1, keepdims=True)
    gw = jnp.max(pg, axis=-1, keepdims=True)
    grp = jnp.min(jnp.where(gmask & (pg == gw), lane, big), axis=-1, keepdims=True) - ROUTER_GROUP_LANE

    group_of_lane = (lane_i >> 3).astype(F32)
    emask = (lane_i < N_EXPERTS) & (group_of_lane == grp)
    le = jnp.where(emask, logits, NEG_INF)
    ee = jnp.exp(le - jnp.max(le, axis=-1, keepdims=True))
    pe = jnp.where(emask, ee / jnp.sum(ee, axis=-1, keepdims=True), -1.0)
    v1 = jnp.max(pe, axis=-1, keepdims=True)
    i1 = jnp.min(jnp.where(pe == v1, lane, big), axis=-1, keepdims=True)
    pe2 = jnp.where(lane == i1, -1.0, pe)
    v2 = jnp.max(pe2, axis=-1, keepdims=True)
    i2 = jnp.min(jnp.where(pe2 == v2, lane, big), axis=-1, keepdims=True)
    tot = v1 + v2
    w1 = v1 / tot * gw
    w2 = v2 / tot * gw

    oh1 = jnp.where(lane == i1, 1.0, 0.0)
    oh2 = jnp.where(lane == i2, 1.0, 0.0)
    both = oh1 + oh2
    before = _dot(tri_ref[...], both.astype(BF16)) + run_ref[0:1]
    r1 = jnp.sum(oh1 * before, axis=-1, keepdims=True)
    r2 = jnp.sum(oh2 * before, axis=-1, keepdims=True)
    run = run_ref[0:1] + jnp.sum(both, axis=0, keepdims=True)
    run_ref[...] = jnp.broadcast_to(run, run_ref.shape)
    cnt_ref[...] = jnp.broadcast_to(run, cnt_ref.shape)

    rt = jnp.where(lane_i == 0, i1, 0.0)
    rt = jnp.where(lane_i == 1, i2, rt)
    rt = jnp.where(lane_i == 2, r1, rt)
    rt = jnp.where(lane_i == 3, r2, rt)
    rt = jnp.where(lane_i == 4, w1, rt)
    rt = jnp.where(lane_i == 5, w2, rt)
    rt_ref[...] = rt


def _post1(x2d, co, at, wo, nf, wr, br, run0, tm):
    n, d = x2d.shape
    tri = jnp.asarray(np.tril(np.ones((tm, tm), np.float32), -1), BF16)
    rows = lambda w: pl.BlockSpec((tm, w), lambda i: (i, 0))
    const = lambda a: pl.BlockSpec(a.shape, lambda i: (0,) * a.ndim)
    return pl.pallas_call(
        functools.partial(_post1_body, tm),
        grid=(n // tm,),
        in_specs=[rows(d), rows(CONV_DIM), rows(ATTN_DIM), const(wo), const(nf), const(wr), const(br),
                  const(tri), const(run0)],
        out_specs=[rows(d), pl.BlockSpec((tm * TOKEN_TILE_ROWS, LANES), lambda i: (i, 0)), rows(LANES),
                   pl.BlockSpec((8, LANES), lambda i: (0, 0))],
        out_shape=[jax.ShapeDtypeStruct((n, d), F32), jax.ShapeDtypeStruct((n * TOKEN_TILE_ROWS, LANES), F32),
                   jax.ShapeDtypeStruct((n, LANES), F32), jax.ShapeDtypeStruct((8, LANES), F32)],
        scratch_shapes=[pltpu.VMEM((8, LANES), F32)],
        compiler_params=_cparams("arbitrary"),
        name="post1",
    )(x2d, co, at, wo, nf, wr, br, tri, run0)


def _token_copy(src_ref, dst_ref, s, d, sem):
    r = TOKEN_TILE_ROWS
    return pltpu.make_async_copy(src_ref.at[pl.ds(pl.multiple_of(s * r, r), r)],
                                 dst_ref.at[pl.ds(pl.multiple_of(d * r, r), r)], sem)


def _scatter_rows_body(ts, dest_ref, src_ref, init_ref, out_ref, sem):
    del init_ref
    base = pl.program_id(0) * (2 * ts)

    def issue(t, _):
        for k in range(2):
            _token_copy(src_ref, out_ref, t, dest_ref[base + 2 * t + k], sem).start(priority=k)
        return 0

    lax.fori_loop(0, ts, issue, 0)
    for _ in range(2):
        pltpu.make_async_copy(src_ref, out_ref.at[pl.ds(0, ts * TOKEN_TILE_ROWS)], sem).wait()


def _scatter_rows(dest, src, slots):
    n_tok = dest.shape[0] // 2
    ts = min(SCATTER_TOKENS, n_tok)
    assert n_tok % ts == 0
    any_spec = pl.BlockSpec(memory_space=pl.ANY)
    return pl.pallas_call(
        functools.partial(_scatter_rows_body, ts),
        grid_spec=pltpu.PrefetchScalarGridSpec(
            num_scalar_prefetch=1, grid=(n_tok // ts,),
            in_specs=[pl.BlockSpec((ts * TOKEN_TILE_ROWS, LANES), lambda i, dest: (i, 0)), any_spec],
            out_specs=any_spec, scratch_shapes=[pltpu.SemaphoreType.DMA(())]),
        out_shape=jax.ShapeDtypeStruct(slots.shape, slots.dtype),
        input_output_aliases={2: 0},
        compiler_params=pltpu.CompilerParams(dimension_semantics=("arbitrary",)),
        name="scatter_rows",
    )(dest, src, slots)


def _experts_body(be_ref, nu_ref, x_ref, wg_ref, wu_ref, wd_ref, o_ref, wg_s, wu_s, wd_s):
    i = pl.program_id(0)

    @pl.when(i < nu_ref[0])
    def _():
        prev = be_ref[jnp.maximum(i - 1, 0)]

        @pl.when((i == 0) | (be_ref[i] != prev))
        def _():
            wg_s[...] = wg_ref[0].astype(BF16)
            wu_s[...] = wu_ref[0].astype(BF16)
            wd_s[...] = wd_ref[0].astype(BF16)

        x = _load_token_tiles(x_ref, EXPERT_ROWS, TOKEN_TILE_ROWS).astype(BF16)
        g = _dot(x, wg_s[...])
        u = _dot(x, wu_s[...])
        a = g * _sigmoid(g) * u
        _store_token_tiles(o_ref, _dot(a.astype(BF16), wd_s[...]), EXPERT_ROWS)

    @pl.when(i >= nu_ref[0])
    def _():
        o_ref[...] = jnp.zeros_like(o_ref)


def _experts(blk_expert, n_used, xs, wg, wu, wd):
    blk_rows = EXPERT_ROWS * TOKEN_TILE_ROWS
    n_blk = xs.shape[0] // blk_rows
    _, d, de = wg.shape
    xmap = lambda i, be, nu: (jnp.minimum(i, jnp.maximum(nu[0] - 1, 0)), 0)
    wmap = lambda i, be, nu: (be[jnp.minimum(i, jnp.maximum(nu[0] - 1, 0))], 0, 0)
    grid_spec = pltpu.PrefetchScalarGridSpec(
        num_scalar_prefetch=2, grid=(n_blk,),
        in_specs=[pl.BlockSpec((blk_rows, LANES), xmap), pl.BlockSpec((1, d, de), wmap),
                  pl.BlockSpec((1, d, de), wmap), pl.BlockSpec((1, de, d), wmap)],
        out_specs=pl.BlockSpec((blk_rows, LANES), lambda i, be, nu: (i, 0)),
        scratch_shapes=[pltpu.VMEM((d, de), BF16), pltpu.VMEM((d, de), BF16), pltpu.VMEM((de, d), BF16)])
    return pl.pallas_call(
        _experts_body, grid_spec=grid_spec,
        out_shape=jax.ShapeDtypeStruct(xs.shape, F32),
        compiler_params=_cparams("arbitrary"),
        name="experts",
    )(blk_expert, n_used, xs, wg, wu, wd)


def _post2_body(tm, dest_ref, h_ref, rt_ref, p_ref, yb_ref, wple_ref, wpg_ref, bpg_ref, np_ref, o_ref, buf, sem):
    i = pl.program_id(0)
    n = pl.num_programs(0)

    def fetch(step, slot):
        base = step * (2 * tm)

        def issue(t, _):
            for k in range(2):
                _token_copy(yb_ref, buf.at[slot], dest_ref[base + 2 * t + k], k * tm + t, sem.at[slot]).start(priority=k)
            return 0

        lax.fori_loop(0, tm, issue, 0)

    @pl.when(i == 0)
    def _():
        fetch(0, 0)

    @pl.when(i + 1 < n)
    def _():
        fetch(i + 1, (i + 1) & 1)

    slot = i & 1

    pltpu.make_async_copy(yb_ref.at[pl.ds(0, 2 * tm * TOKEN_TILE_ROWS)], buf.at[slot], sem.at[slot]).wait()
    rt = rt_ref[...]
    y0 = _load_token_tiles(buf.at[slot], tm, TOKEN_TILE_ROWS)
    y1 = _load_token_tiles(buf.at[slot], tm, TOKEN_TILE_ROWS, first=tm * TOKEN_TILE_ROWS)
    h = h_ref[...] + (y0 * rt[:, 4:5] + y1 * rt[:, 5:6])
    gate = _sigmoid(_dot(_rms(h, np_ref[...]).astype(BF16), wpg_ref[...]) + bpg_ref[...])
    o_ref[...] = h + gate * _dot(p_ref[...].astype(BF16), wple_ref[...])


def _post2(dest, h, rt, p2d, yb, wple, wpg, bpg, npl, tm):
    n, d = h.shape
    rows = lambda w: pl.BlockSpec((tm, w), lambda i, dest: (i, 0))
    const = lambda a: pl.BlockSpec(a.shape, lambda i, dest: (0,) * a.ndim)
    grid_spec = pltpu.PrefetchScalarGridSpec(
        num_scalar_prefetch=1, grid=(n // tm,),
        in_specs=[rows(d), rows(LANES), rows(p2d.shape[1]), pl.BlockSpec(memory_space=pl.ANY), const(wple),
                  const(wpg), const(bpg), const(npl)],
        out_specs=rows(d),
        scratch_shapes=[pltpu.VMEM((2, 2 * tm * TOKEN_TILE_ROWS, LANES), F32), pltpu.SemaphoreType.DMA((2,))])
    return pl.pallas_call(
        functools.partial(_post2_body, tm),
        grid_spec=grid_spec,
        out_shape=jax.ShapeDtypeStruct((n, d), F32),
        compiler_params=_cparams("arbitrary"),
        name="post2",
    )(dest, h, rt, p2d, yb, wple, wpg, bpg, npl)


def _row_tile(n, cap=512):
    t = min(cap, n)
    assert n % t == 0 and t % 8 == 0
    return t


def kernel(x_prompt, x_sample, p_prompt, p_sample, cache_k_cmp, cache_v_cmp, cache_k_slc, cache_v_slc, cache_k_win, cache_v_win, state_conv, page_table, w_in, w_out, conv_w, norm_mix, norm_ffn, norm_ple, q_norm, k_norm, cmp_pe_k, cmp_w1_k, cmp_w2_k, cmp_pe_v, cmp_w1_v, cmp_w2_v, rel_bias, w_router_group, b_router_group, w_router_expert, b_router_expert, w_exp_gate, w_exp_up, w_exp_down, w_ple, w_ple_gate, b_ple_gate):
    assert w_in.shape[0] == 1, "single-layer step"
    bp, t, d = x_prompt.shape
    bs, ts, _ = x_sample.shape
    n_pages = page_table.shape[1]
    page = cache_k_cmp.shape[2]
    past_len = n_pages * page
    w_buf = cache_k_win.shape[2]
    n_phys = cache_k_cmp.shape[1]
    assert t % Q_BLOCK == 0 and t >= WINDOW + Q_BLOCK and page % CMP_STRIDE == 0 and ts == 4 and d == D_MODEL
    assert past_len % SLC_BLOCK == 0
    np_rows, ns_rows = bp * t, bs * ts

    row = lambda v: v.reshape(1, -1).astype(F32)
    w_in_b = jnp.pad(w_in[0], ((0, 0), (0, Z_COLS - w_in.shape[2]))).astype(BF16)
    qn = row(jnp.tile(q_norm[0], N_HEADS))
    kn1 = row(jnp.tile(k_norm[0, 1], N_KV_HEADS))
    kn2 = row(jnp.tile(k_norm[0, 2], N_KV_HEADS))
    bd = jnp.asarray(np.kron(np.eye(N_HEADS), np.ones((HEAD_DIM, HEAD_DIM))), BF16)
    pw = (row(norm_mix[0]), w_in_b, qn, kn1, kn2, conv_w[0].astype(F32), bd)
    cw_k = _compress_weights(cmp_pe_k[0], cmp_w1_k[0], cmp_w2_k[0])
    cw_v = _compress_weights(cmp_pe_v[0], cmp_w1_v[0], cmp_w2_v[0])
    kn0 = row(k_norm[0, 0])
    fext = _bias_by_distance(rel_bias)
    pt_flat = page_table.reshape(-1).astype(jnp.int32)
    wr = jnp.zeros((d, LANES), F32).at[:, :N_EXPERTS].set(w_router_expert[0])
    wr = wr.at[:, ROUTER_GROUP_LANE:ROUTER_GROUP_LANE + N_GROUPS].set(w_router_group[0])
    br = jnp.zeros((1, LANES), F32).at[0, :N_EXPERTS].set(b_router_expert[0])
    br = br.at[0, ROUTER_GROUP_LANE:ROUTER_GROUP_LANE + N_GROUPS].set(b_router_group[0])
    wo_b = w_out[0].astype(BF16)
    wple_b = w_ple[0].astype(BF16)
    wpg_b = w_ple_gate[0].astype(BF16)

    tm_p = _row_tile(t)
    (co_p, q_p, kc_p, vc_p, ks_p, vs_p, kw_p, vw_p, gt_p, cs_p, ksb, vsb, kwb, vwb) = _project(
        x_prompt.reshape(np_rows, d), bp, t, tm_p, pw)
    chunk_w = CMP_STRIDE * KV_DIM
    kcmp_p = _compress_rows(kc_p.reshape(bp, t // CMP_STRIDE, chunk_w), cw_k, kn0, True)
    vcmp_p = _compress_rows(vc_p.reshape(bp, t // CMP_STRIDE, chunk_w), cw_v, kn0, False)
    at_p = _attn_prompt(q_p, gt_p, kcmp_p, vcmp_p, ksb, vsb, kwb, vwb, fext, bp, t)

    st = state_conv[0].astype(F32)
    s0 = jnp.repeat(st[:, 0], ts, axis=0)
    s1 = jnp.repeat(st[:, 1], ts, axis=0)
    (co_s, q_s, kc_s, vc_s, ks_s, vs_s, kw_s, vw_s, gt_s, u_s) = _project(
        x_sample.reshape(ns_rows, d), bs, ts, ns_rows, pw, state=(s0, s1))
    feature_major = lambda c, n, rows_: jnp.transpose(c[0], (0, 2, 3, 1)).reshape(n, KV_DIM, rows_)
    kcmp_s = _compress_pages(feature_major(cache_k_cmp, n_phys, page), pt_flat, bs, n_pages, cw_k, kn0, True)
    vcmp_s = _compress_pages(feature_major(cache_v_cmp, n_phys, page), pt_flat, bs, n_pages, cw_v, kn0, False)
    at_s = _attn_sample(q_s, gt_s, kcmp_s, vcmp_s, ks_s, vs_s, kw_s, vw_s,
                        feature_major(cache_k_slc, n_phys, page), feature_major(cache_v_slc, n_phys, page),
                        feature_major(cache_k_win, bs, w_buf), feature_major(cache_v_win, bs, w_buf),
                        pt_flat, fext, bs, ts, n_pages, page)

    tp1 = _row_tile(np_rows)
    ts1 = _row_tile(ns_rows)
    nf = row(norm_ffn[0])
    h_p, hn_p, rt_p, cnt_p = _post1(x_prompt.reshape(np_rows, d), co_p, at_p, wo_b, nf, wr, br,
                                    jnp.zeros((8, LANES), F32), tp1)
    h_s, hn_s, rt_s, cnt_s = _post1(x_sample.reshape(ns_rows, d), co_s, at_s, wo_b, nf, wr, br, cnt_p, ts1)

    counts = cnt_s[0, :N_EXPERTS].astype(jnp.int32)
    padded = (counts + EXPERT_ROWS - 1) // EXPERT_ROWS * EXPERT_ROWS
    pad_end = jnp.cumsum(padded)
    pad_start = pad_end - padded
    n_assign = 2 * (np_rows + ns_rows)
    n_blk = (n_assign + N_EXPERTS * (EXPERT_ROWS - 1) + EXPERT_ROWS - 1) // EXPERT_ROWS
    blk_first = jnp.arange(n_blk, dtype=jnp.int32) * EXPERT_ROWS
    blk_expert = jnp.minimum(jnp.sum((pad_end[None, :] <= blk_first[:, None]).astype(jnp.int32), axis=1),
                             N_EXPERTS - 1)
    n_used = (pad_end[-1:] // EXPERT_ROWS).astype(jnp.int32)

    def dest_of(rt):
        e = rt[:, 0:2].astype(jnp.int32).reshape(-1)
        return pad_start[e] + rt[:, 2:4].astype(jnp.int32).reshape(-1)

    dest_p = dest_of(rt_p)
    dest_s = dest_of(rt_s)

    xs = jnp.zeros((n_blk * EXPERT_ROWS * TOKEN_TILE_ROWS, LANES), F32)
    xs = _scatter_rows(dest_p, hn_p, xs)
    xs = _scatter_rows(dest_s, hn_s, xs)
    yb = _experts(blk_expert, n_used, xs, w_exp_gate[0], w_exp_up[0], w_exp_down[0])

    bpg = row(b_ple_gate[0])
    npl = row(norm_ple[0])
    y_p = _post2(dest_p, h_p, rt_p, p_prompt[0].reshape(np_rows, -1), yb, wple_b, wpg_b, bpg, npl, tp1)
    y_s = _post2(dest_s, h_s, rt_s, p_sample[0].reshape(ns_rows, -1), yb, wple_b, wpg_b, bpg, npl, ts1)

    kv5 = lambda a, b, s: a.reshape(1, b, s, N_KV_HEADS, HEAD_DIM)
    wp = min(WINDOW, t)
    win_p = lambda a: kv5(a, bp, t)[:, :, t - wp:]
    win_s = lambda c, new: jnp.concatenate([c[0], new.reshape(bs, ts, N_KV_HEADS, HEAD_DIM)], axis=1)[None, :, ts:]
    conv_p = cs_p[:, 8 - (CONV_K - 1):][None]
    conv_s = u_s.reshape(bs, ts, CONV_DIM)[:, ts - (CONV_K - 1):][None]
    return (y_p.reshape(bp, t, d), y_s.reshape(bs, ts, d),
            kv5(kc_p, bp, t), kv5(vc_p, bp, t), kv5(ks_p, bp, t), kv5(vs_p, bp, t), win_p(kw_p), win_p(vw_p), conv_p,
            kv5(kc_s, bs, ts), kv5(vc_s, bs, ts), kv5(ks_s, bs, ts), kv5(vs_s, bs, ts),
            win_s(cache_k_win, kw_s), win_s(cache_v_win, vw_s), conv_s)
```

```python
import functools
import math

import numpy as np
import jax
import jax.numpy as jnp
from jax import lax
from jax.experimental import pallas as pl
from jax.experimental.pallas import tpu as pltpu

F32 = jnp.float32
BF16 = jnp.bfloat16
NEG_INF = float("-inf")
MASKED = -1e30

HEAD_DIM = 64
N_HEADS = 8
N_KV_HEADS = 2
GROUP = N_HEADS // N_KV_HEADS
CONV_DIM = 512
ATTN_DIM = 512
KV_DIM = N_KV_HEADS * HEAD_DIM
N_BRANCH = 3
CONV_K = 3
CMP_BLOCK = 32
CMP_STRIDE = 16
CMP_HIDDEN = 256
SLC_BLOCK = 64
N_SEL = 16
WINDOW = 512
Q_BLOCK = 128
FORCE_SCORE = 1e4
NUM_BUCKETS = 32
MAX_DISTANCE = 128
N_GROUPS = 4
EXPERTS_PER_GROUP = 8
N_EXPERTS = N_GROUPS * EXPERTS_PER_GROUP
D_EXPERT = 512
EPS = 1e-6

D_MODEL = 1024
LANES = 128
TOKEN_TILE_ROWS = D_MODEL // LANES
Z_COLS = 3 * CONV_DIM + ATTN_DIM + 6 * KV_DIM + LANES
BIAS_DMAX = 768
EXPERT_ROWS = 512
ROUTER_GROUP_LANE = 32
SCATTER_TOKENS = 256
COMPRESS_SEQS = 4
VMEM_LIMIT = 56 * 1024 * 1024


def _cparams(*sem):
    return pltpu.CompilerParams(dimension_semantics=sem, vmem_limit_bytes=VMEM_LIMIT)


def _dot(a, b):
    return jnp.dot(a, b, preferred_element_type=F32)


def _dot_nt(a, b):
    return lax.dot_general(a, b, (((1,), (1,)), ((), ())), preferred_element_type=F32)


def _split3(x):
    hi = x.astype(BF16)
    r = x - hi.astype(F32)
    mid = r.astype(BF16)
    lo = (r - mid.astype(F32)).astype(BF16)
    return hi, mid, lo


def _rms(x, g):
    return x * lax.rsqrt(jnp.mean(x * x, axis=-1, keepdims=True) + EPS) * g


def _head_rms(x, bd, g):
    hi, mid, lo = _split3(x * x)
    ss = _dot(hi, bd) + _dot(mid, bd) + _dot(lo, bd)
    return x * lax.rsqrt(ss * (1.0 / HEAD_DIM) + EPS) * g


def _sigmoid(x):
    return 1.0 / (1.0 + jnp.exp(-x))


def _store_token_tiles(ref, x, n):
    r = x.shape[1] // LANES
    for j in range(r):
        ref[pl.ds(j, n, stride=r), :] = x[:, j * LANES:(j + 1) * LANES]


def _load_token_tiles(ref, n, r, first=0):
    return jnp.concatenate([ref[pl.ds(first + j, n, stride=r), :] for j in range(r)], axis=-1)


def _softmax_parts(s):
    m = jnp.max(s, axis=-1, keepdims=True)
    m = jnp.where(m == NEG_INF, 0.0, m)
    p = jnp.exp(s - m)
    l = jnp.sum(p, axis=-1, keepdims=True)
    return p, l


def _proj_body(sample, tm, *refs):
    if sample:
        (x_ref, nm_ref, w_ref, qn_ref, kn1_ref, kn2_ref, cw_ref, bd_ref, s0_ref, s1_ref,
         co_ref, q_ref, kc_ref, vc_ref, ks_ref, vs_ref, kw_ref, vw_ref, gt_ref, u_ref) = refs
    else:
        (x_ref, nm_ref, w_ref, qn_ref, kn1_ref, kn2_ref, cw_ref, bd_ref,
         co_ref, q_ref, kc_ref, vc_ref, ks_ref, vs_ref, kw_ref, vw_ref, gt_ref, cs_ref,
         ksb_ref, vsb_ref, kwb_ref, vwb_ref, kct_ref, vct_ref, kst_ref, vst_ref, carry_ref) = refs

    xn = _rms(x_ref[...], nm_ref[...]).astype(BF16)

    def seg(a, b):
        return _dot(xn, w_ref[:, a:b])

    c3 = 3 * CONV_DIM
    u = seg(2 * CONV_DIM, c3) * seg(0, CONV_DIM)
    bg = seg(CONV_DIM, 2 * CONV_DIM)
    row = lax.broadcasted_iota(jnp.int32, (tm, 1), 0)
    um1 = pltpu.roll(u, 1, axis=0)
    um2 = pltpu.roll(u, 2, axis=0)
    if sample:
        r = row & 3
        s0 = s0_ref[...]
        s1 = s1_ref[...]
        prev1 = jnp.where(r == 0, s1, um1)
        prev2 = jnp.where(r == 0, s0, jnp.where(r == 1, s1, um2))
        u_ref[...] = u
    else:
        @pl.when(pl.program_id(1) == 0)
        def _():
            carry_ref[...] = jnp.zeros_like(carry_ref)
        c = carry_ref[...]
        prev1 = jnp.where(row == 0, c[7:8], um1)
        prev2 = jnp.where(row == 0, c[6:7], jnp.where(row == 1, c[7:8], um2))
        carry_ref[...] = u[tm - 8:tm]
        cs_ref[0] = u[tm - 8:tm]
    cw = cw_ref[...]
    y = cw[0:1] * prev2 + cw[1:2] * prev1 + cw[2:3] * u
    co_ref[...] = (bg * y).astype(BF16)

    bd = bd_ref[...]
    q = _head_rms(seg(c3, c3 + ATTN_DIM), bd, qn_ref[...]) * (HEAD_DIM ** -0.5)
    lane = lax.broadcasted_iota(jnp.int32, (tm, LANES), 1)
    low = lane < HEAD_DIM

    def head_planes(x, fill):
        return [jnp.where(low, x if h == 0 else pltpu.roll(x, HEAD_DIM, axis=1), fill) for h in range(N_KV_HEADS)]

    if sample:
        q_ref[...] = q
    else:
        for hd in range(N_HEADS):
            pair = q[:, (hd // 2) * LANES:(hd // 2 + 1) * LANES]
            if hd % 2:
                pair = pltpu.roll(pair, HEAD_DIM, axis=1)
            q_ref[0, hd] = jnp.where(low, pair, 0.0).astype(BF16)

    k0 = c3 + ATTN_DIM
    bdk = bd[:KV_DIM, :KV_DIM]
    kc = seg(k0, k0 + KV_DIM)
    vc = seg(k0 + KV_DIM, k0 + 2 * KV_DIM)
    ks = _head_rms(seg(k0 + 2 * KV_DIM, k0 + 3 * KV_DIM), bdk, kn1_ref[...])
    vs = seg(k0 + 3 * KV_DIM, k0 + 4 * KV_DIM)
    kw = _head_rms(seg(k0 + 4 * KV_DIM, k0 + 5 * KV_DIM), bdk, kn2_ref[...])
    vw = seg(k0 + 5 * KV_DIM, k0 + 6 * KV_DIM)
    kc_ref[...] = kc
    vc_ref[...] = vc
    ks_ref[...] = ks
    vs_ref[...] = vs
    kw_ref[...] = kw
    vw_ref[...] = vw
    if not sample:
        kct_ref[0] = kc.T
        vct_ref[0] = vc.T
        kst_ref[0] = ks.T
        vst_ref[0] = vs.T
    gates = _sigmoid(seg(k0 + 6 * KV_DIM, k0 + 6 * KV_DIM + LANES))
    if sample:
        gt_ref[...] = gates
    else:
        pos = pl.program_id(1) * tm + lax.broadcasted_iota(jnp.int32, (tm, LANES), 0)
        block_onehot = jnp.where(lane - HEAD_DIM == (pos >> 6), 1.0, 0.0)
        for h, (a, b, c, e) in enumerate(zip(head_planes(ks, block_onehot), head_planes(vs, 1.0),
                                             head_planes(kw, 0.0), head_planes(vw, 1.0))):
            ksb_ref[h] = a.astype(BF16)
            vsb_ref[h] = b.astype(BF16)
            kwb_ref[h] = c.astype(BF16)
            vwb_ref[h] = e.astype(BF16)
        gt_ref[0] = gates
        gt_ref[1] = pltpu.roll(gates, LANES - GROUP * N_BRANCH, axis=1)


def _project(x2d, batch, seq, tm, weights, state=None):
    n, d = x2d.shape
    sample = state is not None
    nt = seq // tm if not sample else 1
    const = lambda shape: pl.BlockSpec(shape, lambda b, t: (0,) * len(shape))
    rows = lambda w: pl.BlockSpec((tm, w), lambda b, t: (b * nt + t, 0))
    nm, w_in, qn, kn1, kn2, cw, bd = weights
    in_specs = [rows(d), const(nm.shape), const(w_in.shape), const(qn.shape), const(kn1.shape),
                const(kn2.shape), const(cw.shape), const(bd.shape)]
    args = [x2d, nm, w_in, qn, kn1, kn2, cw, bd]
    kv_f32 = [jax.ShapeDtypeStruct((n, KV_DIM), F32)] * 6
    if sample:
        in_specs += [rows(CONV_DIM), rows(CONV_DIM)]
        args += list(state)
        out_shape = ([jax.ShapeDtypeStruct((n, CONV_DIM), BF16), jax.ShapeDtypeStruct((n, ATTN_DIM), F32)]
                     + kv_f32 + [jax.ShapeDtypeStruct((n, LANES), F32), jax.ShapeDtypeStruct((n, CONV_DIM), F32)])
        out_specs = [rows(CONV_DIM), rows(ATTN_DIM)] + [rows(KV_DIM)] * 6 + [rows(LANES), rows(CONV_DIM)]
        scratch = []
        grid = (1, 1)
    else:
        planes = pl.BlockSpec((N_KV_HEADS, tm, LANES), lambda b, t: (0, b * nt + t, 0))
        out_shape = ([jax.ShapeDtypeStruct((n, CONV_DIM), BF16),
                      jax.ShapeDtypeStruct((batch, N_HEADS, seq, LANES), BF16)]
                     + kv_f32 + [jax.ShapeDtypeStruct((N_KV_HEADS, n, LANES), F32),
                                 jax.ShapeDtypeStruct((batch, 8, CONV_DIM), F32)]
                     + [jax.ShapeDtypeStruct((N_KV_HEADS, n, LANES), BF16)] * 4
                     + [jax.ShapeDtypeStruct((batch, KV_DIM, seq), F32)] * 4)
        out_specs = ([rows(CONV_DIM), pl.BlockSpec((1, N_HEADS, tm, LANES), lambda b, t: (b, 0, t, 0))]
                     + [rows(KV_DIM)] * 6 + [planes, pl.BlockSpec((1, 8, CONV_DIM), lambda b, t: (b, 0, 0))]
                     + [planes] * 4 + [pl.BlockSpec((1, KV_DIM, tm), lambda b, t: (b, 0, t))] * 4)
        scratch = [pltpu.VMEM((8, CONV_DIM), F32)]
        grid = (batch, nt)
    return pl.pallas_call(
        functools.partial(_proj_body, sample, tm),
        grid=grid, in_specs=in_specs, out_specs=out_specs, out_shape=out_shape, scratch_shapes=scratch,
        compiler_params=_cparams("arbitrary", "arbitrary"),
        name="proj_sample" if sample else "proj_prompt",
    )(*args)


def _gelu_tanh(x):
    cdf = 0.5 * (1.0 + jnp.tanh(math.sqrt(2.0 / math.pi) * (x + 0.044715 * (x * x * x))))
    return x * cdf


def _compress_core(norm, x, pe_ref, we_ref, w2_ref, g_ref):
    n = x.shape[0]
    a0 = _dot((x + pe_ref[0:1]).astype(BF16), we_ref[0])
    a1 = _dot((x + pe_ref[1:2]).astype(BF16), we_ref[1])
    hid = a0 + pltpu.roll(a1, n - 1, axis=0)
    w2 = w2_ref[...]
    outs = []
    for h in range(N_KV_HEADS):
        act = _gelu_tanh(hid[:, h * CMP_HIDDEN:(h + 1) * CMP_HIDDEN])
        o = _dot(act.astype(BF16), w2)
        if norm:
            o = _rms(o, g_ref[...])
        outs.append(o)
    return outs


def _compress_rows_body(norm, x_ref, pe_ref, we_ref, w2_ref, g_ref, o_ref):
    outs = _compress_core(norm, x_ref[0], pe_ref, we_ref, w2_ref, g_ref)
    for h in range(N_KV_HEADS):
        o_ref[0, h] = jnp.concatenate([outs[h], jnp.zeros_like(outs[h])], axis=-1).astype(BF16)


def _compress_pages_body(norm, nch, n_seq, n_pages, pt_ref, *refs):
    pages = refs[:n_seq * n_pages]
    pe_ref, we_ref, w2_ref, g_ref, eye_ref, o_ref, rows_ref = refs[n_seq * n_pages:]
    eye = eye_ref[...]
    page = pages[0].shape[2]
    for j, p in enumerate(pages):
        hi, mid, lo = _split3(p[0])
        rows_ref[j * page:(j + 1) * page, :] = _dot_nt(eye, hi) + _dot_nt(eye, mid) + _dot_nt(eye, lo)
    x = jnp.concatenate([rows_ref[pl.ds(r, n_seq * nch, stride=CMP_STRIDE), :] for r in range(CMP_STRIDE)], axis=-1)
    out = jnp.concatenate(_compress_core(norm, x, pe_ref, we_ref, w2_ref, g_ref), axis=-1)
    for b in range(n_seq):
        o_ref[b] = out[b * nch:(b + 1) * nch]


def _compress_weights(pe, w1, w2):
    w1r = w1.reshape(2, CMP_STRIDE, HEAD_DIM, CMP_HIDDEN)
    eye = jnp.eye(N_KV_HEADS, dtype=w1.dtype)
    we = jnp.einsum("jrdc,hk->jrhdkc", w1r, eye).reshape(2, CMP_STRIDE * KV_DIM, N_KV_HEADS * CMP_HIDDEN)
    per = pe.reshape(2, CMP_STRIDE, 1, HEAD_DIM)
    pex = jnp.broadcast_to(per, (2, CMP_STRIDE, N_KV_HEADS, HEAD_DIM)).reshape(2, CMP_STRIDE * KV_DIM)
    return pex.astype(F32), we.astype(BF16), w2.astype(BF16)


def _compress_rows(rows3, cw, gain, norm):
    b, nch, width = rows3.shape
    pex, we, w2 = cw
    const = lambda a: pl.BlockSpec(a.shape, lambda i: (0,) * a.ndim)
    return pl.pallas_call(
        functools.partial(_compress_rows_body, norm),
        grid=(b,),
        in_specs=[pl.BlockSpec((1, nch, width), lambda i: (i, 0, 0)), const(pex), const(we), const(w2), const(gain)],
        out_specs=pl.BlockSpec((1, N_KV_HEADS, nch, LANES), lambda i: (i, 0, 0, 0)),
        out_shape=jax.ShapeDtypeStruct((b, N_KV_HEADS, nch, LANES), BF16),
        compiler_params=_cparams("arbitrary"),
        name="compress_rows",
    )(rows3, pex, we, w2, gain)


def _compress_pages(cache_t, pt_flat, n_batch, n_pages, cw, gain, norm):
    _, _, page = cache_t.shape
    nch = n_pages * page // CMP_STRIDE
    pex, we, w2 = cw
    eye = jnp.asarray(np.eye(page), BF16)
    n_seq = COMPRESS_SEQS if n_batch % COMPRESS_SEQS == 0 else 1
    const = lambda a: pl.BlockSpec(a.shape, lambda i, pt: (0,) * a.ndim)
    page_spec = lambda j: pl.BlockSpec((1, KV_DIM, page), lambda i, pt: (pt[i * (n_seq * n_pages) + j], 0, 0))
    n_ops = n_seq * n_pages
    grid_spec = pltpu.PrefetchScalarGridSpec(
        num_scalar_prefetch=1, grid=(n_batch // n_seq,),
        in_specs=[page_spec(j) for j in range(n_ops)] + [const(pex), const(we), const(w2), const(gain), const(eye)],
        out_specs=pl.BlockSpec((n_seq, nch, KV_DIM), lambda i, pt: (i, 0, 0)),
        scratch_shapes=[pltpu.VMEM((n_ops * page, KV_DIM), F32)])
    return pl.pallas_call(
        functools.partial(_compress_pages_body, norm, nch, n_seq, n_pages),
        grid_spec=grid_spec,
        out_shape=jax.ShapeDtypeStruct((n_batch, nch, KV_DIM), F32),
        compiler_params=_cparams("arbitrary"),
        name="compress_pages",
    )(pt_flat, *([cache_t] * n_ops), pex, we, w2, gain, eye)


def _rel_bucket(dist):
    n = jnp.maximum(dist, 0)
    max_exact = NUM_BUCKETS // 2
    nf = jnp.maximum(n, 1).astype(F32)
    large = max_exact + (jnp.log(nf / max_exact) / math.log(MAX_DISTANCE / max_exact)
                         * (NUM_BUCKETS - max_exact)).astype(jnp.int32)
    large = jnp.minimum(large, NUM_BUCKETS - 1)
    return jnp.where(n < max_exact, n, large)


def _bias_by_distance(rel_bias):
    d = jnp.arange(BIAS_DMAX, dtype=jnp.int32)
    f = rel_bias.astype(F32)[_rel_bucket(d)]
    f = (f - f[BIAS_DMAX - 1:BIAS_DMAX]).T
    return jnp.concatenate([f, jnp.full((N_HEADS, 1), NEG_INF, F32)], axis=1)


def _bias_index(d, valid):
    return np.where(valid, np.clip(d, 0, BIAS_DMAX - 1), BIAS_DMAX).astype(np.int32)


def _bias_table(fext, d, valid):
    return jnp.take(fext, jnp.asarray(_bias_index(d, valid)), axis=1)


def _toeplitz(g, n_rows, k):
    h = g.shape[0]
    w = n_rows + k
    v = jnp.pad(g, ((0, 0), (0, w - g.shape[1])))
    t = jnp.tile(v, (1, n_rows + 1))[:, :n_rows * (w + 1)].reshape(h, n_rows, w + 1)[:, :, :k]
    return t[:, :, ::-1]


def _select_blocks(imp_t, srow, qpos, n_rank):
    qblk = qpos >> 6
    forced = (srow == 0) | (srow == qblk) | (srow == qblk - 1)
    valid = (srow << 6) <= qpos
    imp_t = jnp.where(valid, imp_t + jnp.where(forced, FORCE_SCORE, 0.0), NEG_INF)
    n_rows = imp_t.shape[0]
    assert n_rows % 8 == 0
    slabs = [imp_t[a:a + 8] for a in range(0, n_rows, 8)]
    rows8 = [srow[a:a + 8] for a in range(0, n_rows, 8)]
    cnts = [jnp.zeros(x.shape, jnp.int32) for x in slabs]
    for s in range(n_rank):
        r = imp_t[s:s + 1, :]
        for j, x in enumerate(slabs):
            if 8 * j > s:
                beats = r >= x
            elif 8 * j + 7 <= s:
                beats = r > x
            else:
                beats = (r > x) | ((r == x) & (rows8[j] > s))
            cnts[j] = cnts[j] + jnp.where(beats, 1, 0)
    cnt = jnp.concatenate(cnts, axis=0)
    return jnp.where((cnt < N_SEL) & valid, 1.0, 0.0)


def _exp_pv(s, m, v):
    return _dot(jnp.exp((s - m).astype(BF16)), v)


def _normalize_pv(pv):
    return pv / jnp.maximum(pv[:, HEAD_DIM:HEAD_DIM + 1], 1e-30)


def _attn_prompt_body(kt, n_slc, q_ref, gt_ref, kcmp_ref, vcmp_ref, ks_ref, vs_ref, kw_ref, vw_ref,
                      ctab_ref, ntab_ref, wtab_ref, ovl_ref, o_ref):
    i = pl.program_id(2)
    qb = Q_BLOCK
    rows = GROUP * qb
    n_cmp_pad = kcmp_ref.shape[2]
    first_near_block = 2 * jnp.maximum(i - 1, 0)
    near_start = pl.multiple_of(jnp.maximum(i - 1, 0) * qb, qb)
    win_start = pl.multiple_of(jnp.maximum(i * qb - WINDOW, 0), qb)
    band = WINDOW + qb
    gates = gt_ref[0]
    lane = lax.broadcasted_iota(jnp.int32, (qb, LANES), 1)
    q0 = q_ref[0].reshape(rows, LANES)

    s = _dot_nt(q0, kw_ref[0, pl.ds(win_start, band), :]) + wtab_ref[0].reshape(rows, band)
    m = jnp.max(s, axis=-1, keepdims=True)
    m = jnp.where(m == NEG_INF, 0.0, m)
    o_w = _normalize_pv(_exp_pv(s, m, vw_ref[0, pl.ds(win_start, band), :]))

    per_qb = qb // CMP_STRIDE
    ctab = pltpu.roll(ctab_ref[...].reshape(rows, 2 * n_cmp_pad), i * per_qb, axis=1)[:, n_cmp_pad:]
    s = _dot_nt(q0, kcmp_ref[0, 0]) + ctab
    p, l = _softmax_parts(s)
    pn = p / jnp.maximum(l, 1e-30)
    o_c = _dot(pn.astype(BF16), vcmp_ref[0, 0])

    psum = pn[0:qb] + pn[qb:2 * qb] + pn[2 * qb:3 * qb] + pn[3 * qb:4 * qb]
    hi, mid, lo = _split3(psum)
    ovl = ovl_ref[...]
    imp_t = _dot_nt(ovl, hi) + _dot_nt(ovl, mid) + _dot_nt(ovl, lo)
    srow = lax.broadcasted_iota(jnp.int32, (n_slc, qb), 0)
    qpos_t = i * qb + lax.broadcasted_iota(jnp.int32, (n_slc, qb), 1)
    sel_t = _select_blocks(imp_t, srow, qpos_t, n_slc)
    sel_t = jnp.concatenate([sel_t, jnp.zeros((LANES - n_slc, qb), F32)], axis=0)
    sel = sel_t.T

    def query_with_mask(keep):
        m = pltpu.roll(jnp.where(keep, 0.0, MASKED), HEAD_DIM, axis=1).astype(BF16)
        m = jnp.concatenate([jnp.where(lane < HEAD_DIM, q0[g * qb:(g + 1) * qb], m) for g in range(GROUP)], axis=0)
        return m

    q_near = query_with_mask(sel > 0.5)
    q_far = query_with_mask((sel > 0.5) & (lane < first_near_block))

    s = _dot_nt(q_near, ks_ref[0, pl.ds(near_start, 2 * qb), :]) + ntab_ref[0].reshape(rows, 2 * qb)
    m0 = jnp.max(s, axis=-1, keepdims=True)
    m0 = jnp.where(m0 == NEG_INF, 0.0, m0)
    a0 = _exp_pv(s, m0, vs_ref[0, pl.ds(near_start, 2 * qb), :])

    def far_tile(t, carry):
        m_old, acc = carry
        k0 = pl.multiple_of(t * kt, kt)
        s = _dot_nt(q_far, ks_ref[0, pl.ds(k0, kt), :])
        m_new = jnp.maximum(m_old, jnp.max(s, axis=-1, keepdims=True))
        acc = jnp.exp(m_old - m_new) * acc + _exp_pv(s, m_new, vs_ref[0, pl.ds(k0, kt), :])
        return m_new, acc

    n_far = (near_start + kt - 1) // kt
    _, acc_s = lax.fori_loop(0, n_far, far_tile, (m0, a0))
    o_s = _normalize_pv(acc_s)

    heads_out = []
    for g in range(GROUP):
        c = g * N_BRANCH
        sl = slice(g * qb, (g + 1) * qb)
        heads_out.append(gates[:, c:c + 1] * o_c[sl] + gates[:, c + 1:c + 2] * o_s[sl] + gates[:, c + 2:c + 3] * o_w[sl])
    tiles = [jnp.where(lane < HEAD_DIM, heads_out[2 * j], pltpu.roll(heads_out[2 * j + 1], HEAD_DIM, axis=1))
             for j in range(GROUP // 2)]
    o_ref[...] = jnp.concatenate(tiles, axis=-1).astype(BF16)


def _attn_prompt(qp, gates, kcmp, vcmp, ksb, vsb, kwb, vwb, fext, batch, seq):
    qb = Q_BLOCK
    nqb = seq // qb
    n_slc = seq // SLC_BLOCK
    assert n_slc <= LANES - HEAD_DIM
    n_cmp_pad = kcmp.shape[2]
    n_cmp = n_cmp_pad - 1
    kt = min(512, seq)
    band = WINDOW + qb
    iq = np.arange(qb)

    per_qb = qb // CMP_STRIDE
    half = 2 * per_qb
    m = np.arange(-half, half)
    d = iq[:, None] - (m[None, :] * CMP_STRIDE + CMP_BLOCK - 1)
    assert d[:, 0].min() >= MAX_DISTANCE and d[:, -1].max() < 0
    ctab = jnp.concatenate([jnp.zeros((N_HEADS, qb, n_cmp_pad - half), F32), _bias_table(fext, d, d >= 0),
                            jnp.full((N_HEADS, qb, n_cmp_pad - half), NEG_INF, F32)], axis=2)
    assert per_qb * (nqb - 1) < n_cmp_pad
    assert (n_cmp_pad - 1) * CMP_STRIDE + CMP_BLOCK - 1 >= seq and n_cmp == n_cmp_pad - 1

    def toeplitz_variants(n_var, k, valid):
        j = np.arange(qb + k - 1)
        tabs = []
        for v in range(n_var):
            dj = qb * v + j - (k - 1)
            tabs.append(_toeplitz(_bias_table(fext, dj, valid(dj)), qb, k))
        return jnp.stack(tabs, axis=0)

    ntab = toeplitz_variants(2, 2 * qb, lambda dj: dj >= 0)
    nv = WINDOW // qb + 1
    wtab = toeplitz_variants(nv, band, lambda dj: (dj >= 0) & (dj < WINDOW))

    c_start = np.arange(n_cmp_pad) * CMP_STRIDE
    s_start = np.arange(n_slc) * SLC_BLOCK
    ovl = ((c_start[None, :] < s_start[:, None] + SLC_BLOCK) & (c_start[None, :] + CMP_BLOCK > s_start[:, None])
           & (np.arange(n_cmp_pad) < n_cmp)[None, :])
    ovl = jnp.asarray(ovl, BF16)

    per_head = lambda: pl.BlockSpec((1, seq, LANES), lambda b, h, i: (h, b, 0))
    return pl.pallas_call(
        functools.partial(_attn_prompt_body, kt, n_slc),
        grid=(batch, N_KV_HEADS, nqb),
        in_specs=[pl.BlockSpec((1, GROUP, qb, LANES), lambda b, h, i: (b, h, i, 0)),
                  pl.BlockSpec((1, qb, LANES), lambda b, h, i: (h, b * nqb + i, 0)),
                  pl.BlockSpec((1, 1, n_cmp_pad, LANES), lambda b, h, i: (b, h, 0, 0)),
                  pl.BlockSpec((1, 1, n_cmp_pad, LANES), lambda b, h, i: (b, h, 0, 0)),
                  per_head(), per_head(), per_head(), per_head(),
                  pl.BlockSpec((GROUP, qb, 2 * n_cmp_pad), lambda b, h, i: (h, 0, 0)),
                  pl.BlockSpec((1, GROUP, qb, 2 * qb), lambda b, h, i: (jnp.minimum(i, 1), h, 0, 0)),
                  pl.BlockSpec((1, GROUP, qb, band), lambda b, h, i: (jnp.minimum(i, nv - 1), h, 0, 0)),
                  pl.BlockSpec(ovl.shape, lambda b, h, i: (0, 0))],
        out_specs=pl.BlockSpec((qb, GROUP * HEAD_DIM), lambda b, h, i: (b * nqb + i, h)),
        out_shape=jax.ShapeDtypeStruct((batch * seq, ATTN_DIM), BF16),
        compiler_params=_cparams("arbitrary", "arbitrary", "arbitrary"),
        name="attn_prompt",
    )(qp, gates, kcmp, vcmp, ksb, vsb, kwb, vwb, ctab, ntab, wtab, ovl)


def _attn_sample_body(n_pages, ts, past_len, pt_ref, *refs):
    kpages = refs[:n_pages]
    vpages = refs[n_pages:2 * n_pages]
    (q_ref, gt_ref, kcmp_ref, vcmp_ref, ksn_ref, vsn_ref, kwc_ref, vwc_ref, kwn_ref, vwn_ref,
     ctab_ref, stab_ref, sntab_ref, wtab_ref, wntab_ref, eexp_ref, ovl_ref, o_ref) = refs[2 * n_pages:]
    rows = GROUP * N_KV_HEADS * ts
    rq = N_KV_HEADS * ts
    q = q_ref[0]
    gates = gt_ref[0]

    s = _dot_nt(q, kcmp_ref[0].astype(BF16)) + ctab_ref[...]
    p, l = _softmax_parts(s)
    pn = p / jnp.maximum(l, 1e-30)
    o_c = _dot(pn.astype(BF16), vcmp_ref[0].astype(BF16))

    psum = pn[0:rq]
    for g in range(1, GROUP):
        psum = psum + pn[g * rq:(g + 1) * rq]
    hi, mid, lo = _split3(psum)
    ovl = ovl_ref[...]
    imp = _dot(hi, ovl) + _dot(mid, ovl) + _dot(lo, ovl)
    n_slc = -(-(past_len + ts) // SLC_BLOCK)
    blk = lax.broadcasted_iota(jnp.int32, (rq, LANES), 1)
    qpos = past_len + (lax.broadcasted_iota(jnp.int32, (rq, LANES), 0) & (ts - 1))
    qblk = qpos >> 6
    forced = (blk == 0) | (blk == qblk) | (blk == qblk - 1)
    valid = ((blk << 6) <= qpos) & (blk < n_slc)
    imp = jnp.where(valid, imp + jnp.where(forced, FORCE_SCORE, 0.0), NEG_INF)
    cnt = jnp.zeros((rq, LANES), jnp.int32)
    for sidx in range(n_slc):
        r = imp[:, sidx:sidx + 1]
        beats = (r > imp) | ((r == imp) & (blk > sidx))
        cnt = cnt + jnp.where(beats, 1, 0)
    sel = jnp.where((cnt < N_SEL) & valid, 1.0, 0.0)
    sel = jnp.concatenate([sel] * GROUP, axis=0).astype(BF16)

    kc_t = jnp.concatenate([p_[0] for p_ in kpages], axis=1).astype(BF16)
    vc_t = jnp.concatenate([p_[0] for p_ in vpages], axis=1).astype(BF16)
    mexp = _dot(sel, eexp_ref[...])
    s1 = jnp.where(mexp > 0.5, _dot(q, kc_t) + stab_ref[...], NEG_INF)
    last = sel[:, n_slc - 1:n_slc].astype(F32)
    s2 = jnp.where(last > 0.5, _dot_nt(q, ksn_ref[0].astype(BF16)) + sntab_ref[...], NEG_INF)
    m = jnp.maximum(jnp.max(s1, axis=-1, keepdims=True), jnp.max(s2, axis=-1, keepdims=True))
    m = jnp.where(m == NEG_INF, 0.0, m)
    p1 = jnp.exp(s1 - m)
    p2 = jnp.exp(s2 - m)
    l = jnp.sum(p1, axis=-1, keepdims=True) + jnp.sum(p2, axis=-1, keepdims=True)
    o_s = (_dot_nt(p1.astype(BF16), vc_t) + _dot(p2.astype(BF16), vsn_ref[0].astype(BF16))) / jnp.maximum(l, 1e-30)

    s1 = _dot(q, kwc_ref[0].astype(BF16)) + wtab_ref[...]
    s2 = _dot_nt(q, kwn_ref[0].astype(BF16)) + wntab_ref[...]
    m = jnp.maximum(jnp.max(s1, axis=-1, keepdims=True), jnp.max(s2, axis=-1, keepdims=True))
    m = jnp.where(m == NEG_INF, 0.0, m)
    p1 = jnp.exp(s1 - m)
    p2 = jnp.exp(s2 - m)
    l = jnp.sum(p1, axis=-1, keepdims=True) + jnp.sum(p2, axis=-1, keepdims=True)
    o_w = (_dot_nt(p1.astype(BF16), vwc_ref[0].astype(BF16))
           + _dot(p2.astype(BF16), vwn_ref[0].astype(BF16))) / jnp.maximum(l, 1e-30)

    o_ref[0] = gates[:, 0:1] * o_c + gates[:, 1:2] * o_s + gates[:, 2:3] * o_w


def _attn_sample(q_s, gates_s, kcmp, vcmp, ks_new, vs_new, kw_new, vw_new, cache_ks, cache_vs,
                 cache_kw, cache_vw, pt_flat, fext, n_batch, ts, n_pages, page):
    past_len = n_pages * page
    w_buf = cache_kw.shape[2]
    rows = GROUP * N_KV_HEADS * ts
    n_new = 8
    n_cmp_pad = kcmp.shape[1]
    n_cmp = n_cmp_pad - 1
    n_slc = -(-(past_len + ts) // SLC_BLOCK)

    q5 = q_s.reshape(n_batch, ts, N_KV_HEADS, GROUP, HEAD_DIM).transpose(0, 3, 2, 1, 4)
    eye = jnp.eye(N_KV_HEADS, dtype=q_s.dtype)
    qr = jnp.einsum("bghtd,hk->bghtkd", q5, eye).reshape(n_batch, rows, LANES).astype(BF16)
    g5 = gates_s[:, :N_HEADS * N_BRANCH].reshape(n_batch, ts, N_KV_HEADS, GROUP, N_BRANCH).transpose(0, 3, 2, 1, 4)
    gr = jnp.pad(g5.reshape(n_batch, rows, N_BRANCH), ((0, 0), (0, 0), (0, LANES - N_BRANCH)))
    pad_new = lambda a: jnp.pad(a.reshape(n_batch, ts, KV_DIM), ((0, 0), (0, n_new - ts), (0, 0)))
    ks_new, vs_new, kw_new, vw_new = map(pad_new, (ks_new, vs_new, kw_new, vw_new))

    g_i, h_i, t_i = np.meshgrid(np.arange(GROUP), np.arange(N_KV_HEADS), np.arange(ts), indexing="ij")
    head = (h_i * GROUP + g_i).reshape(rows)
    tq = t_i.reshape(rows)
    pos_q = past_len + tq

    f_rows = fext[jnp.asarray(head)]

    def table(d, valid):
        return jnp.take_along_axis(f_rows, jnp.asarray(_bias_index(d, valid)), axis=1)

    nn = np.arange(n_cmp_pad)
    d = pos_q[:, None] - (nn[None, :] * CMP_STRIDE + CMP_BLOCK - 1)
    ctab = table(d, (d >= 0) & (nn < n_cmp)[None, :])
    near = np.arange(past_len - MAX_DISTANCE, past_len)
    d = pos_q[:, None] - near[None, :]
    assert past_len >= MAX_DISTANCE and d.min() >= 0
    stab = jnp.concatenate([jnp.zeros((rows, past_len - MAX_DISTANCE), F32), table(d, d >= 0)], axis=1)
    jn = np.arange(n_new)
    d = tq[:, None] - jn[None, :]
    sntab = table(d, (d >= 0) & (jn < ts)[None, :])
    pos_w = past_len - w_buf + np.arange(w_buf)
    d = pos_q[:, None] - pos_w[None, :]
    wtab = table(d, (d >= 0) & (d < WINDOW) & (pos_w >= 0)[None, :])
    wntab = table(tq[:, None] - jn[None, :], (tq[:, None] >= jn[None, :]) & (jn < ts)[None, :])

    eexp = jnp.asarray(np.arange(past_len)[None, :] // SLC_BLOCK == np.arange(LANES)[:, None], BF16)
    c_start = nn * CMP_STRIDE
    s_start = np.arange(LANES) * SLC_BLOCK
    ovl = jnp.asarray((c_start[:, None] < s_start[None, :] + SLC_BLOCK) & (c_start[:, None] + CMP_BLOCK > s_start[None, :])
                      & (nn < n_cmp)[:, None] & (np.arange(LANES) < n_slc)[None, :], BF16)

    const = lambda a: pl.BlockSpec(a.shape, lambda b, pt: (0,) * a.ndim)
    per_b = lambda a: pl.BlockSpec((1,) + a.shape[1:], lambda b, pt: (b,) + (0,) * (a.ndim - 1))
    page_spec = lambda j: pl.BlockSpec((1, KV_DIM, page), lambda b, pt: (pt[b * n_pages + j], 0, 0))
    small = [qr, gr, kcmp, vcmp, ks_new, vs_new, cache_kw, cache_vw, kw_new, vw_new]
    consts = [ctab, stab, sntab, wtab, wntab, eexp, ovl]
    grid_spec = pltpu.PrefetchScalarGridSpec(
        num_scalar_prefetch=1, grid=(n_batch,),
        in_specs=[page_spec(j) for j in range(n_pages)] * 2 + [per_b(a) for a in small] + [const(a) for a in consts],
        out_specs=pl.BlockSpec((1, rows, LANES), lambda b, pt: (b, 0, 0)))
    o = pl.pallas_call(
        functools.partial(_attn_sample_body, n_pages, ts, past_len),
        grid_spec=grid_spec,
        out_shape=jax.ShapeDtypeStruct((n_batch, rows, LANES), F32),
        compiler_params=_cparams("arbitrary"),
        name="attn_sample",
    )(pt_flat, *([cache_ks] * n_pages), *([cache_vs] * n_pages), *small, *consts)
    o6 = o.reshape(n_batch, GROUP, N_KV_HEADS, ts, N_KV_HEADS, HEAD_DIM)
    o5 = jnp.stack([o6[:, :, h, :, h] for h in range(N_KV_HEADS)], axis=2)
    return o5.transpose(0, 3, 2, 1, 4).reshape(n_batch * ts, ATTN_DIM).astype(BF16)


def _post1_body(tm, x_ref, co_ref, at_ref, wo_ref, nf_ref, wr_ref, br_ref, tri_ref, run0_ref,
                h_ref, hn_ref, rt_ref, cnt_ref, run_ref):
    @pl.when(pl.program_id(0) == 0)
    def _():
        run_ref[...] = run0_ref[...]

    h = x_ref[...] + _dot(co_ref[...], wo_ref[0:CONV_DIM]) + _dot(at_ref[...], wo_ref[CONV_DIM:CONV_DIM + ATTN_DIM])
    hn = _rms(h, nf_ref[...])
    h_ref[...] = h
    _store_token_tiles(hn_ref, hn, tm)

    hi = hn.astype(BF16)
    lo = (hn - hi.astype(F32)).astype(BF16)
    wr = wr_ref[...]
    whi = wr.astype(BF16)
    wlo = (wr - whi.astype(F32)).astype(BF16)
    logits = _dot(hi, whi) + _dot(lo, whi) + _dot(hi, wlo) + br_ref[...]

    lane_i = lax.broadcasted_iota(jnp.int32, (tm, LANES), 1)
    lane = lane_i.astype(F32)
    big = float(LANES)
    gmask = (lane_i >= ROUTER_GROUP_LANE) & (lane_i < ROUTER_GROUP_LANE + N_GROUPS)
    lg = jnp.where(gmask, logits, NEG_INF)
    eg = jnp.exp(lg - jnp.max(lg, axis=-1, keepdims=True))
    pg = eg / jnp.sum(eg, axis=-1, keepdims=True)
    gw = jnp.max(pg, axis=-1, keepdims=True)
    grp = jnp.min(jnp.where(gmask & (pg == gw), lane, big), axis=-1, keepdims=True) - ROUTER_GROUP_LANE

    group_of_lane = (lane_i >> 3).astype(F32)
    emask = (lane_i < N_EXPERTS) & (group_of_lane == grp)
    le = jnp.where(emask, logits, NEG_INF)
    ee = jnp.exp(le - jnp.max(le, axis=-1, keepdims=True))
    pe = jnp.where(emask, ee / jnp.sum(ee, axis=-1, keepdims=True), -1.0)
    v1 = jnp.max(pe, axis=-1, keepdims=True)
    i1 = jnp.min(jnp.where(pe == v1, lane, big), axis=-1, keepdims=True)
    pe2 = jnp.where(lane == i1, -1.0, pe)
    v2 = jnp.max(pe2, axis=-1, keepdims=True)
    i2 = jnp.min(jnp.where(pe2 == v2, lane, big), axis=-1, keepdims=True)
    tot = v1 + v2
    w1 = v1 / tot * gw
    w2 = v2 / tot * gw

    oh1 = jnp.where(lane == i1, 1.0, 0.0)
    oh2 = jnp.where(lane == i2, 1.0, 0.0)
    both = oh1 + oh2
    before = _dot(tri_ref[...], both.astype(BF16)) + run_ref[0:1]
    r1 = jnp.sum(oh1 * before, axis=-1, keepdims=True)
    r2 = jnp.sum(oh2 * before, axis=-1, keepdims=True)
    run = run_ref[0:1] + jnp.sum(both, axis=0, keepdims=True)
    run_ref[...] = jnp.broadcast_to(run, run_ref.shape)
    cnt_ref[...] = jnp.broadcast_to(run, cnt_ref.shape)

    rt = jnp.where(lane_i == 0, i1, 0.0)
    rt = jnp.where(lane_i == 1, i2, rt)
    rt = jnp.where(lane_i == 2, r1, rt)
    rt = jnp.where(lane_i == 3, r2, rt)
    rt = jnp.where(lane_i == 4, w1, rt)
    rt = jnp.where(lane_i == 5, w2, rt)
    rt_ref[...] = rt


def _post1(x2d, co, at, wo, nf, wr, br, run0, tm):
    n, d = x2d.shape
    tri = jnp.asarray(np.tril(np.ones((tm, tm), np.float32), -1), BF16)
    rows = lambda w: pl.BlockSpec((tm, w), lambda i: (i, 0))
    const = lambda a: pl.BlockSpec(a.shape, lambda i: (0,) * a.ndim)
    return pl.pallas_call(
        functools.partial(_post1_body, tm),
        grid=(n // tm,),
        in_specs=[rows(d), rows(CONV_DIM), rows(ATTN_DIM), const(wo), const(nf), const(wr), const(br),
                  const(tri), const(run0)],
        out_specs=[rows(d), pl.BlockSpec((tm * TOKEN_TILE_ROWS, LANES), lambda i: (i, 0)), rows(LANES),
                   pl.BlockSpec((8, LANES), lambda i: (0, 0))],
        out_shape=[jax.ShapeDtypeStruct((n, d), F32), jax.ShapeDtypeStruct((n * TOKEN_TILE_ROWS, LANES), F32),
                   jax.ShapeDtypeStruct((n, LANES), F32), jax.ShapeDtypeStruct((8, LANES), F32)],
        scratch_shapes=[pltpu.VMEM((8, LANES), F32)],
        compiler_params=_cparams("arbitrary"),
        name="post1",
    )(x2d, co, at, wo, nf, wr, br, tri, run0)


def _token_copy(src_ref, dst_ref, s, d, sem):
    r = TOKEN_TILE_ROWS
    return pltpu.make_async_copy(src_ref.at[pl.ds(pl.multiple_of(s * r, r), r)],
                                 dst_ref.at[pl.ds(pl.multiple_of(d * r, r), r)], sem)


def _scatter_rows_body(ts, dest_ref, src_ref, init_ref, out_ref, sem):
    del init_ref
    base = pl.program_id(0) * (2 * ts)

    def issue(t, _):
        for k in range(2):
            _token_copy(src_ref, out_ref, t, dest_ref[base + 2 * t + k], sem).start(priority=k)
        return 0

    lax.fori_loop(0, ts, issue, 0)
    for _ in range(2):
        pltpu.make_async_copy(src_ref, out_ref.at[pl.ds(0, ts * TOKEN_TILE_ROWS)], sem).wait()


def _scatter_rows(dest, src, slots):
    n_tok = dest.shape[0] // 2
    ts = min(SCATTER_TOKENS, n_tok)
    assert n_tok % ts == 0
    any_spec = pl.BlockSpec(memory_space=pl.ANY)
    return pl.pallas_call(
        functools.partial(_scatter_rows_body, ts),
        grid_spec=pltpu.PrefetchScalarGridSpec(
            num_scalar_prefetch=1, grid=(n_tok // ts,),
            in_specs=[pl.BlockSpec((ts * TOKEN_TILE_ROWS, LANES), lambda i, dest: (i, 0)), any_spec],
            out_specs=any_spec, scratch_shapes=[pltpu.SemaphoreType.DMA(())]),
        out_shape=jax.ShapeDtypeStruct(slots.shape, slots.dtype),
        input_output_aliases={2: 0},
        compiler_params=pltpu.CompilerParams(dimension_semantics=("arbitrary",)),
        name="scatter_rows",
    )(dest, src, slots)


def _experts_body(be_ref, nu_ref, x_ref, wg_ref, wu_ref, wd_ref, o_ref, wg_s, wu_s, wd_s):
    i = pl.program_id(0)

    @pl.when(i < nu_ref[0])
    def _():
        prev = be_ref[jnp.maximum(i - 1, 0)]

        @pl.when((i == 0) | (be_ref[i] != prev))
        def _():
            wg_s[...] = wg_ref[0].astype(BF16)
            wu_s[...] = wu_ref[0].astype(BF16)
            wd_s[...] = wd_ref[0].astype(BF16)

        x = _load_token_tiles(x_ref, EXPERT_ROWS, TOKEN_TILE_ROWS).astype(BF16)
        g = _dot(x, wg_s[...])
        u = _dot(x, wu_s[...])
        a = g * _sigmoid(g) * u
        _store_token_tiles(o_ref, _dot(a.astype(BF16), wd_s[...]), EXPERT_ROWS)

    @pl.when(i >= nu_ref[0])
    def _():
        o_ref[...] = jnp.zeros_like(o_ref)


def _experts(blk_expert, n_used, xs, wg, wu, wd):
    blk_rows = EXPERT_ROWS * TOKEN_TILE_ROWS
    n_blk = xs.shape[0] // blk_rows
    _, d, de = wg.shape
    xmap = lambda i, be, nu: (jnp.minimum(i, jnp.maximum(nu[0] - 1, 0)), 0)
    wmap = lambda i, be, nu: (be[jnp.minimum(i, jnp.maximum(nu[0] - 1, 0))], 0, 0)
    grid_spec = pltpu.PrefetchScalarGridSpec(
        num_scalar_prefetch=2, grid=(n_blk,),
        in_specs=[pl.BlockSpec((blk_rows, LANES), xmap), pl.BlockSpec((1, d, de), wmap),
                  pl.BlockSpec((1, d, de), wmap), pl.BlockSpec((1, de, d), wmap)],
        out_specs=pl.BlockSpec((blk_rows, LANES), lambda i, be, nu: (i, 0)),
        scratch_shapes=[pltpu.VMEM((d, de), BF16), pltpu.VMEM((d, de), BF16), pltpu.VMEM((de, d), BF16)])
    return pl.pallas_call(
        _experts_body, grid_spec=grid_spec,
        out_shape=jax.ShapeDtypeStruct(xs.shape, F32),
        compiler_params=_cparams("arbitrary"),
        name="experts",
    )(blk_expert, n_used, xs, wg, wu, wd)


def _post2_body(tm, dest_ref, h_ref, rt_ref, p_ref, yb_ref, wple_ref, wpg_ref, bpg_ref, np_ref, o_ref, buf, sem):
    i = pl.program_id(0)
    n = pl.num_programs(0)

    def fetch(step, slot):
        base = step * (2 * tm)

        def issue(t, _):
            for k in range(2):
                _token_copy(yb_ref, buf.at[slot], dest_ref[base + 2 * t + k], k * tm + t, sem.at[slot]).start(priority=k)
            return 0

        lax.fori_loop(0, tm, issue, 0)

    @pl.when(i == 0)
    def _():
        fetch(0, 0)

    @pl.when(i + 1 < n)
    def _():
        fetch(i + 1, (i + 1) & 1)

    slot = i & 1

    pltpu.make_async_copy(yb_ref.at[pl.ds(0, 2 * tm * TOKEN_TILE_ROWS)], buf.at[slot], sem.at[slot]).wait()
    rt = rt_ref[...]
    y0 = _load_token_tiles(buf.at[slot], tm, TOKEN_TILE_ROWS)
    y1 = _load_token_tiles(buf.at[slot], tm, TOKEN_TILE_ROWS, first=tm * TOKEN_TILE_ROWS)
    h = h_ref[...] + (y0 * rt[:, 4:5] + y1 * rt[:, 5:6])
    gate = _sigmoid(_dot(_rms(h, np_ref[...]).astype(BF16), wpg_ref[...]) + bpg_ref[...])
    o_ref[...] = h + gate * _dot(p_ref[...].astype(BF16), wple_ref[...])


def _post2(dest, h, rt, p2d, yb, wple, wpg, bpg, npl, tm):
    n, d = h.shape
    rows = lambda w: pl.BlockSpec((tm, w), lambda i, dest: (i, 0))
    const = lambda a: pl.BlockSpec(a.shape, lambda i, dest: (0,) * a.ndim)
    grid_spec = pltpu.PrefetchScalarGridSpec(
        num_scalar_prefetch=1, grid=(n // tm,),
        in_specs=[rows(d), rows(LANES), rows(p2d.shape[1]), pl.BlockSpec(memory_space=pl.ANY), const(wple),
                  const(wpg), const(bpg), const(npl)],
        out_specs=rows(d),
        scratch_shapes=[pltpu.VMEM((2, 2 * tm * TOKEN_TILE_ROWS, LANES), F32), pltpu.SemaphoreType.DMA((2,))])
    return pl.pallas_call(
        functools.partial(_post2_body, tm),
        grid_spec=grid_spec,
        out_shape=jax.ShapeDtypeStruct((n, d), F32),
        compiler_params=_cparams("arbitrary"),
        name="post2",
    )(dest, h, rt, p2d, yb, wple, wpg, bpg, npl)


def _row_tile(n, cap=512):
    t = min(cap, n)
    assert n % t == 0 and t % 8 == 0
    return t


def kernel(x_prompt, x_sample, p_prompt, p_sample, cache_k_cmp, cache_v_cmp, cache_k_slc, cache_v_slc, cache_k_win, cache_v_win, state_conv, page_table, w_in, w_out, conv_w, norm_mix, norm_ffn, norm_ple, q_norm, k_norm, cmp_pe_k, cmp_w1_k, cmp_w2_k, cmp_pe_v, cmp_w1_v, cmp_w2_v, rel_bias, w_router_group, b_router_group, w_router_expert, b_router_expert, w_exp_gate, w_exp_up, w_exp_down, w_ple, w_ple_gate, b_ple_gate):
    assert w_in.shape[0] == 1, "single-layer step"
    bp, t, d = x_prompt.shape
    bs, ts, _ = x_sample.shape
    n_pages = page_table.shape[1]
    page = cache_k_cmp.shape[2]
    past_len = n_pages * page
    w_buf = cache_k_win.shape[2]
    n_phys = cache_k_cmp.shape[1]
    assert t % Q_BLOCK == 0 and t >= WINDOW + Q_BLOCK and page % CMP_STRIDE == 0 and ts == 4 and d == D_MODEL
    assert past_len % SLC_BLOCK == 0
    np_rows, ns_rows = bp * t, bs * ts

    row = lambda v: v.reshape(1, -1).astype(F32)
    w_in_b = jnp.pad(w_in[0], ((0, 0), (0, Z_COLS - w_in.shape[2]))).astype(BF16)
    qn = row(jnp.tile(q_norm[0], N_HEADS))
    kn1 = row(jnp.tile(k_norm[0, 1], N_KV_HEADS))
    kn2 = row(jnp.tile(k_norm[0, 2], N_KV_HEADS))
    bd = jnp.asarray(np.kron(np.eye(N_HEADS), np.ones((HEAD_DIM, HEAD_DIM))), BF16)
    pw = (row(norm_mix[0]), w_in_b, qn, kn1, kn2, conv_w[0].astype(F32), bd)
    cw_k = _compress_weights(cmp_pe_k[0], cmp_w1_k[0], cmp_w2_k[0])
    cw_v = _compress_weights(cmp_pe_v[0], cmp_w1_v[0], cmp_w2_v[0])
    kn0 = row(k_norm[0, 0])
    fext = _bias_by_distance(rel_bias)
    pt_flat = page_table.reshape(-1).astype(jnp.int32)
    wr = jnp.zeros((d, LANES), F32).at[:, :N_EXPERTS].set(w_router_expert[0])
    wr = wr.at[:, ROUTER_GROUP_LANE:ROUTER_GROUP_LANE + N_GROUPS].set(w_router_group[0])
    br = jnp.zeros((1, LANES), F32).at[0, :N_EXPERTS].set(b_router_expert[0])
    br = br.at[0, ROUTER_GROUP_LANE:ROUTER_GROUP_LANE + N_GROUPS].set(b_router_group[0])
    wo_b = w_out[0].astype(BF16)
    wple_b = w_ple[0].astype(BF16)
    wpg_b = w_ple_gate[0].astype(BF16)

    tm_p = _row_tile(t, cap=1024)
    (co_p, q_p, kc_p, vc_p, ks_p, vs_p, kw_p, vw_p, gt_p, cs_p, ksb, vsb, kwb, vwb, kc_t, vc_t, ks_t, vs_t) = _project(
        x_prompt.reshape(np_rows, d), bp, t, tm_p, pw)
    chunk_w = CMP_STRIDE * KV_DIM
    kcmp_p = _compress_rows(kc_p.reshape(bp, t // CMP_STRIDE, chunk_w), cw_k, kn0, True)
    vcmp_p = _compress_rows(vc_p.reshape(bp, t // CMP_STRIDE, chunk_w), cw_v, kn0, False)
    at_p = _attn_prompt(q_p, gt_p, kcmp_p, vcmp_p, ksb, vsb, kwb, vwb, fext, bp, t)

    st = state_conv[0].astype(F32)
    s0 = jnp.repeat(st[:, 0], ts, axis=0)
    s1 = jnp.repeat(st[:, 1], ts, axis=0)
    (co_s, q_s, kc_s, vc_s, ks_s, vs_s, kw_s, vw_s, gt_s, u_s) = _project(
        x_sample.reshape(ns_rows, d), bs, ts, ns_rows, pw, state=(s0, s1))
    feature_major = lambda c, n, rows_: jnp.transpose(c[0], (0, 2, 3, 1)).reshape(n, KV_DIM, rows_)
    kcmp_s = _compress_pages(feature_major(cache_k_cmp, n_phys, page), pt_flat, bs, n_pages, cw_k, kn0, True)
    vcmp_s = _compress_pages(feature_major(cache_v_cmp, n_phys, page), pt_flat, bs, n_pages, cw_v, kn0, False)
    at_s = _attn_sample(q_s, gt_s, kcmp_s, vcmp_s, ks_s, vs_s, kw_s, vw_s,
                        feature_major(cache_k_slc, n_phys, page), feature_major(cache_v_slc, n_phys, page),
                        feature_major(cache_k_win, bs, w_buf), feature_major(cache_v_win, bs, w_buf),
                        pt_flat, fext, bs, ts, n_pages, page)

    tp1 = _row_tile(np_rows)
    ts1 = _row_tile(ns_rows)
    nf = row(norm_ffn[0])
    h_p, hn_p, rt_p, cnt_p = _post1(x_prompt.reshape(np_rows, d), co_p, at_p, wo_b, nf, wr, br,
                                    jnp.zeros((8, LANES), F32), tp1)
    h_s, hn_s, rt_s, cnt_s = _post1(x_sample.reshape(ns_rows, d), co_s, at_s, wo_b, nf, wr, br, cnt_p, ts1)

    counts = cnt_s[0, :N_EXPERTS].astype(jnp.int32)
    padded = (counts + EXPERT_ROWS - 1) // EXPERT_ROWS * EXPERT_ROWS
    pad_end = jnp.cumsum(padded)
    pad_start = pad_end - padded
    n_assign = 2 * (np_rows + ns_rows)
    n_blk = (n_assign + N_EXPERTS * (EXPERT_ROWS - 1) + EXPERT_ROWS - 1) // EXPERT_ROWS
    blk_first = jnp.arange(n_blk, dtype=jnp.int32) * EXPERT_ROWS
    blk_expert = jnp.minimum(jnp.sum((pad_end[None, :] <= blk_first[:, None]).astype(jnp.int32), axis=1),
                             N_EXPERTS - 1)
    n_used = (pad_end[-1:] // EXPERT_ROWS).astype(jnp.int32)

    def dest_of(rt):
        e = rt[:, 0:2].astype(jnp.int32).reshape(-1)
        return pad_start[e] + rt[:, 2:4].astype(jnp.int32).reshape(-1)

    dest_p = dest_of(rt_p)
    dest_s = dest_of(rt_s)

    xs = jnp.zeros((n_blk * EXPERT_ROWS * TOKEN_TILE_ROWS, LANES), F32)
    xs = _scatter_rows(dest_p, hn_p, xs)
    xs = _scatter_rows(dest_s, hn_s, xs)
    yb = _experts(blk_expert, n_used, xs, w_exp_gate[0], w_exp_up[0], w_exp_down[0])

    bpg = row(b_ple_gate[0])
    npl = row(norm_ple[0])
    y_p = _post2(dest_p, h_p, rt_p, p_prompt[0].reshape(np_rows, -1), yb, wple_b, wpg_b, bpg, npl, tp1)
    y_s = _post2(dest_s, h_s, rt_s, p_sample[0].reshape(ns_rows, -1), yb, wple_b, wpg_b, bpg, npl, ts1)

    kv5 = lambda a, b, s: a.reshape(1, b, s, N_KV_HEADS, HEAD_DIM)
    wp = min(WINDOW, t)
    win_p = lambda a: kv5(a, bp, t)[:, :, t - wp:]
    win_s = lambda c, new: jnp.concatenate([c[0], new.reshape(bs, ts, N_KV_HEADS, HEAD_DIM)], axis=1)[None, :, ts:]
    conv_p = cs_p[:, 8 - (CONV_K - 1):][None]
    conv_s = u_s.reshape(bs, ts, CONV_DIM)[:, ts - (CONV_K - 1):][None]
    from_t = lambda a: jnp.transpose(a.reshape(bp, N_KV_HEADS, HEAD_DIM, t), (0, 3, 1, 2))[None]
    return (y_p.reshape(bp, t, d), y_s.reshape(bs, ts, d),
            from_t(kc_t), from_t(vc_t), from_t(ks_t), from_t(vs_t), win_p(kw_p), win_p(vw_p), conv_p,
            kv5(kc_s, bs, ts), kv5(vc_s, bs, ts), kv5(ks_s, bs, ts), kv5(vs_s, bs, ts),
            win_s(cache_k_win, kw_s), win_s(cache_v_win, vw_s), conv_s)
```

```python
import functools
import math

import numpy as np
import jax
import jax.numpy as jnp
from jax import lax
from jax.experimental import pallas as pl
from jax.experimental.pallas import tpu as pltpu

F32 = jnp.float32
BF16 = jnp.bfloat16
NEG_INF = float("-inf")
MASKED = -1e30

HEAD_DIM = 64
N_HEADS = 8
N_KV_HEADS = 2
GROUP = N_HEADS // N_KV_HEADS
CONV_DIM = 512
ATTN_DIM = 512
KV_DIM = N_KV_HEADS * HEAD_DIM
N_BRANCH = 3
CONV_K = 3
CMP_BLOCK = 32
CMP_STRIDE = 16
CMP_HIDDEN = 256
SLC_BLOCK = 64
N_SEL = 16
WINDOW = 512
Q_BLOCK = 128
FORCE_SCORE = 1e4
NUM_BUCKETS = 32
MAX_DISTANCE = 128
N_GROUPS = 4
EXPERTS_PER_GROUP = 8
N_EXPERTS = N_GROUPS * EXPERTS_PER_GROUP
D_EXPERT = 512
EPS = 1e-6

D_MODEL = 1024
LANES = 128
TOKEN_TILE_ROWS = D_MODEL // LANES
Z_COLS = 3 * CONV_DIM + ATTN_DIM + 6 * KV_DIM + LANES
BIAS_DMAX = 768
EXPERT_ROWS = 512
ROUTER_GROUP_LANE = 32
SCATTER_TOKENS = 256
COMPRESS_SEQS = 4
SAMPLE_SEQS = 2
VMEM_LIMIT = 56 * 1024 * 1024


def _cparams(*sem):
    return pltpu.CompilerParams(dimension_semantics=sem, vmem_limit_bytes=VMEM_LIMIT)


def _dot(a, b):
    return jnp.dot(a, b, preferred_element_type=F32)


def _dot_nt(a, b):
    return lax.dot_general(a, b, (((1,), (1,)), ((), ())), preferred_element_type=F32)


def _split3(x):
    hi = x.astype(BF16)
    r = x - hi.astype(F32)
    mid = r.astype(BF16)
    lo = (r - mid.astype(F32)).astype(BF16)
    return hi, mid, lo


def _rms(x, g):
    return x * lax.rsqrt(jnp.mean(x * x, axis=-1, keepdims=True) + EPS) * g


def _head_rms(x, bd, g):
    hi, mid, lo = _split3(x * x)
    ss = _dot(hi, bd) + _dot(mid, bd) + _dot(lo, bd)
    return x * lax.rsqrt(ss * (1.0 / HEAD_DIM) + EPS) * g


def _sigmoid(x):
    return 1.0 / (1.0 + jnp.exp(-x))


def _store_token_tiles(ref, x, n):
    r = x.shape[1] // LANES
    for j in range(r):
        ref[pl.ds(j, n, stride=r), :] = x[:, j * LANES:(j + 1) * LANES]


def _load_token_tiles(ref, n, r, first=0):
    return jnp.concatenate([ref[pl.ds(first + j, n, stride=r), :] for j in range(r)], axis=-1)


def _softmax_parts(s):
    m = jnp.max(s, axis=-1, keepdims=True)
    m = jnp.where(m == NEG_INF, 0.0, m)
    p = jnp.exp(s - m)
    l = jnp.sum(p, axis=-1, keepdims=True)
    return p, l


def _proj_body(sample, tm, *refs):
    if sample:
        (x_ref, nm_ref, w_ref, qn_ref, kn1_ref, kn2_ref, cw_ref, bd_ref, s0_ref, s1_ref,
         co_ref, q_ref, kc_ref, vc_ref, ks_ref, vs_ref, kw_ref, vw_ref, gt_ref, u_ref) = refs
    else:
        (x_ref, nm_ref, w_ref, qn_ref, kn1_ref, kn2_ref, cw_ref, bd_ref,
         co_ref, q_ref, kc_ref, vc_ref, ks_ref, vs_ref, kw_ref, vw_ref, gt_ref, cs_ref,
         ksb_ref, vsb_ref, kwb_ref, vwb_ref, kct_ref, vct_ref, kst_ref, vst_ref, carry_ref) = refs

    xn = _rms(x_ref[...], nm_ref[...]).astype(BF16)

    def seg(a, b):
        return _dot(xn, w_ref[:, a:b])

    c3 = 3 * CONV_DIM
    u = seg(2 * CONV_DIM, c3) * seg(0, CONV_DIM)
    bg = seg(CONV_DIM, 2 * CONV_DIM)
    row = lax.broadcasted_iota(jnp.int32, (tm, 1), 0)
    um1 = pltpu.roll(u, 1, axis=0)
    um2 = pltpu.roll(u, 2, axis=0)
    if sample:
        r = row & 3
        s0 = s0_ref[...]
        s1 = s1_ref[...]
        prev1 = jnp.where(r == 0, s1, um1)
        prev2 = jnp.where(r == 0, s0, jnp.where(r == 1, s1, um2))
        u_ref[...] = u
    else:
        @pl.when(pl.program_id(1) == 0)
        def _():
            carry_ref[...] = jnp.zeros_like(carry_ref)
        c = carry_ref[...]
        prev1 = jnp.where(row == 0, c[7:8], um1)
        prev2 = jnp.where(row == 0, c[6:7], jnp.where(row == 1, c[7:8], um2))
        carry_ref[...] = u[tm - 8:tm]
        cs_ref[0] = u[tm - 8:tm]
    cw = cw_ref[...]
    y = cw[0:1] * prev2 + cw[1:2] * prev1 + cw[2:3] * u
    co_ref[...] = (bg * y).astype(BF16)

    bd = bd_ref[...]
    q = _head_rms(seg(c3, c3 + ATTN_DIM), bd, qn_ref[...]) * (HEAD_DIM ** -0.5)
    lane = lax.broadcasted_iota(jnp.int32, (tm, LANES), 1)
    low = lane < HEAD_DIM

    def head_planes(x, fill):
        return [jnp.where(low, x if h == 0 else pltpu.roll(x, HEAD_DIM, axis=1), fill) for h in range(N_KV_HEADS)]

    if sample:
        q_ref[...] = q
    else:
        for hd in range(N_HEADS):
            pair = q[:, (hd // 2) * LANES:(hd // 2 + 1) * LANES]
            if hd % 2:
                pair = pltpu.roll(pair, HEAD_DIM, axis=1)
            q_ref[0, hd] = jnp.where(low, pair, 0.0).astype(BF16)

    k0 = c3 + ATTN_DIM
    bdk = bd[:KV_DIM, :KV_DIM]
    kc = seg(k0, k0 + KV_DIM)
    vc = seg(k0 + KV_DIM, k0 + 2 * KV_DIM)
    ks = _head_rms(seg(k0 + 2 * KV_DIM, k0 + 3 * KV_DIM), bdk, kn1_ref[...])
    vs = seg(k0 + 3 * KV_DIM, k0 + 4 * KV_DIM)
    kw = _head_rms(seg(k0 + 4 * KV_DIM, k0 + 5 * KV_DIM), bdk, kn2_ref[...])
    vw = seg(k0 + 5 * KV_DIM, k0 + 6 * KV_DIM)
    kc_ref[...] = kc
    vc_ref[...] = vc
    ks_ref[...] = ks
    vs_ref[...] = vs
    kw_ref[...] = kw
    vw_ref[...] = vw
    if not sample:
        kct_ref[0] = kc.T
        vct_ref[0] = vc.T
        kst_ref[0] = ks.T
        vst_ref[0] = vs.T
    gates = _sigmoid(seg(k0 + 6 * KV_DIM, k0 + 6 * KV_DIM + LANES))
    if sample:
        gt_ref[...] = gates
    else:
        pos = pl.program_id(1) * tm + lax.broadcasted_iota(jnp.int32, (tm, LANES), 0)
        block_onehot = jnp.where(lane - HEAD_DIM == (pos >> 6), 1.0, 0.0)
        for h, (a, b, c, e) in enumerate(zip(head_planes(ks, block_onehot), head_planes(vs, 1.0),
                                             head_planes(kw, 0.0), head_planes(vw, 1.0))):
            ksb_ref[h] = a.astype(BF16)
            vsb_ref[h] = b.astype(BF16)
            kwb_ref[h] = c.astype(BF16)
            vwb_ref[h] = e.astype(BF16)
        gt_ref[0] = gates
        gt_ref[1] = pltpu.roll(gates, LANES - GROUP * N_BRANCH, axis=1)


def _project(x2d, batch, seq, tm, weights, state=None):
    n, d = x2d.shape
    sample = state is not None
    nt = seq // tm if not sample else 1
    const = lambda shape: pl.BlockSpec(shape, lambda b, t: (0,) * len(shape))
    rows = lambda w: pl.BlockSpec((tm, w), lambda b, t: (b * nt + t, 0))
    nm, w_in, qn, kn1, kn2, cw, bd = weights
    in_specs = [rows(d), const(nm.shape), const(w_in.shape), const(qn.shape), const(kn1.shape),
                const(kn2.shape), const(cw.shape), const(bd.shape)]
    args = [x2d, nm, w_in, qn, kn1, kn2, cw, bd]
    kv_f32 = [jax.ShapeDtypeStruct((n, KV_DIM), F32)] * 6
    if sample:
        in_specs += [rows(CONV_DIM), rows(CONV_DIM)]
        args += list(state)
        out_shape = ([jax.ShapeDtypeStruct((n, CONV_DIM), BF16), jax.ShapeDtypeStruct((n, ATTN_DIM), F32)]
                     + kv_f32 + [jax.ShapeDtypeStruct((n, LANES), F32), jax.ShapeDtypeStruct((n, CONV_DIM), F32)])
        out_specs = [rows(CONV_DIM), rows(ATTN_DIM)] + [rows(KV_DIM)] * 6 + [rows(LANES), rows(CONV_DIM)]
        scratch = []
        grid = (1, 1)
    else:
        planes = pl.BlockSpec((N_KV_HEADS, tm, LANES), lambda b, t: (0, b * nt + t, 0))
        out_shape = ([jax.ShapeDtypeStruct((n, CONV_DIM), BF16),
                      jax.ShapeDtypeStruct((batch, N_HEADS, seq, LANES), BF16)]
                     + kv_f32 + [jax.ShapeDtypeStruct((N_KV_HEADS, n, LANES), F32),
                                 jax.ShapeDtypeStruct((batch, 8, CONV_DIM), F32)]
                     + [jax.ShapeDtypeStruct((N_KV_HEADS, n, LANES), BF16)] * 4
                     + [jax.ShapeDtypeStruct((batch, KV_DIM, seq), F32)] * 4)
        out_specs = ([rows(CONV_DIM), pl.BlockSpec((1, N_HEADS, tm, LANES), lambda b, t: (b, 0, t, 0))]
                     + [rows(KV_DIM)] * 6 + [planes, pl.BlockSpec((1, 8, CONV_DIM), lambda b, t: (b, 0, 0))]
                     + [planes] * 4 + [pl.BlockSpec((1, KV_DIM, tm), lambda b, t: (b, 0, t))] * 4)
        scratch = [pltpu.VMEM((8, CONV_DIM), F32)]
        grid = (batch, nt)
    return pl.pallas_call(
        functools.partial(_proj_body, sample, tm),
        grid=grid, in_specs=in_specs, out_specs=out_specs, out_shape=out_shape, scratch_shapes=scratch,
        compiler_params=_cparams("arbitrary", "arbitrary"),
        name="proj_sample" if sample else "proj_prompt",
    )(*args)


def _gelu_tanh(x):
    cdf = 0.5 * (1.0 + jnp.tanh(math.sqrt(2.0 / math.pi) * (x + 0.044715 * (x * x * x))))
    return x * cdf


def _compress_core(norm, x, pe_ref, we_ref, w2_ref, g_ref):
    n = x.shape[0]
    a0 = _dot((x + pe_ref[0:1]).astype(BF16), we_ref[0])
    a1 = _dot((x + pe_ref[1:2]).astype(BF16), we_ref[1])
    hid = a0 + pltpu.roll(a1, n - 1, axis=0)
    w2 = w2_ref[...]
    outs = []
    for h in range(N_KV_HEADS):
        act = _gelu_tanh(hid[:, h * CMP_HIDDEN:(h + 1) * CMP_HIDDEN])
        o = _dot(act.astype(BF16), w2)
        if norm:
            o = _rms(o, g_ref[...])
        outs.append(o)
    return outs


def _compress_rows_body(norm, x_ref, pe_ref, we_ref, w2_ref, g_ref, o_ref):
    outs = _compress_core(norm, x_ref[0], pe_ref, we_ref, w2_ref, g_ref)
    for h in range(N_KV_HEADS):
        o_ref[0, h] = jnp.concatenate([outs[h], jnp.zeros_like(outs[h])], axis=-1).astype(BF16)


def _compress_pages_body(norm, nch, n_seq, n_pages, pt_ref, *refs):
    pages = refs[:n_seq * n_pages]
    pe_ref, we_ref, w2_ref, g_ref, eye_ref, o_ref, rows_ref = refs[n_seq * n_pages:]
    eye = eye_ref[...]
    page = pages[0].shape[2]
    for j, p in enumerate(pages):
        hi, lo, _ = _split3(p[0])
        rows_ref[j * page:(j + 1) * page, :] = _dot_nt(eye, hi) + _dot_nt(eye, lo)
    x = jnp.concatenate([rows_ref[pl.ds(r, n_seq * nch, stride=CMP_STRIDE), :] for r in range(CMP_STRIDE)], axis=-1)
    out = jnp.concatenate(_compress_core(norm, x, pe_ref, we_ref, w2_ref, g_ref), axis=-1)
    for b in range(n_seq):
        o_ref[b] = out[b * nch:(b + 1) * nch]


def _compress_weights(pe, w1, w2):
    w1r = w1.reshape(2, CMP_STRIDE, HEAD_DIM, CMP_HIDDEN)
    eye = jnp.eye(N_KV_HEADS, dtype=w1.dtype)
    we = jnp.einsum("jrdc,hk->jrhdkc", w1r, eye).reshape(2, CMP_STRIDE * KV_DIM, N_KV_HEADS * CMP_HIDDEN)
    per = pe.reshape(2, CMP_STRIDE, 1, HEAD_DIM)
    pex = jnp.broadcast_to(per, (2, CMP_STRIDE, N_KV_HEADS, HEAD_DIM)).reshape(2, CMP_STRIDE * KV_DIM)
    return pex.astype(F32), we.astype(BF16), w2.astype(BF16)


def _compress_rows(rows3, cw, gain, norm):
    b, nch, width = rows3.shape
    pex, we, w2 = cw
    const = lambda a: pl.BlockSpec(a.shape, lambda i: (0,) * a.ndim)
    return pl.pallas_call(
        functools.partial(_compress_rows_body, norm),
        grid=(b,),
        in_specs=[pl.BlockSpec((1, nch, width), lambda i: (i, 0, 0)), const(pex), const(we), const(w2), const(gain)],
        out_specs=pl.BlockSpec((1, N_KV_HEADS, nch, LANES), lambda i: (i, 0, 0, 0)),
        out_shape=jax.ShapeDtypeStruct((b, N_KV_HEADS, nch, LANES), BF16),
        compiler_params=_cparams("arbitrary"),
        name="compress_rows",
    )(rows3, pex, we, w2, gain)


def _compress_pages(cache_t, pt_flat, n_batch, n_pages, cw, gain, norm):
    _, _, page = cache_t.shape
    nch = n_pages * page // CMP_STRIDE
    pex, we, w2 = cw
    eye = jnp.asarray(np.eye(page), BF16)
    n_seq = COMPRESS_SEQS if n_batch % COMPRESS_SEQS == 0 else 1
    const = lambda a: pl.BlockSpec(a.shape, lambda i, pt: (0,) * a.ndim)
    page_spec = lambda j: pl.BlockSpec((1, KV_DIM, page), lambda i, pt: (pt[i * (n_seq * n_pages) + j], 0, 0))
    n_ops = n_seq * n_pages
    grid_spec = pltpu.PrefetchScalarGridSpec(
        num_scalar_prefetch=1, grid=(n_batch // n_seq,),
        in_specs=[page_spec(j) for j in range(n_ops)] + [const(pex), const(we), const(w2), const(gain), const(eye)],
        out_specs=pl.BlockSpec((n_seq, nch, KV_DIM), lambda i, pt: (i, 0, 0)),
        scratch_shapes=[pltpu.VMEM((n_ops * page, KV_DIM), F32)])
    return pl.pallas_call(
        functools.partial(_compress_pages_body, norm, nch, n_seq, n_pages),
        grid_spec=grid_spec,
        out_shape=jax.ShapeDtypeStruct((n_batch, nch, KV_DIM), F32),
        compiler_params=_cparams("arbitrary"),
        name="compress_pages",
    )(pt_flat, *([cache_t] * n_ops), pex, we, w2, gain, eye)


def _rel_bucket(dist):
    n = jnp.maximum(dist, 0)
    max_exact = NUM_BUCKETS // 2
    nf = jnp.maximum(n, 1).astype(F32)
    large = max_exact + (jnp.log(nf / max_exact) / math.log(MAX_DISTANCE / max_exact)
                         * (NUM_BUCKETS - max_exact)).astype(jnp.int32)
    large = jnp.minimum(large, NUM_BUCKETS - 1)
    return jnp.where(n < max_exact, n, large)


def _bias_by_distance(rel_bias):
    d = jnp.arange(BIAS_DMAX, dtype=jnp.int32)
    f = rel_bias.astype(F32)[_rel_bucket(d)]
    f = (f - f[BIAS_DMAX - 1:BIAS_DMAX]).T
    return jnp.concatenate([f, jnp.full((N_HEADS, 1), NEG_INF, F32)], axis=1)


def _bias_index(d, valid):
    return np.where(valid, np.clip(d, 0, BIAS_DMAX - 1), BIAS_DMAX).astype(np.int32)


def _bias_table(fext, d, valid):
    return jnp.take(fext, jnp.asarray(_bias_index(d, valid)), axis=1)


def _toeplitz(g, n_rows, k):
    h = g.shape[0]
    w = n_rows + k
    v = jnp.pad(g, ((0, 0), (0, w - g.shape[1])))
    t = jnp.tile(v, (1, n_rows + 1))[:, :n_rows * (w + 1)].reshape(h, n_rows, w + 1)[:, :, :k]
    return t[:, :, ::-1]


def _select_blocks(imp_t, srow, qpos, n_rank):
    qblk = qpos >> 6
    forced = (srow == 0) | (srow == qblk) | (srow == qblk - 1)
    valid = (srow << 6) <= qpos
    imp_t = jnp.where(valid, imp_t + jnp.where(forced, FORCE_SCORE, 0.0), NEG_INF)
    n_rows = imp_t.shape[0]
    assert n_rows % 8 == 0
    slabs = [imp_t[a:a + 8] for a in range(0, n_rows, 8)]
    rows8 = [srow[a:a + 8] for a in range(0, n_rows, 8)]
    cnts = [jnp.zeros(x.shape, jnp.int32) for x in slabs]
    for s in range(n_rank):
        r = imp_t[s:s + 1, :]
        for j, x in enumerate(slabs):
            if 8 * j > s:
                beats = r >= x
            elif 8 * j + 7 <= s:
                beats = r > x
            else:
                beats = (r > x) | ((r == x) & (rows8[j] > s))
            cnts[j] = cnts[j] + jnp.where(beats, 1, 0)
    cnt = jnp.concatenate(cnts, axis=0)
    return jnp.where((cnt < N_SEL) & valid, 1.0, 0.0)


def _exp_pv(s, m, v):
    return _dot(jnp.exp((s - m).astype(BF16)), v)


def _normalize_pv(pv):
    return pv / jnp.maximum(pv[:, HEAD_DIM:HEAD_DIM + 1], 1e-30)


def _attn_prompt_body(kt, n_slc, q_ref, gt_ref, kcmp_ref, vcmp_ref, ks_ref, vs_ref, kw_ref, vw_ref,
                      ctab_ref, ntab_ref, wtab_ref, ovl_ref, o_ref, sa_ref, sb_ref):
    i = pl.program_id(2)
    qb = Q_BLOCK
    rows = GROUP * qb
    n_cmp_pad = kcmp_ref.shape[2]
    first_near_block = 2 * jnp.maximum(i - 1, 0)
    near_start = pl.multiple_of(jnp.maximum(i - 1, 0) * qb, qb)
    win_start = pl.multiple_of(jnp.maximum(i * qb - WINDOW, 0), qb)
    band = WINDOW + qb
    gates = gt_ref[0]
    lane = lax.broadcasted_iota(jnp.int32, (qb, LANES), 1)
    q0 = q_ref[0].reshape(rows, LANES)

    s = _dot_nt(q0, kw_ref[0, pl.ds(win_start, band), :]) + wtab_ref[0].reshape(rows, band)
    m = jnp.max(s, axis=-1, keepdims=True)
    m = jnp.where(m == NEG_INF, 0.0, m)
    o_w = _normalize_pv(_exp_pv(s, m, vw_ref[0, pl.ds(win_start, band), :]))

    per_qb = qb // CMP_STRIDE
    ctab = pltpu.roll(ctab_ref[...].reshape(rows, 2 * n_cmp_pad), i * per_qb, axis=1)[:, n_cmp_pad:]
    s = _dot_nt(q0, kcmp_ref[0, 0]) + ctab
    p, l = _softmax_parts(s)
    pn = p / jnp.maximum(l, 1e-30)
    o_c = _dot(pn.astype(BF16), vcmp_ref[0, 0])

    psum = pn[0:qb] + pn[qb:2 * qb] + pn[2 * qb:3 * qb] + pn[3 * qb:4 * qb]
    hi, mid, lo = _split3(psum)
    ovl = ovl_ref[...]
    imp_t = _dot_nt(ovl, hi) + _dot_nt(ovl, mid) + _dot_nt(ovl, lo)
    srow = lax.broadcasted_iota(jnp.int32, (n_slc, qb), 0)
    qpos_t = i * qb + lax.broadcasted_iota(jnp.int32, (n_slc, qb), 1)
    sel_t = _select_blocks(imp_t, srow, qpos_t, n_slc)
    sel_t = jnp.concatenate([sel_t, jnp.zeros((LANES - n_slc, qb), F32)], axis=0)
    sel = sel_t.T

    def query_with_mask(keep):
        m = pltpu.roll(jnp.where(keep, 0.0, MASKED), HEAD_DIM, axis=1).astype(BF16)
        m = jnp.concatenate([jnp.where(lane < HEAD_DIM, q0[g * qb:(g + 1) * qb], m) for g in range(GROUP)], axis=0)
        return m

    q_near = query_with_mask(sel > 0.5)
    q_far = query_with_mask((sel > 0.5) & (lane < first_near_block))

    s = _dot_nt(q_near, ks_ref[0, pl.ds(near_start, 2 * qb), :]) + ntab_ref[0].reshape(rows, 2 * qb)
    m0 = jnp.max(s, axis=-1, keepdims=True)
    m0 = jnp.where(m0 == NEG_INF, 0.0, m0)
    a0 = _exp_pv(s, m0, vs_ref[0, pl.ds(near_start, 2 * qb), :])

    n_kt = ks_ref.shape[1] // kt

    def scores(t):
        k0 = pl.multiple_of(jnp.minimum(t, n_kt - 1) * kt, kt)
        return _dot_nt(q_far, ks_ref[0, pl.ds(k0, kt), :])

    def consume(t, s, m_old, acc):
        k0 = pl.multiple_of(jnp.minimum(t, n_kt - 1) * kt, kt)
        m_new = jnp.maximum(m_old, jnp.max(s, axis=-1, keepdims=True))
        return m_new, jnp.exp(m_old - m_new) * acc + _exp_pv(s, m_new, vs_ref[0, pl.ds(k0, kt), :])

    def far_pair(u, carry):
        m, acc = carry
        sb_ref[...] = scores(2 * u + 1)
        m, acc = consume(2 * u, sa_ref[...], m, acc)
        sa_ref[...] = scores(2 * u + 2)
        return consume(2 * u + 1, sb_ref[...], m, acc)

    n_far = (near_start + kt - 1) // kt
    sa_ref[...] = scores(0)
    _, acc_s = lax.fori_loop(0, (n_far + 1) // 2, far_pair, (m0, a0))
    o_s = _normalize_pv(acc_s)

    heads_out = []
    for g in range(GROUP):
        c = g * N_BRANCH
        sl = slice(g * qb, (g + 1) * qb)
        heads_out.append(gates[:, c:c + 1] * o_c[sl] + gates[:, c + 1:c + 2] * o_s[sl] + gates[:, c + 2:c + 3] * o_w[sl])
    tiles = [jnp.where(lane < HEAD_DIM, heads_out[2 * j], pltpu.roll(heads_out[2 * j + 1], HEAD_DIM, axis=1))
             for j in range(GROUP // 2)]
    o_ref[...] = jnp.concatenate(tiles, axis=-1).astype(BF16)


def _attn_prompt(qp, gates, kcmp, vcmp, ksb, vsb, kwb, vwb, fext, batch, seq):
    qb = Q_BLOCK
    nqb = seq // qb
    n_slc = seq // SLC_BLOCK
    assert n_slc <= LANES - HEAD_DIM
    n_cmp_pad = kcmp.shape[2]
    n_cmp = n_cmp_pad - 1
    kt = min(512, seq)
    assert (seq // kt) % 2 == 0
    band = WINDOW + qb
    iq = np.arange(qb)

    per_qb = qb // CMP_STRIDE
    half = 2 * per_qb
    m = np.arange(-half, half)
    d = iq[:, None] - (m[None, :] * CMP_STRIDE + CMP_BLOCK - 1)
    assert d[:, 0].min() >= MAX_DISTANCE and d[:, -1].max() < 0
    ctab = jnp.concatenate([jnp.zeros((N_HEADS, qb, n_cmp_pad - half), F32), _bias_table(fext, d, d >= 0),
                            jnp.full((N_HEADS, qb, n_cmp_pad - half), NEG_INF, F32)], axis=2)
    assert per_qb * (nqb - 1) < n_cmp_pad
    assert (n_cmp_pad - 1) * CMP_STRIDE + CMP_BLOCK - 1 >= seq and n_cmp == n_cmp_pad - 1

    nv = WINDOW // qb + 1
    kw_ = WINDOW + band
    dj = np.arange(qb + kw_ - 1) - (kw_ - 1) + WINDOW
    wide = _toeplitz(_bias_table(fext, dj, (dj >= 0) & (dj < WINDOW)), qb, kw_)
    wtab = jnp.stack([wide[:, :, WINDOW - qb * v:WINDOW - qb * v + band] for v in range(nv)], axis=0)
    ntab = jnp.stack([wide[:, :, WINDOW - qb * v:WINDOW - qb * v + 2 * qb] for v in range(2)], axis=0)
    assert 3 * qb <= WINDOW

    c_start = np.arange(n_cmp_pad) * CMP_STRIDE
    s_start = np.arange(n_slc) * SLC_BLOCK
    ovl = ((c_start[None, :] < s_start[:, None] + SLC_BLOCK) & (c_start[None, :] + CMP_BLOCK > s_start[:, None])
           & (np.arange(n_cmp_pad) < n_cmp)[None, :])
    ovl = jnp.asarray(ovl, BF16)

    per_head = lambda: pl.BlockSpec((1, seq, LANES), lambda b, h, i: (h, b, 0))
    return pl.pallas_call(
        functools.partial(_attn_prompt_body, kt, n_slc),
        grid=(batch, N_KV_HEADS, nqb),
        in_specs=[pl.BlockSpec((1, GROUP, qb, LANES), lambda b, h, i: (b, h, i, 0)),
                  pl.BlockSpec((1, qb, LANES), lambda b, h, i: (h, b * nqb + i, 0)),
                  pl.BlockSpec((1, 1, n_cmp_pad, LANES), lambda b, h, i: (b, h, 0, 0)),
                  pl.BlockSpec((1, 1, n_cmp_pad, LANES), lambda b, h, i: (b, h, 0, 0)),
                  per_head(), per_head(), per_head(), per_head(),
                  pl.BlockSpec((GROUP, qb, 2 * n_cmp_pad), lambda b, h, i: (h, 0, 0)),
                  pl.BlockSpec((1, GROUP, qb, 2 * qb), lambda b, h, i: (jnp.minimum(i, 1), h, 0, 0)),
                  pl.BlockSpec((1, GROUP, qb, band), lambda b, h, i: (jnp.minimum(i, nv - 1), h, 0, 0)),
                  pl.BlockSpec(ovl.shape, lambda b, h, i: (0, 0))],
        out_specs=pl.BlockSpec((qb, GROUP * HEAD_DIM), lambda b, h, i: (b * nqb + i, h)),
        out_shape=jax.ShapeDtypeStruct((batch * seq, ATTN_DIM), BF16),
        scratch_shapes=[pltpu.VMEM((GROUP * qb, kt), F32)] * 2,
        compiler_params=_cparams("arbitrary", "arbitrary", "arbitrary"),
        name="attn_prompt",
    )(qp, gates, kcmp, vcmp, ksb, vsb, kwb, vwb, ctab, ntab, wtab, ovl)


def _attn_sample_body(n_seq, n_pages, ts, past_len, pt_ref, *refs):
    n_ops = n_seq * n_pages
    for b in range(n_seq):
        _attn_sample_one(b, refs[b * n_pages:(b + 1) * n_pages], refs[n_ops + b * n_pages:n_ops + (b + 1) * n_pages],
                         refs[2 * n_ops:], ts, past_len)


def _attn_sample_one(b, kpages, vpages, refs, ts, past_len):
    (q_ref, gt_ref, kcmp_ref, vcmp_ref, ksn_ref, vsn_ref, kwc_ref, vwc_ref, kwn_ref, vwn_ref,
     ctab_ref, stab_ref, sntab_ref, wtab_ref, wntab_ref, eexp_ref, ovl_ref, o_ref) = refs
    rows = GROUP * N_KV_HEADS * ts
    rq = N_KV_HEADS * ts
    q = q_ref[b]
    gates = gt_ref[b]

    s = _dot_nt(q, kcmp_ref[b].astype(BF16)) + ctab_ref[...]
    p, l = _softmax_parts(s)
    pn = p / jnp.maximum(l, 1e-30)
    o_c = _dot(pn.astype(BF16), vcmp_ref[b].astype(BF16))

    psum = pn[0:rq]
    for g in range(1, GROUP):
        psum = psum + pn[g * rq:(g + 1) * rq]
    hi, mid, lo = _split3(psum)
    ovl = ovl_ref[...]
    imp = _dot(hi, ovl) + _dot(mid, ovl) + _dot(lo, ovl)
    n_slc = -(-(past_len + ts) // SLC_BLOCK)
    blk = lax.broadcasted_iota(jnp.int32, (rq, LANES), 1)
    qpos = past_len + (lax.broadcasted_iota(jnp.int32, (rq, LANES), 0) & (ts - 1))
    qblk = qpos >> 6
    forced = (blk == 0) | (blk == qblk) | (blk == qblk - 1)
    valid = ((blk << 6) <= qpos) & (blk < n_slc)
    imp = jnp.where(valid, imp + jnp.where(forced, FORCE_SCORE, 0.0), NEG_INF)
    cnt = jnp.zeros((rq, LANES), jnp.int32)
    for sidx in range(n_slc):
        r = imp[:, sidx:sidx + 1]
        beats = (r > imp) | ((r == imp) & (blk > sidx))
        cnt = cnt + jnp.where(beats, 1, 0)
    sel = jnp.where((cnt < N_SEL) & valid, 1.0, 0.0)
    sel = jnp.concatenate([sel] * GROUP, axis=0).astype(BF16)

    kc_t = jnp.concatenate([p_[0] for p_ in kpages], axis=1).astype(BF16)
    vc_t = jnp.concatenate([p_[0] for p_ in vpages], axis=1).astype(BF16)
    mexp = _dot(sel, eexp_ref[...])
    s1 = jnp.where(mexp > 0.5, _dot(q, kc_t) + stab_ref[...], NEG_INF)
    last = sel[:, n_slc - 1:n_slc].astype(F32)
    s2 = jnp.where(last > 0.5, _dot_nt(q, ksn_ref[b].astype(BF16)) + sntab_ref[...], NEG_INF)
    m = jnp.maximum(jnp.max(s1, axis=-1, keepdims=True), jnp.max(s2, axis=-1, keepdims=True))
    m = jnp.where(m == NEG_INF, 0.0, m)
    p1 = jnp.exp(s1 - m)
    p2 = jnp.exp(s2 - m)
    l = jnp.sum(p1, axis=-1, keepdims=True) + jnp.sum(p2, axis=-1, keepdims=True)
    o_s = (_dot_nt(p1.astype(BF16), vc_t) + _dot(p2.astype(BF16), vsn_ref[b].astype(BF16))) / jnp.maximum(l, 1e-30)

    s1 = _dot(q, kwc_ref[b].astype(BF16)) + wtab_ref[...]
    s2 = _dot_nt(q, kwn_ref[b].astype(BF16)) + wntab_ref[...]
    m = jnp.maximum(jnp.max(s1, axis=-1, keepdims=True), jnp.max(s2, axis=-1, keepdims=True))
    m = jnp.where(m == NEG_INF, 0.0, m)
    p1 = jnp.exp(s1 - m)
    p2 = jnp.exp(s2 - m)
    l = jnp.sum(p1, axis=-1, keepdims=True) + jnp.sum(p2, axis=-1, keepdims=True)
    o_w = (_dot_nt(p1.astype(BF16), vwc_ref[b].astype(BF16))
           + _dot(p2.astype(BF16), vwn_ref[b].astype(BF16))) / jnp.maximum(l, 1e-30)

    o_ref[b] = gates[:, 0:1] * o_c + gates[:, 1:2] * o_s + gates[:, 2:3] * o_w


def _attn_sample(q_s, gates_s, kcmp, vcmp, ks_new, vs_new, kw_new, vw_new, cache_ks, cache_vs,
                 cache_kw, cache_vw, pt_flat, fext, n_batch, ts, n_pages, page):
    past_len = n_pages * page
    w_buf = cache_kw.shape[2]
    rows = GROUP * N_KV_HEADS * ts
    n_new = 8
    n_cmp_pad = kcmp.shape[1]
    n_cmp = n_cmp_pad - 1
    n_slc = -(-(past_len + ts) // SLC_BLOCK)

    q5 = q_s.reshape(n_batch, ts, N_KV_HEADS, GROUP, HEAD_DIM).transpose(0, 3, 2, 1, 4)
    eye = jnp.eye(N_KV_HEADS, dtype=q_s.dtype)
    qr = jnp.einsum("bghtd,hk->bghtkd", q5, eye).reshape(n_batch, rows, LANES).astype(BF16)
    g5 = gates_s[:, :N_HEADS * N_BRANCH].reshape(n_batch, ts, N_KV_HEADS, GROUP, N_BRANCH).transpose(0, 3, 2, 1, 4)
    gr = jnp.pad(g5.reshape(n_batch, rows, N_BRANCH), ((0, 0), (0, 0), (0, LANES - N_BRANCH)))
    pad_new = lambda a: jnp.pad(a.reshape(n_batch, ts, KV_DIM), ((0, 0), (0, n_new - ts), (0, 0)))
    ks_new, vs_new, kw_new, vw_new = map(pad_new, (ks_new, vs_new, kw_new, vw_new))

    g_i, h_i, t_i = np.meshgrid(np.arange(GROUP), np.arange(N_KV_HEADS), np.arange(ts), indexing="ij")
    head = (h_i * GROUP + g_i).reshape(rows)
    tq = t_i.reshape(rows)
    pos_q = past_len + tq

    f_rows = fext[jnp.asarray(head)]

    def table(d, valid):
        return jnp.take_along_axis(f_rows, jnp.asarray(_bias_index(d, valid)), axis=1)

    nn = np.arange(n_cmp_pad)
    d = pos_q[:, None] - (nn[None, :] * CMP_STRIDE + CMP_BLOCK - 1)
    ctab = table(d, (d >= 0) & (nn < n_cmp)[None, :])
    near = np.arange(past_len - MAX_DISTANCE, past_len)
    d = pos_q[:, None] - near[None, :]
    assert past_len >= MAX_DISTANCE and d.min() >= 0
    stab = jnp.concatenate([jnp.zeros((rows, past_len - MAX_DISTANCE), F32), table(d, d >= 0)], axis=1)
    jn = np.arange(n_new)
    d = tq[:, None] - jn[None, :]
    sntab = table(d, (d >= 0) & (jn < ts)[None, :])
    pos_w = past_len - w_buf + np.arange(w_buf)
    d = pos_q[:, None] - pos_w[None, :]
    wtab = table(d, (d >= 0) & (d < WINDOW) & (pos_w >= 0)[None, :])
    wntab = table(tq[:, None] - jn[None, :], (tq[:, None] >= jn[None, :]) & (jn < ts)[None, :])

    eexp = jnp.asarray(np.arange(past_len)[None, :] // SLC_BLOCK == np.arange(LANES)[:, None], BF16)
    c_start = nn * CMP_STRIDE
    s_start = np.arange(LANES) * SLC_BLOCK
    ovl = jnp.asarray((c_start[:, None] < s_start[None, :] + SLC_BLOCK) & (c_start[:, None] + CMP_BLOCK > s_start[None, :])
                      & (nn < n_cmp)[:, None] & (np.arange(LANES) < n_slc)[None, :], BF16)

    n_seq = SAMPLE_SEQS if n_batch % SAMPLE_SEQS == 0 else 1
    n_ops = n_seq * n_pages
    const = lambda a: pl.BlockSpec(a.shape, lambda b, pt: (0,) * a.ndim)
    per_b = lambda a: pl.BlockSpec((n_seq,) + a.shape[1:], lambda b, pt: (b,) + (0,) * (a.ndim - 1))
    page_spec = lambda j: pl.BlockSpec((1, KV_DIM, page), lambda b, pt: (pt[b * n_ops + j], 0, 0))
    small = [qr, gr, kcmp, vcmp, ks_new, vs_new, cache_kw, cache_vw, kw_new, vw_new]
    consts = [ctab, stab, sntab, wtab, wntab, eexp, ovl]
    grid_spec = pltpu.PrefetchScalarGridSpec(
        num_scalar_prefetch=1, grid=(n_batch // n_seq,),
        in_specs=[page_spec(j) for j in range(n_ops)] * 2 + [per_b(a) for a in small] + [const(a) for a in consts],
        out_specs=pl.BlockSpec((n_seq, rows, LANES), lambda b, pt: (b, 0, 0)))
    o = pl.pallas_call(
        functools.partial(_attn_sample_body, n_seq, n_pages, ts, past_len),
        grid_spec=grid_spec,
        out_shape=jax.ShapeDtypeStruct((n_batch, rows, LANES), F32),
        compiler_params=_cparams("arbitrary"),
        name="attn_sample",
    )(pt_flat, *([cache_ks] * n_ops), *([cache_vs] * n_ops), *small, *consts)
    o6 = o.reshape(n_batch, GROUP, N_KV_HEADS, ts, N_KV_HEADS, HEAD_DIM)
    o5 = jnp.stack([o6[:, :, h, :, h] for h in range(N_KV_HEADS)], axis=2)
    return o5.transpose(0, 3, 2, 1, 4).reshape(n_batch * ts, ATTN_DIM).astype(BF16)


def _post1_body(tm, x_ref, co_ref, at_ref, wo_ref, nf_ref, wr_ref, br_ref, tri_ref, run0_ref,
                h_ref, hn_ref, rt_ref, cnt_ref, run_ref):
    @pl.when(pl.program_id(0) == 0)
    def _():
        run_ref[...] = run0_ref[...]

    h = x_ref[...] + _dot(co_ref[...], wo_ref[0:CONV_DIM]) + _dot(at_ref[...], wo_ref[CONV_DIM:CONV_DIM + ATTN_DIM])
    hn = _rms(h, nf_ref[...])
    h_ref[...] = h
    _store_token_tiles(hn_ref, hn, tm)

    hi = hn.astype(BF16)
    lo = (hn - hi.astype(F32)).astype(BF16)
    wr = wr_ref[...]
    whi = wr.astype(BF16)
    wlo = (wr - whi.astype(F32)).astype(BF16)
    logits = _dot(hi, whi) + _dot(lo, whi) + _dot(hi, wlo) + br_ref[...]

    lane_i = lax.broadcasted_iota(jnp.int32, (tm, LANES), 1)
    lane = lane_i.astype(F32)
    big = float(LANES)
    gmask = (lane_i >= ROUTER_GROUP_LANE) & (lane_i < ROUTER_GROUP_LANE + N_GROUPS)
    lg = jnp.where(gmask, logits, NEG_INF)
    eg = jnp.exp(lg - jnp.max(lg, axis=-1, keepdims=True))
    pg = eg / jnp.sum(eg, axis=-1, keepdims=True)
    gw = jnp.max(pg, axis=-1, keepdims=True)
    grp = jnp.min(jnp.where(gmask & (pg == gw), lane, big), axis=-1, keepdims=True) - ROUTER_GROUP_LANE

    group_of_lane = (lane_i >> 3).astype(F32)
    emask = (lane_i < N_EXPERTS) & (group_of_lane == grp)
    le = jnp.where(emask, logits, NEG_INF)
    ee = jnp.exp(le - jnp.max(le, axis=-1, keepdims=True))
    pe = jnp.where(emask, ee / jnp.sum(ee, axis=-1, keepdims=True), -1.0)
    v1 = jnp.max(pe, axis=-1, keepdims=True)
    i1 = jnp.min(jnp.where(pe == v1, lane, big), axis=-1, keepdims=True)
    pe2 = jnp.where(lane == i1, -1.0, pe)
    v2 = jnp.max(pe2, axis=-1, keepdims=True)
    i2 = jnp.min(jnp.where(pe2 == v2, lane, big), axis=-1, keepdims=True)
    tot = v1 + v2
    w1 = v1 / tot * gw
    w2 = v2 / tot * gw

    oh1 = jnp.where(lane == i1, 1.0, 0.0)
    oh2 = jnp.where(lane == i2, 1.0, 0.0)
    both = oh1 + oh2
    before = _dot(tri_ref[...], both.astype(BF16)) + run_ref[0:1]
    r1 = jnp.sum(oh1 * before, axis=-1, keepdims=True)
    r2 = jnp.sum(oh2 * before, axis=-1, keepdims=True)
    run = run_ref[0:1] + jnp.sum(both, axis=0, keepdims=True)
    run_ref[...] = jnp.broadcast_to(run, run_ref.shape)
    cnt_ref[...] = jnp.broadcast_to(run, cnt_ref.shape)

    rt = jnp.where(lane_i == 0, i1, 0.0)
    rt = jnp.where(lane_i == 1, i2, rt)
    rt = jnp.where(lane_i == 2, r1, rt)
    rt = jnp.where(lane_i == 3, r2, rt)
    rt = jnp.where(lane_i == 4, w1, rt)
    rt = jnp.where(lane_i == 5, w2, rt)
    rt_ref[...] = rt


def _post1(x2d, co, at, wo, nf, wr, br, run0, tm):
    n, d = x2d.shape
    tri = jnp.asarray(np.tril(np.ones((tm, tm), np.float32), -1), BF16)
    rows = lambda w: pl.BlockSpec((tm, w), lambda i: (i, 0))
    const = lambda a: pl.BlockSpec(a.shape, lambda i: (0,) * a.ndim)
    return pl.pallas_call(
        functools.partial(_post1_body, tm),
        grid=(n // tm,),
        in_specs=[rows(d), rows(CONV_DIM), rows(ATTN_DIM), const(wo), const(nf), const(wr), const(br),
                  const(tri), const(run0)],
        out_specs=[rows(d), pl.BlockSpec((tm * TOKEN_TILE_ROWS, LANES), lambda i: (i, 0)), rows(LANES),
                   pl.BlockSpec((8, LANES), lambda i: (0, 0))],
        out_shape=[jax.ShapeDtypeStruct((n, d), F32), jax.ShapeDtypeStruct((n * TOKEN_TILE_ROWS, LANES), F32),
                   jax.ShapeDtypeStruct((n, LANES), F32), jax.ShapeDtypeStruct((8, LANES), F32)],
        scratch_shapes=[pltpu.VMEM((8, LANES), F32)],
        compiler_params=_cparams("arbitrary"),
        name="post1",
    )(x2d, co, at, wo, nf, wr, br, tri, run0)


def _token_copy(src_ref, dst_ref, s, d, sem):
    r = TOKEN_TILE_ROWS
    return pltpu.make_async_copy(src_ref.at[pl.ds(pl.multiple_of(s * r, r), r)],
                                 dst_ref.at[pl.ds(pl.multiple_of(d * r, r), r)], sem)


def _scatter_rows_body(ts, dest_ref, src_ref, init_ref, out_ref, sem):
    del init_ref
    base = pl.program_id(0) * (2 * ts)

    def issue(t, _):
        for k in range(2):
            _token_copy(src_ref, out_ref, t, dest_ref[base + 2 * t + k], sem).start(priority=k)
        return 0

    lax.fori_loop(0, ts, issue, 0)
    for _ in range(2):
        pltpu.make_async_copy(src_ref, out_ref.at[pl.ds(0, ts * TOKEN_TILE_ROWS)], sem).wait()


def _scatter_rows(dest, src, slots):
    n_tok = dest.shape[0] // 2
    ts = min(SCATTER_TOKENS, n_tok)
    assert n_tok % ts == 0
    any_spec = pl.BlockSpec(memory_space=pl.ANY)
    return pl.pallas_call(
        functools.partial(_scatter_rows_body, ts),
        grid_spec=pltpu.PrefetchScalarGridSpec(
            num_scalar_prefetch=1, grid=(n_tok // ts,),
            in_specs=[pl.BlockSpec((ts * TOKEN_TILE_ROWS, LANES), lambda i, dest: (i, 0)), any_spec],
            out_specs=any_spec, scratch_shapes=[pltpu.SemaphoreType.DMA(())]),
        out_shape=jax.ShapeDtypeStruct(slots.shape, slots.dtype),
        input_output_aliases={2: 0},
        compiler_params=pltpu.CompilerParams(dimension_semantics=("arbitrary",)),
        name="scatter_rows",
    )(dest, src, slots)


def _experts_body(be_ref, nu_ref, x_ref, wg_ref, wu_ref, wd_ref, o_ref, wg_s, wu_s, wd_s):
    i = pl.program_id(0)

    @pl.when(i < nu_ref[0])
    def _():
        prev = be_ref[jnp.maximum(i - 1, 0)]

        @pl.when((i == 0) | (be_ref[i] != prev))
        def _():
            wg_s[...] = wg_ref[0].astype(BF16)
            wu_s[...] = wu_ref[0].astype(BF16)
            wd_s[...] = wd_ref[0].astype(BF16)

        x = _load_token_tiles(x_ref, EXPERT_ROWS, TOKEN_TILE_ROWS).astype(BF16)
        g = _dot(x, wg_s[...])
        u = _dot(x, wu_s[...])
        a = g * _sigmoid(g) * u
        _store_token_tiles(o_ref, _dot(a.astype(BF16), wd_s[...]), EXPERT_ROWS)

    @pl.when(i >= nu_ref[0])
    def _():
        o_ref[...] = jnp.zeros_like(o_ref)


def _experts(blk_expert, n_used, xs, wg, wu, wd):
    blk_rows = EXPERT_ROWS * TOKEN_TILE_ROWS
    n_blk = xs.shape[0] // blk_rows
    _, d, de = wg.shape
    xmap = lambda i, be, nu: (jnp.minimum(i, jnp.maximum(nu[0] - 1, 0)), 0)
    wmap = lambda i, be, nu: (be[jnp.minimum(i, jnp.maximum(nu[0] - 1, 0))], 0, 0)
    grid_spec = pltpu.PrefetchScalarGridSpec(
        num_scalar_prefetch=2, grid=(n_blk,),
        in_specs=[pl.BlockSpec((blk_rows, LANES), xmap), pl.BlockSpec((1, d, de), wmap),
                  pl.BlockSpec((1, d, de), wmap), pl.BlockSpec((1, de, d), wmap)],
        out_specs=pl.BlockSpec((blk_rows, LANES), lambda i, be, nu: (i, 0)),
        scratch_shapes=[pltpu.VMEM((d, de), BF16), pltpu.VMEM((d, de), BF16), pltpu.VMEM((de, d), BF16)])
    return pl.pallas_call(
        _experts_body, grid_spec=grid_spec,
        out_shape=jax.ShapeDtypeStruct(xs.shape, F32),
        compiler_params=_cparams("arbitrary"),
        name="experts",
    )(blk_expert, n_used, xs, wg, wu, wd)


def _post2_body(tm, dest_ref, h_ref, rt_ref, p_ref, yb_ref, wple_ref, wpg_ref, bpg_ref, np_ref, o_ref, buf, sem):
    i = pl.program_id(0)
    n = pl.num_programs(0)

    def fetch(step, slot):
        base = step * (2 * tm)

        def issue(t, _):
            for k in range(2):
                _token_copy(yb_ref, buf.at[slot], dest_ref[base + 2 * t + k], k * tm + t, sem.at[slot]).start(priority=k)
            return 0

        lax.fori_loop(0, tm, issue, 0)

    @pl.when(i == 0)
    def _():
        fetch(0, 0)

    @pl.when(i + 1 < n)
    def _():
        fetch(i + 1, (i + 1) & 1)

    slot = i & 1

    pltpu.make_async_copy(yb_ref.at[pl.ds(0, 2 * tm * TOKEN_TILE_ROWS)], buf.at[slot], sem.at[slot]).wait()
    rt = rt_ref[...]
    y0 = _load_token_tiles(buf.at[slot], tm, TOKEN_TILE_ROWS)
    y1 = _load_token_tiles(buf.at[slot], tm, TOKEN_TILE_ROWS, first=tm * TOKEN_TILE_ROWS)
    h = h_ref[...] + (y0 * rt[:, 4:5] + y1 * rt[:, 5:6])
    gate = _sigmoid(_dot(_rms(h, np_ref[...]).astype(BF16), wpg_ref[...]) + bpg_ref[...])
    o_ref[...] = h + gate * _dot(p_ref[...].astype(BF16), wple_ref[...])


def _post2(dest, h, rt, p2d, yb, wple, wpg, bpg, npl, tm):
    n, d = h.shape
    rows = lambda w: pl.BlockSpec((tm, w), lambda i, dest: (i, 0))
    const = lambda a: pl.BlockSpec(a.shape, lambda i, dest: (0,) * a.ndim)
    grid_spec = pltpu.PrefetchScalarGridSpec(
        num_scalar_prefetch=1, grid=(n // tm,),
        in_specs=[rows(d), rows(LANES), rows(p2d.shape[1]), pl.BlockSpec(memory_space=pl.ANY), const(wple),
                  const(wpg), const(bpg), const(npl)],
        out_specs=rows(d),
        scratch_shapes=[pltpu.VMEM((2, 2 * tm * TOKEN_TILE_ROWS, LANES), F32), pltpu.SemaphoreType.DMA((2,))])
    return pl.pallas_call(
        functools.partial(_post2_body, tm),
        grid_spec=grid_spec,
        out_shape=jax.ShapeDtypeStruct((n, d), F32),
        compiler_params=_cparams("arbitrary"),
        name="post2",
    )(dest, h, rt, p2d, yb, wple, wpg, bpg, npl)


def _row_tile(n, cap=512):
    t = min(cap, n)
    assert n % t == 0 and t % 8 == 0
    return t


def kernel(x_prompt, x_sample, p_prompt, p_sample, cache_k_cmp, cache_v_cmp, cache_k_slc, cache_v_slc, cache_k_win, cache_v_win, state_conv, page_table, w_in, w_out, conv_w, norm_mix, norm_ffn, norm_ple, q_norm, k_norm, cmp_pe_k, cmp_w1_k, cmp_w2_k, cmp_pe_v, cmp_w1_v, cmp_w2_v, rel_bias, w_router_group, b_router_group, w_router_expert, b_router_expert, w_exp_gate, w_exp_up, w_exp_down, w_ple, w_ple_gate, b_ple_gate):
    assert w_in.shape[0] == 1, "single-layer step"
    bp, t, d = x_prompt.shape
    bs, ts, _ = x_sample.shape
    n_pages = page_table.shape[1]
    page = cache_k_cmp.shape[2]
    past_len = n_pages * page
    w_buf = cache_k_win.shape[2]
    n_phys = cache_k_cmp.shape[1]
    assert t % Q_BLOCK == 0 and t >= WINDOW + Q_BLOCK and page % CMP_STRIDE == 0 and ts == 4 and d == D_MODEL
    assert past_len % SLC_BLOCK == 0
    np_rows, ns_rows = bp * t, bs * ts

    row = lambda v: v.reshape(1, -1).astype(F32)
    w_in_b = jnp.pad(w_in[0], ((0, 0), (0, Z_COLS - w_in.shape[2]))).astype(BF16)
    qn = row(jnp.tile(q_norm[0], N_HEADS))
    kn1 = row(jnp.tile(k_norm[0, 1], N_KV_HEADS))
    kn2 = row(jnp.tile(k_norm[0, 2], N_KV_HEADS))
    bd = jnp.asarray(np.kron(np.eye(N_HEADS), np.ones((HEAD_DIM, HEAD_DIM))), BF16)
    pw = (row(norm_mix[0]), w_in_b, qn, kn1, kn2, conv_w[0].astype(F32), bd)
    cw_k = _compress_weights(cmp_pe_k[0], cmp_w1_k[0], cmp_w2_k[0])
    cw_v = _compress_weights(cmp_pe_v[0], cmp_w1_v[0], cmp_w2_v[0])
    kn0 = row(k_norm[0, 0])
    fext = _bias_by_distance(rel_bias)
    pt_flat = page_table.reshape(-1).astype(jnp.int32)
    wr = jnp.zeros((d, LANES), F32).at[:, :N_EXPERTS].set(w_router_expert[0])
    wr = wr.at[:, ROUTER_GROUP_LANE:ROUTER_GROUP_LANE + N_GROUPS].set(w_router_group[0])
    br = jnp.zeros((1, LANES), F32).at[0, :N_EXPERTS].set(b_router_expert[0])
    br = br.at[0, ROUTER_GROUP_LANE:ROUTER_GROUP_LANE + N_GROUPS].set(b_router_group[0])
    wo_b = w_out[0].astype(BF16)
    wple_b = w_ple[0].astype(BF16)
    wpg_b = w_ple_gate[0].astype(BF16)

    tm_p = _row_tile(t, cap=1024)
    (co_p, q_p, kc_p, vc_p, ks_p, vs_p, kw_p, vw_p, gt_p, cs_p, ksb, vsb, kwb, vwb, kc_t, vc_t, ks_t, vs_t) = _project(
        x_prompt.reshape(np_rows, d), bp, t, tm_p, pw)
    chunk_w = CMP_STRIDE * KV_DIM
    kcmp_p = _compress_rows(kc_p.reshape(bp, t // CMP_STRIDE, chunk_w), cw_k, kn0, True)
    vcmp_p = _compress_rows(vc_p.reshape(bp, t // CMP_STRIDE, chunk_w), cw_v, kn0, False)
    at_p = _attn_prompt(q_p, gt_p, kcmp_p, vcmp_p, ksb, vsb, kwb, vwb, fext, bp, t)

    st = state_conv[0].astype(F32)
    s0 = jnp.repeat(st[:, 0], ts, axis=0)
    s1 = jnp.repeat(st[:, 1], ts, axis=0)
    (co_s, q_s, kc_s, vc_s, ks_s, vs_s, kw_s, vw_s, gt_s, u_s) = _project(
        x_sample.reshape(ns_rows, d), bs, ts, ns_rows, pw, state=(s0, s1))
    feature_major = lambda c, n, rows_: jnp.transpose(c[0], (0, 2, 3, 1)).reshape(n, KV_DIM, rows_)
    kcmp_s = _compress_pages(feature_major(cache_k_cmp, n_phys, page), pt_flat, bs, n_pages, cw_k, kn0, True)
    vcmp_s = _compress_pages(feature_major(cache_v_cmp, n_phys, page), pt_flat, bs, n_pages, cw_v, kn0, False)
    at_s = _attn_sample(q_s, gt_s, kcmp_s, vcmp_s, ks_s, vs_s, kw_s, vw_s,
                        feature_major(cache_k_slc, n_phys, page), feature_major(cache_v_slc, n_phys, page),
                        feature_major(cache_k_win, bs, w_buf), feature_major(cache_v_win, bs, w_buf),
                        pt_flat, fext, bs, ts, n_pages, page)

    tp1 = _row_tile(np_rows)
    ts1 = _row_tile(ns_rows)
    nf = row(norm_ffn[0])
    h_p, hn_p, rt_p, cnt_p = _post1(x_prompt.reshape(np_rows, d), co_p, at_p, wo_b, nf, wr, br,
                                    jnp.zeros((8, LANES), F32), tp1)
    h_s, hn_s, rt_s, cnt_s = _post1(x_sample.reshape(ns_rows, d), co_s, at_s, wo_b, nf, wr, br, cnt_p, ts1)

    counts = cnt_s[0, :N_EXPERTS].astype(jnp.int32)
    padded = (counts + EXPERT_ROWS - 1) // EXPERT_ROWS * EXPERT_ROWS
    pad_end = jnp.cumsum(padded)
    pad_start = pad_end - padded
    n_assign = 2 * (np_rows + ns_rows)
    n_blk = (n_assign + N_EXPERTS * (EXPERT_ROWS - 1) + EXPERT_ROWS - 1) // EXPERT_ROWS
    blk_first = jnp.arange(n_blk, dtype=jnp.int32) * EXPERT_ROWS
    blk_expert = jnp.minimum(jnp.sum((pad_end[None, :] <= blk_first[:, None]).astype(jnp.int32), axis=1),
                             N_EXPERTS - 1)
    n_used = (pad_end[-1:] // EXPERT_ROWS).astype(jnp.int32)

    def dest_of(rt):
        e = rt[:, 0:2].astype(jnp.int32).reshape(-1)
        return pad_start[e] + rt[:, 2:4].astype(jnp.int32).reshape(-1)

    dest_p = dest_of(rt_p)
    dest_s = dest_of(rt_s)

    xs = jnp.zeros((n_blk * EXPERT_ROWS * TOKEN_TILE_ROWS, LANES), F32)
    xs = _scatter_rows(dest_p, hn_p, xs)
    xs = _scatter_rows(dest_s, hn_s, xs)
    yb = _experts(blk_expert, n_used, xs, w_exp_gate[0], w_exp_up[0], w_exp_down[0])

    bpg = row(b_ple_gate[0])
    npl = row(norm_ple[0])
    y_p = _post2(dest_p, h_p, rt_p, p_prompt[0].reshape(np_rows, -1), yb, wple_b, wpg_b, bpg, npl, tp1)
    y_s = _post2(dest_s, h_s, rt_s, p_sample[0].reshape(ns_rows, -1), yb, wple_b, wpg_b, bpg, npl, ts1)

    kv5 = lambda a, b, s: a.reshape(1, b, s, N_KV_HEADS, HEAD_DIM)
    wp = min(WINDOW, t)
    win_p = lambda a: kv5(a, bp, t)[:, :, t - wp:]
    win_s = lambda c, new: jnp.concatenate([c[0], new.reshape(bs, ts, N_KV_HEADS, HEAD_DIM)], axis=1)[None, :, ts:]
    conv_p = cs_p[:, 8 - (CONV_K - 1):][None]
    conv_s = u_s.reshape(bs, ts, CONV_DIM)[:, ts - (CONV_K - 1):][None]
    from_t = lambda a: jnp.transpose(a.reshape(bp, N_KV_HEADS, HEAD_DIM, t), (0, 3, 1, 2))[None]
    return (y_p.reshape(bp, t, d), y_s.reshape(bs, ts, d),
            from_t(kc_t), from_t(vc_t), from_t(ks_t), from_t(vs_t), win_p(kw_p), win_p(vw_p), conv_p,
            kv5(kc_s, bs, ts), kv5(vc_s, bs, ts), kv5(ks_s, bs, ts), kv5(vs_s, bs, ts),
            win_s(cache_k_win, kw_s), win_s(cache_v_win, vw_s), conv_s)
```

```python
import functools
import math

import numpy as np
import jax
import jax.numpy as jnp
from jax import lax
from jax.experimental import pallas as pl
from jax.experimental.pallas import tpu as pltpu

F32 = jnp.float32
BF16 = jnp.bfloat16
NEG_INF = float("-inf")
MASKED = -1e30

HEAD_DIM = 64
N_HEADS = 8
N_KV_HEADS = 2
GROUP = N_HEADS // N_KV_HEADS
CONV_DIM = 512
ATTN_DIM = 512
KV_DIM = N_KV_HEADS * HEAD_DIM
N_BRANCH = 3
CONV_K = 3
CMP_BLOCK = 32
CMP_STRIDE = 16
CMP_HIDDEN = 256
SLC_BLOCK = 64
N_SEL = 16
WINDOW = 512
Q_BLOCK = 128
FORCE_SCORE = 1e4
NUM_BUCKETS = 32
MAX_DISTANCE = 128
N_GROUPS = 4
EXPERTS_PER_GROUP = 8
N_EXPERTS = N_GROUPS * EXPERTS_PER_GROUP
D_EXPERT = 512
EPS = 1e-6

D_MODEL = 1024
LANES = 128
TOKEN_TILE_ROWS = D_MODEL // LANES
Z_COLS = 3 * CONV_DIM + ATTN_DIM + 6 * KV_DIM + LANES
BIAS_DMAX = 768
EXPERT_ROWS = 512
ROUTER_GROUP_LANE = 32
SCATTER_TOKENS = 256
COMPRESS_SEQS = 4
SAMPLE_SEQS = 2
VMEM_LIMIT = 56 * 1024 * 1024


def _cparams(*sem):
    return pltpu.CompilerParams(dimension_semantics=sem, vmem_limit_bytes=VMEM_LIMIT)


def _dot(a, b):
    return jnp.dot(a, b, preferred_element_type=F32)


def _dot_nt(a, b):
    return lax.dot_general(a, b, (((1,), (1,)), ((), ())), preferred_element_type=F32)


def _split3(x):
    hi = x.astype(BF16)
    r = x - hi.astype(F32)
    mid = r.astype(BF16)
    lo = (r - mid.astype(F32)).astype(BF16)
    return hi, mid, lo


def _rms(x, g):
    return x * lax.rsqrt(jnp.mean(x * x, axis=-1, keepdims=True) + EPS) * g


def _head_rms(x, bd, g):
    hi, mid, _ = _split3(x * x)
    ss = _dot(hi, bd) + _dot(mid, bd)
    return x * lax.rsqrt(ss * (1.0 / HEAD_DIM) + EPS) * g


def _sigmoid(x):
    return 1.0 / (1.0 + jnp.exp(-x))


def _store_token_tiles(ref, x, n):
    r = x.shape[1] // LANES
    for j in range(r):
        ref[pl.ds(j, n, stride=r), :] = x[:, j * LANES:(j + 1) * LANES]


def _load_token_tiles(ref, n, r, first=0):
    return jnp.concatenate([ref[pl.ds(first + j, n, stride=r), :] for j in range(r)], axis=-1)


def _softmax_parts(s):
    m = jnp.max(s, axis=-1, keepdims=True)
    m = jnp.where(m == NEG_INF, 0.0, m)
    p = jnp.exp(s - m)
    l = jnp.sum(p, axis=-1, keepdims=True)
    return p, l


def _proj_body(sample, tm, *refs):
    if sample:
        (x_ref, nm_ref, w_ref, qn_ref, kn1_ref, kn2_ref, cw_ref, bd_ref, s0_ref, s1_ref,
         co_ref, q_ref, kc_ref, vc_ref, ks_ref, vs_ref, kw_ref, vw_ref, gt_ref, u_ref) = refs
    else:
        (x_ref, nm_ref, w_ref, qn_ref, kn1_ref, kn2_ref, cw_ref, bd_ref,
         co_ref, q_ref, kc_ref, vc_ref, ks_ref, vs_ref, kw_ref, vw_ref, gt_ref, cs_ref,
         ksb_ref, vsb_ref, kwb_ref, vwb_ref, kct_ref, vct_ref, kst_ref, vst_ref, carry_ref) = refs

    xn = _rms(x_ref[...], nm_ref[...]).astype(BF16)

    def seg(a, b):
        return _dot(xn, w_ref[:, a:b])

    c3 = 3 * CONV_DIM
    u = seg(2 * CONV_DIM, c3) * seg(0, CONV_DIM)
    bg = seg(CONV_DIM, 2 * CONV_DIM)
    row = lax.broadcasted_iota(jnp.int32, (tm, 1), 0)
    um1 = pltpu.roll(u, 1, axis=0)
    um2 = pltpu.roll(u, 2, axis=0)
    if sample:
        r = row & 3
        s0 = s0_ref[...]
        s1 = s1_ref[...]
        prev1 = jnp.where(r == 0, s1, um1)
        prev2 = jnp.where(r == 0, s0, jnp.where(r == 1, s1, um2))
        u_ref[...] = u
    else:
        @pl.when(pl.program_id(1) == 0)
        def _():
            carry_ref[...] = jnp.zeros_like(carry_ref)
        c = carry_ref[...]
        prev1 = jnp.where(row == 0, c[7:8], um1)
        prev2 = jnp.where(row == 0, c[6:7], jnp.where(row == 1, c[7:8], um2))
        carry_ref[...] = u[tm - 8:tm]
        cs_ref[0] = u[tm - 8:tm]
    cw = cw_ref[...]
    y = cw[0:1] * prev2 + cw[1:2] * prev1 + cw[2:3] * u
    co_ref[...] = (bg * y).astype(BF16)

    bd = bd_ref[...]
    q = _head_rms(seg(c3, c3 + ATTN_DIM), bd, qn_ref[...]) * (HEAD_DIM ** -0.5)
    lane = lax.broadcasted_iota(jnp.int32, (tm, LANES), 1)
    low = lane < HEAD_DIM

    def head_planes(x, fill):
        return [jnp.where(low, x if h == 0 else pltpu.roll(x, HEAD_DIM, axis=1), fill) for h in range(N_KV_HEADS)]

    if sample:
        q_ref[...] = q
    else:
        for hd in range(N_HEADS):
            pair = q[:, (hd // 2) * LANES:(hd // 2 + 1) * LANES]
            if hd % 2:
                pair = pltpu.roll(pair, HEAD_DIM, axis=1)
            q_ref[0, hd] = jnp.where(low, pair, 0.0).astype(BF16)

    k0 = c3 + ATTN_DIM
    bdk = bd[:KV_DIM, :KV_DIM]
    kc = seg(k0, k0 + KV_DIM)
    vc = seg(k0 + KV_DIM, k0 + 2 * KV_DIM)
    ks = _head_rms(seg(k0 + 2 * KV_DIM, k0 + 3 * KV_DIM), bdk, kn1_ref[...])
    vs = seg(k0 + 3 * KV_DIM, k0 + 4 * KV_DIM)
    kw = _head_rms(seg(k0 + 4 * KV_DIM, k0 + 5 * KV_DIM), bdk, kn2_ref[...])
    vw = seg(k0 + 5 * KV_DIM, k0 + 6 * KV_DIM)
    kc_ref[...] = kc
    vc_ref[...] = vc
    ks_ref[...] = ks
    vs_ref[...] = vs
    kw_ref[...] = kw
    vw_ref[...] = vw
    if not sample:
        kct_ref[0] = kc.T
        vct_ref[0] = vc.T
        kst_ref[0] = ks.T
        vst_ref[0] = vs.T
    gates = _sigmoid(seg(k0 + 6 * KV_DIM, k0 + 6 * KV_DIM + LANES))
    if sample:
        gt_ref[...] = gates
    else:
        pos = pl.program_id(1) * tm + lax.broadcasted_iota(jnp.int32, (tm, LANES), 0)
        block_onehot = jnp.where(lane - HEAD_DIM == (pos >> 6), 1.0, 0.0)
        for h, (a, b, c, e) in enumerate(zip(head_planes(ks, block_onehot), head_planes(vs, 1.0),
                                             head_planes(kw, 0.0), head_planes(vw, 1.0))):
            ksb_ref[h] = a.astype(BF16)
            vsb_ref[h] = b.astype(BF16)
            kwb_ref[h] = c.astype(BF16)
            vwb_ref[h] = e.astype(BF16)
        gt_ref[0] = gates
        gt_ref[1] = pltpu.roll(gates, LANES - GROUP * N_BRANCH, axis=1)


def _project(x2d, batch, seq, tm, weights, state=None):
    n, d = x2d.shape
    sample = state is not None
    nt = seq // tm if not sample else 1
    const = lambda shape: pl.BlockSpec(shape, lambda b, t: (0,) * len(shape))
    rows = lambda w: pl.BlockSpec((tm, w), lambda b, t: (b * nt + t, 0))
    nm, w_in, qn, kn1, kn2, cw, bd = weights
    in_specs = [rows(d), const(nm.shape), const(w_in.shape), const(qn.shape), const(kn1.shape),
                const(kn2.shape), const(cw.shape), const(bd.shape)]
    args = [x2d, nm, w_in, qn, kn1, kn2, cw, bd]
    kv_f32 = [jax.ShapeDtypeStruct((n, KV_DIM), F32)] * 6
    if sample:
        in_specs += [rows(CONV_DIM), rows(CONV_DIM)]
        args += list(state)
        out_shape = ([jax.ShapeDtypeStruct((n, CONV_DIM), BF16), jax.ShapeDtypeStruct((n, ATTN_DIM), F32)]
                     + kv_f32 + [jax.ShapeDtypeStruct((n, LANES), F32), jax.ShapeDtypeStruct((n, CONV_DIM), F32)])
        out_specs = [rows(CONV_DIM), rows(ATTN_DIM)] + [rows(KV_DIM)] * 6 + [rows(LANES), rows(CONV_DIM)]
        scratch = []
        grid = (1, 1)
    else:
        planes = pl.BlockSpec((N_KV_HEADS, tm, LANES), lambda b, t: (0, b * nt + t, 0))
        out_shape = ([jax.ShapeDtypeStruct((n, CONV_DIM), BF16),
                      jax.ShapeDtypeStruct((batch, N_HEADS, seq, LANES), BF16)]
                     + kv_f32 + [jax.ShapeDtypeStruct((N_KV_HEADS, n, LANES), F32),
                                 jax.ShapeDtypeStruct((batch, 8, CONV_DIM), F32)]
                     + [jax.ShapeDtypeStruct((N_KV_HEADS, n, LANES), BF16)] * 4
                     + [jax.ShapeDtypeStruct((batch, KV_DIM, seq), F32)] * 4)
        out_specs = ([rows(CONV_DIM), pl.BlockSpec((1, N_HEADS, tm, LANES), lambda b, t: (b, 0, t, 0))]
                     + [rows(KV_DIM)] * 6 + [planes, pl.BlockSpec((1, 8, CONV_DIM), lambda b, t: (b, 0, 0))]
                     + [planes] * 4 + [pl.BlockSpec((1, KV_DIM, tm), lambda b, t: (b, 0, t))] * 4)
        scratch = [pltpu.VMEM((8, CONV_DIM), F32)]
        grid = (batch, nt)
    return pl.pallas_call(
        functools.partial(_proj_body, sample, tm),
        grid=grid, in_specs=in_specs, out_specs=out_specs, out_shape=out_shape, scratch_shapes=scratch,
        compiler_params=_cparams("arbitrary", "arbitrary"),
        name="proj_sample" if sample else "proj_prompt",
    )(*args)


def _gelu_tanh(x):
    cdf = 0.5 * (1.0 + jnp.tanh(math.sqrt(2.0 / math.pi) * (x + 0.044715 * (x * x * x))))
    return x * cdf


def _compress_core(norm, x, pe_ref, we_ref, w2_ref, g_ref):
    n = x.shape[0]
    a0 = _dot((x + pe_ref[0:1]).astype(BF16), we_ref[0])
    a1 = _dot((x + pe_ref[1:2]).astype(BF16), we_ref[1])
    hid = a0 + pltpu.roll(a1, n - 1, axis=0)
    w2 = w2_ref[...]
    outs = []
    for h in range(N_KV_HEADS):
        act = _gelu_tanh(hid[:, h * CMP_HIDDEN:(h + 1) * CMP_HIDDEN])
        o = _dot(act.astype(BF16), w2)
        if norm:
            o = _rms(o, g_ref[...])
        outs.append(o)
    return outs


def _compress_rows_body(norm, x_ref, pe_ref, we_ref, w2_ref, g_ref, o_ref):
    outs = _compress_core(norm, x_ref[0], pe_ref, we_ref, w2_ref, g_ref)
    for h in range(N_KV_HEADS):
        o_ref[0, h] = jnp.concatenate([outs[h], jnp.zeros_like(outs[h])], axis=-1).astype(BF16)


def _compress_pages_body(norm, nch, n_seq, n_pages, pt_ref, *refs):
    pages = refs[:n_seq * n_pages]
    pe_ref, we_ref, w2_ref, g_ref, o_ref, rows_ref = refs[n_seq * n_pages:]
    page = pages[0].shape[2]
    for j, p in enumerate(pages):
        rows_ref[j * page:(j + 1) * page, :] = p[0].T
    x = jnp.concatenate([rows_ref[pl.ds(r, n_seq * nch, stride=CMP_STRIDE), :] for r in range(CMP_STRIDE)], axis=-1)
    out = jnp.concatenate(_compress_core(norm, x, pe_ref, we_ref, w2_ref, g_ref), axis=-1)
    for b in range(n_seq):
        o_ref[b] = out[b * nch:(b + 1) * nch]


def _compress_weights(pe, w1, w2):
    w1r = w1.reshape(2, CMP_STRIDE, HEAD_DIM, CMP_HIDDEN)
    eye = jnp.eye(N_KV_HEADS, dtype=w1.dtype)
    we = jnp.einsum("jrdc,hk->jrhdkc", w1r, eye).reshape(2, CMP_STRIDE * KV_DIM, N_KV_HEADS * CMP_HIDDEN)
    per = pe.reshape(2, CMP_STRIDE, 1, HEAD_DIM)
    pex = jnp.broadcast_to(per, (2, CMP_STRIDE, N_KV_HEADS, HEAD_DIM)).reshape(2, CMP_STRIDE * KV_DIM)
    return pex.astype(F32), we.astype(BF16), w2.astype(BF16)


def _compress_rows(rows3, cw, gain, norm):
    b, nch, width = rows3.shape
    pex, we, w2 = cw
    const = lambda a: pl.BlockSpec(a.shape, lambda i: (0,) * a.ndim)
    return pl.pallas_call(
        functools.partial(_compress_rows_body, norm),
        grid=(b,),
        in_specs=[pl.BlockSpec((1, nch, width), lambda i: (i, 0, 0)), const(pex), const(we), const(w2), const(gain)],
        out_specs=pl.BlockSpec((1, N_KV_HEADS, nch, LANES), lambda i: (i, 0, 0, 0)),
        out_shape=jax.ShapeDtypeStruct((b, N_KV_HEADS, nch, LANES), BF16),
        compiler_params=_cparams("arbitrary"),
        name="compress_rows",
    )(rows3, pex, we, w2, gain)


def _compress_pages(cache_t, pt_flat, n_batch, n_pages, cw, gain, norm):
    _, _, page = cache_t.shape
    nch = n_pages * page // CMP_STRIDE
    pex, we, w2 = cw
    n_seq = COMPRESS_SEQS if n_batch % COMPRESS_SEQS == 0 else 1
    const = lambda a: pl.BlockSpec(a.shape, lambda i, pt: (0,) * a.ndim)
    page_spec = lambda j: pl.BlockSpec((1, KV_DIM, page), lambda i, pt: (pt[i * (n_seq * n_pages) + j], 0, 0))
    n_ops = n_seq * n_pages
    grid_spec = pltpu.PrefetchScalarGridSpec(
        num_scalar_prefetch=1, grid=(n_batch // n_seq,),
        in_specs=[page_spec(j) for j in range(n_ops)] + [const(pex), const(we), const(w2), const(gain)],
        out_specs=pl.BlockSpec((n_seq, nch, KV_DIM), lambda i, pt: (i, 0, 0)),
        scratch_shapes=[pltpu.VMEM((n_ops * page, KV_DIM), F32)])
    return pl.pallas_call(
        functools.partial(_compress_pages_body, norm, nch, n_seq, n_pages),
        grid_spec=grid_spec,
        out_shape=jax.ShapeDtypeStruct((n_batch, nch, KV_DIM), F32),
        compiler_params=_cparams("arbitrary"),
        name="compress_pages",
    )(pt_flat, *([cache_t] * n_ops), pex, we, w2, gain)


def _rel_bucket(dist):
    n = jnp.maximum(dist, 0)
    max_exact = NUM_BUCKETS // 2
    nf = jnp.maximum(n, 1).astype(F32)
    large = max_exact + (jnp.log(nf / max_exact) / math.log(MAX_DISTANCE / max_exact)
                         * (NUM_BUCKETS - max_exact)).astype(jnp.int32)
    large = jnp.minimum(large, NUM_BUCKETS - 1)
    return jnp.where(n < max_exact, n, large)


def _bias_by_distance(rel_bias):
    d = jnp.arange(BIAS_DMAX, dtype=jnp.int32)
    f = rel_bias.astype(F32)[_rel_bucket(d)]
    f = (f - f[BIAS_DMAX - 1:BIAS_DMAX]).T
    return jnp.concatenate([f, jnp.full((N_HEADS, 1), NEG_INF, F32)], axis=1)


def _bias_index(d, valid):
    return np.where(valid, np.clip(d, 0, BIAS_DMAX - 1), BIAS_DMAX).astype(np.int32)


def _bias_table(fext, d, valid):
    return jnp.take(fext, jnp.asarray(_bias_index(d, valid)), axis=1)


def _toeplitz_body(n_rows, v_ref, o_ref):
    x = jnp.broadcast_to(v_ref[0], (n_rows, v_ref.shape[2]))
    o_ref[0] = pltpu.roll(x, 0, axis=1, stride=1, stride_axis=0)


def _toeplitz_rows(v, n_rows):
    h, w = v.shape
    return pl.pallas_call(
        functools.partial(_toeplitz_body, n_rows),
        grid=(h,),
        in_specs=[pl.BlockSpec((1, 1, w), lambda i: (i, 0, 0))],
        out_specs=pl.BlockSpec((1, n_rows, w), lambda i: (i, 0, 0)),
        out_shape=jax.ShapeDtypeStruct((h, n_rows, w), F32),
        compiler_params=_cparams("arbitrary"),
        name="toeplitz_rows",
    )(v.reshape(h, 1, w))


def _select_blocks(imp_t, srow, qpos, n_rank):
    qblk = qpos >> 6
    forced = (srow == 0) | (srow == qblk) | (srow == qblk - 1)
    valid = (srow << 6) <= qpos
    imp_t = jnp.where(valid, imp_t + jnp.where(forced, FORCE_SCORE, 0.0), NEG_INF)
    n_rows = imp_t.shape[0]
    assert n_rows % 8 == 0
    slabs = [imp_t[a:a + 8] for a in range(0, n_rows, 8)]
    rows8 = [srow[a:a + 8] for a in range(0, n_rows, 8)]
    cnts = [jnp.zeros(x.shape, jnp.int32) for x in slabs]
    for s in range(n_rank):
        r = imp_t[s:s + 1, :]
        for j, x in enumerate(slabs):
            if 8 * j > s:
                beats = r >= x
            elif 8 * j + 7 <= s:
                beats = r > x
            else:
                beats = (r > x) | ((r == x) & (rows8[j] > s))
            cnts[j] = cnts[j] + jnp.where(beats, 1, 0)
    cnt = jnp.concatenate(cnts, axis=0)
    return jnp.where((cnt < N_SEL) & valid, 1.0, 0.0)


def _exp_pv(s, m, v):
    return _dot(jnp.exp((s - m).astype(BF16)), v)


def _normalize_pv(pv):
    return pv / jnp.maximum(pv[:, HEAD_DIM:HEAD_DIM + 1], 1e-30)


def _attn_prompt_body(kt, n_slc, q_ref, gt_ref, kcmp_ref, vcmp_ref, ks_ref, vs_ref, kw_ref, vw_ref,
                      ctab_ref, ntab_ref, wtab_ref, ovl_ref, o_ref, sa_ref, sb_ref):
    i = pl.program_id(2)
    qb = Q_BLOCK
    rows = GROUP * qb
    n_cmp_pad = kcmp_ref.shape[2]
    first_near_block = 2 * jnp.maximum(i - 1, 0)
    near_start = pl.multiple_of(jnp.maximum(i - 1, 0) * qb, qb)
    win_start = pl.multiple_of(jnp.maximum(i * qb - WINDOW, 0), qb)
    band = WINDOW + qb
    gates = gt_ref[0]
    lane = lax.broadcasted_iota(jnp.int32, (qb, LANES), 1)
    q0 = q_ref[0].reshape(rows, LANES)

    s = _dot_nt(q0, kw_ref[0, pl.ds(win_start, band), :]) + wtab_ref[0].reshape(rows, band)
    m = jnp.max(s, axis=-1, keepdims=True)
    m = jnp.where(m == NEG_INF, 0.0, m)
    o_w = _normalize_pv(_exp_pv(s, m, vw_ref[0, pl.ds(win_start, band), :]))

    per_qb = qb // CMP_STRIDE
    ctab = pltpu.roll(ctab_ref[...].reshape(rows, 2 * n_cmp_pad), i * per_qb, axis=1)[:, n_cmp_pad:]
    s = _dot_nt(q0, kcmp_ref[0, 0]) + ctab
    p, l = _softmax_parts(s)
    pn = p / jnp.maximum(l, 1e-30)
    o_c = _dot(pn.astype(BF16), vcmp_ref[0, 0])

    psum = pn[0:qb] + pn[qb:2 * qb] + pn[2 * qb:3 * qb] + pn[3 * qb:4 * qb]
    hi, mid, lo = _split3(psum)
    ovl = ovl_ref[...]
    imp_t = _dot_nt(ovl, hi) + _dot_nt(ovl, mid) + _dot_nt(ovl, lo)
    srow = lax.broadcasted_iota(jnp.int32, (n_slc, qb), 0)
    qpos_t = i * qb + lax.broadcasted_iota(jnp.int32, (n_slc, qb), 1)
    sel_t = _select_blocks(imp_t, srow, qpos_t, n_slc)
    sel_t = jnp.concatenate([sel_t, jnp.zeros((LANES - n_slc, qb), F32)], axis=0)
    sel = sel_t.T

    def query_with_mask(keep):
        m = pltpu.roll(jnp.where(keep, 0.0, MASKED), HEAD_DIM, axis=1).astype(BF16)
        m = jnp.concatenate([jnp.where(lane < HEAD_DIM, q0[g * qb:(g + 1) * qb], m) for g in range(GROUP)], axis=0)
        return m

    q_near = query_with_mask(sel > 0.5)
    q_far = query_with_mask((sel > 0.5) & (lane < first_near_block))

    s = _dot_nt(q_near, ks_ref[0, pl.ds(near_start, 2 * qb), :]) + ntab_ref[0].reshape(rows, 2 * qb)
    m0 = jnp.max(s, axis=-1, keepdims=True)
    m0 = jnp.where(m0 == NEG_INF, 0.0, m0)
    a0 = _exp_pv(s, m0, vs_ref[0, pl.ds(near_start, 2 * qb), :])

    n_kt = ks_ref.shape[1] // kt

    def scores(t):
        k0 = pl.multiple_of(jnp.minimum(t, n_kt - 1) * kt, kt)
        return _dot_nt(q_far, ks_ref[0, pl.ds(k0, kt), :])

    def consume(t, s, m_old, acc):
        k0 = pl.multiple_of(jnp.minimum(t, n_kt - 1) * kt, kt)
        m_new = jnp.maximum(m_old, jnp.max(s, axis=-1, keepdims=True))
        return m_new, jnp.exp(m_old - m_new) * acc + _exp_pv(s, m_new, vs_ref[0, pl.ds(k0, kt), :])

    def far_pair(u, carry):
        m, acc = carry
        sb_ref[...] = scores(2 * u + 1)
        m, acc = consume(2 * u, sa_ref[...], m, acc)
        sa_ref[...] = scores(2 * u + 2)
        return consume(2 * u + 1, sb_ref[...], m, acc)

    n_far = (near_start + kt - 1) // kt
    sa_ref[...] = scores(0)
    _, acc_s = lax.fori_loop(0, (n_far + 1) // 2, far_pair, (m0, a0))
    o_s = _normalize_pv(acc_s)

    heads_out = []
    for g in range(GROUP):
        c = g * N_BRANCH
        sl = slice(g * qb, (g + 1) * qb)
        heads_out.append(gates[:, c:c + 1] * o_c[sl] + gates[:, c + 1:c + 2] * o_s[sl] + gates[:, c + 2:c + 3] * o_w[sl])
    tiles = [jnp.where(lane < HEAD_DIM, heads_out[2 * j], pltpu.roll(heads_out[2 * j + 1], HEAD_DIM, axis=1))
             for j in range(GROUP // 2)]
    o_ref[...] = jnp.concatenate(tiles, axis=-1).astype(BF16)


def _attn_prompt(qp, gates, kcmp, vcmp, ksb, vsb, kwb, vwb, fext, batch, seq):
    qb = Q_BLOCK
    nqb = seq // qb
    n_slc = seq // SLC_BLOCK
    assert n_slc <= LANES - HEAD_DIM
    n_cmp_pad = kcmp.shape[2]
    n_cmp = n_cmp_pad - 1
    kt = min(512, seq)
    assert (seq // kt) % 2 == 0
    band = WINDOW + qb
    iq = np.arange(qb)

    per_qb = qb // CMP_STRIDE
    half = 2 * per_qb
    m = np.arange(-half, half)
    d = iq[:, None] - (m[None, :] * CMP_STRIDE + CMP_BLOCK - 1)
    assert d[:, 0].min() >= MAX_DISTANCE and d[:, -1].max() < 0
    ctab = jnp.concatenate([jnp.zeros((N_HEADS, qb, n_cmp_pad - half), F32), _bias_table(fext, d, d >= 0),
                            jnp.full((N_HEADS, qb, n_cmp_pad - half), NEG_INF, F32)], axis=2)
    assert per_qb * (nqb - 1) < n_cmp_pad
    assert (n_cmp_pad - 1) * CMP_STRIDE + CMP_BLOCK - 1 >= seq and n_cmp == n_cmp_pad - 1

    nv = WINDOW // qb + 1
    kw_ = WINDOW + band
    w = qb + kw_
    assert w % LANES == 0
    m = np.arange(w)
    m = np.where(m < kw_, m, m - w)
    dj = WINDOW - m
    wide = _toeplitz_rows(_bias_table(fext, dj, (dj >= 0) & (dj < WINDOW)), qb)
    wtab = jnp.stack([wide[:, :, WINDOW - qb * v:WINDOW - qb * v + band] for v in range(nv)], axis=0)
    ntab = jnp.stack([wide[:, :, WINDOW - qb * v:WINDOW - qb * v + 2 * qb] for v in range(2)], axis=0)
    assert 3 * qb <= WINDOW

    c_start = np.arange(n_cmp_pad) * CMP_STRIDE
    s_start = np.arange(n_slc) * SLC_BLOCK
    ovl = ((c_start[None, :] < s_start[:, None] + SLC_BLOCK) & (c_start[None, :] + CMP_BLOCK > s_start[:, None])
           & (np.arange(n_cmp_pad) < n_cmp)[None, :])
    ovl = jnp.asarray(ovl, BF16)

    per_head = lambda: pl.BlockSpec((1, seq, LANES), lambda b, h, i: (h, b, 0))
    return pl.pallas_call(
        functools.partial(_attn_prompt_body, kt, n_slc),
        grid=(batch, N_KV_HEADS, nqb),
        in_specs=[pl.BlockSpec((1, GROUP, qb, LANES), lambda b, h, i: (b, h, i, 0)),
                  pl.BlockSpec((1, qb, LANES), lambda b, h, i: (h, b * nqb + i, 0)),
                  pl.BlockSpec((1, 1, n_cmp_pad, LANES), lambda b, h, i: (b, h, 0, 0)),
                  pl.BlockSpec((1, 1, n_cmp_pad, LANES), lambda b, h, i: (b, h, 0, 0)),
                  per_head(), per_head(), per_head(), per_head(),
                  pl.BlockSpec((GROUP, qb, 2 * n_cmp_pad), lambda b, h, i: (h, 0, 0)),
                  pl.BlockSpec((1, GROUP, qb, 2 * qb), lambda b, h, i: (jnp.minimum(i, 1), h, 0, 0)),
                  pl.BlockSpec((1, GROUP, qb, band), lambda b, h, i: (jnp.minimum(i, nv - 1), h, 0, 0)),
                  pl.BlockSpec(ovl.shape, lambda b, h, i: (0, 0))],
        out_specs=pl.BlockSpec((qb, GROUP * HEAD_DIM), lambda b, h, i: (b * nqb + i, h)),
        out_shape=jax.ShapeDtypeStruct((batch * seq, ATTN_DIM), BF16),
        scratch_shapes=[pltpu.VMEM((GROUP * qb, kt), F32)] * 2,
        compiler_params=_cparams("arbitrary", "arbitrary", "arbitrary"),
        name="attn_prompt",
    )(qp, gates, kcmp, vcmp, ksb, vsb, kwb, vwb, ctab, ntab, wtab, ovl)


def _attn_sample_body(n_seq, n_pages, ts, past_len, pt_ref, *refs):
    n_ops = n_seq * n_pages
    for b in range(n_seq):
        _attn_sample_one(b, refs[b * n_pages:(b + 1) * n_pages], refs[n_ops + b * n_pages:n_ops + (b + 1) * n_pages],
                         refs[2 * n_ops:], ts, past_len)


def _attn_sample_one(b, kpages, vpages, refs, ts, past_len):
    (q_ref, gt_ref, kcmp_ref, vcmp_ref, ksn_ref, vsn_ref, kwc_ref, vwc_ref, kwn_ref, vwn_ref,
     ctab_ref, stab_ref, sntab_ref, wtab_ref, wntab_ref, eexp_ref, ovl_ref, o_ref) = refs
    rows = GROUP * N_KV_HEADS * ts
    rq = N_KV_HEADS * ts
    q = q_ref[b]
    gates = gt_ref[b]

    s = _dot_nt(q, kcmp_ref[b].astype(BF16)) + ctab_ref[...]
    p, l = _softmax_parts(s)
    pn = p / jnp.maximum(l, 1e-30)
    o_c = _dot(pn.astype(BF16), vcmp_ref[b].astype(BF16))

    psum = pn[0:rq]
    for g in range(1, GROUP):
        psum = psum + pn[g * rq:(g + 1) * rq]
    hi, mid, lo = _split3(psum)
    ovl = ovl_ref[...]
    imp = _dot(hi, ovl) + _dot(mid, ovl) + _dot(lo, ovl)
    n_slc = -(-(past_len + ts) // SLC_BLOCK)
    blk = lax.broadcasted_iota(jnp.int32, (rq, LANES), 1)
    qpos = past_len + (lax.broadcasted_iota(jnp.int32, (rq, LANES), 0) & (ts - 1))
    qblk = qpos >> 6
    forced = (blk == 0) | (blk == qblk) | (blk == qblk - 1)
    valid = ((blk << 6) <= qpos) & (blk < n_slc)
    imp = jnp.where(valid, imp + jnp.where(forced, FORCE_SCORE, 0.0), NEG_INF)
    cnt = jnp.zeros((rq, LANES), jnp.int32)
    for sidx in range(n_slc):
        r = imp[:, sidx:sidx + 1]
        beats = (r > imp) | ((r == imp) & (blk > sidx))
        cnt = cnt + jnp.where(beats, 1, 0)
    sel = jnp.where((cnt < N_SEL) & valid, 1.0, 0.0)
    sel = jnp.concatenate([sel] * GROUP, axis=0).astype(BF16)

    kc_t = jnp.concatenate([p_[0] for p_ in kpages], axis=1).astype(BF16)
    vc_t = jnp.concatenate([p_[0] for p_ in vpages], axis=1).astype(BF16)
    mexp = _dot(sel, eexp_ref[...])
    s1 = jnp.where(mexp > 0.5, _dot(q, kc_t) + stab_ref[...], NEG_INF)
    last = sel[:, n_slc - 1:n_slc].astype(F32)
    s2 = jnp.where(last > 0.5, _dot_nt(q, ksn_ref[b].astype(BF16)) + sntab_ref[...], NEG_INF)
    m = jnp.maximum(jnp.max(s1, axis=-1, keepdims=True), jnp.max(s2, axis=-1, keepdims=True))
    m = jnp.where(m == NEG_INF, 0.0, m)
    p1 = jnp.exp(s1 - m)
    p2 = jnp.exp(s2 - m)
    l = jnp.sum(p1, axis=-1, keepdims=True) + jnp.sum(p2, axis=-1, keepdims=True)
    o_s = (_dot_nt(p1.astype(BF16), vc_t) + _dot(p2.astype(BF16), vsn_ref[b].astype(BF16))) / jnp.maximum(l, 1e-30)

    s1 = _dot(q, kwc_ref[b].astype(BF16)) + wtab_ref[...]
    s2 = _dot_nt(q, kwn_ref[b].astype(BF16)) + wntab_ref[...]
    m = jnp.maximum(jnp.max(s1, axis=-1, keepdims=True), jnp.max(s2, axis=-1, keepdims=True))
    m = jnp.where(m == NEG_INF, 0.0, m)
    p1 = jnp.exp(s1 - m)
    p2 = jnp.exp(s2 - m)
    l = jnp.sum(p1, axis=-1, keepdims=True) + jnp.sum(p2, axis=-1, keepdims=True)
    o_w = (_dot_nt(p1.astype(BF16), vwc_ref[b].astype(BF16))
           + _dot(p2.astype(BF16), vwn_ref[b].astype(BF16))) / jnp.maximum(l, 1e-30)

    o_ref[b] = gates[:, 0:1] * o_c + gates[:, 1:2] * o_s + gates[:, 2:3] * o_w


def _attn_sample(q_s, gates_s, kcmp, vcmp, ks_new, vs_new, kw_new, vw_new, cache_ks, cache_vs,
                 cache_kw, cache_vw, pt_flat, fext, n_batch, ts, n_pages, page):
    past_len = n_pages * page
    w_buf = cache_kw.shape[2]
    rows = GROUP * N_KV_HEADS * ts
    n_new = 8
    n_cmp_pad = kcmp.shape[1]
    n_cmp = n_cmp_pad - 1
    n_slc = -(-(past_len + ts) // SLC_BLOCK)

    q5 = q_s.reshape(n_batch, ts, N_KV_HEADS, GROUP, HEAD_DIM).transpose(0, 3, 2, 1, 4)
    eye = jnp.eye(N_KV_HEADS, dtype=q_s.dtype)
    qr = jnp.einsum("bghtd,hk->bghtkd", q5, eye).reshape(n_batch, rows, LANES).astype(BF16)
    g5 = gates_s[:, :N_HEADS * N_BRANCH].reshape(n_batch, ts, N_KV_HEADS, GROUP, N_BRANCH).transpose(0, 3, 2, 1, 4)
    gr = jnp.pad(g5.reshape(n_batch, rows, N_BRANCH), ((0, 0), (0, 0), (0, LANES - N_BRANCH)))
    pad_new = lambda a: jnp.pad(a.reshape(n_batch, ts, KV_DIM), ((0, 0), (0, n_new - ts), (0, 0)))
    ks_new, vs_new, kw_new, vw_new = map(pad_new, (ks_new, vs_new, kw_new, vw_new))

    g_i, h_i, t_i = np.meshgrid(np.arange(GROUP), np.arange(N_KV_HEADS), np.arange(ts), indexing="ij")
    head = (h_i * GROUP + g_i).reshape(rows)
    tq = t_i.reshape(rows)
    pos_q = past_len + tq

    f_rows = fext[jnp.asarray(head)]

    def table(d, valid):
        return jnp.take_along_axis(f_rows, jnp.asarray(_bias_index(d, valid)), axis=1)

    nn = np.arange(n_cmp_pad)
    d = pos_q[:, None] - (nn[None, :] * CMP_STRIDE + CMP_BLOCK - 1)
    ctab = table(d, (d >= 0) & (nn < n_cmp)[None, :])
    near = np.arange(past_len - MAX_DISTANCE, past_len)
    d = pos_q[:, None] - near[None, :]
    assert past_len >= MAX_DISTANCE and d.min() >= 0
    stab = jnp.concatenate([jnp.zeros((rows, past_len - MAX_DISTANCE), F32), table(d, d >= 0)], axis=1)
    jn = np.arange(n_new)
    d = tq[:, None] - jn[None, :]
    sntab = table(d, (d >= 0) & (jn < ts)[None, :])
    pos_w = past_len - w_buf + np.arange(w_buf)
    d = pos_q[:, None] - pos_w[None, :]
    wtab = table(d, (d >= 0) & (d < WINDOW) & (pos_w >= 0)[None, :])
    wntab = table(tq[:, None] - jn[None, :], (tq[:, None] >= jn[None, :]) & (jn < ts)[None, :])

    eexp = jnp.asarray(np.arange(past_len)[None, :] // SLC_BLOCK == np.arange(LANES)[:, None], BF16)
    c_start = nn * CMP_STRIDE
    s_start = np.arange(LANES) * SLC_BLOCK
    ovl = jnp.asarray((c_start[:, None] < s_start[None, :] + SLC_BLOCK) & (c_start[:, None] + CMP_BLOCK > s_start[None, :])
                      & (nn < n_cmp)[:, None] & (np.arange(LANES) < n_slc)[None, :], BF16)

    n_seq = SAMPLE_SEQS if n_batch % SAMPLE_SEQS == 0 else 1
    n_ops = n_seq * n_pages
    const = lambda a: pl.BlockSpec(a.shape, lambda b, pt: (0,) * a.ndim)
    per_b = lambda a: pl.BlockSpec((n_seq,) + a.shape[1:], lambda b, pt: (b,) + (0,) * (a.ndim - 1))
    page_spec = lambda j: pl.BlockSpec((1, KV_DIM, page), lambda b, pt: (pt[b * n_ops + j], 0, 0))
    small = [qr, gr, kcmp, vcmp, ks_new, vs_new, cache_kw, cache_vw, kw_new, vw_new]
    consts = [ctab, stab, sntab, wtab, wntab, eexp, ovl]
    grid_spec = pltpu.PrefetchScalarGridSpec(
        num_scalar_prefetch=1, grid=(n_batch // n_seq,),
        in_specs=[page_spec(j) for j in range(n_ops)] * 2 + [per_b(a) for a in small] + [const(a) for a in consts],
        out_specs=pl.BlockSpec((n_seq, rows, LANES), lambda b, pt: (b, 0, 0)))
    o = pl.pallas_call(
        functools.partial(_attn_sample_body, n_seq, n_pages, ts, past_len),
        grid_spec=grid_spec,
        out_shape=jax.ShapeDtypeStruct((n_batch, rows, LANES), F32),
        compiler_params=_cparams("arbitrary"),
        name="attn_sample",
    )(pt_flat, *([cache_ks] * n_ops), *([cache_vs] * n_ops), *small, *consts)
    o6 = o.reshape(n_batch, GROUP, N_KV_HEADS, ts, N_KV_HEADS, HEAD_DIM)
    o5 = jnp.stack([o6[:, :, h, :, h] for h in range(N_KV_HEADS)], axis=2)
    return o5.transpose(0, 3, 2, 1, 4).reshape(n_batch * ts, ATTN_DIM).astype(BF16)


def _post1_body(tm, x_ref, co_ref, at_ref, wo_ref, nf_ref, wr_ref, br_ref, tri_ref, run0_ref,
                h_ref, hn_ref, rt_ref, cnt_ref, run_ref):
    @pl.when(pl.program_id(0) == 0)
    def _():
        run_ref[...] = run0_ref[...]

    h = x_ref[...] + _dot(co_ref[...], wo_ref[0:CONV_DIM]) + _dot(at_ref[...], wo_ref[CONV_DIM:CONV_DIM + ATTN_DIM])
    hn = _rms(h, nf_ref[...])
    h_ref[...] = h
    _store_token_tiles(hn_ref, hn, tm)

    hi = hn.astype(BF16)
    lo = (hn - hi.astype(F32)).astype(BF16)
    wr = wr_ref[...]
    whi = wr.astype(BF16)
    wlo = (wr - whi.astype(F32)).astype(BF16)
    logits = _dot(hi, whi) + _dot(lo, whi) + _dot(hi, wlo) + br_ref[...]

    lane_i = lax.broadcasted_iota(jnp.int32, (tm, LANES), 1)
    lane = lane_i.astype(F32)
    big = float(LANES)
    gmask = (lane_i >= ROUTER_GROUP_LANE) & (lane_i < ROUTER_GROUP_LANE + N_GROUPS)
    lg = jnp.where(gmask, logits, NEG_INF)
    eg = jnp.exp(lg - jnp.max(lg, axis=-1, keepdims=True))
    pg = eg / jnp.sum(eg, axis=-1, keepdims=True)
    gw = jnp.max(pg, axis=-1, keepdims=True)
    grp = jnp.min(jnp.where(gmask & (pg == gw), lane, big), axis=-1, keepdims=True) - ROUTER_GROUP_LANE

    group_of_lane = (lane_i >> 3).astype(F32)
    emask = (lane_i < N_EXPERTS) & (group_of_lane == grp)
    le = jnp.where(emask, logits, NEG_INF)
    ee = jnp.exp(le - jnp.max(le, axis=-1, keepdims=True))
    pe = jnp.where(emask, ee / jnp.sum(ee, axis=-1, keepdims=True), -1.0)
    v1 = jnp.max(pe, axis=-1, keepdims=True)
    i1 = jnp.min(jnp.where(pe == v1, lane, big), axis=-1, keepdims=True)
    pe2 = jnp.where(lane == i1, -1.0, pe)
    v2 = jnp.max(pe2, axis=-1, keepdims=True)
    i2 = jnp.min(jnp.where(pe2 == v2, lane, big), axis=-1, keepdims=True)
    tot = v1 + v2
    w1 = v1 / tot * gw
    w2 = v2 / tot * gw

    oh1 = jnp.where(lane == i1, 1.0, 0.0)
    oh2 = jnp.where(lane == i2, 1.0, 0.0)
    both = oh1 + oh2
    before = _dot(tri_ref[...], both.astype(BF16)) + run_ref[0:1]
    r1 = jnp.sum(oh1 * before, axis=-1, keepdims=True)
    r2 = jnp.sum(oh2 * before, axis=-1, keepdims=True)
    run = run_ref[0:1] + jnp.sum(both, axis=0, keepdims=True)
    run_ref[...] = jnp.broadcast_to(run, run_ref.shape)
    cnt_ref[...] = jnp.broadcast_to(run, cnt_ref.shape)

    rt = jnp.where(lane_i == 0, i1, 0.0)
    rt = jnp.where(lane_i == 1, i2, rt)
    rt = jnp.where(lane_i == 2, r1, rt)
    rt = jnp.where(lane_i == 3, r2, rt)
    rt = jnp.where(lane_i == 4, w1, rt)
    rt = jnp.where(lane_i == 5, w2, rt)
    rt_ref[...] = rt


def _post1(x2d, co, at, wo, nf, wr, br, run0, tm):
    n, d = x2d.shape
    tri = jnp.asarray(np.tril(np.ones((tm, tm), np.float32), -1), BF16)
    rows = lambda w: pl.BlockSpec((tm, w), lambda i: (i, 0))
    const = lambda a: pl.BlockSpec(a.shape, lambda i: (0,) * a.ndim)
    return pl.pallas_call(
        functools.partial(_post1_body, tm),
        grid=(n // tm,),
        in_specs=[rows(d), rows(CONV_DIM), rows(ATTN_DIM), const(wo), const(nf), const(wr), const(br),
                  const(tri), const(run0)],
        out_specs=[rows(d), pl.BlockSpec((tm * TOKEN_TILE_ROWS, LANES), lambda i: (i, 0)), rows(LANES),
                   pl.BlockSpec((8, LANES), lambda i: (0, 0))],
        out_shape=[jax.ShapeDtypeStruct((n, d), F32), jax.ShapeDtypeStruct((n * TOKEN_TILE_ROWS, LANES), F32),
                   jax.ShapeDtypeStruct((n, LANES), F32), jax.ShapeDtypeStruct((8, LANES), F32)],
        scratch_shapes=[pltpu.VMEM((8, LANES), F32)],
        compiler_params=_cparams("arbitrary"),
        name="post1",
    )(x2d, co, at, wo, nf, wr, br, tri, run0)


def _token_copy(src_ref, dst_ref, s, d, sem):
    r = TOKEN_TILE_ROWS
    return pltpu.make_async_copy(src_ref.at[pl.ds(pl.multiple_of(s * r, r), r)],
                                 dst_ref.at[pl.ds(pl.multiple_of(d * r, r), r)], sem)


def _scatter_rows_body(ts, dest_ref, src_ref, init_ref, out_ref, sem):
    del init_ref
    base = pl.program_id(0) * (2 * ts)

    def issue(t, _):
        for k in range(2):
            _token_copy(src_ref, out_ref, t, dest_ref[base + 2 * t + k], sem).start(priority=k)
        return 0

    lax.fori_loop(0, ts, issue, 0)
    for _ in range(2):
        pltpu.make_async_copy(src_ref, out_ref.at[pl.ds(0, ts * TOKEN_TILE_ROWS)], sem).wait()


def _scatter_rows(dest, src, slots):
    n_tok = dest.shape[0] // 2
    ts = min(SCATTER_TOKENS, n_tok)
    assert n_tok % ts == 0
    any_spec = pl.BlockSpec(memory_space=pl.ANY)
    return pl.pallas_call(
        functools.partial(_scatter_rows_body, ts),
        grid_spec=pltpu.PrefetchScalarGridSpec(
            num_scalar_prefetch=1, grid=(n_tok // ts,),
            in_specs=[pl.BlockSpec((ts * TOKEN_TILE_ROWS, LANES), lambda i, dest: (i, 0)), any_spec],
            out_specs=any_spec, scratch_shapes=[pltpu.SemaphoreType.DMA(())]),
        out_shape=jax.ShapeDtypeStruct(slots.shape, slots.dtype),
        input_output_aliases={2: 0},
        compiler_params=pltpu.CompilerParams(dimension_semantics=("arbitrary",)),
        name="scatter_rows",
    )(dest, src, slots)


def _experts_body(be_ref, nu_ref, x_ref, wg_ref, wu_ref, wd_ref, o_ref, wg_s, wu_s, wd_s):
    i = pl.program_id(0)

    @pl.when(i < nu_ref[0])
    def _():
        prev = be_ref[jnp.maximum(i - 1, 0)]

        @pl.when((i == 0) | (be_ref[i] != prev))
        def _():
            wg_s[...] = wg_ref[0].astype(BF16)
            wu_s[...] = wu_ref[0].astype(BF16)
            wd_s[...] = wd_ref[0].astype(BF16)

        x = _load_token_tiles(x_ref, EXPERT_ROWS, TOKEN_TILE_ROWS).astype(BF16)
        g = _dot(x, wg_s[...])
        u = _dot(x, wu_s[...])
        a = g * _sigmoid(g) * u
        _store_token_tiles(o_ref, _dot(a.astype(BF16), wd_s[...]), EXPERT_ROWS)

    @pl.when(i >= nu_ref[0])
    def _():
        o_ref[...] = jnp.zeros_like(o_ref)


def _experts(blk_expert, n_used, xs, wg, wu, wd):
    blk_rows = EXPERT_ROWS * TOKEN_TILE_ROWS
    n_blk = xs.shape[0] // blk_rows
    _, d, de = wg.shape
    xmap = lambda i, be, nu: (jnp.minimum(i, jnp.maximum(nu[0] - 1, 0)), 0)
    wmap = lambda i, be, nu: (be[jnp.minimum(i, jnp.maximum(nu[0] - 1, 0))], 0, 0)
    grid_spec = pltpu.PrefetchScalarGridSpec(
        num_scalar_prefetch=2, grid=(n_blk,),
        in_specs=[pl.BlockSpec((blk_rows, LANES), xmap), pl.BlockSpec((1, d, de), wmap),
                  pl.BlockSpec((1, d, de), wmap), pl.BlockSpec((1, de, d), wmap)],
        out_specs=pl.BlockSpec((blk_rows, LANES), lambda i, be, nu: (i, 0)),
        scratch_shapes=[pltpu.VMEM((d, de), BF16), pltpu.VMEM((d, de), BF16), pltpu.VMEM((de, d), BF16)])
    return pl.pallas_call(
        _experts_body, grid_spec=grid_spec,
        out_shape=jax.ShapeDtypeStruct(xs.shape, F32),
        compiler_params=_cparams("arbitrary"),
        name="experts",
    )(blk_expert, n_used, xs, wg, wu, wd)


def _post2_body(tm, dest_ref, h_ref, rt_ref, p_ref, yb_ref, wple_ref, wpg_ref, bpg_ref, np_ref, o_ref, buf, sem):
    i = pl.program_id(0)
    n = pl.num_programs(0)

    def fetch(step, slot):
        base = step * (2 * tm)

        def issue(t, _):
            for k in range(2):
                _token_copy(yb_ref, buf.at[slot], dest_ref[base + 2 * t + k], k * tm + t, sem.at[slot]).start(priority=k)
            return 0

        lax.fori_loop(0, tm, issue, 0)

    @pl.when(i == 0)
    def _():
        fetch(0, 0)

    @pl.when(i + 1 < n)
    def _():
        fetch(i + 1, (i + 1) & 1)

    slot = i & 1

    pltpu.make_async_copy(yb_ref.at[pl.ds(0, 2 * tm * TOKEN_TILE_ROWS)], buf.at[slot], sem.at[slot]).wait()
    rt = rt_ref[...]
    y0 = _load_token_tiles(buf.at[slot], tm, TOKEN_TILE_ROWS)
    y1 = _load_token_tiles(buf.at[slot], tm, TOKEN_TILE_ROWS, first=tm * TOKEN_TILE_ROWS)
    h = h_ref[...] + (y0 * rt[:, 4:5] + y1 * rt[:, 5:6])
    gate = _sigmoid(_dot(_rms(h, np_ref[...]).astype(BF16), wpg_ref[...]) + bpg_ref[...])
    o_ref[...] = h + gate * _dot(p_ref[...].astype(BF16), wple_ref[...])


def _post2(dest, h, rt, p2d, yb, wple, wpg, bpg, npl, tm):
    n, d = h.shape
    rows = lambda w: pl.BlockSpec((tm, w), lambda i, dest: (i, 0))
    const = lambda a: pl.BlockSpec(a.shape, lambda i, dest: (0,) * a.ndim)
    grid_spec = pltpu.PrefetchScalarGridSpec(
        num_scalar_prefetch=1, grid=(n // tm,),
        in_specs=[rows(d), rows(LANES), rows(p2d.shape[1]), pl.BlockSpec(memory_space=pl.ANY), const(wple),
                  const(wpg), const(bpg), const(npl)],
        out_specs=rows(d),
        scratch_shapes=[pltpu.VMEM((2, 2 * tm * TOKEN_TILE_ROWS, LANES), F32), pltpu.SemaphoreType.DMA((2,))])
    return pl.pallas_call(
        functools.partial(_post2_body, tm),
        grid_spec=grid_spec,
        out_shape=jax.ShapeDtypeStruct((n, d), F32),
        compiler_params=_cparams("arbitrary"),
        name="post2",
    )(dest, h, rt, p2d, yb, wple, wpg, bpg, npl)


def _row_tile(n, cap=512):
    t = min(cap, n)
    assert n % t == 0 and t % 8 == 0
    return t


def kernel(x_prompt, x_sample, p_prompt, p_sample, cache_k_cmp, cache_v_cmp, cache_k_slc, cache_v_slc, cache_k_win, cache_v_win, state_conv, page_table, w_in, w_out, conv_w, norm_mix, norm_ffn, norm_ple, q_norm, k_norm, cmp_pe_k, cmp_w1_k, cmp_w2_k, cmp_pe_v, cmp_w1_v, cmp_w2_v, rel_bias, w_router_group, b_router_group, w_router_expert, b_router_expert, w_exp_gate, w_exp_up, w_exp_down, w_ple, w_ple_gate, b_ple_gate):
    assert w_in.shape[0] == 1, "single-layer step"
    bp, t, d = x_prompt.shape
    bs, ts, _ = x_sample.shape
    n_pages = page_table.shape[1]
    page = cache_k_cmp.shape[2]
    past_len = n_pages * page
    w_buf = cache_k_win.shape[2]
    n_phys = cache_k_cmp.shape[1]
    assert t % Q_BLOCK == 0 and t >= WINDOW + Q_BLOCK and page % CMP_STRIDE == 0 and ts == 4 and d == D_MODEL
    assert past_len % SLC_BLOCK == 0
    np_rows, ns_rows = bp * t, bs * ts

    row = lambda v: v.reshape(1, -1).astype(F32)
    w_in_b = jnp.pad(w_in[0], ((0, 0), (0, Z_COLS - w_in.shape[2]))).astype(BF16)
    qn = row(jnp.tile(q_norm[0], N_HEADS))
    kn1 = row(jnp.tile(k_norm[0, 1], N_KV_HEADS))
    kn2 = row(jnp.tile(k_norm[0, 2], N_KV_HEADS))
    bd = jnp.asarray(np.kron(np.eye(N_HEADS), np.ones((HEAD_DIM, HEAD_DIM))), BF16)
    pw = (row(norm_mix[0]), w_in_b, qn, kn1, kn2, conv_w[0].astype(F32), bd)
    cw_k = _compress_weights(cmp_pe_k[0], cmp_w1_k[0], cmp_w2_k[0])
    cw_v = _compress_weights(cmp_pe_v[0], cmp_w1_v[0], cmp_w2_v[0])
    kn0 = row(k_norm[0, 0])
    fext = _bias_by_distance(rel_bias)
    pt_flat = page_table.reshape(-1).astype(jnp.int32)
    wr = jnp.zeros((d, LANES), F32).at[:, :N_EXPERTS].set(w_router_expert[0])
    wr = wr.at[:, ROUTER_GROUP_LANE:ROUTER_GROUP_LANE + N_GROUPS].set(w_router_group[0])
    br = jnp.zeros((1, LANES), F32).at[0, :N_EXPERTS].set(b_router_expert[0])
    br = br.at[0, ROUTER_GROUP_LANE:ROUTER_GROUP_LANE + N_GROUPS].set(b_router_group[0])
    wo_b = w_out[0].astype(BF16)
    wple_b = w_ple[0].astype(BF16)
    wpg_b = w_ple_gate[0].astype(BF16)

    tm_p = _row_tile(t, cap=1024)
    (co_p, q_p, kc_p, vc_p, ks_p, vs_p, kw_p, vw_p, gt_p, cs_p, ksb, vsb, kwb, vwb, kc_t, vc_t, ks_t, vs_t) = _project(
        x_prompt.reshape(np_rows, d), bp, t, tm_p, pw)
    chunk_w = CMP_STRIDE * KV_DIM
    kcmp_p = _compress_rows(kc_p.reshape(bp, t // CMP_STRIDE, chunk_w), cw_k, kn0, True)
    vcmp_p = _compress_rows(vc_p.reshape(bp, t // CMP_STRIDE, chunk_w), cw_v, kn0, False)
    at_p = _attn_prompt(q_p, gt_p, kcmp_p, vcmp_p, ksb, vsb, kwb, vwb, fext, bp, t)

    st = state_conv[0].astype(F32)
    s0 = jnp.repeat(st[:, 0], ts, axis=0)
    s1 = jnp.repeat(st[:, 1], ts, axis=0)
    (co_s, q_s, kc_s, vc_s, ks_s, vs_s, kw_s, vw_s, gt_s, u_s) = _project(
        x_sample.reshape(ns_rows, d), bs, ts, ns_rows, pw, state=(s0, s1))
    feature_major = lambda c, n, rows_: jnp.transpose(c[0], (0, 2, 3, 1)).reshape(n, KV_DIM, rows_)
    kcmp_s = _compress_pages(feature_major(cache_k_cmp, n_phys, page), pt_flat, bs, n_pages, cw_k, kn0, True)
    vcmp_s = _compress_pages(feature_major(cache_v_cmp, n_phys, page), pt_flat, bs, n_pages, cw_v, kn0, False)
    at_s = _attn_sample(q_s, gt_s, kcmp_s, vcmp_s, ks_s, vs_s, kw_s, vw_s,
                        feature_major(cache_k_slc, n_phys, page), feature_major(cache_v_slc, n_phys, page),
                        feature_major(cache_k_win, bs, w_buf), feature_major(cache_v_win, bs, w_buf),
                        pt_flat, fext, bs, ts, n_pages, page)

    tp1 = _row_tile(np_rows)
    ts1 = _row_tile(ns_rows)
    nf = row(norm_ffn[0])
    h_p, hn_p, rt_p, cnt_p = _post1(x_prompt.reshape(np_rows, d), co_p, at_p, wo_b, nf, wr, br,
                                    jnp.zeros((8, LANES), F32), tp1)
    h_s, hn_s, rt_s, cnt_s = _post1(x_sample.reshape(ns_rows, d), co_s, at_s, wo_b, nf, wr, br, cnt_p, ts1)

    counts = cnt_s[0, :N_EXPERTS].astype(jnp.int32)
    padded = (counts + EXPERT_ROWS - 1) // EXPERT_ROWS * EXPERT_ROWS
    pad_end = jnp.cumsum(padded)
    pad_start = pad_end - padded
    n_assign = 2 * (np_rows + ns_rows)
    n_blk = (n_assign + N_EXPERTS * (EXPERT_ROWS - 1) + EXPERT_ROWS - 1) // EXPERT_ROWS
    blk_first = jnp.arange(n_blk, dtype=jnp.int32) * EXPERT_ROWS
    blk_expert = jnp.minimum(jnp.sum((pad_end[None, :] <= blk_first[:, None]).astype(jnp.int32), axis=1),
                             N_EXPERTS - 1)
    n_used = (pad_end[-1:] // EXPERT_ROWS).astype(jnp.int32)

    def dest_of(rt):
        e = rt[:, 0:2].astype(jnp.int32).reshape(-1)
        return pad_start[e] + rt[:, 2:4].astype(jnp.int32).reshape(-1)

    dest_p = dest_of(rt_p)
    dest_s = dest_of(rt_s)

    xs = jnp.zeros((n_blk * EXPERT_ROWS * TOKEN_TILE_ROWS, LANES), F32)
    xs = _scatter_rows(dest_p, hn_p, xs)
    xs = _scatter_rows(dest_s, hn_s, xs)
    yb = _experts(blk_expert, n_used, xs, w_exp_gate[0], w_exp_up[0], w_exp_down[0])

    bpg = row(b_ple_gate[0])
    npl = row(norm_ple[0])
    y_p = _post2(dest_p, h_p, rt_p, p_prompt[0].reshape(np_rows, -1), yb, wple_b, wpg_b, bpg, npl, tp1)
    y_s = _post2(dest_s, h_s, rt_s, p_sample[0].reshape(ns_rows, -1), yb, wple_b, wpg_b, bpg, npl, ts1)

    kv5 = lambda a, b, s: a.reshape(1, b, s, N_KV_HEADS, HEAD_DIM)
    wp = min(WINDOW, t)
    win_p = lambda a: kv5(a, bp, t)[:, :, t - wp:]
    win_s = lambda c, new: jnp.concatenate([c[0], new.reshape(bs, ts, N_KV_HEADS, HEAD_DIM)], axis=1)[None, :, ts:]
    conv_p = cs_p[:, 8 - (CONV_K - 1):][None]
    conv_s = u_s.reshape(bs, ts, CONV_DIM)[:, ts - (CONV_K - 1):][None]
    from_t = lambda a: jnp.transpose(a.reshape(bp, N_KV_HEADS, HEAD_DIM, t), (0, 3, 1, 2))[None]
    return (y_p.reshape(bp, t, d), y_s.reshape(bs, ts, d),
            from_t(kc_t), from_t(vc_t), from_t(ks_t), from_t(vs_t), win_p(kw_p), win_p(vw_p), conv_p,
            kv5(kc_s, bs, ts), kv5(vc_s, bs, ts), kv5(ks_s, bs, ts), kv5(vs_s, bs, ts),
            win_s(cache_k_win, kw_s), win_s(cache_v_win, vw_s), conv_s)
```

```python
import functools
import math

import numpy as np
import jax
import jax.numpy as jnp
from jax import lax
from jax.experimental import pallas as pl
from jax.experimental.pallas import tpu as pltpu

F32 = jnp.float32
BF16 = jnp.bfloat16
NEG_INF = float("-inf")
MASKED = -1e30

HEAD_DIM = 64
N_HEADS = 8
N_KV_HEADS = 2
GROUP = N_HEADS // N_KV_HEADS
CONV_DIM = 512
ATTN_DIM = 512
KV_DIM = N_KV_HEADS * HEAD_DIM
N_BRANCH = 3
CONV_K = 3
CMP_BLOCK = 32
CMP_STRIDE = 16
CMP_HIDDEN = 256
SLC_BLOCK = 64
N_SEL = 16
WINDOW = 512
Q_BLOCK = 128
FORCE_SCORE = 1e4
NUM_BUCKETS = 32
MAX_DISTANCE = 128
N_GROUPS = 4
EXPERTS_PER_GROUP = 8
N_EXPERTS = N_GROUPS * EXPERTS_PER_GROUP
D_EXPERT = 512
EPS = 1e-6

D_MODEL = 1024
LANES = 128
TOKEN_TILE_ROWS = D_MODEL // LANES
Z_COLS = 3 * CONV_DIM + ATTN_DIM + 6 * KV_DIM + LANES
BIAS_DMAX = 768
EXPERT_ROWS = 512
ROUTER_GROUP_LANE = 32
SCATTER_TOKENS = 256
COMPRESS_SEQS = 4
SAMPLE_SEQS = 2
VMEM_LIMIT = 56 * 1024 * 1024


def _cparams(*sem):
    return pltpu.CompilerParams(dimension_semantics=sem, vmem_limit_bytes=VMEM_LIMIT)


def _dot(a, b):
    return jnp.dot(a, b, preferred_element_type=F32)


def _dot_nt(a, b):
    return lax.dot_general(a, b, (((1,), (1,)), ((), ())), preferred_element_type=F32)


def _split3(x):
    hi = x.astype(BF16)
    r = x - hi.astype(F32)
    mid = r.astype(BF16)
    lo = (r - mid.astype(F32)).astype(BF16)
    return hi, mid, lo


def _rms(x, g):
    return x * lax.rsqrt(jnp.mean(x * x, axis=-1, keepdims=True) + EPS) * g


def _head_rms(x, bd, g):
    hi, mid, _ = _split3(x * x)
    ss = _dot(hi, bd) + _dot(mid, bd)
    return x * lax.rsqrt(ss * (1.0 / HEAD_DIM) + EPS) * g


def _sigmoid(x):
    return 1.0 / (1.0 + jnp.exp(-x))


def _store_token_tiles(ref, x, n):
    r = x.shape[1] // LANES
    for j in range(r):
        ref[pl.ds(j, n, stride=r), :] = x[:, j * LANES:(j + 1) * LANES]


def _load_token_tiles(ref, n, r, first=0):
    return jnp.concatenate([ref[pl.ds(first + j, n, stride=r), :] for j in range(r)], axis=-1)


def _softmax_parts(s):
    m = jnp.max(s, axis=-1, keepdims=True)
    m = jnp.where(m == NEG_INF, 0.0, m)
    p = jnp.exp(s - m)
    l = jnp.sum(p, axis=-1, keepdims=True)
    return p, l


def _proj_body(sample, tm, *refs):
    if sample:
        (x_ref, nm_ref, w_ref, qn_ref, kn1_ref, kn2_ref, cw_ref, bd_ref, s0_ref, s1_ref,
         co_ref, q_ref, kc_ref, vc_ref, ks_ref, vs_ref, kw_ref, vw_ref, gt_ref, u_ref) = refs
    else:
        (x_ref, nm_ref, w_ref, qn_ref, kn1_ref, kn2_ref, cw_ref, bd_ref,
         co_ref, q_ref, kc_ref, vc_ref, ks_ref, vs_ref, kw_ref, vw_ref, gt_ref, cs_ref,
         ksb_ref, vsb_ref, kwb_ref, vwb_ref, kct_ref, vct_ref, kst_ref, vst_ref, carry_ref) = refs

    xn = _rms(x_ref[...], nm_ref[...]).astype(BF16)

    def seg(a, b):
        return _dot(xn, w_ref[:, a:b])

    c3 = 3 * CONV_DIM
    u = seg(2 * CONV_DIM, c3) * seg(0, CONV_DIM)
    bg = seg(CONV_DIM, 2 * CONV_DIM)
    row = lax.broadcasted_iota(jnp.int32, (tm, 1), 0)
    um1 = pltpu.roll(u, 1, axis=0)
    um2 = pltpu.roll(u, 2, axis=0)
    if sample:
        r = row & 3
        s0 = s0_ref[...]
        s1 = s1_ref[...]
        prev1 = jnp.where(r == 0, s1, um1)
        prev2 = jnp.where(r == 0, s0, jnp.where(r == 1, s1, um2))
        u_ref[...] = u
    else:
        @pl.when(pl.program_id(1) == 0)
        def _():
            carry_ref[...] = jnp.zeros_like(carry_ref)
        c = carry_ref[...]
        prev1 = jnp.where(row == 0, c[7:8], um1)
        prev2 = jnp.where(row == 0, c[6:7], jnp.where(row == 1, c[7:8], um2))
        carry_ref[...] = u[tm - 8:tm]
        cs_ref[0] = u[tm - 8:tm]
    cw = cw_ref[...]
    y = cw[0:1] * prev2 + cw[1:2] * prev1 + cw[2:3] * u
    co_ref[...] = (bg * y).astype(BF16)

    bd = bd_ref[...]
    q = _head_rms(seg(c3, c3 + ATTN_DIM), bd, qn_ref[...]) * (HEAD_DIM ** -0.5)
    lane = lax.broadcasted_iota(jnp.int32, (tm, LANES), 1)
    low = lane < HEAD_DIM

    def head_planes(x, fill):
        return [jnp.where(low, x if h == 0 else pltpu.roll(x, HEAD_DIM, axis=1), fill) for h in range(N_KV_HEADS)]

    if sample:
        q_ref[...] = q
    else:
        for hd in range(N_HEADS):
            pair = q[:, (hd // 2) * LANES:(hd // 2 + 1) * LANES]
            if hd % 2:
                pair = pltpu.roll(pair, HEAD_DIM, axis=1)
            q_ref[0, hd] = jnp.where(low, pair, 0.0).astype(BF16)

    k0 = c3 + ATTN_DIM
    bdk = bd[:KV_DIM, :KV_DIM]
    kc = seg(k0, k0 + KV_DIM)
    vc = seg(k0 + KV_DIM, k0 + 2 * KV_DIM)
    ks = _head_rms(seg(k0 + 2 * KV_DIM, k0 + 3 * KV_DIM), bdk, kn1_ref[...])
    vs = seg(k0 + 3 * KV_DIM, k0 + 4 * KV_DIM)
    kw = _head_rms(seg(k0 + 4 * KV_DIM, k0 + 5 * KV_DIM), bdk, kn2_ref[...])
    vw = seg(k0 + 5 * KV_DIM, k0 + 6 * KV_DIM)
    kc_ref[...] = kc
    vc_ref[...] = vc
    ks_ref[...] = ks
    vs_ref[...] = vs
    kw_ref[...] = kw
    vw_ref[...] = vw
    if not sample:
        kct_ref[0] = kc.T
        vct_ref[0] = vc.T
        kst_ref[0] = ks.T
        vst_ref[0] = vs.T
    gates = _sigmoid(seg(k0 + 6 * KV_DIM, k0 + 6 * KV_DIM + LANES))
    if sample:
        gt_ref[...] = gates
    else:
        pos = pl.program_id(1) * tm + lax.broadcasted_iota(jnp.int32, (tm, LANES), 0)
        block_onehot = jnp.where(lane - HEAD_DIM == (pos >> 6), 1.0, 0.0)
        for h, (a, b, c, e) in enumerate(zip(head_planes(ks, block_onehot), head_planes(vs, 1.0),
                                             head_planes(kw, 0.0), head_planes(vw, 1.0))):
            ksb_ref[h] = a.astype(BF16)
            vsb_ref[h] = b.astype(BF16)
            kwb_ref[h] = c.astype(BF16)
            vwb_ref[h] = e.astype(BF16)
        gt_ref[0] = gates
        gt_ref[1] = pltpu.roll(gates, LANES - GROUP * N_BRANCH, axis=1)


def _project(x2d, batch, seq, tm, weights, state=None):
    n, d = x2d.shape
    sample = state is not None
    nt = seq // tm if not sample else 1
    const = lambda shape: pl.BlockSpec(shape, lambda b, t: (0,) * len(shape))
    rows = lambda w: pl.BlockSpec((tm, w), lambda b, t: (b * nt + t, 0))
    nm, w_in, qn, kn1, kn2, cw, bd = weights
    in_specs = [rows(d), const(nm.shape), const(w_in.shape), const(qn.shape), const(kn1.shape),
                const(kn2.shape), const(cw.shape), const(bd.shape)]
    args = [x2d, nm, w_in, qn, kn1, kn2, cw, bd]
    kv_f32 = [jax.ShapeDtypeStruct((n, KV_DIM), F32)] * 6
    if sample:
        in_specs += [rows(CONV_DIM), rows(CONV_DIM)]
        args += list(state)
        out_shape = ([jax.ShapeDtypeStruct((n, CONV_DIM), BF16), jax.ShapeDtypeStruct((n, ATTN_DIM), F32)]
                     + kv_f32 + [jax.ShapeDtypeStruct((n, LANES), F32), jax.ShapeDtypeStruct((n, CONV_DIM), F32)])
        out_specs = [rows(CONV_DIM), rows(ATTN_DIM)] + [rows(KV_DIM)] * 6 + [rows(LANES), rows(CONV_DIM)]
        scratch = []
        grid = (1, 1)
    else:
        planes = pl.BlockSpec((N_KV_HEADS, tm, LANES), lambda b, t: (0, b * nt + t, 0))
        out_shape = ([jax.ShapeDtypeStruct((n, CONV_DIM), BF16),
                      jax.ShapeDtypeStruct((batch, N_HEADS, seq, LANES), BF16)]
                     + kv_f32 + [jax.ShapeDtypeStruct((N_KV_HEADS, n, LANES), F32),
                                 jax.ShapeDtypeStruct((batch, 8, CONV_DIM), F32)]
                     + [jax.ShapeDtypeStruct((N_KV_HEADS, n, LANES), BF16)] * 4
                     + [jax.ShapeDtypeStruct((batch, KV_DIM, seq), F32)] * 4)
        out_specs = ([rows(CONV_DIM), pl.BlockSpec((1, N_HEADS, tm, LANES), lambda b, t: (b, 0, t, 0))]
                     + [rows(KV_DIM)] * 6 + [planes, pl.BlockSpec((1, 8, CONV_DIM), lambda b, t: (b, 0, 0))]
                     + [planes] * 4 + [pl.BlockSpec((1, KV_DIM, tm), lambda b, t: (b, 0, t))] * 4)
        scratch = [pltpu.VMEM((8, CONV_DIM), F32)]
        grid = (batch, nt)
    return pl.pallas_call(
        functools.partial(_proj_body, sample, tm),
        grid=grid, in_specs=in_specs, out_specs=out_specs, out_shape=out_shape, scratch_shapes=scratch,
        compiler_params=_cparams("arbitrary", "arbitrary"),
        name="proj_sample" if sample else "proj_prompt",
    )(*args)


def _gelu_tanh(x):
    cdf = 0.5 * (1.0 + jnp.tanh(math.sqrt(2.0 / math.pi) * (x + 0.044715 * (x * x * x))))
    return x * cdf


def _compress_core(norm, x, pe_ref, we_ref, w2_ref, g_ref):
    n = x.shape[0]
    a0 = _dot((x + pe_ref[0:1]).astype(BF16), we_ref[0])
    a1 = _dot((x + pe_ref[1:2]).astype(BF16), we_ref[1])
    hid = a0 + pltpu.roll(a1, n - 1, axis=0)
    w2 = w2_ref[...]
    outs = []
    for h in range(N_KV_HEADS):
        act = _gelu_tanh(hid[:, h * CMP_HIDDEN:(h + 1) * CMP_HIDDEN])
        o = _dot(act.astype(BF16), w2)
        if norm:
            o = _rms(o, g_ref[...])
        outs.append(o)
    return outs


def _compress_rows_body(norm, x_ref, pe_ref, we_ref, w2_ref, g_ref, o_ref):
    outs = _compress_core(norm, x_ref[0], pe_ref, we_ref, w2_ref, g_ref)
    for h in range(N_KV_HEADS):
        o_ref[0, h] = jnp.concatenate([outs[h], jnp.zeros_like(outs[h])], axis=-1).astype(BF16)


def _fetch_pages(pt_ref, cache_ref, buf_ref, sem_ref, step, slot, n_ops, priority):
    base = step * n_ops

    def issue(j, _):
        pltpu.make_async_copy(cache_ref.at[pt_ref[base + j]], buf_ref.at[slot, j], sem_ref.at[slot]).start(
            priority=priority)
        return 0

    lax.fori_loop(0, n_ops, issue, 0)


def _paged_prefetch(pt_ref, caches, bufs, sems, n_ops):
    i = pl.program_id(0)
    slot = i & 1
    for k, (cache_ref, buf_ref, sem_ref) in enumerate(zip(caches, bufs, sems)):
        @pl.when(i == 0)
        def _():
            _fetch_pages(pt_ref, cache_ref, buf_ref, sem_ref, 0, 0, n_ops, k % 2)

        @pl.when(i + 1 < pl.num_programs(0))
        def _():
            _fetch_pages(pt_ref, cache_ref, buf_ref, sem_ref, i + 1, 1 - slot, n_ops, k % 2)

    for cache_ref, buf_ref, sem_ref in zip(caches, bufs, sems):
        pltpu.make_async_copy(cache_ref.at[pl.ds(0, n_ops)], buf_ref.at[slot], sem_ref.at[slot]).wait()
    return slot


def _compress_pages_body(norm, nch, n_seq, n_pages, pt_ref, cache_ref, pe_ref, we_ref, w2_ref, g_ref, o_ref,
                         rows_ref, buf_ref, sem_ref):
    n_ops = n_seq * n_pages
    slot = _paged_prefetch(pt_ref, [cache_ref], [buf_ref], [sem_ref], n_ops)
    page = buf_ref.shape[3]
    for j in range(n_ops):
        rows_ref[j * page:(j + 1) * page, :] = buf_ref[slot, j].T
    x = jnp.concatenate([rows_ref[pl.ds(r, n_seq * nch, stride=CMP_STRIDE), :] for r in range(CMP_STRIDE)], axis=-1)
    out = jnp.concatenate(_compress_core(norm, x, pe_ref, we_ref, w2_ref, g_ref), axis=-1)
    for b in range(n_seq):
        o_ref[b] = out[b * nch:(b + 1) * nch]


def _compress_weights(pe, w1, w2):
    w1r = w1.reshape(2, CMP_STRIDE, HEAD_DIM, CMP_HIDDEN)
    eye = jnp.eye(N_KV_HEADS, dtype=w1.dtype)
    we = jnp.einsum("jrdc,hk->jrhdkc", w1r, eye).reshape(2, CMP_STRIDE * KV_DIM, N_KV_HEADS * CMP_HIDDEN)
    per = pe.reshape(2, CMP_STRIDE, 1, HEAD_DIM)
    pex = jnp.broadcast_to(per, (2, CMP_STRIDE, N_KV_HEADS, HEAD_DIM)).reshape(2, CMP_STRIDE * KV_DIM)
    return pex.astype(F32), we.astype(BF16), w2.astype(BF16)


def _compress_rows(rows3, cw, gain, norm):
    b, nch, width = rows3.shape
    pex, we, w2 = cw
    const = lambda a: pl.BlockSpec(a.shape, lambda i: (0,) * a.ndim)
    return pl.pallas_call(
        functools.partial(_compress_rows_body, norm),
        grid=(b,),
        in_specs=[pl.BlockSpec((1, nch, width), lambda i: (i, 0, 0)), const(pex), const(we), const(w2), const(gain)],
        out_specs=pl.BlockSpec((1, N_KV_HEADS, nch, LANES), lambda i: (i, 0, 0, 0)),
        out_shape=jax.ShapeDtypeStruct((b, N_KV_HEADS, nch, LANES), BF16),
        compiler_params=_cparams("arbitrary"),
        name="compress_rows",
    )(rows3, pex, we, w2, gain)


def _compress_pages(cache_t, pt_flat, n_batch, n_pages, cw, gain, norm):
    _, _, page = cache_t.shape
    nch = n_pages * page // CMP_STRIDE
    pex, we, w2 = cw
    n_seq = COMPRESS_SEQS if n_batch % COMPRESS_SEQS == 0 else 1
    const = lambda a: pl.BlockSpec(a.shape, lambda i, pt: (0,) * a.ndim)
    n_ops = n_seq * n_pages
    grid_spec = pltpu.PrefetchScalarGridSpec(
        num_scalar_prefetch=1, grid=(n_batch // n_seq,),
        in_specs=[pl.BlockSpec(memory_space=pl.ANY), const(pex), const(we), const(w2), const(gain)],
        out_specs=pl.BlockSpec((n_seq, nch, KV_DIM), lambda i, pt: (i, 0, 0)),
        scratch_shapes=[pltpu.VMEM((n_ops * page, KV_DIM), F32), pltpu.VMEM((2, n_ops, KV_DIM, page), F32),
                        pltpu.SemaphoreType.DMA((2,))])
    return pl.pallas_call(
        functools.partial(_compress_pages_body, norm, nch, n_seq, n_pages),
        grid_spec=grid_spec,
        out_shape=jax.ShapeDtypeStruct((n_batch, nch, KV_DIM), F32),
        compiler_params=_cparams("arbitrary"),
        name="compress_pages",
    )(pt_flat, cache_t, pex, we, w2, gain)


def _rel_bucket(dist):
    n = jnp.maximum(dist, 0)
    max_exact = NUM_BUCKETS // 2
    nf = jnp.maximum(n, 1).astype(F32)
    large = max_exact + (jnp.log(nf / max_exact) / math.log(MAX_DISTANCE / max_exact)
                         * (NUM_BUCKETS - max_exact)).astype(jnp.int32)
    large = jnp.minimum(large, NUM_BUCKETS - 1)
    return jnp.where(n < max_exact, n, large)


def _bias_by_distance(rel_bias):
    d = jnp.arange(BIAS_DMAX, dtype=jnp.int32)
    f = rel_bias.astype(F32)[_rel_bucket(d)]
    f = (f - f[BIAS_DMAX - 1:BIAS_DMAX]).T
    return jnp.concatenate([f, jnp.full((N_HEADS, 1), NEG_INF, F32)], axis=1)


def _bias_index(d, valid):
    return np.where(valid, np.clip(d, 0, BIAS_DMAX - 1), BIAS_DMAX).astype(np.int32)


def _bias_table(fext, d, valid):
    return jnp.take(fext, jnp.asarray(_bias_index(d, valid)), axis=1)


def _toeplitz_body(n_rows, v_ref, o_ref):
    x = jnp.broadcast_to(v_ref[0], (n_rows, v_ref.shape[2]))
    o_ref[0] = pltpu.roll(x, 0, axis=1, stride=1, stride_axis=0)


def _toeplitz_rows(v, n_rows):
    h, w = v.shape
    return pl.pallas_call(
        functools.partial(_toeplitz_body, n_rows),
        grid=(h,),
        in_specs=[pl.BlockSpec((1, 1, w), lambda i: (i, 0, 0))],
        out_specs=pl.BlockSpec((1, n_rows, w), lambda i: (i, 0, 0)),
        out_shape=jax.ShapeDtypeStruct((h, n_rows, w), F32),
        compiler_params=_cparams("arbitrary"),
        name="toeplitz_rows",
    )(v.reshape(h, 1, w))


def _select_blocks(imp_t, srow, qpos, n_rank):
    qblk = qpos >> 6
    forced = (srow == 0) | (srow == qblk) | (srow == qblk - 1)
    valid = (srow << 6) <= qpos
    imp_t = jnp.where(valid, imp_t + jnp.where(forced, FORCE_SCORE, 0.0), NEG_INF)
    n_rows = imp_t.shape[0]
    assert n_rows % 8 == 0
    slabs = [imp_t[a:a + 8] for a in range(0, n_rows, 8)]
    rows8 = [srow[a:a + 8] for a in range(0, n_rows, 8)]
    cnts = [jnp.zeros(x.shape, jnp.int32) for x in slabs]
    for s in range(n_rank):
        r = imp_t[s:s + 1, :]
        for j, x in enumerate(slabs):
            if 8 * j > s:
                beats = r >= x
            elif 8 * j + 7 <= s:
                beats = r > x
            else:
                beats = (r > x) | ((r == x) & (rows8[j] > s))
            cnts[j] = cnts[j] + jnp.where(beats, 1, 0)
    cnt = jnp.concatenate(cnts, axis=0)
    return jnp.where((cnt < N_SEL) & valid, 1.0, 0.0)


def _exp_pv(s, m, v):
    return _dot(jnp.exp((s - m).astype(BF16)), v)


def _normalize_pv(pv):
    return pv / jnp.maximum(pv[:, HEAD_DIM:HEAD_DIM + 1], 1e-30)


def _attn_prompt_body(kt, n_slc, q_ref, gt_ref, kcmp_ref, vcmp_ref, ks_ref, vs_ref, kw_ref, vw_ref,
                      ctab_ref, ntab_ref, wtab_ref, ovl_ref, o_ref, sa_ref, sb_ref):
    i = pl.program_id(2)
    qb = Q_BLOCK
    rows = GROUP * qb
    n_cmp_pad = kcmp_ref.shape[2]
    first_near_block = (qb // SLC_BLOCK) * jnp.maximum(i - 1, 0)
    near_start = pl.multiple_of(jnp.maximum(i - 1, 0) * qb, qb)
    win_start = pl.multiple_of(jnp.maximum(i * qb - WINDOW, 0), qb)
    band = WINDOW + qb
    gates = gt_ref[0]
    lane = lax.broadcasted_iota(jnp.int32, (qb, LANES), 1)
    q0 = q_ref[0].reshape(rows, LANES)

    s = _dot_nt(q0, kw_ref[0, pl.ds(win_start, band), :]) + wtab_ref[0].reshape(rows, band)
    m = jnp.max(s, axis=-1, keepdims=True)
    m = jnp.where(m == NEG_INF, 0.0, m)
    o_w = _normalize_pv(_exp_pv(s, m, vw_ref[0, pl.ds(win_start, band), :]))

    per_qb = qb // CMP_STRIDE
    ctab = pltpu.roll(ctab_ref[...].reshape(rows, 2 * n_cmp_pad), i * per_qb, axis=1)[:, n_cmp_pad:]
    s = _dot_nt(q0, kcmp_ref[0, 0]) + ctab
    p, l = _softmax_parts(s)
    pn = p / jnp.maximum(l, 1e-30)
    o_c = _dot(pn.astype(BF16), vcmp_ref[0, 0])

    psum = pn[0:qb] + pn[qb:2 * qb] + pn[2 * qb:3 * qb] + pn[3 * qb:4 * qb]
    hi, mid, lo = _split3(psum)
    ovl = ovl_ref[...]
    imp_t = _dot_nt(ovl, hi) + _dot_nt(ovl, mid) + _dot_nt(ovl, lo)
    srow = lax.broadcasted_iota(jnp.int32, (n_slc, qb), 0)
    qpos_t = i * qb + lax.broadcasted_iota(jnp.int32, (n_slc, qb), 1)
    sel_t = _select_blocks(imp_t, srow, qpos_t, n_slc)
    sel_t = jnp.concatenate([sel_t, jnp.zeros((LANES - n_slc, qb), F32)], axis=0)
    sel = sel_t.T

    def query_with_mask(keep):
        m = pltpu.roll(jnp.where(keep, 0.0, MASKED), HEAD_DIM, axis=1).astype(BF16)
        m = jnp.concatenate([jnp.where(lane < HEAD_DIM, q0[g * qb:(g + 1) * qb], m) for g in range(GROUP)], axis=0)
        return m

    q_near = query_with_mask(sel > 0.5)
    q_far = query_with_mask((sel > 0.5) & (lane < first_near_block))

    s = _dot_nt(q_near, ks_ref[0, pl.ds(near_start, 2 * qb), :]) + ntab_ref[0].reshape(rows, 2 * qb)
    m0 = jnp.max(s, axis=-1, keepdims=True)
    m0 = jnp.where(m0 == NEG_INF, 0.0, m0)
    a0 = _exp_pv(s, m0, vs_ref[0, pl.ds(near_start, 2 * qb), :])

    n_kt = ks_ref.shape[1] // kt

    def scores(t):
        k0 = pl.multiple_of(jnp.minimum(t, n_kt - 1) * kt, kt)
        return _dot_nt(q_far, ks_ref[0, pl.ds(k0, kt), :])

    def consume(t, s, m_old, acc):
        k0 = pl.multiple_of(jnp.minimum(t, n_kt - 1) * kt, kt)
        m_new = jnp.maximum(m_old, jnp.max(s, axis=-1, keepdims=True))
        return m_new, jnp.exp(m_old - m_new) * acc + _exp_pv(s, m_new, vs_ref[0, pl.ds(k0, kt), :])

    def far_pair(u, carry):
        m, acc = carry
        sb_ref[...] = scores(2 * u + 1)
        m, acc = consume(2 * u, sa_ref[...], m, acc)
        sa_ref[...] = scores(2 * u + 2)
        return consume(2 * u + 1, sb_ref[...], m, acc)

    n_far = (near_start + kt - 1) // kt
    sa_ref[...] = scores(0)
    _, acc_s = lax.fori_loop(0, (n_far + 1) // 2, far_pair, (m0, a0))
    o_s = _normalize_pv(acc_s)

    heads_out = []
    for g in range(GROUP):
        c = g * N_BRANCH
        sl = slice(g * qb, (g + 1) * qb)
        heads_out.append(gates[:, c:c + 1] * o_c[sl] + gates[:, c + 1:c + 2] * o_s[sl] + gates[:, c + 2:c + 3] * o_w[sl])
    tiles = [jnp.where(lane < HEAD_DIM, heads_out[2 * j], pltpu.roll(heads_out[2 * j + 1], HEAD_DIM, axis=1))
             for j in range(GROUP // 2)]
    o_ref[...] = jnp.concatenate(tiles, axis=-1).astype(BF16)


def _attn_prompt(qp, gates, kcmp, vcmp, ksb, vsb, kwb, vwb, fext, batch, seq):
    qb = Q_BLOCK
    nqb = seq // qb
    n_slc = seq // SLC_BLOCK
    assert n_slc <= LANES - HEAD_DIM
    n_cmp_pad = kcmp.shape[2]
    n_cmp = n_cmp_pad - 1
    kt = min(512, seq)
    assert (seq // kt) % 2 == 0
    band = WINDOW + qb
    iq = np.arange(qb)

    per_qb = qb // CMP_STRIDE
    half = 2 * per_qb
    m = np.arange(-half, half)
    d = iq[:, None] - (m[None, :] * CMP_STRIDE + CMP_BLOCK - 1)
    assert d[:, 0].min() >= MAX_DISTANCE and d[:, -1].max() < 0
    ctab = jnp.concatenate([jnp.zeros((N_HEADS, qb, n_cmp_pad - half), F32), _bias_table(fext, d, d >= 0),
                            jnp.full((N_HEADS, qb, n_cmp_pad - half), NEG_INF, F32)], axis=2)
    assert per_qb * (nqb - 1) < n_cmp_pad
    assert (n_cmp_pad - 1) * CMP_STRIDE + CMP_BLOCK - 1 >= seq and n_cmp == n_cmp_pad - 1

    nv = WINDOW // qb + 1
    kw_ = WINDOW + band
    w = qb + kw_
    assert w % LANES == 0
    m = np.arange(w)
    m = np.where(m < kw_, m, m - w)
    dj = WINDOW - m
    wide = _toeplitz_rows(_bias_table(fext, dj, (dj >= 0) & (dj < WINDOW)), qb)
    wtab = jnp.stack([wide[:, :, WINDOW - qb * v:WINDOW - qb * v + band] for v in range(nv)], axis=0)
    ntab = jnp.stack([wide[:, :, WINDOW - qb * v:WINDOW - qb * v + 2 * qb] for v in range(2)], axis=0)
    assert 2 * qb <= WINDOW

    c_start = np.arange(n_cmp_pad) * CMP_STRIDE
    s_start = np.arange(n_slc) * SLC_BLOCK
    ovl = ((c_start[None, :] < s_start[:, None] + SLC_BLOCK) & (c_start[None, :] + CMP_BLOCK > s_start[:, None])
           & (np.arange(n_cmp_pad) < n_cmp)[None, :])
    ovl = jnp.asarray(ovl, BF16)

    per_head = lambda: pl.BlockSpec((1, seq, LANES), lambda b, h, i: (h, b, 0))
    return pl.pallas_call(
        functools.partial(_attn_prompt_body, kt, n_slc),
        grid=(batch, N_KV_HEADS, nqb),
        in_specs=[pl.BlockSpec((1, GROUP, qb, LANES), lambda b, h, i: (b, h, i, 0)),
                  pl.BlockSpec((1, qb, LANES), lambda b, h, i: (h, b * nqb + i, 0)),
                  pl.BlockSpec((1, 1, n_cmp_pad, LANES), lambda b, h, i: (b, h, 0, 0)),
                  pl.BlockSpec((1, 1, n_cmp_pad, LANES), lambda b, h, i: (b, h, 0, 0)),
                  per_head(), per_head(), per_head(), per_head(),
                  pl.BlockSpec((GROUP, qb, 2 * n_cmp_pad), lambda b, h, i: (h, 0, 0)),
                  pl.BlockSpec((1, GROUP, qb, 2 * qb), lambda b, h, i: (jnp.minimum(i, 1), h, 0, 0)),
                  pl.BlockSpec((1, GROUP, qb, band), lambda b, h, i: (jnp.minimum(i, nv - 1), h, 0, 0)),
                  pl.BlockSpec(ovl.shape, lambda b, h, i: (0, 0))],
        out_specs=pl.BlockSpec((qb, GROUP * HEAD_DIM), lambda b, h, i: (b * nqb + i, h)),
        out_shape=jax.ShapeDtypeStruct((batch * seq, ATTN_DIM), BF16),
        scratch_shapes=[pltpu.VMEM((GROUP * qb, kt), F32)] * 2,
        compiler_params=_cparams("arbitrary", "arbitrary", "arbitrary"),
        name="attn_prompt",
    )(qp, gates, kcmp, vcmp, ksb, vsb, kwb, vwb, ctab, ntab, wtab, ovl)


def _attn_sample_body(n_seq, n_pages, ts, past_len, pt_ref, cache_k_ref, cache_v_ref, *refs):
    kbuf, vbuf, ksem, vsem = refs[-4:]
    n_ops = n_seq * n_pages
    slot = _paged_prefetch(pt_ref, [cache_k_ref, cache_v_ref], [kbuf, vbuf], [ksem, vsem], n_ops)
    for b in range(n_seq):
        _attn_sample_one(b, [kbuf.at[slot, b * n_pages + j] for j in range(n_pages)],
                         [vbuf.at[slot, b * n_pages + j] for j in range(n_pages)], refs[:-4], ts, past_len)


def _attn_sample_one(b, kpages, vpages, refs, ts, past_len):
    (q_ref, gt_ref, kcmp_ref, vcmp_ref, ksn_ref, vsn_ref, kwc_ref, vwc_ref, kwn_ref, vwn_ref,
     ctab_ref, stab_ref, sntab_ref, wtab_ref, wntab_ref, eexp_ref, ovl_ref, o_ref) = refs
    rows = GROUP * N_KV_HEADS * ts
    rq = N_KV_HEADS * ts
    q = q_ref[b]
    gates = gt_ref[b]

    s = _dot_nt(q, kcmp_ref[b].astype(BF16)) + ctab_ref[...]
    p, l = _softmax_parts(s)
    pn = p / jnp.maximum(l, 1e-30)
    o_c = _dot(pn.astype(BF16), vcmp_ref[b].astype(BF16))

    psum = pn[0:rq]
    for g in range(1, GROUP):
        psum = psum + pn[g * rq:(g + 1) * rq]
    hi, mid, lo = _split3(psum)
    ovl = ovl_ref[...]
    imp = _dot(hi, ovl) + _dot(mid, ovl) + _dot(lo, ovl)
    n_slc = -(-(past_len + ts) // SLC_BLOCK)
    blk = lax.broadcasted_iota(jnp.int32, (rq, LANES), 1)
    qpos = past_len + (lax.broadcasted_iota(jnp.int32, (rq, LANES), 0) & (ts - 1))
    qblk = qpos >> 6
    forced = (blk == 0) | (blk == qblk) | (blk == qblk - 1)
    valid = ((blk << 6) <= qpos) & (blk < n_slc)
    imp = jnp.where(valid, imp + jnp.where(forced, FORCE_SCORE, 0.0), NEG_INF)
    cnt = jnp.zeros((rq, LANES), jnp.int32)
    for sidx in range(n_slc):
        r = imp[:, sidx:sidx + 1]
        beats = (r > imp) | ((r == imp) & (blk > sidx))
        cnt = cnt + jnp.where(beats, 1, 0)
    sel = jnp.where((cnt < N_SEL) & valid, 1.0, 0.0)
    sel = jnp.concatenate([sel] * GROUP, axis=0).astype(BF16)

    kc_t = jnp.concatenate([p_[...] for p_ in kpages], axis=1).astype(BF16)
    vc_t = jnp.concatenate([p_[...] for p_ in vpages], axis=1).astype(BF16)
    mexp = _dot(sel, eexp_ref[...])
    s1 = jnp.where(mexp > 0.5, _dot(q, kc_t) + stab_ref[...], NEG_INF)
    last = sel[:, n_slc - 1:n_slc].astype(F32)
    s2 = jnp.where(last > 0.5, _dot_nt(q, ksn_ref[b].astype(BF16)) + sntab_ref[...], NEG_INF)
    m = jnp.maximum(jnp.max(s1, axis=-1, keepdims=True), jnp.max(s2, axis=-1, keepdims=True))
    m = jnp.where(m == NEG_INF, 0.0, m)
    p1 = jnp.exp(s1 - m)
    p2 = jnp.exp(s2 - m)
    l = jnp.sum(p1, axis=-1, keepdims=True) + jnp.sum(p2, axis=-1, keepdims=True)
    o_s = (_dot_nt(p1.astype(BF16), vc_t) + _dot(p2.astype(BF16), vsn_ref[b].astype(BF16))) / jnp.maximum(l, 1e-30)

    s1 = _dot(q, kwc_ref[b].astype(BF16)) + wtab_ref[...]
    s2 = _dot_nt(q, kwn_ref[b].astype(BF16)) + wntab_ref[...]
    m = jnp.maximum(jnp.max(s1, axis=-1, keepdims=True), jnp.max(s2, axis=-1, keepdims=True))
    m = jnp.where(m == NEG_INF, 0.0, m)
    p1 = jnp.exp(s1 - m)
    p2 = jnp.exp(s2 - m)
    l = jnp.sum(p1, axis=-1, keepdims=True) + jnp.sum(p2, axis=-1, keepdims=True)
    o_w = (_dot_nt(p1.astype(BF16), vwc_ref[b].astype(BF16))
           + _dot(p2.astype(BF16), vwn_ref[b].astype(BF16))) / jnp.maximum(l, 1e-30)

    o_ref[b] = gates[:, 0:1] * o_c + gates[:, 1:2] * o_s + gates[:, 2:3] * o_w


def _attn_sample(q_s, gates_s, kcmp, vcmp, ks_new, vs_new, kw_new, vw_new, cache_ks, cache_vs,
                 cache_kw, cache_vw, pt_flat, fext, n_batch, ts, n_pages, page):
    past_len = n_pages * page
    w_buf = cache_kw.shape[2]
    rows = GROUP * N_KV_HEADS * ts
    n_new = 8
    n_cmp_pad = kcmp.shape[1]
    n_cmp = n_cmp_pad - 1
    n_slc = -(-(past_len + ts) // SLC_BLOCK)

    q5 = q_s.reshape(n_batch, ts, N_KV_HEADS, GROUP, HEAD_DIM).transpose(0, 3, 2, 1, 4)
    eye = jnp.eye(N_KV_HEADS, dtype=q_s.dtype)
    qr = jnp.einsum("bghtd,hk->bghtkd", q5, eye).reshape(n_batch, rows, LANES).astype(BF16)
    g5 = gates_s[:, :N_HEADS * N_BRANCH].reshape(n_batch, ts, N_KV_HEADS, GROUP, N_BRANCH).transpose(0, 3, 2, 1, 4)
    gr = jnp.pad(g5.reshape(n_batch, rows, N_BRANCH), ((0, 0), (0, 0), (0, LANES - N_BRANCH)))
    pad_new = lambda a: jnp.pad(a.reshape(n_batch, ts, KV_DIM), ((0, 0), (0, n_new - ts), (0, 0)))
    ks_new, vs_new, kw_new, vw_new = map(pad_new, (ks_new, vs_new, kw_new, vw_new))

    g_i, h_i, t_i = np.meshgrid(np.arange(GROUP), np.arange(N_KV_HEADS), np.arange(ts), indexing="ij")
    head = (h_i * GROUP + g_i).reshape(rows)
    tq = t_i.reshape(rows)
    pos_q = past_len + tq

    f_rows = fext[jnp.asarray(head)]

    def table(d, valid):
        return jnp.take_along_axis(f_rows, jnp.asarray(_bias_index(d, valid)), axis=1)

    nn = np.arange(n_cmp_pad)
    d = pos_q[:, None] - (nn[None, :] * CMP_STRIDE + CMP_BLOCK - 1)
    ctab = table(d, (d >= 0) & (nn < n_cmp)[None, :])
    near = np.arange(past_len - MAX_DISTANCE, past_len)
    d = pos_q[:, None] - near[None, :]
    assert past_len >= MAX_DISTANCE and d.min() >= 0
    stab = jnp.concatenate([jnp.zeros((rows, past_len - MAX_DISTANCE), F32), table(d, d >= 0)], axis=1)
    jn = np.arange(n_new)
    d = tq[:, None] - jn[None, :]
    sntab = table(d, (d >= 0) & (jn < ts)[None, :])
    pos_w = past_len - w_buf + np.arange(w_buf)
    d = pos_q[:, None] - pos_w[None, :]
    wtab = table(d, (d >= 0) & (d < WINDOW) & (pos_w >= 0)[None, :])
    wntab = table(tq[:, None] - jn[None, :], (tq[:, None] >= jn[None, :]) & (jn < ts)[None, :])

    eexp = jnp.asarray(np.arange(past_len)[None, :] // SLC_BLOCK == np.arange(LANES)[:, None], BF16)
    c_start = nn * CMP_STRIDE
    s_start = np.arange(LANES) * SLC_BLOCK
    ovl = jnp.asarray((c_start[:, None] < s_start[None, :] + SLC_BLOCK) & (c_start[:, None] + CMP_BLOCK > s_start[None, :])
                      & (nn < n_cmp)[:, None] & (np.arange(LANES) < n_slc)[None, :], BF16)

    n_seq = SAMPLE_SEQS if n_batch % SAMPLE_SEQS == 0 else 1
    n_ops = n_seq * n_pages
    const = lambda a: pl.BlockSpec(a.shape, lambda b, pt: (0,) * a.ndim)
    per_b = lambda a: pl.BlockSpec((n_seq,) + a.shape[1:], lambda b, pt: (b,) + (0,) * (a.ndim - 1))
    any_spec = pl.BlockSpec(memory_space=pl.ANY)
    small = [qr, gr, kcmp, vcmp, ks_new, vs_new, cache_kw, cache_vw, kw_new, vw_new]
    consts = [ctab, stab, sntab, wtab, wntab, eexp, ovl]
    page_buf = pltpu.VMEM((2, n_ops, KV_DIM, page), F32)
    grid_spec = pltpu.PrefetchScalarGridSpec(
        num_scalar_prefetch=1, grid=(n_batch // n_seq,),
        in_specs=[any_spec, any_spec] + [per_b(a) for a in small] + [const(a) for a in consts],
        out_specs=pl.BlockSpec((n_seq, rows, LANES), lambda b, pt: (b, 0, 0)),
        scratch_shapes=[page_buf, page_buf, pltpu.SemaphoreType.DMA((2,)), pltpu.SemaphoreType.DMA((2,))])
    o = pl.pallas_call(
        functools.partial(_attn_sample_body, n_seq, n_pages, ts, past_len),
        grid_spec=grid_spec,
        out_shape=jax.ShapeDtypeStruct((n_batch, rows, LANES), F32),
        compiler_params=_cparams("arbitrary"),
        name="attn_sample",
    )(pt_flat, cache_ks, cache_vs, *small, *consts)
    o6 = o.reshape(n_batch, GROUP, N_KV_HEADS, ts, N_KV_HEADS, HEAD_DIM)
    o5 = jnp.stack([o6[:, :, h, :, h] for h in range(N_KV_HEADS)], axis=2)
    return o5.transpose(0, 3, 2, 1, 4).reshape(n_batch * ts, ATTN_DIM).astype(BF16)


def _post1_body(tm, x_ref, co_ref, at_ref, wo_ref, nf_ref, wr_ref, br_ref, tri_ref, run0_ref,
                h_ref, hn_ref, rt_ref, cnt_ref, run_ref):
    @pl.when(pl.program_id(0) == 0)
    def _():
        run_ref[...] = run0_ref[...]

    h = x_ref[...] + _dot(co_ref[...], wo_ref[0:CONV_DIM]) + _dot(at_ref[...], wo_ref[CONV_DIM:CONV_DIM + ATTN_DIM])
    hn = _rms(h, nf_ref[...])
    h_ref[...] = h
    _store_token_tiles(hn_ref, hn, tm)

    hi = hn.astype(BF16)
    lo = (hn - hi.astype(F32)).astype(BF16)
    wr = wr_ref[...]
    whi = wr.astype(BF16)
    wlo = (wr - whi.astype(F32)).astype(BF16)
    logits = _dot(hi, whi) + _dot(lo, whi) + _dot(hi, wlo) + br_ref[...]

    lane_i = lax.broadcasted_iota(jnp.int32, (tm, LANES), 1)
    lane = lane_i.astype(F32)
    big = float(LANES)
    gmask = (lane_i >= ROUTER_GROUP_LANE) & (lane_i < ROUTER_GROUP_LANE + N_GROUPS)
    lg = jnp.where(gmask, logits, NEG_INF)
    eg = jnp.exp(lg - jnp.max(lg, axis=-1, keepdims=True))
    pg = eg / jnp.sum(eg, axis=-1, keepdims=True)
    gw = jnp.max(pg, axis=-1, keepdims=True)
    grp = jnp.min(jnp.where(gmask & (pg == gw), lane, big), axis=-1, keepdims=True) - ROUTER_GROUP_LANE

    group_of_lane = (lane_i >> 3).astype(F32)
    emask = (lane_i < N_EXPERTS) & (group_of_lane == grp)
    le = jnp.where(emask, logits, NEG_INF)
    ee = jnp.exp(le - jnp.max(le, axis=-1, keepdims=True))
    pe = jnp.where(emask, ee / jnp.sum(ee, axis=-1, keepdims=True), -1.0)
    v1 = jnp.max(pe, axis=-1, keepdims=True)
    i1 = jnp.min(jnp.where(pe == v1, lane, big), axis=-1, keepdims=True)
    pe2 = jnp.where(lane == i1, -1.0, pe)
    v2 = jnp.max(pe2, axis=-1, keepdims=True)
    i2 = jnp.min(jnp.where(pe2 == v2, lane, big), axis=-1, keepdims=True)
    tot = v1 + v2
    w1 = v1 / tot * gw
    w2 = v2 / tot * gw

    oh1 = jnp.where(lane == i1, 1.0, 0.0)
    oh2 = jnp.where(lane == i2, 1.0, 0.0)
    both = oh1 + oh2
    before = _dot(tri_ref[...], both.astype(BF16)) + run_ref[0:1]
    r1 = jnp.sum(oh1 * before, axis=-1, keepdims=True)
    r2 = jnp.sum(oh2 * before, axis=-1, keepdims=True)
    run = run_ref[0:1] + jnp.sum(both, axis=0, keepdims=True)
    run_ref[...] = jnp.broadcast_to(run, run_ref.shape)
    cnt_ref[...] = jnp.broadcast_to(run, cnt_ref.shape)

    rt = jnp.where(lane_i == 0, i1, 0.0)
    rt = jnp.where(lane_i == 1, i2, rt)
    rt = jnp.where(lane_i == 2, r1, rt)
    rt = jnp.where(lane_i == 3, r2, rt)
    rt = jnp.where(lane_i == 4, w1, rt)
    rt = jnp.where(lane_i == 5, w2, rt)
    rt_ref[...] = rt


def _post1(x2d, co, at, wo, nf, wr, br, run0, tm):
    n, d = x2d.shape
    tri = jnp.asarray(np.tril(np.ones((tm, tm), np.float32), -1), BF16)
    rows = lambda w: pl.BlockSpec((tm, w), lambda i: (i, 0))
    const = lambda a: pl.BlockSpec(a.shape, lambda i: (0,) * a.ndim)
    return pl.pallas_call(
        functools.partial(_post1_body, tm),
        grid=(n // tm,),
        in_specs=[rows(d), rows(CONV_DIM), rows(ATTN_DIM), const(wo), const(nf), const(wr), const(br),
                  const(tri), const(run0)],
        out_specs=[rows(d), pl.BlockSpec((tm * TOKEN_TILE_ROWS, LANES), lambda i: (i, 0)), rows(LANES),
                   pl.BlockSpec((8, LANES), lambda i: (0, 0))],
        out_shape=[jax.ShapeDtypeStruct((n, d), F32), jax.ShapeDtypeStruct((n * TOKEN_TILE_ROWS, LANES), F32),
                   jax.ShapeDtypeStruct((n, LANES), F32), jax.ShapeDtypeStruct((8, LANES), F32)],
        scratch_shapes=[pltpu.VMEM((8, LANES), F32)],
        compiler_params=_cparams("arbitrary"),
        name="post1",
    )(x2d, co, at, wo, nf, wr, br, tri, run0)


def _token_copy(src_ref, dst_ref, s, d, sem):
    r = TOKEN_TILE_ROWS
    return pltpu.make_async_copy(src_ref.at[pl.ds(pl.multiple_of(s * r, r), r)],
                                 dst_ref.at[pl.ds(pl.multiple_of(d * r, r), r)], sem)


def _scatter_rows_body(ts, dest_ref, src_ref, init_ref, out_ref, sem):
    del init_ref
    base = pl.program_id(0) * (2 * ts)

    def issue(t, _):
        for k in range(2):
            _token_copy(src_ref, out_ref, t, dest_ref[base + 2 * t + k], sem).start(priority=k)
        return 0

    lax.fori_loop(0, ts, issue, 0)
    for _ in range(2):
        pltpu.make_async_copy(src_ref, out_ref.at[pl.ds(0, ts * TOKEN_TILE_ROWS)], sem).wait()


def _scatter_rows(dest, src, slots):
    n_tok = dest.shape[0] // 2
    ts = min(SCATTER_TOKENS, n_tok)
    assert n_tok % ts == 0
    any_spec = pl.BlockSpec(memory_space=pl.ANY)
    return pl.pallas_call(
        functools.partial(_scatter_rows_body, ts),
        grid_spec=pltpu.PrefetchScalarGridSpec(
            num_scalar_prefetch=1, grid=(n_tok // ts,),
            in_specs=[pl.BlockSpec((ts * TOKEN_TILE_ROWS, LANES), lambda i, dest: (i, 0)), any_spec],
            out_specs=any_spec, scratch_shapes=[pltpu.SemaphoreType.DMA(())]),
        out_shape=jax.ShapeDtypeStruct(slots.shape, slots.dtype),
        input_output_aliases={2: 0},
        compiler_params=pltpu.CompilerParams(dimension_semantics=("arbitrary",)),
        name="scatter_rows",
    )(dest, src, slots)


def _experts_body(be_ref, nu_ref, x_ref, wg_ref, wu_ref, wd_ref, o_ref, wg_s, wu_s, wd_s):
    i = pl.program_id(0)

    @pl.when(i < nu_ref[0])
    def _():
        prev = be_ref[jnp.maximum(i - 1, 0)]

        @pl.when((i == 0) | (be_ref[i] != prev))
        def _():
            wg_s[...] = wg_ref[0].astype(BF16)
            wu_s[...] = wu_ref[0].astype(BF16)
            wd_s[...] = wd_ref[0].astype(BF16)

        x = _load_token_tiles(x_ref, EXPERT_ROWS, TOKEN_TILE_ROWS).astype(BF16)
        g = _dot(x, wg_s[...])
        u = _dot(x, wu_s[...])
        a = g * _sigmoid(g) * u
        _store_token_tiles(o_ref, _dot(a.astype(BF16), wd_s[...]), EXPERT_ROWS)

    @pl.when(i >= nu_ref[0])
    def _():
        o_ref[...] = jnp.zeros_like(o_ref)


def _experts(blk_expert, n_used, xs, wg, wu, wd):
    blk_rows = EXPERT_ROWS * TOKEN_TILE_ROWS
    n_blk = xs.shape[0] // blk_rows
    _, d, de = wg.shape
    xmap = lambda i, be, nu: (jnp.minimum(i, jnp.maximum(nu[0] - 1, 0)), 0)
    wmap = lambda i, be, nu: (be[jnp.minimum(i, jnp.maximum(nu[0] - 1, 0))], 0, 0)
    grid_spec = pltpu.PrefetchScalarGridSpec(
        num_scalar_prefetch=2, grid=(n_blk,),
        in_specs=[pl.BlockSpec((blk_rows, LANES), xmap), pl.BlockSpec((1, d, de), wmap),
                  pl.BlockSpec((1, d, de), wmap), pl.BlockSpec((1, de, d), wmap)],
        out_specs=pl.BlockSpec((blk_rows, LANES), lambda i, be, nu: (i, 0)),
        scratch_shapes=[pltpu.VMEM((d, de), BF16), pltpu.VMEM((d, de), BF16), pltpu.VMEM((de, d), BF16)])
    return pl.pallas_call(
        _experts_body, grid_spec=grid_spec,
        out_shape=jax.ShapeDtypeStruct(xs.shape, F32),
        compiler_params=_cparams("arbitrary"),
        name="experts",
    )(blk_expert, n_used, xs, wg, wu, wd)


def _post2_body(tm, dest_ref, h_ref, rt_ref, p_ref, yb_ref, wple_ref, wpg_ref, bpg_ref, np_ref, o_ref, buf, sem):
    i = pl.program_id(0)
    n = pl.num_programs(0)

    def fetch(step, slot):
        base = step * (2 * tm)

        def issue(t, _):
            for k in range(2):
                _token_copy(yb_ref, buf.at[slot], dest_ref[base + 2 * t + k], k * tm + t, sem.at[slot]).start(priority=k)
            return 0

        lax.fori_loop(0, tm, issue, 0)

    @pl.when(i == 0)
    def _():
        fetch(0, 0)

    @pl.when(i + 1 < n)
    def _():
        fetch(i + 1, (i + 1) & 1)

    slot = i & 1

    pltpu.make_async_copy(yb_ref.at[pl.ds(0, 2 * tm * TOKEN_TILE_ROWS)], buf.at[slot], sem.at[slot]).wait()
    rt = rt_ref[...]
    y0 = _load_token_tiles(buf.at[slot], tm, TOKEN_TILE_ROWS)
    y1 = _load_token_tiles(buf.at[slot], tm, TOKEN_TILE_ROWS, first=tm * TOKEN_TILE_ROWS)
    h = h_ref[...] + (y0 * rt[:, 4:5] + y1 * rt[:, 5:6])
    gate = _sigmoid(_dot(_rms(h, np_ref[...]).astype(BF16), wpg_ref[...]) + bpg_ref[...])
    o_ref[...] = h + gate * _dot(p_ref[...].astype(BF16), wple_ref[...])


def _post2(dest, h, rt, p2d, yb, wple, wpg, bpg, npl, tm):
    n, d = h.shape
    rows = lambda w: pl.BlockSpec((tm, w), lambda i, dest: (i, 0))
    const = lambda a: pl.BlockSpec(a.shape, lambda i, dest: (0,) * a.ndim)
    grid_spec = pltpu.PrefetchScalarGridSpec(
        num_scalar_prefetch=1, grid=(n // tm,),
        in_specs=[rows(d), rows(LANES), rows(p2d.shape[1]), pl.BlockSpec(memory_space=pl.ANY), const(wple),
                  const(wpg), const(bpg), const(npl)],
        out_specs=rows(d),
        scratch_shapes=[pltpu.VMEM((2, 2 * tm * TOKEN_TILE_ROWS, LANES), F32), pltpu.SemaphoreType.DMA((2,))])
    return pl.pallas_call(
        functools.partial(_post2_body, tm),
        grid_spec=grid_spec,
        out_shape=jax.ShapeDtypeStruct((n, d), F32),
        compiler_params=_cparams("arbitrary"),
        name="post2",
    )(dest, h, rt, p2d, yb, wple, wpg, bpg, npl)


def _row_tile(n, cap=512):
    t = min(cap, n)
    assert n % t == 0 and t % 8 == 0
    return t


def kernel(x_prompt, x_sample, p_prompt, p_sample, cache_k_cmp, cache_v_cmp, cache_k_slc, cache_v_slc, cache_k_win, cache_v_win, state_conv, page_table, w_in, w_out, conv_w, norm_mix, norm_ffn, norm_ple, q_norm, k_norm, cmp_pe_k, cmp_w1_k, cmp_w2_k, cmp_pe_v, cmp_w1_v, cmp_w2_v, rel_bias, w_router_group, b_router_group, w_router_expert, b_router_expert, w_exp_gate, w_exp_up, w_exp_down, w_ple, w_ple_gate, b_ple_gate):
    assert w_in.shape[0] == 1, "single-layer step"
    bp, t, d = x_prompt.shape
    bs, ts, _ = x_sample.shape
    n_pages = page_table.shape[1]
    page = cache_k_cmp.shape[2]
    past_len = n_pages * page
    w_buf = cache_k_win.shape[2]
    n_phys = cache_k_cmp.shape[1]
    assert t % Q_BLOCK == 0 and t >= WINDOW + Q_BLOCK and page % CMP_STRIDE == 0 and ts == 4 and d == D_MODEL
    assert past_len % SLC_BLOCK == 0
    np_rows, ns_rows = bp * t, bs * ts

    row = lambda v: v.reshape(1, -1).astype(F32)
    w_in_b = jnp.pad(w_in[0], ((0, 0), (0, Z_COLS - w_in.shape[2]))).astype(BF16)
    qn = row(jnp.tile(q_norm[0], N_HEADS))
    kn1 = row(jnp.tile(k_norm[0, 1], N_KV_HEADS))
    kn2 = row(jnp.tile(k_norm[0, 2], N_KV_HEADS))
    bd = jnp.asarray(np.kron(np.eye(N_HEADS), np.ones((HEAD_DIM, HEAD_DIM))), BF16)
    pw = (row(norm_mix[0]), w_in_b, qn, kn1, kn2, conv_w[0].astype(F32), bd)
    cw_k = _compress_weights(cmp_pe_k[0], cmp_w1_k[0], cmp_w2_k[0])
    cw_v = _compress_weights(cmp_pe_v[0], cmp_w1_v[0], cmp_w2_v[0])
    kn0 = row(k_norm[0, 0])
    fext = _bias_by_distance(rel_bias)
    pt_flat = page_table.reshape(-1).astype(jnp.int32)
    wr = jnp.zeros((d, LANES), F32).at[:, :N_EXPERTS].set(w_router_expert[0])
    wr = wr.at[:, ROUTER_GROUP_LANE:ROUTER_GROUP_LANE + N_GROUPS].set(w_router_group[0])
    br = jnp.zeros((1, LANES), F32).at[0, :N_EXPERTS].set(b_router_expert[0])
    br = br.at[0, ROUTER_GROUP_LANE:ROUTER_GROUP_LANE + N_GROUPS].set(b_router_group[0])
    wo_b = w_out[0].astype(BF16)
    wple_b = w_ple[0].astype(BF16)
    wpg_b = w_ple_gate[0].astype(BF16)

    tm_p = _row_tile(t, cap=1024)
    (co_p, q_p, kc_p, vc_p, ks_p, vs_p, kw_p, vw_p, gt_p, cs_p, ksb, vsb, kwb, vwb, kc_t, vc_t, ks_t, vs_t) = _project(
        x_prompt.reshape(np_rows, d), bp, t, tm_p, pw)
    chunk_w = CMP_STRIDE * KV_DIM
    kcmp_p = _compress_rows(kc_p.reshape(bp, t // CMP_STRIDE, chunk_w), cw_k, kn0, True)
    vcmp_p = _compress_rows(vc_p.reshape(bp, t // CMP_STRIDE, chunk_w), cw_v, kn0, False)
    at_p = _attn_prompt(q_p, gt_p, kcmp_p, vcmp_p, ksb, vsb, kwb, vwb, fext, bp, t)

    st = state_conv[0].astype(F32)
    s0 = jnp.repeat(st[:, 0], ts, axis=0)
    s1 = jnp.repeat(st[:, 1], ts, axis=0)
    (co_s, q_s, kc_s, vc_s, ks_s, vs_s, kw_s, vw_s, gt_s, u_s) = _project(
        x_sample.reshape(ns_rows, d), bs, ts, ns_rows, pw, state=(s0, s1))
    feature_major = lambda c, n, rows_: jnp.transpose(c[0], (0, 2, 3, 1)).reshape(n, KV_DIM, rows_)
    kcmp_s = _compress_pages(feature_major(cache_k_cmp, n_phys, page), pt_flat, bs, n_pages, cw_k, kn0, True)
    vcmp_s = _compress_pages(feature_major(cache_v_cmp, n_phys, page), pt_flat, bs, n_pages, cw_v, kn0, False)
    at_s = _attn_sample(q_s, gt_s, kcmp_s, vcmp_s, ks_s, vs_s, kw_s, vw_s,
                        feature_major(cache_k_slc, n_phys, page), feature_major(cache_v_slc, n_phys, page),
                        feature_major(cache_k_win, bs, w_buf), feature_major(cache_v_win, bs, w_buf),
                        pt_flat, fext, bs, ts, n_pages, page)

    tp1 = _row_tile(np_rows)
    ts1 = _row_tile(ns_rows)
    nf = row(norm_ffn[0])
    h_p, hn_p, rt_p, cnt_p = _post1(x_prompt.reshape(np_rows, d), co_p, at_p, wo_b, nf, wr, br,
                                    jnp.zeros((8, LANES), F32), tp1)
    h_s, hn_s, rt_s, cnt_s = _post1(x_sample.reshape(ns_rows, d), co_s, at_s, wo_b, nf, wr, br, cnt_p, ts1)

    counts = cnt_s[0, :N_EXPERTS].astype(jnp.int32)
    padded = (counts + EXPERT_ROWS - 1) // EXPERT_ROWS * EXPERT_ROWS
    pad_end = jnp.cumsum(padded)
    pad_start = pad_end - padded
    n_assign = 2 * (np_rows + ns_rows)
    n_blk = (n_assign + N_EXPERTS * (EXPERT_ROWS - 1) + EXPERT_ROWS - 1) // EXPERT_ROWS
    blk_first = jnp.arange(n_blk, dtype=jnp.int32) * EXPERT_ROWS
    blk_expert = jnp.minimum(jnp.sum((pad_end[None, :] <= blk_first[:, None]).astype(jnp.int32), axis=1),
                             N_EXPERTS - 1)
    n_used = (pad_end[-1:] // EXPERT_ROWS).astype(jnp.int32)

    def dest_of(rt):
        e = rt[:, 0:2].astype(jnp.int32).reshape(-1)
        return pad_start[e] + rt[:, 2:4].astype(jnp.int32).reshape(-1)

    dest_p = dest_of(rt_p)
    dest_s = dest_of(rt_s)

    xs = jnp.zeros((n_blk * EXPERT_ROWS * TOKEN_TILE_ROWS, LANES), F32)
    xs = _scatter_rows(dest_p, hn_p, xs)
    xs = _scatter_rows(dest_s, hn_s, xs)
    yb = _experts(blk_expert, n_used, xs, w_exp_gate[0], w_exp_up[0], w_exp_down[0])

    bpg = row(b_ple_gate[0])
    npl = row(norm_ple[0])
    y_p = _post2(dest_p, h_p, rt_p, p_prompt[0].reshape(np_rows, -1), yb, wple_b, wpg_b, bpg, npl, tp1)
    y_s = _post2(dest_s, h_s, rt_s, p_sample[0].reshape(ns_rows, -1), yb, wple_b, wpg_b, bpg, npl, ts1)

    kv5 = lambda a, b, s: a.reshape(1, b, s, N_KV_HEADS, HEAD_DIM)
    wp = min(WINDOW, t)
    win_p = lambda a: kv5(a, bp, t)[:, :, t - wp:]
    win_s = lambda c, new: jnp.concatenate([c[0], new.reshape(bs, ts, N_KV_HEADS, HEAD_DIM)], axis=1)[None, :, ts:]
    conv_p = cs_p[:, 8 - (CONV_K - 1):][None]
    conv_s = u_s.reshape(bs, ts, CONV_DIM)[:, ts - (CONV_K - 1):][None]
    from_t = lambda a: jnp.transpose(a.reshape(bp, N_KV_HEADS, HEAD_DIM, t), (0, 3, 1, 2))[None]
    return (y_p.reshape(bp, t, d), y_s.reshape(bs, ts, d),
            from_t(kc_t), from_t(vc_t), from_t(ks_t), from_t(vs_t), win_p(kw_p), win_p(vw_p), conv_p,
            kv5(kc_s, bs, ts), kv5(vc_s, bs, ts), kv5(ks_s, bs, ts), kv5(vs_s, bs, ts),
            win_s(cache_k_win, kw_s), win_s(cache_v_win, vw_s), conv_s)
```

```python
import functools
import math

import numpy as np
import jax
import jax.numpy as jnp
from jax import lax
from jax.experimental import pallas as pl
from jax.experimental.pallas import tpu as pltpu

F32 = jnp.float32
BF16 = jnp.bfloat16
NEG_INF = float("-inf")
MASKED = -1e30

HEAD_DIM = 64
N_HEADS = 8
N_KV_HEADS = 2
GROUP = N_HEADS // N_KV_HEADS
CONV_DIM = 512
ATTN_DIM = 512
KV_DIM = N_KV_HEADS * HEAD_DIM
N_BRANCH = 3
CONV_K = 3
CMP_BLOCK = 32
CMP_STRIDE = 16
CMP_HIDDEN = 256
SLC_BLOCK = 64
N_SEL = 16
WINDOW = 512
Q_BLOCK = 128
FORCE_SCORE = 1e4
NUM_BUCKETS = 32
MAX_DISTANCE = 128
N_GROUPS = 4
EXPERTS_PER_GROUP = 8
N_EXPERTS = N_GROUPS * EXPERTS_PER_GROUP
D_EXPERT = 512
EPS = 1e-6

D_MODEL = 1024
LANES = 128
TOKEN_TILE_ROWS = D_MODEL // LANES
Z_COLS = 3 * CONV_DIM + ATTN_DIM + 6 * KV_DIM + LANES
BIAS_DMAX = 768
EXPERT_ROWS = 512
ROUTER_GROUP_LANE = 32
SCATTER_TOKENS = 256
DMA_ISSUE_UNROLL = 8
COMPRESS_SEQS = 8
SAMPLE_SEQS = 4
VMEM_LIMIT = 56 * 1024 * 1024


def _cparams(*sem):
    return pltpu.CompilerParams(dimension_semantics=sem, vmem_limit_bytes=VMEM_LIMIT)


def _dot(a, b):
    return jnp.dot(a, b, preferred_element_type=F32)


def _dot_nt(a, b):
    return lax.dot_general(a, b, (((1,), (1,)), ((), ())), preferred_element_type=F32)


def _split3(x):
    hi = x.astype(BF16)
    r = x - hi.astype(F32)
    mid = r.astype(BF16)
    lo = (r - mid.astype(F32)).astype(BF16)
    return hi, mid, lo


def _rms(x, g):
    return x * lax.rsqrt(jnp.mean(x * x, axis=-1, keepdims=True) + EPS) * g


def _head_rms(x, bd, g):
    hi, mid, _ = _split3(x * x)
    ss = _dot(hi, bd) + _dot(mid, bd)
    return x * lax.rsqrt(ss * (1.0 / HEAD_DIM) + EPS) * g


def _sigmoid(x):
    return 1.0 / (1.0 + jnp.exp(-x))


def _store_token_tiles(ref, x, n):
    r = x.shape[1] // LANES
    for j in range(r):
        ref[pl.ds(j, n, stride=r), :] = x[:, j * LANES:(j + 1) * LANES]


def _load_token_tiles(ref, n, r, first=0):
    return jnp.concatenate([ref[pl.ds(first + j, n, stride=r), :] for j in range(r)], axis=-1)


def _softmax_parts(s):
    m = jnp.max(s, axis=-1, keepdims=True)
    m = jnp.where(m == NEG_INF, 0.0, m)
    p = jnp.exp(s - m)
    l = jnp.sum(p, axis=-1, keepdims=True)
    return p, l


def _proj_body(sample, tm, *refs):
    if sample:
        (x_ref, nm_ref, w_ref, qn_ref, kn1_ref, kn2_ref, cw_ref, bd_ref, s0_ref, s1_ref,
         co_ref, q_ref, kc_ref, vc_ref, ks_ref, vs_ref, kw_ref, vw_ref, gt_ref, u_ref) = refs
    else:
        (x_ref, nm_ref, w_ref, qn_ref, kn1_ref, kn2_ref, cw_ref, bd_ref,
         co_ref, q_ref, kc_ref, vc_ref, ks_ref, vs_ref, kw_ref, vw_ref, gt_ref, cs_ref,
         ksb_ref, vsb_ref, kwb_ref, vwb_ref, kct_ref, vct_ref, kst_ref, vst_ref, carry_ref) = refs

    xn = _rms(x_ref[...], nm_ref[...]).astype(BF16)

    def seg(a, b):
        return _dot(xn, w_ref[:, a:b])

    c3 = 3 * CONV_DIM
    u = seg(2 * CONV_DIM, c3) * seg(0, CONV_DIM)
    bg = seg(CONV_DIM, 2 * CONV_DIM)
    row = lax.broadcasted_iota(jnp.int32, (tm, 1), 0)
    um1 = pltpu.roll(u, 1, axis=0)
    um2 = pltpu.roll(u, 2, axis=0)
    if sample:
        r = row & 3
        s0 = s0_ref[...]
        s1 = s1_ref[...]
        prev1 = jnp.where(r == 0, s1, um1)
        prev2 = jnp.where(r == 0, s0, jnp.where(r == 1, s1, um2))
        u_ref[...] = u
    else:
        @pl.when(pl.program_id(1) == 0)
        def _():
            carry_ref[...] = jnp.zeros_like(carry_ref)
        c = carry_ref[...]
        prev1 = jnp.where(row == 0, c[7:8], um1)
        prev2 = jnp.where(row == 0, c[6:7], jnp.where(row == 1, c[7:8], um2))
        carry_ref[...] = u[tm - 8:tm]
        cs_ref[0] = u[tm - 8:tm]
    cw = cw_ref[...]
    y = cw[0:1] * prev2 + cw[1:2] * prev1 + cw[2:3] * u
    co_ref[...] = (bg * y).astype(BF16)

    bd = bd_ref[...]
    q = _head_rms(seg(c3, c3 + ATTN_DIM), bd, qn_ref[...]) * (HEAD_DIM ** -0.5)
    lane = lax.broadcasted_iota(jnp.int32, (tm, LANES), 1)
    low = lane < HEAD_DIM

    def head_planes(x, fill):
        return [jnp.where(low, x if h == 0 else pltpu.roll(x, HEAD_DIM, axis=1), fill) for h in range(N_KV_HEADS)]

    if sample:
        q_ref[...] = q
    else:
        for hd in range(N_HEADS):
            pair = q[:, (hd // 2) * LANES:(hd // 2 + 1) * LANES]
            if hd % 2:
                pair = pltpu.roll(pair, HEAD_DIM, axis=1)
            q_ref[0, hd] = jnp.where(low, pair, 0.0).astype(BF16)

    k0 = c3 + ATTN_DIM
    bdk = bd[:KV_DIM, :KV_DIM]
    kc = seg(k0, k0 + KV_DIM)
    vc = seg(k0 + KV_DIM, k0 + 2 * KV_DIM)
    ks = _head_rms(seg(k0 + 2 * KV_DIM, k0 + 3 * KV_DIM), bdk, kn1_ref[...])
    vs = seg(k0 + 3 * KV_DIM, k0 + 4 * KV_DIM)
    kw = _head_rms(seg(k0 + 4 * KV_DIM, k0 + 5 * KV_DIM), bdk, kn2_ref[...])
    vw = seg(k0 + 5 * KV_DIM, k0 + 6 * KV_DIM)
    kc_ref[...] = kc
    vc_ref[...] = vc
    ks_ref[...] = ks
    vs_ref[...] = vs
    kw_ref[...] = kw
    vw_ref[...] = vw
    if not sample:
        kct_ref[0] = kc.T
        vct_ref[0] = vc.T
        kst_ref[0] = ks.T
        vst_ref[0] = vs.T
    gates = _sigmoid(seg(k0 + 6 * KV_DIM, k0 + 6 * KV_DIM + LANES))
    if sample:
        gt_ref[...] = gates
    else:
        pos = pl.program_id(1) * tm + lax.broadcasted_iota(jnp.int32, (tm, LANES), 0)
        block_onehot = jnp.where(lane - HEAD_DIM == (pos >> 6), 1.0, 0.0)
        for h, (a, b, c, e) in enumerate(zip(head_planes(ks, block_onehot), head_planes(vs, 1.0),
                                             head_planes(kw, 0.0), head_planes(vw, 1.0))):
            ksb_ref[h] = a.astype(BF16)
            vsb_ref[h] = b.astype(BF16)
            kwb_ref[h] = c.astype(BF16)
            vwb_ref[h] = e.astype(BF16)
        gt_ref[0] = gates
        gt_ref[1] = pltpu.roll(gates, LANES - GROUP * N_BRANCH, axis=1)


def _project(x2d, batch, seq, tm, weights, state=None):
    n, d = x2d.shape
    sample = state is not None
    nt = seq // tm if not sample else 1
    const = lambda shape: pl.BlockSpec(shape, lambda b, t: (0,) * len(shape))
    rows = lambda w: pl.BlockSpec((tm, w), lambda b, t: (b * nt + t, 0))
    nm, w_in, qn, kn1, kn2, cw, bd = weights
    in_specs = [rows(d), const(nm.shape), const(w_in.shape), const(qn.shape), const(kn1.shape),
                const(kn2.shape), const(cw.shape), const(bd.shape)]
    args = [x2d, nm, w_in, qn, kn1, kn2, cw, bd]
    kv_f32 = [jax.ShapeDtypeStruct((n, KV_DIM), F32)] * 6
    if sample:
        in_specs += [rows(CONV_DIM), rows(CONV_DIM)]
        args += list(state)
        out_shape = ([jax.ShapeDtypeStruct((n, CONV_DIM), BF16), jax.ShapeDtypeStruct((n, ATTN_DIM), F32)]
                     + kv_f32 + [jax.ShapeDtypeStruct((n, LANES), F32), jax.ShapeDtypeStruct((n, CONV_DIM), F32)])
        out_specs = [rows(CONV_DIM), rows(ATTN_DIM)] + [rows(KV_DIM)] * 6 + [rows(LANES), rows(CONV_DIM)]
        scratch = []
        grid = (1, 1)
    else:
        planes = pl.BlockSpec((N_KV_HEADS, tm, LANES), lambda b, t: (0, b * nt + t, 0))
        out_shape = ([jax.ShapeDtypeStruct((n, CONV_DIM), BF16),
                      jax.ShapeDtypeStruct((batch, N_HEADS, seq, LANES), BF16)]
                     + kv_f32 + [jax.ShapeDtypeStruct((N_KV_HEADS, n, LANES), F32),
                                 jax.ShapeDtypeStruct((batch, 8, CONV_DIM), F32)]
                     + [jax.ShapeDtypeStruct((N_KV_HEADS, n, LANES), BF16)] * 4
                     + [jax.ShapeDtypeStruct((batch, KV_DIM, seq), F32)] * 4)
        out_specs = ([rows(CONV_DIM), pl.BlockSpec((1, N_HEADS, tm, LANES), lambda b, t: (b, 0, t, 0))]
                     + [rows(KV_DIM)] * 6 + [planes, pl.BlockSpec((1, 8, CONV_DIM), lambda b, t: (b, 0, 0))]
                     + [planes] * 4 + [pl.BlockSpec((1, KV_DIM, tm), lambda b, t: (b, 0, t))] * 4)
        scratch = [pltpu.VMEM((8, CONV_DIM), F32)]
        grid = (batch, nt)
    return pl.pallas_call(
        functools.partial(_proj_body, sample, tm),
        grid=grid, in_specs=in_specs, out_specs=out_specs, out_shape=out_shape, scratch_shapes=scratch,
        compiler_params=_cparams("arbitrary", "arbitrary"),
        name="proj_sample" if sample else "proj_prompt",
    )(*args)


def _gelu_tanh(x):
    cdf = 0.5 * (1.0 + jnp.tanh(math.sqrt(2.0 / math.pi) * (x + 0.044715 * (x * x * x))))
    return x * cdf


def _compress_core(norm, x, pe_ref, we_ref, w2_ref, g_ref):
    n = x.shape[0]
    a0 = _dot((x + pe_ref[0:1]).astype(BF16), we_ref[0])
    a1 = _dot((x + pe_ref[1:2]).astype(BF16), we_ref[1])
    hid = a0 + pltpu.roll(a1, n - 1, axis=0)
    w2 = w2_ref[...]
    outs = []
    for h in range(N_KV_HEADS):
        act = _gelu_tanh(hid[:, h * CMP_HIDDEN:(h + 1) * CMP_HIDDEN])
        o = _dot(act.astype(BF16), w2)
        if norm:
            o = _rms(o, g_ref[...])
        outs.append(o)
    return outs


def _compress_rows_body(norm, x_ref, pe_ref, we_ref, w2_ref, g_ref, o_ref):
    outs = _compress_core(norm, x_ref[0], pe_ref, we_ref, w2_ref, g_ref)
    for h in range(N_KV_HEADS):
        o_ref[0, h] = jnp.concatenate([outs[h], jnp.zeros_like(outs[h])], axis=-1).astype(BF16)


def _fetch_pages(pt_ref, cache_ref, buf_ref, sem_ref, step, slot, n_ops, priority):
    base = step * n_ops

    def issue(j, _):
        pltpu.make_async_copy(cache_ref.at[pt_ref[base + j]], buf_ref.at[slot, j], sem_ref.at[slot]).start(
            priority=priority)
        return 0

    lax.fori_loop(0, n_ops, issue, 0)


def _paged_prefetch(pt_ref, caches, bufs, sems, n_ops):
    i = pl.program_id(0)
    slot = i & 1
    for k, (cache_ref, buf_ref, sem_ref) in enumerate(zip(caches, bufs, sems)):
        @pl.when(i == 0)
        def _():
            _fetch_pages(pt_ref, cache_ref, buf_ref, sem_ref, 0, 0, n_ops, k % 2)

        @pl.when(i + 1 < pl.num_programs(0))
        def _():
            _fetch_pages(pt_ref, cache_ref, buf_ref, sem_ref, i + 1, 1 - slot, n_ops, k % 2)

    for cache_ref, buf_ref, sem_ref in zip(caches, bufs, sems):
        pltpu.make_async_copy(cache_ref.at[pl.ds(0, n_ops)], buf_ref.at[slot], sem_ref.at[slot]).wait()
    return slot


def _compress_pages_body(norm, nch, n_seq, n_pages, pt_ref, cache_ref, pe_ref, we_ref, w2_ref, g_ref, o_ref,
                         rows_ref, buf_ref, sem_ref):
    n_ops = n_seq * n_pages
    slot = _paged_prefetch(pt_ref, [cache_ref], [buf_ref], [sem_ref], n_ops)
    page = buf_ref.shape[3]
    for j in range(n_ops):
        rows_ref[j * page:(j + 1) * page, :] = buf_ref[slot, j].T
    x = jnp.concatenate([rows_ref[pl.ds(r, n_seq * nch, stride=CMP_STRIDE), :] for r in range(CMP_STRIDE)], axis=-1)
    out = jnp.concatenate(_compress_core(norm, x, pe_ref, we_ref, w2_ref, g_ref), axis=-1)
    for b in range(n_seq):
        o_ref[b] = out[b * nch:(b + 1) * nch]


def _compress_weights(pe, w1, w2):
    w1r = w1.reshape(2, CMP_STRIDE, HEAD_DIM, CMP_HIDDEN)
    eye = jnp.eye(N_KV_HEADS, dtype=w1.dtype)
    we = jnp.einsum("jrdc,hk->jrhdkc", w1r, eye).reshape(2, CMP_STRIDE * KV_DIM, N_KV_HEADS * CMP_HIDDEN)
    per = pe.reshape(2, CMP_STRIDE, 1, HEAD_DIM)
    pex = jnp.broadcast_to(per, (2, CMP_STRIDE, N_KV_HEADS, HEAD_DIM)).reshape(2, CMP_STRIDE * KV_DIM)
    return pex.astype(F32), we.astype(BF16), w2.astype(BF16)


def _compress_rows(rows3, cw, gain, norm):
    b, nch, width = rows3.shape
    pex, we, w2 = cw
    const = lambda a: pl.BlockSpec(a.shape, lambda i: (0,) * a.ndim)
    return pl.pallas_call(
        functools.partial(_compress_rows_body, norm),
        grid=(b,),
        in_specs=[pl.BlockSpec((1, nch, width), lambda i: (i, 0, 0)), const(pex), const(we), const(w2), const(gain)],
        out_specs=pl.BlockSpec((1, N_KV_HEADS, nch, LANES), lambda i: (i, 0, 0, 0)),
        out_shape=jax.ShapeDtypeStruct((b, N_KV_HEADS, nch, LANES), BF16),
        compiler_params=_cparams("arbitrary"),
        name="compress_rows",
    )(rows3, pex, we, w2, gain)


def _compress_pages(cache_t, pt_flat, n_batch, n_pages, cw, gain, norm):
    _, _, page = cache_t.shape
    nch = n_pages * page // CMP_STRIDE
    pex, we, w2 = cw
    n_seq = COMPRESS_SEQS if n_batch % COMPRESS_SEQS == 0 else 1
    const = lambda a: pl.BlockSpec(a.shape, lambda i, pt: (0,) * a.ndim)
    n_ops = n_seq * n_pages
    grid_spec = pltpu.PrefetchScalarGridSpec(
        num_scalar_prefetch=1, grid=(n_batch // n_seq,),
        in_specs=[pl.BlockSpec(memory_space=pl.ANY), const(pex), const(we), const(w2), const(gain)],
        out_specs=pl.BlockSpec((n_seq, nch, KV_DIM), lambda i, pt: (i, 0, 0)),
        scratch_shapes=[pltpu.VMEM((n_ops * page, KV_DIM), F32), pltpu.VMEM((2, n_ops, KV_DIM, page), F32),
                        pltpu.SemaphoreType.DMA((2,))])
    return pl.pallas_call(
        functools.partial(_compress_pages_body, norm, nch, n_seq, n_pages),
        grid_spec=grid_spec,
        out_shape=jax.ShapeDtypeStruct((n_batch, nch, KV_DIM), F32),
        compiler_params=_cparams("arbitrary"),
        name="compress_pages",
    )(pt_flat, cache_t, pex, we, w2, gain)


def _rel_bucket(dist):
    n = jnp.maximum(dist, 0)
    max_exact = NUM_BUCKETS // 2
    nf = jnp.maximum(n, 1).astype(F32)
    large = max_exact + (jnp.log(nf / max_exact) / math.log(MAX_DISTANCE / max_exact)
                         * (NUM_BUCKETS - max_exact)).astype(jnp.int32)
    large = jnp.minimum(large, NUM_BUCKETS - 1)
    return jnp.where(n < max_exact, n, large)


def _bias_by_distance(rel_bias):
    d = jnp.arange(BIAS_DMAX, dtype=jnp.int32)
    f = rel_bias.astype(F32)[_rel_bucket(d)]
    f = (f - f[BIAS_DMAX - 1:BIAS_DMAX]).T
    return jnp.concatenate([f, jnp.full((N_HEADS, 1), NEG_INF, F32)], axis=1)


def _bias_index(d, valid):
    return np.where(valid, np.clip(d, 0, BIAS_DMAX - 1), BIAS_DMAX).astype(np.int32)


def _bias_table(fext, d, valid):
    return jnp.take(fext, jnp.asarray(_bias_index(d, valid)), axis=1)


def _toeplitz_body(n_rows, v_ref, o_ref):
    x = jnp.broadcast_to(v_ref[0], (n_rows, v_ref.shape[2]))
    o_ref[0] = pltpu.roll(x, 0, axis=1, stride=1, stride_axis=0)


def _toeplitz_rows(v, n_rows):
    h, w = v.shape
    return pl.pallas_call(
        functools.partial(_toeplitz_body, n_rows),
        grid=(h,),
        in_specs=[pl.BlockSpec((1, 1, w), lambda i: (i, 0, 0))],
        out_specs=pl.BlockSpec((1, n_rows, w), lambda i: (i, 0, 0)),
        out_shape=jax.ShapeDtypeStruct((h, n_rows, w), F32),
        compiler_params=_cparams("arbitrary"),
        name="toeplitz_rows",
    )(v.reshape(h, 1, w))


def _select_blocks(imp_t, srow, qpos, n_rank):
    qblk = qpos >> 6
    forced = (srow == 0) | (srow == qblk) | (srow == qblk - 1)
    valid = (srow << 6) <= qpos
    imp_t = jnp.where(valid, imp_t + jnp.where(forced, FORCE_SCORE, 0.0), NEG_INF)
    n_rows = imp_t.shape[0]
    assert n_rows % 8 == 0
    slabs = [imp_t[a:a + 8] for a in range(0, n_rows, 8)]
    rows8 = [srow[a:a + 8] for a in range(0, n_rows, 8)]
    cnts = [jnp.zeros(x.shape, jnp.int32) for x in slabs]
    for s in range(n_rank):
        r = imp_t[s:s + 1, :]
        for j, x in enumerate(slabs):
            if 8 * j > s:
                beats = r >= x
            elif 8 * j + 7 <= s:
                beats = r > x
            else:
                beats = (r > x) | ((r == x) & (rows8[j] > s))
            cnts[j] = cnts[j] + jnp.where(beats, 1, 0)
    cnt = jnp.concatenate(cnts, axis=0)
    return jnp.where((cnt < N_SEL) & valid, 1.0, 0.0)


def _exp_pv(s, m, v):
    return _dot(jnp.exp((s - m).astype(BF16)), v)


def _normalize_pv(pv):
    return pv / jnp.maximum(pv[:, HEAD_DIM:HEAD_DIM + 1], 1e-30)


def _attn_prompt_body(kt, n_slc, q_ref, gt_ref, kcmp_ref, vcmp_ref, ks_ref, vs_ref, kw_ref, vw_ref,
                      ctab_ref, ntab_ref, wtab_ref, ovl_ref, o_ref, sa_ref, sb_ref):
    i = pl.program_id(2)
    qb = Q_BLOCK
    rows = GROUP * qb
    n_cmp_pad = kcmp_ref.shape[2]
    first_near_block = (qb // SLC_BLOCK) * jnp.maximum(i - 1, 0)
    near_start = pl.multiple_of(jnp.maximum(i - 1, 0) * qb, qb)
    win_start = pl.multiple_of(jnp.maximum(i * qb - WINDOW, 0), qb)
    band = WINDOW + qb
    gates = gt_ref[0]
    lane = lax.broadcasted_iota(jnp.int32, (qb, LANES), 1)
    q0 = q_ref[0].reshape(rows, LANES)

    s = _dot_nt(q0, kw_ref[0, pl.ds(win_start, band), :]) + wtab_ref[0].reshape(rows, band)
    m = jnp.max(s, axis=-1, keepdims=True)
    m = jnp.where(m == NEG_INF, 0.0, m)
    o_w = _normalize_pv(_exp_pv(s, m, vw_ref[0, pl.ds(win_start, band), :]))

    per_qb = qb // CMP_STRIDE
    ctab = pltpu.roll(ctab_ref[...].reshape(rows, 2 * n_cmp_pad), i * per_qb, axis=1)[:, n_cmp_pad:]
    s = _dot_nt(q0, kcmp_ref[0, 0]) + ctab
    p, l = _softmax_parts(s)
    pn = p / jnp.maximum(l, 1e-30)
    o_c = _dot(pn.astype(BF16), vcmp_ref[0, 0])

    psum = pn[0:qb] + pn[qb:2 * qb] + pn[2 * qb:3 * qb] + pn[3 * qb:4 * qb]
    hi, mid, lo = _split3(psum)
    ovl = ovl_ref[...]
    imp_t = _dot_nt(ovl, hi) + _dot_nt(ovl, mid) + _dot_nt(ovl, lo)
    srow = lax.broadcasted_iota(jnp.int32, (n_slc, qb), 0)
    qpos_t = i * qb + lax.broadcasted_iota(jnp.int32, (n_slc, qb), 1)
    sel_t = _select_blocks(imp_t, srow, qpos_t, n_slc)
    sel_t = jnp.concatenate([sel_t, jnp.zeros((LANES - n_slc, qb), F32)], axis=0)
    sel = sel_t.T

    def query_with_mask(keep):
        m = pltpu.roll(jnp.where(keep, 0.0, MASKED), HEAD_DIM, axis=1).astype(BF16)
        m = jnp.concatenate([jnp.where(lane < HEAD_DIM, q0[g * qb:(g + 1) * qb], m) for g in range(GROUP)], axis=0)
        return m

    q_near = query_with_mask(sel > 0.5)
    q_far = query_with_mask((sel > 0.5) & (lane < first_near_block))

    s = _dot_nt(q_near, ks_ref[0, pl.ds(near_start, 2 * qb), :]) + ntab_ref[0].reshape(rows, 2 * qb)
    m0 = jnp.max(s, axis=-1, keepdims=True)
    m0 = jnp.where(m0 == NEG_INF, 0.0, m0)
    a0 = _exp_pv(s, m0, vs_ref[0, pl.ds(near_start, 2 * qb), :])

    n_kt = ks_ref.shape[1] // kt

    def scores(t):
        k0 = pl.multiple_of(jnp.minimum(t, n_kt - 1) * kt, kt)
        return _dot_nt(q_far, ks_ref[0, pl.ds(k0, kt), :])

    def consume(t, s, m_old, acc):
        k0 = pl.multiple_of(jnp.minimum(t, n_kt - 1) * kt, kt)
        m_new = jnp.maximum(m_old, jnp.max(s, axis=-1, keepdims=True))
        return m_new, jnp.exp(m_old - m_new) * acc + _exp_pv(s, m_new, vs_ref[0, pl.ds(k0, kt), :])

    def far_pair(u, carry):
        m, acc = carry
        sb_ref[...] = scores(2 * u + 1)
        m, acc = consume(2 * u, sa_ref[...], m, acc)
        sa_ref[...] = scores(2 * u + 2)
        return consume(2 * u + 1, sb_ref[...], m, acc)

    n_far = (near_start + kt - 1) // kt
    sa_ref[...] = scores(0)
    _, acc_s = lax.fori_loop(0, (n_far + 1) // 2, far_pair, (m0, a0))
    o_s = _normalize_pv(acc_s)

    heads_out = []
    for g in range(GROUP):
        c = g * N_BRANCH
        sl = slice(g * qb, (g + 1) * qb)
        heads_out.append(gates[:, c:c + 1] * o_c[sl] + gates[:, c + 1:c + 2] * o_s[sl] + gates[:, c + 2:c + 3] * o_w[sl])
    tiles = [jnp.where(lane < HEAD_DIM, heads_out[2 * j], pltpu.roll(heads_out[2 * j + 1], HEAD_DIM, axis=1))
             for j in range(GROUP // 2)]
    o_ref[...] = jnp.concatenate(tiles, axis=-1).astype(BF16)


def _attn_prompt(qp, gates, kcmp, vcmp, ksb, vsb, kwb, vwb, fext, batch, seq):
    qb = Q_BLOCK
    nqb = seq // qb
    n_slc = seq // SLC_BLOCK
    assert n_slc <= LANES - HEAD_DIM
    n_cmp_pad = kcmp.shape[2]
    n_cmp = n_cmp_pad - 1
    kt = min(512, seq)
    assert (seq // kt) % 2 == 0
    band = WINDOW + qb
    iq = np.arange(qb)

    per_qb = qb // CMP_STRIDE
    half = 2 * per_qb
    m = np.arange(-half, half)
    d = iq[:, None] - (m[None, :] * CMP_STRIDE + CMP_BLOCK - 1)
    assert d[:, 0].min() >= MAX_DISTANCE and d[:, -1].max() < 0
    ctab = jnp.concatenate([jnp.zeros((N_HEADS, qb, n_cmp_pad - half), F32), _bias_table(fext, d, d >= 0),
                            jnp.full((N_HEADS, qb, n_cmp_pad - half), NEG_INF, F32)], axis=2)
    assert per_qb * (nqb - 1) < n_cmp_pad
    assert (n_cmp_pad - 1) * CMP_STRIDE + CMP_BLOCK - 1 >= seq and n_cmp == n_cmp_pad - 1

    nv = WINDOW // qb + 1
    kw_ = WINDOW + band
    w = qb + kw_
    assert w % LANES == 0
    m = np.arange(w)
    m = np.where(m < kw_, m, m - w)
    dj = WINDOW - m
    wide = _toeplitz_rows(_bias_table(fext, dj, (dj >= 0) & (dj < WINDOW)), qb)
    wtab = jnp.stack([wide[:, :, WINDOW - qb * v:WINDOW - qb * v + band] for v in range(nv)], axis=0)
    ntab = jnp.stack([wide[:, :, WINDOW - qb * v:WINDOW - qb * v + 2 * qb] for v in range(2)], axis=0)
    assert 2 * qb <= WINDOW

    c_start = np.arange(n_cmp_pad) * CMP_STRIDE
    s_start = np.arange(n_slc) * SLC_BLOCK
    ovl = ((c_start[None, :] < s_start[:, None] + SLC_BLOCK) & (c_start[None, :] + CMP_BLOCK > s_start[:, None])
           & (np.arange(n_cmp_pad) < n_cmp)[None, :])
    ovl = jnp.asarray(ovl, BF16)

    per_head = lambda: pl.BlockSpec((1, seq, LANES), lambda b, h, i: (h, b, 0))
    return pl.pallas_call(
        functools.partial(_attn_prompt_body, kt, n_slc),
        grid=(batch, N_KV_HEADS, nqb),
        in_specs=[pl.BlockSpec((1, GROUP, qb, LANES), lambda b, h, i: (b, h, i, 0)),
                  pl.BlockSpec((1, qb, LANES), lambda b, h, i: (h, b * nqb + i, 0)),
                  pl.BlockSpec((1, 1, n_cmp_pad, LANES), lambda b, h, i: (b, h, 0, 0)),
                  pl.BlockSpec((1, 1, n_cmp_pad, LANES), lambda b, h, i: (b, h, 0, 0)),
                  per_head(), per_head(), per_head(), per_head(),
                  pl.BlockSpec((GROUP, qb, 2 * n_cmp_pad), lambda b, h, i: (h, 0, 0)),
                  pl.BlockSpec((1, GROUP, qb, 2 * qb), lambda b, h, i: (jnp.minimum(i, 1), h, 0, 0)),
                  pl.BlockSpec((1, GROUP, qb, band), lambda b, h, i: (jnp.minimum(i, nv - 1), h, 0, 0)),
                  pl.BlockSpec(ovl.shape, lambda b, h, i: (0, 0))],
        out_specs=pl.BlockSpec((qb, GROUP * HEAD_DIM), lambda b, h, i: (b * nqb + i, h)),
        out_shape=jax.ShapeDtypeStruct((batch * seq, ATTN_DIM), BF16),
        scratch_shapes=[pltpu.VMEM((GROUP * qb, kt), F32)] * 2,
        compiler_params=_cparams("arbitrary", "arbitrary", "arbitrary"),
        name="attn_prompt",
    )(qp, gates, kcmp, vcmp, ksb, vsb, kwb, vwb, ctab, ntab, wtab, ovl)


def _attn_sample_body(n_seq, n_pages, ts, past_len, pt_ref, cache_k_ref, cache_v_ref, *refs):
    kbuf, vbuf, ksem, vsem = refs[-4:]
    n_ops = n_seq * n_pages
    slot = _paged_prefetch(pt_ref, [cache_k_ref, cache_v_ref], [kbuf, vbuf], [ksem, vsem], n_ops)
    for b in range(n_seq):
        _attn_sample_one(b, [kbuf.at[slot, b * n_pages + j] for j in range(n_pages)],
                         [vbuf.at[slot, b * n_pages + j] for j in range(n_pages)], refs[:-4], ts, past_len)


def _attn_sample_one(b, kpages, vpages, refs, ts, past_len):
    (q_ref, gt_ref, kcmp_ref, vcmp_ref, ksn_ref, vsn_ref, kwc_ref, vwc_ref, kwn_ref, vwn_ref,
     ctab_ref, stab_ref, sntab_ref, wtab_ref, wntab_ref, eexp_ref, ovl_ref, o_ref) = refs
    rows = GROUP * N_KV_HEADS * ts
    rq = N_KV_HEADS * ts
    q = q_ref[b]
    gates = gt_ref[b]

    s = _dot_nt(q, kcmp_ref[b].astype(BF16)) + ctab_ref[...]
    p, l = _softmax_parts(s)
    pn = p / jnp.maximum(l, 1e-30)
    o_c = _dot(pn.astype(BF16), vcmp_ref[b].astype(BF16))

    psum = pn[0:rq]
    for g in range(1, GROUP):
        psum = psum + pn[g * rq:(g + 1) * rq]
    hi, mid, lo = _split3(psum)
    ovl = ovl_ref[...]
    imp = _dot(hi, ovl) + _dot(mid, ovl) + _dot(lo, ovl)
    n_slc = -(-(past_len + ts) // SLC_BLOCK)
    blk = lax.broadcasted_iota(jnp.int32, (rq, LANES), 1)
    qpos = past_len + (lax.broadcasted_iota(jnp.int32, (rq, LANES), 0) & (ts - 1))
    qblk = qpos >> 6
    forced = (blk == 0) | (blk == qblk) | (blk == qblk - 1)
    valid = ((blk << 6) <= qpos) & (blk < n_slc)
    imp = jnp.where(valid, imp + jnp.where(forced, FORCE_SCORE, 0.0), NEG_INF)
    cnt = jnp.zeros((rq, LANES), jnp.int32)
    for sidx in range(n_slc):
        r = imp[:, sidx:sidx + 1]
        beats = (r > imp) | ((r == imp) & (blk > sidx))
        cnt = cnt + jnp.where(beats, 1, 0)
    sel = jnp.where((cnt < N_SEL) & valid, 1.0, 0.0)
    sel = jnp.concatenate([sel] * GROUP, axis=0).astype(BF16)

    kc_t = jnp.concatenate([p_[...] for p_ in kpages], axis=1).astype(BF16)
    vc_t = jnp.concatenate([p_[...] for p_ in vpages], axis=1).astype(BF16)
    mexp = _dot(sel, eexp_ref[...])
    s1 = jnp.where(mexp > 0.5, _dot(q, kc_t) + stab_ref[...], NEG_INF)
    last = sel[:, n_slc - 1:n_slc].astype(F32)
    s2 = jnp.where(last > 0.5, _dot_nt(q, ksn_ref[b].astype(BF16)) + sntab_ref[...], NEG_INF)
    m = jnp.maximum(jnp.max(s1, axis=-1, keepdims=True), jnp.max(s2, axis=-1, keepdims=True))
    m = jnp.where(m == NEG_INF, 0.0, m)
    p1 = jnp.exp(s1 - m)
    p2 = jnp.exp(s2 - m)
    l = jnp.sum(p1, axis=-1, keepdims=True) + jnp.sum(p2, axis=-1, keepdims=True)
    o_s = (_dot_nt(p1.astype(BF16), vc_t) + _dot(p2.astype(BF16), vsn_ref[b].astype(BF16))) / jnp.maximum(l, 1e-30)

    s1 = _dot(q, kwc_ref[b].astype(BF16)) + wtab_ref[...]
    s2 = _dot_nt(q, kwn_ref[b].astype(BF16)) + wntab_ref[...]
    m = jnp.maximum(jnp.max(s1, axis=-1, keepdims=True), jnp.max(s2, axis=-1, keepdims=True))
    m = jnp.where(m == NEG_INF, 0.0, m)
    p1 = jnp.exp(s1 - m)
    p2 = jnp.exp(s2 - m)
    l = jnp.sum(p1, axis=-1, keepdims=True) + jnp.sum(p2, axis=-1, keepdims=True)
    o_w = (_dot_nt(p1.astype(BF16), vwc_ref[b].astype(BF16))
           + _dot(p2.astype(BF16), vwn_ref[b].astype(BF16))) / jnp.maximum(l, 1e-30)

    o_ref[b] = gates[:, 0:1] * o_c + gates[:, 1:2] * o_s + gates[:, 2:3] * o_w


def _attn_sample(q_s, gates_s, kcmp, vcmp, ks_new, vs_new, kw_new, vw_new, cache_ks, cache_vs,
                 cache_kw, cache_vw, pt_flat, fext, n_batch, ts, n_pages, page):
    past_len = n_pages * page
    w_buf = cache_kw.shape[2]
    rows = GROUP * N_KV_HEADS * ts
    n_new = 8
    n_cmp_pad = kcmp.shape[1]
    n_cmp = n_cmp_pad - 1
    n_slc = -(-(past_len + ts) // SLC_BLOCK)

    q5 = q_s.reshape(n_batch, ts, N_KV_HEADS, GROUP, HEAD_DIM).transpose(0, 3, 2, 1, 4)
    eye = jnp.eye(N_KV_HEADS, dtype=q_s.dtype)
    qr = jnp.einsum("bghtd,hk->bghtkd", q5, eye).reshape(n_batch, rows, LANES).astype(BF16)
    g5 = gates_s[:, :N_HEADS * N_BRANCH].reshape(n_batch, ts, N_KV_HEADS, GROUP, N_BRANCH).transpose(0, 3, 2, 1, 4)
    gr = jnp.pad(g5.reshape(n_batch, rows, N_BRANCH), ((0, 0), (0, 0), (0, LANES - N_BRANCH)))
    pad_new = lambda a: jnp.pad(a.reshape(n_batch, ts, KV_DIM), ((0, 0), (0, n_new - ts), (0, 0)))
    ks_new, vs_new, kw_new, vw_new = map(pad_new, (ks_new, vs_new, kw_new, vw_new))

    g_i, h_i, t_i = np.meshgrid(np.arange(GROUP), np.arange(N_KV_HEADS), np.arange(ts), indexing="ij")
    head = (h_i * GROUP + g_i).reshape(rows)
    tq = t_i.reshape(rows)
    pos_q = past_len + tq

    f_rows = fext[jnp.asarray(head)]

    def table(d, valid):
        return jnp.take_along_axis(f_rows, jnp.asarray(_bias_index(d, valid)), axis=1)

    nn = np.arange(n_cmp_pad)
    d = pos_q[:, None] - (nn[None, :] * CMP_STRIDE + CMP_BLOCK - 1)
    ctab = table(d, (d >= 0) & (nn < n_cmp)[None, :])
    near = np.arange(past_len - MAX_DISTANCE, past_len)
    d = pos_q[:, None] - near[None, :]
    assert past_len >= MAX_DISTANCE and d.min() >= 0
    stab = jnp.concatenate([jnp.zeros((rows, past_len - MAX_DISTANCE), F32), table(d, d >= 0)], axis=1)
    jn = np.arange(n_new)
    d = tq[:, None] - jn[None, :]
    sntab = table(d, (d >= 0) & (jn < ts)[None, :])
    pos_w = past_len - w_buf + np.arange(w_buf)
    d = pos_q[:, None] - pos_w[None, :]
    wtab = table(d, (d >= 0) & (d < WINDOW) & (pos_w >= 0)[None, :])
    wntab = table(tq[:, None] - jn[None, :], (tq[:, None] >= jn[None, :]) & (jn < ts)[None, :])

    eexp = jnp.asarray(np.arange(past_len)[None, :] // SLC_BLOCK == np.arange(LANES)[:, None], BF16)
    c_start = nn * CMP_STRIDE
    s_start = np.arange(LANES) * SLC_BLOCK
    ovl = jnp.asarray((c_start[:, None] < s_start[None, :] + SLC_BLOCK) & (c_start[:, None] + CMP_BLOCK > s_start[None, :])
                      & (nn < n_cmp)[:, None] & (np.arange(LANES) < n_slc)[None, :], BF16)

    n_seq = SAMPLE_SEQS if n_batch % SAMPLE_SEQS == 0 else 1
    n_ops = n_seq * n_pages
    const = lambda a: pl.BlockSpec(a.shape, lambda b, pt: (0,) * a.ndim)
    per_b = lambda a: pl.BlockSpec((n_seq,) + a.shape[1:], lambda b, pt: (b,) + (0,) * (a.ndim - 1))
    any_spec = pl.BlockSpec(memory_space=pl.ANY)
    small = [qr, gr, kcmp, vcmp, ks_new, vs_new, cache_kw, cache_vw, kw_new, vw_new]
    consts = [ctab, stab, sntab, wtab, wntab, eexp, ovl]
    page_buf = pltpu.VMEM((2, n_ops, KV_DIM, page), F32)
    grid_spec = pltpu.PrefetchScalarGridSpec(
        num_scalar_prefetch=1, grid=(n_batch // n_seq,),
        in_specs=[any_spec, any_spec] + [per_b(a) for a in small] + [const(a) for a in consts],
        out_specs=pl.BlockSpec((n_seq, rows, LANES), lambda b, pt: (b, 0, 0)),
        scratch_shapes=[page_buf, page_buf, pltpu.SemaphoreType.DMA((2,)), pltpu.SemaphoreType.DMA((2,))])
    o = pl.pallas_call(
        functools.partial(_attn_sample_body, n_seq, n_pages, ts, past_len),
        grid_spec=grid_spec,
        out_shape=jax.ShapeDtypeStruct((n_batch, rows, LANES), F32),
        compiler_params=_cparams("arbitrary"),
        name="attn_sample",
    )(pt_flat, cache_ks, cache_vs, *small, *consts)
    o6 = o.reshape(n_batch, GROUP, N_KV_HEADS, ts, N_KV_HEADS, HEAD_DIM)
    o5 = jnp.stack([o6[:, :, h, :, h] for h in range(N_KV_HEADS)], axis=2)
    return o5.transpose(0, 3, 2, 1, 4).reshape(n_batch * ts, ATTN_DIM).astype(BF16)


def _post1_body(tm, x_ref, co_ref, at_ref, wo_ref, nf_ref, wr_ref, br_ref, tri_ref, run0_ref,
                h_ref, hn_ref, rt_ref, cnt_ref, run_ref):
    @pl.when(pl.program_id(0) == 0)
    def _():
        run_ref[...] = run0_ref[...]

    h = x_ref[...] + _dot(co_ref[...], wo_ref[0:CONV_DIM]) + _dot(at_ref[...], wo_ref[CONV_DIM:CONV_DIM + ATTN_DIM])
    hn = _rms(h, nf_ref[...])
    h_ref[...] = h
    _store_token_tiles(hn_ref, hn, tm)

    hi = hn.astype(BF16)
    lo = (hn - hi.astype(F32)).astype(BF16)
    wr = wr_ref[...]
    whi = wr.astype(BF16)
    wlo = (wr - whi.astype(F32)).astype(BF16)
    logits = _dot(hi, whi) + _dot(lo, whi) + _dot(hi, wlo) + br_ref[...]

    lane_i = lax.broadcasted_iota(jnp.int32, (tm, LANES), 1)
    lane = lane_i.astype(F32)
    big = float(LANES)
    gmask = (lane_i >= ROUTER_GROUP_LANE) & (lane_i < ROUTER_GROUP_LANE + N_GROUPS)
    lg = jnp.where(gmask, logits, NEG_INF)
    eg = jnp.exp(lg - jnp.max(lg, axis=-1, keepdims=True))
    pg = eg / jnp.sum(eg, axis=-1, keepdims=True)
    gw = jnp.max(pg, axis=-1, keepdims=True)
    grp = jnp.min(jnp.where(gmask & (pg == gw), lane, big), axis=-1, keepdims=True) - ROUTER_GROUP_LANE

    group_of_lane = (lane_i >> 3).astype(F32)
    emask = (lane_i < N_EXPERTS) & (group_of_lane == grp)
    le = jnp.where(emask, logits, NEG_INF)
    ee = jnp.exp(le - jnp.max(le, axis=-1, keepdims=True))
    pe = jnp.where(emask, ee / jnp.sum(ee, axis=-1, keepdims=True), -1.0)
    v1 = jnp.max(pe, axis=-1, keepdims=True)
    i1 = jnp.min(jnp.where(pe == v1, lane, big), axis=-1, keepdims=True)
    pe2 = jnp.where(lane == i1, -1.0, pe)
    v2 = jnp.max(pe2, axis=-1, keepdims=True)
    i2 = jnp.min(jnp.where(pe2 == v2, lane, big), axis=-1, keepdims=True)
    tot = v1 + v2
    w1 = v1 / tot * gw
    w2 = v2 / tot * gw

    oh1 = jnp.where(lane == i1, 1.0, 0.0)
    oh2 = jnp.where(lane == i2, 1.0, 0.0)
    both = oh1 + oh2
    before = _dot(tri_ref[...], both.astype(BF16)) + run_ref[0:1]
    r1 = jnp.sum(oh1 * before, axis=-1, keepdims=True)
    r2 = jnp.sum(oh2 * before, axis=-1, keepdims=True)
    run = run_ref[0:1] + jnp.sum(both, axis=0, keepdims=True)
    run_ref[...] = jnp.broadcast_to(run, run_ref.shape)
    cnt_ref[...] = jnp.broadcast_to(run, cnt_ref.shape)

    rt = jnp.where(lane_i == 0, i1, 0.0)
    rt = jnp.where(lane_i == 1, i2, rt)
    rt = jnp.where(lane_i == 2, r1, rt)
    rt = jnp.where(lane_i == 3, r2, rt)
    rt = jnp.where(lane_i == 4, w1, rt)
    rt = jnp.where(lane_i == 5, w2, rt)
    rt_ref[...] = rt


def _post1(x2d, co, at, wo, nf, wr, br, run0, tm):
    n, d = x2d.shape
    tri = jnp.asarray(np.tril(np.ones((tm, tm), np.float32), -1), BF16)
    rows = lambda w: pl.BlockSpec((tm, w), lambda i: (i, 0))
    const = lambda a: pl.BlockSpec(a.shape, lambda i: (0,) * a.ndim)
    return pl.pallas_call(
        functools.partial(_post1_body, tm),
        grid=(n // tm,),
        in_specs=[rows(d), rows(CONV_DIM), rows(ATTN_DIM), const(wo), const(nf), const(wr), const(br),
                  const(tri), const(run0)],
        out_specs=[rows(d), pl.BlockSpec((tm * TOKEN_TILE_ROWS, LANES), lambda i: (i, 0)), rows(LANES),
                   pl.BlockSpec((8, LANES), lambda i: (0, 0))],
        out_shape=[jax.ShapeDtypeStruct((n, d), F32), jax.ShapeDtypeStruct((n * TOKEN_TILE_ROWS, LANES), F32),
                   jax.ShapeDtypeStruct((n, LANES), F32), jax.ShapeDtypeStruct((8, LANES), F32)],
        scratch_shapes=[pltpu.VMEM((8, LANES), F32)],
        compiler_params=_cparams("arbitrary"),
        name="post1",
    )(x2d, co, at, wo, nf, wr, br, tri, run0)


def _token_copy(src_ref, dst_ref, s, d, sem):
    r = TOKEN_TILE_ROWS
    return pltpu.make_async_copy(src_ref.at[pl.ds(pl.multiple_of(s * r, r), r)],
                                 dst_ref.at[pl.ds(pl.multiple_of(d * r, r), r)], sem)


def _scatter_rows_body(ts, dest_ref, src_ref, init_ref, out_ref, sem):
    del init_ref
    base = pl.program_id(0) * (2 * ts)

    def issue(t, _):
        for k in range(2):
            _token_copy(src_ref, out_ref, t, dest_ref[base + 2 * t + k], sem).start(priority=k)
        return 0

    lax.fori_loop(0, ts, issue, 0, unroll=DMA_ISSUE_UNROLL)
    for _ in range(2):
        pltpu.make_async_copy(src_ref, out_ref.at[pl.ds(0, ts * TOKEN_TILE_ROWS)], sem).wait()


def _scatter_rows(dest, src, slots):
    n_tok = dest.shape[0] // 2
    ts = min(SCATTER_TOKENS, n_tok)
    assert n_tok % ts == 0
    any_spec = pl.BlockSpec(memory_space=pl.ANY)
    return pl.pallas_call(
        functools.partial(_scatter_rows_body, ts),
        grid_spec=pltpu.PrefetchScalarGridSpec(
            num_scalar_prefetch=1, grid=(n_tok // ts,),
            in_specs=[pl.BlockSpec((ts * TOKEN_TILE_ROWS, LANES), lambda i, dest: (i, 0)), any_spec],
            out_specs=any_spec, scratch_shapes=[pltpu.SemaphoreType.DMA(())]),
        out_shape=jax.ShapeDtypeStruct(slots.shape, slots.dtype),
        input_output_aliases={2: 0},
        compiler_params=pltpu.CompilerParams(dimension_semantics=("arbitrary",)),
        name="scatter_rows",
    )(dest, src, slots)


def _experts_body(be_ref, nu_ref, x_ref, wg_ref, wu_ref, wd_ref, o_ref, wg_s, wu_s, wd_s):
    i = pl.program_id(0)

    @pl.when(i < nu_ref[0])
    def _():
        prev = be_ref[jnp.maximum(i - 1, 0)]

        @pl.when((i == 0) | (be_ref[i] != prev))
        def _():
            wg_s[...] = wg_ref[0].astype(BF16)
            wu_s[...] = wu_ref[0].astype(BF16)
            wd_s[...] = wd_ref[0].astype(BF16)

        x = _load_token_tiles(x_ref, EXPERT_ROWS, TOKEN_TILE_ROWS).astype(BF16)
        g = _dot(x, wg_s[...])
        u = _dot(x, wu_s[...])
        a = g * _sigmoid(g) * u
        _store_token_tiles(o_ref, _dot(a.astype(BF16), wd_s[...]), EXPERT_ROWS)

    @pl.when(i >= nu_ref[0])
    def _():
        o_ref[...] = jnp.zeros_like(o_ref)


def _experts(blk_expert, n_used, xs, wg, wu, wd):
    blk_rows = EXPERT_ROWS * TOKEN_TILE_ROWS
    n_blk = xs.shape[0] // blk_rows
    _, d, de = wg.shape
    xmap = lambda i, be, nu: (jnp.minimum(i, jnp.maximum(nu[0] - 1, 0)), 0)
    wmap = lambda i, be, nu: (be[jnp.minimum(i, jnp.maximum(nu[0] - 1, 0))], 0, 0)
    grid_spec = pltpu.PrefetchScalarGridSpec(
        num_scalar_prefetch=2, grid=(n_blk,),
        in_specs=[pl.BlockSpec((blk_rows, LANES), xmap), pl.BlockSpec((1, d, de), wmap),
                  pl.BlockSpec((1, d, de), wmap), pl.BlockSpec((1, de, d), wmap)],
        out_specs=pl.BlockSpec((blk_rows, LANES), lambda i, be, nu: (i, 0)),
        scratch_shapes=[pltpu.VMEM((d, de), BF16), pltpu.VMEM((d, de), BF16), pltpu.VMEM((de, d), BF16)])
    return pl.pallas_call(
        _experts_body, grid_spec=grid_spec,
        out_shape=jax.ShapeDtypeStruct(xs.shape, F32),
        compiler_params=_cparams("arbitrary"),
        name="experts",
    )(blk_expert, n_used, xs, wg, wu, wd)


def _post2_body(tm, dest_ref, h_ref, rt_ref, p_ref, yb_ref, wple_ref, wpg_ref, bpg_ref, np_ref, o_ref, buf, sem):
    i = pl.program_id(0)
    n = pl.num_programs(0)

    def fetch(step, slot):
        base = step * (2 * tm)

        def issue(t, _):
            for k in range(2):
                _token_copy(yb_ref, buf.at[slot], dest_ref[base + 2 * t + k], k * tm + t, sem.at[slot]).start(priority=k)
            return 0

        lax.fori_loop(0, tm, issue, 0, unroll=DMA_ISSUE_UNROLL)

    @pl.when(i == 0)
    def _():
        fetch(0, 0)

    @pl.when(i + 1 < n)
    def _():
        fetch(i + 1, (i + 1) & 1)

    slot = i & 1

    pltpu.make_async_copy(yb_ref.at[pl.ds(0, 2 * tm * TOKEN_TILE_ROWS)], buf.at[slot], sem.at[slot]).wait()
    rt = rt_ref[...]
    y0 = _load_token_tiles(buf.at[slot], tm, TOKEN_TILE_ROWS)
    y1 = _load_token_tiles(buf.at[slot], tm, TOKEN_TILE_ROWS, first=tm * TOKEN_TILE_ROWS)
    h = h_ref[...] + (y0 * rt[:, 4:5] + y1 * rt[:, 5:6])
    gate = _sigmoid(_dot(_rms(h, np_ref[...]).astype(BF16), wpg_ref[...]) + bpg_ref[...])
    o_ref[...] = h + gate * _dot(p_ref[...].astype(BF16), wple_ref[...])


def _post2(dest, h, rt, p2d, yb, wple, wpg, bpg, npl, tm):
    n, d = h.shape
    rows = lambda w: pl.BlockSpec((tm, w), lambda i, dest: (i, 0))
    const = lambda a: pl.BlockSpec(a.shape, lambda i, dest: (0,) * a.ndim)
    grid_spec = pltpu.PrefetchScalarGridSpec(
        num_scalar_prefetch=1, grid=(n // tm,),
        in_specs=[rows(d), rows(LANES), rows(p2d.shape[1]), pl.BlockSpec(memory_space=pl.ANY), const(wple),
                  const(wpg), const(bpg), const(npl)],
        out_specs=rows(d),
        scratch_shapes=[pltpu.VMEM((2, 2 * tm * TOKEN_TILE_ROWS, LANES), F32), pltpu.SemaphoreType.DMA((2,))])
    return pl.pallas_call(
        functools.partial(_post2_body, tm),
        grid_spec=grid_spec,
        out_shape=jax.ShapeDtypeStruct((n, d), F32),
        compiler_params=_cparams("arbitrary"),
        name="post2",
    )(dest, h, rt, p2d, yb, wple, wpg, bpg, npl)


def _row_tile(n, cap=512):
    t = min(cap, n)
    assert n % t == 0 and t % 8 == 0
    return t


def kernel(x_prompt, x_sample, p_prompt, p_sample, cache_k_cmp, cache_v_cmp, cache_k_slc, cache_v_slc, cache_k_win, cache_v_win, state_conv, page_table, w_in, w_out, conv_w, norm_mix, norm_ffn, norm_ple, q_norm, k_norm, cmp_pe_k, cmp_w1_k, cmp_w2_k, cmp_pe_v, cmp_w1_v, cmp_w2_v, rel_bias, w_router_group, b_router_group, w_router_expert, b_router_expert, w_exp_gate, w_exp_up, w_exp_down, w_ple, w_ple_gate, b_ple_gate):
    assert w_in.shape[0] == 1, "single-layer step"
    bp, t, d = x_prompt.shape
    bs, ts, _ = x_sample.shape
    n_pages = page_table.shape[1]
    page = cache_k_cmp.shape[2]
    past_len = n_pages * page
    w_buf = cache_k_win.shape[2]
    n_phys = cache_k_cmp.shape[1]
    assert t % Q_BLOCK == 0 and t >= WINDOW + Q_BLOCK and page % CMP_STRIDE == 0 and ts == 4 and d == D_MODEL
    assert past_len % SLC_BLOCK == 0
    np_rows, ns_rows = bp * t, bs * ts

    row = lambda v: v.reshape(1, -1).astype(F32)
    w_in_b = jnp.pad(w_in[0], ((0, 0), (0, Z_COLS - w_in.shape[2]))).astype(BF16)
    qn = row(jnp.tile(q_norm[0], N_HEADS))
    kn1 = row(jnp.tile(k_norm[0, 1], N_KV_HEADS))
    kn2 = row(jnp.tile(k_norm[0, 2], N_KV_HEADS))
    bd = jnp.asarray(np.kron(np.eye(N_HEADS), np.ones((HEAD_DIM, HEAD_DIM))), BF16)
    pw = (row(norm_mix[0]), w_in_b, qn, kn1, kn2, conv_w[0].astype(F32), bd)
    cw_k = _compress_weights(cmp_pe_k[0], cmp_w1_k[0], cmp_w2_k[0])
    cw_v = _compress_weights(cmp_pe_v[0], cmp_w1_v[0], cmp_w2_v[0])
    kn0 = row(k_norm[0, 0])
    fext = _bias_by_distance(rel_bias)
    pt_flat = page_table.reshape(-1).astype(jnp.int32)
    wr = jnp.zeros((d, LANES), F32).at[:, :N_EXPERTS].set(w_router_expert[0])
    wr = wr.at[:, ROUTER_GROUP_LANE:ROUTER_GROUP_LANE + N_GROUPS].set(w_router_group[0])
    br = jnp.zeros((1, LANES), F32).at[0, :N_EXPERTS].set(b_router_expert[0])
    br = br.at[0, ROUTER_GROUP_LANE:ROUTER_GROUP_LANE + N_GROUPS].set(b_router_group[0])
    wo_b = w_out[0].astype(BF16)
    wple_b = w_ple[0].astype(BF16)
    wpg_b = w_ple_gate[0].astype(BF16)

    tm_p = _row_tile(t, cap=1024)
    (co_p, q_p, kc_p, vc_p, ks_p, vs_p, kw_p, vw_p, gt_p, cs_p, ksb, vsb, kwb, vwb, kc_t, vc_t, ks_t, vs_t) = _project(
        x_prompt.reshape(np_rows, d), bp, t, tm_p, pw)
    chunk_w = CMP_STRIDE * KV_DIM
    kcmp_p = _compress_rows(kc_p.reshape(bp, t // CMP_STRIDE, chunk_w), cw_k, kn0, True)
    vcmp_p = _compress_rows(vc_p.reshape(bp, t // CMP_STRIDE, chunk_w), cw_v, kn0, False)
    at_p = _attn_prompt(q_p, gt_p, kcmp_p, vcmp_p, ksb, vsb, kwb, vwb, fext, bp, t)

    st = state_conv[0].astype(F32)
    s0 = jnp.repeat(st[:, 0], ts, axis=0)
    s1 = jnp.repeat(st[:, 1], ts, axis=0)
    (co_s, q_s, kc_s, vc_s, ks_s, vs_s, kw_s, vw_s, gt_s, u_s) = _project(
        x_sample.reshape(ns_rows, d), bs, ts, ns_rows, pw, state=(s0, s1))
    feature_major = lambda c, n, rows_: jnp.transpose(c[0], (0, 2, 3, 1)).reshape(n, KV_DIM, rows_)
    kcmp_s = _compress_pages(feature_major(cache_k_cmp, n_phys, page), pt_flat, bs, n_pages, cw_k, kn0, True)
    vcmp_s = _compress_pages(feature_major(cache_v_cmp, n_phys, page), pt_flat, bs, n_pages, cw_v, kn0, False)
    at_s = _attn_sample(q_s, gt_s, kcmp_s, vcmp_s, ks_s, vs_s, kw_s, vw_s,
                        feature_major(cache_k_slc, n_phys, page), feature_major(cache_v_slc, n_phys, page),
                        feature_major(cache_k_win, bs, w_buf), feature_major(cache_v_win, bs, w_buf),
                        pt_flat, fext, bs, ts, n_pages, page)

    tp1 = _row_tile(np_rows)
    ts1 = _row_tile(ns_rows)
    nf = row(norm_ffn[0])
    h_p, hn_p, rt_p, cnt_p = _post1(x_prompt.reshape(np_rows, d), co_p, at_p, wo_b, nf, wr, br,
                                    jnp.zeros((8, LANES), F32), tp1)
    h_s, hn_s, rt_s, cnt_s = _post1(x_sample.reshape(ns_rows, d), co_s, at_s, wo_b, nf, wr, br, cnt_p, ts1)

    counts = cnt_s[0, :N_EXPERTS].astype(jnp.int32)
    padded = (counts + EXPERT_ROWS - 1) // EXPERT_ROWS * EXPERT_ROWS
    pad_end = jnp.cumsum(padded)
    pad_start = pad_end - padded
    n_assign = 2 * (np_rows + ns_rows)
    n_blk = (n_assign + N_EXPERTS * (EXPERT_ROWS - 1) + EXPERT_ROWS - 1) // EXPERT_ROWS
    blk_first = jnp.arange(n_blk, dtype=jnp.int32) * EXPERT_ROWS
    blk_expert = jnp.minimum(jnp.sum((pad_end[None, :] <= blk_first[:, None]).astype(jnp.int32), axis=1),
                             N_EXPERTS - 1)
    n_used = (pad_end[-1:] // EXPERT_ROWS).astype(jnp.int32)

    def dest_of(rt):
        e = rt[:, 0:2].astype(jnp.int32).reshape(-1)
        return pad_start[e] + rt[:, 2:4].astype(jnp.int32).reshape(-1)

    dest_p = dest_of(rt_p)
    dest_s = dest_of(rt_s)

    xs = jnp.zeros((n_blk * EXPERT_ROWS * TOKEN_TILE_ROWS, LANES), F32)
    xs = _scatter_rows(dest_p, hn_p, xs)
    xs = _scatter_rows(dest_s, hn_s, xs)
    yb = _experts(blk_expert, n_used, xs, w_exp_gate[0], w_exp_up[0], w_exp_down[0])

    bpg = row(b_ple_gate[0])
    npl = row(norm_ple[0])
    y_p = _post2(dest_p, h_p, rt_p, p_prompt[0].reshape(np_rows, -1), yb, wple_b, wpg_b, bpg, npl, tp1)
    y_s = _post2(dest_s, h_s, rt_s, p_sample[0].reshape(ns_rows, -1), yb, wple_b, wpg_b, bpg, npl, ts1)

    kv5 = lambda a, b, s: a.reshape(1, b, s, N_KV_HEADS, HEAD_DIM)
    wp = min(WINDOW, t)
    win_p = lambda a: kv5(a, bp, t)[:, :, t - wp:]
    win_s = lambda c, new: jnp.concatenate([c[0], new.reshape(bs, ts, N_KV_HEADS, HEAD_DIM)], axis=1)[None, :, ts:]
    conv_p = cs_p[:, 8 - (CONV_K - 1):][None]
    conv_s = u_s.reshape(bs, ts, CONV_DIM)[:, ts - (CONV_K - 1):][None]
    from_t = lambda a: jnp.transpose(a.reshape(bp, N_KV_HEADS, HEAD_DIM, t), (0, 3, 1, 2))[None]
    return (y_p.reshape(bp, t, d), y_s.reshape(bs, ts, d),
            from_t(kc_t), from_t(vc_t), from_t(ks_t), from_t(vs_t), win_p(kw_p), win_p(vw_p), conv_p,
            kv5(kc_s, bs, ts), kv5(vc_s, bs, ts), kv5(ks_s, bs, ts), kv5(vs_s, bs, ts),
            win_s(cache_k_win, kw_s), win_s(cache_v_win, vw_s), conv_s)
```

```python
import functools
import math

import numpy as np
import jax
import jax.numpy as jnp
from jax import lax
from jax.experimental import pallas as pl
from jax.experimental.pallas import tpu as pltpu

F32 = jnp.float32
BF16 = jnp.bfloat16
NEG_INF = float("-inf")
MASKED = -1e30

HEAD_DIM = 64
N_HEADS = 8
N_KV_HEADS = 2
GROUP = N_HEADS // N_KV_HEADS
CONV_DIM = 512
ATTN_DIM = 512
KV_DIM = N_KV_HEADS * HEAD_DIM
N_BRANCH = 3
CONV_K = 3
CMP_BLOCK = 32
CMP_STRIDE = 16
CMP_HIDDEN = 256
SLC_BLOCK = 64
N_SEL = 16
WINDOW = 512
Q_BLOCK = 128
FORCE_SCORE = 1e4
NUM_BUCKETS = 32
MAX_DISTANCE = 128
N_GROUPS = 4
EXPERTS_PER_GROUP = 8
N_EXPERTS = N_GROUPS * EXPERTS_PER_GROUP
D_EXPERT = 512
EPS = 1e-6

D_MODEL = 1024
LANES = 128
TOKEN_TILE_ROWS = D_MODEL // LANES
Z_COLS = 3 * CONV_DIM + ATTN_DIM + 6 * KV_DIM + LANES
BIAS_DMAX = 768
EXPERT_ROWS = 512
ROUTER_GROUP_LANE = 32
SCATTER_TOKENS = 256
DMA_ISSUE_UNROLL = 8
COMPRESS_SEQS = 8
SAMPLE_SEQS = 4
VMEM_LIMIT = 56 * 1024 * 1024


def _cparams(*sem):
    return pltpu.CompilerParams(dimension_semantics=sem, vmem_limit_bytes=VMEM_LIMIT)


def _dot(a, b):
    return jnp.dot(a, b, preferred_element_type=F32)


def _dot_nt(a, b):
    return lax.dot_general(a, b, (((1,), (1,)), ((), ())), preferred_element_type=F32)


def _split3(x):
    hi = x.astype(BF16)
    r = x - hi.astype(F32)
    mid = r.astype(BF16)
    lo = (r - mid.astype(F32)).astype(BF16)
    return hi, mid, lo


def _rms(x, g):
    return x * lax.rsqrt(jnp.mean(x * x, axis=-1, keepdims=True) + EPS) * g


def _head_rms(x, bd, g):
    hi, mid, _ = _split3(x * x)
    ss = _dot(hi, bd) + _dot(mid, bd)
    return x * lax.rsqrt(ss * (1.0 / HEAD_DIM) + EPS) * g


def _sigmoid(x):
    return 1.0 / (1.0 + jnp.exp(-x))


def _store_token_tiles(ref, x, n):
    r = x.shape[1] // LANES
    for j in range(r):
        ref[pl.ds(j, n, stride=r), :] = x[:, j * LANES:(j + 1) * LANES]


def _load_token_tiles(ref, n, r, first=0):
    return jnp.concatenate([ref[pl.ds(first + j, n, stride=r), :] for j in range(r)], axis=-1)


def _softmax_parts(s):
    m = jnp.max(s, axis=-1, keepdims=True)
    m = jnp.where(m == NEG_INF, 0.0, m)
    p = jnp.exp(s - m)
    l = jnp.sum(p, axis=-1, keepdims=True)
    return p, l


def _proj_body(sample, tm, *refs):
    if sample:
        (x_ref, nm_ref, w_ref, qn_ref, kn1_ref, kn2_ref, cw_ref, bd_ref, s0_ref, s1_ref,
         co_ref, q_ref, kc_ref, vc_ref, ks_ref, vs_ref, kw_ref, vw_ref, gt_ref, u_ref) = refs
    else:
        (x_ref, nm_ref, w_ref, qn_ref, kn1_ref, kn2_ref, cw_ref, bd_ref,
         co_ref, q_ref, kc_ref, vc_ref, ks_ref, vs_ref, kw_ref, vw_ref, gt_ref, cs_ref,
         ksb_ref, vsb_ref, kwb_ref, vwb_ref, kct_ref, vct_ref, kst_ref, vst_ref, kcx_ref, vcx_ref,
         carry_ref, stage_ref) = refs

    xn = _rms(x_ref[...], nm_ref[...]).astype(BF16)

    def seg(a, b):
        return _dot(xn, w_ref[:, a:b])

    c3 = 3 * CONV_DIM
    u = seg(2 * CONV_DIM, c3) * seg(0, CONV_DIM)
    bg = seg(CONV_DIM, 2 * CONV_DIM)
    row = lax.broadcasted_iota(jnp.int32, (tm, 1), 0)
    um1 = pltpu.roll(u, 1, axis=0)
    um2 = pltpu.roll(u, 2, axis=0)
    if sample:
        r = row & 3
        s0 = s0_ref[...]
        s1 = s1_ref[...]
        prev1 = jnp.where(r == 0, s1, um1)
        prev2 = jnp.where(r == 0, s0, jnp.where(r == 1, s1, um2))
        u_ref[...] = u
    else:
        @pl.when(pl.program_id(1) == 0)
        def _():
            carry_ref[...] = jnp.zeros_like(carry_ref)
        c = carry_ref[...]
        prev1 = jnp.where(row == 0, c[7:8], um1)
        prev2 = jnp.where(row == 0, c[6:7], jnp.where(row == 1, c[7:8], um2))
        carry_ref[...] = u[tm - 8:tm]
        cs_ref[0] = u[tm - 8:tm]
    cw = cw_ref[...]
    y = cw[0:1] * prev2 + cw[1:2] * prev1 + cw[2:3] * u
    co_ref[...] = (bg * y).astype(BF16)

    bd = bd_ref[...]
    q = _head_rms(seg(c3, c3 + ATTN_DIM), bd, qn_ref[...]) * (HEAD_DIM ** -0.5)
    lane = lax.broadcasted_iota(jnp.int32, (tm, LANES), 1)
    low = lane < HEAD_DIM

    def head_planes(x, fill):
        return [jnp.where(low, x if h == 0 else pltpu.roll(x, HEAD_DIM, axis=1), fill) for h in range(N_KV_HEADS)]

    if sample:
        q_ref[...] = q
    else:
        for hd in range(N_HEADS):
            pair = q[:, (hd // 2) * LANES:(hd // 2 + 1) * LANES]
            if hd % 2:
                pair = pltpu.roll(pair, HEAD_DIM, axis=1)
            q_ref[0, hd] = jnp.where(low, pair, 0.0).astype(BF16)

    k0 = c3 + ATTN_DIM
    bdk = bd[:KV_DIM, :KV_DIM]
    kc = seg(k0, k0 + KV_DIM)
    vc = seg(k0 + KV_DIM, k0 + 2 * KV_DIM)
    ks = _head_rms(seg(k0 + 2 * KV_DIM, k0 + 3 * KV_DIM), bdk, kn1_ref[...])
    vs = seg(k0 + 3 * KV_DIM, k0 + 4 * KV_DIM)
    kw = _head_rms(seg(k0 + 4 * KV_DIM, k0 + 5 * KV_DIM), bdk, kn2_ref[...])
    vw = seg(k0 + 5 * KV_DIM, k0 + 6 * KV_DIM)
    kc_ref[...] = kc
    vc_ref[...] = vc
    ks_ref[...] = ks
    vs_ref[...] = vs
    kw_ref[...] = kw
    vw_ref[...] = vw
    if not sample:
        kct_ref[0] = kc.T
        vct_ref[0] = vc.T
        kst_ref[0] = ks.T
        vst_ref[0] = vs.T
        for src, dst in ((kc, kcx_ref), (vc, vcx_ref)):
            stage_ref[...] = src
            dst[...] = jnp.concatenate([stage_ref[pl.ds(r, tm // CMP_STRIDE, stride=CMP_STRIDE), :]
                                        for r in range(CMP_STRIDE)], axis=-1)
    gates = _sigmoid(seg(k0 + 6 * KV_DIM, k0 + 6 * KV_DIM + LANES))
    if sample:
        gt_ref[...] = gates
    else:
        pos = pl.program_id(1) * tm + lax.broadcasted_iota(jnp.int32, (tm, LANES), 0)
        block_onehot = jnp.where(lane - HEAD_DIM == (pos >> 6), 1.0, 0.0)
        for h, (a, b, c, e) in enumerate(zip(head_planes(ks, block_onehot), head_planes(vs, 1.0),
                                             head_planes(kw, 0.0), head_planes(vw, 1.0))):
            ksb_ref[h] = a.astype(BF16)
            vsb_ref[h] = b.astype(BF16)
            kwb_ref[h] = c.astype(BF16)
            vwb_ref[h] = e.astype(BF16)
        gt_ref[0] = gates
        gt_ref[1] = pltpu.roll(gates, LANES - GROUP * N_BRANCH, axis=1)


def _project(x2d, batch, seq, tm, weights, state=None):
    n, d = x2d.shape
    sample = state is not None
    nt = seq // tm if not sample else 1
    const = lambda shape: pl.BlockSpec(shape, lambda b, t: (0,) * len(shape))
    rows = lambda w: pl.BlockSpec((tm, w), lambda b, t: (b * nt + t, 0))
    nm, w_in, qn, kn1, kn2, cw, bd = weights
    in_specs = [rows(d), const(nm.shape), const(w_in.shape), const(qn.shape), const(kn1.shape),
                const(kn2.shape), const(cw.shape), const(bd.shape)]
    args = [x2d, nm, w_in, qn, kn1, kn2, cw, bd]
    kv_f32 = [jax.ShapeDtypeStruct((n, KV_DIM), F32)] * 6
    if sample:
        in_specs += [rows(CONV_DIM), rows(CONV_DIM)]
        args += list(state)
        out_shape = ([jax.ShapeDtypeStruct((n, CONV_DIM), BF16), jax.ShapeDtypeStruct((n, ATTN_DIM), F32)]
                     + kv_f32 + [jax.ShapeDtypeStruct((n, LANES), F32), jax.ShapeDtypeStruct((n, CONV_DIM), F32)])
        out_specs = [rows(CONV_DIM), rows(ATTN_DIM)] + [rows(KV_DIM)] * 6 + [rows(LANES), rows(CONV_DIM)]
        scratch = []
        grid = (1, 1)
    else:
        planes = pl.BlockSpec((N_KV_HEADS, tm, LANES), lambda b, t: (0, b * nt + t, 0))
        out_shape = ([jax.ShapeDtypeStruct((n, CONV_DIM), BF16),
                      jax.ShapeDtypeStruct((batch, N_HEADS, seq, LANES), BF16)]
                     + kv_f32 + [jax.ShapeDtypeStruct((N_KV_HEADS, n, LANES), F32),
                                 jax.ShapeDtypeStruct((batch, 8, CONV_DIM), F32)]
                     + [jax.ShapeDtypeStruct((N_KV_HEADS, n, LANES), BF16)] * 4
                     + [jax.ShapeDtypeStruct((batch, KV_DIM, seq), F32)] * 4
                     + [jax.ShapeDtypeStruct((n // CMP_STRIDE, CMP_STRIDE * KV_DIM), F32)] * 2)
        out_specs = ([rows(CONV_DIM), pl.BlockSpec((1, N_HEADS, tm, LANES), lambda b, t: (b, 0, t, 0))]
                     + [rows(KV_DIM)] * 6 + [planes, pl.BlockSpec((1, 8, CONV_DIM), lambda b, t: (b, 0, 0))]
                     + [planes] * 4 + [pl.BlockSpec((1, KV_DIM, tm), lambda b, t: (b, 0, t))] * 4
                     + [pl.BlockSpec((tm // CMP_STRIDE, CMP_STRIDE * KV_DIM), lambda b, t: (b * nt + t, 0))] * 2)
        scratch = [pltpu.VMEM((8, CONV_DIM), F32), pltpu.VMEM((tm, KV_DIM), F32)]
        grid = (batch, nt)
    return pl.pallas_call(
        functools.partial(_proj_body, sample, tm),
        grid=grid, in_specs=in_specs, out_specs=out_specs, out_shape=out_shape, scratch_shapes=scratch,
        compiler_params=_cparams("arbitrary", "arbitrary"),
        name="proj_sample" if sample else "proj_prompt",
    )(*args)


def _gelu_tanh(x):
    cdf = 0.5 * (1.0 + jnp.tanh(math.sqrt(2.0 / math.pi) * (x + 0.044715 * (x * x * x))))
    return x * cdf


def _compress_core(norm, x, pe_ref, we_ref, w2_ref, g_ref):
    n = x.shape[0]
    a0 = _dot((x + pe_ref[0:1]).astype(BF16), we_ref[0])
    a1 = _dot((x + pe_ref[1:2]).astype(BF16), we_ref[1])
    hid = a0 + pltpu.roll(a1, n - 1, axis=0)
    w2 = w2_ref[...]
    outs = []
    for h in range(N_KV_HEADS):
        act = _gelu_tanh(hid[:, h * CMP_HIDDEN:(h + 1) * CMP_HIDDEN])
        o = _dot(act.astype(BF16), w2)
        if norm:
            o = _rms(o, g_ref[...])
        outs.append(o)
    return outs


def _compress_rows_body(norm, x_ref, pe_ref, we_ref, w2_ref, g_ref, o_ref):
    outs = _compress_core(norm, x_ref[0], pe_ref, we_ref, w2_ref, g_ref)
    for h in range(N_KV_HEADS):
        o_ref[0, h] = jnp.concatenate([outs[h], jnp.zeros_like(outs[h])], axis=-1).astype(BF16)


def _fetch_pages(pt_ref, cache_ref, buf_ref, sem_ref, step, slot, n_ops, priority):
    base = step * n_ops

    def issue(j, _):
        pltpu.make_async_copy(cache_ref.at[pt_ref[base + j]], buf_ref.at[slot, j], sem_ref.at[slot]).start(
            priority=priority)
        return 0

    lax.fori_loop(0, n_ops, issue, 0)


def _paged_prefetch(pt_ref, caches, bufs, sems, n_ops):
    i = pl.program_id(0)
    slot = i & 1
    for k, (cache_ref, buf_ref, sem_ref) in enumerate(zip(caches, bufs, sems)):
        @pl.when(i == 0)
        def _():
            _fetch_pages(pt_ref, cache_ref, buf_ref, sem_ref, 0, 0, n_ops, k % 2)

        @pl.when(i + 1 < pl.num_programs(0))
        def _():
            _fetch_pages(pt_ref, cache_ref, buf_ref, sem_ref, i + 1, 1 - slot, n_ops, k % 2)

    for cache_ref, buf_ref, sem_ref in zip(caches, bufs, sems):
        pltpu.make_async_copy(cache_ref.at[pl.ds(0, n_ops)], buf_ref.at[slot], sem_ref.at[slot]).wait()
    return slot


def _compress_pages_body(norm, nch, n_seq, n_pages, pt_ref, cache_ref, pe_ref, we_ref, w2_ref, g_ref, o_ref,
                         rows_ref, buf_ref, sem_ref):
    n_ops = n_seq * n_pages
    slot = _paged_prefetch(pt_ref, [cache_ref], [buf_ref], [sem_ref], n_ops)
    page = buf_ref.shape[3]
    for j in range(n_ops):
        rows_ref[j * page:(j + 1) * page, :] = buf_ref[slot, j].T
    x = jnp.concatenate([rows_ref[pl.ds(r, n_seq * nch, stride=CMP_STRIDE), :] for r in range(CMP_STRIDE)], axis=-1)
    out = jnp.concatenate(_compress_core(norm, x, pe_ref, we_ref, w2_ref, g_ref), axis=-1)
    for b in range(n_seq):
        o_ref[b] = out[b * nch:(b + 1) * nch]


def _compress_weights(pe, w1, w2):
    w1r = w1.reshape(2, CMP_STRIDE, HEAD_DIM, CMP_HIDDEN).astype(BF16)
    z = jnp.zeros_like(w1r)
    we = jnp.stack([jnp.concatenate([w1r, z], axis=-1), jnp.concatenate([z, w1r], axis=-1)], axis=2)
    we = we.reshape(2, CMP_STRIDE * KV_DIM, N_KV_HEADS * CMP_HIDDEN)
    per = pe.reshape(2, CMP_STRIDE, 1, HEAD_DIM)
    pex = jnp.broadcast_to(per, (2, CMP_STRIDE, N_KV_HEADS, HEAD_DIM)).reshape(2, CMP_STRIDE * KV_DIM)
    return pex.astype(F32), we.astype(BF16), w2.astype(BF16)


def _compress_rows(rows3, cw, gain, norm):
    b, nch, width = rows3.shape
    pex, we, w2 = cw
    const = lambda a: pl.BlockSpec(a.shape, lambda i: (0,) * a.ndim)
    return pl.pallas_call(
        functools.partial(_compress_rows_body, norm),
        grid=(b,),
        in_specs=[pl.BlockSpec((1, nch, width), lambda i: (i, 0, 0)), const(pex), const(we), const(w2), const(gain)],
        out_specs=pl.BlockSpec((1, N_KV_HEADS, nch, LANES), lambda i: (i, 0, 0, 0)),
        out_shape=jax.ShapeDtypeStruct((b, N_KV_HEADS, nch, LANES), BF16),
        compiler_params=_cparams("arbitrary"),
        name="compress_rows",
    )(rows3, pex, we, w2, gain)


def _compress_pages(cache_t, pt_flat, n_batch, n_pages, cw, gain, norm):
    _, _, page = cache_t.shape
    nch = n_pages * page // CMP_STRIDE
    pex, we, w2 = cw
    n_seq = COMPRESS_SEQS if n_batch % COMPRESS_SEQS == 0 else 1
    const = lambda a: pl.BlockSpec(a.shape, lambda i, pt: (0,) * a.ndim)
    n_ops = n_seq * n_pages
    grid_spec = pltpu.PrefetchScalarGridSpec(
        num_scalar_prefetch=1, grid=(n_batch // n_seq,),
        in_specs=[pl.BlockSpec(memory_space=pl.ANY), const(pex), const(we), const(w2), const(gain)],
        out_specs=pl.BlockSpec((n_seq, nch, KV_DIM), lambda i, pt: (i, 0, 0)),
        scratch_shapes=[pltpu.VMEM((n_ops * page, KV_DIM), F32), pltpu.VMEM((2, n_ops, KV_DIM, page), F32),
                        pltpu.SemaphoreType.DMA((2,))])
    return pl.pallas_call(
        functools.partial(_compress_pages_body, norm, nch, n_seq, n_pages),
        grid_spec=grid_spec,
        out_shape=jax.ShapeDtypeStruct((n_batch, nch, KV_DIM), F32),
        compiler_params=_cparams("arbitrary"),
        name="compress_pages",
    )(pt_flat, cache_t, pex, we, w2, gain)


def _rel_bucket(dist):
    n = jnp.maximum(dist, 0)
    max_exact = NUM_BUCKETS // 2
    nf = jnp.maximum(n, 1).astype(F32)
    large = max_exact + (jnp.log(nf / max_exact) / math.log(MAX_DISTANCE / max_exact)
                         * (NUM_BUCKETS - max_exact)).astype(jnp.int32)
    large = jnp.minimum(large, NUM_BUCKETS - 1)
    return jnp.where(n < max_exact, n, large)


def _bias_by_distance(rel_bias):
    d = jnp.arange(BIAS_DMAX, dtype=jnp.int32)
    f = jnp.take(rel_bias.astype(F32), _rel_bucket(d), axis=0)
    f = (f - f[BIAS_DMAX - 1:BIAS_DMAX]).T
    return jnp.concatenate([f, jnp.full((N_HEADS, 1), NEG_INF, F32)], axis=1)


def _bias_index(d, valid):
    return np.where(valid, np.clip(d, 0, BIAS_DMAX - 1), BIAS_DMAX).astype(np.int32)


def _bias_table(fext, d, valid):
    return jnp.take(fext, jnp.asarray(_bias_index(d, valid)), axis=1)


def _toeplitz_body(n_rows, v_ref, o_ref):
    x = jnp.broadcast_to(v_ref[0], (n_rows, v_ref.shape[2]))
    o_ref[0] = pltpu.roll(x, 0, axis=1, stride=1, stride_axis=0)


def _toeplitz_rows(v, n_rows):
    h, w = v.shape
    return pl.pallas_call(
        functools.partial(_toeplitz_body, n_rows),
        grid=(h,),
        in_specs=[pl.BlockSpec((1, 1, w), lambda i: (i, 0, 0))],
        out_specs=pl.BlockSpec((1, n_rows, w), lambda i: (i, 0, 0)),
        out_shape=jax.ShapeDtypeStruct((h, n_rows, w), F32),
        compiler_params=_cparams("arbitrary"),
        name="toeplitz_rows",
    )(v.reshape(h, 1, w))


def _select_blocks(imp_t, srow, qpos, n_rank):
    qblk = qpos >> 6
    forced = (srow == 0) | (srow == qblk) | (srow == qblk - 1)
    valid = (srow << 6) <= qpos
    imp_t = jnp.where(valid, imp_t + jnp.where(forced, FORCE_SCORE, 0.0), NEG_INF)
    n_rows = imp_t.shape[0]
    assert n_rows % 8 == 0
    slabs = [imp_t[a:a + 8] for a in range(0, n_rows, 8)]
    rows8 = [srow[a:a + 8] for a in range(0, n_rows, 8)]
    cnts = [jnp.zeros(x.shape, jnp.int32) for x in slabs]
    for s in range(n_rank):
        r = imp_t[s:s + 1, :]
        for j, x in enumerate(slabs):
            if 8 * j > s:
                beats = r >= x
            elif 8 * j + 7 <= s:
                beats = r > x
            else:
                beats = (r > x) | ((r == x) & (rows8[j] > s))
            cnts[j] = cnts[j] + jnp.where(beats, 1, 0)
    cnt = jnp.concatenate(cnts, axis=0)
    return jnp.where((cnt < N_SEL) & valid, 1.0, 0.0)


def _exp_pv(s, m, v):
    return _dot(jnp.exp((s - m).astype(BF16)), v)


def _normalize_pv(pv):
    return pv / jnp.maximum(pv[:, HEAD_DIM:HEAD_DIM + 1], 1e-30)


def _attn_prompt_body(kt, n_slc, q_ref, gt_ref, kcmp_ref, vcmp_ref, ks_ref, vs_ref, kw_ref, vw_ref,
                      ctab_ref, ntab_ref, wtab_ref, ovl_ref, o_ref, sa_ref, sb_ref):
    i = pl.program_id(2)
    qb = Q_BLOCK
    rows = GROUP * qb
    n_cmp_pad = kcmp_ref.shape[2]
    first_near_block = (qb // SLC_BLOCK) * jnp.maximum(i - 1, 0)
    near_start = pl.multiple_of(jnp.maximum(i - 1, 0) * qb, qb)
    win_start = pl.multiple_of(jnp.maximum(i * qb - WINDOW, 0), qb)
    band = WINDOW + qb
    gates = gt_ref[0]
    lane = lax.broadcasted_iota(jnp.int32, (qb, LANES), 1)
    q0 = q_ref[0].reshape(rows, LANES)

    s = _dot_nt(q0, kw_ref[0, pl.ds(win_start, band), :]) + wtab_ref[0].reshape(rows, band)
    m = jnp.max(s, axis=-1, keepdims=True)
    m = jnp.where(m == NEG_INF, 0.0, m)
    o_w = _normalize_pv(_exp_pv(s, m, vw_ref[0, pl.ds(win_start, band), :]))

    per_qb = qb // CMP_STRIDE
    ctab = pltpu.roll(ctab_ref[...].reshape(rows, 2 * n_cmp_pad), i * per_qb, axis=1)[:, n_cmp_pad:]
    s = _dot_nt(q0, kcmp_ref[0, 0]) + ctab
    p, l = _softmax_parts(s)
    pn = p / jnp.maximum(l, 1e-30)
    o_c = _dot(pn.astype(BF16), vcmp_ref[0, 0])

    psum = pn[0:qb] + pn[qb:2 * qb] + pn[2 * qb:3 * qb] + pn[3 * qb:4 * qb]
    hi, mid, lo = _split3(psum)
    ovl = ovl_ref[...]
    imp_t = _dot_nt(ovl, hi) + _dot_nt(ovl, mid) + _dot_nt(ovl, lo)
    srow = lax.broadcasted_iota(jnp.int32, (n_slc, qb), 0)
    qpos_t = i * qb + lax.broadcasted_iota(jnp.int32, (n_slc, qb), 1)
    sel_t = _select_blocks(imp_t, srow, qpos_t, n_slc)
    sel_t = jnp.concatenate([sel_t, jnp.zeros((LANES - n_slc, qb), F32)], axis=0)
    sel = sel_t.T

    def query_with_mask(keep):
        m = pltpu.roll(jnp.where(keep, 0.0, MASKED), HEAD_DIM, axis=1).astype(BF16)
        m = jnp.concatenate([jnp.where(lane < HEAD_DIM, q0[g * qb:(g + 1) * qb], m) for g in range(GROUP)], axis=0)
        return m

    q_near = query_with_mask(sel > 0.5)
    q_far = query_with_mask((sel > 0.5) & (lane < first_near_block))

    s = _dot_nt(q_near, ks_ref[0, pl.ds(near_start, 2 * qb), :]) + ntab_ref[0].reshape(rows, 2 * qb)
    m0 = jnp.max(s, axis=-1, keepdims=True)
    m0 = jnp.where(m0 == NEG_INF, 0.0, m0)
    a0 = _exp_pv(s, m0, vs_ref[0, pl.ds(near_start, 2 * qb), :])

    n_kt = ks_ref.shape[1] // kt

    def scores(t):
        k0 = pl.multiple_of(jnp.minimum(t, n_kt - 1) * kt, kt)
        return _dot_nt(q_far, ks_ref[0, pl.ds(k0, kt), :])

    def consume(t, s, m_old, acc):
        k0 = pl.multiple_of(jnp.minimum(t, n_kt - 1) * kt, kt)
        m_new = jnp.maximum(m_old, jnp.max(s, axis=-1, keepdims=True))
        return m_new, jnp.exp(m_old - m_new) * acc + _exp_pv(s, m_new, vs_ref[0, pl.ds(k0, kt), :])

    def far_pair(u, carry):
        m, acc = carry
        sb_ref[...] = scores(2 * u + 1)
        m, acc = consume(2 * u, sa_ref[...], m, acc)
        sa_ref[...] = scores(2 * u + 2)
        return consume(2 * u + 1, sb_ref[...], m, acc)

    n_far = (near_start + kt - 1) // kt
    sa_ref[...] = scores(0)
    _, acc_s = lax.fori_loop(0, (n_far + 1) // 2, far_pair, (m0, a0))
    o_s = _normalize_pv(acc_s)

    heads_out = []
    for g in range(GROUP):
        c = g * N_BRANCH
        sl = slice(g * qb, (g + 1) * qb)
        heads_out.append(gates[:, c:c + 1] * o_c[sl] + gates[:, c + 1:c + 2] * o_s[sl] + gates[:, c + 2:c + 3] * o_w[sl])
    tiles = [jnp.where(lane < HEAD_DIM, heads_out[2 * j], pltpu.roll(heads_out[2 * j + 1], HEAD_DIM, axis=1))
             for j in range(GROUP // 2)]
    o_ref[...] = jnp.concatenate(tiles, axis=-1).astype(BF16)


def _attn_prompt(qp, gates, kcmp, vcmp, ksb, vsb, kwb, vwb, fext, batch, seq):
    qb = Q_BLOCK
    nqb = seq // qb
    n_slc = seq // SLC_BLOCK
    assert n_slc <= LANES - HEAD_DIM
    n_cmp_pad = kcmp.shape[2]
    n_cmp = n_cmp_pad - 1
    kt = min(512, seq)
    assert (seq // kt) % 2 == 0
    band = WINDOW + qb
    iq = np.arange(qb)

    per_qb = qb // CMP_STRIDE
    half = 2 * per_qb
    m = np.arange(-half, half)
    d = iq[:, None] - (m[None, :] * CMP_STRIDE + CMP_BLOCK - 1)
    assert d[:, 0].min() >= MAX_DISTANCE and d[:, -1].max() < 0
    ctab = jnp.concatenate([jnp.zeros((N_HEADS, qb, n_cmp_pad - half), F32), _bias_table(fext, d, d >= 0),
                            jnp.full((N_HEADS, qb, n_cmp_pad - half), NEG_INF, F32)], axis=2)
    assert per_qb * (nqb - 1) < n_cmp_pad
    assert (n_cmp_pad - 1) * CMP_STRIDE + CMP_BLOCK - 1 >= seq and n_cmp == n_cmp_pad - 1

    nv = WINDOW // qb + 1
    kw_ = WINDOW + band
    w = qb + kw_
    assert w % LANES == 0
    m = np.arange(w)
    m = np.where(m < kw_, m, m - w)
    dj = WINDOW - m
    wide = _toeplitz_rows(_bias_table(fext, dj, (dj >= 0) & (dj < WINDOW)), qb)
    wtab = jnp.stack([wide[:, :, WINDOW - qb * v:WINDOW - qb * v + band] for v in range(nv)], axis=0)
    ntab = jnp.stack([wide[:, :, WINDOW - qb * v:WINDOW - qb * v + 2 * qb] for v in range(2)], axis=0)
    assert 2 * qb <= WINDOW

    c_start = np.arange(n_cmp_pad) * CMP_STRIDE
    s_start = np.arange(n_slc) * SLC_BLOCK
    ovl = ((c_start[None, :] < s_start[:, None] + SLC_BLOCK) & (c_start[None, :] + CMP_BLOCK > s_start[:, None])
           & (np.arange(n_cmp_pad) < n_cmp)[None, :])
    ovl = jnp.asarray(ovl, BF16)

    per_head = lambda: pl.BlockSpec((1, seq, LANES), lambda b, h, i: (h, b, 0))
    return pl.pallas_call(
        functools.partial(_attn_prompt_body, kt, n_slc),
        grid=(batch, N_KV_HEADS, nqb),
        in_specs=[pl.BlockSpec((1, GROUP, qb, LANES), lambda b, h, i: (b, h, i, 0)),
                  pl.BlockSpec((1, qb, LANES), lambda b, h, i: (h, b * nqb + i, 0)),
                  pl.BlockSpec((1, 1, n_cmp_pad, LANES), lambda b, h, i: (b, h, 0, 0)),
                  pl.BlockSpec((1, 1, n_cmp_pad, LANES), lambda b, h, i: (b, h, 0, 0)),
                  per_head(), per_head(), per_head(), per_head(),
                  pl.BlockSpec((GROUP, qb, 2 * n_cmp_pad), lambda b, h, i: (h, 0, 0)),
                  pl.BlockSpec((1, GROUP, qb, 2 * qb), lambda b, h, i: (jnp.minimum(i, 1), h, 0, 0)),
                  pl.BlockSpec((1, GROUP, qb, band), lambda b, h, i: (jnp.minimum(i, nv - 1), h, 0, 0)),
                  pl.BlockSpec(ovl.shape, lambda b, h, i: (0, 0))],
        out_specs=pl.BlockSpec((qb, GROUP * HEAD_DIM), lambda b, h, i: (b * nqb + i, h)),
        out_shape=jax.ShapeDtypeStruct((batch * seq, ATTN_DIM), BF16),
        scratch_shapes=[pltpu.VMEM((GROUP * qb, kt), F32)] * 2,
        compiler_params=_cparams("arbitrary", "arbitrary", "arbitrary"),
        name="attn_prompt",
    )(qp, gates, kcmp, vcmp, ksb, vsb, kwb, vwb, ctab, ntab, wtab, ovl)


def _attn_sample_body(n_seq, n_pages, ts, past_len, pt_ref, cache_k_ref, cache_v_ref, *refs):
    kbuf, vbuf, ksem, vsem = refs[-4:]
    n_ops = n_seq * n_pages
    slot = _paged_prefetch(pt_ref, [cache_k_ref, cache_v_ref], [kbuf, vbuf], [ksem, vsem], n_ops)
    for b in range(n_seq):
        _attn_sample_one(b, [kbuf.at[slot, b * n_pages + j] for j in range(n_pages)],
                         [vbuf.at[slot, b * n_pages + j] for j in range(n_pages)], refs[:-4], ts, past_len)


def _attn_sample_one(b, kpages, vpages, refs, ts, past_len):
    (q_ref, gt_ref, kcmp_ref, vcmp_ref, ksn_ref, vsn_ref, kwc_ref, vwc_ref, kwn_ref, vwn_ref,
     ctab_ref, stab_ref, sntab_ref, wtab_ref, wntab_ref, eexp_ref, ovl_ref, o_ref) = refs
    rows = GROUP * N_KV_HEADS * ts
    rq = N_KV_HEADS * ts
    q = q_ref[b]
    gates = gt_ref[b]

    s = _dot_nt(q, kcmp_ref[b].astype(BF16)) + ctab_ref[...]
    p, l = _softmax_parts(s)
    pn = p / jnp.maximum(l, 1e-30)
    o_c = _dot(pn.astype(BF16), vcmp_ref[b].astype(BF16))

    psum = pn[0:rq]
    for g in range(1, GROUP):
        psum = psum + pn[g * rq:(g + 1) * rq]
    hi, mid, lo = _split3(psum)
    ovl = ovl_ref[...]
    imp = _dot(hi, ovl) + _dot(mid, ovl) + _dot(lo, ovl)
    n_slc = -(-(past_len + ts) // SLC_BLOCK)
    blk = lax.broadcasted_iota(jnp.int32, (rq, LANES), 1)
    qpos = past_len + (lax.broadcasted_iota(jnp.int32, (rq, LANES), 0) & (ts - 1))
    qblk = qpos >> 6
    forced = (blk == 0) | (blk == qblk) | (blk == qblk - 1)
    valid = ((blk << 6) <= qpos) & (blk < n_slc)
    imp = jnp.where(valid, imp + jnp.where(forced, FORCE_SCORE, 0.0), NEG_INF)
    cnt = jnp.zeros((rq, LANES), jnp.int32)
    for sidx in range(n_slc):
        r = imp[:, sidx:sidx + 1]
        beats = (r > imp) | ((r == imp) & (blk > sidx))
        cnt = cnt + jnp.where(beats, 1, 0)
    sel = jnp.where((cnt < N_SEL) & valid, 1.0, 0.0)
    sel = jnp.concatenate([sel] * GROUP, axis=0).astype(BF16)

    kc_t = jnp.concatenate([p_[...] for p_ in kpages], axis=1).astype(BF16)
    vc_t = jnp.concatenate([p_[...] for p_ in vpages], axis=1).astype(BF16)
    mexp = _dot(sel, eexp_ref[...])
    s1 = jnp.where(mexp > 0.5, _dot(q, kc_t) + stab_ref[...], NEG_INF)
    last = sel[:, n_slc - 1:n_slc].astype(F32)
    s2 = jnp.where(last > 0.5, _dot_nt(q, ksn_ref[b].astype(BF16)) + sntab_ref[...], NEG_INF)
    m = jnp.maximum(jnp.max(s1, axis=-1, keepdims=True), jnp.max(s2, axis=-1, keepdims=True))
    m = jnp.where(m == NEG_INF, 0.0, m)
    p1 = jnp.exp(s1 - m)
    p2 = jnp.exp(s2 - m)
    l = jnp.sum(p1, axis=-1, keepdims=True) + jnp.sum(p2, axis=-1, keepdims=True)
    o_s = (_dot_nt(p1.astype(BF16), vc_t) + _dot(p2.astype(BF16), vsn_ref[b].astype(BF16))) / jnp.maximum(l, 1e-30)

    s1 = _dot(q, kwc_ref[b].astype(BF16)) + wtab_ref[...]
    s2 = _dot_nt(q, kwn_ref[b].astype(BF16)) + wntab_ref[...]
    m = jnp.maximum(jnp.max(s1, axis=-1, keepdims=True), jnp.max(s2, axis=-1, keepdims=True))
    m = jnp.where(m == NEG_INF, 0.0, m)
    p1 = jnp.exp(s1 - m)
    p2 = jnp.exp(s2 - m)
    l = jnp.sum(p1, axis=-1, keepdims=True) + jnp.sum(p2, axis=-1, keepdims=True)
    o_w = (_dot_nt(p1.astype(BF16), vwc_ref[b].astype(BF16))
           + _dot(p2.astype(BF16), vwn_ref[b].astype(BF16))) / jnp.maximum(l, 1e-30)

    o_ref[b] = gates[:, 0:1] * o_c + gates[:, 1:2] * o_s + gates[:, 2:3] * o_w


def _attn_sample(q_s, gates_s, kcmp, vcmp, ks_new, vs_new, kw_new, vw_new, cache_ks, cache_vs,
                 cache_kw, cache_vw, pt_flat, fext, n_batch, ts, n_pages, page):
    past_len = n_pages * page
    w_buf = cache_kw.shape[2]
    rows = GROUP * N_KV_HEADS * ts
    n_new = 8
    n_cmp_pad = kcmp.shape[1]
    n_cmp = n_cmp_pad - 1
    n_slc = -(-(past_len + ts) // SLC_BLOCK)

    q5 = q_s.reshape(n_batch, ts, N_KV_HEADS, GROUP, HEAD_DIM).transpose(0, 3, 2, 1, 4)
    eye = jnp.eye(N_KV_HEADS, dtype=q_s.dtype)
    qr = jnp.einsum("bghtd,hk->bghtkd", q5, eye).reshape(n_batch, rows, LANES).astype(BF16)
    g5 = gates_s[:, :N_HEADS * N_BRANCH].reshape(n_batch, ts, N_KV_HEADS, GROUP, N_BRANCH).transpose(0, 3, 2, 1, 4)
    gr = jnp.pad(g5.reshape(n_batch, rows, N_BRANCH), ((0, 0), (0, 0), (0, LANES - N_BRANCH)))
    pad_new = lambda a: jnp.pad(a.reshape(n_batch, ts, KV_DIM), ((0, 0), (0, n_new - ts), (0, 0)))
    ks_new, vs_new, kw_new, vw_new = map(pad_new, (ks_new, vs_new, kw_new, vw_new))

    g_i, h_i, t_i = np.meshgrid(np.arange(GROUP), np.arange(N_KV_HEADS), np.arange(ts), indexing="ij")
    head = (h_i * GROUP + g_i).reshape(rows)
    tq = t_i.reshape(rows)
    pos_q = past_len + tq

    f_rows = fext[jnp.asarray(head)]

    def table(d, valid):
        return jnp.take_along_axis(f_rows, jnp.asarray(_bias_index(d, valid)), axis=1)

    nn = np.arange(n_cmp_pad)
    d = pos_q[:, None] - (nn[None, :] * CMP_STRIDE + CMP_BLOCK - 1)
    ctab = table(d, (d >= 0) & (nn < n_cmp)[None, :])
    near = np.arange(past_len - MAX_DISTANCE, past_len)
    d = pos_q[:, None] - near[None, :]
    assert past_len >= MAX_DISTANCE and d.min() >= 0
    stab = jnp.concatenate([jnp.zeros((rows, past_len - MAX_DISTANCE), F32), table(d, d >= 0)], axis=1)
    jn = np.arange(n_new)
    d = tq[:, None] - jn[None, :]
    sntab = table(d, (d >= 0) & (jn < ts)[None, :])
    pos_w = past_len - w_buf + np.arange(w_buf)
    d = pos_q[:, None] - pos_w[None, :]
    wtab = table(d, (d >= 0) & (d < WINDOW) & (pos_w >= 0)[None, :])
    wntab = table(tq[:, None] - jn[None, :], (tq[:, None] >= jn[None, :]) & (jn < ts)[None, :])

    eexp = jnp.asarray(np.arange(past_len)[None, :] // SLC_BLOCK == np.arange(LANES)[:, None], BF16)
    c_start = nn * CMP_STRIDE
    s_start = np.arange(LANES) * SLC_BLOCK
    ovl = jnp.asarray((c_start[:, None] < s_start[None, :] + SLC_BLOCK) & (c_start[:, None] + CMP_BLOCK > s_start[None, :])
                      & (nn < n_cmp)[:, None] & (np.arange(LANES) < n_slc)[None, :], BF16)

    n_seq = SAMPLE_SEQS if n_batch % SAMPLE_SEQS == 0 else 1
    n_ops = n_seq * n_pages
    const = lambda a: pl.BlockSpec(a.shape, lambda b, pt: (0,) * a.ndim)
    per_b = lambda a: pl.BlockSpec((n_seq,) + a.shape[1:], lambda b, pt: (b,) + (0,) * (a.ndim - 1))
    any_spec = pl.BlockSpec(memory_space=pl.ANY)
    small = [qr, gr, kcmp, vcmp, ks_new, vs_new, cache_kw, cache_vw, kw_new, vw_new]
    consts = [ctab, stab, sntab, wtab, wntab, eexp, ovl]
    page_buf = pltpu.VMEM((2, n_ops, KV_DIM, page), F32)
    grid_spec = pltpu.PrefetchScalarGridSpec(
        num_scalar_prefetch=1, grid=(n_batch // n_seq,),
        in_specs=[any_spec, any_spec] + [per_b(a) for a in small] + [const(a) for a in consts],
        out_specs=pl.BlockSpec((n_seq, rows, LANES), lambda b, pt: (b, 0, 0)),
        scratch_shapes=[page_buf, page_buf, pltpu.SemaphoreType.DMA((2,)), pltpu.SemaphoreType.DMA((2,))])
    o = pl.pallas_call(
        functools.partial(_attn_sample_body, n_seq, n_pages, ts, past_len),
        grid_spec=grid_spec,
        out_shape=jax.ShapeDtypeStruct((n_batch, rows, LANES), F32),
        compiler_params=_cparams("arbitrary"),
        name="attn_sample",
    )(pt_flat, cache_ks, cache_vs, *small, *consts)
    o6 = o.reshape(n_batch, GROUP, N_KV_HEADS, ts, N_KV_HEADS, HEAD_DIM)
    o5 = jnp.stack([o6[:, :, h, :, h] for h in range(N_KV_HEADS)], axis=2)
    return o5.transpose(0, 3, 2, 1, 4).reshape(n_batch * ts, ATTN_DIM).astype(BF16)


def _post1_body(tm, x_ref, co_ref, at_ref, wo_ref, nf_ref, wr_ref, br_ref, tri_ref, run0_ref,
                h_ref, hn_ref, rt_ref, cnt_ref, run_ref):
    @pl.when(pl.program_id(0) == 0)
    def _():
        run_ref[...] = run0_ref[...]

    h = x_ref[...] + _dot(co_ref[...], wo_ref[0:CONV_DIM]) + _dot(at_ref[...], wo_ref[CONV_DIM:CONV_DIM + ATTN_DIM])
    hn = _rms(h, nf_ref[...])
    h_ref[...] = h
    _store_token_tiles(hn_ref, hn, tm)

    hi = hn.astype(BF16)
    lo = (hn - hi.astype(F32)).astype(BF16)
    wr = wr_ref[...]
    whi = wr.astype(BF16)
    wlo = (wr - whi.astype(F32)).astype(BF16)
    logits = _dot(hi, whi) + _dot(lo, whi) + _dot(hi, wlo) + br_ref[...]

    lane_i = lax.broadcasted_iota(jnp.int32, (tm, LANES), 1)
    lane = lane_i.astype(F32)
    big = float(LANES)
    gmask = (lane_i >= ROUTER_GROUP_LANE) & (lane_i < ROUTER_GROUP_LANE + N_GROUPS)
    lg = jnp.where(gmask, logits, NEG_INF)
    eg = jnp.exp(lg - jnp.max(lg, axis=-1, keepdims=True))
    pg = eg / jnp.sum(eg, axis=-1, keepdims=True)
    gw = jnp.max(pg, axis=-1, keepdims=True)
    grp = jnp.min(jnp.where(gmask & (pg == gw), lane, big), axis=-1, keepdims=True) - ROUTER_GROUP_LANE

    group_of_lane = (lane_i >> 3).astype(F32)
    emask = (lane_i < N_EXPERTS) & (group_of_lane == grp)
    le = jnp.where(emask, logits, NEG_INF)
    ee = jnp.exp(le - jnp.max(le, axis=-1, keepdims=True))
    pe = jnp.where(emask, ee / jnp.sum(ee, axis=-1, keepdims=True), -1.0)
    v1 = jnp.max(pe, axis=-1, keepdims=True)
    i1 = jnp.min(jnp.where(pe == v1, lane, big), axis=-1, keepdims=True)
    pe2 = jnp.where(lane == i1, -1.0, pe)
    v2 = jnp.max(pe2, axis=-1, keepdims=True)
    i2 = jnp.min(jnp.where(pe2 == v2, lane, big), axis=-1, keepdims=True)
    tot = v1 + v2
    w1 = v1 / tot * gw
    w2 = v2 / tot * gw

    oh1 = jnp.where(lane == i1, 1.0, 0.0)
    oh2 = jnp.where(lane == i2, 1.0, 0.0)
    both = oh1 + oh2
    before = _dot(tri_ref[...], both.astype(BF16)) + run_ref[0:1]
    r1 = jnp.sum(oh1 * before, axis=-1, keepdims=True)
    r2 = jnp.sum(oh2 * before, axis=-1, keepdims=True)
    run = run_ref[0:1] + jnp.sum(both, axis=0, keepdims=True)
    run_ref[...] = jnp.broadcast_to(run, run_ref.shape)
    cnt_ref[...] = jnp.broadcast_to(run, cnt_ref.shape)

    rt = jnp.where(lane_i == 0, i1, 0.0)
    rt = jnp.where(lane_i == 1, i2, rt)
    rt = jnp.where(lane_i == 2, r1, rt)
    rt = jnp.where(lane_i == 3, r2, rt)
    rt = jnp.where(lane_i == 4, w1, rt)
    rt = jnp.where(lane_i == 5, w2, rt)
    rt_ref[...] = rt


def _post1(x2d, co, at, wo, nf, wr, br, run0, tm):
    n, d = x2d.shape
    tri = jnp.asarray(np.tril(np.ones((tm, tm), np.float32), -1), BF16)
    rows = lambda w: pl.BlockSpec((tm, w), lambda i: (i, 0))
    const = lambda a: pl.BlockSpec(a.shape, lambda i: (0,) * a.ndim)
    return pl.pallas_call(
        functools.partial(_post1_body, tm),
        grid=(n // tm,),
        in_specs=[rows(d), rows(CONV_DIM), rows(ATTN_DIM), const(wo), const(nf), const(wr), const(br),
                  const(tri), const(run0)],
        out_specs=[rows(d), pl.BlockSpec((tm * TOKEN_TILE_ROWS, LANES), lambda i: (i, 0)), rows(LANES),
                   pl.BlockSpec((8, LANES), lambda i: (0, 0))],
        out_shape=[jax.ShapeDtypeStruct((n, d), F32), jax.ShapeDtypeStruct((n * TOKEN_TILE_ROWS, LANES), F32),
                   jax.ShapeDtypeStruct((n, LANES), F32), jax.ShapeDtypeStruct((8, LANES), F32)],
        scratch_shapes=[pltpu.VMEM((8, LANES), F32)],
        compiler_params=_cparams("arbitrary"),
        name="post1",
    )(x2d, co, at, wo, nf, wr, br, tri, run0)


def _token_copy(src_ref, dst_ref, s, d, sem):
    r = TOKEN_TILE_ROWS
    return pltpu.make_async_copy(src_ref.at[pl.ds(pl.multiple_of(s * r, r), r)],
                                 dst_ref.at[pl.ds(pl.multiple_of(d * r, r), r)], sem)


def _scatter_rows_body(ts, dest0_ref, dest1_ref, src_ref, init_ref, out_ref, sem):
    del init_ref
    base = pl.program_id(0) * ts

    def issue(t, _):
        for k, dest_ref in enumerate((dest0_ref, dest1_ref)):
            _token_copy(src_ref, out_ref, t, dest_ref[base + t], sem).start(priority=k)
        return 0

    lax.fori_loop(0, ts, issue, 0, unroll=DMA_ISSUE_UNROLL)
    for _ in range(2):
        pltpu.make_async_copy(src_ref, out_ref.at[pl.ds(0, ts * TOKEN_TILE_ROWS)], sem).wait()


def _scatter_rows(dests, src, slots):
    n_tok = dests[0].shape[0]
    ts = min(SCATTER_TOKENS, n_tok)
    assert n_tok % ts == 0
    any_spec = pl.BlockSpec(memory_space=pl.ANY)
    return pl.pallas_call(
        functools.partial(_scatter_rows_body, ts),
        grid_spec=pltpu.PrefetchScalarGridSpec(
            num_scalar_prefetch=2, grid=(n_tok // ts,),
            in_specs=[pl.BlockSpec((ts * TOKEN_TILE_ROWS, LANES), lambda i, d0, d1: (i, 0)), any_spec],
            out_specs=any_spec, scratch_shapes=[pltpu.SemaphoreType.DMA(())]),
        out_shape=jax.ShapeDtypeStruct(slots.shape, slots.dtype),
        input_output_aliases={3: 0},
        compiler_params=pltpu.CompilerParams(dimension_semantics=("arbitrary",)),
        name="scatter_rows",
    )(dests[0], dests[1], src, slots)


def _experts_body(be_ref, nu_ref, x_ref, wg_ref, wu_ref, wd_ref, o_ref, wg_s, wu_s, wd_s):
    i = pl.program_id(0)

    @pl.when(i < nu_ref[0])
    def _():
        prev = be_ref[jnp.maximum(i - 1, 0)]

        @pl.when((i == 0) | (be_ref[i] != prev))
        def _():
            wg_s[...] = wg_ref[0].astype(BF16)
            wu_s[...] = wu_ref[0].astype(BF16)
            wd_s[...] = wd_ref[0].astype(BF16)

        x = _load_token_tiles(x_ref, EXPERT_ROWS, TOKEN_TILE_ROWS).astype(BF16)
        g = _dot(x, wg_s[...])
        u = _dot(x, wu_s[...])
        a = g * _sigmoid(g) * u
        _store_token_tiles(o_ref, _dot(a.astype(BF16), wd_s[...]), EXPERT_ROWS)

    @pl.when(i >= nu_ref[0])
    def _():
        o_ref[...] = jnp.zeros_like(o_ref)


def _experts(blk_expert, n_used, xs, wg, wu, wd):
    blk_rows = EXPERT_ROWS * TOKEN_TILE_ROWS
    n_blk = xs.shape[0] // blk_rows
    _, d, de = wg.shape
    xmap = lambda i, be, nu: (jnp.minimum(i, jnp.maximum(nu[0] - 1, 0)), 0)
    wmap = lambda i, be, nu: (be[jnp.minimum(i, jnp.maximum(nu[0] - 1, 0))], 0, 0)
    grid_spec = pltpu.PrefetchScalarGridSpec(
        num_scalar_prefetch=2, grid=(n_blk,),
        in_specs=[pl.BlockSpec((blk_rows, LANES), xmap), pl.BlockSpec((1, d, de), wmap),
                  pl.BlockSpec((1, d, de), wmap), pl.BlockSpec((1, de, d), wmap)],
        out_specs=pl.BlockSpec((blk_rows, LANES), lambda i, be, nu: (i, 0)),
        scratch_shapes=[pltpu.VMEM((d, de), BF16), pltpu.VMEM((d, de), BF16), pltpu.VMEM((de, d), BF16)])
    return pl.pallas_call(
        _experts_body, grid_spec=grid_spec,
        out_shape=jax.ShapeDtypeStruct(xs.shape, F32),
        compiler_params=_cparams("arbitrary"),
        name="experts",
    )(blk_expert, n_used, xs, wg, wu, wd)


def _post2_body(tm, dest0_ref, dest1_ref, h_ref, rt_ref, p_ref, yb_ref, wple_ref, wpg_ref, bpg_ref, np_ref, o_ref,
                buf, sem):
    i = pl.program_id(0)
    n = pl.num_programs(0)

    def fetch(step, slot):
        base = step * tm

        def issue(t, _):
            for k, dest_ref in enumerate((dest0_ref, dest1_ref)):
                _token_copy(yb_ref, buf.at[slot], dest_ref[base + t], k * tm + t, sem.at[slot]).start(priority=k)
            return 0

        lax.fori_loop(0, tm, issue, 0, unroll=DMA_ISSUE_UNROLL)

    @pl.when(i == 0)
    def _():
        fetch(0, 0)

    @pl.when(i + 1 < n)
    def _():
        fetch(i + 1, (i + 1) & 1)

    slot = i & 1

    pltpu.make_async_copy(yb_ref.at[pl.ds(0, 2 * tm * TOKEN_TILE_ROWS)], buf.at[slot], sem.at[slot]).wait()
    rt = rt_ref[...]
    y0 = _load_token_tiles(buf.at[slot], tm, TOKEN_TILE_ROWS)
    y1 = _load_token_tiles(buf.at[slot], tm, TOKEN_TILE_ROWS, first=tm * TOKEN_TILE_ROWS)
    h = h_ref[...] + (y0 * rt[:, 4:5] + y1 * rt[:, 5:6])
    gate = _sigmoid(_dot(_rms(h, np_ref[...]).astype(BF16), wpg_ref[...]) + bpg_ref[...])
    o_ref[...] = h + gate * _dot(p_ref[...].astype(BF16), wple_ref[...])


def _post2(dests, h, rt, p2d, yb, wple, wpg, bpg, npl, tm):
    n, d = h.shape
    rows = lambda w: pl.BlockSpec((tm, w), lambda i, d0, d1: (i, 0))
    const = lambda a: pl.BlockSpec(a.shape, lambda i, d0, d1: (0,) * a.ndim)
    grid_spec = pltpu.PrefetchScalarGridSpec(
        num_scalar_prefetch=2, grid=(n // tm,),
        in_specs=[rows(d), rows(LANES), rows(p2d.shape[1]), pl.BlockSpec(memory_space=pl.ANY), const(wple),
                  const(wpg), const(bpg), const(npl)],
        out_specs=rows(d),
        scratch_shapes=[pltpu.VMEM((2, 2 * tm * TOKEN_TILE_ROWS, LANES), F32), pltpu.SemaphoreType.DMA((2,))])
    return pl.pallas_call(
        functools.partial(_post2_body, tm),
        grid_spec=grid_spec,
        out_shape=jax.ShapeDtypeStruct((n, d), F32),
        compiler_params=_cparams("arbitrary"),
        name="post2",
    )(dests[0], dests[1], h, rt, p2d, yb, wple, wpg, bpg, npl)


def _row_tile(n, cap=512):
    t = min(cap, n)
    assert n % t == 0 and t % 8 == 0
    return t


def kernel(x_prompt, x_sample, p_prompt, p_sample, cache_k_cmp, cache_v_cmp, cache_k_slc, cache_v_slc, cache_k_win, cache_v_win, state_conv, page_table, w_in, w_out, conv_w, norm_mix, norm_ffn, norm_ple, q_norm, k_norm, cmp_pe_k, cmp_w1_k, cmp_w2_k, cmp_pe_v, cmp_w1_v, cmp_w2_v, rel_bias, w_router_group, b_router_group, w_router_expert, b_router_expert, w_exp_gate, w_exp_up, w_exp_down, w_ple, w_ple_gate, b_ple_gate):
    assert w_in.shape[0] == 1, "single-layer step"
    bp, t, d = x_prompt.shape
    bs, ts, _ = x_sample.shape
    n_pages = page_table.shape[1]
    page = cache_k_cmp.shape[2]
    past_len = n_pages * page
    w_buf = cache_k_win.shape[2]
    n_phys = cache_k_cmp.shape[1]
    assert t % Q_BLOCK == 0 and t >= WINDOW + Q_BLOCK and page % CMP_STRIDE == 0 and ts == 4 and d == D_MODEL
    assert past_len % SLC_BLOCK == 0
    np_rows, ns_rows = bp * t, bs * ts

    row = lambda v: v.reshape(1, -1).astype(F32)
    w_in_b = jnp.pad(w_in[0], ((0, 0), (0, Z_COLS - w_in.shape[2]))).astype(BF16)
    qn = row(jnp.tile(q_norm[0], N_HEADS))
    kn1 = row(jnp.tile(k_norm[0, 1], N_KV_HEADS))
    kn2 = row(jnp.tile(k_norm[0, 2], N_KV_HEADS))
    bd = jnp.asarray(np.kron(np.eye(N_HEADS), np.ones((HEAD_DIM, HEAD_DIM))), BF16)
    pw = (row(norm_mix[0]), w_in_b, qn, kn1, kn2, conv_w[0].astype(F32), bd)
    cw_k = _compress_weights(cmp_pe_k[0], cmp_w1_k[0], cmp_w2_k[0])
    cw_v = _compress_weights(cmp_pe_v[0], cmp_w1_v[0], cmp_w2_v[0])
    kn0 = row(k_norm[0, 0])
    fext = _bias_by_distance(rel_bias)
    pt_flat = page_table.reshape(-1).astype(jnp.int32)
    wr = jnp.zeros((d, LANES), F32).at[:, :N_EXPERTS].set(w_router_expert[0])
    wr = wr.at[:, ROUTER_GROUP_LANE:ROUTER_GROUP_LANE + N_GROUPS].set(w_router_group[0])
    br = jnp.zeros((1, LANES), F32).at[0, :N_EXPERTS].set(b_router_expert[0])
    br = br.at[0, ROUTER_GROUP_LANE:ROUTER_GROUP_LANE + N_GROUPS].set(b_router_group[0])
    wo_b = w_out[0].astype(BF16)
    wple_b = w_ple[0].astype(BF16)
    wpg_b = w_ple_gate[0].astype(BF16)

    tm_p = _row_tile(t, cap=1024)
    (co_p, q_p, kc_p, vc_p, ks_p, vs_p, kw_p, vw_p, gt_p, cs_p, ksb, vsb, kwb, vwb, kc_t, vc_t, ks_t, vs_t,
     kc_x, vc_x) = _project(
        x_prompt.reshape(np_rows, d), bp, t, tm_p, pw)
    chunk_w = CMP_STRIDE * KV_DIM
    kcmp_p = _compress_rows(kc_x.reshape(bp, t // CMP_STRIDE, chunk_w), cw_k, kn0, True)
    vcmp_p = _compress_rows(vc_x.reshape(bp, t // CMP_STRIDE, chunk_w), cw_v, kn0, False)
    at_p = _attn_prompt(q_p, gt_p, kcmp_p, vcmp_p, ksb, vsb, kwb, vwb, fext, bp, t)

    st = state_conv[0].astype(F32)
    s0 = jnp.repeat(st[:, 0], ts, axis=0)
    s1 = jnp.repeat(st[:, 1], ts, axis=0)
    (co_s, q_s, kc_s, vc_s, ks_s, vs_s, kw_s, vw_s, gt_s, u_s) = _project(
        x_sample.reshape(ns_rows, d), bs, ts, ns_rows, pw, state=(s0, s1))
    feature_major = lambda c, n, rows_: jnp.transpose(c[0], (0, 2, 3, 1)).reshape(n, KV_DIM, rows_)
    kcmp_s = _compress_pages(feature_major(cache_k_cmp, n_phys, page), pt_flat, bs, n_pages, cw_k, kn0, True)
    vcmp_s = _compress_pages(feature_major(cache_v_cmp, n_phys, page), pt_flat, bs, n_pages, cw_v, kn0, False)
    at_s = _attn_sample(q_s, gt_s, kcmp_s, vcmp_s, ks_s, vs_s, kw_s, vw_s,
                        feature_major(cache_k_slc, n_phys, page), feature_major(cache_v_slc, n_phys, page),
                        feature_major(cache_k_win, bs, w_buf), feature_major(cache_v_win, bs, w_buf),
                        pt_flat, fext, bs, ts, n_pages, page)

    tp1 = _row_tile(np_rows)
    ts1 = _row_tile(ns_rows)
    nf = row(norm_ffn[0])
    h_p, hn_p, rt_p, cnt_p = _post1(x_prompt.reshape(np_rows, d), co_p, at_p, wo_b, nf, wr, br,
                                    jnp.zeros((8, LANES), F32), tp1)
    h_s, hn_s, rt_s, cnt_s = _post1(x_sample.reshape(ns_rows, d), co_s, at_s, wo_b, nf, wr, br, cnt_p, ts1)

    counts = cnt_s[0, :N_EXPERTS].astype(jnp.int32)
    padded = (counts + EXPERT_ROWS - 1) // EXPERT_ROWS * EXPERT_ROWS
    pad_end = jnp.cumsum(padded)
    pad_start = pad_end - padded
    n_assign = 2 * (np_rows + ns_rows)
    n_blk = (n_assign + N_EXPERTS * (EXPERT_ROWS - 1) + EXPERT_ROWS - 1) // EXPERT_ROWS
    blk_first = jnp.arange(n_blk, dtype=jnp.int32) * EXPERT_ROWS
    blk_expert = jnp.minimum(jnp.sum((pad_end[None, :] <= blk_first[:, None]).astype(jnp.int32), axis=1),
                             N_EXPERTS - 1)
    n_used = (pad_end[-1:] // EXPERT_ROWS).astype(jnp.int32)

    def dest_of(rt):
        return [pad_start[rt[:, k].astype(jnp.int32)] + rt[:, 2 + k].astype(jnp.int32) for k in range(2)]

    dest_p = dest_of(rt_p)
    dest_s = dest_of(rt_s)

    xs = jnp.zeros((n_blk * EXPERT_ROWS * TOKEN_TILE_ROWS, LANES), F32)
    xs = _scatter_rows(dest_p, hn_p, xs)
    xs = _scatter_rows(dest_s, hn_s, xs)
    yb = _experts(blk_expert, n_used, xs, w_exp_gate[0], w_exp_up[0], w_exp_down[0])

    bpg = row(b_ple_gate[0])
    npl = row(norm_ple[0])
    y_p = _post2(dest_p, h_p, rt_p, p_prompt[0].reshape(np_rows, -1), yb, wple_b, wpg_b, bpg, npl, tp1)
    y_s = _post2(dest_s, h_s, rt_s, p_sample[0].reshape(ns_rows, -1), yb, wple_b, wpg_b, bpg, npl, ts1)

    kv5 = lambda a, b, s: a.reshape(1, b, s, N_KV_HEADS, HEAD_DIM)
    wp = min(WINDOW, t)
    win_p = lambda a: kv5(a.reshape(bp, t, KV_DIM)[:, t - wp:], bp, wp)
    win_s = lambda c, new: jnp.concatenate([c[0], new.reshape(bs, ts, N_KV_HEADS, HEAD_DIM)], axis=1)[None, :, ts:]
    conv_p = cs_p[:, 8 - (CONV_K - 1):][None]
    conv_s = u_s.reshape(bs, ts, CONV_DIM)[:, ts - (CONV_K - 1):][None]
    from_t = lambda a: jnp.transpose(a.reshape(bp, N_KV_HEADS, HEAD_DIM, t), (0, 3, 1, 2))[None]
    return (y_p.reshape(bp, t, d), y_s.reshape(bs, ts, d),
            from_t(kc_t), from_t(vc_t), from_t(ks_t), from_t(vs_t), win_p(kw_p), win_p(vw_p), conv_p,
            kv5(kc_s, bs, ts), kv5(vc_s, bs, ts), kv5(ks_s, bs, ts), kv5(vs_s, bs, ts),
            win_s(cache_k_win, kw_s), win_s(cache_v_win, vw_s), conv_s)
```

```python
import functools
import math

import numpy as np
import jax
import jax.numpy as jnp
from jax import lax
from jax.experimental import pallas as pl
from jax.experimental.pallas import tpu as pltpu

F32 = jnp.float32
BF16 = jnp.bfloat16
NEG_INF = float("-inf")
MASKED = -1e30

HEAD_DIM = 64
N_HEADS = 8
N_KV_HEADS = 2
GROUP = N_HEADS // N_KV_HEADS
CONV_DIM = 512
ATTN_DIM = 512
KV_DIM = N_KV_HEADS * HEAD_DIM
N_BRANCH = 3
CONV_K = 3
CMP_BLOCK = 32
CMP_STRIDE = 16
CMP_HIDDEN = 256
SLC_BLOCK = 64
N_SEL = 16
WINDOW = 512
Q_BLOCK = 128
FORCE_SCORE = 1e4
NUM_BUCKETS = 32
MAX_DISTANCE = 128
N_GROUPS = 4
EXPERTS_PER_GROUP = 8
N_EXPERTS = N_GROUPS * EXPERTS_PER_GROUP
D_EXPERT = 512
EPS = 1e-6

D_MODEL = 1024
LANES = 128
TOKEN_TILE_ROWS = D_MODEL // LANES
Z_COLS = 3 * CONV_DIM + ATTN_DIM + 6 * KV_DIM + LANES
BIAS_DMAX = 768
EXPERT_ROWS = 512
ROUTER_GROUP_LANE = 32
SCATTER_TOKENS = 256
DMA_ISSUE_UNROLL = 8
COMPRESS_SEQS = 8
SAMPLE_SEQS = 4
VMEM_LIMIT = 56 * 1024 * 1024


def _cparams(*sem):
    return pltpu.CompilerParams(dimension_semantics=sem, vmem_limit_bytes=VMEM_LIMIT)


def _dot(a, b):
    return jnp.dot(a, b, preferred_element_type=F32)


def _dot_nt(a, b):
    return lax.dot_general(a, b, (((1,), (1,)), ((), ())), preferred_element_type=F32)


def _split3(x):
    hi = x.astype(BF16)
    r = x - hi.astype(F32)
    mid = r.astype(BF16)
    lo = (r - mid.astype(F32)).astype(BF16)
    return hi, mid, lo


def _rms(x, g):
    return x * lax.rsqrt(jnp.mean(x * x, axis=-1, keepdims=True) + EPS) * g


def _head_rms(x, bd, g):
    hi, mid, _ = _split3(x * x)
    ss = _dot(hi, bd) + _dot(mid, bd)
    return x * lax.rsqrt(ss * (1.0 / HEAD_DIM) + EPS) * g


def _sigmoid(x):
    return 1.0 / (1.0 + jnp.exp(-x))


def _store_token_tiles(ref, x, n):
    r = x.shape[1] // LANES
    for j in range(r):
        ref[pl.ds(j, n, stride=r), :] = x[:, j * LANES:(j + 1) * LANES]


def _load_token_tiles(ref, n, r, first=0):
    return jnp.concatenate([ref[pl.ds(first + j, n, stride=r), :] for j in range(r)], axis=-1)


def _softmax_parts(s):
    m = jnp.max(s, axis=-1, keepdims=True)
    m = jnp.where(m == NEG_INF, 0.0, m)
    p = jnp.exp(s - m)
    l = jnp.sum(p, axis=-1, keepdims=True)
    return p, l


def _proj_body(sample, tm, *refs):
    if sample:
        (x_ref, nm_ref, w_ref, qn_ref, kn1_ref, kn2_ref, cw_ref, bd_ref, s0_ref, s1_ref,
         co_ref, q_ref, kc_ref, vc_ref, ks_ref, vs_ref, kw_ref, vw_ref, gt_ref, u_ref) = refs
    else:
        (x_ref, nm_ref, w_ref, qn_ref, kn1_ref, kn2_ref, cw_ref, bd_ref,
         co_ref, q_ref, kw_ref, vw_ref, gt_ref, cs_ref,
         ksb_ref, vsb_ref, kwb_ref, vwb_ref, kct_ref, vct_ref, kst_ref, vst_ref, kcx_ref, vcx_ref,
         carry_ref, stage_ref) = refs

    xn = _rms(x_ref[...], nm_ref[...]).astype(BF16)

    def seg(a, b):
        return _dot(xn, w_ref[:, a:b])

    c3 = 3 * CONV_DIM
    u = seg(2 * CONV_DIM, c3) * seg(0, CONV_DIM)
    bg = seg(CONV_DIM, 2 * CONV_DIM)
    row = lax.broadcasted_iota(jnp.int32, (tm, 1), 0)
    um1 = pltpu.roll(u, 1, axis=0)
    um2 = pltpu.roll(u, 2, axis=0)
    if sample:
        r = row & 3
        s0 = s0_ref[...]
        s1 = s1_ref[...]
        prev1 = jnp.where(r == 0, s1, um1)
        prev2 = jnp.where(r == 0, s0, jnp.where(r == 1, s1, um2))
        u_ref[...] = u
    else:
        @pl.when(pl.program_id(1) == 0)
        def _():
            carry_ref[...] = jnp.zeros_like(carry_ref)
        c = carry_ref[...]
        prev1 = jnp.where(row == 0, c[7:8], um1)
        prev2 = jnp.where(row == 0, c[6:7], jnp.where(row == 1, c[7:8], um2))
        carry_ref[...] = u[tm - 8:tm]
        cs_ref[0] = u[tm - 8:tm]
    cw = cw_ref[...]
    y = cw[0:1] * prev2 + cw[1:2] * prev1 + cw[2:3] * u
    co_ref[...] = (bg * y).astype(BF16)

    bd = bd_ref[...]
    q = _head_rms(seg(c3, c3 + ATTN_DIM), bd, qn_ref[...]) * (HEAD_DIM ** -0.5)
    lane = lax.broadcasted_iota(jnp.int32, (tm, LANES), 1)
    low = lane < HEAD_DIM

    def head_planes(x, fill):
        return [jnp.where(low, x if h == 0 else pltpu.roll(x, HEAD_DIM, axis=1), fill) for h in range(N_KV_HEADS)]

    if sample:
        q_ref[...] = q
    else:
        for hd in range(N_HEADS):
            pair = q[:, (hd // 2) * LANES:(hd // 2 + 1) * LANES]
            if hd % 2:
                pair = pltpu.roll(pair, HEAD_DIM, axis=1)
            q_ref[0, hd] = jnp.where(low, pair, 0.0).astype(BF16)

    k0 = c3 + ATTN_DIM
    bdk = bd[:KV_DIM, :KV_DIM]
    kc = seg(k0, k0 + KV_DIM)
    vc = seg(k0 + KV_DIM, k0 + 2 * KV_DIM)
    ks = _head_rms(seg(k0 + 2 * KV_DIM, k0 + 3 * KV_DIM), bdk, kn1_ref[...])
    vs = seg(k0 + 3 * KV_DIM, k0 + 4 * KV_DIM)
    kw = _head_rms(seg(k0 + 4 * KV_DIM, k0 + 5 * KV_DIM), bdk, kn2_ref[...])
    vw = seg(k0 + 5 * KV_DIM, k0 + 6 * KV_DIM)
    kw_ref[...] = kw
    vw_ref[...] = vw
    if sample:
        kc_ref[...] = kc
        vc_ref[...] = vc
        ks_ref[...] = ks
        vs_ref[...] = vs
    else:
        kct_ref[0] = kc.T
        vct_ref[0] = vc.T
        kst_ref[0] = ks.T
        vst_ref[0] = vs.T
        for src, dst in ((kc, kcx_ref), (vc, vcx_ref)):
            stage_ref[...] = src
            dst[...] = jnp.concatenate([stage_ref[pl.ds(r, tm // CMP_STRIDE, stride=CMP_STRIDE), :]
                                        for r in range(CMP_STRIDE)], axis=-1)
    gates = _sigmoid(seg(k0 + 6 * KV_DIM, k0 + 6 * KV_DIM + LANES))
    if sample:
        gt_ref[...] = gates
    else:
        pos = pl.program_id(1) * tm + lax.broadcasted_iota(jnp.int32, (tm, LANES), 0)
        block_onehot = jnp.where(lane - HEAD_DIM == (pos >> 6), 1.0, 0.0)
        for h, (a, b, c, e) in enumerate(zip(head_planes(ks, block_onehot), head_planes(vs, 1.0),
                                             head_planes(kw, 0.0), head_planes(vw, 1.0))):
            ksb_ref[h] = a.astype(BF16)
            vsb_ref[h] = b.astype(BF16)
            kwb_ref[h] = c.astype(BF16)
            vwb_ref[h] = e.astype(BF16)
        gt_ref[0] = gates
        gt_ref[1] = pltpu.roll(gates, LANES - GROUP * N_BRANCH, axis=1)


def _project(x2d, batch, seq, tm, weights, state=None):
    n, d = x2d.shape
    sample = state is not None
    nt = seq // tm if not sample else 1
    const = lambda shape: pl.BlockSpec(shape, lambda b, t: (0,) * len(shape))
    rows = lambda w: pl.BlockSpec((tm, w), lambda b, t: (b * nt + t, 0))
    nm, w_in, qn, kn1, kn2, cw, bd = weights
    in_specs = [rows(d), const(nm.shape), const(w_in.shape), const(qn.shape), const(kn1.shape),
                const(kn2.shape), const(cw.shape), const(bd.shape)]
    args = [x2d, nm, w_in, qn, kn1, kn2, cw, bd]
    kv_f32 = [jax.ShapeDtypeStruct((n, KV_DIM), F32)] * 6
    if sample:
        in_specs += [rows(CONV_DIM), rows(CONV_DIM)]
        args += list(state)
        out_shape = ([jax.ShapeDtypeStruct((n, CONV_DIM), BF16), jax.ShapeDtypeStruct((n, ATTN_DIM), F32)]
                     + kv_f32 + [jax.ShapeDtypeStruct((n, LANES), F32), jax.ShapeDtypeStruct((n, CONV_DIM), F32)])
        out_specs = [rows(CONV_DIM), rows(ATTN_DIM)] + [rows(KV_DIM)] * 6 + [rows(LANES), rows(CONV_DIM)]
        scratch = []
        grid = (1, 1)
    else:
        planes = pl.BlockSpec((N_KV_HEADS, tm, LANES), lambda b, t: (0, b * nt + t, 0))
        out_shape = ([jax.ShapeDtypeStruct((n, CONV_DIM), BF16),
                      jax.ShapeDtypeStruct((batch, N_HEADS, seq, LANES), BF16)]
                     + kv_f32[:2] + [jax.ShapeDtypeStruct((N_KV_HEADS, n, LANES), F32),
                                     jax.ShapeDtypeStruct((batch, 8, CONV_DIM), F32)]
                     + [jax.ShapeDtypeStruct((N_KV_HEADS, n, LANES), BF16)] * 4
                     + [jax.ShapeDtypeStruct((batch, KV_DIM, seq), F32)] * 4
                     + [jax.ShapeDtypeStruct((n // CMP_STRIDE, CMP_STRIDE * KV_DIM), F32)] * 2)
        out_specs = ([rows(CONV_DIM), pl.BlockSpec((1, N_HEADS, tm, LANES), lambda b, t: (b, 0, t, 0))]
                     + [rows(KV_DIM)] * 2 + [planes, pl.BlockSpec((1, 8, CONV_DIM), lambda b, t: (b, 0, 0))]
                     + [planes] * 4 + [pl.BlockSpec((1, KV_DIM, tm), lambda b, t: (b, 0, t))] * 4
                     + [pl.BlockSpec((tm // CMP_STRIDE, CMP_STRIDE * KV_DIM), lambda b, t: (b * nt + t, 0))] * 2)
        scratch = [pltpu.VMEM((8, CONV_DIM), F32), pltpu.VMEM((tm, KV_DIM), F32)]
        grid = (batch, nt)
    return pl.pallas_call(
        functools.partial(_proj_body, sample, tm),
        grid=grid, in_specs=in_specs, out_specs=out_specs, out_shape=out_shape, scratch_shapes=scratch,
        compiler_params=_cparams("arbitrary", "arbitrary"),
        name="proj_sample" if sample else "proj_prompt",
    )(*args)


def _gelu_tanh(x):
    cdf = 0.5 * (1.0 + jnp.tanh(math.sqrt(2.0 / math.pi) * (x + 0.044715 * (x * x * x))))
    return x * cdf


def _compress_core(norm, x, pe_ref, we_ref, w2_ref, g_ref):
    n = x.shape[0]
    a0 = _dot((x + pe_ref[0:1]).astype(BF16), we_ref[0])
    a1 = _dot((x + pe_ref[1:2]).astype(BF16), we_ref[1])
    hid = a0 + pltpu.roll(a1, n - 1, axis=0)
    w2 = w2_ref[...]
    outs = []
    for h in range(N_KV_HEADS):
        act = _gelu_tanh(hid[:, h * CMP_HIDDEN:(h + 1) * CMP_HIDDEN])
        o = _dot(act.astype(BF16), w2)
        if norm:
            o = _rms(o, g_ref[...])
        outs.append(o)
    return outs


def _compress_rows_body(norm, x_ref, pe_ref, we_ref, w2_ref, g_ref, o_ref):
    outs = _compress_core(norm, x_ref[0], pe_ref, we_ref, w2_ref, g_ref)
    for h in range(N_KV_HEADS):
        o_ref[0, h] = jnp.concatenate([outs[h], jnp.zeros_like(outs[h])], axis=-1).astype(BF16)


def _fetch_pages(pt_ref, cache_ref, buf_ref, sem_ref, step, slot, n_ops, priority):
    base = step * n_ops

    def issue(j, _):
        pltpu.make_async_copy(cache_ref.at[pt_ref[base + j]], buf_ref.at[slot, j], sem_ref.at[slot]).start(
            priority=priority)
        return 0

    lax.fori_loop(0, n_ops, issue, 0)


def _paged_prefetch(pt_ref, caches, bufs, sems, n_ops):
    i = pl.program_id(0)
    slot = i & 1
    for k, (cache_ref, buf_ref, sem_ref) in enumerate(zip(caches, bufs, sems)):
        @pl.when(i == 0)
        def _():
            _fetch_pages(pt_ref, cache_ref, buf_ref, sem_ref, 0, 0, n_ops, k % 2)

        @pl.when(i + 1 < pl.num_programs(0))
        def _():
            _fetch_pages(pt_ref, cache_ref, buf_ref, sem_ref, i + 1, 1 - slot, n_ops, k % 2)

    for cache_ref, buf_ref, sem_ref in zip(caches, bufs, sems):
        pltpu.make_async_copy(cache_ref.at[pl.ds(0, n_ops)], buf_ref.at[slot], sem_ref.at[slot]).wait()
    return slot


def _compress_pages_body(norm, nch, n_seq, n_pages, pt_ref, cache_ref, pe_ref, we_ref, w2_ref, g_ref, o_ref,
                         rows_ref, buf_ref, sem_ref):
    n_ops = n_seq * n_pages
    slot = _paged_prefetch(pt_ref, [cache_ref], [buf_ref], [sem_ref], n_ops)
    page = buf_ref.shape[3]
    for j in range(n_ops):
        rows_ref[j * page:(j + 1) * page, :] = buf_ref[slot, j].T
    x = jnp.concatenate([rows_ref[pl.ds(r, n_seq * nch, stride=CMP_STRIDE), :] for r in range(CMP_STRIDE)], axis=-1)
    out = jnp.concatenate(_compress_core(norm, x, pe_ref, we_ref, w2_ref, g_ref), axis=-1)
    for b in range(n_seq):
        o_ref[b] = out[b * nch:(b + 1) * nch]


def _compress_weights(pe, w1, w2):
    w1r = w1.reshape(2, CMP_STRIDE, HEAD_DIM, CMP_HIDDEN).astype(BF16)
    z = jnp.zeros_like(w1r)
    we = jnp.stack([jnp.concatenate([w1r, z], axis=-1), jnp.concatenate([z, w1r], axis=-1)], axis=2)
    we = we.reshape(2, CMP_STRIDE * KV_DIM, N_KV_HEADS * CMP_HIDDEN)
    per = pe.reshape(2, CMP_STRIDE, 1, HEAD_DIM)
    pex = jnp.broadcast_to(per, (2, CMP_STRIDE, N_KV_HEADS, HEAD_DIM)).reshape(2, CMP_STRIDE * KV_DIM)
    return pex.astype(F32), we.astype(BF16), w2.astype(BF16)


def _compress_rows(rows3, cw, gain, norm):
    b, nch, width = rows3.shape
    pex, we, w2 = cw
    const = lambda a: pl.BlockSpec(a.shape, lambda i: (0,) * a.ndim)
    return pl.pallas_call(
        functools.partial(_compress_rows_body, norm),
        grid=(b,),
        in_specs=[pl.BlockSpec((1, nch, width), lambda i: (i, 0, 0)), const(pex), const(we), const(w2), const(gain)],
        out_specs=pl.BlockSpec((1, N_KV_HEADS, nch, LANES), lambda i: (i, 0, 0, 0)),
        out_shape=jax.ShapeDtypeStruct((b, N_KV_HEADS, nch, LANES), BF16),
        compiler_params=_cparams("arbitrary"),
        name="compress_rows",
    )(rows3, pex, we, w2, gain)


def _compress_pages(cache_t, pt_flat, n_batch, n_pages, cw, gain, norm):
    _, _, page = cache_t.shape
    nch = n_pages * page // CMP_STRIDE
    pex, we, w2 = cw
    n_seq = COMPRESS_SEQS if n_batch % COMPRESS_SEQS == 0 else 1
    const = lambda a: pl.BlockSpec(a.shape, lambda i, pt: (0,) * a.ndim)
    n_ops = n_seq * n_pages
    grid_spec = pltpu.PrefetchScalarGridSpec(
        num_scalar_prefetch=1, grid=(n_batch // n_seq,),
        in_specs=[pl.BlockSpec(memory_space=pl.ANY), const(pex), const(we), const(w2), const(gain)],
        out_specs=pl.BlockSpec((n_seq, nch, KV_DIM), lambda i, pt: (i, 0, 0)),
        scratch_shapes=[pltpu.VMEM((n_ops * page, KV_DIM), F32), pltpu.VMEM((2, n_ops, KV_DIM, page), F32),
                        pltpu.SemaphoreType.DMA((2,))])
    return pl.pallas_call(
        functools.partial(_compress_pages_body, norm, nch, n_seq, n_pages),
        grid_spec=grid_spec,
        out_shape=jax.ShapeDtypeStruct((n_batch, nch, KV_DIM), F32),
        compiler_params=_cparams("arbitrary"),
        name="compress_pages",
    )(pt_flat, cache_t, pex, we, w2, gain)


def _rel_bucket(dist):
    n = jnp.maximum(dist, 0)
    max_exact = NUM_BUCKETS // 2
    nf = jnp.maximum(n, 1).astype(F32)
    large = max_exact + (jnp.log(nf / max_exact) / math.log(MAX_DISTANCE / max_exact)
                         * (NUM_BUCKETS - max_exact)).astype(jnp.int32)
    large = jnp.minimum(large, NUM_BUCKETS - 1)
    return jnp.where(n < max_exact, n, large)


def _bias_by_distance(rel_bias):
    d = jnp.arange(BIAS_DMAX, dtype=jnp.int32)
    f = rel_bias.astype(F32)[_rel_bucket(d)]
    f = (f - f[BIAS_DMAX - 1:BIAS_DMAX]).T
    return jnp.concatenate([f, jnp.full((N_HEADS, 1), NEG_INF, F32)], axis=1)


def _bias_index(d, valid):
    return np.where(valid, np.clip(d, 0, BIAS_DMAX - 1), BIAS_DMAX).astype(np.int32)


def _bias_table(fext, d, valid):
    return jnp.take(fext, jnp.asarray(_bias_index(d, valid)), axis=1)


def _toeplitz_body(n_rows, v_ref, o_ref):
    x = jnp.broadcast_to(v_ref[0], (n_rows, v_ref.shape[2]))
    o_ref[0] = pltpu.roll(x, 0, axis=1, stride=1, stride_axis=0)


def _toeplitz_rows(v, n_rows):
    h, w = v.shape
    return pl.pallas_call(
        functools.partial(_toeplitz_body, n_rows),
        grid=(h,),
        in_specs=[pl.BlockSpec((1, 1, w), lambda i: (i, 0, 0))],
        out_specs=pl.BlockSpec((1, n_rows, w), lambda i: (i, 0, 0)),
        out_shape=jax.ShapeDtypeStruct((h, n_rows, w), F32),
        compiler_params=_cparams("arbitrary"),
        name="toeplitz_rows",
    )(v.reshape(h, 1, w))


def _select_blocks(imp_t, srow, qpos, n_rank):
    qblk = qpos >> 6
    forced = (srow == 0) | (srow == qblk) | (srow == qblk - 1)
    valid = (srow << 6) <= qpos
    imp_t = jnp.where(valid, imp_t + jnp.where(forced, FORCE_SCORE, 0.0), NEG_INF)
    n_rows = imp_t.shape[0]
    assert n_rows % 8 == 0
    slabs = [imp_t[a:a + 8] for a in range(0, n_rows, 8)]
    rows8 = [srow[a:a + 8] for a in range(0, n_rows, 8)]
    cnts = [jnp.zeros(x.shape, jnp.int32) for x in slabs]
    for s in range(n_rank):
        r = imp_t[s:s + 1, :]
        for j, x in enumerate(slabs):
            if 8 * j > s:
                beats = r >= x
            elif 8 * j + 7 <= s:
                beats = r > x
            else:
                beats = (r > x) | ((r == x) & (rows8[j] > s))
            cnts[j] = cnts[j] + jnp.where(beats, 1, 0)
    cnt = jnp.concatenate(cnts, axis=0)
    return jnp.where((cnt < N_SEL) & valid, 1.0, 0.0)


def _exp_pv(s, m, v):
    return _dot(jnp.exp((s - m).astype(BF16)), v)


def _normalize_pv(pv):
    return pv / jnp.maximum(pv[:, HEAD_DIM:HEAD_DIM + 1], 1e-30)


def _attn_prompt_body(kt, n_slc, q_ref, gt_ref, kcmp_ref, vcmp_ref, ks_ref, vs_ref, kw_ref, vw_ref,
                      ctab_ref, ntab_ref, wtab_ref, ovl_ref, o_ref, sa_ref, sb_ref):
    i = pl.program_id(2)
    qb = Q_BLOCK
    rows = GROUP * qb
    n_cmp_pad = kcmp_ref.shape[2]
    first_near_block = (qb // SLC_BLOCK) * jnp.maximum(i - 1, 0)
    near_start = pl.multiple_of(jnp.maximum(i - 1, 0) * qb, qb)
    win_start = pl.multiple_of(jnp.maximum(i * qb - WINDOW, 0), qb)
    band = WINDOW + qb
    gates = gt_ref[0]
    lane = lax.broadcasted_iota(jnp.int32, (qb, LANES), 1)
    q0 = q_ref[0].reshape(rows, LANES)

    s = _dot_nt(q0, kw_ref[0, pl.ds(win_start, band), :]) + wtab_ref[0].reshape(rows, band)
    m = jnp.max(s, axis=-1, keepdims=True)
    m = jnp.where(m == NEG_INF, 0.0, m)
    o_w = _normalize_pv(_exp_pv(s, m, vw_ref[0, pl.ds(win_start, band), :]))

    per_qb = qb // CMP_STRIDE
    ctab = pltpu.roll(ctab_ref[...].reshape(rows, 2 * n_cmp_pad), i * per_qb, axis=1)[:, n_cmp_pad:]
    s = _dot_nt(q0, kcmp_ref[0, 0]) + ctab
    p, l = _softmax_parts(s)
    pn = p / jnp.maximum(l, 1e-30)
    o_c = _dot(pn.astype(BF16), vcmp_ref[0, 0])

    psum = pn[0:qb] + pn[qb:2 * qb] + pn[2 * qb:3 * qb] + pn[3 * qb:4 * qb]
    hi, mid, lo = _split3(psum)
    ovl = ovl_ref[...]
    imp_t = _dot_nt(ovl, hi) + _dot_nt(ovl, mid) + _dot_nt(ovl, lo)
    srow = lax.broadcasted_iota(jnp.int32, (n_slc, qb), 0)
    qpos_t = i * qb + lax.broadcasted_iota(jnp.int32, (n_slc, qb), 1)
    sel_t = _select_blocks(imp_t, srow, qpos_t, n_slc)
    sel_t = jnp.concatenate([sel_t, jnp.zeros((LANES - n_slc, qb), F32)], axis=0)
    sel = sel_t.T

    def query_with_mask(keep):
        m = pltpu.roll(jnp.where(keep, 0.0, MASKED), HEAD_DIM, axis=1).astype(BF16)
        m = jnp.concatenate([jnp.where(lane < HEAD_DIM, q0[g * qb:(g + 1) * qb], m) for g in range(GROUP)], axis=0)
        return m

    q_near = query_with_mask(sel > 0.5)
    q_far = query_with_mask((sel > 0.5) & (lane < first_near_block))

    s = _dot_nt(q_near, ks_ref[0, pl.ds(near_start, 2 * qb), :]) + ntab_ref[0].reshape(rows, 2 * qb)
    m0 = jnp.max(s, axis=-1, keepdims=True)
    m0 = jnp.where(m0 == NEG_INF, 0.0, m0)
    a0 = _exp_pv(s, m0, vs_ref[0, pl.ds(near_start, 2 * qb), :])

    n_kt = ks_ref.shape[1] // kt

    def scores(t):
        k0 = pl.multiple_of(jnp.minimum(t, n_kt - 1) * kt, kt)
        return _dot_nt(q_far, ks_ref[0, pl.ds(k0, kt), :])

    def consume(t, s, m_old, acc):
        k0 = pl.multiple_of(jnp.minimum(t, n_kt - 1) * kt, kt)
        m_new = jnp.maximum(m_old, jnp.max(s, axis=-1, keepdims=True))
        return m_new, jnp.exp(m_old - m_new) * acc + _exp_pv(s, m_new, vs_ref[0, pl.ds(k0, kt), :])

    def far_pair(u, carry):
        m, acc = carry
        sb_ref[...] = scores(2 * u + 1)
        m, acc = consume(2 * u, sa_ref[...], m, acc)
        sa_ref[...] = scores(2 * u + 2)
        return consume(2 * u + 1, sb_ref[...], m, acc)

    n_far = (near_start + kt - 1) // kt
    sa_ref[...] = scores(0)
    _, acc_s = lax.fori_loop(0, (n_far + 1) // 2, far_pair, (m0, a0))
    o_s = _normalize_pv(acc_s)

    heads_out = []
    for g in range(GROUP):
        c = g * N_BRANCH
        sl = slice(g * qb, (g + 1) * qb)
        heads_out.append(gates[:, c:c + 1] * o_c[sl] + gates[:, c + 1:c + 2] * o_s[sl] + gates[:, c + 2:c + 3] * o_w[sl])
    tiles = [jnp.where(lane < HEAD_DIM, heads_out[2 * j], pltpu.roll(heads_out[2 * j + 1], HEAD_DIM, axis=1))
             for j in range(GROUP // 2)]
    o_ref[...] = jnp.concatenate(tiles, axis=-1).astype(BF16)


def _attn_prompt(qp, gates, kcmp, vcmp, ksb, vsb, kwb, vwb, fext, batch, seq):
    qb = Q_BLOCK
    nqb = seq // qb
    n_slc = seq // SLC_BLOCK
    assert n_slc <= LANES - HEAD_DIM
    n_cmp_pad = kcmp.shape[2]
    n_cmp = n_cmp_pad - 1
    kt = min(512, seq)
    assert (seq // kt) % 2 == 0
    band = WINDOW + qb
    iq = np.arange(qb)

    per_qb = qb // CMP_STRIDE
    half = 2 * per_qb
    m = np.arange(-half, half)
    d = iq[:, None] - (m[None, :] * CMP_STRIDE + CMP_BLOCK - 1)
    assert d[:, 0].min() >= MAX_DISTANCE and d[:, -1].max() < 0
    ctab = jnp.concatenate([jnp.zeros((N_HEADS, qb, n_cmp_pad - half), F32), _bias_table(fext, d, d >= 0),
                            jnp.full((N_HEADS, qb, n_cmp_pad - half), NEG_INF, F32)], axis=2)
    assert per_qb * (nqb - 1) < n_cmp_pad
    assert (n_cmp_pad - 1) * CMP_STRIDE + CMP_BLOCK - 1 >= seq and n_cmp == n_cmp_pad - 1

    nv = WINDOW // qb + 1
    kw_ = WINDOW + band
    w = qb + kw_
    assert w % LANES == 0
    m = np.arange(w)
    m = np.where(m < kw_, m, m - w)
    dj = WINDOW - m
    wide = _toeplitz_rows(_bias_table(fext, dj, (dj >= 0) & (dj < WINDOW)), qb)
    wtab = jnp.stack([wide[:, :, WINDOW - qb * v:WINDOW - qb * v + band] for v in range(nv)], axis=0)
    ntab = jnp.stack([wide[:, :, WINDOW - qb * v:WINDOW - qb * v + 2 * qb] for v in range(2)], axis=0)
    assert 2 * qb <= WINDOW

    c_start = np.arange(n_cmp_pad) * CMP_STRIDE
    s_start = np.arange(n_slc) * SLC_BLOCK
    ovl = ((c_start[None, :] < s_start[:, None] + SLC_BLOCK) & (c_start[None, :] + CMP_BLOCK > s_start[:, None])
           & (np.arange(n_cmp_pad) < n_cmp)[None, :])
    ovl = jnp.asarray(ovl, BF16)

    per_head = lambda: pl.BlockSpec((1, seq, LANES), lambda b, h, i: (h, b, 0))
    return pl.pallas_call(
        functools.partial(_attn_prompt_body, kt, n_slc),
        grid=(batch, N_KV_HEADS, nqb),
        in_specs=[pl.BlockSpec((1, GROUP, qb, LANES), lambda b, h, i: (b, h, i, 0)),
                  pl.BlockSpec((1, qb, LANES), lambda b, h, i: (h, b * nqb + i, 0)),
                  pl.BlockSpec((1, 1, n_cmp_pad, LANES), lambda b, h, i: (b, h, 0, 0)),
                  pl.BlockSpec((1, 1, n_cmp_pad, LANES), lambda b, h, i: (b, h, 0, 0)),
                  per_head(), per_head(), per_head(), per_head(),
                  pl.BlockSpec((GROUP, qb, 2 * n_cmp_pad), lambda b, h, i: (h, 0, 0)),
                  pl.BlockSpec((1, GROUP, qb, 2 * qb), lambda b, h, i: (jnp.minimum(i, 1), h, 0, 0)),
                  pl.BlockSpec((1, GROUP, qb, band), lambda b, h, i: (jnp.minimum(i, nv - 1), h, 0, 0)),
                  pl.BlockSpec(ovl.shape, lambda b, h, i: (0, 0))],
        out_specs=pl.BlockSpec((qb, GROUP * HEAD_DIM), lambda b, h, i: (b * nqb + i, h)),
        out_shape=jax.ShapeDtypeStruct((batch * seq, ATTN_DIM), BF16),
        scratch_shapes=[pltpu.VMEM((GROUP * qb, kt), F32)] * 2,
        compiler_params=_cparams("arbitrary", "arbitrary", "arbitrary"),
        name="attn_prompt",
    )(qp, gates, kcmp, vcmp, ksb, vsb, kwb, vwb, ctab, ntab, wtab, ovl)


def _attn_sample_body(n_seq, n_pages, ts, past_len, pt_ref, cache_k_ref, cache_v_ref, *refs):
    kbuf, vbuf, ksem, vsem = refs[-4:]
    n_ops = n_seq * n_pages
    slot = _paged_prefetch(pt_ref, [cache_k_ref, cache_v_ref], [kbuf, vbuf], [ksem, vsem], n_ops)
    for b in range(n_seq):
        _attn_sample_one(b, [kbuf.at[slot, b * n_pages + j] for j in range(n_pages)],
                         [vbuf.at[slot, b * n_pages + j] for j in range(n_pages)], refs[:-4], ts, past_len)


def _attn_sample_one(b, kpages, vpages, refs, ts, past_len):
    (q_ref, gt_ref, kcmp_ref, vcmp_ref, ksn_ref, vsn_ref, kwc_ref, vwc_ref, kwn_ref, vwn_ref,
     ctab_ref, stab_ref, sntab_ref, wtab_ref, wntab_ref, eexp_ref, ovl_ref, o_ref) = refs
    rows = GROUP * N_KV_HEADS * ts
    rq = N_KV_HEADS * ts
    q = q_ref[b]
    gates = gt_ref[b]

    s = _dot_nt(q, kcmp_ref[b].astype(BF16)) + ctab_ref[...]
    p, l = _softmax_parts(s)
    pn = p / jnp.maximum(l, 1e-30)
    o_c = _dot(pn.astype(BF16), vcmp_ref[b].astype(BF16))

    psum = pn[0:rq]
    for g in range(1, GROUP):
        psum = psum + pn[g * rq:(g + 1) * rq]
    hi, mid, lo = _split3(psum)
    ovl = ovl_ref[...]
    imp = _dot(hi, ovl) + _dot(mid, ovl) + _dot(lo, ovl)
    n_slc = -(-(past_len + ts) // SLC_BLOCK)
    blk = lax.broadcasted_iota(jnp.int32, (rq, LANES), 1)
    qpos = past_len + (lax.broadcasted_iota(jnp.int32, (rq, LANES), 0) & (ts - 1))
    qblk = qpos >> 6
    forced = (blk == 0) | (blk == qblk) | (blk == qblk - 1)
    valid = ((blk << 6) <= qpos) & (blk < n_slc)
    imp = jnp.where(valid, imp + jnp.where(forced, FORCE_SCORE, 0.0), NEG_INF)
    cnt = jnp.zeros((rq, LANES), jnp.int32)
    for sidx in range(n_slc):
        r = imp[:, sidx:sidx + 1]
        beats = (r > imp) | ((r == imp) & (blk > sidx))
        cnt = cnt + jnp.where(beats, 1, 0)
    sel = jnp.where((cnt < N_SEL) & valid, 1.0, 0.0)
    sel = jnp.concatenate([sel] * GROUP, axis=0).astype(BF16)

    kc_t = jnp.concatenate([p_[...] for p_ in kpages], axis=1).astype(BF16)
    vc_t = jnp.concatenate([p_[...] for p_ in vpages], axis=1).astype(BF16)
    mexp = _dot(sel, eexp_ref[...])
    s1 = jnp.where(mexp > 0.5, _dot(q, kc_t) + stab_ref[...], NEG_INF)
    last = sel[:, n_slc - 1:n_slc].astype(F32)
    s2 = jnp.where(last > 0.5, _dot_nt(q, ksn_ref[b].astype(BF16)) + sntab_ref[...], NEG_INF)
    m = jnp.maximum(jnp.max(s1, axis=-1, keepdims=True), jnp.max(s2, axis=-1, keepdims=True))
    m = jnp.where(m == NEG_INF, 0.0, m)
    p1 = jnp.exp(s1 - m)
    p2 = jnp.exp(s2 - m)
    l = jnp.sum(p1, axis=-1, keepdims=True) + jnp.sum(p2, axis=-1, keepdims=True)
    o_s = (_dot_nt(p1.astype(BF16), vc_t) + _dot(p2.astype(BF16), vsn_ref[b].astype(BF16))) / jnp.maximum(l, 1e-30)

    s1 = _dot(q, kwc_ref[b].astype(BF16)) + wtab_ref[...]
    s2 = _dot_nt(q, kwn_ref[b].astype(BF16)) + wntab_ref[...]
    m = jnp.maximum(jnp.max(s1, axis=-1, keepdims=True), jnp.max(s2, axis=-1, keepdims=True))
    m = jnp.where(m == NEG_INF, 0.0, m)
    p1 = jnp.exp(s1 - m)
    p2 = jnp.exp(s2 - m)
    l = jnp.sum(p1, axis=-1, keepdims=True) + jnp.sum(p2, axis=-1, keepdims=True)
    o_w = (_dot_nt(p1.astype(BF16), vwc_ref[b].astype(BF16))
           + _dot(p2.astype(BF16), vwn_ref[b].astype(BF16))) / jnp.maximum(l, 1e-30)

    o_ref[b] = gates[:, 0:1] * o_c + gates[:, 1:2] * o_s + gates[:, 2:3] * o_w


def _attn_sample(q_s, gates_s, kcmp, vcmp, ks_new, vs_new, kw_new, vw_new, cache_ks, cache_vs,
                 cache_kw, cache_vw, pt_flat, fext, n_batch, ts, n_pages, page):
    past_len = n_pages * page
    w_buf = cache_kw.shape[2]
    rows = GROUP * N_KV_HEADS * ts
    n_new = 8
    n_cmp_pad = kcmp.shape[1]
    n_cmp = n_cmp_pad - 1
    n_slc = -(-(past_len + ts) // SLC_BLOCK)

    q5 = q_s.reshape(n_batch, ts, N_KV_HEADS, GROUP, HEAD_DIM).transpose(0, 3, 2, 1, 4)
    eye = jnp.eye(N_KV_HEADS, dtype=q_s.dtype)
    qr = jnp.einsum("bghtd,hk->bghtkd", q5, eye).reshape(n_batch, rows, LANES).astype(BF16)
    g5 = gates_s[:, :N_HEADS * N_BRANCH].reshape(n_batch, ts, N_KV_HEADS, GROUP, N_BRANCH).transpose(0, 3, 2, 1, 4)
    gr = jnp.pad(g5.reshape(n_batch, rows, N_BRANCH), ((0, 0), (0, 0), (0, LANES - N_BRANCH)))
    pad_new = lambda a: jnp.pad(a.reshape(n_batch, ts, KV_DIM), ((0, 0), (0, n_new - ts), (0, 0)))
    ks_new, vs_new, kw_new, vw_new = map(pad_new, (ks_new, vs_new, kw_new, vw_new))

    g_i, h_i, t_i = np.meshgrid(np.arange(GROUP), np.arange(N_KV_HEADS), np.arange(ts), indexing="ij")
    head = (h_i * GROUP + g_i).reshape(rows)
    tq = t_i.reshape(rows)
    pos_q = past_len + tq

    f_rows = fext[jnp.asarray(head)]

    def table(d, valid):
        return jnp.take_along_axis(f_rows, jnp.asarray(_bias_index(d, valid)), axis=1)

    nn = np.arange(n_cmp_pad)
    d = pos_q[:, None] - (nn[None, :] * CMP_STRIDE + CMP_BLOCK - 1)
    ctab = table(d, (d >= 0) & (nn < n_cmp)[None, :])
    near = np.arange(past_len - MAX_DISTANCE, past_len)
    d = pos_q[:, None] - near[None, :]
    assert past_len >= MAX_DISTANCE and d.min() >= 0
    stab = jnp.concatenate([jnp.zeros((rows, past_len - MAX_DISTANCE), F32), table(d, d >= 0)], axis=1)
    jn = np.arange(n_new)
    d = tq[:, None] - jn[None, :]
    sntab = table(d, (d >= 0) & (jn < ts)[None, :])
    pos_w = past_len - w_buf + np.arange(w_buf)
    d = pos_q[:, None] - pos_w[None, :]
    wtab = table(d, (d >= 0) & (d < WINDOW) & (pos_w >= 0)[None, :])
    wntab = table(tq[:, None] - jn[None, :], (tq[:, None] >= jn[None, :]) & (jn < ts)[None, :])

    eexp = jnp.asarray(np.arange(past_len)[None, :] // SLC_BLOCK == np.arange(LANES)[:, None], BF16)
    c_start = nn * CMP_STRIDE
    s_start = np.arange(LANES) * SLC_BLOCK
    ovl = jnp.asarray((c_start[:, None] < s_start[None, :] + SLC_BLOCK) & (c_start[:, None] + CMP_BLOCK > s_start[None, :])
                      & (nn < n_cmp)[:, None] & (np.arange(LANES) < n_slc)[None, :], BF16)

    n_seq = SAMPLE_SEQS if n_batch % SAMPLE_SEQS == 0 else 1
    n_ops = n_seq * n_pages
    const = lambda a: pl.BlockSpec(a.shape, lambda b, pt: (0,) * a.ndim)
    per_b = lambda a: pl.BlockSpec((n_seq,) + a.shape[1:], lambda b, pt: (b,) + (0,) * (a.ndim - 1))
    any_spec = pl.BlockSpec(memory_space=pl.ANY)
    small = [qr, gr, kcmp, vcmp, ks_new, vs_new, cache_kw, cache_vw, kw_new, vw_new]
    consts = [ctab, stab, sntab, wtab, wntab, eexp, ovl]
    page_buf = pltpu.VMEM((2, n_ops, KV_DIM, page), F32)
    grid_spec = pltpu.PrefetchScalarGridSpec(
        num_scalar_prefetch=1, grid=(n_batch // n_seq,),
        in_specs=[any_spec, any_spec] + [per_b(a) for a in small] + [const(a) for a in consts],
        out_specs=pl.BlockSpec((n_seq, rows, LANES), lambda b, pt: (b, 0, 0)),
        scratch_shapes=[page_buf, page_buf, pltpu.SemaphoreType.DMA((2,)), pltpu.SemaphoreType.DMA((2,))])
    o = pl.pallas_call(
        functools.partial(_attn_sample_body, n_seq, n_pages, ts, past_len),
        grid_spec=grid_spec,
        out_shape=jax.ShapeDtypeStruct((n_batch, rows, LANES), F32),
        compiler_params=_cparams("arbitrary"),
        name="attn_sample",
    )(pt_flat, cache_ks, cache_vs, *small, *consts)
    o6 = o.reshape(n_batch, GROUP, N_KV_HEADS, ts, N_KV_HEADS, HEAD_DIM)
    o5 = jnp.stack([o6[:, :, h, :, h] for h in range(N_KV_HEADS)], axis=2)
    return o5.transpose(0, 3, 2, 1, 4).reshape(n_batch * ts, ATTN_DIM).astype(BF16)


def _post1_body(tm, x_ref, co_ref, at_ref, wo_ref, nf_ref, wr_ref, br_ref, tri_ref, run0_ref,
                h_ref, hn_ref, rt_ref, cnt_ref, run_ref):
    @pl.when(pl.program_id(0) == 0)
    def _():
        run_ref[...] = run0_ref[...]

    h = x_ref[...] + _dot(co_ref[...], wo_ref[0:CONV_DIM]) + _dot(at_ref[...], wo_ref[CONV_DIM:CONV_DIM + ATTN_DIM])
    hn = _rms(h, nf_ref[...])
    h_ref[...] = h
    _store_token_tiles(hn_ref, hn, tm)

    hi = hn.astype(BF16)
    lo = (hn - hi.astype(F32)).astype(BF16)
    wr = wr_ref[...]
    whi = wr.astype(BF16)
    wlo = (wr - whi.astype(F32)).astype(BF16)
    logits = _dot(hi, whi) + _dot(lo, whi) + _dot(hi, wlo) + br_ref[...]

    lane_i = lax.broadcasted_iota(jnp.int32, (tm, LANES), 1)
    lane = lane_i.astype(F32)
    big = float(LANES)
    gmask = (lane_i >= ROUTER_GROUP_LANE) & (lane_i < ROUTER_GROUP_LANE + N_GROUPS)
    lg = jnp.where(gmask, logits, NEG_INF)
    eg = jnp.exp(lg - jnp.max(lg, axis=-1, keepdims=True))
    pg = eg / jnp.sum(eg, axis=-1, keepdims=True)
    gw = jnp.max(pg, axis=-1, keepdims=True)
    grp = jnp.min(jnp.where(gmask & (pg == gw), lane, big), axis=-1, keepdims=True) - ROUTER_GROUP_LANE

    group_of_lane = (lane_i >> 3).astype(F32)
    emask = (lane_i < N_EXPERTS) & (group_of_lane == grp)
    le = jnp.where(emask, logits, NEG_INF)
    ee = jnp.exp(le - jnp.max(le, axis=-1, keepdims=True))
    pe = jnp.where(emask, ee / jnp.sum(ee, axis=-1, keepdims=True), -1.0)
    v1 = jnp.max(pe, axis=-1, keepdims=True)
    i1 = jnp.min(jnp.where(pe == v1, lane, big), axis=-1, keepdims=True)
    pe2 = jnp.where(lane == i1, -1.0, pe)
    v2 = jnp.max(pe2, axis=-1, keepdims=True)
    i2 = jnp.min(jnp.where(pe2 == v2, lane, big), axis=-1, keepdims=True)
    tot = v1 + v2
    w1 = v1 / tot * gw
    w2 = v2 / tot * gw

    oh1 = jnp.where(lane == i1, 1.0, 0.0)
    oh2 = jnp.where(lane == i2, 1.0, 0.0)
    both = oh1 + oh2
    before = _dot(tri_ref[...], both.astype(BF16)) + run_ref[0:1]
    r1 = jnp.sum(oh1 * before, axis=-1, keepdims=True)
    r2 = jnp.sum(oh2 * before, axis=-1, keepdims=True)
    run = run_ref[0:1] + jnp.sum(both, axis=0, keepdims=True)
    run_ref[...] = jnp.broadcast_to(run, run_ref.shape)
    cnt_ref[...] = jnp.broadcast_to(run, cnt_ref.shape)

    rt = jnp.where(lane_i == 0, i1, 0.0)
    rt = jnp.where(lane_i == 1, i2, rt)
    rt = jnp.where(lane_i == 2, r1, rt)
    rt = jnp.where(lane_i == 3, r2, rt)
    rt = jnp.where(lane_i == 4, w1, rt)
    rt = jnp.where(lane_i == 5, w2, rt)
    rt_ref[...] = rt


def _post1(x2d, co, at, wo, nf, wr, br, run0, tm):
    n, d = x2d.shape
    tri = jnp.asarray(np.tril(np.ones((tm, tm), np.float32), -1), BF16)
    rows = lambda w: pl.BlockSpec((tm, w), lambda i: (i, 0))
    const = lambda a: pl.BlockSpec(a.shape, lambda i: (0,) * a.ndim)
    return pl.pallas_call(
        functools.partial(_post1_body, tm),
        grid=(n // tm,),
        in_specs=[rows(d), rows(CONV_DIM), rows(ATTN_DIM), const(wo), const(nf), const(wr), const(br),
                  const(tri), const(run0)],
        out_specs=[rows(d), pl.BlockSpec((tm * TOKEN_TILE_ROWS, LANES), lambda i: (i, 0)), rows(LANES),
                   pl.BlockSpec((8, LANES), lambda i: (0, 0))],
        out_shape=[jax.ShapeDtypeStruct((n, d), F32), jax.ShapeDtypeStruct((n * TOKEN_TILE_ROWS, LANES), F32),
                   jax.ShapeDtypeStruct((n, LANES), F32), jax.ShapeDtypeStruct((8, LANES), F32)],
        scratch_shapes=[pltpu.VMEM((8, LANES), F32)],
        compiler_params=_cparams("arbitrary"),
        name="post1",
    )(x2d, co, at, wo, nf, wr, br, tri, run0)


def _token_copy(src_ref, dst_ref, s, d, sem):
    r = TOKEN_TILE_ROWS
    return pltpu.make_async_copy(src_ref.at[pl.ds(pl.multiple_of(s * r, r), r)],
                                 dst_ref.at[pl.ds(pl.multiple_of(d * r, r), r)], sem)


def _scatter_rows_body(ts, dest_ref, src_ref, init_ref, out_ref, sem):
    del init_ref
    base = pl.program_id(0) * (2 * ts)

    def issue(t, _):
        for k in range(2):
            _token_copy(src_ref, out_ref, t, dest_ref[base + 2 * t + k], sem).start(priority=k)
        return 0

    lax.fori_loop(0, ts, issue, 0, unroll=DMA_ISSUE_UNROLL)
    for _ in range(2):
        pltpu.make_async_copy(src_ref, out_ref.at[pl.ds(0, ts * TOKEN_TILE_ROWS)], sem).wait()


def _scatter_rows(dest, src, slots):
    n_tok = dest.shape[0] // 2
    ts = min(SCATTER_TOKENS, n_tok)
    assert n_tok % ts == 0
    any_spec = pl.BlockSpec(memory_space=pl.ANY)
    return pl.pallas_call(
        functools.partial(_scatter_rows_body, ts),
        grid_spec=pltpu.PrefetchScalarGridSpec(
            num_scalar_prefetch=1, grid=(n_tok // ts,),
            in_specs=[pl.BlockSpec((ts * TOKEN_TILE_ROWS, LANES), lambda i, dest: (i, 0)), any_spec],
            out_specs=any_spec, scratch_shapes=[pltpu.SemaphoreType.DMA(())]),
        out_shape=jax.ShapeDtypeStruct(slots.shape, slots.dtype),
        input_output_aliases={2: 0},
        compiler_params=pltpu.CompilerParams(dimension_semantics=("arbitrary",)),
        name="scatter_rows",
    )(dest, src, slots)


def _experts_body(be_ref, nu_ref, x_ref, wg_ref, wu_ref, wd_ref, o_ref, wg_s, wu_s, wd_s):
    i = pl.program_id(0)

    @pl.when(i < nu_ref[0])
    def _():
        prev = be_ref[jnp.maximum(i - 1, 0)]

        @pl.when((i == 0) | (be_ref[i] != prev))
        def _():
            wg_s[...] = wg_ref[0].astype(BF16)
            wu_s[...] = wu_ref[0].astype(BF16)
            wd_s[...] = wd_ref[0].astype(BF16)

        x = _load_token_tiles(x_ref, EXPERT_ROWS, TOKEN_TILE_ROWS).astype(BF16)
        g = _dot(x, wg_s[...])
        u = _dot(x, wu_s[...])
        a = g * _sigmoid(g) * u
        _store_token_tiles(o_ref, _dot(a.astype(BF16), wd_s[...]), EXPERT_ROWS)

    @pl.when(i >= nu_ref[0])
    def _():
        o_ref[...] = jnp.zeros_like(o_ref)


def _experts(blk_expert, n_used, xs, wg, wu, wd):
    blk_rows = EXPERT_ROWS * TOKEN_TILE_ROWS
    n_blk = xs.shape[0] // blk_rows
    _, d, de = wg.shape
    xmap = lambda i, be, nu: (jnp.minimum(i, jnp.maximum(nu[0] - 1, 0)), 0)
    wmap = lambda i, be, nu: (be[jnp.minimum(i, jnp.maximum(nu[0] - 1, 0))], 0, 0)
    grid_spec = pltpu.PrefetchScalarGridSpec(
        num_scalar_prefetch=2, grid=(n_blk,),
        in_specs=[pl.BlockSpec((blk_rows, LANES), xmap), pl.BlockSpec((1, d, de), wmap),
                  pl.BlockSpec((1, d, de), wmap), pl.BlockSpec((1, de, d), wmap)],
        out_specs=pl.BlockSpec((blk_rows, LANES), lambda i, be, nu: (i, 0)),
        scratch_shapes=[pltpu.VMEM((d, de), BF16), pltpu.VMEM((d, de), BF16), pltpu.VMEM((de, d), BF16)])
    return pl.pallas_call(
        _experts_body, grid_spec=grid_spec,
        out_shape=jax.ShapeDtypeStruct(xs.shape, F32),
        compiler_params=_cparams("arbitrary"),
        name="experts",
    )(blk_expert, n_used, xs, wg, wu, wd)


def _post2_body(tm, dest_ref, h_ref, rt_ref, p_ref, yb_ref, wple_ref, wpg_ref, bpg_ref, np_ref, o_ref, buf, sem):
    i = pl.program_id(0)
    n = pl.num_programs(0)

    def fetch(step, slot):
        base = step * (2 * tm)

        def issue(t, _):
            for k in range(2):
                _token_copy(yb_ref, buf.at[slot], dest_ref[base + 2 * t + k], k * tm + t, sem.at[slot]).start(priority=k)
            return 0

        lax.fori_loop(0, tm, issue, 0, unroll=DMA_ISSUE_UNROLL)

    @pl.when(i == 0)
    def _():
        fetch(0, 0)

    @pl.when(i + 1 < n)
    def _():
        fetch(i + 1, (i + 1) & 1)

    slot = i & 1

    pltpu.make_async_copy(yb_ref.at[pl.ds(0, 2 * tm * TOKEN_TILE_ROWS)], buf.at[slot], sem.at[slot]).wait()
    rt = rt_ref[...]
    y0 = _load_token_tiles(buf.at[slot], tm, TOKEN_TILE_ROWS)
    y1 = _load_token_tiles(buf.at[slot], tm, TOKEN_TILE_ROWS, first=tm * TOKEN_TILE_ROWS)
    h = h_ref[...] + (y0 * rt[:, 4:5] + y1 * rt[:, 5:6])
    gate = _sigmoid(_dot(_rms(h, np_ref[...]).astype(BF16), wpg_ref[...]) + bpg_ref[...])
    o_ref[...] = h + gate * _dot(p_ref[...].astype(BF16), wple_ref[...])


def _post2(dest, h, rt, p2d, yb, wple, wpg, bpg, npl, tm):
    n, d = h.shape
    rows = lambda w: pl.BlockSpec((tm, w), lambda i, dest: (i, 0))
    const = lambda a: pl.BlockSpec(a.shape, lambda i, dest: (0,) * a.ndim)
    grid_spec = pltpu.PrefetchScalarGridSpec(
        num_scalar_prefetch=1, grid=(n // tm,),
        in_specs=[rows(d), rows(LANES), rows(p2d.shape[1]), pl.BlockSpec(memory_space=pl.ANY), const(wple),
                  const(wpg), const(bpg), const(npl)],
        out_specs=rows(d),
        scratch_shapes=[pltpu.VMEM((2, 2 * tm * TOKEN_TILE_ROWS, LANES), F32), pltpu.SemaphoreType.DMA((2,))])
    return pl.pallas_call(
        functools.partial(_post2_body, tm),
        grid_spec=grid_spec,
        out_shape=jax.ShapeDtypeStruct((n, d), F32),
        compiler_params=_cparams("arbitrary"),
        name="post2",
    )(dest, h, rt, p2d, yb, wple, wpg, bpg, npl)


def _row_tile(n, cap=512):
    t = min(cap, n)
    assert n % t == 0 and t % 8 == 0
    return t


def kernel(x_prompt, x_sample, p_prompt, p_sample, cache_k_cmp, cache_v_cmp, cache_k_slc, cache_v_slc, cache_k_win, cache_v_win, state_conv, page_table, w_in, w_out, conv_w, norm_mix, norm_ffn, norm_ple, q_norm, k_norm, cmp_pe_k, cmp_w1_k, cmp_w2_k, cmp_pe_v, cmp_w1_v, cmp_w2_v, rel_bias, w_router_group, b_router_group, w_router_expert, b_router_expert, w_exp_gate, w_exp_up, w_exp_down, w_ple, w_ple_gate, b_ple_gate):
    assert w_in.shape[0] == 1, "single-layer step"
    bp, t, d = x_prompt.shape
    bs, ts, _ = x_sample.shape
    n_pages = page_table.shape[1]
    page = cache_k_cmp.shape[2]
    past_len = n_pages * page
    w_buf = cache_k_win.shape[2]
    n_phys = cache_k_cmp.shape[1]
    assert t % Q_BLOCK == 0 and t >= WINDOW + Q_BLOCK and page % CMP_STRIDE == 0 and ts == 4 and d == D_MODEL
    assert past_len % SLC_BLOCK == 0
    np_rows, ns_rows = bp * t, bs * ts

    row = lambda v: v.reshape(1, -1).astype(F32)
    w_in_b = jnp.pad(w_in[0], ((0, 0), (0, Z_COLS - w_in.shape[2]))).astype(BF16)
    qn = row(jnp.tile(q_norm[0], N_HEADS))
    kn1 = row(jnp.tile(k_norm[0, 1], N_KV_HEADS))
    kn2 = row(jnp.tile(k_norm[0, 2], N_KV_HEADS))
    bd = jnp.asarray(np.kron(np.eye(N_HEADS), np.ones((HEAD_DIM, HEAD_DIM))), BF16)
    pw = (row(norm_mix[0]), w_in_b, qn, kn1, kn2, conv_w[0].astype(F32), bd)
    cw_k = _compress_weights(cmp_pe_k[0], cmp_w1_k[0], cmp_w2_k[0])
    cw_v = _compress_weights(cmp_pe_v[0], cmp_w1_v[0], cmp_w2_v[0])
    kn0 = row(k_norm[0, 0])
    fext = _bias_by_distance(rel_bias)
    pt_flat = page_table.reshape(-1).astype(jnp.int32)
    wr = jnp.zeros((d, LANES), F32).at[:, :N_EXPERTS].set(w_router_expert[0])
    wr = wr.at[:, ROUTER_GROUP_LANE:ROUTER_GROUP_LANE + N_GROUPS].set(w_router_group[0])
    br = jnp.zeros((1, LANES), F32).at[0, :N_EXPERTS].set(b_router_expert[0])
    br = br.at[0, ROUTER_GROUP_LANE:ROUTER_GROUP_LANE + N_GROUPS].set(b_router_group[0])
    wo_b = w_out[0].astype(BF16)
    wple_b = w_ple[0].astype(BF16)
    wpg_b = w_ple_gate[0].astype(BF16)

    tm_p = _row_tile(t, cap=1024)
    (co_p, q_p, kw_p, vw_p, gt_p, cs_p, ksb, vsb, kwb, vwb, kc_t, vc_t, ks_t, vs_t, kc_x, vc_x) = _project(
        x_prompt.reshape(np_rows, d), bp, t, tm_p, pw)
    chunk_w = CMP_STRIDE * KV_DIM
    kcmp_p = _compress_rows(kc_x.reshape(bp, t // CMP_STRIDE, chunk_w), cw_k, kn0, True)
    vcmp_p = _compress_rows(vc_x.reshape(bp, t // CMP_STRIDE, chunk_w), cw_v, kn0, False)
    at_p = _attn_prompt(q_p, gt_p, kcmp_p, vcmp_p, ksb, vsb, kwb, vwb, fext, bp, t)

    st = state_conv[0].astype(F32)
    s0 = jnp.repeat(st[:, 0], ts, axis=0)
    s1 = jnp.repeat(st[:, 1], ts, axis=0)
    (co_s, q_s, kc_s, vc_s, ks_s, vs_s, kw_s, vw_s, gt_s, u_s) = _project(
        x_sample.reshape(ns_rows, d), bs, ts, ns_rows, pw, state=(s0, s1))
    feature_major = lambda c, n, rows_: jnp.transpose(c[0], (0, 2, 3, 1)).reshape(n, KV_DIM, rows_)
    kcmp_s = _compress_pages(feature_major(cache_k_cmp, n_phys, page), pt_flat, bs, n_pages, cw_k, kn0, True)
    vcmp_s = _compress_pages(feature_major(cache_v_cmp, n_phys, page), pt_flat, bs, n_pages, cw_v, kn0, False)
    at_s = _attn_sample(q_s, gt_s, kcmp_s, vcmp_s, ks_s, vs_s, kw_s, vw_s,
                        feature_major(cache_k_slc, n_phys, page), feature_major(cache_v_slc, n_phys, page),
                        feature_major(cache_k_win, bs, w_buf), feature_major(cache_v_win, bs, w_buf),
                        pt_flat, fext, bs, ts, n_pages, page)

    tp1 = _row_tile(np_rows)
    ts1 = _row_tile(ns_rows)
    nf = row(norm_ffn[0])
    h_p, hn_p, rt_p, cnt_p = _post1(x_prompt.reshape(np_rows, d), co_p, at_p, wo_b, nf, wr, br,
                                    jnp.zeros((8, LANES), F32), tp1)
    h_s, hn_s, rt_s, cnt_s = _post1(x_sample.reshape(ns_rows, d), co_s, at_s, wo_b, nf, wr, br, cnt_p, ts1)

    counts = cnt_s[0, :N_EXPERTS].astype(jnp.int32)
    padded = (counts + EXPERT_ROWS - 1) // EXPERT_ROWS * EXPERT_ROWS
    pad_end = jnp.cumsum(padded)
    pad_start = pad_end - padded
    n_assign = 2 * (np_rows + ns_rows)
    n_blk = (n_assign + N_EXPERTS * (EXPERT_ROWS - 1) + EXPERT_ROWS - 1) // EXPERT_ROWS
    blk_first = jnp.arange(n_blk, dtype=jnp.int32) * EXPERT_ROWS
    blk_expert = jnp.minimum(jnp.sum((pad_end[None, :] <= blk_first[:, None]).astype(jnp.int32), axis=1),
                             N_EXPERTS - 1)
    n_used = (pad_end[-1:] // EXPERT_ROWS).astype(jnp.int32)

    def dest_of(rt):
        e = rt[:, 0:2].astype(jnp.int32).reshape(-1)
        return pad_start[e] + rt[:, 2:4].astype(jnp.int32).reshape(-1)

    dest_p = dest_of(rt_p)
    dest_s = dest_of(rt_s)

    xs = jnp.zeros((n_blk * EXPERT_ROWS * TOKEN_TILE_ROWS, LANES), F32)
    xs = _scatter_rows(dest_p, hn_p, xs)
    xs = _scatter_rows(dest_s, hn_s, xs)
    yb = _experts(blk_expert, n_used, xs, w_exp_gate[0], w_exp_up[0], w_exp_down[0])

    bpg = row(b_ple_gate[0])
    npl = row(norm_ple[0])
    y_p = _post2(dest_p, h_p, rt_p, p_prompt[0].reshape(np_rows, -1), yb, wple_b, wpg_b, bpg, npl, tp1)
    y_s = _post2(dest_s, h_s, rt_s, p_sample[0].reshape(ns_rows, -1), yb, wple_b, wpg_b, bpg, npl, ts1)

    kv5 = lambda a, b, s: a.reshape(1, b, s, N_KV_HEADS, HEAD_DIM)
    wp = min(WINDOW, t)
    win_p = lambda a: kv5(a.reshape(bp, t, KV_DIM)[:, t - wp:], bp, wp)
    win_s = lambda c, new: jnp.concatenate([c[0], new.reshape(bs, ts, N_KV_HEADS, HEAD_DIM)], axis=1)[None, :, ts:]
    conv_p = cs_p[:, 8 - (CONV_K - 1):][None]
    conv_s = u_s.reshape(bs, ts, CONV_DIM)[:, ts - (CONV_K - 1):][None]
    from_t = lambda a: jnp.transpose(a.reshape(bp, N_KV_HEADS, HEAD_DIM, t), (0, 3, 1, 2))[None]
    return (y_p.reshape(bp, t, d), y_s.reshape(bs, ts, d),
            from_t(kc_t), from_t(vc_t), from_t(ks_t), from_t(vs_t), win_p(kw_p), win_p(vw_p), conv_p,
            kv5(kc_s, bs, ts), kv5(vc_s, bs, ts), kv5(ks_s, bs, ts), kv5(vs_s, bs, ts),
            win_s(cache_k_win, kw_s), win_s(cache_v_win, vw_s), conv_s)
```

```python
import functools
import math

import numpy as np
import jax
import jax.numpy as jnp
from jax import lax
from jax.experimental import pallas as pl
from jax.experimental.pallas import tpu as pltpu

F32 = jnp.float32
BF16 = jnp.bfloat16
NEG_INF = float("-inf")
MASKED = -1e30

HEAD_DIM = 64
N_HEADS = 8
N_KV_HEADS = 2
GROUP = N_HEADS // N_KV_HEADS
CONV_DIM = 512
ATTN_DIM = 512
KV_DIM = N_KV_HEADS * HEAD_DIM
N_BRANCH = 3
CONV_K = 3
CMP_BLOCK = 32
CMP_STRIDE = 16
CMP_HIDDEN = 256
SLC_BLOCK = 64
N_SEL = 16
WINDOW = 512
Q_BLOCK = 128
FORCE_SCORE = 1e4
NUM_BUCKETS = 32
MAX_DISTANCE = 128
N_GROUPS = 4
EXPERTS_PER_GROUP = 8
N_EXPERTS = N_GROUPS * EXPERTS_PER_GROUP
D_EXPERT = 512
EPS = 1e-6

D_MODEL = 1024
LANES = 128
TOKEN_TILE_ROWS = D_MODEL // LANES
Z_COLS = 3 * CONV_DIM + ATTN_DIM + 6 * KV_DIM + LANES
BIAS_DMAX = 768
EXPERT_ROWS = 512
ROUTER_GROUP_LANE = 32
SCATTER_TOKENS = 256
DMA_ISSUE_UNROLL = 8
COMPRESS_SEQS = 4
CHUNK_PITCH = 24
SAMPLE_SEQS = 4
VMEM_LIMIT = 56 * 1024 * 1024


def _cparams(*sem):
    return pltpu.CompilerParams(dimension_semantics=sem, vmem_limit_bytes=VMEM_LIMIT)


def _dot(a, b):
    return jnp.dot(a, b, preferred_element_type=F32)


def _dot_nt(a, b):
    return lax.dot_general(a, b, (((1,), (1,)), ((), ())), preferred_element_type=F32)


def _split3(x):
    hi = x.astype(BF16)
    r = x - hi.astype(F32)
    mid = r.astype(BF16)
    lo = (r - mid.astype(F32)).astype(BF16)
    return hi, mid, lo


def _rms(x, g):
    return x * lax.rsqrt(jnp.mean(x * x, axis=-1, keepdims=True) + EPS) * g


def _head_rms(x, bd, g):
    hi, mid, _ = _split3(x * x)
    ss = _dot(hi, bd) + _dot(mid, bd)
    return x * lax.rsqrt(ss * (1.0 / HEAD_DIM) + EPS) * g


def _sigmoid(x):
    return 1.0 / (1.0 + jnp.exp(-x))


def _store_token_tiles(ref, x, n):
    r = x.shape[1] // LANES
    for j in range(r):
        ref[pl.ds(j, n, stride=r), :] = x[:, j * LANES:(j + 1) * LANES]


def _load_token_tiles(ref, n, r, first=0):
    return jnp.concatenate([ref[pl.ds(first + j, n, stride=r), :] for j in range(r)], axis=-1)


def _softmax_parts(s):
    m = jnp.max(s, axis=-1, keepdims=True)
    m = jnp.where(m == NEG_INF, 0.0, m)
    p = jnp.exp(s - m)
    l = jnp.sum(p, axis=-1, keepdims=True)
    return p, l


def _proj_body(sample, tm, *refs):
    if sample:
        (x_ref, nm_ref, w_ref, qn_ref, kn1_ref, kn2_ref, cw_ref, bd_ref, s0_ref, s1_ref,
         co_ref, q_ref, kc_ref, vc_ref, ks_ref, vs_ref, kw_ref, vw_ref, gt_ref, u_ref) = refs
    else:
        (x_ref, nm_ref, w_ref, qn_ref, kn1_ref, kn2_ref, cw_ref, bd_ref,
         co_ref, q_ref, kw_ref, vw_ref, gt_ref, cs_ref,
         ksb_ref, vsb_ref, kwb_ref, vwb_ref, kct_ref, vct_ref, kst_ref, vst_ref, kcx_ref, vcx_ref,
         carry_ref, stage_ref) = refs

    xn = _rms(x_ref[...], nm_ref[...]).astype(BF16)

    def seg(a, b):
        return _dot(xn, w_ref[:, a:b])

    c3 = 3 * CONV_DIM
    u = seg(2 * CONV_DIM, c3) * seg(0, CONV_DIM)
    bg = seg(CONV_DIM, 2 * CONV_DIM)
    row = lax.broadcasted_iota(jnp.int32, (tm, 1), 0)
    um1 = pltpu.roll(u, 1, axis=0)
    um2 = pltpu.roll(u, 2, axis=0)
    if sample:
        r = row & 3
        s0 = s0_ref[...]
        s1 = s1_ref[...]
        prev1 = jnp.where(r == 0, s1, um1)
        prev2 = jnp.where(r == 0, s0, jnp.where(r == 1, s1, um2))
        u_ref[...] = u
    else:
        @pl.when(pl.program_id(1) == 0)
        def _():
            carry_ref[...] = jnp.zeros_like(carry_ref)
        c = carry_ref[...]
        prev1 = jnp.where(row == 0, c[7:8], um1)
        prev2 = jnp.where(row == 0, c[6:7], jnp.where(row == 1, c[7:8], um2))
        carry_ref[...] = u[tm - 8:tm]
        cs_ref[0] = u[tm - 8:tm]
    cw = cw_ref[...]
    y = cw[0:1] * prev2 + cw[1:2] * prev1 + cw[2:3] * u
    co_ref[...] = (bg * y).astype(BF16)

    bd = bd_ref[...]
    q = _head_rms(seg(c3, c3 + ATTN_DIM), bd, qn_ref[...]) * (HEAD_DIM ** -0.5)
    lane = lax.broadcasted_iota(jnp.int32, (tm, LANES), 1)
    low = lane < HEAD_DIM

    def head_planes(x, fill):
        return [jnp.where(low, x if h == 0 else pltpu.roll(x, HEAD_DIM, axis=1), fill) for h in range(N_KV_HEADS)]

    if sample:
        q_ref[...] = q
    else:
        for hd in range(N_HEADS):
            pair = q[:, (hd // 2) * LANES:(hd // 2 + 1) * LANES]
            if hd % 2:
                pair = pltpu.roll(pair, HEAD_DIM, axis=1)
            q_ref[0, hd] = jnp.where(low, pair, 0.0).astype(BF16)

    k0 = c3 + ATTN_DIM
    bdk = bd[:KV_DIM, :KV_DIM]
    kc = seg(k0, k0 + KV_DIM)
    vc = seg(k0 + KV_DIM, k0 + 2 * KV_DIM)
    ks = _head_rms(seg(k0 + 2 * KV_DIM, k0 + 3 * KV_DIM), bdk, kn1_ref[...])
    vs = seg(k0 + 3 * KV_DIM, k0 + 4 * KV_DIM)
    kw = _head_rms(seg(k0 + 4 * KV_DIM, k0 + 5 * KV_DIM), bdk, kn2_ref[...])
    vw = seg(k0 + 5 * KV_DIM, k0 + 6 * KV_DIM)
    kw_ref[...] = kw
    vw_ref[...] = vw
    if sample:
        kc_ref[...] = kc
        vc_ref[...] = vc
        ks_ref[...] = ks
        vs_ref[...] = vs
    else:
        kct_ref[0] = kc.T
        vct_ref[0] = vc.T
        kst_ref[0] = ks.T
        vst_ref[0] = vs.T
        for src, dst in ((kc, kcx_ref), (vc, vcx_ref)):
            stage_ref[...] = src
            dst[...] = jnp.concatenate([stage_ref[pl.ds(r, tm // CMP_STRIDE, stride=CMP_STRIDE), :]
                                        for r in range(CMP_STRIDE)], axis=-1)
    gates = _sigmoid(seg(k0 + 6 * KV_DIM, k0 + 6 * KV_DIM + LANES))
    if sample:
        gt_ref[...] = gates
    else:
        pos = pl.program_id(1) * tm + lax.broadcasted_iota(jnp.int32, (tm, LANES), 0)
        block_onehot = jnp.where(lane - HEAD_DIM == (pos >> 6), 1.0, 0.0)
        for h, (a, b, c, e) in enumerate(zip(head_planes(ks, block_onehot), head_planes(vs, 1.0),
                                             head_planes(kw, 0.0), head_planes(vw, 1.0))):
            ksb_ref[h] = a.astype(BF16)
            vsb_ref[h] = b.astype(BF16)
            kwb_ref[h] = c.astype(BF16)
            vwb_ref[h] = e.astype(BF16)
        gt_ref[0] = gates
        gt_ref[1] = pltpu.roll(gates, LANES - GROUP * N_BRANCH, axis=1)


def _project(x2d, batch, seq, tm, weights, state=None):
    n, d = x2d.shape
    sample = state is not None
    nt = seq // tm if not sample else 1
    const = lambda shape: pl.BlockSpec(shape, lambda b, t: (0,) * len(shape))
    rows = lambda w: pl.BlockSpec((tm, w), lambda b, t: (b * nt + t, 0))
    nm, w_in, qn, kn1, kn2, cw, bd = weights
    in_specs = [rows(d), const(nm.shape), const(w_in.shape), const(qn.shape), const(kn1.shape),
                const(kn2.shape), const(cw.shape), const(bd.shape)]
    args = [x2d, nm, w_in, qn, kn1, kn2, cw, bd]
    kv_f32 = [jax.ShapeDtypeStruct((n, KV_DIM), F32)] * 6
    if sample:
        in_specs += [rows(CONV_DIM), rows(CONV_DIM)]
        args += list(state)
        out_shape = ([jax.ShapeDtypeStruct((n, CONV_DIM), BF16), jax.ShapeDtypeStruct((n, ATTN_DIM), F32)]
                     + kv_f32 + [jax.ShapeDtypeStruct((n, LANES), F32), jax.ShapeDtypeStruct((n, CONV_DIM), F32)])
        out_specs = [rows(CONV_DIM), rows(ATTN_DIM)] + [rows(KV_DIM)] * 6 + [rows(LANES), rows(CONV_DIM)]
        scratch = []
        grid = (1, 1)
    else:
        planes = pl.BlockSpec((N_KV_HEADS, tm, LANES), lambda b, t: (0, b * nt + t, 0))
        out_shape = ([jax.ShapeDtypeStruct((n, CONV_DIM), BF16),
                      jax.ShapeDtypeStruct((batch, N_HEADS, seq, LANES), BF16)]
                     + kv_f32[:2] + [jax.ShapeDtypeStruct((N_KV_HEADS, n, LANES), F32),
                                     jax.ShapeDtypeStruct((batch, 8, CONV_DIM), F32)]
                     + [jax.ShapeDtypeStruct((N_KV_HEADS, n, LANES), BF16)] * 4
                     + [jax.ShapeDtypeStruct((batch, KV_DIM, seq), F32)] * 4
                     + [jax.ShapeDtypeStruct((n // CMP_STRIDE, CMP_STRIDE * KV_DIM), F32)] * 2)
        out_specs = ([rows(CONV_DIM), pl.BlockSpec((1, N_HEADS, tm, LANES), lambda b, t: (b, 0, t, 0))]
                     + [rows(KV_DIM)] * 2 + [planes, pl.BlockSpec((1, 8, CONV_DIM), lambda b, t: (b, 0, 0))]
                     + [planes] * 4 + [pl.BlockSpec((1, KV_DIM, tm), lambda b, t: (b, 0, t))] * 4
                     + [pl.BlockSpec((tm // CMP_STRIDE, CMP_STRIDE * KV_DIM), lambda b, t: (b * nt + t, 0))] * 2)
        scratch = [pltpu.VMEM((8, CONV_DIM), F32), pltpu.VMEM((tm, KV_DIM), F32)]
        grid = (batch, nt)
    return pl.pallas_call(
        functools.partial(_proj_body, sample, tm),
        grid=grid, in_specs=in_specs, out_specs=out_specs, out_shape=out_shape, scratch_shapes=scratch,
        compiler_params=_cparams("arbitrary", "arbitrary"),
        name="proj_sample" if sample else "proj_prompt",
    )(*args)


def _gelu_tanh(x):
    cdf = 0.5 * (1.0 + jnp.tanh(math.sqrt(2.0 / math.pi) * (x + 0.044715 * (x * x * x))))
    return x * cdf


def _compress_core(norm, x, pe_ref, we_ref, w2_ref, g_ref):
    n = x.shape[0]
    a0 = _dot((x + pe_ref[0:1]).astype(BF16), we_ref[0])
    a1 = _dot((x + pe_ref[1:2]).astype(BF16), we_ref[1])
    hid = a0 + pltpu.roll(a1, n - 1, axis=0)
    w2 = w2_ref[...]
    outs = []
    for h in range(N_KV_HEADS):
        act = _gelu_tanh(hid[:, h * CMP_HIDDEN:(h + 1) * CMP_HIDDEN])
        o = _dot(act.astype(BF16), w2)
        if norm:
            o = _rms(o, g_ref[...])
        outs.append(o)
    return outs


def _compress_rows_body(norm, x_ref, pe_ref, we_ref, w2_ref, g_ref, o_ref):
    outs = _compress_core(norm, x_ref[0], pe_ref, we_ref, w2_ref, g_ref)
    for h in range(N_KV_HEADS):
        o_ref[0, h] = jnp.concatenate([outs[h], jnp.zeros_like(outs[h])], axis=-1).astype(BF16)


def _fetch_pages(pt_ref, cache_ref, buf_ref, sem_ref, step, slot, n_ops, priority):
    base = step * n_ops

    def issue(j, _):
        pltpu.make_async_copy(cache_ref.at[pt_ref[base + j]], buf_ref.at[slot, j], sem_ref.at[slot]).start(
            priority=priority)
        return 0

    lax.fori_loop(0, n_ops, issue, 0)


def _paged_prefetch(pt_ref, caches, bufs, sems, n_ops):
    i = pl.program_id(0)
    slot = i & 1
    for k, (cache_ref, buf_ref, sem_ref) in enumerate(zip(caches, bufs, sems)):
        @pl.when(i == 0)
        def _():
            _fetch_pages(pt_ref, cache_ref, buf_ref, sem_ref, 0, 0, n_ops, k % 2)

        @pl.when(i + 1 < pl.num_programs(0))
        def _():
            _fetch_pages(pt_ref, cache_ref, buf_ref, sem_ref, i + 1, 1 - slot, n_ops, k % 2)

    for cache_ref, buf_ref, sem_ref in zip(caches, bufs, sems):
        pltpu.make_async_copy(cache_ref.at[pl.ds(0, n_ops)], buf_ref.at[slot], sem_ref.at[slot]).wait()
    return slot


def _compress_pages_body(norm, nch, n_seq, n_pages, pt_ref, cache_ref, pe_ref, we_ref, w2_ref, g_ref, o_ref,
                         rows_ref, buf_ref, sem_ref):
    n_ops = n_seq * n_pages
    slot = _paged_prefetch(pt_ref, [cache_ref], [buf_ref], [sem_ref], n_ops)
    page = buf_ref.shape[3]
    per_page = page // CMP_STRIDE
    for j in range(n_ops):
        rows = buf_ref[slot, j].T
        for c in range(per_page):
            first = (j * per_page + c) * CHUNK_PITCH
            rows_ref[first:first + CMP_STRIDE, :] = rows[c * CMP_STRIDE:(c + 1) * CMP_STRIDE]
    x = jnp.concatenate([rows_ref[pl.ds(r, n_seq * nch, stride=CHUNK_PITCH), :] for r in range(CMP_STRIDE)], axis=-1)
    out = jnp.concatenate(_compress_core(norm, x, pe_ref, we_ref, w2_ref, g_ref), axis=-1)
    for b in range(n_seq):
        o_ref[b] = out[b * nch:(b + 1) * nch]


def _compress_weights(pe, w1, w2):
    w1r = w1.reshape(2, CMP_STRIDE, HEAD_DIM, CMP_HIDDEN).astype(BF16)
    z = jnp.zeros_like(w1r)
    we = jnp.stack([jnp.concatenate([w1r, z], axis=-1), jnp.concatenate([z, w1r], axis=-1)], axis=2)
    we = we.reshape(2, CMP_STRIDE * KV_DIM, N_KV_HEADS * CMP_HIDDEN)
    per = pe.reshape(2, CMP_STRIDE, 1, HEAD_DIM)
    pex = jnp.broadcast_to(per, (2, CMP_STRIDE, N_KV_HEADS, HEAD_DIM)).reshape(2, CMP_STRIDE * KV_DIM)
    return pex.astype(F32), we.astype(BF16), w2.astype(BF16)


def _compress_rows(rows3, cw, gain, norm):
    b, nch, width = rows3.shape
    pex, we, w2 = cw
    const = lambda a: pl.BlockSpec(a.shape, lambda i: (0,) * a.ndim)
    return pl.pallas_call(
        functools.partial(_compress_rows_body, norm),
        grid=(b,),
        in_specs=[pl.BlockSpec((1, nch, width), lambda i: (i, 0, 0)), const(pex), const(we), const(w2), const(gain)],
        out_specs=pl.BlockSpec((1, N_KV_HEADS, nch, LANES), lambda i: (i, 0, 0, 0)),
        out_shape=jax.ShapeDtypeStruct((b, N_KV_HEADS, nch, LANES), BF16),
        compiler_params=_cparams("arbitrary"),
        name="compress_rows",
    )(rows3, pex, we, w2, gain)


def _compress_pages(cache_t, pt_flat, n_batch, n_pages, cw, gain, norm):
    _, _, page = cache_t.shape
    nch = n_pages * page // CMP_STRIDE
    pex, we, w2 = cw
    n_seq = COMPRESS_SEQS if n_batch % COMPRESS_SEQS == 0 else 1
    const = lambda a: pl.BlockSpec(a.shape, lambda i, pt: (0,) * a.ndim)
    n_ops = n_seq * n_pages
    grid_spec = pltpu.PrefetchScalarGridSpec(
        num_scalar_prefetch=1, grid=(n_batch // n_seq,),
        in_specs=[pl.BlockSpec(memory_space=pl.ANY), const(pex), const(we), const(w2), const(gain)],
        out_specs=pl.BlockSpec((n_seq, nch, KV_DIM), lambda i, pt: (i, 0, 0)),
        scratch_shapes=[pltpu.VMEM((n_ops * (page // CMP_STRIDE) * CHUNK_PITCH, KV_DIM), F32),
                        pltpu.VMEM((2, n_ops, KV_DIM, page), F32),
                        pltpu.SemaphoreType.DMA((2,))])
    return pl.pallas_call(
        functools.partial(_compress_pages_body, norm, nch, n_seq, n_pages),
        grid_spec=grid_spec,
        out_shape=jax.ShapeDtypeStruct((n_batch, nch, KV_DIM), F32),
        compiler_params=_cparams("arbitrary"),
        name="compress_pages",
    )(pt_flat, cache_t, pex, we, w2, gain)


def _rel_bucket(dist):
    n = jnp.maximum(dist, 0)
    max_exact = NUM_BUCKETS // 2
    nf = jnp.maximum(n, 1).astype(F32)
    large = max_exact + (jnp.log(nf / max_exact) / math.log(MAX_DISTANCE / max_exact)
                         * (NUM_BUCKETS - max_exact)).astype(jnp.int32)
    large = jnp.minimum(large, NUM_BUCKETS - 1)
    return jnp.where(n < max_exact, n, large)


def _bias_by_distance(rel_bias):
    d = jnp.arange(BIAS_DMAX, dtype=jnp.int32)
    f = rel_bias.astype(F32)[_rel_bucket(d)]
    f = (f - f[BIAS_DMAX - 1:BIAS_DMAX]).T
    return jnp.concatenate([f, jnp.full((N_HEADS, 1), NEG_INF, F32)], axis=1)


def _bias_index(d, valid):
    return np.where(valid, np.clip(d, 0, BIAS_DMAX - 1), BIAS_DMAX).astype(np.int32)


def _bias_table(fext, d, valid):
    return jnp.take(fext, jnp.asarray(_bias_index(d, valid)), axis=1)


def _toeplitz_body(n_rows, v_ref, o_ref):
    x = jnp.broadcast_to(v_ref[0], (n_rows, v_ref.shape[2]))
    o_ref[0] = pltpu.roll(x, 0, axis=1, stride=1, stride_axis=0)


def _toeplitz_rows(v, n_rows):
    h, w = v.shape
    return pl.pallas_call(
        functools.partial(_toeplitz_body, n_rows),
        grid=(h,),
        in_specs=[pl.BlockSpec((1, 1, w), lambda i: (i, 0, 0))],
        out_specs=pl.BlockSpec((1, n_rows, w), lambda i: (i, 0, 0)),
        out_shape=jax.ShapeDtypeStruct((h, n_rows, w), F32),
        compiler_params=_cparams("arbitrary"),
        name="toeplitz_rows",
    )(v.reshape(h, 1, w))


def _select_blocks(imp_t, srow, qpos, n_rank):
    qblk = qpos >> 6
    forced = (srow == 0) | (srow == qblk) | (srow == qblk - 1)
    valid = (srow << 6) <= qpos
    imp_t = jnp.where(valid, imp_t + jnp.where(forced, FORCE_SCORE, 0.0), NEG_INF)
    n_rows = imp_t.shape[0]
    assert n_rows % 8 == 0
    slabs = [imp_t[a:a + 8] for a in range(0, n_rows, 8)]
    rows8 = [srow[a:a + 8] for a in range(0, n_rows, 8)]
    cnts = [jnp.zeros(x.shape, jnp.int32) for x in slabs]
    for s in range(n_rank):
        r = imp_t[s:s + 1, :]
        for j, x in enumerate(slabs):
            if 8 * j > s:
                beats = r >= x
            elif 8 * j + 7 <= s:
                beats = r > x
            else:
                beats = (r > x) | ((r == x) & (rows8[j] > s))
            cnts[j] = cnts[j] + jnp.where(beats, 1, 0)
    cnt = jnp.concatenate(cnts, axis=0)
    return jnp.where((cnt < N_SEL) & valid, 1.0, 0.0)


def _exp_pv(s, m, v):
    return _dot(jnp.exp((s - m).astype(BF16)), v)


def _normalize_pv(pv):
    return pv / jnp.maximum(pv[:, HEAD_DIM:HEAD_DIM + 1], 1e-30)


def _attn_prompt_body(kt, n_slc, q_ref, gt_ref, kcmp_ref, vcmp_ref, ks_ref, vs_ref, kw_ref, vw_ref,
                      ctab_ref, ntab_ref, wtab_ref, ovl_ref, o_ref, sa_ref, sb_ref):
    i = pl.program_id(2)
    qb = Q_BLOCK
    rows = GROUP * qb
    n_cmp_pad = kcmp_ref.shape[2]
    first_near_block = (qb // SLC_BLOCK) * jnp.maximum(i - 1, 0)
    near_start = pl.multiple_of(jnp.maximum(i - 1, 0) * qb, qb)
    win_start = pl.multiple_of(jnp.maximum(i * qb - WINDOW, 0), qb)
    band = WINDOW + qb
    gates = gt_ref[0]
    lane = lax.broadcasted_iota(jnp.int32, (qb, LANES), 1)
    q0 = q_ref[0].reshape(rows, LANES)

    s = _dot_nt(q0, kw_ref[0, pl.ds(win_start, band), :]) + wtab_ref[0].reshape(rows, band)
    m = jnp.max(s, axis=-1, keepdims=True)
    m = jnp.where(m == NEG_INF, 0.0, m)
    o_w = _normalize_pv(_exp_pv(s, m, vw_ref[0, pl.ds(win_start, band), :]))

    per_qb = qb // CMP_STRIDE
    ctab = pltpu.roll(ctab_ref[...].reshape(rows, 2 * n_cmp_pad), i * per_qb, axis=1)[:, n_cmp_pad:]
    s = _dot_nt(q0, kcmp_ref[0, 0]) + ctab
    p, l = _softmax_parts(s)
    pn = p / jnp.maximum(l, 1e-30)
    o_c = _dot(pn.astype(BF16), vcmp_ref[0, 0])

    psum = pn[0:qb] + pn[qb:2 * qb] + pn[2 * qb:3 * qb] + pn[3 * qb:4 * qb]
    hi, mid, lo = _split3(psum)
    ovl = ovl_ref[...]
    imp_t = _dot_nt(ovl, hi) + _dot_nt(ovl, mid) + _dot_nt(ovl, lo)
    srow = lax.broadcasted_iota(jnp.int32, (n_slc, qb), 0)
    qpos_t = i * qb + lax.broadcasted_iota(jnp.int32, (n_slc, qb), 1)
    sel_t = _select_blocks(imp_t, srow, qpos_t, n_slc)
    sel_t = jnp.concatenate([sel_t, jnp.zeros((LANES - n_slc, qb), F32)], axis=0)
    sel = sel_t.T

    def query_with_mask(keep):
        m = pltpu.roll(jnp.where(keep, 0.0, MASKED), HEAD_DIM, axis=1).astype(BF16)
        m = jnp.concatenate([jnp.where(lane < HEAD_DIM, q0[g * qb:(g + 1) * qb], m) for g in range(GROUP)], axis=0)
        return m

    q_near = query_with_mask(sel > 0.5)
    q_far = query_with_mask((sel > 0.5) & (lane < first_near_block))

    s = _dot_nt(q_near, ks_ref[0, pl.ds(near_start, 2 * qb), :]) + ntab_ref[0].reshape(rows, 2 * qb)
    m0 = jnp.max(s, axis=-1, keepdims=True)
    m0 = jnp.where(m0 == NEG_INF, 0.0, m0)
    a0 = _exp_pv(s, m0, vs_ref[0, pl.ds(near_start, 2 * qb), :])

    n_kt = ks_ref.shape[1] // kt

    def scores(t):
        k0 = pl.multiple_of(jnp.minimum(t, n_kt - 1) * kt, kt)
        return _dot_nt(q_far, ks_ref[0, pl.ds(k0, kt), :])

    def consume(t, s, m_old, acc):
        k0 = pl.multiple_of(jnp.minimum(t, n_kt - 1) * kt, kt)
        m_new = jnp.maximum(m_old, jnp.max(s, axis=-1, keepdims=True))
        return m_new, jnp.exp(m_old - m_new) * acc + _exp_pv(s, m_new, vs_ref[0, pl.ds(k0, kt), :])

    def far_pair(u, carry):
        m, acc = carry
        sb_ref[...] = scores(2 * u + 1)
        m, acc = consume(2 * u, sa_ref[...], m, acc)
        sa_ref[...] = scores(2 * u + 2)
        return consume(2 * u + 1, sb_ref[...], m, acc)

    n_far = (near_start + kt - 1) // kt
    sa_ref[...] = scores(0)
    _, acc_s = lax.fori_loop(0, (n_far + 1) // 2, far_pair, (m0, a0))
    o_s = _normalize_pv(acc_s)

    heads_out = []
    for g in range(GROUP):
        c = g * N_BRANCH
        sl = slice(g * qb, (g + 1) * qb)
        heads_out.append(gates[:, c:c + 1] * o_c[sl] + gates[:, c + 1:c + 2] * o_s[sl] + gates[:, c + 2:c + 3] * o_w[sl])
    tiles = [jnp.where(lane < HEAD_DIM, heads_out[2 * j], pltpu.roll(heads_out[2 * j + 1], HEAD_DIM, axis=1))
             for j in range(GROUP // 2)]
    o_ref[...] = jnp.concatenate(tiles, axis=-1).astype(BF16)


def _attn_prompt(qp, gates, kcmp, vcmp, ksb, vsb, kwb, vwb, fext, batch, seq):
    qb = Q_BLOCK
    nqb = seq // qb
    n_slc = seq // SLC_BLOCK
    assert n_slc <= LANES - HEAD_DIM
    n_cmp_pad = kcmp.shape[2]
    n_cmp = n_cmp_pad - 1
    kt = min(512, seq)
    assert (seq // kt) % 2 == 0
    band = WINDOW + qb
    iq = np.arange(qb)

    per_qb = qb // CMP_STRIDE
    half = 2 * per_qb
    m = np.arange(-half, half)
    d = iq[:, None] - (m[None, :] * CMP_STRIDE + CMP_BLOCK - 1)
    assert d[:, 0].min() >= MAX_DISTANCE and d[:, -1].max() < 0
    ctab = jnp.concatenate([jnp.zeros((N_HEADS, qb, n_cmp_pad - half), F32), _bias_table(fext, d, d >= 0),
                            jnp.full((N_HEADS, qb, n_cmp_pad - half), NEG_INF, F32)], axis=2)
    assert per_qb * (nqb - 1) < n_cmp_pad
    assert (n_cmp_pad - 1) * CMP_STRIDE + CMP_BLOCK - 1 >= seq and n_cmp == n_cmp_pad - 1

    nv = WINDOW // qb + 1
    kw_ = WINDOW + band
    w = qb + kw_
    assert w % LANES == 0
    m = np.arange(w)
    m = np.where(m < kw_, m, m - w)
    dj = WINDOW - m
    wide = _toeplitz_rows(_bias_table(fext, dj, (dj >= 0) & (dj < WINDOW)), qb)
    wtab = jnp.stack([wide[:, :, WINDOW - qb * v:WINDOW - qb * v + band] for v in range(nv)], axis=0)
    ntab = jnp.stack([wide[:, :, WINDOW - qb * v:WINDOW - qb * v + 2 * qb] for v in range(2)], axis=0)
    assert 2 * qb <= WINDOW

    c_start = np.arange(n_cmp_pad) * CMP_STRIDE
    s_start = np.arange(n_slc) * SLC_BLOCK
    ovl = ((c_start[None, :] < s_start[:, None] + SLC_BLOCK) & (c_start[None, :] + CMP_BLOCK > s_start[:, None])
           & (np.arange(n_cmp_pad) < n_cmp)[None, :])
    ovl = jnp.asarray(ovl, BF16)

    per_head = lambda: pl.BlockSpec((1, seq, LANES), lambda b, h, i: (h, b, 0))
    return pl.pallas_call(
        functools.partial(_attn_prompt_body, kt, n_slc),
        grid=(batch, N_KV_HEADS, nqb),
        in_specs=[pl.BlockSpec((1, GROUP, qb, LANES), lambda b, h, i: (b, h, i, 0)),
                  pl.BlockSpec((1, qb, LANES), lambda b, h, i: (h, b * nqb + i, 0)),
                  pl.BlockSpec((1, 1, n_cmp_pad, LANES), lambda b, h, i: (b, h, 0, 0)),
                  pl.BlockSpec((1, 1, n_cmp_pad, LANES), lambda b, h, i: (b, h, 0, 0)),
                  per_head(), per_head(), per_head(), per_head(),
                  pl.BlockSpec((GROUP, qb, 2 * n_cmp_pad), lambda b, h, i: (h, 0, 0)),
                  pl.BlockSpec((1, GROUP, qb, 2 * qb), lambda b, h, i: (jnp.minimum(i, 1), h, 0, 0)),
                  pl.BlockSpec((1, GROUP, qb, band), lambda b, h, i: (jnp.minimum(i, nv - 1), h, 0, 0)),
                  pl.BlockSpec(ovl.shape, lambda b, h, i: (0, 0))],
        out_specs=pl.BlockSpec((qb, GROUP * HEAD_DIM), lambda b, h, i: (b * nqb + i, h)),
        out_shape=jax.ShapeDtypeStruct((batch * seq, ATTN_DIM), BF16),
        scratch_shapes=[pltpu.VMEM((GROUP * qb, kt), F32)] * 2,
        compiler_params=_cparams("arbitrary", "arbitrary", "arbitrary"),
        name="attn_prompt",
    )(qp, gates, kcmp, vcmp, ksb, vsb, kwb, vwb, ctab, ntab, wtab, ovl)


def _attn_sample_body(n_seq, n_pages, ts, past_len, pt_ref, cache_k_ref, cache_v_ref, *refs):
    kbuf, vbuf, ksem, vsem = refs[-4:]
    n_ops = n_seq * n_pages
    slot = _paged_prefetch(pt_ref, [cache_k_ref, cache_v_ref], [kbuf, vbuf], [ksem, vsem], n_ops)
    for b in range(n_seq):
        _attn_sample_one(b, [kbuf.at[slot, b * n_pages + j] for j in range(n_pages)],
                         [vbuf.at[slot, b * n_pages + j] for j in range(n_pages)], refs[:-4], ts, past_len)


def _attn_sample_one(b, kpages, vpages, refs, ts, past_len):
    (q_ref, gt_ref, kcmp_ref, vcmp_ref, ksn_ref, vsn_ref, kwc_ref, vwc_ref, kwn_ref, vwn_ref,
     ctab_ref, stab_ref, sntab_ref, wtab_ref, wntab_ref, ovl_ref, o_ref) = refs
    rows = GROUP * N_KV_HEADS * ts
    rq = N_KV_HEADS * ts
    q = q_ref[b]
    gates = gt_ref[b]

    s = _dot_nt(q, kcmp_ref[b].astype(BF16)) + ctab_ref[...]
    p, l = _softmax_parts(s)
    pn = p / jnp.maximum(l, 1e-30)
    o_c = _dot(pn.astype(BF16), vcmp_ref[b].astype(BF16))

    psum = pn[0:rq]
    for g in range(1, GROUP):
        psum = psum + pn[g * rq:(g + 1) * rq]
    hi, mid, lo = _split3(psum)
    ovl = ovl_ref[...]
    imp = _dot(hi, ovl) + _dot(mid, ovl) + _dot(lo, ovl)
    n_slc = -(-(past_len + ts) // SLC_BLOCK)
    blk = lax.broadcasted_iota(jnp.int32, (rq, LANES), 1)
    qpos = past_len + (lax.broadcasted_iota(jnp.int32, (rq, LANES), 0) & (ts - 1))
    qblk = qpos >> 6
    forced = (blk == 0) | (blk == qblk) | (blk == qblk - 1)
    valid = ((blk << 6) <= qpos) & (blk < n_slc)
    imp = jnp.where(valid, imp + jnp.where(forced, FORCE_SCORE, 0.0), NEG_INF)
    cnt = jnp.zeros((rq, LANES), jnp.int32)
    for sidx in range(n_slc):
        r = imp[:, sidx:sidx + 1]
        beats = (r > imp) | ((r == imp) & (blk > sidx))
        cnt = cnt + jnp.where(beats, 1, 0)
    sel = jnp.where((cnt < N_SEL) & valid, 1.0, 0.0)
    sel = jnp.concatenate([sel] * GROUP, axis=0)

    kc_t = jnp.concatenate([p_[...] for p_ in kpages], axis=1).astype(BF16)
    vc_t = jnp.concatenate([p_[...] for p_ in vpages], axis=1).astype(BF16)
    lane_r = lax.broadcasted_iota(jnp.int32, (rows, LANES), 1)
    per_tile = LANES // SLC_BLOCK
    tiles = []
    for j in range(past_len // LANES):
        m = sel[:, per_tile * j:per_tile * j + 1]
        for c in range(1, per_tile):
            m = jnp.where(lane_r >= c * SLC_BLOCK, sel[:, per_tile * j + c:per_tile * j + c + 1], m)
        tiles.append(m)
    mexp = jnp.concatenate(tiles, axis=1)
    s1 = jnp.where(mexp > 0.5, _dot(q, kc_t) + stab_ref[...], NEG_INF)
    last = sel[:, n_slc - 1:n_slc]
    s2 = jnp.where(last > 0.5, _dot_nt(q, ksn_ref[b].astype(BF16)) + sntab_ref[...], NEG_INF)
    m = jnp.maximum(jnp.max(s1, axis=-1, keepdims=True), jnp.max(s2, axis=-1, keepdims=True))
    m = jnp.where(m == NEG_INF, 0.0, m)
    p1 = jnp.exp(s1 - m)
    p2 = jnp.exp(s2 - m)
    l = jnp.sum(p1, axis=-1, keepdims=True) + jnp.sum(p2, axis=-1, keepdims=True)
    o_s = (_dot_nt(p1.astype(BF16), vc_t) + _dot(p2.astype(BF16), vsn_ref[b].astype(BF16))) / jnp.maximum(l, 1e-30)

    s1 = _dot(q, kwc_ref[b].astype(BF16)) + wtab_ref[...]
    s2 = _dot_nt(q, kwn_ref[b].astype(BF16)) + wntab_ref[...]
    m = jnp.maximum(jnp.max(s1, axis=-1, keepdims=True), jnp.max(s2, axis=-1, keepdims=True))
    m = jnp.where(m == NEG_INF, 0.0, m)
    p1 = jnp.exp(s1 - m)
    p2 = jnp.exp(s2 - m)
    l = jnp.sum(p1, axis=-1, keepdims=True) + jnp.sum(p2, axis=-1, keepdims=True)
    o_w = (_dot_nt(p1.astype(BF16), vwc_ref[b].astype(BF16))
           + _dot(p2.astype(BF16), vwn_ref[b].astype(BF16))) / jnp.maximum(l, 1e-30)

    o_ref[b] = gates[:, 0:1] * o_c + gates[:, 1:2] * o_s + gates[:, 2:3] * o_w


def _attn_sample(q_s, gates_s, kcmp, vcmp, ks_new, vs_new, kw_new, vw_new, cache_ks, cache_vs,
                 cache_kw, cache_vw, pt_flat, fext, n_batch, ts, n_pages, page):
    past_len = n_pages * page
    w_buf = cache_kw.shape[2]
    rows = GROUP * N_KV_HEADS * ts
    n_new = 8
    n_cmp_pad = kcmp.shape[1]
    n_cmp = n_cmp_pad - 1
    n_slc = -(-(past_len + ts) // SLC_BLOCK)

    q5 = q_s.reshape(n_batch, ts, N_KV_HEADS, GROUP, HEAD_DIM).transpose(0, 3, 2, 1, 4)
    eye = jnp.eye(N_KV_HEADS, dtype=q_s.dtype)
    qr = jnp.einsum("bghtd,hk->bghtkd", q5, eye).reshape(n_batch, rows, LANES).astype(BF16)
    g5 = gates_s[:, :N_HEADS * N_BRANCH].reshape(n_batch, ts, N_KV_HEADS, GROUP, N_BRANCH).transpose(0, 3, 2, 1, 4)
    gr = jnp.pad(g5.reshape(n_batch, rows, N_BRANCH), ((0, 0), (0, 0), (0, LANES - N_BRANCH)))
    pad_new = lambda a: jnp.pad(a.reshape(n_batch, ts, KV_DIM), ((0, 0), (0, n_new - ts), (0, 0)))
    ks_new, vs_new, kw_new, vw_new = map(pad_new, (ks_new, vs_new, kw_new, vw_new))

    g_i, h_i, t_i = np.meshgrid(np.arange(GROUP), np.arange(N_KV_HEADS), np.arange(ts), indexing="ij")
    head = (h_i * GROUP + g_i).reshape(rows)
    tq = t_i.reshape(rows)
    pos_q = past_len + tq

    f_rows = fext[jnp.asarray(head)]

    def table(d, valid):
        return jnp.take_along_axis(f_rows, jnp.asarray(_bias_index(d, valid)), axis=1)

    nn = np.arange(n_cmp_pad)
    d = pos_q[:, None] - (nn[None, :] * CMP_STRIDE + CMP_BLOCK - 1)
    ctab = table(d, (d >= 0) & (nn < n_cmp)[None, :])
    near = np.arange(past_len - MAX_DISTANCE, past_len)
    d = pos_q[:, None] - near[None, :]
    assert past_len >= MAX_DISTANCE and d.min() >= 0
    stab = jnp.concatenate([jnp.zeros((rows, past_len - MAX_DISTANCE), F32), table(d, d >= 0)], axis=1)
    jn = np.arange(n_new)
    d = tq[:, None] - jn[None, :]
    sntab = table(d, (d >= 0) & (jn < ts)[None, :])
    pos_w = past_len - w_buf + np.arange(w_buf)
    d = pos_q[:, None] - pos_w[None, :]
    wtab = table(d, (d >= 0) & (d < WINDOW) & (pos_w >= 0)[None, :])
    wntab = table(tq[:, None] - jn[None, :], (tq[:, None] >= jn[None, :]) & (jn < ts)[None, :])

    c_start = nn * CMP_STRIDE
    s_start = np.arange(LANES) * SLC_BLOCK
    ovl = jnp.asarray((c_start[:, None] < s_start[None, :] + SLC_BLOCK) & (c_start[:, None] + CMP_BLOCK > s_start[None, :])
                      & (nn < n_cmp)[:, None] & (np.arange(LANES) < n_slc)[None, :], BF16)

    n_seq = SAMPLE_SEQS if n_batch % SAMPLE_SEQS == 0 else 1
    n_ops = n_seq * n_pages
    const = lambda a: pl.BlockSpec(a.shape, lambda b, pt: (0,) * a.ndim)
    per_b = lambda a: pl.BlockSpec((n_seq,) + a.shape[1:], lambda b, pt: (b,) + (0,) * (a.ndim - 1))
    any_spec = pl.BlockSpec(memory_space=pl.ANY)
    small = [qr, gr, kcmp, vcmp, ks_new, vs_new, cache_kw, cache_vw, kw_new, vw_new]
    consts = [ctab, stab, sntab, wtab, wntab, ovl]
    page_buf = pltpu.VMEM((2, n_ops, KV_DIM, page), F32)
    grid_spec = pltpu.PrefetchScalarGridSpec(
        num_scalar_prefetch=1, grid=(n_batch // n_seq,),
        in_specs=[any_spec, any_spec] + [per_b(a) for a in small] + [const(a) for a in consts],
        out_specs=pl.BlockSpec((n_seq, rows, LANES), lambda b, pt: (b, 0, 0)),
        scratch_shapes=[page_buf, page_buf, pltpu.SemaphoreType.DMA((2,)), pltpu.SemaphoreType.DMA((2,))])
    o = pl.pallas_call(
        functools.partial(_attn_sample_body, n_seq, n_pages, ts, past_len),
        grid_spec=grid_spec,
        out_shape=jax.ShapeDtypeStruct((n_batch, rows, LANES), F32),
        compiler_params=_cparams("arbitrary"),
        name="attn_sample",
    )(pt_flat, cache_ks, cache_vs, *small, *consts)
    o6 = o.reshape(n_batch, GROUP, N_KV_HEADS, ts, N_KV_HEADS, HEAD_DIM)
    o5 = jnp.stack([o6[:, :, h, :, h] for h in range(N_KV_HEADS)], axis=2)
    return o5.transpose(0, 3, 2, 1, 4).reshape(n_batch * ts, ATTN_DIM).astype(BF16)


def _post1_body(tm, x_ref, co_ref, at_ref, wo_ref, nf_ref, wr_ref, br_ref, tri_ref, run0_ref,
                h_ref, hn_ref, rt_ref, cnt_ref, run_ref):
    @pl.when(pl.program_id(0) == 0)
    def _():
        run_ref[...] = run0_ref[...]

    h = x_ref[...] + _dot(co_ref[...], wo_ref[0:CONV_DIM]) + _dot(at_ref[...], wo_ref[CONV_DIM:CONV_DIM + ATTN_DIM])
    hn = _rms(h, nf_ref[...])
    h_ref[...] = h
    _store_token_tiles(hn_ref, hn, tm)

    hi = hn.astype(BF16)
    lo = (hn - hi.astype(F32)).astype(BF16)
    wr = wr_ref[...]
    whi = wr.astype(BF16)
    wlo = (wr - whi.astype(F32)).astype(BF16)
    logits = _dot(hi, whi) + _dot(lo, whi) + _dot(hi, wlo) + br_ref[...]

    lane_i = lax.broadcasted_iota(jnp.int32, (tm, LANES), 1)
    lane = lane_i.astype(F32)
    big = float(LANES)
    gmask = (lane_i >= ROUTER_GROUP_LANE) & (lane_i < ROUTER_GROUP_LANE + N_GROUPS)
    lg = jnp.where(gmask, logits, NEG_INF)
    eg = jnp.exp(lg - jnp.max(lg, axis=-1, keepdims=True))
    pg = eg / jnp.sum(eg, axis=-1, keepdims=True)
    gw = jnp.max(pg, axis=-1, keepdims=True)
    grp = jnp.min(jnp.where(gmask & (pg == gw), lane, big), axis=-1, keepdims=True) - ROUTER_GROUP_LANE

    group_of_lane = (lane_i >> 3).astype(F32)
    emask = (lane_i < N_EXPERTS) & (group_of_lane == grp)
    le = jnp.where(emask, logits, NEG_INF)
    ee = jnp.exp(le - jnp.max(le, axis=-1, keepdims=True))
    pe = jnp.where(emask, ee / jnp.sum(ee, axis=-1, keepdims=True), -1.0)
    v1 = jnp.max(pe, axis=-1, keepdims=True)
    i1 = jnp.min(jnp.where(pe == v1, lane, big), axis=-1, keepdims=True)
    pe2 = jnp.where(lane == i1, -1.0, pe)
    v2 = jnp.max(pe2, axis=-1, keepdims=True)
    i2 = jnp.min(jnp.where(pe2 == v2, lane, big), axis=-1, keepdims=True)
    tot = v1 + v2
    w1 = v1 / tot * gw
    w2 = v2 / tot * gw

    oh1 = jnp.where(lane == i1, 1.0, 0.0)
    oh2 = jnp.where(lane == i2, 1.0, 0.0)
    both = oh1 + oh2
    before = _dot(tri_ref[...], both.astype(BF16)) + run_ref[0:1]
    r1 = jnp.sum(oh1 * before, axis=-1, keepdims=True)
    r2 = jnp.sum(oh2 * before, axis=-1, keepdims=True)
    run = run_ref[0:1] + jnp.sum(both, axis=0, keepdims=True)
    run_ref[...] = jnp.broadcast_to(run, run_ref.shape)
    cnt_ref[...] = jnp.broadcast_to(run, cnt_ref.shape)

    rt = jnp.where(lane_i == 0, i1, 0.0)
    rt = jnp.where(lane_i == 1, i2, rt)
    rt = jnp.where(lane_i == 2, r1, rt)
    rt = jnp.where(lane_i == 3, r2, rt)
    rt = jnp.where(lane_i == 4, w1, rt)
    rt = jnp.where(lane_i == 5, w2, rt)
    rt_ref[...] = rt


def _post1(x2d, co, at, wo, nf, wr, br, run0, tm):
    n, d = x2d.shape
    tri = jnp.asarray(np.tril(np.ones((tm, tm), np.float32), -1), BF16)
    rows = lambda w: pl.BlockSpec((tm, w), lambda i: (i, 0))
    const = lambda a: pl.BlockSpec(a.shape, lambda i: (0,) * a.ndim)
    return pl.pallas_call(
        functools.partial(_post1_body, tm),
        grid=(n // tm,),
        in_specs=[rows(d), rows(CONV_DIM), rows(ATTN_DIM), const(wo), const(nf), const(wr), const(br),
                  const(tri), const(run0)],
        out_specs=[rows(d), pl.BlockSpec((tm * TOKEN_TILE_ROWS, LANES), lambda i: (i, 0)), rows(LANES),
                   pl.BlockSpec((8, LANES), lambda i: (0, 0))],
        out_shape=[jax.ShapeDtypeStruct((n, d), F32), jax.ShapeDtypeStruct((n * TOKEN_TILE_ROWS, LANES), F32),
                   jax.ShapeDtypeStruct((n, LANES), F32), jax.ShapeDtypeStruct((8, LANES), F32)],
        scratch_shapes=[pltpu.VMEM((8, LANES), F32)],
        compiler_params=_cparams("arbitrary"),
        name="post1",
    )(x2d, co, at, wo, nf, wr, br, tri, run0)


def _token_copy(src_ref, dst_ref, s, d, sem):
    r = TOKEN_TILE_ROWS
    return pltpu.make_async_copy(src_ref.at[pl.ds(pl.multiple_of(s * r, r), r)],
                                 dst_ref.at[pl.ds(pl.multiple_of(d * r, r), r)], sem)


def _scatter_rows_body(ts, dest_ref, src_ref, init_ref, out_ref, sem):
    del init_ref
    base = pl.program_id(0) * (2 * ts)

    def issue(t, _):
        for k in range(2):
            _token_copy(src_ref, out_ref, t, dest_ref[base + 2 * t + k], sem).start(priority=k)
        return 0

    lax.fori_loop(0, ts, issue, 0, unroll=DMA_ISSUE_UNROLL)
    for _ in range(2):
        pltpu.make_async_copy(src_ref, out_ref.at[pl.ds(0, ts * TOKEN_TILE_ROWS)], sem).wait()


def _scatter_rows(dest, src, slots):
    n_tok = dest.shape[0] // 2
    ts = min(SCATTER_TOKENS, n_tok)
    assert n_tok % ts == 0
    any_spec = pl.BlockSpec(memory_space=pl.ANY)
    return pl.pallas_call(
        functools.partial(_scatter_rows_body, ts),
        grid_spec=pltpu.PrefetchScalarGridSpec(
            num_scalar_prefetch=1, grid=(n_tok // ts,),
            in_specs=[pl.BlockSpec((ts * TOKEN_TILE_ROWS, LANES), lambda i, dest: (i, 0)), any_spec],
            out_specs=any_spec, scratch_shapes=[pltpu.SemaphoreType.DMA(())]),
        out_shape=jax.ShapeDtypeStruct(slots.shape, slots.dtype),
        input_output_aliases={2: 0},
        compiler_params=pltpu.CompilerParams(dimension_semantics=("arbitrary",)),
        name="scatter_rows",
    )(dest, src, slots)


def _experts_body(be_ref, nu_ref, x_ref, wg_ref, wu_ref, wd_ref, o_ref, wg_s, wu_s, wd_s):
    i = pl.program_id(0)

    @pl.when(i < nu_ref[0])
    def _():
        prev = be_ref[jnp.maximum(i - 1, 0)]

        @pl.when((i == 0) | (be_ref[i] != prev))
        def _():
            wg_s[...] = wg_ref[0].astype(BF16)
            wu_s[...] = wu_ref[0].astype(BF16)
            wd_s[...] = wd_ref[0].astype(BF16)

        x = _load_token_tiles(x_ref, EXPERT_ROWS, TOKEN_TILE_ROWS).astype(BF16)
        g = _dot(x, wg_s[...])
        u = _dot(x, wu_s[...])
        a = g * _sigmoid(g) * u
        _store_token_tiles(o_ref, _dot(a.astype(BF16), wd_s[...]), EXPERT_ROWS)

    @pl.when(i >= nu_ref[0])
    def _():
        o_ref[...] = jnp.zeros_like(o_ref)


def _experts(blk_expert, n_used, xs, wg, wu, wd):
    blk_rows = EXPERT_ROWS * TOKEN_TILE_ROWS
    n_blk = xs.shape[0] // blk_rows
    _, d, de = wg.shape
    xmap = lambda i, be, nu: (jnp.minimum(i, jnp.maximum(nu[0] - 1, 0)), 0)
    wmap = lambda i, be, nu: (be[jnp.minimum(i, jnp.maximum(nu[0] - 1, 0))], 0, 0)
    grid_spec = pltpu.PrefetchScalarGridSpec(
        num_scalar_prefetch=2, grid=(n_blk,),
        in_specs=[pl.BlockSpec((blk_rows, LANES), xmap), pl.BlockSpec((1, d, de), wmap),
                  pl.BlockSpec((1, d, de), wmap), pl.BlockSpec((1, de, d), wmap)],
        out_specs=pl.BlockSpec((blk_rows, LANES), lambda i, be, nu: (i, 0)),
        scratch_shapes=[pltpu.VMEM((d, de), BF16), pltpu.VMEM((d, de), BF16), pltpu.VMEM((de, d), BF16)])
    return pl.pallas_call(
        _experts_body, grid_spec=grid_spec,
        out_shape=jax.ShapeDtypeStruct(xs.shape, F32),
        compiler_params=_cparams("arbitrary"),
        name="experts",
    )(blk_expert, n_used, xs, wg, wu, wd)


def _post2_body(tm, dest_ref, h_ref, rt_ref, p_ref, yb_ref, wple_ref, wpg_ref, bpg_ref, np_ref, o_ref, buf, sem):
    i = pl.program_id(0)
    n = pl.num_programs(0)

    def fetch(step, slot):
        base = step * (2 * tm)

        def issue(t, _):
            for k in range(2):
                _token_copy(yb_ref, buf.at[slot], dest_ref[base + 2 * t + k], k * tm + t, sem.at[slot]).start(priority=k)
            return 0

        lax.fori_loop(0, tm, issue, 0, unroll=DMA_ISSUE_UNROLL)

    @pl.when(i == 0)
    def _():
        fetch(0, 0)

    @pl.when(i + 1 < n)
    def _():
        fetch(i + 1, (i + 1) & 1)

    slot = i & 1

    pltpu.make_async_copy(yb_ref.at[pl.ds(0, 2 * tm * TOKEN_TILE_ROWS)], buf.at[slot], sem.at[slot]).wait()
    rt = rt_ref[...]
    y0 = _load_token_tiles(buf.at[slot], tm, TOKEN_TILE_ROWS)
    y1 = _load_token_tiles(buf.at[slot], tm, TOKEN_TILE_ROWS, first=tm * TOKEN_TILE_ROWS)
    h = h_ref[...] + (y0 * rt[:, 4:5] + y1 * rt[:, 5:6])
    gate = _sigmoid(_dot(_rms(h, np_ref[...]).astype(BF16), wpg_ref[...]) + bpg_ref[...])
    o_ref[...] = h + gate * _dot(p_ref[...].astype(BF16), wple_ref[...])


def _post2(dest, h, rt, p2d, yb, wple, wpg, bpg, npl, tm):
    n, d = h.shape
    rows = lambda w: pl.BlockSpec((tm, w), lambda i, dest: (i, 0))
    const = lambda a: pl.BlockSpec(a.shape, lambda i, dest: (0,) * a.ndim)
    grid_spec = pltpu.PrefetchScalarGridSpec(
        num_scalar_prefetch=1, grid=(n // tm,),
        in_specs=[rows(d), rows(LANES), rows(p2d.shape[1]), pl.BlockSpec(memory_space=pl.ANY), const(wple),
                  const(wpg), const(bpg), const(npl)],
        out_specs=rows(d),
        scratch_shapes=[pltpu.VMEM((2, 2 * tm * TOKEN_TILE_ROWS, LANES), F32), pltpu.SemaphoreType.DMA((2,))])
    return pl.pallas_call(
        functools.partial(_post2_body, tm),
        grid_spec=grid_spec,
        out_shape=jax.ShapeDtypeStruct((n, d), F32),
        compiler_params=_cparams("arbitrary"),
        name="post2",
    )(dest, h, rt, p2d, yb, wple, wpg, bpg, npl)


def _row_tile(n, cap=512):
    t = min(cap, n)
    assert n % t == 0 and t % 8 == 0
    return t


def kernel(x_prompt, x_sample, p_prompt, p_sample, cache_k_cmp, cache_v_cmp, cache_k_slc, cache_v_slc, cache_k_win, cache_v_win, state_conv, page_table, w_in, w_out, conv_w, norm_mix, norm_ffn, norm_ple, q_norm, k_norm, cmp_pe_k, cmp_w1_k, cmp_w2_k, cmp_pe_v, cmp_w1_v, cmp_w2_v, rel_bias, w_router_group, b_router_group, w_router_expert, b_router_expert, w_exp_gate, w_exp_up, w_exp_down, w_ple, w_ple_gate, b_ple_gate):
    assert w_in.shape[0] == 1, "single-layer step"
    bp, t, d = x_prompt.shape
    bs, ts, _ = x_sample.shape
    n_pages = page_table.shape[1]
    page = cache_k_cmp.shape[2]
    past_len = n_pages * page
    w_buf = cache_k_win.shape[2]
    n_phys = cache_k_cmp.shape[1]
    assert t % Q_BLOCK == 0 and t >= WINDOW + Q_BLOCK and page % CMP_STRIDE == 0 and ts == 4 and d == D_MODEL
    assert past_len % SLC_BLOCK == 0
    np_rows, ns_rows = bp * t, bs * ts

    row = lambda v: v.reshape(1, -1).astype(F32)
    w_in_b = jnp.pad(w_in[0], ((0, 0), (0, Z_COLS - w_in.shape[2]))).astype(BF16)
    qn = row(jnp.tile(q_norm[0], N_HEADS))
    kn1 = row(jnp.tile(k_norm[0, 1], N_KV_HEADS))
    kn2 = row(jnp.tile(k_norm[0, 2], N_KV_HEADS))
    bd = jnp.asarray(np.kron(np.eye(N_HEADS), np.ones((HEAD_DIM, HEAD_DIM))), BF16)
    pw = (row(norm_mix[0]), w_in_b, qn, kn1, kn2, conv_w[0].astype(F32), bd)
    cw_k = _compress_weights(cmp_pe_k[0], cmp_w1_k[0], cmp_w2_k[0])
    cw_v = _compress_weights(cmp_pe_v[0], cmp_w1_v[0], cmp_w2_v[0])
    kn0 = row(k_norm[0, 0])
    fext = _bias_by_distance(rel_bias)
    pt_flat = page_table.reshape(-1).astype(jnp.int32)
    wr = jnp.zeros((d, LANES), F32).at[:, :N_EXPERTS].set(w_router_expert[0])
    wr = wr.at[:, ROUTER_GROUP_LANE:ROUTER_GROUP_LANE + N_GROUPS].set(w_router_group[0])
    br = jnp.zeros((1, LANES), F32).at[0, :N_EXPERTS].set(b_router_expert[0])
    br = br.at[0, ROUTER_GROUP_LANE:ROUTER_GROUP_LANE + N_GROUPS].set(b_router_group[0])
    wo_b = w_out[0].astype(BF16)
    wple_b = w_ple[0].astype(BF16)
    wpg_b = w_ple_gate[0].astype(BF16)

    tm_p = _row_tile(t, cap=1024)
    (co_p, q_p, kw_p, vw_p, gt_p, cs_p, ksb, vsb, kwb, vwb, kc_t, vc_t, ks_t, vs_t, kc_x, vc_x) = _project(
        x_prompt.reshape(np_rows, d), bp, t, tm_p, pw)
    chunk_w = CMP_STRIDE * KV_DIM
    kcmp_p = _compress_rows(kc_x.reshape(bp, t // CMP_STRIDE, chunk_w), cw_k, kn0, True)
    vcmp_p = _compress_rows(vc_x.reshape(bp, t // CMP_STRIDE, chunk_w), cw_v, kn0, False)
    at_p = _attn_prompt(q_p, gt_p, kcmp_p, vcmp_p, ksb, vsb, kwb, vwb, fext, bp, t)

    st = state_conv[0].astype(F32)
    s0 = jnp.repeat(st[:, 0], ts, axis=0)
    s1 = jnp.repeat(st[:, 1], ts, axis=0)
    (co_s, q_s, kc_s, vc_s, ks_s, vs_s, kw_s, vw_s, gt_s, u_s) = _project(
        x_sample.reshape(ns_rows, d), bs, ts, ns_rows, pw, state=(s0, s1))
    feature_major = lambda c, n, rows_: jnp.transpose(c[0], (0, 2, 3, 1)).reshape(n, KV_DIM, rows_)
    kcmp_s = _compress_pages(feature_major(cache_k_cmp, n_phys, page), pt_flat, bs, n_pages, cw_k, kn0, True)
    vcmp_s = _compress_pages(feature_major(cache_v_cmp, n_phys, page), pt_flat, bs, n_pages, cw_v, kn0, False)
    at_s = _attn_sample(q_s, gt_s, kcmp_s, vcmp_s, ks_s, vs_s, kw_s, vw_s,
                        feature_major(cache_k_slc, n_phys, page), feature_major(cache_v_slc, n_phys, page),
                        feature_major(cache_k_win, bs, w_buf), feature_major(cache_v_win, bs, w_buf),
                        pt_flat, fext, bs, ts, n_pages, page)

    tp1 = _row_tile(np_rows)
    ts1 = _row_tile(ns_rows)
    nf = row(norm_ffn[0])
    h_p, hn_p, rt_p, cnt_p = _post1(x_prompt.reshape(np_rows, d), co_p, at_p, wo_b, nf, wr, br,
                                    jnp.zeros((8, LANES), F32), tp1)
    h_s, hn_s, rt_s, cnt_s = _post1(x_sample.reshape(ns_rows, d), co_s, at_s, wo_b, nf, wr, br, cnt_p, ts1)

    counts = cnt_s[0, :N_EXPERTS].astype(jnp.int32)
    padded = (counts + EXPERT_ROWS - 1) // EXPERT_ROWS * EXPERT_ROWS
    pad_end = jnp.cumsum(padded)
    pad_start = pad_end - padded
    n_assign = 2 * (np_rows + ns_rows)
    n_blk = (n_assign + N_EXPERTS * (EXPERT_ROWS - 1) + EXPERT_ROWS - 1) // EXPERT_ROWS
    blk_first = jnp.arange(n_blk, dtype=jnp.int32) * EXPERT_ROWS
    blk_expert = jnp.minimum(jnp.sum((pad_end[None, :] <= blk_first[:, None]).astype(jnp.int32), axis=1),
                             N_EXPERTS - 1)
    n_used = (pad_end[-1:] // EXPERT_ROWS).astype(jnp.int32)

    def dest_of(rt):
        e = rt[:, 0:2].astype(jnp.int32).reshape(-1)
        return pad_start[e] + rt[:, 2:4].astype(jnp.int32).reshape(-1)

    dest_p = dest_of(rt_p)
    dest_s = dest_of(rt_s)

    xs = jnp.zeros((n_blk * EXPERT_ROWS * TOKEN_TILE_ROWS, LANES), F32)
    xs = _scatter_rows(dest_p, hn_p, xs)
    xs = _scatter_rows(dest_s, hn_s, xs)
    yb = _experts(blk_expert, n_used, xs, w_exp_gate[0], w_exp_up[0], w_exp_down[0])

    bpg = row(b_ple_gate[0])
    npl = row(norm_ple[0])
    y_p = _post2(dest_p, h_p, rt_p, p_prompt[0].reshape(np_rows, -1), yb, wple_b, wpg_b, bpg, npl, tp1)
    y_s = _post2(dest_s, h_s, rt_s, p_sample[0].reshape(ns_rows, -1), yb, wple_b, wpg_b, bpg, npl, ts1)

    kv5 = lambda a, b, s: a.reshape(1, b, s, N_KV_HEADS, HEAD_DIM)
    wp = min(WINDOW, t)
    win_p = lambda a: kv5(a.reshape(bp, t, KV_DIM)[:, t - wp:], bp, wp)
    win_s = lambda c, new: jnp.concatenate([c[0], new.reshape(bs, ts, N_KV_HEADS, HEAD_DIM)], axis=1)[None, :, ts:]
    conv_p = cs_p[:, 8 - (CONV_K - 1):][None]
    conv_s = u_s.reshape(bs, ts, CONV_DIM)[:, ts - (CONV_K - 1):][None]
    from_t = lambda a: jnp.transpose(a.reshape(bp, N_KV_HEADS, HEAD_DIM, t), (0, 3, 1, 2))[None]
    return (y_p.reshape(bp, t, d), y_s.reshape(bs, ts, d),
            from_t(kc_t), from_t(vc_t), from_t(ks_t), from_t(vs_t), win_p(kw_p), win_p(vw_p), conv_p,
            kv5(kc_s, bs, ts), kv5(vc_s, bs, ts), kv5(ks_s, bs, ts), kv5(vs_s, bs, ts),
            win_s(cache_k_win, kw_s), win_s(cache_v_win, vw_s), conv_s)
```

```python
import functools
import math

import numpy as np
import jax
import jax.numpy as jnp
from jax import lax
from jax.experimental import pallas as pl
from jax.experimental.pallas import tpu as pltpu

F32 = jnp.float32
BF16 = jnp.bfloat16
NEG_INF = float("-inf")
MASKED = -1e30

HEAD_DIM = 64
N_HEADS = 8
N_KV_HEADS = 2
GROUP = N_HEADS // N_KV_HEADS
CONV_DIM = 512
ATTN_DIM = 512
KV_DIM = N_KV_HEADS * HEAD_DIM
N_BRANCH = 3
CONV_K = 3
CMP_BLOCK = 32
CMP_STRIDE = 16
CMP_HIDDEN = 256
SLC_BLOCK = 64
N_SEL = 16
WINDOW = 512
Q_BLOCK = 128
FORCE_SCORE = 1e4
NUM_BUCKETS = 32
MAX_DISTANCE = 128
N_GROUPS = 4
EXPERTS_PER_GROUP = 8
N_EXPERTS = N_GROUPS * EXPERTS_PER_GROUP
D_EXPERT = 512
EPS = 1e-6

D_MODEL = 1024
LANES = 128
TOKEN_TILE_ROWS = D_MODEL // LANES
Z_COLS = 3 * CONV_DIM + ATTN_DIM + 6 * KV_DIM + LANES
BIAS_DMAX = 768
EXPERT_ROWS = 512
ROUTER_GROUP_LANE = 32
SCATTER_TOKENS = 512
DMA_ISSUE_UNROLL = 8
COMPRESS_SEQS = 4
CHUNK_PITCH = 24
SAMPLE_SEQS = 4
VMEM_LIMIT = 56 * 1024 * 1024


def _cparams(*sem):
    return pltpu.CompilerParams(dimension_semantics=sem, vmem_limit_bytes=VMEM_LIMIT)


def _dot(a, b):
    return jnp.dot(a, b, preferred_element_type=F32)


def _dot_nt(a, b):
    return lax.dot_general(a, b, (((1,), (1,)), ((), ())), preferred_element_type=F32)


def _split3(x):
    hi = x.astype(BF16)
    r = x - hi.astype(F32)
    mid = r.astype(BF16)
    lo = (r - mid.astype(F32)).astype(BF16)
    return hi, mid, lo


def _rms(x, g):
    return x * lax.rsqrt(jnp.mean(x * x, axis=-1, keepdims=True) + EPS) * g


def _head_rms(x, bd, g):
    hi, mid, _ = _split3(x * x)
    ss = _dot(hi, bd) + _dot(mid, bd)
    return x * lax.rsqrt(ss * (1.0 / HEAD_DIM) + EPS) * g


def _sigmoid(x):
    return 1.0 / (1.0 + jnp.exp(-x))


def _store_token_tiles(ref, x, n):
    r = x.shape[1] // LANES
    for j in range(r):
        ref[pl.ds(j, n, stride=r), :] = x[:, j * LANES:(j + 1) * LANES]


def _load_token_tiles(ref, n, r, first=0):
    return jnp.concatenate([ref[pl.ds(first + j, n, stride=r), :] for j in range(r)], axis=-1)


def _softmax_parts(s):
    m = jnp.max(s, axis=-1, keepdims=True)
    m = jnp.where(m == NEG_INF, 0.0, m)
    p = jnp.exp(s - m)
    l = jnp.sum(p, axis=-1, keepdims=True)
    return p, l


def _proj_body(sample, tm, *refs):
    if sample:
        (x_ref, nm_ref, w_ref, qn_ref, kn1_ref, kn2_ref, cw_ref, bd_ref, s0_ref, s1_ref,
         co_ref, q_ref, kc_ref, vc_ref, ks_ref, vs_ref, kw_ref, vw_ref, gt_ref, u_ref) = refs
    else:
        (x_ref, nm_ref, w_ref, qn_ref, kn1_ref, kn2_ref, cw_ref, bd_ref,
         co_ref, q_ref, kw_ref, vw_ref, gt_ref, cs_ref,
         ksb_ref, vsb_ref, kwb_ref, vwb_ref, kct_ref, vct_ref, kst_ref, vst_ref, kcx_ref, vcx_ref,
         carry_ref, stage_ref) = refs

    xn = _rms(x_ref[...], nm_ref[...]).astype(BF16)

    def seg(a, b):
        return _dot(xn, w_ref[:, a:b])

    c3 = 3 * CONV_DIM
    u = seg(2 * CONV_DIM, c3) * seg(0, CONV_DIM)
    bg = seg(CONV_DIM, 2 * CONV_DIM)
    row = lax.broadcasted_iota(jnp.int32, (tm, 1), 0)
    um1 = pltpu.roll(u, 1, axis=0)
    um2 = pltpu.roll(u, 2, axis=0)
    if sample:
        r = row & 3
        s0 = s0_ref[...]
        s1 = s1_ref[...]
        prev1 = jnp.where(r == 0, s1, um1)
        prev2 = jnp.where(r == 0, s0, jnp.where(r == 1, s1, um2))
        u_ref[...] = u
    else:
        @pl.when(pl.program_id(1) == 0)
        def _():
            carry_ref[...] = jnp.zeros_like(carry_ref)
        c = carry_ref[...]
        prev1 = jnp.where(row == 0, c[7:8], um1)
        prev2 = jnp.where(row == 0, c[6:7], jnp.where(row == 1, c[7:8], um2))
        carry_ref[...] = u[tm - 8:tm]
        cs_ref[0] = u[tm - 8:tm]
    cw = cw_ref[...]
    y = cw[0:1] * prev2 + cw[1:2] * prev1 + cw[2:3] * u
    co_ref[...] = (bg * y).astype(BF16)

    bd = bd_ref[...]
    q = _head_rms(seg(c3, c3 + ATTN_DIM), bd, qn_ref[...]) * (HEAD_DIM ** -0.5)
    lane = lax.broadcasted_iota(jnp.int32, (tm, LANES), 1)
    low = lane < HEAD_DIM

    def head_planes(x, fill):
        return [jnp.where(low, x if h == 0 else pltpu.roll(x, HEAD_DIM, axis=1), fill) for h in range(N_KV_HEADS)]

    if sample:
        q_ref[...] = q
    else:
        for hd in range(N_HEADS):
            pair = q[:, (hd // 2) * LANES:(hd // 2 + 1) * LANES]
            if hd % 2:
                pair = pltpu.roll(pair, HEAD_DIM, axis=1)
            q_ref[0, hd] = jnp.where(low, pair, 0.0).astype(BF16)

    k0 = c3 + ATTN_DIM
    bdk = bd[:KV_DIM, :KV_DIM]
    kc = seg(k0, k0 + KV_DIM)
    vc = seg(k0 + KV_DIM, k0 + 2 * KV_DIM)
    ks = _head_rms(seg(k0 + 2 * KV_DIM, k0 + 3 * KV_DIM), bdk, kn1_ref[...])
    vs = seg(k0 + 3 * KV_DIM, k0 + 4 * KV_DIM)
    kw = _head_rms(seg(k0 + 4 * KV_DIM, k0 + 5 * KV_DIM), bdk, kn2_ref[...])
    vw = seg(k0 + 5 * KV_DIM, k0 + 6 * KV_DIM)
    kw_ref[...] = kw
    vw_ref[...] = vw
    if sample:
        kc_ref[...] = kc
        vc_ref[...] = vc
        ks_ref[...] = ks
        vs_ref[...] = vs
    else:
        kct_ref[0] = kc.T
        vct_ref[0] = vc.T
        kst_ref[0] = ks.T
        vst_ref[0] = vs.T
        for src, dst in ((kc, kcx_ref), (vc, vcx_ref)):
            stage_ref[...] = src
            dst[...] = jnp.concatenate([stage_ref[pl.ds(r, tm // CMP_STRIDE, stride=CMP_STRIDE), :]
                                        for r in range(CMP_STRIDE)], axis=-1)
    gates = _sigmoid(seg(k0 + 6 * KV_DIM, k0 + 6 * KV_DIM + LANES))
    if sample:
        gt_ref[...] = gates
    else:
        pos = pl.program_id(1) * tm + lax.broadcasted_iota(jnp.int32, (tm, LANES), 0)
        block_onehot = jnp.where(lane - HEAD_DIM == (pos >> 6), 1.0, 0.0)
        for h, (a, b, c, e) in enumerate(zip(head_planes(ks, block_onehot), head_planes(vs, 1.0),
                                             head_planes(kw, 0.0), head_planes(vw, 1.0))):
            ksb_ref[h] = a.astype(BF16)
            vsb_ref[h] = b.astype(BF16)
            kwb_ref[h] = c.astype(BF16)
            vwb_ref[h] = e.astype(BF16)
        gt_ref[0] = gates
        gt_ref[1] = pltpu.roll(gates, LANES - GROUP * N_BRANCH, axis=1)


def _project(x2d, batch, seq, tm, weights, state=None):
    n, d = x2d.shape
    sample = state is not None
    nt = seq // tm if not sample else 1
    const = lambda shape: pl.BlockSpec(shape, lambda b, t: (0,) * len(shape))
    rows = lambda w: pl.BlockSpec((tm, w), lambda b, t: (b * nt + t, 0))
    nm, w_in, qn, kn1, kn2, cw, bd = weights
    in_specs = [rows(d), const(nm.shape), const(w_in.shape), const(qn.shape), const(kn1.shape),
                const(kn2.shape), const(cw.shape), const(bd.shape)]
    args = [x2d, nm, w_in, qn, kn1, kn2, cw, bd]
    kv_f32 = [jax.ShapeDtypeStruct((n, KV_DIM), F32)] * 6
    if sample:
        in_specs += [rows(CONV_DIM), rows(CONV_DIM)]
        args += list(state)
        out_shape = ([jax.ShapeDtypeStruct((n, CONV_DIM), BF16), jax.ShapeDtypeStruct((n, ATTN_DIM), F32)]
                     + kv_f32 + [jax.ShapeDtypeStruct((n, LANES), F32), jax.ShapeDtypeStruct((n, CONV_DIM), F32)])
        out_specs = [rows(CONV_DIM), rows(ATTN_DIM)] + [rows(KV_DIM)] * 6 + [rows(LANES), rows(CONV_DIM)]
        scratch = []
        grid = (1, 1)
    else:
        planes = pl.BlockSpec((N_KV_HEADS, tm, LANES), lambda b, t: (0, b * nt + t, 0))
        out_shape = ([jax.ShapeDtypeStruct((n, CONV_DIM), BF16),
                      jax.ShapeDtypeStruct((batch, N_HEADS, seq, LANES), BF16)]
                     + kv_f32[:2] + [jax.ShapeDtypeStruct((N_KV_HEADS, n, LANES), F32),
                                     jax.ShapeDtypeStruct((batch, 8, CONV_DIM), F32)]
                     + [jax.ShapeDtypeStruct((N_KV_HEADS, n, LANES), BF16)] * 4
                     + [jax.ShapeDtypeStruct((batch, KV_DIM, seq), F32)] * 4
                     + [jax.ShapeDtypeStruct((n // CMP_STRIDE, CMP_STRIDE * KV_DIM), F32)] * 2)
        out_specs = ([rows(CONV_DIM), pl.BlockSpec((1, N_HEADS, tm, LANES), lambda b, t: (b, 0, t, 0))]
                     + [rows(KV_DIM)] * 2 + [planes, pl.BlockSpec((1, 8, CONV_DIM), lambda b, t: (b, 0, 0))]
                     + [planes] * 4 + [pl.BlockSpec((1, KV_DIM, tm), lambda b, t: (b, 0, t))] * 4
                     + [pl.BlockSpec((tm // CMP_STRIDE, CMP_STRIDE * KV_DIM), lambda b, t: (b * nt + t, 0))] * 2)
        scratch = [pltpu.VMEM((8, CONV_DIM), F32), pltpu.VMEM((tm, KV_DIM), F32)]
        grid = (batch, nt)
    return pl.pallas_call(
        functools.partial(_proj_body, sample, tm),
        grid=grid, in_specs=in_specs, out_specs=out_specs, out_shape=out_shape, scratch_shapes=scratch,
        compiler_params=_cparams("arbitrary", "arbitrary"),
        name="proj_sample" if sample else "proj_prompt",
    )(*args)


def _gelu_tanh(x):
    cdf = 0.5 * (1.0 + jnp.tanh(math.sqrt(2.0 / math.pi) * (x + 0.044715 * (x * x * x))))
    return x * cdf


def _compress_core(norm, x, pe_ref, we_ref, w2_ref, g_ref):
    n = x.shape[0]
    a0 = _dot((x + pe_ref[0:1]).astype(BF16), we_ref[0])
    a1 = _dot((x + pe_ref[1:2]).astype(BF16), we_ref[1])
    hid = a0 + pltpu.roll(a1, n - 1, axis=0)
    w2 = w2_ref[...]
    outs = []
    for h in range(N_KV_HEADS):
        act = _gelu_tanh(hid[:, h * CMP_HIDDEN:(h + 1) * CMP_HIDDEN])
        o = _dot(act.astype(BF16), w2)
        if norm:
            o = _rms(o, g_ref[...])
        outs.append(o)
    return outs


def _compress_rows_body(norm, x_ref, pe_ref, we_ref, w2_ref, g_ref, o_ref):
    outs = _compress_core(norm, x_ref[0], pe_ref, we_ref, w2_ref, g_ref)
    for h in range(N_KV_HEADS):
        o_ref[0, h] = jnp.concatenate([outs[h], jnp.zeros_like(outs[h])], axis=-1).astype(BF16)


def _fetch_pages(pt_ref, cache_ref, buf_ref, sem_ref, step, slot, n_ops, priority):
    base = step * n_ops

    def issue(j, _):
        pltpu.make_async_copy(cache_ref.at[pt_ref[base + j]], buf_ref.at[slot, j], sem_ref.at[slot]).start(
            priority=priority)
        return 0

    lax.fori_loop(0, n_ops, issue, 0)


def _paged_prefetch(pt_ref, caches, bufs, sems, n_ops):
    i = pl.program_id(0)
    slot = i & 1
    for k, (cache_ref, buf_ref, sem_ref) in enumerate(zip(caches, bufs, sems)):
        @pl.when(i == 0)
        def _():
            _fetch_pages(pt_ref, cache_ref, buf_ref, sem_ref, 0, 0, n_ops, k % 2)

        @pl.when(i + 1 < pl.num_programs(0))
        def _():
            _fetch_pages(pt_ref, cache_ref, buf_ref, sem_ref, i + 1, 1 - slot, n_ops, k % 2)

    for cache_ref, buf_ref, sem_ref in zip(caches, bufs, sems):
        pltpu.make_async_copy(cache_ref.at[pl.ds(0, n_ops)], buf_ref.at[slot], sem_ref.at[slot]).wait()
    return slot


def _compress_pages_body(norm, nch, n_seq, n_pages, pt_ref, cache_ref, pe_ref, we_ref, w2_ref, g_ref, o_ref,
                         rows_ref, buf_ref, sem_ref):
    n_ops = n_seq * n_pages
    slot = _paged_prefetch(pt_ref, [cache_ref], [buf_ref], [sem_ref], n_ops)
    page = buf_ref.shape[3]
    per_page = page // CMP_STRIDE
    for j in range(n_ops):
        rows = buf_ref[slot, j].T
        for c in range(per_page):
            first = (j * per_page + c) * CHUNK_PITCH
            rows_ref[first:first + CMP_STRIDE, :] = rows[c * CMP_STRIDE:(c + 1) * CMP_STRIDE]
    x = jnp.concatenate([rows_ref[pl.ds(r, n_seq * nch, stride=CHUNK_PITCH), :] for r in range(CMP_STRIDE)], axis=-1)
    out = jnp.concatenate(_compress_core(norm, x, pe_ref, we_ref, w2_ref, g_ref), axis=-1)
    for b in range(n_seq):
        o_ref[b] = out[b * nch:(b + 1) * nch]


def _compress_weights(pe, w1, w2):
    w1r = w1.reshape(2, CMP_STRIDE, HEAD_DIM, CMP_HIDDEN).astype(BF16)
    z = jnp.zeros_like(w1r)
    we = jnp.stack([jnp.concatenate([w1r, z], axis=-1), jnp.concatenate([z, w1r], axis=-1)], axis=2)
    we = we.reshape(2, CMP_STRIDE * KV_DIM, N_KV_HEADS * CMP_HIDDEN)
    per = pe.reshape(2, CMP_STRIDE, 1, HEAD_DIM)
    pex = jnp.broadcast_to(per, (2, CMP_STRIDE, N_KV_HEADS, HEAD_DIM)).reshape(2, CMP_STRIDE * KV_DIM)
    return pex.astype(F32), we.astype(BF16), w2.astype(BF16)


def _compress_rows(rows3, cw, gain, norm):
    b, nch, width = rows3.shape
    pex, we, w2 = cw
    const = lambda a: pl.BlockSpec(a.shape, lambda i: (0,) * a.ndim)
    return pl.pallas_call(
        functools.partial(_compress_rows_body, norm),
        grid=(b,),
        in_specs=[pl.BlockSpec((1, nch, width), lambda i: (i, 0, 0)), const(pex), const(we), const(w2), const(gain)],
        out_specs=pl.BlockSpec((1, N_KV_HEADS, nch, LANES), lambda i: (i, 0, 0, 0)),
        out_shape=jax.ShapeDtypeStruct((b, N_KV_HEADS, nch, LANES), BF16),
        compiler_params=_cparams("arbitrary"),
        name="compress_rows",
    )(rows3, pex, we, w2, gain)


def _compress_pages(cache_t, pt_flat, n_batch, n_pages, cw, gain, norm):
    _, _, page = cache_t.shape
    nch = n_pages * page // CMP_STRIDE
    pex, we, w2 = cw
    n_seq = COMPRESS_SEQS if n_batch % COMPRESS_SEQS == 0 else 1
    const = lambda a: pl.BlockSpec(a.shape, lambda i, pt: (0,) * a.ndim)
    n_ops = n_seq * n_pages
    grid_spec = pltpu.PrefetchScalarGridSpec(
        num_scalar_prefetch=1, grid=(n_batch // n_seq,),
        in_specs=[pl.BlockSpec(memory_space=pl.ANY), const(pex), const(we), const(w2), const(gain)],
        out_specs=pl.BlockSpec((n_seq, nch, KV_DIM), lambda i, pt: (i, 0, 0)),
        scratch_shapes=[pltpu.VMEM((n_ops * (page // CMP_STRIDE) * CHUNK_PITCH, KV_DIM), F32),
                        pltpu.VMEM((2, n_ops, KV_DIM, page), F32),
                        pltpu.SemaphoreType.DMA((2,))])
    return pl.pallas_call(
        functools.partial(_compress_pages_body, norm, nch, n_seq, n_pages),
        grid_spec=grid_spec,
        out_shape=jax.ShapeDtypeStruct((n_batch, nch, KV_DIM), F32),
        compiler_params=_cparams("arbitrary"),
        name="compress_pages",
    )(pt_flat, cache_t, pex, we, w2, gain)


def _rel_bucket(dist):
    n = jnp.maximum(dist, 0)
    max_exact = NUM_BUCKETS // 2
    nf = jnp.maximum(n, 1).astype(F32)
    large = max_exact + (jnp.log(nf / max_exact) / math.log(MAX_DISTANCE / max_exact)
                         * (NUM_BUCKETS - max_exact)).astype(jnp.int32)
    large = jnp.minimum(large, NUM_BUCKETS - 1)
    return jnp.where(n < max_exact, n, large)


def _bias_by_distance(rel_bias):
    d = jnp.arange(BIAS_DMAX, dtype=jnp.int32)
    f = rel_bias.astype(F32)[_rel_bucket(d)]
    f = (f - f[BIAS_DMAX - 1:BIAS_DMAX]).T
    return jnp.concatenate([f, jnp.full((N_HEADS, 1), NEG_INF, F32)], axis=1)


def _bias_index(d, valid):
    return np.where(valid, np.clip(d, 0, BIAS_DMAX - 1), BIAS_DMAX).astype(np.int32)


def _bias_table(fext, d, valid):
    return jnp.take(fext, jnp.asarray(_bias_index(d, valid)), axis=1)


def _toeplitz_body(n_rows, v_ref, o_ref):
    x = jnp.broadcast_to(v_ref[0], (n_rows, v_ref.shape[2]))
    o_ref[0] = pltpu.roll(x, 0, axis=1, stride=1, stride_axis=0)


def _toeplitz_rows(v, n_rows):
    h, w = v.shape
    return pl.pallas_call(
        functools.partial(_toeplitz_body, n_rows),
        grid=(h,),
        in_specs=[pl.BlockSpec((1, 1, w), lambda i: (i, 0, 0))],
        out_specs=pl.BlockSpec((1, n_rows, w), lambda i: (i, 0, 0)),
        out_shape=jax.ShapeDtypeStruct((h, n_rows, w), F32),
        compiler_params=_cparams("arbitrary"),
        name="toeplitz_rows",
    )(v.reshape(h, 1, w))


def _select_blocks(imp_t, srow, qpos, n_rank):
    qblk = qpos >> 6
    forced = (srow == 0) | (srow == qblk) | (srow == qblk - 1)
    valid = (srow << 6) <= qpos
    imp_t = jnp.where(valid, imp_t + jnp.where(forced, FORCE_SCORE, 0.0), NEG_INF)
    n_rows = imp_t.shape[0]
    assert n_rows % 8 == 0
    slabs = [imp_t[a:a + 8] for a in range(0, n_rows, 8)]
    rows8 = [srow[a:a + 8] for a in range(0, n_rows, 8)]
    cnts = [jnp.zeros(x.shape, jnp.int32) for x in slabs]
    for s in range(n_rank):
        r = imp_t[s:s + 1, :]
        for j, x in enumerate(slabs):
            if 8 * j > s:
                beats = r >= x
            elif 8 * j + 7 <= s:
                beats = r > x
            else:
                beats = (r > x) | ((r == x) & (rows8[j] > s))
            cnts[j] = cnts[j] + jnp.where(beats, 1, 0)
    cnt = jnp.concatenate(cnts, axis=0)
    return jnp.where((cnt < N_SEL) & valid, 1.0, 0.0)


def _exp_pv(s, m, v):
    return _dot(jnp.exp((s - m).astype(BF16)), v)


def _normalize_pv(pv):
    return pv / jnp.maximum(pv[:, HEAD_DIM:HEAD_DIM + 1], 1e-30)


def _attn_prompt_body(kt, n_slc, q_ref, gt_ref, kcmp_ref, vcmp_ref, ks_ref, vs_ref, kw_ref, vw_ref,
                      ctab_ref, ntab_ref, wtab_ref, ovl_ref, o_ref, sa_ref, sb_ref):
    i = pl.program_id(2)
    qb = Q_BLOCK
    rows = GROUP * qb
    n_cmp_pad = kcmp_ref.shape[2]
    first_near_block = (qb // SLC_BLOCK) * jnp.maximum(i - 1, 0)
    near_start = pl.multiple_of(jnp.maximum(i - 1, 0) * qb, qb)
    win_start = pl.multiple_of(jnp.maximum(i * qb - WINDOW, 0), qb)
    band = WINDOW + qb
    gates = gt_ref[0]
    lane = lax.broadcasted_iota(jnp.int32, (qb, LANES), 1)
    q0 = q_ref[0].reshape(rows, LANES)

    s = _dot_nt(q0, kw_ref[0, pl.ds(win_start, band), :]) + wtab_ref[0].reshape(rows, band)
    m = jnp.max(s, axis=-1, keepdims=True)
    m = jnp.where(m == NEG_INF, 0.0, m)
    o_w = _normalize_pv(_exp_pv(s, m, vw_ref[0, pl.ds(win_start, band), :]))

    per_qb = qb // CMP_STRIDE
    ctab = pltpu.roll(ctab_ref[...].reshape(rows, 2 * n_cmp_pad), i * per_qb, axis=1)[:, n_cmp_pad:]
    s = _dot_nt(q0, kcmp_ref[0, 0]) + ctab
    p, l = _softmax_parts(s)
    pn = p / jnp.maximum(l, 1e-30)
    o_c = _dot(pn.astype(BF16), vcmp_ref[0, 0])

    psum = pn[0:qb] + pn[qb:2 * qb] + pn[2 * qb:3 * qb] + pn[3 * qb:4 * qb]
    hi, mid, lo = _split3(psum)
    ovl = ovl_ref[...]
    imp_t = _dot_nt(ovl, hi) + _dot_nt(ovl, mid) + _dot_nt(ovl, lo)
    srow = lax.broadcasted_iota(jnp.int32, (n_slc, qb), 0)
    qpos_t = i * qb + lax.broadcasted_iota(jnp.int32, (n_slc, qb), 1)
    sel_t = _select_blocks(imp_t, srow, qpos_t, n_slc)
    sel_t = jnp.concatenate([sel_t, jnp.zeros((LANES - n_slc, qb), F32)], axis=0)
    sel = sel_t.T

    def query_with_mask(keep):
        m = pltpu.roll(jnp.where(keep, 0.0, MASKED), HEAD_DIM, axis=1).astype(BF16)
        m = jnp.concatenate([jnp.where(lane < HEAD_DIM, q0[g * qb:(g + 1) * qb], m) for g in range(GROUP)], axis=0)
        return m

    q_near = query_with_mask(sel > 0.5)
    q_far = query_with_mask((sel > 0.5) & (lane < first_near_block))

    s = _dot_nt(q_near, ks_ref[0, pl.ds(near_start, 2 * qb), :]) + ntab_ref[0].reshape(rows, 2 * qb)
    m0 = jnp.max(s, axis=-1, keepdims=True)
    m0 = jnp.where(m0 == NEG_INF, 0.0, m0)
    a0 = _exp_pv(s, m0, vs_ref[0, pl.ds(near_start, 2 * qb), :])

    n_kt = ks_ref.shape[1] // kt

    def scores(t):
        k0 = pl.multiple_of(jnp.minimum(t, n_kt - 1) * kt, kt)
        return _dot_nt(q_far, ks_ref[0, pl.ds(k0, kt), :])

    def consume(t, s, m_old, acc):
        k0 = pl.multiple_of(jnp.minimum(t, n_kt - 1) * kt, kt)
        m_new = jnp.maximum(m_old, jnp.max(s, axis=-1, keepdims=True))
        return m_new, jnp.exp(m_old - m_new) * acc + _exp_pv(s, m_new, vs_ref[0, pl.ds(k0, kt), :])

    def far_pair(u, carry):
        m, acc = carry
        sb_ref[...] = scores(2 * u + 1)
        m, acc = consume(2 * u, sa_ref[...], m, acc)
        sa_ref[...] = scores(2 * u + 2)
        return consume(2 * u + 1, sb_ref[...], m, acc)

    n_far = (near_start + kt - 1) // kt
    sa_ref[...] = scores(0)
    _, acc_s = lax.fori_loop(0, (n_far + 1) // 2, far_pair, (m0, a0))
    o_s = _normalize_pv(acc_s)

    heads_out = []
    for g in range(GROUP):
        c = g * N_BRANCH
        sl = slice(g * qb, (g + 1) * qb)
        heads_out.append(gates[:, c:c + 1] * o_c[sl] + gates[:, c + 1:c + 2] * o_s[sl] + gates[:, c + 2:c + 3] * o_w[sl])
    tiles = [jnp.where(lane < HEAD_DIM, heads_out[2 * j], pltpu.roll(heads_out[2 * j + 1], HEAD_DIM, axis=1))
             for j in range(GROUP // 2)]
    o_ref[...] = jnp.concatenate(tiles, axis=-1).astype(BF16)


def _attn_prompt(qp, gates, kcmp, vcmp, ksb, vsb, kwb, vwb, fext, batch, seq):
    qb = Q_BLOCK
    nqb = seq // qb
    n_slc = seq // SLC_BLOCK
    assert n_slc <= LANES - HEAD_DIM
    n_cmp_pad = kcmp.shape[2]
    n_cmp = n_cmp_pad - 1
    kt = min(512, seq)
    assert (seq // kt) % 2 == 0
    band = WINDOW + qb
    iq = np.arange(qb)

    per_qb = qb // CMP_STRIDE
    half = 2 * per_qb
    m = np.arange(-half, half)
    d = iq[:, None] - (m[None, :] * CMP_STRIDE + CMP_BLOCK - 1)
    assert d[:, 0].min() >= MAX_DISTANCE and d[:, -1].max() < 0
    ctab = jnp.concatenate([jnp.zeros((N_HEADS, qb, n_cmp_pad - half), F32), _bias_table(fext, d, d >= 0),
                            jnp.full((N_HEADS, qb, n_cmp_pad - half), NEG_INF, F32)], axis=2)
    assert per_qb * (nqb - 1) < n_cmp_pad
    assert (n_cmp_pad - 1) * CMP_STRIDE + CMP_BLOCK - 1 >= seq and n_cmp == n_cmp_pad - 1

    nv = WINDOW // qb + 1
    kw_ = WINDOW + band
    w = qb + kw_
    assert w % LANES == 0
    m = np.arange(w)
    m = np.where(m < kw_, m, m - w)
    dj = WINDOW - m
    wide = _toeplitz_rows(_bias_table(fext, dj, (dj >= 0) & (dj < WINDOW)), qb)
    wtab = jnp.stack([wide[:, :, WINDOW - qb * v:WINDOW - qb * v + band] for v in range(nv)], axis=0)
    ntab = jnp.stack([wide[:, :, WINDOW - qb * v:WINDOW - qb * v + 2 * qb] for v in range(2)], axis=0)
    assert 2 * qb <= WINDOW

    c_start = np.arange(n_cmp_pad) * CMP_STRIDE
    s_start = np.arange(n_slc) * SLC_BLOCK
    ovl = ((c_start[None, :] < s_start[:, None] + SLC_BLOCK) & (c_start[None, :] + CMP_BLOCK > s_start[:, None])
           & (np.arange(n_cmp_pad) < n_cmp)[None, :])
    ovl = jnp.asarray(ovl, BF16)

    per_head = lambda: pl.BlockSpec((1, seq, LANES), lambda b, h, i: (h, b, 0))
    return pl.pallas_call(
        functools.partial(_attn_prompt_body, kt, n_slc),
        grid=(batch, N_KV_HEADS, nqb),
        in_specs=[pl.BlockSpec((1, GROUP, qb, LANES), lambda b, h, i: (b, h, i, 0)),
                  pl.BlockSpec((1, qb, LANES), lambda b, h, i: (h, b * nqb + i, 0)),
                  pl.BlockSpec((1, 1, n_cmp_pad, LANES), lambda b, h, i: (b, h, 0, 0)),
                  pl.BlockSpec((1, 1, n_cmp_pad, LANES), lambda b, h, i: (b, h, 0, 0)),
                  per_head(), per_head(), per_head(), per_head(),
                  pl.BlockSpec((GROUP, qb, 2 * n_cmp_pad), lambda b, h, i: (h, 0, 0)),
                  pl.BlockSpec((1, GROUP, qb, 2 * qb), lambda b, h, i: (jnp.minimum(i, 1), h, 0, 0)),
                  pl.BlockSpec((1, GROUP, qb, band), lambda b, h, i: (jnp.minimum(i, nv - 1), h, 0, 0)),
                  pl.BlockSpec(ovl.shape, lambda b, h, i: (0, 0))],
        out_specs=pl.BlockSpec((qb, GROUP * HEAD_DIM), lambda b, h, i: (b * nqb + i, h)),
        out_shape=jax.ShapeDtypeStruct((batch * seq, ATTN_DIM), BF16),
        scratch_shapes=[pltpu.VMEM((GROUP * qb, kt), F32)] * 2,
        compiler_params=_cparams("arbitrary", "arbitrary", "arbitrary"),
        name="attn_prompt",
    )(qp, gates, kcmp, vcmp, ksb, vsb, kwb, vwb, ctab, ntab, wtab, ovl)


def _attn_sample_body(n_seq, n_pages, ts, past_len, pt_ref, cache_k_ref, cache_v_ref, *refs):
    kbuf, vbuf, ksem, vsem = refs[-4:]
    n_ops = n_seq * n_pages
    slot = _paged_prefetch(pt_ref, [cache_k_ref, cache_v_ref], [kbuf, vbuf], [ksem, vsem], n_ops)
    for b in range(n_seq):
        _attn_sample_one(b, [kbuf.at[slot, b * n_pages + j] for j in range(n_pages)],
                         [vbuf.at[slot, b * n_pages + j] for j in range(n_pages)], refs[:-4], ts, past_len)


def _attn_sample_one(b, kpages, vpages, refs, ts, past_len):
    (q_ref, gt_ref, kcmp_ref, vcmp_ref, ksn_ref, vsn_ref, kwc_ref, vwc_ref, kwn_ref, vwn_ref,
     ctab_ref, stab_ref, sntab_ref, wtab_ref, wntab_ref, ovl_ref, o_ref) = refs
    rows = GROUP * N_KV_HEADS * ts
    rq = N_KV_HEADS * ts
    q = q_ref[b]
    gates = gt_ref[b]

    s = _dot_nt(q, kcmp_ref[b].astype(BF16)) + ctab_ref[...]
    p, l = _softmax_parts(s)
    pn = p / jnp.maximum(l, 1e-30)
    o_c = _dot(pn.astype(BF16), vcmp_ref[b].astype(BF16))

    psum = pn[0:rq]
    for g in range(1, GROUP):
        psum = psum + pn[g * rq:(g + 1) * rq]
    hi, mid, lo = _split3(psum)
    ovl = ovl_ref[...]
    imp = _dot(hi, ovl) + _dot(mid, ovl) + _dot(lo, ovl)
    n_slc = -(-(past_len + ts) // SLC_BLOCK)
    blk = lax.broadcasted_iota(jnp.int32, (rq, LANES), 1)
    qpos = past_len + (lax.broadcasted_iota(jnp.int32, (rq, LANES), 0) & (ts - 1))
    qblk = qpos >> 6
    forced = (blk == 0) | (blk == qblk) | (blk == qblk - 1)
    valid = ((blk << 6) <= qpos) & (blk < n_slc)
    imp = jnp.where(valid, imp + jnp.where(forced, FORCE_SCORE, 0.0), NEG_INF)
    cnt = jnp.zeros((rq, LANES), jnp.int32)
    for sidx in range(n_slc):
        r = imp[:, sidx:sidx + 1]
        beats = (r > imp) | ((r == imp) & (blk > sidx))
        cnt = cnt + jnp.where(beats, 1, 0)
    sel = jnp.where((cnt < N_SEL) & valid, 1.0, 0.0)
    sel = jnp.concatenate([sel] * GROUP, axis=0)

    kc_t = jnp.concatenate([p_[...] for p_ in kpages], axis=1).astype(BF16)
    vc_t = jnp.concatenate([p_[...] for p_ in vpages], axis=1).astype(BF16)
    lane_r = lax.broadcasted_iota(jnp.int32, (rows, LANES), 1)
    per_tile = LANES // SLC_BLOCK
    tiles = []
    for j in range(past_len // LANES):
        m = sel[:, per_tile * j:per_tile * j + 1]
        for c in range(1, per_tile):
            m = jnp.where(lane_r >= c * SLC_BLOCK, sel[:, per_tile * j + c:per_tile * j + c + 1], m)
        tiles.append(m)
    mexp = jnp.concatenate(tiles, axis=1)
    s1 = jnp.where(mexp > 0.5, _dot(q, kc_t) + stab_ref[...], NEG_INF)
    last = sel[:, n_slc - 1:n_slc]
    s2 = jnp.where(last > 0.5, _dot_nt(q, ksn_ref[b].astype(BF16)) + sntab_ref[...], NEG_INF)
    m = jnp.maximum(jnp.max(s1, axis=-1, keepdims=True), jnp.max(s2, axis=-1, keepdims=True))
    m = jnp.where(m == NEG_INF, 0.0, m)
    p1 = jnp.exp(s1 - m)
    p2 = jnp.exp(s2 - m)
    l = jnp.sum(p1, axis=-1, keepdims=True) + jnp.sum(p2, axis=-1, keepdims=True)
    o_s = (_dot_nt(p1.astype(BF16), vc_t) + _dot(p2.astype(BF16), vsn_ref[b].astype(BF16))) / jnp.maximum(l, 1e-30)

    s1 = _dot(q, kwc_ref[b].astype(BF16)) + wtab_ref[...]
    s2 = _dot_nt(q, kwn_ref[b].astype(BF16)) + wntab_ref[...]
    m = jnp.maximum(jnp.max(s1, axis=-1, keepdims=True), jnp.max(s2, axis=-1, keepdims=True))
    m = jnp.where(m == NEG_INF, 0.0, m)
    p1 = jnp.exp(s1 - m)
    p2 = jnp.exp(s2 - m)
    l = jnp.sum(p1, axis=-1, keepdims=True) + jnp.sum(p2, axis=-1, keepdims=True)
    o_w = (_dot_nt(p1.astype(BF16), vwc_ref[b].astype(BF16))
           + _dot(p2.astype(BF16), vwn_ref[b].astype(BF16))) / jnp.maximum(l, 1e-30)

    o_ref[b] = gates[:, 0:1] * o_c + gates[:, 1:2] * o_s + gates[:, 2:3] * o_w


def _attn_sample(q_s, gates_s, kcmp, vcmp, ks_new, vs_new, kw_new, vw_new, cache_ks, cache_vs,
                 cache_kw, cache_vw, pt_flat, fext, n_batch, ts, n_pages, page):
    past_len = n_pages * page
    w_buf = cache_kw.shape[2]
    rows = GROUP * N_KV_HEADS * ts
    n_new = 8
    n_cmp_pad = kcmp.shape[1]
    n_cmp = n_cmp_pad - 1
    n_slc = -(-(past_len + ts) // SLC_BLOCK)

    q5 = q_s.reshape(n_batch, ts, N_KV_HEADS, GROUP, HEAD_DIM).transpose(0, 3, 2, 1, 4)
    eye = jnp.eye(N_KV_HEADS, dtype=q_s.dtype)
    qr = jnp.einsum("bghtd,hk->bghtkd", q5, eye).reshape(n_batch, rows, LANES).astype(BF16)
    g5 = gates_s[:, :N_HEADS * N_BRANCH].reshape(n_batch, ts, N_KV_HEADS, GROUP, N_BRANCH).transpose(0, 3, 2, 1, 4)
    gr = jnp.pad(g5.reshape(n_batch, rows, N_BRANCH), ((0, 0), (0, 0), (0, LANES - N_BRANCH)))
    pad_new = lambda a: jnp.pad(a.reshape(n_batch, ts, KV_DIM), ((0, 0), (0, n_new - ts), (0, 0)))
    ks_new, vs_new, kw_new, vw_new = map(pad_new, (ks_new, vs_new, kw_new, vw_new))

    g_i, h_i, t_i = np.meshgrid(np.arange(GROUP), np.arange(N_KV_HEADS), np.arange(ts), indexing="ij")
    head = (h_i * GROUP + g_i).reshape(rows)
    tq = t_i.reshape(rows)
    pos_q = past_len + tq

    f_rows = fext[jnp.asarray(head)]

    def table(d, valid):
        return jnp.take_along_axis(f_rows, jnp.asarray(_bias_index(d, valid)), axis=1)

    nn = np.arange(n_cmp_pad)
    d = pos_q[:, None] - (nn[None, :] * CMP_STRIDE + CMP_BLOCK - 1)
    ctab = table(d, (d >= 0) & (nn < n_cmp)[None, :])
    near = np.arange(past_len - MAX_DISTANCE, past_len)
    d = pos_q[:, None] - near[None, :]
    assert past_len >= MAX_DISTANCE and d.min() >= 0
    stab = jnp.concatenate([jnp.zeros((rows, past_len - MAX_DISTANCE), F32), table(d, d >= 0)], axis=1)
    jn = np.arange(n_new)
    d = tq[:, None] - jn[None, :]
    sntab = table(d, (d >= 0) & (jn < ts)[None, :])
    pos_w = past_len - w_buf + np.arange(w_buf)
    d = pos_q[:, None] - pos_w[None, :]
    wtab = table(d, (d >= 0) & (d < WINDOW) & (pos_w >= 0)[None, :])
    wntab = table(tq[:, None] - jn[None, :], (tq[:, None] >= jn[None, :]) & (jn < ts)[None, :])

    c_start = nn * CMP_STRIDE
    s_start = np.arange(LANES) * SLC_BLOCK
    ovl = jnp.asarray((c_start[:, None] < s_start[None, :] + SLC_BLOCK) & (c_start[:, None] + CMP_BLOCK > s_start[None, :])
                      & (nn < n_cmp)[:, None] & (np.arange(LANES) < n_slc)[None, :], BF16)

    n_seq = SAMPLE_SEQS if n_batch % SAMPLE_SEQS == 0 else 1
    n_ops = n_seq * n_pages
    const = lambda a: pl.BlockSpec(a.shape, lambda b, pt: (0,) * a.ndim)
    per_b = lambda a: pl.BlockSpec((n_seq,) + a.shape[1:], lambda b, pt: (b,) + (0,) * (a.ndim - 1))
    any_spec = pl.BlockSpec(memory_space=pl.ANY)
    small = [qr, gr, kcmp, vcmp, ks_new, vs_new, cache_kw, cache_vw, kw_new, vw_new]
    consts = [ctab, stab, sntab, wtab, wntab, ovl]
    page_buf = pltpu.VMEM((2, n_ops, KV_DIM, page), F32)
    grid_spec = pltpu.PrefetchScalarGridSpec(
        num_scalar_prefetch=1, grid=(n_batch // n_seq,),
        in_specs=[any_spec, any_spec] + [per_b(a) for a in small] + [const(a) for a in consts],
        out_specs=pl.BlockSpec((n_seq, rows, LANES), lambda b, pt: (b, 0, 0)),
        scratch_shapes=[page_buf, page_buf, pltpu.SemaphoreType.DMA((2,)), pltpu.SemaphoreType.DMA((2,))])
    o = pl.pallas_call(
        functools.partial(_attn_sample_body, n_seq, n_pages, ts, past_len),
        grid_spec=grid_spec,
        out_shape=jax.ShapeDtypeStruct((n_batch, rows, LANES), F32),
        compiler_params=_cparams("arbitrary"),
        name="attn_sample",
    )(pt_flat, cache_ks, cache_vs, *small, *consts)
    o6 = o.reshape(n_batch, GROUP, N_KV_HEADS, ts, N_KV_HEADS, HEAD_DIM)
    o5 = jnp.stack([o6[:, :, h, :, h] for h in range(N_KV_HEADS)], axis=2)
    return o5.transpose(0, 3, 2, 1, 4).reshape(n_batch * ts, ATTN_DIM).astype(BF16)


def _post1_body(tm, x_ref, co_ref, at_ref, wo_ref, nf_ref, wr_ref, br_ref, tri_ref, run0_ref,
                h_ref, hn_ref, rt_ref, cnt_ref, run_ref):
    @pl.when(pl.program_id(0) == 0)
    def _():
        run_ref[...] = run0_ref[...]

    h = x_ref[...] + _dot(co_ref[...], wo_ref[0:CONV_DIM]) + _dot(at_ref[...], wo_ref[CONV_DIM:CONV_DIM + ATTN_DIM])
    hn = _rms(h, nf_ref[...])
    h_ref[...] = h
    _store_token_tiles(hn_ref, hn, tm)

    hi = hn.astype(BF16)
    lo = (hn - hi.astype(F32)).astype(BF16)
    wr = wr_ref[...]
    whi = wr.astype(BF16)
    wlo = (wr - whi.astype(F32)).astype(BF16)
    both = _dot(hi, jnp.concatenate([whi, wlo], axis=1))
    logits = both[:, :LANES] + both[:, LANES:] + _dot(lo, whi) + br_ref[...]

    lane_i = lax.broadcasted_iota(jnp.int32, (tm, LANES), 1)
    lane = lane_i.astype(F32)
    big = float(LANES)
    gmask = (lane_i >= ROUTER_GROUP_LANE) & (lane_i < ROUTER_GROUP_LANE + N_GROUPS)
    lg = jnp.where(gmask, logits, NEG_INF)
    eg = jnp.exp(lg - jnp.max(lg, axis=-1, keepdims=True))
    pg = eg / jnp.sum(eg, axis=-1, keepdims=True)
    gw = jnp.max(pg, axis=-1, keepdims=True)
    grp = jnp.min(jnp.where(gmask & (pg == gw), lane, big), axis=-1, keepdims=True) - ROUTER_GROUP_LANE

    group_of_lane = (lane_i >> 3).astype(F32)
    emask = (lane_i < N_EXPERTS) & (group_of_lane == grp)
    le = jnp.where(emask, logits, NEG_INF)
    ee = jnp.exp(le - jnp.max(le, axis=-1, keepdims=True))
    pe = jnp.where(emask, ee / jnp.sum(ee, axis=-1, keepdims=True), -1.0)
    v1 = jnp.max(pe, axis=-1, keepdims=True)
    i1 = jnp.min(jnp.where(pe == v1, lane, big), axis=-1, keepdims=True)
    pe2 = jnp.where(lane == i1, -1.0, pe)
    v2 = jnp.max(pe2, axis=-1, keepdims=True)
    i2 = jnp.min(jnp.where(pe2 == v2, lane, big), axis=-1, keepdims=True)
    tot = v1 + v2
    w1 = v1 / tot * gw
    w2 = v2 / tot * gw

    oh1 = jnp.where(lane == i1, 1.0, 0.0)
    oh2 = jnp.where(lane == i2, 1.0, 0.0)
    both = oh1 + oh2
    before = _dot(tri_ref[...], both.astype(BF16)) + run_ref[0:1]
    r1 = jnp.sum(oh1 * before, axis=-1, keepdims=True)
    r2 = jnp.sum(oh2 * before, axis=-1, keepdims=True)
    run = run_ref[0:1] + jnp.sum(both, axis=0, keepdims=True)
    run_ref[...] = jnp.broadcast_to(run, run_ref.shape)
    cnt_ref[...] = jnp.broadcast_to(run, cnt_ref.shape)

    rt = jnp.where(lane_i == 0, i1, 0.0)
    rt = jnp.where(lane_i == 1, i2, rt)
    rt = jnp.where(lane_i == 2, r1, rt)
    rt = jnp.where(lane_i == 3, r2, rt)
    rt = jnp.where(lane_i == 4, w1, rt)
    rt = jnp.where(lane_i == 5, w2, rt)
    rt_ref[...] = rt


def _post1(x2d, co, at, wo, nf, wr, br, run0, tm):
    n, d = x2d.shape
    tri = jnp.asarray(np.tril(np.ones((tm, tm), np.float32), -1), BF16)
    rows = lambda w: pl.BlockSpec((tm, w), lambda i: (i, 0))
    const = lambda a: pl.BlockSpec(a.shape, lambda i: (0,) * a.ndim)
    return pl.pallas_call(
        functools.partial(_post1_body, tm),
        grid=(n // tm,),
        in_specs=[rows(d), rows(CONV_DIM), rows(ATTN_DIM), const(wo), const(nf), const(wr), const(br),
                  const(tri), const(run0)],
        out_specs=[rows(d), pl.BlockSpec((tm * TOKEN_TILE_ROWS, LANES), lambda i: (i, 0)), rows(LANES),
                   pl.BlockSpec((8, LANES), lambda i: (0, 0))],
        out_shape=[jax.ShapeDtypeStruct((n, d), F32), jax.ShapeDtypeStruct((n * TOKEN_TILE_ROWS, LANES), F32),
                   jax.ShapeDtypeStruct((n, LANES), F32), jax.ShapeDtypeStruct((8, LANES), F32)],
        scratch_shapes=[pltpu.VMEM((8, LANES), F32)],
        compiler_params=_cparams("arbitrary"),
        name="post1",
    )(x2d, co, at, wo, nf, wr, br, tri, run0)


def _token_copy(src_ref, dst_ref, s, d, sem):
    r = TOKEN_TILE_ROWS
    return pltpu.make_async_copy(src_ref.at[pl.ds(pl.multiple_of(s * r, r), r)],
                                 dst_ref.at[pl.ds(pl.multiple_of(d * r, r), r)], sem)


def _scatter_rows_body(ts, dest_ref, src_ref, init_ref, out_ref, sem):
    del init_ref
    base = pl.program_id(0) * (2 * ts)

    def issue(t, _):
        for k in range(2):
            _token_copy(src_ref, out_ref, t, dest_ref[base + 2 * t + k], sem).start(priority=k)
        return 0

    lax.fori_loop(0, ts, issue, 0, unroll=DMA_ISSUE_UNROLL)
    for _ in range(2):
        pltpu.make_async_copy(src_ref, out_ref.at[pl.ds(0, ts * TOKEN_TILE_ROWS)], sem).wait()


def _scatter_rows(dest, src, slots):
    n_tok = dest.shape[0] // 2
    ts = min(SCATTER_TOKENS, n_tok)
    assert n_tok % ts == 0
    any_spec = pl.BlockSpec(memory_space=pl.ANY)
    return pl.pallas_call(
        functools.partial(_scatter_rows_body, ts),
        grid_spec=pltpu.PrefetchScalarGridSpec(
            num_scalar_prefetch=1, grid=(n_tok // ts,),
            in_specs=[pl.BlockSpec((ts * TOKEN_TILE_ROWS, LANES), lambda i, dest: (i, 0)), any_spec],
            out_specs=any_spec, scratch_shapes=[pltpu.SemaphoreType.DMA(())]),
        out_shape=jax.ShapeDtypeStruct(slots.shape, slots.dtype),
        input_output_aliases={2: 0},
        compiler_params=pltpu.CompilerParams(dimension_semantics=("arbitrary",)),
        name="scatter_rows",
    )(dest, src, slots)


def _experts_body(be_ref, nu_ref, x_ref, wg_ref, wu_ref, wd_ref, o_ref, wg_s, wu_s, wd_s):
    i = pl.program_id(0)

    @pl.when(i < nu_ref[0])
    def _():
        prev = be_ref[jnp.maximum(i - 1, 0)]

        @pl.when((i == 0) | (be_ref[i] != prev))
        def _():
            wg_s[...] = wg_ref[0].astype(BF16)
            wu_s[...] = wu_ref[0].astype(BF16)
            wd_s[...] = wd_ref[0].astype(BF16)

        x = _load_token_tiles(x_ref, EXPERT_ROWS, TOKEN_TILE_ROWS).astype(BF16)
        g = _dot(x, wg_s[...])
        u = _dot(x, wu_s[...])
        a = g * _sigmoid(g) * u
        _store_token_tiles(o_ref, _dot(a.astype(BF16), wd_s[...]), EXPERT_ROWS)

    @pl.when(i >= nu_ref[0])
    def _():
        o_ref[...] = jnp.zeros_like(o_ref)


def _experts(blk_expert, n_used, xs, wg, wu, wd):
    blk_rows = EXPERT_ROWS * TOKEN_TILE_ROWS
    n_blk = xs.shape[0] // blk_rows
    _, d, de = wg.shape
    xmap = lambda i, be, nu: (jnp.minimum(i, jnp.maximum(nu[0] - 1, 0)), 0)
    wmap = lambda i, be, nu: (be[jnp.minimum(i, jnp.maximum(nu[0] - 1, 0))], 0, 0)
    grid_spec = pltpu.PrefetchScalarGridSpec(
        num_scalar_prefetch=2, grid=(n_blk,),
        in_specs=[pl.BlockSpec((blk_rows, LANES), xmap), pl.BlockSpec((1, d, de), wmap),
                  pl.BlockSpec((1, d, de), wmap), pl.BlockSpec((1, de, d), wmap)],
        out_specs=pl.BlockSpec((blk_rows, LANES), lambda i, be, nu: (i, 0)),
        scratch_shapes=[pltpu.VMEM((d, de), BF16), pltpu.VMEM((d, de), BF16), pltpu.VMEM((de, d), BF16)])
    return pl.pallas_call(
        _experts_body, grid_spec=grid_spec,
        out_shape=jax.ShapeDtypeStruct(xs.shape, F32),
        compiler_params=_cparams("arbitrary"),
        name="experts",
    )(blk_expert, n_used, xs, wg, wu, wd)


def _post2_body(tm, dest_ref, h_ref, rt_ref, p_ref, yb_ref, wple_ref, wpg_ref, bpg_ref, np_ref, o_ref, buf, sem):
    i = pl.program_id(0)
    n = pl.num_programs(0)

    def fetch(step, slot):
        base = step * (2 * tm)

        def issue(t, _):
            for k in range(2):
                _token_copy(yb_ref, buf.at[slot], dest_ref[base + 2 * t + k], k * tm + t, sem.at[slot]).start(priority=k)
            return 0

        lax.fori_loop(0, tm, issue, 0, unroll=DMA_ISSUE_UNROLL)

    @pl.when(i == 0)
    def _():
        fetch(0, 0)

    @pl.when(i + 1 < n)
    def _():
        fetch(i + 1, (i + 1) & 1)

    slot = i & 1

    pltpu.make_async_copy(yb_ref.at[pl.ds(0, 2 * tm * TOKEN_TILE_ROWS)], buf.at[slot], sem.at[slot]).wait()
    rt = rt_ref[...]
    y0 = _load_token_tiles(buf.at[slot], tm, TOKEN_TILE_ROWS)
    y1 = _load_token_tiles(buf.at[slot], tm, TOKEN_TILE_ROWS, first=tm * TOKEN_TILE_ROWS)
    h = h_ref[...] + (y0 * rt[:, 4:5] + y1 * rt[:, 5:6])
    gate = _sigmoid(_dot(_rms(h, np_ref[...]).astype(BF16), wpg_ref[...]) + bpg_ref[...])
    o_ref[...] = h + gate * _dot(p_ref[...].astype(BF16), wple_ref[...])


def _post2(dest, h, rt, p2d, yb, wple, wpg, bpg, npl, tm):
    n, d = h.shape
    rows = lambda w: pl.BlockSpec((tm, w), lambda i, dest: (i, 0))
    const = lambda a: pl.BlockSpec(a.shape, lambda i, dest: (0,) * a.ndim)
    grid_spec = pltpu.PrefetchScalarGridSpec(
        num_scalar_prefetch=1, grid=(n // tm,),
        in_specs=[rows(d), rows(LANES), rows(p2d.shape[1]), pl.BlockSpec(memory_space=pl.ANY), const(wple),
                  const(wpg), const(bpg), const(npl)],
        out_specs=rows(d),
        scratch_shapes=[pltpu.VMEM((2, 2 * tm * TOKEN_TILE_ROWS, LANES), F32), pltpu.SemaphoreType.DMA((2,))])
    return pl.pallas_call(
        functools.partial(_post2_body, tm),
        grid_spec=grid_spec,
        out_shape=jax.ShapeDtypeStruct((n, d), F32),
        compiler_params=_cparams("arbitrary"),
        name="post2",
    )(dest, h, rt, p2d, yb, wple, wpg, bpg, npl)


def _row_tile(n, cap=512):
    t = min(cap, n)
    assert n % t == 0 and t % 8 == 0
    return t


def kernel(x_prompt, x_sample, p_prompt, p_sample, cache_k_cmp, cache_v_cmp, cache_k_slc, cache_v_slc, cache_k_win, cache_v_win, state_conv, page_table, w_in, w_out, conv_w, norm_mix, norm_ffn, norm_ple, q_norm, k_norm, cmp_pe_k, cmp_w1_k, cmp_w2_k, cmp_pe_v, cmp_w1_v, cmp_w2_v, rel_bias, w_router_group, b_router_group, w_router_expert, b_router_expert, w_exp_gate, w_exp_up, w_exp_down, w_ple, w_ple_gate, b_ple_gate):
    assert w_in.shape[0] == 1, "single-layer step"
    bp, t, d = x_prompt.shape
    bs, ts, _ = x_sample.shape
    n_pages = page_table.shape[1]
    page = cache_k_cmp.shape[2]
    past_len = n_pages * page
    w_buf = cache_k_win.shape[2]
    n_phys = cache_k_cmp.shape[1]
    assert t % Q_BLOCK == 0 and t >= WINDOW + Q_BLOCK and page % CMP_STRIDE == 0 and ts == 4 and d == D_MODEL
    assert past_len % SLC_BLOCK == 0
    np_rows, ns_rows = bp * t, bs * ts

    row = lambda v: v.reshape(1, -1).astype(F32)
    w_in_b = jnp.pad(w_in[0], ((0, 0), (0, Z_COLS - w_in.shape[2]))).astype(BF16)
    qn = row(jnp.tile(q_norm[0], N_HEADS))
    kn1 = row(jnp.tile(k_norm[0, 1], N_KV_HEADS))
    kn2 = row(jnp.tile(k_norm[0, 2], N_KV_HEADS))
    bd = jnp.asarray(np.kron(np.eye(N_HEADS), np.ones((HEAD_DIM, HEAD_DIM))), BF16)
    pw = (row(norm_mix[0]), w_in_b, qn, kn1, kn2, conv_w[0].astype(F32), bd)
    cw_k = _compress_weights(cmp_pe_k[0], cmp_w1_k[0], cmp_w2_k[0])
    cw_v = _compress_weights(cmp_pe_v[0], cmp_w1_v[0], cmp_w2_v[0])
    kn0 = row(k_norm[0, 0])
    fext = _bias_by_distance(rel_bias)
    pt_flat = page_table.reshape(-1).astype(jnp.int32)
    wr = jnp.zeros((d, LANES), F32).at[:, :N_EXPERTS].set(w_router_expert[0])
    wr = wr.at[:, ROUTER_GROUP_LANE:ROUTER_GROUP_LANE + N_GROUPS].set(w_router_group[0])
    br = jnp.zeros((1, LANES), F32).at[0, :N_EXPERTS].set(b_router_expert[0])
    br = br.at[0, ROUTER_GROUP_LANE:ROUTER_GROUP_LANE + N_GROUPS].set(b_router_group[0])
    wo_b = w_out[0].astype(BF16)
    wple_b = w_ple[0].astype(BF16)
    wpg_b = w_ple_gate[0].astype(BF16)

    tm_p = _row_tile(t, cap=1024)
    (co_p, q_p, kw_p, vw_p, gt_p, cs_p, ksb, vsb, kwb, vwb, kc_t, vc_t, ks_t, vs_t, kc_x, vc_x) = _project(
        x_prompt.reshape(np_rows, d), bp, t, tm_p, pw)
    chunk_w = CMP_STRIDE * KV_DIM
    kcmp_p = _compress_rows(kc_x.reshape(bp, t // CMP_STRIDE, chunk_w), cw_k, kn0, True)
    vcmp_p = _compress_rows(vc_x.reshape(bp, t // CMP_STRIDE, chunk_w), cw_v, kn0, False)
    at_p = _attn_prompt(q_p, gt_p, kcmp_p, vcmp_p, ksb, vsb, kwb, vwb, fext, bp, t)

    st = state_conv[0].astype(F32)
    s0 = jnp.repeat(st[:, 0], ts, axis=0)
    s1 = jnp.repeat(st[:, 1], ts, axis=0)
    (co_s, q_s, kc_s, vc_s, ks_s, vs_s, kw_s, vw_s, gt_s, u_s) = _project(
        x_sample.reshape(ns_rows, d), bs, ts, ns_rows, pw, state=(s0, s1))
    feature_major = lambda c, n, rows_: jnp.transpose(c[0], (0, 2, 3, 1)).reshape(n, KV_DIM, rows_)
    kcmp_s = _compress_pages(feature_major(cache_k_cmp, n_phys, page), pt_flat, bs, n_pages, cw_k, kn0, True)
    vcmp_s = _compress_pages(feature_major(cache_v_cmp, n_phys, page), pt_flat, bs, n_pages, cw_v, kn0, False)
    at_s = _attn_sample(q_s, gt_s, kcmp_s, vcmp_s, ks_s, vs_s, kw_s, vw_s,
                        feature_major(cache_k_slc, n_phys, page), feature_major(cache_v_slc, n_phys, page),
                        feature_major(cache_k_win, bs, w_buf), feature_major(cache_v_win, bs, w_buf),
                        pt_flat, fext, bs, ts, n_pages, page)

    tp1 = _row_tile(np_rows)
    ts1 = _row_tile(ns_rows)
    nf = row(norm_ffn[0])
    h_p, hn_p, rt_p, cnt_p = _post1(x_prompt.reshape(np_rows, d), co_p, at_p, wo_b, nf, wr, br,
                                    jnp.zeros((8, LANES), F32), tp1)
    h_s, hn_s, rt_s, cnt_s = _post1(x_sample.reshape(ns_rows, d), co_s, at_s, wo_b, nf, wr, br, cnt_p, ts1)

    counts = cnt_s[0, :N_EXPERTS].astype(jnp.int32)
    padded = (counts + EXPERT_ROWS - 1) // EXPERT_ROWS * EXPERT_ROWS
    pad_end = jnp.cumsum(padded)
    pad_start = pad_end - padded
    n_assign = 2 * (np_rows + ns_rows)
    n_blk = (n_assign + N_EXPERTS * (EXPERT_ROWS - 1) + EXPERT_ROWS - 1) // EXPERT_ROWS
    blk_first = jnp.arange(n_blk, dtype=jnp.int32) * EXPERT_ROWS
    blk_expert = jnp.minimum(jnp.sum((pad_end[None, :] <= blk_first[:, None]).astype(jnp.int32), axis=1),
                             N_EXPERTS - 1)
    n_used = (pad_end[-1:] // EXPERT_ROWS).astype(jnp.int32)

    def dest_of(rt):
        e = rt[:, 0:2].astype(jnp.int32).reshape(-1)
        return pad_start[e] + rt[:, 2:4].astype(jnp.int32).reshape(-1)

    dest_p = dest_of(rt_p)
    dest_s = dest_of(rt_s)

    xs = jnp.zeros((n_blk * EXPERT_ROWS * TOKEN_TILE_ROWS, LANES), F32)
    xs = _scatter_rows(dest_p, hn_p, xs)
    xs = _scatter_rows(dest_s, hn_s, xs)
    yb = _experts(blk_expert, n_used, xs, w_exp_gate[0], w_exp_up[0], w_exp_down[0])

    bpg = row(b_ple_gate[0])
    npl = row(norm_ple[0])
    y_p = _post2(dest_p, h_p, rt_p, p_prompt[0].reshape(np_rows, -1), yb, wple_b, wpg_b, bpg, npl, tp1)
    y_s = _post2(dest_s, h_s, rt_s, p_sample[0].reshape(ns_rows, -1), yb, wple_b, wpg_b, bpg, npl, ts1)

    kv5 = lambda a, b, s: a.reshape(1, b, s, N_KV_HEADS, HEAD_DIM)
    wp = min(WINDOW, t)
    win_p = lambda a: kv5(a.reshape(bp, t, KV_DIM)[:, t - wp:], bp, wp)
    win_s = lambda c, new: jnp.concatenate([c[0], new.reshape(bs, ts, N_KV_HEADS, HEAD_DIM)], axis=1)[None, :, ts:]
    conv_p = cs_p[:, 8 - (CONV_K - 1):][None]
    conv_s = u_s.reshape(bs, ts, CONV_DIM)[:, ts - (CONV_K - 1):][None]
    from_t = lambda a: jnp.transpose(a.reshape(bp, N_KV_HEADS, HEAD_DIM, t), (0, 3, 1, 2))[None]
    return (y_p.reshape(bp, t, d), y_s.reshape(bs, ts, d),
            from_t(kc_t), from_t(vc_t), from_t(ks_t), from_t(vs_t), win_p(kw_p), win_p(vw_p), conv_p,
            kv5(kc_s, bs, ts), kv5(vc_s, bs, ts), kv5(ks_s, bs, ts), kv5(vs_s, bs, ts),
            win_s(cache_k_win, kw_s), win_s(cache_v_win, vw_s), conv_s)
```

```python
import functools
import math

import numpy as np
import jax
import jax.numpy as jnp
from jax import lax
from jax.experimental import pallas as pl
from jax.experimental.pallas import tpu as pltpu

F32 = jnp.float32
BF16 = jnp.bfloat16
NEG_INF = float("-inf")
MASKED = -1e30

HEAD_DIM = 64
N_HEADS = 8
N_KV_HEADS = 2
GROUP = N_HEADS // N_KV_HEADS
CONV_DIM = 512
ATTN_DIM = 512
KV_DIM = N_KV_HEADS * HEAD_DIM
N_BRANCH = 3
CONV_K = 3
CMP_BLOCK = 32
CMP_STRIDE = 16
CMP_HIDDEN = 256
SLC_BLOCK = 64
N_SEL = 16
WINDOW = 512
Q_BLOCK = 128
FORCE_SCORE = 1e4
NUM_BUCKETS = 32
MAX_DISTANCE = 128
N_GROUPS = 4
EXPERTS_PER_GROUP = 8
N_EXPERTS = N_GROUPS * EXPERTS_PER_GROUP
D_EXPERT = 512
EPS = 1e-6

D_MODEL = 1024
LANES = 128
TOKEN_TILE_ROWS = D_MODEL // LANES
Z_COLS = 3 * CONV_DIM + ATTN_DIM + 6 * KV_DIM + LANES
BIAS_DMAX = 768
EXPERT_ROWS = 512
ROUTER_GROUP_LANE = 32
SCATTER_TOKENS = 512
DMA_ISSUE_UNROLL = 8
COMPRESS_SEQS = 4
CHUNK_PITCH = 24
SAMPLE_SEQS = 4
VMEM_LIMIT = 56 * 1024 * 1024


def _cparams(*sem):
    return pltpu.CompilerParams(dimension_semantics=sem, vmem_limit_bytes=VMEM_LIMIT)


def _dot(a, b):
    return jnp.dot(a, b, preferred_element_type=F32)


def _dot_nt(a, b):
    return lax.dot_general(a, b, (((1,), (1,)), ((), ())), preferred_element_type=F32)


def _split3(x):
    hi = x.astype(BF16)
    r = x - hi.astype(F32)
    mid = r.astype(BF16)
    lo = (r - mid.astype(F32)).astype(BF16)
    return hi, mid, lo


def _rms(x, g):
    return x * lax.rsqrt(jnp.mean(x * x, axis=-1, keepdims=True) + EPS) * g


def _head_rms(x, bd, g):
    hi, mid, _ = _split3(x * x)
    ss = _dot(hi, bd) + _dot(mid, bd)
    return x * lax.rsqrt(ss * (1.0 / HEAD_DIM) + EPS) * g


def _sigmoid(x):
    return 1.0 / (1.0 + jnp.exp(-x))


def _store_token_tiles(ref, x, n):
    r = x.shape[1] // LANES
    for j in range(r):
        ref[pl.ds(j, n, stride=r), :] = x[:, j * LANES:(j + 1) * LANES]


def _load_token_tiles(ref, n, r, first=0):
    return jnp.concatenate([ref[pl.ds(first + j, n, stride=r), :] for j in range(r)], axis=-1)


def _softmax_parts(s):
    m = jnp.max(s, axis=-1, keepdims=True)
    m = jnp.where(m == NEG_INF, 0.0, m)
    p = jnp.exp(s - m)
    l = jnp.sum(p, axis=-1, keepdims=True)
    return p, l


def _proj_body(sample, tm, *refs):
    if sample:
        (x_ref, nm_ref, w_ref, qn_ref, kn1_ref, kn2_ref, cw_ref, bd_ref, s0_ref, s1_ref,
         co_ref, q_ref, kc_ref, vc_ref, ks_ref, vs_ref, kw_ref, vw_ref, gt_ref, u_ref) = refs
    else:
        (x_ref, nm_ref, w_ref, qn_ref, kn1_ref, kn2_ref, cw_ref, bd_ref,
         co_ref, q_ref, kw_ref, vw_ref, gt_ref, cs_ref,
         ksb_ref, vsb_ref, kwb_ref, vwb_ref, kct_ref, vct_ref, kst_ref, vst_ref, kcx_ref, vcx_ref,
         carry_ref, stage_ref) = refs

    xn = _rms(x_ref[...], nm_ref[...]).astype(BF16)

    def seg(a, b):
        return _dot(xn, w_ref[:, a:b])

    c3 = 3 * CONV_DIM
    u = seg(2 * CONV_DIM, c3) * seg(0, CONV_DIM)
    bg = seg(CONV_DIM, 2 * CONV_DIM)
    row = lax.broadcasted_iota(jnp.int32, (tm, 1), 0)
    um1 = pltpu.roll(u, 1, axis=0)
    um2 = pltpu.roll(u, 2, axis=0)
    if sample:
        r = row & 3
        s0 = s0_ref[...]
        s1 = s1_ref[...]
        prev1 = jnp.where(r == 0, s1, um1)
        prev2 = jnp.where(r == 0, s0, jnp.where(r == 1, s1, um2))
        u_ref[...] = u
    else:
        @pl.when(pl.program_id(1) == 0)
        def _():
            carry_ref[...] = jnp.zeros_like(carry_ref)
        c = carry_ref[...]
        prev1 = jnp.where(row == 0, c[7:8], um1)
        prev2 = jnp.where(row == 0, c[6:7], jnp.where(row == 1, c[7:8], um2))
        carry_ref[...] = u[tm - 8:tm]
        cs_ref[0] = u[tm - 8:tm]
    cw = cw_ref[...]
    y = cw[0:1] * prev2 + cw[1:2] * prev1 + cw[2:3] * u
    co_ref[...] = (bg * y).astype(BF16)

    bd = bd_ref[...]
    q = _head_rms(seg(c3, c3 + ATTN_DIM), bd, qn_ref[...]) * (HEAD_DIM ** -0.5)
    lane = lax.broadcasted_iota(jnp.int32, (tm, LANES), 1)
    low = lane < HEAD_DIM

    def head_planes(x, fill):
        return [jnp.where(low, x if h == 0 else pltpu.roll(x, HEAD_DIM, axis=1), fill) for h in range(N_KV_HEADS)]

    if sample:
        q_ref[...] = q
    else:
        for hd in range(N_HEADS):
            pair = q[:, (hd // 2) * LANES:(hd // 2 + 1) * LANES]
            if hd % 2:
                pair = pltpu.roll(pair, HEAD_DIM, axis=1)
            q_ref[0, hd] = jnp.where(low, pair, 0.0).astype(BF16)

    k0 = c3 + ATTN_DIM
    bdk = bd[:KV_DIM, :KV_DIM]
    kc = seg(k0, k0 + KV_DIM)
    vc = seg(k0 + KV_DIM, k0 + 2 * KV_DIM)
    ks = _head_rms(seg(k0 + 2 * KV_DIM, k0 + 3 * KV_DIM), bdk, kn1_ref[...])
    vs = seg(k0 + 3 * KV_DIM, k0 + 4 * KV_DIM)
    kw = _head_rms(seg(k0 + 4 * KV_DIM, k0 + 5 * KV_DIM), bdk, kn2_ref[...])
    vw = seg(k0 + 5 * KV_DIM, k0 + 6 * KV_DIM)
    kw_ref[...] = kw
    vw_ref[...] = vw
    if sample:
        kc_ref[...] = kc
        vc_ref[...] = vc
        ks_ref[...] = ks
        vs_ref[...] = vs
    else:
        kct_ref[0] = kc.T
        vct_ref[0] = vc.T
        kst_ref[0] = ks.T
        vst_ref[0] = vs.T
        for src, dst in ((kc, kcx_ref), (vc, vcx_ref)):
            stage_ref[...] = src
            dst[...] = jnp.concatenate([stage_ref[pl.ds(r, tm // CMP_STRIDE, stride=CMP_STRIDE), :]
                                        for r in range(CMP_STRIDE)], axis=-1)
    gates = _sigmoid(seg(k0 + 6 * KV_DIM, k0 + 6 * KV_DIM + LANES))
    if sample:
        gt_ref[...] = gates
    else:
        pos = pl.program_id(1) * tm + lax.broadcasted_iota(jnp.int32, (tm, LANES), 0)
        block_onehot = jnp.where(lane - HEAD_DIM == (pos >> 6), 1.0, 0.0)
        for h, (a, b, c, e) in enumerate(zip(head_planes(ks, block_onehot), head_planes(vs, 1.0),
                                             head_planes(kw, 0.0), head_planes(vw, 1.0))):
            ksb_ref[h] = a.astype(BF16)
            vsb_ref[h] = b.astype(BF16)
            kwb_ref[h] = c.astype(BF16)
            vwb_ref[h] = e.astype(BF16)
        gt_ref[0] = gates
        gt_ref[1] = pltpu.roll(gates, LANES - GROUP * N_BRANCH, axis=1)


def _project(x2d, batch, seq, tm, weights, state=None):
    n, d = x2d.shape
    sample = state is not None
    nt = seq // tm if not sample else 1
    const = lambda shape: pl.BlockSpec(shape, lambda b, t: (0,) * len(shape))
    rows = lambda w: pl.BlockSpec((tm, w), lambda b, t: (b * nt + t, 0))
    nm, w_in, qn, kn1, kn2, cw, bd = weights
    in_specs = [rows(d), const(nm.shape), const(w_in.shape), const(qn.shape), const(kn1.shape),
                const(kn2.shape), const(cw.shape), const(bd.shape)]
    args = [x2d, nm, w_in, qn, kn1, kn2, cw, bd]
    kv_f32 = [jax.ShapeDtypeStruct((n, KV_DIM), F32)] * 6
    if sample:
        in_specs += [rows(CONV_DIM), rows(CONV_DIM)]
        args += list(state)
        out_shape = ([jax.ShapeDtypeStruct((n, CONV_DIM), BF16), jax.ShapeDtypeStruct((n, ATTN_DIM), F32)]
                     + kv_f32 + [jax.ShapeDtypeStruct((n, LANES), F32), jax.ShapeDtypeStruct((n, CONV_DIM), F32)])
        out_specs = [rows(CONV_DIM), rows(ATTN_DIM)] + [rows(KV_DIM)] * 6 + [rows(LANES), rows(CONV_DIM)]
        scratch = []
        grid = (1, 1)
    else:
        planes = pl.BlockSpec((N_KV_HEADS, tm, LANES), lambda b, t: (0, b * nt + t, 0))
        out_shape = ([jax.ShapeDtypeStruct((n, CONV_DIM), BF16),
                      jax.ShapeDtypeStruct((batch, N_HEADS, seq, LANES), BF16)]
                     + kv_f32[:2] + [jax.ShapeDtypeStruct((N_KV_HEADS, n, LANES), F32),
                                     jax.ShapeDtypeStruct((batch, 8, CONV_DIM), F32)]
                     + [jax.ShapeDtypeStruct((N_KV_HEADS, n, LANES), BF16)] * 4
                     + [jax.ShapeDtypeStruct((batch, KV_DIM, seq), F32)] * 4
                     + [jax.ShapeDtypeStruct((n // CMP_STRIDE, CMP_STRIDE * KV_DIM), F32)] * 2)
        out_specs = ([rows(CONV_DIM), pl.BlockSpec((1, N_HEADS, tm, LANES), lambda b, t: (b, 0, t, 0))]
                     + [rows(KV_DIM)] * 2 + [planes, pl.BlockSpec((1, 8, CONV_DIM), lambda b, t: (b, 0, 0))]
                     + [planes] * 4 + [pl.BlockSpec((1, KV_DIM, tm), lambda b, t: (b, 0, t))] * 4
                     + [pl.BlockSpec((tm // CMP_STRIDE, CMP_STRIDE * KV_DIM), lambda b, t: (b * nt + t, 0))] * 2)
        scratch = [pltpu.VMEM((8, CONV_DIM), F32), pltpu.VMEM((tm, KV_DIM), F32)]
        grid = (batch, nt)
    return pl.pallas_call(
        functools.partial(_proj_body, sample, tm),
        grid=grid, in_specs=in_specs, out_specs=out_specs, out_shape=out_shape, scratch_shapes=scratch,
        compiler_params=_cparams("arbitrary", "arbitrary"),
        name="proj_sample" if sample else "proj_prompt",
    )(*args)


def _gelu_tanh(x):
    cdf = 0.5 * (1.0 + jnp.tanh(math.sqrt(2.0 / math.pi) * (x + 0.044715 * (x * x * x))))
    return x * cdf


def _compress_core(norm, x, pe_ref, we_ref, w2_ref, g_ref):
    n = x.shape[0]
    a0 = _dot((x + pe_ref[0:1]).astype(BF16), we_ref[0])
    a1 = _dot((x + pe_ref[1:2]).astype(BF16), we_ref[1])
    hid = a0 + pltpu.roll(a1, n - 1, axis=0)
    w2 = w2_ref[...]
    outs = []
    for h in range(N_KV_HEADS):
        act = _gelu_tanh(hid[:, h * CMP_HIDDEN:(h + 1) * CMP_HIDDEN])
        o = _dot(act.astype(BF16), w2)
        if norm:
            o = _rms(o, g_ref[...])
        outs.append(o)
    return outs


def _compress_rows_body(norm, x_ref, pe_ref, we_ref, w2_ref, g_ref, o_ref):
    outs = _compress_core(norm, x_ref[0], pe_ref, we_ref, w2_ref, g_ref)
    for h in range(N_KV_HEADS):
        o_ref[0, h] = jnp.concatenate([outs[h], jnp.zeros_like(outs[h])], axis=-1).astype(BF16)


def _fetch_pages(pt_ref, cache_ref, buf_ref, sem_ref, step, slot, n_ops, priority):
    base = step * n_ops

    def issue(j, _):
        pltpu.make_async_copy(cache_ref.at[pt_ref[base + j]], buf_ref.at[slot, j], sem_ref.at[slot]).start(
            priority=priority)
        return 0

    lax.fori_loop(0, n_ops, issue, 0)


def _paged_prefetch(pt_ref, caches, bufs, sems, n_ops):
    i = pl.program_id(0)
    slot = i & 1
    for k, (cache_ref, buf_ref, sem_ref) in enumerate(zip(caches, bufs, sems)):
        @pl.when(i == 0)
        def _():
            _fetch_pages(pt_ref, cache_ref, buf_ref, sem_ref, 0, 0, n_ops, k % 2)

        @pl.when(i + 1 < pl.num_programs(0))
        def _():
            _fetch_pages(pt_ref, cache_ref, buf_ref, sem_ref, i + 1, 1 - slot, n_ops, k % 2)

    for cache_ref, buf_ref, sem_ref in zip(caches, bufs, sems):
        pltpu.make_async_copy(cache_ref.at[pl.ds(0, n_ops)], buf_ref.at[slot], sem_ref.at[slot]).wait()
    return slot


def _compress_pages_body(norm, nch, n_seq, n_pages, pt_ref, cache_ref, pe_ref, we_ref, w2_ref, g_ref, o_ref,
                         rows_ref, buf_ref, sem_ref):
    n_ops = n_seq * n_pages
    slot = _paged_prefetch(pt_ref, [cache_ref], [buf_ref], [sem_ref], n_ops)
    page = buf_ref.shape[3]
    per_page = page // CMP_STRIDE
    for j in range(n_ops):
        rows = buf_ref[slot, j].T
        for c in range(per_page):
            first = (j * per_page + c) * CHUNK_PITCH
            rows_ref[first:first + CMP_STRIDE, :] = rows[c * CMP_STRIDE:(c + 1) * CMP_STRIDE]
    x = jnp.concatenate([rows_ref[pl.ds(r, n_seq * nch, stride=CHUNK_PITCH), :] for r in range(CMP_STRIDE)], axis=-1)
    out = jnp.concatenate(_compress_core(norm, x, pe_ref, we_ref, w2_ref, g_ref), axis=-1)
    for b in range(n_seq):
        o_ref[b] = out[b * nch:(b + 1) * nch]


def _compress_weights(pe, w1, w2):
    w1r = w1.reshape(2, CMP_STRIDE, HEAD_DIM, CMP_HIDDEN).astype(BF16)
    z = jnp.zeros_like(w1r)
    we = jnp.stack([jnp.concatenate([w1r, z], axis=-1), jnp.concatenate([z, w1r], axis=-1)], axis=2)
    we = we.reshape(2, CMP_STRIDE * KV_DIM, N_KV_HEADS * CMP_HIDDEN)
    per = pe.reshape(2, CMP_STRIDE, 1, HEAD_DIM)
    pex = jnp.broadcast_to(per, (2, CMP_STRIDE, N_KV_HEADS, HEAD_DIM)).reshape(2, CMP_STRIDE * KV_DIM)
    return pex.astype(F32), we.astype(BF16), w2.astype(BF16)


def _compress_rows(rows3, cw, gain, norm):
    b, nch, width = rows3.shape
    pex, we, w2 = cw
    const = lambda a: pl.BlockSpec(a.shape, lambda i: (0,) * a.ndim)
    return pl.pallas_call(
        functools.partial(_compress_rows_body, norm),
        grid=(b,),
        in_specs=[pl.BlockSpec((1, nch, width), lambda i: (i, 0, 0)), const(pex), const(we), const(w2), const(gain)],
        out_specs=pl.BlockSpec((1, N_KV_HEADS, nch, LANES), lambda i: (i, 0, 0, 0)),
        out_shape=jax.ShapeDtypeStruct((b, N_KV_HEADS, nch, LANES), BF16),
        compiler_params=_cparams("arbitrary"),
        name="compress_rows",
    )(rows3, pex, we, w2, gain)


def _compress_pages(cache_t, pt_flat, n_batch, n_pages, cw, gain, norm):
    _, _, page = cache_t.shape
    nch = n_pages * page // CMP_STRIDE
    pex, we, w2 = cw
    n_seq = COMPRESS_SEQS if n_batch % COMPRESS_SEQS == 0 else 1
    const = lambda a: pl.BlockSpec(a.shape, lambda i, pt: (0,) * a.ndim)
    n_ops = n_seq * n_pages
    grid_spec = pltpu.PrefetchScalarGridSpec(
        num_scalar_prefetch=1, grid=(n_batch // n_seq,),
        in_specs=[pl.BlockSpec(memory_space=pl.ANY), const(pex), const(we), const(w2), const(gain)],
        out_specs=pl.BlockSpec((n_seq, nch, KV_DIM), lambda i, pt: (i, 0, 0)),
        scratch_shapes=[pltpu.VMEM((n_ops * (page // CMP_STRIDE) * CHUNK_PITCH, KV_DIM), F32),
                        pltpu.VMEM((2, n_ops, KV_DIM, page), F32),
                        pltpu.SemaphoreType.DMA((2,))])
    return pl.pallas_call(
        functools.partial(_compress_pages_body, norm, nch, n_seq, n_pages),
        grid_spec=grid_spec,
        out_shape=jax.ShapeDtypeStruct((n_batch, nch, KV_DIM), F32),
        compiler_params=_cparams("arbitrary"),
        name="compress_pages",
    )(pt_flat, cache_t, pex, we, w2, gain)


def _rel_bucket(dist):
    n = jnp.maximum(dist, 0)
    max_exact = NUM_BUCKETS // 2
    nf = jnp.maximum(n, 1).astype(F32)
    large = max_exact + (jnp.log(nf / max_exact) / math.log(MAX_DISTANCE / max_exact)
                         * (NUM_BUCKETS - max_exact)).astype(jnp.int32)
    large = jnp.minimum(large, NUM_BUCKETS - 1)
    return jnp.where(n < max_exact, n, large)


def _bias_by_distance(rel_bias):
    d = jnp.arange(BIAS_DMAX, dtype=jnp.int32)
    f = rel_bias.astype(F32)[_rel_bucket(d)]
    f = (f - f[BIAS_DMAX - 1:BIAS_DMAX]).T
    return jnp.concatenate([f, jnp.full((N_HEADS, 1), NEG_INF, F32)], axis=1)


def _bias_index(d, valid):
    return np.where(valid, np.clip(d, 0, BIAS_DMAX - 1), BIAS_DMAX).astype(np.int32)


def _bias_table(fext, d, valid):
    return jnp.take(fext, jnp.asarray(_bias_index(d, valid)), axis=1)


def _toeplitz_body(n_rows, v_ref, o_ref):
    x = jnp.broadcast_to(v_ref[0], (n_rows, v_ref.shape[2]))
    o_ref[0] = pltpu.roll(x, 0, axis=1, stride=1, stride_axis=0)


def _toeplitz_rows(v, n_rows):
    h, w = v.shape
    return pl.pallas_call(
        functools.partial(_toeplitz_body, n_rows),
        grid=(h,),
        in_specs=[pl.BlockSpec((1, 1, w), lambda i: (i, 0, 0))],
        out_specs=pl.BlockSpec((1, n_rows, w), lambda i: (i, 0, 0)),
        out_shape=jax.ShapeDtypeStruct((h, n_rows, w), F32),
        compiler_params=_cparams("arbitrary"),
        name="toeplitz_rows",
    )(v.reshape(h, 1, w))


def _select_blocks(imp_t, srow, qpos, n_rank):
    qblk = qpos >> 6
    forced = (srow == 0) | (srow == qblk) | (srow == qblk - 1)
    valid = (srow << 6) <= qpos
    imp_t = jnp.where(valid, imp_t + jnp.where(forced, FORCE_SCORE, 0.0), NEG_INF)
    n_rows = imp_t.shape[0]
    assert n_rows % 8 == 0
    slabs = [imp_t[a:a + 8] for a in range(0, n_rows, 8)]
    rows8 = [srow[a:a + 8] for a in range(0, n_rows, 8)]
    cnts = [jnp.zeros(x.shape, jnp.int32) for x in slabs]
    for s in range(n_rank):
        r = imp_t[s:s + 1, :]
        for j, x in enumerate(slabs):
            if 8 * j > s:
                beats = r >= x
            elif 8 * j + 7 <= s:
                beats = r > x
            else:
                beats = (r > x) | ((r == x) & (rows8[j] > s))
            cnts[j] = cnts[j] + jnp.where(beats, 1, 0)
    cnt = jnp.concatenate(cnts, axis=0)
    return jnp.where((cnt < N_SEL) & valid, 1.0, 0.0)


def _exp_pv(s, m, v):
    return _dot(jnp.exp((s - m).astype(BF16)), v)


def _normalize_pv(pv):
    return pv / jnp.maximum(pv[:, HEAD_DIM:HEAD_DIM + 1], 1e-30)


def _attn_prompt_body(kt, n_slc, q_ref, gt_ref, kcmp_ref, vcmp_ref, ks_ref, vs_ref, kw_ref, vw_ref,
                      ctab_ref, ntab_ref, wtab_ref, ovl_ref, o_ref, sa_ref, sb_ref):
    i = pl.program_id(2)
    qb = Q_BLOCK
    rows = GROUP * qb
    n_cmp_pad = kcmp_ref.shape[2]
    first_near_block = (qb // SLC_BLOCK) * jnp.maximum(i - 1, 0)
    near_start = pl.multiple_of(jnp.maximum(i - 1, 0) * qb, qb)
    win_start = pl.multiple_of(jnp.maximum(i * qb - WINDOW, 0), qb)
    band = WINDOW + qb
    gates = gt_ref[0]
    lane = lax.broadcasted_iota(jnp.int32, (qb, LANES), 1)
    q0 = q_ref[0].reshape(rows, LANES)

    s = _dot_nt(q0, kw_ref[0, pl.ds(win_start, band), :]) + wtab_ref[0].reshape(rows, band)
    m = jnp.max(s, axis=-1, keepdims=True)
    m = jnp.where(m == NEG_INF, 0.0, m)
    o_w = _normalize_pv(_exp_pv(s, m, vw_ref[0, pl.ds(win_start, band), :]))

    per_qb = qb // CMP_STRIDE
    ctab = pltpu.roll(ctab_ref[...].reshape(rows, 2 * n_cmp_pad), i * per_qb, axis=1)[:, n_cmp_pad:]
    s = _dot_nt(q0, kcmp_ref[0, 0]) + ctab
    p, l = _softmax_parts(s)
    pn = p / jnp.maximum(l, 1e-30)
    o_c = _dot(pn.astype(BF16), vcmp_ref[0, 0])

    psum = pn[0:qb] + pn[qb:2 * qb] + pn[2 * qb:3 * qb] + pn[3 * qb:4 * qb]
    hi, mid, lo = _split3(psum)
    ovl = ovl_ref[...]
    imp_t = _dot_nt(ovl, hi) + _dot_nt(ovl, mid) + _dot_nt(ovl, lo)
    srow = lax.broadcasted_iota(jnp.int32, (n_slc, qb), 0)
    qpos_t = i * qb + lax.broadcasted_iota(jnp.int32, (n_slc, qb), 1)
    sel_t = _select_blocks(imp_t, srow, qpos_t, n_slc)
    sel_t = jnp.concatenate([sel_t, jnp.zeros((LANES - n_slc, qb), F32)], axis=0)
    sel = sel_t.T

    def query_with_mask(keep):
        m = pltpu.roll(jnp.where(keep, 0.0, MASKED), HEAD_DIM, axis=1).astype(BF16)
        m = jnp.concatenate([jnp.where(lane < HEAD_DIM, q0[g * qb:(g + 1) * qb], m) for g in range(GROUP)], axis=0)
        return m

    q_near = query_with_mask(sel > 0.5)
    q_far = query_with_mask((sel > 0.5) & (lane < first_near_block))

    s = _dot_nt(q_near, ks_ref[0, pl.ds(near_start, 2 * qb), :]) + ntab_ref[0].reshape(rows, 2 * qb)
    m0 = jnp.max(s, axis=-1, keepdims=True)
    m0 = jnp.where(m0 == NEG_INF, 0.0, m0)
    a0 = _exp_pv(s, m0, vs_ref[0, pl.ds(near_start, 2 * qb), :])

    n_kt = ks_ref.shape[1] // kt

    def scores(t):
        k0 = pl.multiple_of(jnp.minimum(t, n_kt - 1) * kt, kt)
        return _dot_nt(q_far, ks_ref[0, pl.ds(k0, kt), :])

    def consume(t, s, m_old, acc):
        k0 = pl.multiple_of(jnp.minimum(t, n_kt - 1) * kt, kt)
        m_new = jnp.maximum(m_old, jnp.max(s, axis=-1, keepdims=True))
        return m_new, jnp.exp(m_old - m_new) * acc + _exp_pv(s, m_new, vs_ref[0, pl.ds(k0, kt), :])

    def far_pair(u, carry):
        m, acc = carry
        sb_ref[...] = scores(2 * u + 1)
        m, acc = consume(2 * u, sa_ref[...], m, acc)
        sa_ref[...] = scores(2 * u + 2)
        return consume(2 * u + 1, sb_ref[...], m, acc)

    n_far = (near_start + kt - 1) // kt
    sa_ref[...] = scores(0)
    _, acc_s = lax.fori_loop(0, (n_far + 1) // 2, far_pair, (m0, a0))
    o_s = _normalize_pv(acc_s)

    heads_out = []
    for g in range(GROUP):
        c = g * N_BRANCH
        sl = slice(g * qb, (g + 1) * qb)
        heads_out.append(gates[:, c:c + 1] * o_c[sl] + gates[:, c + 1:c + 2] * o_s[sl] + gates[:, c + 2:c + 3] * o_w[sl])
    tiles = [jnp.where(lane < HEAD_DIM, heads_out[2 * j], pltpu.roll(heads_out[2 * j + 1], HEAD_DIM, axis=1))
             for j in range(GROUP // 2)]
    o_ref[...] = jnp.concatenate(tiles, axis=-1).astype(BF16)


def _attn_prompt(qp, gates, kcmp, vcmp, ksb, vsb, kwb, vwb, fext, batch, seq):
    qb = Q_BLOCK
    nqb = seq // qb
    n_slc = seq // SLC_BLOCK
    assert n_slc <= LANES - HEAD_DIM
    n_cmp_pad = kcmp.shape[2]
    n_cmp = n_cmp_pad - 1
    kt = min(512, seq)
    assert (seq // kt) % 2 == 0
    band = WINDOW + qb
    iq = np.arange(qb)

    per_qb = qb // CMP_STRIDE
    half = 2 * per_qb
    m = np.arange(-half, half)
    d = iq[:, None] - (m[None, :] * CMP_STRIDE + CMP_BLOCK - 1)
    assert d[:, 0].min() >= MAX_DISTANCE and d[:, -1].max() < 0
    ctab = jnp.concatenate([jnp.zeros((N_HEADS, qb, n_cmp_pad - half), F32), _bias_table(fext, d, d >= 0),
                            jnp.full((N_HEADS, qb, n_cmp_pad - half), NEG_INF, F32)], axis=2)
    assert per_qb * (nqb - 1) < n_cmp_pad
    assert (n_cmp_pad - 1) * CMP_STRIDE + CMP_BLOCK - 1 >= seq and n_cmp == n_cmp_pad - 1

    nv = WINDOW // qb + 1
    kw_ = WINDOW + band
    w = qb + kw_
    assert w % LANES == 0
    m = np.arange(w)
    m = np.where(m < kw_, m, m - w)
    dj = WINDOW - m
    wide = _toeplitz_rows(_bias_table(fext, dj, (dj >= 0) & (dj < WINDOW)), qb)
    wtab = jnp.stack([wide[:, :, WINDOW - qb * v:WINDOW - qb * v + band] for v in range(nv)], axis=0)
    ntab = jnp.stack([wide[:, :, WINDOW - qb * v:WINDOW - qb * v + 2 * qb] for v in range(2)], axis=0)
    assert 2 * qb <= WINDOW

    c_start = np.arange(n_cmp_pad) * CMP_STRIDE
    s_start = np.arange(n_slc) * SLC_BLOCK
    ovl = ((c_start[None, :] < s_start[:, None] + SLC_BLOCK) & (c_start[None, :] + CMP_BLOCK > s_start[:, None])
           & (np.arange(n_cmp_pad) < n_cmp)[None, :])
    ovl = jnp.asarray(ovl, BF16)

    per_head = lambda: pl.BlockSpec((1, seq, LANES), lambda b, h, i: (h, b, 0))
    return pl.pallas_call(
        functools.partial(_attn_prompt_body, kt, n_slc),
        grid=(batch, N_KV_HEADS, nqb),
        in_specs=[pl.BlockSpec((1, GROUP, qb, LANES), lambda b, h, i: (b, h, i, 0)),
                  pl.BlockSpec((1, qb, LANES), lambda b, h, i: (h, b * nqb + i, 0)),
                  pl.BlockSpec((1, 1, n_cmp_pad, LANES), lambda b, h, i: (b, h, 0, 0)),
                  pl.BlockSpec((1, 1, n_cmp_pad, LANES), lambda b, h, i: (b, h, 0, 0)),
                  per_head(), per_head(), per_head(), per_head(),
                  pl.BlockSpec((GROUP, qb, 2 * n_cmp_pad), lambda b, h, i: (h, 0, 0)),
                  pl.BlockSpec((1, GROUP, qb, 2 * qb), lambda b, h, i: (jnp.minimum(i, 1), h, 0, 0)),
                  pl.BlockSpec((1, GROUP, qb, band), lambda b, h, i: (jnp.minimum(i, nv - 1), h, 0, 0)),
                  pl.BlockSpec(ovl.shape, lambda b, h, i: (0, 0))],
        out_specs=pl.BlockSpec((qb, GROUP * HEAD_DIM), lambda b, h, i: (b * nqb + i, h)),
        out_shape=jax.ShapeDtypeStruct((batch * seq, ATTN_DIM), BF16),
        scratch_shapes=[pltpu.VMEM((GROUP * qb, kt), F32)] * 2,
        compiler_params=_cparams("arbitrary", "arbitrary", "arbitrary"),
        name="attn_prompt",
    )(qp, gates, kcmp, vcmp, ksb, vsb, kwb, vwb, ctab, ntab, wtab, ovl)


def _attn_sample_body(n_seq, n_pages, ts, past_len, pt_ref, cache_k_ref, cache_v_ref, *refs):
    kbuf, vbuf, ksem, vsem = refs[-4:]
    n_ops = n_seq * n_pages
    slot = _paged_prefetch(pt_ref, [cache_k_ref, cache_v_ref], [kbuf, vbuf], [ksem, vsem], n_ops)
    for b in range(n_seq):
        _attn_sample_one(b, [kbuf.at[slot, b * n_pages + j] for j in range(n_pages)],
                         [vbuf.at[slot, b * n_pages + j] for j in range(n_pages)], refs[:-4], ts, past_len)


def _attn_sample_one(b, kpages, vpages, refs, ts, past_len):
    (q_ref, gt_ref, kcmp_ref, vcmp_ref, ksn_ref, vsn_ref, kwc_ref, vwc_ref, kwn_ref, vwn_ref,
     ctab_ref, stab_ref, sntab_ref, wtab_ref, wntab_ref, ovl_ref, o_ref) = refs
    rows = GROUP * N_KV_HEADS * ts
    rq = N_KV_HEADS * ts
    q = q_ref[b]
    gates = gt_ref[b]

    s = _dot_nt(q, kcmp_ref[b].astype(BF16)) + ctab_ref[...]
    p, l = _softmax_parts(s)
    pn = p / jnp.maximum(l, 1e-30)
    o_c = _dot(pn.astype(BF16), vcmp_ref[b].astype(BF16))

    psum = pn[0:rq]
    for g in range(1, GROUP):
        psum = psum + pn[g * rq:(g + 1) * rq]
    hi, mid, lo = _split3(psum)
    ovl = ovl_ref[...]
    imp = _dot(hi, ovl) + _dot(mid, ovl) + _dot(lo, ovl)
    n_slc = -(-(past_len + ts) // SLC_BLOCK)
    blk = lax.broadcasted_iota(jnp.int32, (rq, LANES), 1)
    qpos = past_len + (lax.broadcasted_iota(jnp.int32, (rq, LANES), 0) & (ts - 1))
    qblk = qpos >> 6
    forced = (blk == 0) | (blk == qblk) | (blk == qblk - 1)
    valid = ((blk << 6) <= qpos) & (blk < n_slc)
    imp = jnp.where(valid, imp + jnp.where(forced, FORCE_SCORE, 0.0), NEG_INF)
    cnt = jnp.zeros((rq, LANES), jnp.int32)
    for sidx in range(n_slc):
        r = imp[:, sidx:sidx + 1]
        beats = (r > imp) | ((r == imp) & (blk > sidx))
        cnt = cnt + jnp.where(beats, 1, 0)
    sel = jnp.where((cnt < N_SEL) & valid, 1.0, 0.0)
    sel = jnp.concatenate([sel] * GROUP, axis=0)

    kc_t = jnp.concatenate([p_[...] for p_ in kpages], axis=1).astype(BF16)
    vc_t = jnp.concatenate([p_[...] for p_ in vpages], axis=1).astype(BF16)
    lane_r = lax.broadcasted_iota(jnp.int32, (rows, LANES), 1)
    per_tile = LANES // SLC_BLOCK
    tiles = []
    for j in range(past_len // LANES):
        m = sel[:, per_tile * j:per_tile * j + 1]
        for c in range(1, per_tile):
            m = jnp.where(lane_r >= c * SLC_BLOCK, sel[:, per_tile * j + c:per_tile * j + c + 1], m)
        tiles.append(m)
    mexp = jnp.concatenate(tiles, axis=1)
    s1 = jnp.where(mexp > 0.5, _dot(q, kc_t) + stab_ref[...], NEG_INF)
    last = sel[:, n_slc - 1:n_slc]
    s2 = jnp.where(last > 0.5, _dot_nt(q, ksn_ref[b].astype(BF16)) + sntab_ref[...], NEG_INF)
    m = jnp.maximum(jnp.max(s1, axis=-1, keepdims=True), jnp.max(s2, axis=-1, keepdims=True))
    m = jnp.where(m == NEG_INF, 0.0, m)
    p1 = jnp.exp(s1 - m)
    p2 = jnp.exp(s2 - m)
    l = jnp.sum(p1, axis=-1, keepdims=True) + jnp.sum(p2, axis=-1, keepdims=True)
    o_s = (_dot_nt(p1.astype(BF16), vc_t) + _dot(p2.astype(BF16), vsn_ref[b].astype(BF16))) / jnp.maximum(l, 1e-30)

    s1 = _dot(q, kwc_ref[b].astype(BF16)) + wtab_ref[...]
    s2 = _dot_nt(q, kwn_ref[b].astype(BF16)) + wntab_ref[...]
    m = jnp.maximum(jnp.max(s1, axis=-1, keepdims=True), jnp.max(s2, axis=-1, keepdims=True))
    m = jnp.where(m == NEG_INF, 0.0, m)
    p1 = jnp.exp(s1 - m)
    p2 = jnp.exp(s2 - m)
    l = jnp.sum(p1, axis=-1, keepdims=True) + jnp.sum(p2, axis=-1, keepdims=True)
    o_w = (_dot_nt(p1.astype(BF16), vwc_ref[b].astype(BF16))
           + _dot(p2.astype(BF16), vwn_ref[b].astype(BF16))) / jnp.maximum(l, 1e-30)

    o_ref[b] = gates[:, 0:1] * o_c + gates[:, 1:2] * o_s + gates[:, 2:3] * o_w


def _attn_sample(q_s, gates_s, kcmp, vcmp, ks_new, vs_new, kw_new, vw_new, cache_ks, cache_vs,
                 cache_kw, cache_vw, pt_flat, fext, n_batch, ts, n_pages, page):
    past_len = n_pages * page
    w_buf = cache_kw.shape[2]
    rows = GROUP * N_KV_HEADS * ts
    n_new = 8
    n_cmp_pad = kcmp.shape[1]
    n_cmp = n_cmp_pad - 1
    n_slc = -(-(past_len + ts) // SLC_BLOCK)

    q5 = q_s.reshape(n_batch, ts, N_KV_HEADS, GROUP, HEAD_DIM).transpose(0, 3, 2, 1, 4)
    eye = jnp.eye(N_KV_HEADS, dtype=q_s.dtype)
    qr = jnp.einsum("bghtd,hk->bghtkd", q5, eye).reshape(n_batch, rows, LANES).astype(BF16)
    g5 = gates_s[:, :N_HEADS * N_BRANCH].reshape(n_batch, ts, N_KV_HEADS, GROUP, N_BRANCH).transpose(0, 3, 2, 1, 4)
    gr = jnp.pad(g5.reshape(n_batch, rows, N_BRANCH), ((0, 0), (0, 0), (0, LANES - N_BRANCH)))
    pad_new = lambda a: jnp.pad(a.reshape(n_batch, ts, KV_DIM), ((0, 0), (0, n_new - ts), (0, 0)))
    ks_new, vs_new, kw_new, vw_new = map(pad_new, (ks_new, vs_new, kw_new, vw_new))

    g_i, h_i, t_i = np.meshgrid(np.arange(GROUP), np.arange(N_KV_HEADS), np.arange(ts), indexing="ij")
    head = (h_i * GROUP + g_i).reshape(rows)
    tq = t_i.reshape(rows)
    pos_q = past_len + tq

    f_rows = fext[jnp.asarray(head)]

    def table(d, valid):
        return jnp.take_along_axis(f_rows, jnp.asarray(_bias_index(d, valid)), axis=1)

    nn = np.arange(n_cmp_pad)
    d = pos_q[:, None] - (nn[None, :] * CMP_STRIDE + CMP_BLOCK - 1)
    ctab = table(d, (d >= 0) & (nn < n_cmp)[None, :])
    near = np.arange(past_len - MAX_DISTANCE, past_len)
    d = pos_q[:, None] - near[None, :]
    assert past_len >= MAX_DISTANCE and d.min() >= 0
    stab = jnp.concatenate([jnp.zeros((rows, past_len - MAX_DISTANCE), F32), table(d, d >= 0)], axis=1)
    jn = np.arange(n_new)
    d = tq[:, None] - jn[None, :]
    sntab = table(d, (d >= 0) & (jn < ts)[None, :])
    pos_w = past_len - w_buf + np.arange(w_buf)
    d = pos_q[:, None] - pos_w[None, :]
    wtab = table(d, (d >= 0) & (d < WINDOW) & (pos_w >= 0)[None, :])
    wntab = table(tq[:, None] - jn[None, :], (tq[:, None] >= jn[None, :]) & (jn < ts)[None, :])

    c_start = nn * CMP_STRIDE
    s_start = np.arange(LANES) * SLC_BLOCK
    ovl = jnp.asarray((c_start[:, None] < s_start[None, :] + SLC_BLOCK) & (c_start[:, None] + CMP_BLOCK > s_start[None, :])
                      & (nn < n_cmp)[:, None] & (np.arange(LANES) < n_slc)[None, :], BF16)

    n_seq = SAMPLE_SEQS if n_batch % SAMPLE_SEQS == 0 else 1
    n_ops = n_seq * n_pages
    const = lambda a: pl.BlockSpec(a.shape, lambda b, pt: (0,) * a.ndim)
    per_b = lambda a: pl.BlockSpec((n_seq,) + a.shape[1:], lambda b, pt: (b,) + (0,) * (a.ndim - 1))
    any_spec = pl.BlockSpec(memory_space=pl.ANY)
    small = [qr, gr, kcmp, vcmp, ks_new, vs_new, cache_kw, cache_vw, kw_new, vw_new]
    consts = [ctab, stab, sntab, wtab, wntab, ovl]
    page_buf = pltpu.VMEM((2, n_ops, KV_DIM, page), F32)
    grid_spec = pltpu.PrefetchScalarGridSpec(
        num_scalar_prefetch=1, grid=(n_batch // n_seq,),
        in_specs=[any_spec, any_spec] + [per_b(a) for a in small] + [const(a) for a in consts],
        out_specs=pl.BlockSpec((n_seq, rows, LANES), lambda b, pt: (b, 0, 0)),
        scratch_shapes=[page_buf, page_buf, pltpu.SemaphoreType.DMA((2,)), pltpu.SemaphoreType.DMA((2,))])
    o = pl.pallas_call(
        functools.partial(_attn_sample_body, n_seq, n_pages, ts, past_len),
        grid_spec=grid_spec,
        out_shape=jax.ShapeDtypeStruct((n_batch, rows, LANES), F32),
        compiler_params=_cparams("arbitrary"),
        name="attn_sample",
    )(pt_flat, cache_ks, cache_vs, *small, *consts)
    o6 = o.reshape(n_batch, GROUP, N_KV_HEADS, ts, N_KV_HEADS, HEAD_DIM)
    o5 = jnp.stack([o6[:, :, h, :, h] for h in range(N_KV_HEADS)], axis=2)
    return o5.transpose(0, 3, 2, 1, 4).reshape(n_batch * ts, ATTN_DIM).astype(BF16)


def _post1_body(tm, x_ref, co_ref, at_ref, wo_ref, nf_ref, wr_ref, br_ref, tri_ref, run0_ref,
                h_ref, hn_ref, rt_ref, cnt_ref, run_ref):
    @pl.when(pl.program_id(0) == 0)
    def _():
        run_ref[...] = run0_ref[...]

    h = x_ref[...] + _dot(co_ref[...], wo_ref[0:CONV_DIM]) + _dot(at_ref[...], wo_ref[CONV_DIM:CONV_DIM + ATTN_DIM])
    hn = _rms(h, nf_ref[...])
    h_ref[...] = h
    _store_token_tiles(hn_ref, hn, tm)

    hi = hn.astype(BF16)
    lo = (hn - hi.astype(F32)).astype(BF16)
    wr = wr_ref[...]
    whi = wr.astype(BF16)
    wlo = (wr - whi.astype(F32)).astype(BF16)
    both = _dot(hi, jnp.concatenate([whi, wlo], axis=1))
    logits = both[:, :LANES] + both[:, LANES:] + _dot(lo, whi) + br_ref[...]

    lane_i = lax.broadcasted_iota(jnp.int32, (tm, LANES), 1)
    lane = lane_i.astype(F32)
    big = float(LANES)
    gmask = (lane_i >= ROUTER_GROUP_LANE) & (lane_i < ROUTER_GROUP_LANE + N_GROUPS)
    lg = jnp.where(gmask, logits, NEG_INF)
    eg = jnp.exp(lg - jnp.max(lg, axis=-1, keepdims=True))
    pg = eg / jnp.sum(eg, axis=-1, keepdims=True)
    gw = jnp.max(pg, axis=-1, keepdims=True)
    grp = jnp.min(jnp.where(gmask & (pg == gw), lane, big), axis=-1, keepdims=True) - ROUTER_GROUP_LANE

    group_of_lane = (lane_i >> 3).astype(F32)
    emask = (lane_i < N_EXPERTS) & (group_of_lane == grp)
    le = jnp.where(emask, logits, NEG_INF)
    ee = jnp.exp(le - jnp.max(le, axis=-1, keepdims=True))
    pe = jnp.where(emask, ee / jnp.sum(ee, axis=-1, keepdims=True), -1.0)
    v1 = jnp.max(pe, axis=-1, keepdims=True)
    i1 = jnp.min(jnp.where(pe == v1, lane, big), axis=-1, keepdims=True)
    pe2 = jnp.where(lane == i1, -1.0, pe)
    v2 = jnp.max(pe2, axis=-1, keepdims=True)
    i2 = jnp.min(jnp.where(pe2 == v2, lane, big), axis=-1, keepdims=True)
    tot = v1 + v2
    w1 = v1 / tot * gw
    w2 = v2 / tot * gw

    oh1 = jnp.where(lane == i1, 1.0, 0.0)
    oh2 = jnp.where(lane == i2, 1.0, 0.0)
    both = oh1 + oh2
    before = _dot(tri_ref[...], both.astype(BF16)) + run_ref[0:1]
    r1 = jnp.sum(oh1 * before, axis=-1, keepdims=True)
    r2 = jnp.sum(oh2 * before, axis=-1, keepdims=True)
    run = run_ref[0:1] + jnp.sum(both, axis=0, keepdims=True)
    run_ref[...] = jnp.broadcast_to(run, run_ref.shape)
    cnt_ref[...] = jnp.broadcast_to(run, cnt_ref.shape)

    rt = jnp.where(lane_i == 0, i1, 0.0)
    rt = jnp.where(lane_i == 1, i2, rt)
    rt = jnp.where(lane_i == 2, r1, rt)
    rt = jnp.where(lane_i == 3, r2, rt)
    rt = jnp.where(lane_i == 4, w1, rt)
    rt = jnp.where(lane_i == 5, w2, rt)
    rt_ref[...] = rt


def _post1(x2d, co, at, wo, nf, wr, br, run0, tm):
    n, d = x2d.shape
    tri = jnp.asarray(np.tril(np.ones((tm, tm), np.float32), -1), BF16)
    rows = lambda w: pl.BlockSpec((tm, w), lambda i: (i, 0))
    const = lambda a: pl.BlockSpec(a.shape, lambda i: (0,) * a.ndim)
    return pl.pallas_call(
        functools.partial(_post1_body, tm),
        grid=(n // tm,),
        in_specs=[rows(d), rows(CONV_DIM), rows(ATTN_DIM), const(wo), const(nf), const(wr), const(br),
                  const(tri), const(run0)],
        out_specs=[rows(d), pl.BlockSpec((tm * TOKEN_TILE_ROWS, LANES), lambda i: (i, 0)), rows(LANES),
                   pl.BlockSpec((8, LANES), lambda i: (0, 0))],
        out_shape=[jax.ShapeDtypeStruct((n, d), F32), jax.ShapeDtypeStruct((n * TOKEN_TILE_ROWS, LANES), F32),
                   jax.ShapeDtypeStruct((n, LANES), F32), jax.ShapeDtypeStruct((8, LANES), F32)],
        scratch_shapes=[pltpu.VMEM((8, LANES), F32)],
        compiler_params=_cparams("arbitrary"),
        name="post1",
    )(x2d, co, at, wo, nf, wr, br, tri, run0)


def _token_copy(src_ref, dst_ref, s, d, sem):
    r = TOKEN_TILE_ROWS
    return pltpu.make_async_copy(src_ref.at[pl.ds(pl.multiple_of(s * r, r), r)],
                                 dst_ref.at[pl.ds(pl.multiple_of(d * r, r), r)], sem)


def _scatter_rows_body(ts, dest_ref, src_ref, init_ref, out_ref, sem):
    del init_ref
    base = pl.program_id(0) * (2 * ts)

    def issue(t, _):
        for k in range(2):
            _token_copy(src_ref, out_ref, t, dest_ref[base + 2 * t + k], sem).start(priority=k)
        return 0

    lax.fori_loop(0, ts, issue, 0, unroll=DMA_ISSUE_UNROLL)
    for _ in range(2):
        pltpu.make_async_copy(src_ref, out_ref.at[pl.ds(0, ts * TOKEN_TILE_ROWS)], sem).wait()


def _scatter_rows(dest, src, slots):
    n_tok = dest.shape[0] // 2
    ts = min(SCATTER_TOKENS, n_tok)
    assert n_tok % ts == 0
    any_spec = pl.BlockSpec(memory_space=pl.ANY)
    return pl.pallas_call(
        functools.partial(_scatter_rows_body, ts),
        grid_spec=pltpu.PrefetchScalarGridSpec(
            num_scalar_prefetch=1, grid=(n_tok // ts,),
            in_specs=[pl.BlockSpec((ts * TOKEN_TILE_ROWS, LANES), lambda i, dest: (i, 0)), any_spec],
            out_specs=any_spec, scratch_shapes=[pltpu.SemaphoreType.DMA(())]),
        out_shape=jax.ShapeDtypeStruct(slots.shape, slots.dtype),
        input_output_aliases={2: 0},
        compiler_params=pltpu.CompilerParams(dimension_semantics=("arbitrary",)),
        name="scatter_rows",
    )(dest, src, slots)


def _experts_body(be_ref, nu_ref, x_ref, wg_ref, wu_ref, wd_ref, o_ref, wg_s, wu_s, wd_s):
    i = pl.program_id(0)

    @pl.when(i < nu_ref[0])
    def _():
        prev = be_ref[jnp.maximum(i - 1, 0)]

        @pl.when((i == 0) | (be_ref[i] != prev))
        def _():
            wg_s[...] = wg_ref[0].astype(BF16)
            wu_s[...] = wu_ref[0].astype(BF16)
            wd_s[...] = wd_ref[0].astype(BF16)

        x = _load_token_tiles(x_ref, EXPERT_ROWS, TOKEN_TILE_ROWS).astype(BF16)
        g = _dot(x, wg_s[...])
        u = _dot(x, wu_s[...])
        a = g * _sigmoid(g) * u
        _store_token_tiles(o_ref, _dot(a.astype(BF16), wd_s[...]), EXPERT_ROWS)

    @pl.when(i >= nu_ref[0])
    def _():
        o_ref[...] = jnp.zeros_like(o_ref)


def _experts(blk_expert, n_used, xs, wg, wu, wd):
    blk_rows = EXPERT_ROWS * TOKEN_TILE_ROWS
    n_blk = xs.shape[0] // blk_rows
    _, d, de = wg.shape
    xmap = lambda i, be, nu: (jnp.minimum(i, jnp.maximum(nu[0] - 1, 0)), 0)
    wmap = lambda i, be, nu: (be[jnp.minimum(i, jnp.maximum(nu[0] - 1, 0))], 0, 0)
    grid_spec = pltpu.PrefetchScalarGridSpec(
        num_scalar_prefetch=2, grid=(n_blk,),
        in_specs=[pl.BlockSpec((blk_rows, LANES), xmap), pl.BlockSpec((1, d, de), wmap),
                  pl.BlockSpec((1, d, de), wmap), pl.BlockSpec((1, de, d), wmap)],
        out_specs=pl.BlockSpec((blk_rows, LANES), lambda i, be, nu: (i, 0)),
        scratch_shapes=[pltpu.VMEM((d, de), BF16), pltpu.VMEM((d, de), BF16), pltpu.VMEM((de, d), BF16)])
    return pl.pallas_call(
        _experts_body, grid_spec=grid_spec,
        out_shape=jax.ShapeDtypeStruct(xs.shape, F32),
        compiler_params=_cparams("arbitrary"),
        name="experts",
    )(blk_expert, n_used, xs, wg, wu, wd)


def _post2_body(tm, dest_ref, h_ref, rt_ref, p_ref, yb_ref, wple_ref, wpg_ref, bpg_ref, np_ref, o_ref, buf, sem):
    i = pl.program_id(0)
    n = pl.num_programs(0)

    def fetch(step, slot):
        base = step * (2 * tm)

        def issue(t, _):
            for k in range(2):
                _token_copy(yb_ref, buf.at[slot], dest_ref[base + 2 * t + k], k * tm + t, sem.at[slot]).start(priority=k)
            return 0

        lax.fori_loop(0, tm, issue, 0, unroll=DMA_ISSUE_UNROLL)

    @pl.when(i == 0)
    def _():
        fetch(0, 0)

    @pl.when(i + 1 < n)
    def _():
        fetch(i + 1, (i + 1) & 1)

    slot = i & 1

    pltpu.make_async_copy(yb_ref.at[pl.ds(0, 2 * tm * TOKEN_TILE_ROWS)], buf.at[slot], sem.at[slot]).wait()
    rt = rt_ref[...]
    y0 = _load_token_tiles(buf.at[slot], tm, TOKEN_TILE_ROWS)
    y1 = _load_token_tiles(buf.at[slot], tm, TOKEN_TILE_ROWS, first=tm * TOKEN_TILE_ROWS)
    h = h_ref[...] + (y0 * rt[:, 4:5] + y1 * rt[:, 5:6])
    gate = _sigmoid(_dot(_rms(h, np_ref[...]).astype(BF16), wpg_ref[...]) + bpg_ref[...])
    o_ref[...] = h + gate * _dot(p_ref[...].astype(BF16), wple_ref[...])


def _post2(dest, h, rt, p2d, yb, wple, wpg, bpg, npl, tm):
    n, d = h.shape
    rows = lambda w: pl.BlockSpec((tm, w), lambda i, dest: (i, 0))
    const = lambda a: pl.BlockSpec(a.shape, lambda i, dest: (0,) * a.ndim)
    grid_spec = pltpu.PrefetchScalarGridSpec(
        num_scalar_prefetch=1, grid=(n // tm,),
        in_specs=[rows(d), rows(LANES), rows(p2d.shape[1]), pl.BlockSpec(memory_space=pl.ANY), const(wple),
                  const(wpg), const(bpg), const(npl)],
        out_specs=rows(d),
        scratch_shapes=[pltpu.VMEM((2, 2 * tm * TOKEN_TILE_ROWS, LANES), F32), pltpu.SemaphoreType.DMA((2,))])
    return pl.pallas_call(
        functools.partial(_post2_body, tm),
        grid_spec=grid_spec,
        out_shape=jax.ShapeDtypeStruct((n, d), F32),
        compiler_params=_cparams("arbitrary"),
        name="post2",
    )(dest, h, rt, p2d, yb, wple, wpg, bpg, npl)


def _row_tile(n, cap=512):
    t = min(cap, n)
    assert n % t == 0 and t % 8 == 0
    return t


def kernel(x_prompt, x_sample, p_prompt, p_sample, cache_k_cmp, cache_v_cmp, cache_k_slc, cache_v_slc, cache_k_win, cache_v_win, state_conv, page_table, w_in, w_out, conv_w, norm_mix, norm_ffn, norm_ple, q_norm, k_norm, cmp_pe_k, cmp_w1_k, cmp_w2_k, cmp_pe_v, cmp_w1_v, cmp_w2_v, rel_bias, w_router_group, b_router_group, w_router_expert, b_router_expert, w_exp_gate, w_exp_up, w_exp_down, w_ple, w_ple_gate, b_ple_gate):
    assert w_in.shape[0] == 1, "single-layer step"
    bp, t, d = x_prompt.shape
    bs, ts, _ = x_sample.shape
    n_pages = page_table.shape[1]
    page = cache_k_cmp.shape[2]
    past_len = n_pages * page
    w_buf = cache_k_win.shape[2]
    n_phys = cache_k_cmp.shape[1]
    assert t % Q_BLOCK == 0 and t >= WINDOW + Q_BLOCK and page % CMP_STRIDE == 0 and ts == 4 and d == D_MODEL
    assert past_len % SLC_BLOCK == 0
    np_rows, ns_rows = bp * t, bs * ts

    row = lambda v: v.reshape(1, -1).astype(F32)
    w_in_b = jnp.pad(w_in[0], ((0, 0), (0, Z_COLS - w_in.shape[2]))).astype(BF16)
    qn = row(jnp.tile(q_norm[0], N_HEADS))
    kn1 = row(jnp.tile(k_norm[0, 1], N_KV_HEADS))
    kn2 = row(jnp.tile(k_norm[0, 2], N_KV_HEADS))
    bd = jnp.asarray(np.kron(np.eye(N_HEADS), np.ones((HEAD_DIM, HEAD_DIM))), BF16)
    pw = (row(norm_mix[0]), w_in_b, qn, kn1, kn2, conv_w[0].astype(F32), bd)
    cw_k = _compress_weights(cmp_pe_k[0], cmp_w1_k[0], cmp_w2_k[0])
    cw_v = _compress_weights(cmp_pe_v[0], cmp_w1_v[0], cmp_w2_v[0])
    kn0 = row(k_norm[0, 0])
    fext = _bias_by_distance(rel_bias)
    pt_flat = page_table.reshape(-1).astype(jnp.int32)
    wr = jnp.zeros((d, LANES), F32).at[:, :N_EXPERTS].set(w_router_expert[0])
    wr = wr.at[:, ROUTER_GROUP_LANE:ROUTER_GROUP_LANE + N_GROUPS].set(w_router_group[0])
    br = jnp.zeros((1, LANES), F32).at[0, :N_EXPERTS].set(b_router_expert[0])
    br = br.at[0, ROUTER_GROUP_LANE:ROUTER_GROUP_LANE + N_GROUPS].set(b_router_group[0])
    wo_b = w_out[0].astype(BF16)
    wple_b = w_ple[0].astype(BF16)
    wpg_b = w_ple_gate[0].astype(BF16)

    tm_p = _row_tile(t, cap=1024)
    (co_p, q_p, kw_p, vw_p, gt_p, cs_p, ksb, vsb, kwb, vwb, kc_t, vc_t, ks_t, vs_t, kc_x, vc_x) = _project(
        x_prompt.reshape(np_rows, d), bp, t, tm_p, pw)
    chunk_w = CMP_STRIDE * KV_DIM
    kcmp_p = _compress_rows(kc_x.reshape(bp, t // CMP_STRIDE, chunk_w), cw_k, kn0, True)
    vcmp_p = _compress_rows(vc_x.reshape(bp, t // CMP_STRIDE, chunk_w), cw_v, kn0, False)
    at_p = _attn_prompt(q_p, gt_p, kcmp_p, vcmp_p, ksb, vsb, kwb, vwb, fext, bp, t)

    st = state_conv[0].astype(F32)
    s0 = jnp.repeat(st[:, 0], ts, axis=0)
    s1 = jnp.repeat(st[:, 1], ts, axis=0)
    (co_s, q_s, kc_s, vc_s, ks_s, vs_s, kw_s, vw_s, gt_s, u_s) = _project(
        x_sample.reshape(ns_rows, d), bs, ts, ns_rows, pw, state=(s0, s1))
    feature_major = lambda c, n, rows_: jnp.transpose(c[0], (0, 2, 3, 1)).reshape(n, KV_DIM, rows_)
    kcmp_s = _compress_pages(feature_major(cache_k_cmp, n_phys, page), pt_flat, bs, n_pages, cw_k, kn0, True)
    vcmp_s = _compress_pages(feature_major(cache_v_cmp, n_phys, page), pt_flat, bs, n_pages, cw_v, kn0, False)
    at_s = _attn_sample(q_s, gt_s, kcmp_s, vcmp_s, ks_s, vs_s, kw_s, vw_s,
                        feature_major(cache_k_slc, n_phys, page), feature_major(cache_v_slc, n_phys, page),
                        feature_major(cache_k_win, bs, w_buf), feature_major(cache_v_win, bs, w_buf),
                        pt_flat, fext, bs, ts, n_pages, page)

    tp1 = _row_tile(np_rows)
    ts1 = _row_tile(ns_rows)
    nf = row(norm_ffn[0])
    h_p, hn_p, rt_p, cnt_p = _post1(x_prompt.reshape(np_rows, d), co_p, at_p, wo_b, nf, wr, br,
                                    jnp.zeros((8, LANES), F32), tp1)
    h_s, hn_s, rt_s, cnt_s = _post1(x_sample.reshape(ns_rows, d), co_s, at_s, wo_b, nf, wr, br, cnt_p, ts1)

    counts = cnt_s[0, :N_EXPERTS].astype(jnp.int32)
    padded = (counts + EXPERT_ROWS - 1) // EXPERT_ROWS * EXPERT_ROWS
    pad_end = jnp.cumsum(padded)
    pad_start = pad_end - padded
    n_assign = 2 * (np_rows + ns_rows)
    n_blk = (n_assign + N_EXPERTS * (EXPERT_ROWS - 1) + EXPERT_ROWS - 1) // EXPERT_ROWS
    blk_first = jnp.arange(n_blk, dtype=jnp.int32) * EXPERT_ROWS
    blk_expert = jnp.minimum(jnp.sum((pad_end[None, :] <= blk_first[:, None]).astype(jnp.int32), axis=1),
                             N_EXPERTS - 1)
    n_used = (pad_end[-1:] // EXPERT_ROWS).astype(jnp.int32)

    def dest_of(rt):
        e = rt[:, 0:2].astype(jnp.int32).reshape(-1)
        return pad_start[e] + rt[:, 2:4].astype(jnp.int32).reshape(-1)

    dest_p = dest_of(rt_p)
    dest_s = dest_of(rt_s)

    xs = jnp.zeros((n_blk * EXPERT_ROWS * TOKEN_TILE_ROWS, LANES), F32)
    xs = _scatter_rows(dest_p, hn_p, xs)
    xs = _scatter_rows(dest_s, hn_s, xs)
    yb = _experts(blk_expert, n_used, xs, w_exp_gate[0], w_exp_up[0], w_exp_down[0])

    bpg = row(b_ple_gate[0])
    npl = row(norm_ple[0])
    y_p = _post2(dest_p, h_p, rt_p, p_prompt[0].reshape(np_rows, -1), yb, wple_b, wpg_b, bpg, npl,
                 _row_tile(np_rows, cap=1024))
    y_s = _post2(dest_s, h_s, rt_s, p_sample[0].reshape(ns_rows, -1), yb, wple_b, wpg_b, bpg, npl, ts1)

    kv5 = lambda a, b, s: a.reshape(1, b, s, N_KV_HEADS, HEAD_DIM)
    wp = min(WINDOW, t)
    win_p = lambda a: kv5(a.reshape(bp, t, KV_DIM)[:, t - wp:], bp, wp)
    win_s = lambda c, new: jnp.concatenate([c[0], new.reshape(bs, ts, N_KV_HEADS, HEAD_DIM)], axis=1)[None, :, ts:]
    conv_p = cs_p[:, 8 - (CONV_K - 1):][None]
    conv_s = u_s.reshape(bs, ts, CONV_DIM)[:, ts - (CONV_K - 1):][None]
    from_t = lambda a: jnp.transpose(a.reshape(bp, N_KV_HEADS, HEAD_DIM, t), (0, 3, 1, 2))[None]
    return (y_p.reshape(bp, t, d), y_s.reshape(bs, ts, d),
            from_t(kc_t), from_t(vc_t), from_t(ks_t), from_t(vs_t), win_p(kw_p), win_p(vw_p), conv_p,
            kv5(kc_s, bs, ts), kv5(vc_s, bs, ts), kv5(ks_s, bs, ts), kv5(vs_s, bs, ts),
            win_s(cache_k_win, kw_s), win_s(cache_v_win, vw_s), conv_s)
```

```python
import functools
import math

import numpy as np
import jax
import jax.numpy as jnp
from jax import lax
from jax.experimental import pallas as pl
from jax.experimental.pallas import tpu as pltpu

F32 = jnp.float32
BF16 = jnp.bfloat16
NEG_INF = float("-inf")
MASKED = -1e30

HEAD_DIM = 64
N_HEADS = 8
N_KV_HEADS = 2
GROUP = N_HEADS // N_KV_HEADS
CONV_DIM = 512
ATTN_DIM = 512
KV_DIM = N_KV_HEADS * HEAD_DIM
N_BRANCH = 3
CONV_K = 3
CMP_BLOCK = 32
CMP_STRIDE = 16
CMP_HIDDEN = 256
SLC_BLOCK = 64
N_SEL = 16
WINDOW = 512
Q_BLOCK = 128
FORCE_SCORE = 1e4
NUM_BUCKETS = 32
MAX_DISTANCE = 128
N_GROUPS = 4
EXPERTS_PER_GROUP = 8
N_EXPERTS = N_GROUPS * EXPERTS_PER_GROUP
D_EXPERT = 512
EPS = 1e-6

D_MODEL = 1024
LANES = 128
TOKEN_TILE_ROWS = D_MODEL // LANES
Z_COLS = 3 * CONV_DIM + ATTN_DIM + 6 * KV_DIM + LANES
BIAS_DMAX = 768
EXPERT_ROWS = 512
ROUTER_GROUP_LANE = 32
SCATTER_TOKENS = 512
DMA_ISSUE_UNROLL = 8
COMPRESS_SEQS = 4
CHUNK_PITCH = 24
SAMPLE_SEQS = 4
VMEM_LIMIT = 56 * 1024 * 1024


def _cparams(*sem):
    return pltpu.CompilerParams(dimension_semantics=sem, vmem_limit_bytes=VMEM_LIMIT)


def _dot(a, b):
    return jnp.dot(a, b, preferred_element_type=F32)


def _dot_nt(a, b):
    return lax.dot_general(a, b, (((1,), (1,)), ((), ())), preferred_element_type=F32)


def _split3(x):
    hi = x.astype(BF16)
    r = x - hi.astype(F32)
    mid = r.astype(BF16)
    lo = (r - mid.astype(F32)).astype(BF16)
    return hi, mid, lo


def _rms(x, g):
    return x * lax.rsqrt(jnp.mean(x * x, axis=-1, keepdims=True) + EPS) * g


def _head_rms(x, bd, g):
    hi, mid, _ = _split3(x * x)
    ss = _dot(hi, bd) + _dot(mid, bd)
    return x * lax.rsqrt(ss * (1.0 / HEAD_DIM) + EPS) * g


def _sigmoid(x):
    return 1.0 / (1.0 + jnp.exp(-x))


def _store_token_tiles(ref, x, n):
    r = x.shape[1] // LANES
    for j in range(r):
        ref[pl.ds(j, n, stride=r), :] = x[:, j * LANES:(j + 1) * LANES]


def _load_token_tiles(ref, n, r, first=0):
    return jnp.concatenate([ref[pl.ds(first + j, n, stride=r), :] for j in range(r)], axis=-1)


def _softmax_parts(s):
    m = jnp.max(s, axis=-1, keepdims=True)
    m = jnp.where(m == NEG_INF, 0.0, m)
    p = jnp.exp(s - m)
    l = jnp.sum(p, axis=-1, keepdims=True)
    return p, l


def _proj_body(sample, tm, *refs):
    if sample:
        (x_ref, nm_ref, w_ref, qn_ref, kn1_ref, kn2_ref, cw_ref, bd_ref, s0_ref, s1_ref,
         co_ref, q_ref, kc_ref, vc_ref, ks_ref, vs_ref, kw_ref, vw_ref, gt_ref, u_ref) = refs
    else:
        (x_ref, nm_ref, w_ref, qn_ref, kn1_ref, kn2_ref, cw_ref, bd_ref,
         co_ref, q_ref, kw_ref, vw_ref, gt_ref, cs_ref,
         ksb_ref, vsb_ref, kwb_ref, vwb_ref, kct_ref, vct_ref, kst_ref, vst_ref, kcx_ref, vcx_ref,
         carry_ref, stage_ref) = refs

    xn = _rms(x_ref[...], nm_ref[...]).astype(BF16)

    def seg(a, b):
        return _dot(xn, w_ref[:, a:b])

    c3 = 3 * CONV_DIM
    u = seg(2 * CONV_DIM, c3) * seg(0, CONV_DIM)
    bg = seg(CONV_DIM, 2 * CONV_DIM)
    row = lax.broadcasted_iota(jnp.int32, (tm, 1), 0)
    um1 = pltpu.roll(u, 1, axis=0)
    um2 = pltpu.roll(u, 2, axis=0)
    if sample:
        r = row & 3
        s0 = s0_ref[...]
        s1 = s1_ref[...]
        prev1 = jnp.where(r == 0, s1, um1)
        prev2 = jnp.where(r == 0, s0, jnp.where(r == 1, s1, um2))
        u_ref[...] = u
    else:
        @pl.when(pl.program_id(1) == 0)
        def _():
            carry_ref[...] = jnp.zeros_like(carry_ref)
        c = carry_ref[...]
        prev1 = jnp.where(row == 0, c[7:8], um1)
        prev2 = jnp.where(row == 0, c[6:7], jnp.where(row == 1, c[7:8], um2))
        carry_ref[...] = u[tm - 8:tm]
        cs_ref[0] = u[tm - 8:tm]
    cw = cw_ref[...]
    y = cw[0:1] * prev2 + cw[1:2] * prev1 + cw[2:3] * u
    co_ref[...] = (bg * y).astype(BF16)

    bd = bd_ref[...]
    q = _head_rms(seg(c3, c3 + ATTN_DIM), bd, qn_ref[...]) * (HEAD_DIM ** -0.5)
    lane = lax.broadcasted_iota(jnp.int32, (tm, LANES), 1)
    low = lane < HEAD_DIM

    def head_planes(x, fill):
        return [jnp.where(low, x if h == 0 else pltpu.roll(x, HEAD_DIM, axis=1), fill) for h in range(N_KV_HEADS)]

    if sample:
        q_ref[...] = q
    else:
        for hd in range(N_HEADS):
            pair = q[:, (hd // 2) * LANES:(hd // 2 + 1) * LANES]
            if hd % 2:
                pair = pltpu.roll(pair, HEAD_DIM, axis=1)
            q_ref[0, hd] = jnp.where(low, pair, 0.0).astype(BF16)

    k0 = c3 + ATTN_DIM
    bdk = bd[:KV_DIM, :KV_DIM]
    kc = seg(k0, k0 + KV_DIM)
    vc = seg(k0 + KV_DIM, k0 + 2 * KV_DIM)
    ks = _head_rms(seg(k0 + 2 * KV_DIM, k0 + 3 * KV_DIM), bdk, kn1_ref[...])
    vs = seg(k0 + 3 * KV_DIM, k0 + 4 * KV_DIM)
    kw = _head_rms(seg(k0 + 4 * KV_DIM, k0 + 5 * KV_DIM), bdk, kn2_ref[...])
    vw = seg(k0 + 5 * KV_DIM, k0 + 6 * KV_DIM)
    kw_ref[...] = kw
    vw_ref[...] = vw
    if sample:
        kc_ref[...] = kc
        vc_ref[...] = vc
        ks_ref[...] = ks
        vs_ref[...] = vs
    else:
        kct_ref[0] = kc.T
        vct_ref[0] = vc.T
        kst_ref[0] = ks.T
        vst_ref[0] = vs.T
        for src, dst in ((kc, kcx_ref), (vc, vcx_ref)):
            stage_ref[...] = src
            dst[...] = jnp.concatenate([stage_ref[pl.ds(r, tm // CMP_STRIDE, stride=CMP_STRIDE), :]
                                        for r in range(CMP_STRIDE)], axis=-1)
    gates = _sigmoid(seg(k0 + 6 * KV_DIM, k0 + 6 * KV_DIM + LANES))
    if sample:
        gt_ref[...] = gates
    else:
        pos = pl.program_id(1) * tm + lax.broadcasted_iota(jnp.int32, (tm, LANES), 0)
        block_onehot = jnp.where(lane - HEAD_DIM == (pos >> 6), 1.0, 0.0)
        for h, (a, b, c, e) in enumerate(zip(head_planes(ks, block_onehot), head_planes(vs, 1.0),
                                             head_planes(kw, 0.0), head_planes(vw, 1.0))):
            ksb_ref[h] = a.astype(BF16)
            vsb_ref[h] = b.astype(BF16)
            kwb_ref[h] = c.astype(BF16)
            vwb_ref[h] = e.astype(BF16)
        gt_ref[0] = gates
        gt_ref[1] = pltpu.roll(gates, LANES - GROUP * N_BRANCH, axis=1)


def _project(x2d, batch, seq, tm, weights, state=None):
    n, d = x2d.shape
    sample = state is not None
    nt = seq // tm if not sample else 1
    const = lambda shape: pl.BlockSpec(shape, lambda b, t: (0,) * len(shape))
    rows = lambda w: pl.BlockSpec((tm, w), lambda b, t: (b * nt + t, 0))
    nm, w_in, qn, kn1, kn2, cw, bd = weights
    in_specs = [rows(d), const(nm.shape), const(w_in.shape), const(qn.shape), const(kn1.shape),
                const(kn2.shape), const(cw.shape), const(bd.shape)]
    args = [x2d, nm, w_in, qn, kn1, kn2, cw, bd]
    kv_f32 = [jax.ShapeDtypeStruct((n, KV_DIM), F32)] * 6
    if sample:
        in_specs += [rows(CONV_DIM), rows(CONV_DIM)]
        args += list(state)
        out_shape = ([jax.ShapeDtypeStruct((n, CONV_DIM), BF16), jax.ShapeDtypeStruct((n, ATTN_DIM), F32)]
                     + kv_f32 + [jax.ShapeDtypeStruct((n, LANES), F32), jax.ShapeDtypeStruct((n, CONV_DIM), F32)])
        out_specs = [rows(CONV_DIM), rows(ATTN_DIM)] + [rows(KV_DIM)] * 6 + [rows(LANES), rows(CONV_DIM)]
        scratch = []
        grid = (1, 1)
    else:
        planes = pl.BlockSpec((N_KV_HEADS, tm, LANES), lambda b, t: (0, b * nt + t, 0))
        out_shape = ([jax.ShapeDtypeStruct((n, CONV_DIM), BF16),
                      jax.ShapeDtypeStruct((batch, N_HEADS, seq, LANES), BF16)]
                     + kv_f32[:2] + [jax.ShapeDtypeStruct((N_KV_HEADS, n, LANES), F32),
                                     jax.ShapeDtypeStruct((batch, 8, CONV_DIM), F32)]
                     + [jax.ShapeDtypeStruct((N_KV_HEADS, n, LANES), BF16)] * 4
                     + [jax.ShapeDtypeStruct((batch, KV_DIM, seq), F32)] * 4
                     + [jax.ShapeDtypeStruct((n // CMP_STRIDE, CMP_STRIDE * KV_DIM), F32)] * 2)
        out_specs = ([rows(CONV_DIM), pl.BlockSpec((1, N_HEADS, tm, LANES), lambda b, t: (b, 0, t, 0))]
                     + [rows(KV_DIM)] * 2 + [planes, pl.BlockSpec((1, 8, CONV_DIM), lambda b, t: (b, 0, 0))]
                     + [planes] * 4 + [pl.BlockSpec((1, KV_DIM, tm), lambda b, t: (b, 0, t))] * 4
                     + [pl.BlockSpec((tm // CMP_STRIDE, CMP_STRIDE * KV_DIM), lambda b, t: (b * nt + t, 0))] * 2)
        scratch = [pltpu.VMEM((8, CONV_DIM), F32), pltpu.VMEM((tm, KV_DIM), F32)]
        grid = (batch, nt)
    return pl.pallas_call(
        functools.partial(_proj_body, sample, tm),
        grid=grid, in_specs=in_specs, out_specs=out_specs, out_shape=out_shape, scratch_shapes=scratch,
        compiler_params=_cparams("arbitrary", "arbitrary"),
        name="proj_sample" if sample else "proj_prompt",
    )(*args)


def _gelu_tanh(x):
    cdf = 0.5 * (1.0 + jnp.tanh(math.sqrt(2.0 / math.pi) * (x + 0.044715 * (x * x * x))))
    return x * cdf


def _compress_core(norm, x, pe_ref, we_ref, w2_ref, g_ref):
    n = x.shape[0]
    a0 = _dot((x + pe_ref[0:1]).astype(BF16), we_ref[0])
    a1 = _dot((x + pe_ref[1:2]).astype(BF16), we_ref[1])
    hid = a0 + pltpu.roll(a1, n - 1, axis=0)
    w2 = w2_ref[...]
    outs = []
    for h in range(N_KV_HEADS):
        act = _gelu_tanh(hid[:, h * CMP_HIDDEN:(h + 1) * CMP_HIDDEN])
        o = _dot(act.astype(BF16), w2)
        if norm:
            o = _rms(o, g_ref[...])
        outs.append(o)
    return outs


def _compress_rows_body(norm, x_ref, pe_ref, we_ref, w2_ref, g_ref, o_ref):
    outs = _compress_core(norm, x_ref[0], pe_ref, we_ref, w2_ref, g_ref)
    for h in range(N_KV_HEADS):
        o_ref[0, h] = jnp.concatenate([outs[h], jnp.zeros_like(outs[h])], axis=-1).astype(BF16)


def _fetch_pages(pt_ref, cache_ref, buf_ref, sem_ref, step, slot, n_ops, priority):
    base = step * n_ops

    def issue(j, _):
        pltpu.make_async_copy(cache_ref.at[pt_ref[base + j]], buf_ref.at[slot, j], sem_ref.at[slot]).start(
            priority=priority)
        return 0

    lax.fori_loop(0, n_ops, issue, 0)


def _paged_prefetch(pt_ref, caches, bufs, sems, n_ops):
    i = pl.program_id(0)
    slot = i & 1
    for k, (cache_ref, buf_ref, sem_ref) in enumerate(zip(caches, bufs, sems)):
        @pl.when(i == 0)
        def _():
            _fetch_pages(pt_ref, cache_ref, buf_ref, sem_ref, 0, 0, n_ops, k % 2)

        @pl.when(i + 1 < pl.num_programs(0))
        def _():
            _fetch_pages(pt_ref, cache_ref, buf_ref, sem_ref, i + 1, 1 - slot, n_ops, k % 2)

    for cache_ref, buf_ref, sem_ref in zip(caches, bufs, sems):
        pltpu.make_async_copy(cache_ref.at[pl.ds(0, n_ops)], buf_ref.at[slot], sem_ref.at[slot]).wait()
    return slot


def _compress_pages_body(norm, nch, n_seq, n_pages, pt_ref, cache_ref, pe_ref, we_ref, w2_ref, g_ref, o_ref,
                         rows_ref, buf_ref, sem_ref):
    n_ops = n_seq * n_pages
    slot = _paged_prefetch(pt_ref, [cache_ref], [buf_ref], [sem_ref], n_ops)
    page = buf_ref.shape[3]
    per_page = page // CMP_STRIDE
    for j in range(n_ops):
        rows = buf_ref[slot, j].T
        for c in range(per_page):
            first = (j * per_page + c) * CHUNK_PITCH
            rows_ref[first:first + CMP_STRIDE, :] = rows[c * CMP_STRIDE:(c + 1) * CMP_STRIDE]
    x = jnp.concatenate([rows_ref[pl.ds(r, n_seq * nch, stride=CHUNK_PITCH), :] for r in range(CMP_STRIDE)], axis=-1)
    out = jnp.concatenate(_compress_core(norm, x, pe_ref, we_ref, w2_ref, g_ref), axis=-1)
    for b in range(n_seq):
        o_ref[b] = out[b * nch:(b + 1) * nch]


def _compress_weights(pe, w1, w2):
    w1r = w1.reshape(2, CMP_STRIDE, HEAD_DIM, CMP_HIDDEN).astype(BF16)
    z = jnp.zeros_like(w1r)
    we = jnp.stack([jnp.concatenate([w1r, z], axis=-1), jnp.concatenate([z, w1r], axis=-1)], axis=2)
    we = we.reshape(2, CMP_STRIDE * KV_DIM, N_KV_HEADS * CMP_HIDDEN)
    per = pe.reshape(2, CMP_STRIDE, 1, HEAD_DIM)
    pex = jnp.broadcast_to(per, (2, CMP_STRIDE, N_KV_HEADS, HEAD_DIM)).reshape(2, CMP_STRIDE * KV_DIM)
    return pex.astype(F32), we.astype(BF16), w2.astype(BF16)


def _compress_rows(rows3, cw, gain, norm):
    b, nch, width = rows3.shape
    pex, we, w2 = cw
    const = lambda a: pl.BlockSpec(a.shape, lambda i: (0,) * a.ndim)
    return pl.pallas_call(
        functools.partial(_compress_rows_body, norm),
        grid=(b,),
        in_specs=[pl.BlockSpec((1, nch, width), lambda i: (i, 0, 0)), const(pex), const(we), const(w2), const(gain)],
        out_specs=pl.BlockSpec((1, N_KV_HEADS, nch, LANES), lambda i: (i, 0, 0, 0)),
        out_shape=jax.ShapeDtypeStruct((b, N_KV_HEADS, nch, LANES), BF16),
        compiler_params=_cparams("arbitrary"),
        name="compress_rows",
    )(rows3, pex, we, w2, gain)


def _compress_pages(cache_t, pt_flat, n_batch, n_pages, cw, gain, norm):
    _, _, page = cache_t.shape
    nch = n_pages * page // CMP_STRIDE
    pex, we, w2 = cw
    n_seq = COMPRESS_SEQS if n_batch % COMPRESS_SEQS == 0 else 1
    const = lambda a: pl.BlockSpec(a.shape, lambda i, pt: (0,) * a.ndim)
    n_ops = n_seq * n_pages
    grid_spec = pltpu.PrefetchScalarGridSpec(
        num_scalar_prefetch=1, grid=(n_batch // n_seq,),
        in_specs=[pl.BlockSpec(memory_space=pl.ANY), const(pex), const(we), const(w2), const(gain)],
        out_specs=pl.BlockSpec((n_seq, nch, KV_DIM), lambda i, pt: (i, 0, 0)),
        scratch_shapes=[pltpu.VMEM((n_ops * (page // CMP_STRIDE) * CHUNK_PITCH, KV_DIM), F32),
                        pltpu.VMEM((2, n_ops, KV_DIM, page), F32),
                        pltpu.SemaphoreType.DMA((2,))])
    return pl.pallas_call(
        functools.partial(_compress_pages_body, norm, nch, n_seq, n_pages),
        grid_spec=grid_spec,
        out_shape=jax.ShapeDtypeStruct((n_batch, nch, KV_DIM), F32),
        compiler_params=_cparams("arbitrary"),
        name="compress_pages",
    )(pt_flat, cache_t, pex, we, w2, gain)


def _rel_bucket(dist):
    n = jnp.maximum(dist, 0)
    max_exact = NUM_BUCKETS // 2
    nf = jnp.maximum(n, 1).astype(F32)
    large = max_exact + (jnp.log(nf / max_exact) / math.log(MAX_DISTANCE / max_exact)
                         * (NUM_BUCKETS - max_exact)).astype(jnp.int32)
    large = jnp.minimum(large, NUM_BUCKETS - 1)
    return jnp.where(n < max_exact, n, large)


def _bias_by_distance(rel_bias):
    d = jnp.arange(BIAS_DMAX, dtype=jnp.int32)
    f = rel_bias.astype(F32)[_rel_bucket(d)]
    f = (f - f[BIAS_DMAX - 1:BIAS_DMAX]).T
    return jnp.concatenate([f, jnp.full((N_HEADS, 1), NEG_INF, F32)], axis=1)


def _bias_index(d, valid):
    return np.where(valid, np.clip(d, 0, BIAS_DMAX - 1), BIAS_DMAX).astype(np.int32)


def _bias_table(fext, d, valid):
    return jnp.take(fext, jnp.asarray(_bias_index(d, valid)), axis=1)


def _toeplitz_body(n_rows, v_ref, o_ref):
    x = jnp.broadcast_to(v_ref[0], (n_rows, v_ref.shape[2]))
    o_ref[0] = pltpu.roll(x, 0, axis=1, stride=1, stride_axis=0)


def _toeplitz_rows(v, n_rows):
    h, w = v.shape
    return pl.pallas_call(
        functools.partial(_toeplitz_body, n_rows),
        grid=(h,),
        in_specs=[pl.BlockSpec((1, 1, w), lambda i: (i, 0, 0))],
        out_specs=pl.BlockSpec((1, n_rows, w), lambda i: (i, 0, 0)),
        out_shape=jax.ShapeDtypeStruct((h, n_rows, w), F32),
        compiler_params=_cparams("arbitrary"),
        name="toeplitz_rows",
    )(v.reshape(h, 1, w))


def _select_blocks(imp_t, srow, qpos, n_rank):
    qblk = qpos >> 6
    forced = (srow == 0) | (srow == qblk) | (srow == qblk - 1)
    valid = (srow << 6) <= qpos
    imp_t = jnp.where(valid, imp_t + jnp.where(forced, FORCE_SCORE, 0.0), NEG_INF)
    n_rows = imp_t.shape[0]
    assert n_rows % 8 == 0
    slabs = [imp_t[a:a + 8] for a in range(0, n_rows, 8)]
    rows8 = [srow[a:a + 8] for a in range(0, n_rows, 8)]
    cnts = [jnp.zeros(x.shape, jnp.int32) for x in slabs]
    for s in range(n_rank):
        r = imp_t[s:s + 1, :]
        for j, x in enumerate(slabs):
            if 8 * j > s:
                beats = r >= x
            elif 8 * j + 7 <= s:
                beats = r > x
            else:
                beats = (r > x) | ((r == x) & (rows8[j] > s))
            cnts[j] = cnts[j] + jnp.where(beats, 1, 0)
    cnt = jnp.concatenate(cnts, axis=0)
    return jnp.where((cnt < N_SEL) & valid, 1.0, 0.0)


def _exp_pv(s, m, v):
    return _dot(jnp.exp((s - m).astype(BF16)), v)


def _normalize_pv(pv):
    return pv / jnp.maximum(pv[:, HEAD_DIM:HEAD_DIM + 1], 1e-30)


def _attn_prompt_body(kt, n_slc, q_ref, gt_ref, kcmp_ref, vcmp_ref, ks_ref, vs_ref, kw_ref, vw_ref,
                      ctab_ref, ntab_ref, wtab_ref, ovl_ref, o_ref, *score_refs):
    i = pl.program_id(1)
    qb = Q_BLOCK
    near_start = pl.multiple_of(jnp.maximum(i - 1, 0) * qb, qb)
    n_kt = ks_ref.shape[1] // kt
    heads = [_attn_head_setup(h, i, n_slc, q_ref, gt_ref, kcmp_ref, vcmp_ref, ks_ref, vs_ref, kw_ref, vw_ref,
                              ctab_ref, ntab_ref, wtab_ref, ovl_ref) for h in range(N_KV_HEADS)]

    def scores(h, t):
        k0 = pl.multiple_of(jnp.minimum(t, n_kt - 1) * kt, kt)
        return _dot_nt(heads[h]["q_far"], ks_ref[h, pl.ds(k0, kt), :])

    def consume(h, t, s, m_old, acc):
        k0 = pl.multiple_of(jnp.minimum(t, n_kt - 1) * kt, kt)
        m_new = jnp.maximum(m_old, jnp.max(s, axis=-1, keepdims=True))
        return m_new, jnp.exp(m_old - m_new) * acc + _exp_pv(s, m_new, vs_ref[h, pl.ds(k0, kt), :])

    def far_pair(u, carry):
        out = []
        for h in range(N_KV_HEADS):
            sa_ref, sb_ref = score_refs[2 * h], score_refs[2 * h + 1]
            m, acc = carry[h]
            sb_ref[...] = scores(h, 2 * u + 1)
            m, acc = consume(h, 2 * u, sa_ref[...], m, acc)
            sa_ref[...] = scores(h, 2 * u + 2)
            out.append(consume(h, 2 * u + 1, sb_ref[...], m, acc))
        return tuple(out)

    n_far = (near_start + kt - 1) // kt
    for h in range(N_KV_HEADS):
        score_refs[2 * h][...] = scores(h, 0)
    final = lax.fori_loop(0, (n_far + 1) // 2, far_pair, tuple((hd["m0"], hd["a0"]) for hd in heads))

    lane = lax.broadcasted_iota(jnp.int32, (qb, LANES), 1)
    for h, hd in enumerate(heads):
        o_s = _normalize_pv(final[h][1])
        gates, o_c, o_w = hd["gates"], hd["o_c"], hd["o_w"]
        heads_out = []
        for g in range(GROUP):
            c = g * N_BRANCH
            sl = slice(g * qb, (g + 1) * qb)
            heads_out.append(gates[:, c:c + 1] * o_c[sl] + gates[:, c + 1:c + 2] * o_s[sl]
                             + gates[:, c + 2:c + 3] * o_w[sl])
        tiles = [jnp.where(lane < HEAD_DIM, heads_out[2 * j], pltpu.roll(heads_out[2 * j + 1], HEAD_DIM, axis=1))
                 for j in range(GROUP // 2)]
        width = GROUP * HEAD_DIM
        o_ref[:, h * width:(h + 1) * width] = jnp.concatenate(tiles, axis=-1).astype(BF16)


def _attn_head_setup(h, i, n_slc, q_ref, gt_ref, kcmp_ref, vcmp_ref, ks_ref, vs_ref, kw_ref, vw_ref,
                     ctab_ref, ntab_ref, wtab_ref, ovl_ref):
    qb = Q_BLOCK
    rows = GROUP * qb
    hs = slice(h * GROUP, (h + 1) * GROUP)
    n_cmp_pad = kcmp_ref.shape[2]
    first_near_block = (qb // SLC_BLOCK) * jnp.maximum(i - 1, 0)
    near_start = pl.multiple_of(jnp.maximum(i - 1, 0) * qb, qb)
    win_start = pl.multiple_of(jnp.maximum(i * qb - WINDOW, 0), qb)
    band = WINDOW + qb
    gates = gt_ref[h]
    lane = lax.broadcasted_iota(jnp.int32, (qb, LANES), 1)
    q0 = q_ref[0, hs].reshape(rows, LANES)

    s = _dot_nt(q0, kw_ref[h, pl.ds(win_start, band), :]) + wtab_ref[0, hs].reshape(rows, band)
    m = jnp.max(s, axis=-1, keepdims=True)
    m = jnp.where(m == NEG_INF, 0.0, m)
    o_w = _normalize_pv(_exp_pv(s, m, vw_ref[h, pl.ds(win_start, band), :]))

    per_qb = qb // CMP_STRIDE
    ctab = pltpu.roll(ctab_ref[hs].reshape(rows, 2 * n_cmp_pad), i * per_qb, axis=1)[:, n_cmp_pad:]
    s = _dot_nt(q0, kcmp_ref[0, h]) + ctab
    p, l = _softmax_parts(s)
    pn = p / jnp.maximum(l, 1e-30)
    o_c = _dot(pn.astype(BF16), vcmp_ref[0, h])

    psum = pn[0:qb] + pn[qb:2 * qb] + pn[2 * qb:3 * qb] + pn[3 * qb:4 * qb]
    hi, mid, lo = _split3(psum)
    ovl = ovl_ref[...]
    imp_t = _dot_nt(ovl, hi) + _dot_nt(ovl, mid) + _dot_nt(ovl, lo)
    srow = lax.broadcasted_iota(jnp.int32, (n_slc, qb), 0)
    qpos_t = i * qb + lax.broadcasted_iota(jnp.int32, (n_slc, qb), 1)
    sel_t = _select_blocks(imp_t, srow, qpos_t, n_slc)
    sel_t = jnp.concatenate([sel_t, jnp.zeros((LANES - n_slc, qb), F32)], axis=0)
    sel = sel_t.T

    def query_with_mask(keep):
        m = pltpu.roll(jnp.where(keep, 0.0, MASKED), HEAD_DIM, axis=1).astype(BF16)
        m = jnp.concatenate([jnp.where(lane < HEAD_DIM, q0[g * qb:(g + 1) * qb], m) for g in range(GROUP)], axis=0)
        return m

    q_near = query_with_mask(sel > 0.5)
    q_far = query_with_mask((sel > 0.5) & (lane < first_near_block))

    s = _dot_nt(q_near, ks_ref[h, pl.ds(near_start, 2 * qb), :]) + ntab_ref[0, hs].reshape(rows, 2 * qb)
    m0 = jnp.max(s, axis=-1, keepdims=True)
    m0 = jnp.where(m0 == NEG_INF, 0.0, m0)
    a0 = _exp_pv(s, m0, vs_ref[h, pl.ds(near_start, 2 * qb), :])
    return dict(gates=gates, o_w=o_w, o_c=o_c, q_far=q_far, m0=m0, a0=a0)


def _attn_prompt(qp, gates, kcmp, vcmp, ksb, vsb, kwb, vwb, fext, batch, seq):
    qb = Q_BLOCK
    nqb = seq // qb
    n_slc = seq // SLC_BLOCK
    assert n_slc <= LANES - HEAD_DIM
    n_cmp_pad = kcmp.shape[2]
    n_cmp = n_cmp_pad - 1
    kt = min(512, seq)
    assert (seq // kt) % 2 == 0
    band = WINDOW + qb
    iq = np.arange(qb)

    per_qb = qb // CMP_STRIDE
    half = 2 * per_qb
    m = np.arange(-half, half)
    d = iq[:, None] - (m[None, :] * CMP_STRIDE + CMP_BLOCK - 1)
    assert d[:, 0].min() >= MAX_DISTANCE and d[:, -1].max() < 0
    ctab = jnp.concatenate([jnp.zeros((N_HEADS, qb, n_cmp_pad - half), F32), _bias_table(fext, d, d >= 0),
                            jnp.full((N_HEADS, qb, n_cmp_pad - half), NEG_INF, F32)], axis=2)
    assert per_qb * (nqb - 1) < n_cmp_pad
    assert (n_cmp_pad - 1) * CMP_STRIDE + CMP_BLOCK - 1 >= seq and n_cmp == n_cmp_pad - 1

    nv = WINDOW // qb + 1
    kw_ = WINDOW + band
    w = qb + kw_
    assert w % LANES == 0
    m = np.arange(w)
    m = np.where(m < kw_, m, m - w)
    dj = WINDOW - m
    wide = _toeplitz_rows(_bias_table(fext, dj, (dj >= 0) & (dj < WINDOW)), qb)
    wtab = jnp.stack([wide[:, :, WINDOW - qb * v:WINDOW - qb * v + band] for v in range(nv)], axis=0)
    ntab = jnp.stack([wide[:, :, WINDOW - qb * v:WINDOW - qb * v + 2 * qb] for v in range(2)], axis=0)
    assert 2 * qb <= WINDOW

    c_start = np.arange(n_cmp_pad) * CMP_STRIDE
    s_start = np.arange(n_slc) * SLC_BLOCK
    ovl = ((c_start[None, :] < s_start[:, None] + SLC_BLOCK) & (c_start[None, :] + CMP_BLOCK > s_start[:, None])
           & (np.arange(n_cmp_pad) < n_cmp)[None, :])
    ovl = jnp.asarray(ovl, BF16)

    planes = lambda: pl.BlockSpec((N_KV_HEADS, seq, LANES), lambda b, i: (0, b, 0))
    return pl.pallas_call(
        functools.partial(_attn_prompt_body, kt, n_slc),
        grid=(batch, nqb),
        in_specs=[pl.BlockSpec((1, N_HEADS, qb, LANES), lambda b, i: (b, 0, i, 0)),
                  pl.BlockSpec((N_KV_HEADS, qb, LANES), lambda b, i: (0, b * nqb + i, 0)),
                  pl.BlockSpec((1, N_KV_HEADS, n_cmp_pad, LANES), lambda b, i: (b, 0, 0, 0)),
                  pl.BlockSpec((1, N_KV_HEADS, n_cmp_pad, LANES), lambda b, i: (b, 0, 0, 0)),
                  planes(), planes(), planes(), planes(),
                  pl.BlockSpec((N_HEADS, qb, 2 * n_cmp_pad), lambda b, i: (0, 0, 0)),
                  pl.BlockSpec((1, N_HEADS, qb, 2 * qb), lambda b, i: (jnp.minimum(i, 1), 0, 0, 0)),
                  pl.BlockSpec((1, N_HEADS, qb, band), lambda b, i: (jnp.minimum(i, nv - 1), 0, 0, 0)),
                  pl.BlockSpec(ovl.shape, lambda b, i: (0, 0))],
        out_specs=pl.BlockSpec((qb, ATTN_DIM), lambda b, i: (b * nqb + i, 0)),
        out_shape=jax.ShapeDtypeStruct((batch * seq, ATTN_DIM), BF16),
        scratch_shapes=[pltpu.VMEM((GROUP * qb, kt), F32)] * (2 * N_KV_HEADS),
        compiler_params=_cparams("arbitrary", "arbitrary"),
        name="attn_prompt",
    )(qp, gates, kcmp, vcmp, ksb, vsb, kwb, vwb, ctab, ntab, wtab, ovl)


def _attn_sample_body(n_seq, n_pages, ts, past_len, pt_ref, cache_k_ref, cache_v_ref, *refs):
    kbuf, vbuf, ksem, vsem = refs[-4:]
    n_ops = n_seq * n_pages
    slot = _paged_prefetch(pt_ref, [cache_k_ref, cache_v_ref], [kbuf, vbuf], [ksem, vsem], n_ops)
    for b in range(n_seq):
        _attn_sample_one(b, [kbuf.at[slot, b * n_pages + j] for j in range(n_pages)],
                         [vbuf.at[slot, b * n_pages + j] for j in range(n_pages)], refs[:-4], ts, past_len)


def _attn_sample_one(b, kpages, vpages, refs, ts, past_len):
    (q_ref, gt_ref, kcmp_ref, vcmp_ref, ksn_ref, vsn_ref, kwc_ref, vwc_ref, kwn_ref, vwn_ref,
     ctab_ref, stab_ref, sntab_ref, wtab_ref, wntab_ref, ovl_ref, o_ref) = refs
    rows = GROUP * N_KV_HEADS * ts
    rq = N_KV_HEADS * ts
    q = q_ref[b]
    gates = gt_ref[b]

    s = _dot_nt(q, kcmp_ref[b].astype(BF16)) + ctab_ref[...]
    p, l = _softmax_parts(s)
    pn = p / jnp.maximum(l, 1e-30)
    o_c = _dot(pn.astype(BF16), vcmp_ref[b].astype(BF16))

    psum = pn[0:rq]
    for g in range(1, GROUP):
        psum = psum + pn[g * rq:(g + 1) * rq]
    hi, mid, lo = _split3(psum)
    ovl = ovl_ref[...]
    imp = _dot(hi, ovl) + _dot(mid, ovl) + _dot(lo, ovl)
    n_slc = -(-(past_len + ts) // SLC_BLOCK)
    blk = lax.broadcasted_iota(jnp.int32, (rq, LANES), 1)
    qpos = past_len + (lax.broadcasted_iota(jnp.int32, (rq, LANES), 0) & (ts - 1))
    qblk = qpos >> 6
    forced = (blk == 0) | (blk == qblk) | (blk == qblk - 1)
    valid = ((blk << 6) <= qpos) & (blk < n_slc)
    imp = jnp.where(valid, imp + jnp.where(forced, FORCE_SCORE, 0.0), NEG_INF)
    cnt = jnp.zeros((rq, LANES), jnp.int32)
    for sidx in range(n_slc):
        r = imp[:, sidx:sidx + 1]
        beats = (r > imp) | ((r == imp) & (blk > sidx))
        cnt = cnt + jnp.where(beats, 1, 0)
    sel = jnp.where((cnt < N_SEL) & valid, 1.0, 0.0)
    sel = jnp.concatenate([sel] * GROUP, axis=0)

    kc_t = jnp.concatenate([p_[...] for p_ in kpages], axis=1).astype(BF16)
    vc_t = jnp.concatenate([p_[...] for p_ in vpages], axis=1).astype(BF16)
    lane_r = lax.broadcasted_iota(jnp.int32, (rows, LANES), 1)
    per_tile = LANES // SLC_BLOCK
    tiles = []
    for j in range(past_len // LANES):
        m = sel[:, per_tile * j:per_tile * j + 1]
        for c in range(1, per_tile):
            m = jnp.where(lane_r >= c * SLC_BLOCK, sel[:, per_tile * j + c:per_tile * j + c + 1], m)
        tiles.append(m)
    mexp = jnp.concatenate(tiles, axis=1)
    s1 = jnp.where(mexp > 0.5, _dot(q, kc_t) + stab_ref[...], NEG_INF)
    last = sel[:, n_slc - 1:n_slc]
    s2 = jnp.where(last > 0.5, _dot_nt(q, ksn_ref[b].astype(BF16)) + sntab_ref[...], NEG_INF)
    m = jnp.maximum(jnp.max(s1, axis=-1, keepdims=True), jnp.max(s2, axis=-1, keepdims=True))
    m = jnp.where(m == NEG_INF, 0.0, m)
    p1 = jnp.exp(s1 - m)
    p2 = jnp.exp(s2 - m)
    l = jnp.sum(p1, axis=-1, keepdims=True) + jnp.sum(p2, axis=-1, keepdims=True)
    o_s = (_dot_nt(p1.astype(BF16), vc_t) + _dot(p2.astype(BF16), vsn_ref[b].astype(BF16))) / jnp.maximum(l, 1e-30)

    s1 = _dot(q, kwc_ref[b].astype(BF16)) + wtab_ref[...]
    s2 = _dot_nt(q, kwn_ref[b].astype(BF16)) + wntab_ref[...]
    m = jnp.maximum(jnp.max(s1, axis=-1, keepdims=True), jnp.max(s2, axis=-1, keepdims=True))
    m = jnp.where(m == NEG_INF, 0.0, m)
    p1 = jnp.exp(s1 - m)
    p2 = jnp.exp(s2 - m)
    l = jnp.sum(p1, axis=-1, keepdims=True) + jnp.sum(p2, axis=-1, keepdims=True)
    o_w = (_dot_nt(p1.astype(BF16), vwc_ref[b].astype(BF16))
           + _dot(p2.astype(BF16), vwn_ref[b].astype(BF16))) / jnp.maximum(l, 1e-30)

    o_ref[b] = gates[:, 0:1] * o_c + gates[:, 1:2] * o_s + gates[:, 2:3] * o_w


def _attn_sample(q_s, gates_s, kcmp, vcmp, ks_new, vs_new, kw_new, vw_new, cache_ks, cache_vs,
                 cache_kw, cache_vw, pt_flat, fext, n_batch, ts, n_pages, page):
    past_len = n_pages * page
    w_buf = cache_kw.shape[2]
    rows = GROUP * N_KV_HEADS * ts
    n_new = 8
    n_cmp_pad = kcmp.shape[1]
    n_cmp = n_cmp_pad - 1
    n_slc = -(-(past_len + ts) // SLC_BLOCK)

    q5 = q_s.reshape(n_batch, ts, N_KV_HEADS, GROUP, HEAD_DIM).transpose(0, 3, 2, 1, 4)
    eye = jnp.eye(N_KV_HEADS, dtype=q_s.dtype)
    qr = jnp.einsum("bghtd,hk->bghtkd", q5, eye).reshape(n_batch, rows, LANES).astype(BF16)
    g5 = gates_s[:, :N_HEADS * N_BRANCH].reshape(n_batch, ts, N_KV_HEADS, GROUP, N_BRANCH).transpose(0, 3, 2, 1, 4)
    gr = jnp.pad(g5.reshape(n_batch, rows, N_BRANCH), ((0, 0), (0, 0), (0, LANES - N_BRANCH)))
    pad_new = lambda a: jnp.pad(a.reshape(n_batch, ts, KV_DIM), ((0, 0), (0, n_new - ts), (0, 0)))
    ks_new, vs_new, kw_new, vw_new = map(pad_new, (ks_new, vs_new, kw_new, vw_new))

    g_i, h_i, t_i = np.meshgrid(np.arange(GROUP), np.arange(N_KV_HEADS), np.arange(ts), indexing="ij")
    head = (h_i * GROUP + g_i).reshape(rows)
    tq = t_i.reshape(rows)
    pos_q = past_len + tq

    f_rows = fext[jnp.asarray(head)]

    def table(d, valid):
        return jnp.take_along_axis(f_rows, jnp.asarray(_bias_index(d, valid)), axis=1)

    nn = np.arange(n_cmp_pad)
    d = pos_q[:, None] - (nn[None, :] * CMP_STRIDE + CMP_BLOCK - 1)
    ctab = table(d, (d >= 0) & (nn < n_cmp)[None, :])
    near = np.arange(past_len - MAX_DISTANCE, past_len)
    d = pos_q[:, None] - near[None, :]
    assert past_len >= MAX_DISTANCE and d.min() >= 0
    stab = jnp.concatenate([jnp.zeros((rows, past_len - MAX_DISTANCE), F32), table(d, d >= 0)], axis=1)
    jn = np.arange(n_new)
    d = tq[:, None] - jn[None, :]
    sntab = table(d, (d >= 0) & (jn < ts)[None, :])
    pos_w = past_len - w_buf + np.arange(w_buf)
    d = pos_q[:, None] - pos_w[None, :]
    wtab = table(d, (d >= 0) & (d < WINDOW) & (pos_w >= 0)[None, :])
    wntab = table(tq[:, None] - jn[None, :], (tq[:, None] >= jn[None, :]) & (jn < ts)[None, :])

    c_start = nn * CMP_STRIDE
    s_start = np.arange(LANES) * SLC_BLOCK
    ovl = jnp.asarray((c_start[:, None] < s_start[None, :] + SLC_BLOCK) & (c_start[:, None] + CMP_BLOCK > s_start[None, :])
                      & (nn < n_cmp)[:, None] & (np.arange(LANES) < n_slc)[None, :], BF16)

    n_seq = SAMPLE_SEQS if n_batch % SAMPLE_SEQS == 0 else 1
    n_ops = n_seq * n_pages
    const = lambda a: pl.BlockSpec(a.shape, lambda b, pt: (0,) * a.ndim)
    per_b = lambda a: pl.BlockSpec((n_seq,) + a.shape[1:], lambda b, pt: (b,) + (0,) * (a.ndim - 1))
    any_spec = pl.BlockSpec(memory_space=pl.ANY)
    small = [qr, gr, kcmp, vcmp, ks_new, vs_new, cache_kw, cache_vw, kw_new, vw_new]
    consts = [ctab, stab, sntab, wtab, wntab, ovl]
    page_buf = pltpu.VMEM((2, n_ops, KV_DIM, page), F32)
    grid_spec = pltpu.PrefetchScalarGridSpec(
        num_scalar_prefetch=1, grid=(n_batch // n_seq,),
        in_specs=[any_spec, any_spec] + [per_b(a) for a in small] + [const(a) for a in consts],
        out_specs=pl.BlockSpec((n_seq, rows, LANES), lambda b, pt: (b, 0, 0)),
        scratch_shapes=[page_buf, page_buf, pltpu.SemaphoreType.DMA((2,)), pltpu.SemaphoreType.DMA((2,))])
    o = pl.pallas_call(
        functools.partial(_attn_sample_body, n_seq, n_pages, ts, past_len),
        grid_spec=grid_spec,
        out_shape=jax.ShapeDtypeStruct((n_batch, rows, LANES), F32),
        compiler_params=_cparams("arbitrary"),
        name="attn_sample",
    )(pt_flat, cache_ks, cache_vs, *small, *consts)
    o6 = o.reshape(n_batch, GROUP, N_KV_HEADS, ts, N_KV_HEADS, HEAD_DIM)
    o5 = jnp.stack([o6[:, :, h, :, h] for h in range(N_KV_HEADS)], axis=2)
    return o5.transpose(0, 3, 2, 1, 4).reshape(n_batch * ts, ATTN_DIM).astype(BF16)


def _post1_body(tm, x_ref, co_ref, at_ref, wo_ref, nf_ref, wr_ref, br_ref, tri_ref, run0_ref,
                h_ref, hn_ref, rt_ref, cnt_ref, run_ref):
    @pl.when(pl.program_id(0) == 0)
    def _():
        run_ref[...] = run0_ref[...]

    h = x_ref[...] + _dot(co_ref[...], wo_ref[0:CONV_DIM]) + _dot(at_ref[...], wo_ref[CONV_DIM:CONV_DIM + ATTN_DIM])
    hn = _rms(h, nf_ref[...])
    h_ref[...] = h
    _store_token_tiles(hn_ref, hn, tm)

    hi = hn.astype(BF16)
    lo = (hn - hi.astype(F32)).astype(BF16)
    wr = wr_ref[...]
    whi = wr.astype(BF16)
    wlo = (wr - whi.astype(F32)).astype(BF16)
    both = _dot(hi, jnp.concatenate([whi, wlo], axis=1))
    logits = both[:, :LANES] + both[:, LANES:] + _dot(lo, whi) + br_ref[...]

    lane_i = lax.broadcasted_iota(jnp.int32, (tm, LANES), 1)
    lane = lane_i.astype(F32)
    big = float(LANES)
    gmask = (lane_i >= ROUTER_GROUP_LANE) & (lane_i < ROUTER_GROUP_LANE + N_GROUPS)
    lg = jnp.where(gmask, logits, NEG_INF)
    eg = jnp.exp(lg - jnp.max(lg, axis=-1, keepdims=True))
    pg = eg / jnp.sum(eg, axis=-1, keepdims=True)
    gw = jnp.max(pg, axis=-1, keepdims=True)
    grp = jnp.min(jnp.where(gmask & (pg == gw), lane, big), axis=-1, keepdims=True) - ROUTER_GROUP_LANE

    group_of_lane = (lane_i >> 3).astype(F32)
    emask = (lane_i < N_EXPERTS) & (group_of_lane == grp)
    le = jnp.where(emask, logits, NEG_INF)
    ee = jnp.exp(le - jnp.max(le, axis=-1, keepdims=True))
    pe = jnp.where(emask, ee / jnp.sum(ee, axis=-1, keepdims=True), -1.0)
    v1 = jnp.max(pe, axis=-1, keepdims=True)
    i1 = jnp.min(jnp.where(pe == v1, lane, big), axis=-1, keepdims=True)
    pe2 = jnp.where(lane == i1, -1.0, pe)
    v2 = jnp.max(pe2, axis=-1, keepdims=True)
    i2 = jnp.min(jnp.where(pe2 == v2, lane, big), axis=-1, keepdims=True)
    tot = v1 + v2
    w1 = v1 / tot * gw
    w2 = v2 / tot * gw

    oh1 = jnp.where(lane == i1, 1.0, 0.0)
    oh2 = jnp.where(lane == i2, 1.0, 0.0)
    both = oh1 + oh2
    before = _dot(tri_ref[...], both.astype(BF16)) + run_ref[0:1]
    r1 = jnp.sum(oh1 * before, axis=-1, keepdims=True)
    r2 = jnp.sum(oh2 * before, axis=-1, keepdims=True)
    run = run_ref[0:1] + jnp.sum(both, axis=0, keepdims=True)
    run_ref[...] = jnp.broadcast_to(run, run_ref.shape)
    cnt_ref[...] = jnp.broadcast_to(run, cnt_ref.shape)

    rt = jnp.where(lane_i == 0, i1, 0.0)
    rt = jnp.where(lane_i == 1, i2, rt)
    rt = jnp.where(lane_i == 2, r1, rt)
    rt = jnp.where(lane_i == 3, r2, rt)
    rt = jnp.where(lane_i == 4, w1, rt)
    rt = jnp.where(lane_i == 5, w2, rt)
    rt_ref[...] = rt


def _post1(x2d, co, at, wo, nf, wr, br, run0, tm):
    n, d = x2d.shape
    tri = jnp.asarray(np.tril(np.ones((tm, tm), np.float32), -1), BF16)
    rows = lambda w: pl.BlockSpec((tm, w), lambda i: (i, 0))
    const = lambda a: pl.BlockSpec(a.shape, lambda i: (0,) * a.ndim)
    return pl.pallas_call(
        functools.partial(_post1_body, tm),
        grid=(n // tm,),
        in_specs=[rows(d), rows(CONV_DIM), rows(ATTN_DIM), const(wo), const(nf), const(wr), const(br),
                  const(tri), const(run0)],
        out_specs=[rows(d), pl.BlockSpec((tm * TOKEN_TILE_ROWS, LANES), lambda i: (i, 0)), rows(LANES),
                   pl.BlockSpec((8, LANES), lambda i: (0, 0))],
        out_shape=[jax.ShapeDtypeStruct((n, d), F32), jax.ShapeDtypeStruct((n * TOKEN_TILE_ROWS, LANES), F32),
                   jax.ShapeDtypeStruct((n, LANES), F32), jax.ShapeDtypeStruct((8, LANES), F32)],
        scratch_shapes=[pltpu.VMEM((8, LANES), F32)],
        compiler_params=_cparams("arbitrary"),
        name="post1",
    )(x2d, co, at, wo, nf, wr, br, tri, run0)


def _token_copy(src_ref, dst_ref, s, d, sem):
    r = TOKEN_TILE_ROWS
    return pltpu.make_async_copy(src_ref.at[pl.ds(pl.multiple_of(s * r, r), r)],
                                 dst_ref.at[pl.ds(pl.multiple_of(d * r, r), r)], sem)


def _scatter_rows_body(ts, dest_ref, src_ref, init_ref, out_ref, sem):
    del init_ref
    base = pl.program_id(0) * (2 * ts)

    def issue(t, _):
        for k in range(2):
            _token_copy(src_ref, out_ref, t, dest_ref[base + 2 * t + k], sem).start(priority=k)
        return 0

    lax.fori_loop(0, ts, issue, 0, unroll=DMA_ISSUE_UNROLL)
    for _ in range(2):
        pltpu.make_async_copy(src_ref, out_ref.at[pl.ds(0, ts * TOKEN_TILE_ROWS)], sem).wait()


def _scatter_rows(dest, src, slots):
    n_tok = dest.shape[0] // 2
    ts = min(SCATTER_TOKENS, n_tok)
    assert n_tok % ts == 0
    any_spec = pl.BlockSpec(memory_space=pl.ANY)
    return pl.pallas_call(
        functools.partial(_scatter_rows_body, ts),
        grid_spec=pltpu.PrefetchScalarGridSpec(
            num_scalar_prefetch=1, grid=(n_tok // ts,),
            in_specs=[pl.BlockSpec((ts * TOKEN_TILE_ROWS, LANES), lambda i, dest: (i, 0)), any_spec],
            out_specs=any_spec, scratch_shapes=[pltpu.SemaphoreType.DMA(())]),
        out_shape=jax.ShapeDtypeStruct(slots.shape, slots.dtype),
        input_output_aliases={2: 0},
        compiler_params=pltpu.CompilerParams(dimension_semantics=("arbitrary",)),
        name="scatter_rows",
    )(dest, src, slots)


def _experts_body(be_ref, nu_ref, x_ref, wg_ref, wu_ref, wd_ref, o_ref, wg_s, wu_s, wd_s):
    i = pl.program_id(0)

    @pl.when(i < nu_ref[0])
    def _():
        prev = be_ref[jnp.maximum(i - 1, 0)]

        @pl.when((i == 0) | (be_ref[i] != prev))
        def _():
            wg_s[...] = wg_ref[0].astype(BF16)
            wu_s[...] = wu_ref[0].astype(BF16)
            wd_s[...] = wd_ref[0].astype(BF16)

        x = _load_token_tiles(x_ref, EXPERT_ROWS, TOKEN_TILE_ROWS).astype(BF16)
        g = _dot(x, wg_s[...])
        u = _dot(x, wu_s[...])
        a = g * _sigmoid(g) * u
        _store_token_tiles(o_ref, _dot(a.astype(BF16), wd_s[...]), EXPERT_ROWS)

    @pl.when(i >= nu_ref[0])
    def _():
        o_ref[...] = jnp.zeros_like(o_ref)


def _experts(blk_expert, n_used, xs, wg, wu, wd):
    blk_rows = EXPERT_ROWS * TOKEN_TILE_ROWS
    n_blk = xs.shape[0] // blk_rows
    _, d, de = wg.shape
    xmap = lambda i, be, nu: (jnp.minimum(i, jnp.maximum(nu[0] - 1, 0)), 0)
    wmap = lambda i, be, nu: (be[jnp.minimum(i, jnp.maximum(nu[0] - 1, 0))], 0, 0)
    grid_spec = pltpu.PrefetchScalarGridSpec(
        num_scalar_prefetch=2, grid=(n_blk,),
        in_specs=[pl.BlockSpec((blk_rows, LANES), xmap), pl.BlockSpec((1, d, de), wmap),
                  pl.BlockSpec((1, d, de), wmap), pl.BlockSpec((1, de, d), wmap)],
        out_specs=pl.BlockSpec((blk_rows, LANES), lambda i, be, nu: (i, 0)),
        scratch_shapes=[pltpu.VMEM((d, de), BF16), pltpu.VMEM((d, de), BF16), pltpu.VMEM((de, d), BF16)])
    return pl.pallas_call(
        _experts_body, grid_spec=grid_spec,
        out_shape=jax.ShapeDtypeStruct(xs.shape, F32),
        compiler_params=_cparams("arbitrary"),
        name="experts",
    )(blk_expert, n_used, xs, wg, wu, wd)


def _post2_body(tm, dest_ref, h_ref, rt_ref, p_ref, yb_ref, wple_ref, wpg_ref, bpg_ref, np_ref, o_ref, buf, sem):
    i = pl.program_id(0)
    n = pl.num_programs(0)

    def fetch(step, slot):
        base = step * (2 * tm)

        def issue(t, _):
            for k in range(2):
                _token_copy(yb_ref, buf.at[slot], dest_ref[base + 2 * t + k], k * tm + t, sem.at[slot]).start(priority=k)
            return 0

        lax.fori_loop(0, tm, issue, 0, unroll=DMA_ISSUE_UNROLL)

    @pl.when(i == 0)
    def _():
        fetch(0, 0)

    @pl.when(i + 1 < n)
    def _():
        fetch(i + 1, (i + 1) & 1)

    slot = i & 1

    pltpu.make_async_copy(yb_ref.at[pl.ds(0, 2 * tm * TOKEN_TILE_ROWS)], buf.at[slot], sem.at[slot]).wait()
    rt = rt_ref[...]
    y0 = _load_token_tiles(buf.at[slot], tm, TOKEN_TILE_ROWS)
    y1 = _load_token_tiles(buf.at[slot], tm, TOKEN_TILE_ROWS, first=tm * TOKEN_TILE_ROWS)
    h = h_ref[...] + (y0 * rt[:, 4:5] + y1 * rt[:, 5:6])
    gate = _sigmoid(_dot(_rms(h, np_ref[...]).astype(BF16), wpg_ref[...]) + bpg_ref[...])
    o_ref[...] = h + gate * _dot(p_ref[...].astype(BF16), wple_ref[...])


def _post2(dest, h, rt, p2d, yb, wple, wpg, bpg, npl, tm):
    n, d = h.shape
    rows = lambda w: pl.BlockSpec((tm, w), lambda i, dest: (i, 0))
    const = lambda a: pl.BlockSpec(a.shape, lambda i, dest: (0,) * a.ndim)
    grid_spec = pltpu.PrefetchScalarGridSpec(
        num_scalar_prefetch=1, grid=(n // tm,),
        in_specs=[rows(d), rows(LANES), rows(p2d.shape[1]), pl.BlockSpec(memory_space=pl.ANY), const(wple),
                  const(wpg), const(bpg), const(npl)],
        out_specs=rows(d),
        scratch_shapes=[pltpu.VMEM((2, 2 * tm * TOKEN_TILE_ROWS, LANES), F32), pltpu.SemaphoreType.DMA((2,))])
    return pl.pallas_call(
        functools.partial(_post2_body, tm),
        grid_spec=grid_spec,
        out_shape=jax.ShapeDtypeStruct((n, d), F32),
        compiler_params=_cparams("arbitrary"),
        name="post2",
    )(dest, h, rt, p2d, yb, wple, wpg, bpg, npl)


def _row_tile(n, cap=512):
    t = min(cap, n)
    assert n % t == 0 and t % 8 == 0
    return t


def kernel(x_prompt, x_sample, p_prompt, p_sample, cache_k_cmp, cache_v_cmp, cache_k_slc, cache_v_slc, cache_k_win, cache_v_win, state_conv, page_table, w_in, w_out, conv_w, norm_mix, norm_ffn, norm_ple, q_norm, k_norm, cmp_pe_k, cmp_w1_k, cmp_w2_k, cmp_pe_v, cmp_w1_v, cmp_w2_v, rel_bias, w_router_group, b_router_group, w_router_expert, b_router_expert, w_exp_gate, w_exp_up, w_exp_down, w_ple, w_ple_gate, b_ple_gate):
    assert w_in.shape[0] == 1, "single-layer step"
    bp, t, d = x_prompt.shape
    bs, ts, _ = x_sample.shape
    n_pages = page_table.shape[1]
    page = cache_k_cmp.shape[2]
    past_len = n_pages * page
    w_buf = cache_k_win.shape[2]
    n_phys = cache_k_cmp.shape[1]
    assert t % Q_BLOCK == 0 and t >= WINDOW + Q_BLOCK and page % CMP_STRIDE == 0 and ts == 4 and d == D_MODEL
    assert past_len % SLC_BLOCK == 0
    np_rows, ns_rows = bp * t, bs * ts

    row = lambda v: v.reshape(1, -1).astype(F32)
    w_in_b = jnp.pad(w_in[0], ((0, 0), (0, Z_COLS - w_in.shape[2]))).astype(BF16)
    qn = row(jnp.tile(q_norm[0], N_HEADS))
    kn1 = row(jnp.tile(k_norm[0, 1], N_KV_HEADS))
    kn2 = row(jnp.tile(k_norm[0, 2], N_KV_HEADS))
    bd = jnp.asarray(np.kron(np.eye(N_HEADS), np.ones((HEAD_DIM, HEAD_DIM))), BF16)
    pw = (row(norm_mix[0]), w_in_b, qn, kn1, kn2, conv_w[0].astype(F32), bd)
    cw_k = _compress_weights(cmp_pe_k[0], cmp_w1_k[0], cmp_w2_k[0])
    cw_v = _compress_weights(cmp_pe_v[0], cmp_w1_v[0], cmp_w2_v[0])
    kn0 = row(k_norm[0, 0])
    fext = _bias_by_distance(rel_bias)
    pt_flat = page_table.reshape(-1).astype(jnp.int32)
    wr = jnp.zeros((d, LANES), F32).at[:, :N_EXPERTS].set(w_router_expert[0])
    wr = wr.at[:, ROUTER_GROUP_LANE:ROUTER_GROUP_LANE + N_GROUPS].set(w_router_group[0])
    br = jnp.zeros((1, LANES), F32).at[0, :N_EXPERTS].set(b_router_expert[0])
    br = br.at[0, ROUTER_GROUP_LANE:ROUTER_GROUP_LANE + N_GROUPS].set(b_router_group[0])
    wo_b = w_out[0].astype(BF16)
    wple_b = w_ple[0].astype(BF16)
    wpg_b = w_ple_gate[0].astype(BF16)

    tm_p = _row_tile(t, cap=1024)
    (co_p, q_p, kw_p, vw_p, gt_p, cs_p, ksb, vsb, kwb, vwb, kc_t, vc_t, ks_t, vs_t, kc_x, vc_x) = _project(
        x_prompt.reshape(np_rows, d), bp, t, tm_p, pw)
    chunk_w = CMP_STRIDE * KV_DIM
    kcmp_p = _compress_rows(kc_x.reshape(bp, t // CMP_STRIDE, chunk_w), cw_k, kn0, True)
    vcmp_p = _compress_rows(vc_x.reshape(bp, t // CMP_STRIDE, chunk_w), cw_v, kn0, False)
    at_p = _attn_prompt(q_p, gt_p, kcmp_p, vcmp_p, ksb, vsb, kwb, vwb, fext, bp, t)

    st = state_conv[0].astype(F32)
    s0 = jnp.repeat(st[:, 0], ts, axis=0)
    s1 = jnp.repeat(st[:, 1], ts, axis=0)
    (co_s, q_s, kc_s, vc_s, ks_s, vs_s, kw_s, vw_s, gt_s, u_s) = _project(
        x_sample.reshape(ns_rows, d), bs, ts, ns_rows, pw, state=(s0, s1))
    feature_major = lambda c, n, rows_: jnp.transpose(c[0], (0, 2, 3, 1)).reshape(n, KV_DIM, rows_)
    kcmp_s = _compress_pages(feature_major(cache_k_cmp, n_phys, page), pt_flat, bs, n_pages, cw_k, kn0, True)
    vcmp_s = _compress_pages(feature_major(cache_v_cmp, n_phys, page), pt_flat, bs, n_pages, cw_v, kn0, False)
    at_s = _attn_sample(q_s, gt_s, kcmp_s, vcmp_s, ks_s, vs_s, kw_s, vw_s,
                        feature_major(cache_k_slc, n_phys, page), feature_major(cache_v_slc, n_phys, page),
                        feature_major(cache_k_win, bs, w_buf), feature_major(cache_v_win, bs, w_buf),
                        pt_flat, fext, bs, ts, n_pages, page)

    tp1 = _row_tile(np_rows)
    ts1 = _row_tile(ns_rows)
    nf = row(norm_ffn[0])
    h_p, hn_p, rt_p, cnt_p = _post1(x_prompt.reshape(np_rows, d), co_p, at_p, wo_b, nf, wr, br,
                                    jnp.zeros((8, LANES), F32), tp1)
    h_s, hn_s, rt_s, cnt_s = _post1(x_sample.reshape(ns_rows, d), co_s, at_s, wo_b, nf, wr, br, cnt_p, ts1)

    counts = cnt_s[0, :N_EXPERTS].astype(jnp.int32)
    padded = (counts + EXPERT_ROWS - 1) // EXPERT_ROWS * EXPERT_ROWS
    pad_end = jnp.cumsum(padded)
    pad_start = pad_end - padded
    n_assign = 2 * (np_rows + ns_rows)
    n_blk = (n_assign + N_EXPERTS * (EXPERT_ROWS - 1) + EXPERT_ROWS - 1) // EXPERT_ROWS
    blk_first = jnp.arange(n_blk, dtype=jnp.int32) * EXPERT_ROWS
    blk_expert = jnp.minimum(jnp.sum((pad_end[None, :] <= blk_first[:, None]).astype(jnp.int32), axis=1),
                             N_EXPERTS - 1)
    n_used = (pad_end[-1:] // EXPERT_ROWS).astype(jnp.int32)

    def dest_of(rt):
        e = rt[:, 0:2].astype(jnp.int32).reshape(-1)
        return pad_start[e] + rt[:, 2:4].astype(jnp.int32).reshape(-1)

    dest_p = dest_of(rt_p)
    dest_s = dest_of(rt_s)

    xs = jnp.zeros((n_blk * EXPERT_ROWS * TOKEN_TILE_ROWS, LANES), F32)
    xs = _scatter_rows(dest_p, hn_p, xs)
    xs = _scatter_rows(dest_s, hn_s, xs)
    yb = _experts(blk_expert, n_used, xs, w_exp_gate[0], w_exp_up[0], w_exp_down[0])

    bpg = row(b_ple_gate[0])
    npl = row(norm_ple[0])
    y_p = _post2(dest_p, h_p, rt_p, p_prompt[0].reshape(np_rows, -1), yb, wple_b, wpg_b, bpg, npl, tp1)
    y_s = _post2(dest_s, h_s, rt_s, p_sample[0].reshape(ns_rows, -1), yb, wple_b, wpg_b, bpg, npl, ts1)

    kv5 = lambda a, b, s: a.reshape(1, b, s, N_KV_HEADS, HEAD_DIM)
    wp = min(WINDOW, t)
    win_p = lambda a: kv5(a.reshape(bp, t, KV_DIM)[:, t - wp:], bp, wp)
    win_s = lambda c, new: jnp.concatenate([c[0], new.reshape(bs, ts, N_KV_HEADS, HEAD_DIM)], axis=1)[None, :, ts:]
    conv_p = cs_p[:, 8 - (CONV_K - 1):][None]
    conv_s = u_s.reshape(bs, ts, CONV_DIM)[:, ts - (CONV_K - 1):][None]
    from_t = lambda a: jnp.transpose(a.reshape(bp, N_KV_HEADS, HEAD_DIM, t), (0, 3, 1, 2))[None]
    return (y_p.reshape(bp, t, d), y_s.reshape(bs, ts, d),
            from_t(kc_t), from_t(vc_t), from_t(ks_t), from_t(vs_t), win_p(kw_p), win_p(vw_p), conv_p,
            kv5(kc_s, bs, ts), kv5(vc_s, bs, ts), kv5(ks_s, bs, ts), kv5(vs_s, bs, ts),
            win_s(cache_k_win, kw_s), win_s(cache_v_win, vw_s), conv_s)
```

```python
import functools
import math

import numpy as np
import jax
import jax.numpy as jnp
from jax import lax
from jax.experimental import pallas as pl
from jax.experimental.pallas import tpu as pltpu

F32 = jnp.float32
BF16 = jnp.bfloat16
NEG_INF = float("-inf")
MASKED = -1e30

HEAD_DIM = 64
N_HEADS = 8
N_KV_HEADS = 2
GROUP = N_HEADS // N_KV_HEADS
CONV_DIM = 512
ATTN_DIM = 512
KV_DIM = N_KV_HEADS * HEAD_DIM
N_BRANCH = 3
CONV_K = 3
CMP_BLOCK = 32
CMP_STRIDE = 16
CMP_HIDDEN = 256
SLC_BLOCK = 64
N_SEL = 16
WINDOW = 512
Q_BLOCK = 128
FORCE_SCORE = 1e4
NUM_BUCKETS = 32
MAX_DISTANCE = 128
N_GROUPS = 4
EXPERTS_PER_GROUP = 8
N_EXPERTS = N_GROUPS * EXPERTS_PER_GROUP
D_EXPERT = 512
EPS = 1e-6

D_MODEL = 1024
LANES = 128
TOKEN_TILE_ROWS = D_MODEL // LANES
Z_COLS = 3 * CONV_DIM + ATTN_DIM + 6 * KV_DIM + LANES
BIAS_DMAX = 768
EXPERT_ROWS = 512
ROUTER_GROUP_LANE = 32
SCATTER_TOKENS = 512
DMA_ISSUE_UNROLL = 8
COMPRESS_SEQS = 4
CHUNK_PITCH = 24
SAMPLE_SEQS = 4
VMEM_LIMIT = 56 * 1024 * 1024


def _cparams(*sem):
    return pltpu.CompilerParams(dimension_semantics=sem, vmem_limit_bytes=VMEM_LIMIT)


def _dot(a, b):
    return jnp.dot(a, b, preferred_element_type=F32)


def _dot_nt(a, b):
    return lax.dot_general(a, b, (((1,), (1,)), ((), ())), preferred_element_type=F32)


def _split3(x):
    hi = x.astype(BF16)
    r = x - hi.astype(F32)
    mid = r.astype(BF16)
    lo = (r - mid.astype(F32)).astype(BF16)
    return hi, mid, lo


def _rms(x, g):
    return x * lax.rsqrt(jnp.mean(x * x, axis=-1, keepdims=True) + EPS) * g


def _head_rms(x, bd, g):
    hi, mid, _ = _split3(x * x)
    ss = _dot(hi, bd) + _dot(mid, bd)
    return x * lax.rsqrt(ss * (1.0 / HEAD_DIM) + EPS) * g


def _sigmoid(x):
    return 1.0 / (1.0 + jnp.exp(-x))


def _store_token_tiles(ref, x, n):
    r = x.shape[1] // LANES
    for j in range(r):
        ref[pl.ds(j, n, stride=r), :] = x[:, j * LANES:(j + 1) * LANES]


def _load_token_tiles(ref, n, r, first=0):
    return jnp.concatenate([ref[pl.ds(first + j, n, stride=r), :] for j in range(r)], axis=-1)


def _softmax_parts(s):
    m = jnp.max(s, axis=-1, keepdims=True)
    m = jnp.where(m == NEG_INF, 0.0, m)
    p = jnp.exp(s - m)
    l = jnp.sum(p, axis=-1, keepdims=True)
    return p, l


def _proj_body(sample, tm, *refs):
    if sample:
        (x_ref, nm_ref, w_ref, qn_ref, kn1_ref, kn2_ref, cw_ref, bd_ref, s0_ref, s1_ref,
         co_ref, q_ref, kc_ref, vc_ref, ks_ref, vs_ref, kw_ref, vw_ref, gt_ref, u_ref) = refs
    else:
        (x_ref, nm_ref, w_ref, qn_ref, kn1_ref, kn2_ref, cw_ref, bd_ref,
         co_ref, q_ref, kw_ref, vw_ref, gt_ref, cs_ref,
         ksb_ref, vsb_ref, kwb_ref, vwb_ref, kct_ref, vct_ref, kst_ref, vst_ref, kcx_ref, vcx_ref,
         carry_ref, stage_ref) = refs

    xn = _rms(x_ref[...], nm_ref[...]).astype(BF16)

    def seg(a, b):
        return _dot(xn, w_ref[:, a:b])

    c3 = 3 * CONV_DIM
    u = seg(2 * CONV_DIM, c3) * seg(0, CONV_DIM)
    bg = seg(CONV_DIM, 2 * CONV_DIM)
    row = lax.broadcasted_iota(jnp.int32, (tm, 1), 0)
    um1 = pltpu.roll(u, 1, axis=0)
    um2 = pltpu.roll(u, 2, axis=0)
    if sample:
        r = row & 3
        s0 = s0_ref[...]
        s1 = s1_ref[...]
        prev1 = jnp.where(r == 0, s1, um1)
        prev2 = jnp.where(r == 0, s0, jnp.where(r == 1, s1, um2))
        u_ref[...] = u
    else:
        @pl.when(pl.program_id(1) == 0)
        def _():
            carry_ref[...] = jnp.zeros_like(carry_ref)
        c = carry_ref[...]
        prev1 = jnp.where(row == 0, c[7:8], um1)
        prev2 = jnp.where(row == 0, c[6:7], jnp.where(row == 1, c[7:8], um2))
        carry_ref[...] = u[tm - 8:tm]
        cs_ref[0] = u[tm - 8:tm]
    cw = cw_ref[...]
    y = cw[0:1] * prev2 + cw[1:2] * prev1 + cw[2:3] * u
    co_ref[...] = (bg * y).astype(BF16)

    bd = bd_ref[...]
    q = _head_rms(seg(c3, c3 + ATTN_DIM), bd, qn_ref[...]) * (HEAD_DIM ** -0.5)
    lane = lax.broadcasted_iota(jnp.int32, (tm, LANES), 1)
    low = lane < HEAD_DIM

    def head_planes(x, fill):
        return [jnp.where(low, x if h == 0 else pltpu.roll(x, HEAD_DIM, axis=1), fill) for h in range(N_KV_HEADS)]

    if sample:
        q_ref[...] = q
    else:
        for hd in range(N_HEADS):
            pair = q[:, (hd // 2) * LANES:(hd // 2 + 1) * LANES]
            if hd % 2:
                pair = pltpu.roll(pair, HEAD_DIM, axis=1)
            q_ref[0, hd] = jnp.where(low, pair, 0.0).astype(BF16)

    k0 = c3 + ATTN_DIM
    bdk = bd[:KV_DIM, :KV_DIM]
    kc = seg(k0, k0 + KV_DIM)
    vc = seg(k0 + KV_DIM, k0 + 2 * KV_DIM)
    ks = _head_rms(seg(k0 + 2 * KV_DIM, k0 + 3 * KV_DIM), bdk, kn1_ref[...])
    vs = seg(k0 + 3 * KV_DIM, k0 + 4 * KV_DIM)
    kw = _head_rms(seg(k0 + 4 * KV_DIM, k0 + 5 * KV_DIM), bdk, kn2_ref[...])
    vw = seg(k0 + 5 * KV_DIM, k0 + 6 * KV_DIM)
    kw_ref[...] = kw
    vw_ref[...] = vw
    if sample:
        kc_ref[...] = kc
        vc_ref[...] = vc
        ks_ref[...] = ks
        vs_ref[...] = vs
    else:
        kct_ref[0] = kc.T
        vct_ref[0] = vc.T
        kst_ref[0] = ks.T
        vst_ref[0] = vs.T
        for src, dst in ((kc, kcx_ref), (vc, vcx_ref)):
            stage_ref[...] = src
            dst[...] = jnp.concatenate([stage_ref[pl.ds(r, tm // CMP_STRIDE, stride=CMP_STRIDE), :]
                                        for r in range(CMP_STRIDE)], axis=-1)
    gates = _sigmoid(seg(k0 + 6 * KV_DIM, k0 + 6 * KV_DIM + LANES))
    if sample:
        gt_ref[...] = gates
    else:
        pos = pl.program_id(1) * tm + lax.broadcasted_iota(jnp.int32, (tm, LANES), 0)
        block_onehot = jnp.where(lane - HEAD_DIM == (pos >> 6), 1.0, 0.0)
        for h, (a, b, c, e) in enumerate(zip(head_planes(ks, block_onehot), head_planes(vs, 1.0),
                                             head_planes(kw, 0.0), head_planes(vw, 1.0))):
            ksb_ref[h] = a.astype(BF16)
            vsb_ref[h] = b.astype(BF16)
            kwb_ref[h] = c.astype(BF16)
            vwb_ref[h] = e.astype(BF16)
        gt_ref[0] = gates
        gt_ref[1] = pltpu.roll(gates, LANES - GROUP * N_BRANCH, axis=1)


def _project(x2d, batch, seq, tm, weights, state=None):
    n, d = x2d.shape
    sample = state is not None
    nt = seq // tm if not sample else 1
    const = lambda shape: pl.BlockSpec(shape, lambda b, t: (0,) * len(shape))
    rows = lambda w: pl.BlockSpec((tm, w), lambda b, t: (b * nt + t, 0))
    nm, w_in, qn, kn1, kn2, cw, bd = weights
    in_specs = [rows(d), const(nm.shape), const(w_in.shape), const(qn.shape), const(kn1.shape),
                const(kn2.shape), const(cw.shape), const(bd.shape)]
    args = [x2d, nm, w_in, qn, kn1, kn2, cw, bd]
    kv_f32 = [jax.ShapeDtypeStruct((n, KV_DIM), F32)] * 6
    if sample:
        in_specs += [rows(CONV_DIM), rows(CONV_DIM)]
        args += list(state)
        out_shape = ([jax.ShapeDtypeStruct((n, CONV_DIM), BF16), jax.ShapeDtypeStruct((n, ATTN_DIM), F32)]
                     + kv_f32 + [jax.ShapeDtypeStruct((n, LANES), F32), jax.ShapeDtypeStruct((n, CONV_DIM), F32)])
        out_specs = [rows(CONV_DIM), rows(ATTN_DIM)] + [rows(KV_DIM)] * 6 + [rows(LANES), rows(CONV_DIM)]
        scratch = []
        grid = (1, 1)
    else:
        planes = pl.BlockSpec((N_KV_HEADS, tm, LANES), lambda b, t: (0, b * nt + t, 0))
        out_shape = ([jax.ShapeDtypeStruct((n, CONV_DIM), BF16),
                      jax.ShapeDtypeStruct((batch, N_HEADS, seq, LANES), BF16)]
                     + kv_f32[:2] + [jax.ShapeDtypeStruct((N_KV_HEADS, n, LANES), F32),
                                     jax.ShapeDtypeStruct((batch, 8, CONV_DIM), F32)]
                     + [jax.ShapeDtypeStruct((N_KV_HEADS, n, LANES), BF16)] * 4
                     + [jax.ShapeDtypeStruct((batch, KV_DIM, seq), F32)] * 4
                     + [jax.ShapeDtypeStruct((n // CMP_STRIDE, CMP_STRIDE * KV_DIM), F32)] * 2)
        out_specs = ([rows(CONV_DIM), pl.BlockSpec((1, N_HEADS, tm, LANES), lambda b, t: (b, 0, t, 0))]
                     + [rows(KV_DIM)] * 2 + [planes, pl.BlockSpec((1, 8, CONV_DIM), lambda b, t: (b, 0, 0))]
                     + [planes] * 4 + [pl.BlockSpec((1, KV_DIM, tm), lambda b, t: (b, 0, t))] * 4
                     + [pl.BlockSpec((tm // CMP_STRIDE, CMP_STRIDE * KV_DIM), lambda b, t: (b * nt + t, 0))] * 2)
        scratch = [pltpu.VMEM((8, CONV_DIM), F32), pltpu.VMEM((tm, KV_DIM), F32)]
        grid = (batch, nt)
    return pl.pallas_call(
        functools.partial(_proj_body, sample, tm),
        grid=grid, in_specs=in_specs, out_specs=out_specs, out_shape=out_shape, scratch_shapes=scratch,
        compiler_params=_cparams("arbitrary", "arbitrary"),
        name="proj_sample" if sample else "proj_prompt",
    )(*args)


def _gelu_tanh(x):
    cdf = 0.5 * (1.0 + jnp.tanh(math.sqrt(2.0 / math.pi) * (x + 0.044715 * (x * x * x))))
    return x * cdf


def _compress_core(norm, x, pe_ref, we_ref, w2_ref, g_ref):
    n = x.shape[0]
    a0 = _dot((x + pe_ref[0:1]).astype(BF16), we_ref[0])
    a1 = _dot((x + pe_ref[1:2]).astype(BF16), we_ref[1])
    hid = a0 + pltpu.roll(a1, n - 1, axis=0)
    w2 = w2_ref[...]
    outs = []
    for h in range(N_KV_HEADS):
        act = _gelu_tanh(hid[:, h * CMP_HIDDEN:(h + 1) * CMP_HIDDEN])
        o = _dot(act.astype(BF16), w2)
        if norm:
            o = _rms(o, g_ref[...])
        outs.append(o)
    return outs


def _compress_rows_body(norm, x_ref, pe_ref, we_ref, w2_ref, g_ref, o_ref):
    outs = _compress_core(norm, x_ref[0], pe_ref, we_ref, w2_ref, g_ref)
    for h in range(N_KV_HEADS):
        o_ref[0, h] = jnp.concatenate([outs[h], jnp.zeros_like(outs[h])], axis=-1).astype(BF16)


def _fetch_pages(pt_ref, cache_ref, buf_ref, sem_ref, step, slot, n_ops, priority):
    base = step * n_ops

    def issue(j, _):
        pltpu.make_async_copy(cache_ref.at[pt_ref[base + j]], buf_ref.at[slot, j], sem_ref.at[slot]).start(
            priority=priority)
        return 0

    lax.fori_loop(0, n_ops, issue, 0)


def _paged_prefetch(pt_ref, caches, bufs, sems, n_ops):
    i = pl.program_id(0)
    slot = i & 1
    for k, (cache_ref, buf_ref, sem_ref) in enumerate(zip(caches, bufs, sems)):
        @pl.when(i == 0)
        def _():
            _fetch_pages(pt_ref, cache_ref, buf_ref, sem_ref, 0, 0, n_ops, k % 2)

        @pl.when(i + 1 < pl.num_programs(0))
        def _():
            _fetch_pages(pt_ref, cache_ref, buf_ref, sem_ref, i + 1, 1 - slot, n_ops, k % 2)

    for cache_ref, buf_ref, sem_ref in zip(caches, bufs, sems):
        pltpu.make_async_copy(cache_ref.at[pl.ds(0, n_ops)], buf_ref.at[slot], sem_ref.at[slot]).wait()
    return slot


def _compress_pages_body(norm, nch, n_seq, n_pages, pt_ref, cache_ref, pe_ref, we_ref, w2_ref, g_ref, o_ref,
                         rows_ref, buf_ref, sem_ref):
    n_ops = n_seq * n_pages
    slot = _paged_prefetch(pt_ref, [cache_ref], [buf_ref], [sem_ref], n_ops)
    page = buf_ref.shape[3]
    per_page = page // CMP_STRIDE
    for j in range(n_ops):
        rows = buf_ref[slot, j].T
        for c in range(per_page):
            first = (j * per_page + c) * CHUNK_PITCH
            rows_ref[first:first + CMP_STRIDE, :] = rows[c * CMP_STRIDE:(c + 1) * CMP_STRIDE]
    x = jnp.concatenate([rows_ref[pl.ds(r, n_seq * nch, stride=CHUNK_PITCH), :] for r in range(CMP_STRIDE)], axis=-1)
    out = jnp.concatenate(_compress_core(norm, x, pe_ref, we_ref, w2_ref, g_ref), axis=-1)
    for b in range(n_seq):
        o_ref[b] = out[b * nch:(b + 1) * nch]


def _compress_weights(pe, w1, w2):
    w1r = w1.reshape(2, CMP_STRIDE, HEAD_DIM, CMP_HIDDEN).astype(BF16)
    z = jnp.zeros_like(w1r)
    we = jnp.stack([jnp.concatenate([w1r, z], axis=-1), jnp.concatenate([z, w1r], axis=-1)], axis=2)
    we = we.reshape(2, CMP_STRIDE * KV_DIM, N_KV_HEADS * CMP_HIDDEN)
    per = pe.reshape(2, CMP_STRIDE, 1, HEAD_DIM)
    pex = jnp.broadcast_to(per, (2, CMP_STRIDE, N_KV_HEADS, HEAD_DIM)).reshape(2, CMP_STRIDE * KV_DIM)
    return pex.astype(F32), we.astype(BF16), w2.astype(BF16)


def _compress_rows(rows3, cw, gain, norm):
    b, nch, width = rows3.shape
    pex, we, w2 = cw
    const = lambda a: pl.BlockSpec(a.shape, lambda i: (0,) * a.ndim)
    return pl.pallas_call(
        functools.partial(_compress_rows_body, norm),
        grid=(b,),
        in_specs=[pl.BlockSpec((1, nch, width), lambda i: (i, 0, 0)), const(pex), const(we), const(w2), const(gain)],
        out_specs=pl.BlockSpec((1, N_KV_HEADS, nch, LANES), lambda i: (i, 0, 0, 0)),
        out_shape=jax.ShapeDtypeStruct((b, N_KV_HEADS, nch, LANES), BF16),
        compiler_params=_cparams("arbitrary"),
        name="compress_rows",
    )(rows3, pex, we, w2, gain)


def _compress_pages(cache_t, pt_flat, n_batch, n_pages, cw, gain, norm):
    _, _, page = cache_t.shape
    nch = n_pages * page // CMP_STRIDE
    pex, we, w2 = cw
    n_seq = COMPRESS_SEQS if n_batch % COMPRESS_SEQS == 0 else 1
    const = lambda a: pl.BlockSpec(a.shape, lambda i, pt: (0,) * a.ndim)
    n_ops = n_seq * n_pages
    grid_spec = pltpu.PrefetchScalarGridSpec(
        num_scalar_prefetch=1, grid=(n_batch // n_seq,),
        in_specs=[pl.BlockSpec(memory_space=pl.ANY), const(pex), const(we), const(w2), const(gain)],
        out_specs=pl.BlockSpec((n_seq, nch, KV_DIM), lambda i, pt: (i, 0, 0)),
        scratch_shapes=[pltpu.VMEM((n_ops * (page // CMP_STRIDE) * CHUNK_PITCH, KV_DIM), F32),
                        pltpu.VMEM((2, n_ops, KV_DIM, page), F32),
                        pltpu.SemaphoreType.DMA((2,))])
    return pl.pallas_call(
        functools.partial(_compress_pages_body, norm, nch, n_seq, n_pages),
        grid_spec=grid_spec,
        out_shape=jax.ShapeDtypeStruct((n_batch, nch, KV_DIM), F32),
        compiler_params=_cparams("arbitrary"),
        name="compress_pages",
    )(pt_flat, cache_t, pex, we, w2, gain)


def _rel_bucket(dist):
    n = jnp.maximum(dist, 0)
    max_exact = NUM_BUCKETS // 2
    nf = jnp.maximum(n, 1).astype(F32)
    large = max_exact + (jnp.log(nf / max_exact) / math.log(MAX_DISTANCE / max_exact)
                         * (NUM_BUCKETS - max_exact)).astype(jnp.int32)
    large = jnp.minimum(large, NUM_BUCKETS - 1)
    return jnp.where(n < max_exact, n, large)


def _bias_by_distance(rel_bias):
    d = jnp.arange(BIAS_DMAX, dtype=jnp.int32)
    f = rel_bias.astype(F32)[_rel_bucket(d)]
    f = (f - f[BIAS_DMAX - 1:BIAS_DMAX]).T
    return jnp.concatenate([f, jnp.full((N_HEADS, 1), NEG_INF, F32)], axis=1)


def _bias_index(d, valid):
    return np.where(valid, np.clip(d, 0, BIAS_DMAX - 1), BIAS_DMAX).astype(np.int32)


def _bias_table(fext, d, valid):
    return jnp.take(fext, jnp.asarray(_bias_index(d, valid)), axis=1)


def _toeplitz_body(n_rows, v_ref, o_ref):
    x = jnp.broadcast_to(v_ref[0], (n_rows, v_ref.shape[2]))
    o_ref[0] = pltpu.roll(x, 0, axis=1, stride=1, stride_axis=0)


def _toeplitz_rows(v, n_rows):
    h, w = v.shape
    return pl.pallas_call(
        functools.partial(_toeplitz_body, n_rows),
        grid=(h,),
        in_specs=[pl.BlockSpec((1, 1, w), lambda i: (i, 0, 0))],
        out_specs=pl.BlockSpec((1, n_rows, w), lambda i: (i, 0, 0)),
        out_shape=jax.ShapeDtypeStruct((h, n_rows, w), F32),
        compiler_params=_cparams("arbitrary"),
        name="toeplitz_rows",
    )(v.reshape(h, 1, w))


def _select_blocks(imp_t, srow, qpos, n_rank):
    qblk = qpos >> 6
    forced = (srow == 0) | (srow == qblk) | (srow == qblk - 1)
    valid = (srow << 6) <= qpos
    imp_t = jnp.where(valid, imp_t + jnp.where(forced, FORCE_SCORE, 0.0), NEG_INF)
    n_rows = imp_t.shape[0]
    assert n_rows % 8 == 0
    slabs = [imp_t[a:a + 8] for a in range(0, n_rows, 8)]
    rows8 = [srow[a:a + 8] for a in range(0, n_rows, 8)]
    cnts = [jnp.zeros(x.shape, jnp.int32) for x in slabs]
    for s in range(n_rank):
        r = imp_t[s:s + 1, :]
        for j, x in enumerate(slabs):
            if 8 * j > s:
                beats = r >= x
            elif 8 * j + 7 <= s:
                beats = r > x
            else:
                beats = (r > x) | ((r == x) & (rows8[j] > s))
            cnts[j] = cnts[j] + jnp.where(beats, 1, 0)
    cnt = jnp.concatenate(cnts, axis=0)
    return jnp.where((cnt < N_SEL) & valid, 1.0, 0.0)


def _exp_pv(s, m, v):
    return _dot(jnp.exp((s - m).astype(BF16)), v)


def _normalize_pv(pv):
    return pv / jnp.maximum(pv[:, HEAD_DIM:HEAD_DIM + 1], 1e-30)


def _attn_prompt_body(kt, n_slc, q_ref, gt_ref, kcmp_ref, vcmp_ref, ks_ref, vs_ref, kw_ref, vw_ref,
                      ctab_ref, ntab_ref, wtab_ref, ovl_ref, o_ref, *score_refs):
    i = pl.program_id(1)
    qb = Q_BLOCK
    near_start = pl.multiple_of(jnp.maximum(i - 1, 0) * qb, qb)
    n_kt = ks_ref.shape[1] // kt
    heads = [_attn_head_setup(h, i, n_slc, q_ref, gt_ref, kcmp_ref, vcmp_ref, ks_ref, vs_ref, kw_ref, vw_ref,
                              ctab_ref, ntab_ref, wtab_ref, ovl_ref) for h in range(N_KV_HEADS)]

    def scores(h, t):
        k0 = pl.multiple_of(jnp.minimum(t, n_kt - 1) * kt, kt)
        return _dot_nt(heads[h]["q_far"], ks_ref[h, pl.ds(k0, kt), :])

    def consume(h, t, s, m_old, acc):
        k0 = pl.multiple_of(jnp.minimum(t, n_kt - 1) * kt, kt)
        m_new = jnp.maximum(m_old, jnp.max(s, axis=-1, keepdims=True))
        return m_new, jnp.exp(m_old - m_new) * acc + _exp_pv(s, m_new, vs_ref[h, pl.ds(k0, kt), :])

    def far_pair(u, carry):
        out = []
        for h in range(N_KV_HEADS):
            sa_ref, sb_ref = score_refs[2 * h], score_refs[2 * h + 1]
            m, acc = carry[h]
            sb_ref[...] = scores(h, 2 * u + 1)
            m, acc = consume(h, 2 * u, sa_ref[...], m, acc)
            sa_ref[...] = scores(h, 2 * u + 2)
            out.append(consume(h, 2 * u + 1, sb_ref[...], m, acc))
        return tuple(out)

    def far_last(_, carry):
        return tuple(consume(h, n_far - 1, score_refs[2 * h][...], *carry[h]) for h in range(N_KV_HEADS))

    n_far = (near_start + kt - 1) // kt
    for h in range(N_KV_HEADS):
        score_refs[2 * h][...] = scores(h, 0)
    carry = lax.fori_loop(0, n_far // 2, far_pair, tuple((hd["m0"], hd["a0"]) for hd in heads))
    final = lax.fori_loop(0, n_far & 1, far_last, carry)

    lane = lax.broadcasted_iota(jnp.int32, (qb, LANES), 1)
    for h, hd in enumerate(heads):
        o_s = _normalize_pv(final[h][1])
        gates, o_c, o_w = hd["gates"], hd["o_c"], hd["o_w"]
        heads_out = []
        for g in range(GROUP):
            c = g * N_BRANCH
            sl = slice(g * qb, (g + 1) * qb)
            heads_out.append(gates[:, c:c + 1] * o_c[sl] + gates[:, c + 1:c + 2] * o_s[sl]
                             + gates[:, c + 2:c + 3] * o_w[sl])
        tiles = [jnp.where(lane < HEAD_DIM, heads_out[2 * j], pltpu.roll(heads_out[2 * j + 1], HEAD_DIM, axis=1))
                 for j in range(GROUP // 2)]
        width = GROUP * HEAD_DIM
        o_ref[:, h * width:(h + 1) * width] = jnp.concatenate(tiles, axis=-1).astype(BF16)


def _attn_head_setup(h, i, n_slc, q_ref, gt_ref, kcmp_ref, vcmp_ref, ks_ref, vs_ref, kw_ref, vw_ref,
                     ctab_ref, ntab_ref, wtab_ref, ovl_ref):
    qb = Q_BLOCK
    rows = GROUP * qb
    hs = slice(h * GROUP, (h + 1) * GROUP)
    n_cmp_pad = kcmp_ref.shape[2]
    first_near_block = (qb // SLC_BLOCK) * jnp.maximum(i - 1, 0)
    near_start = pl.multiple_of(jnp.maximum(i - 1, 0) * qb, qb)
    win_start = pl.multiple_of(jnp.maximum(i * qb - WINDOW, 0), qb)
    band = WINDOW + qb
    gates = gt_ref[h]
    lane = lax.broadcasted_iota(jnp.int32, (qb, LANES), 1)
    q0 = q_ref[0, hs].reshape(rows, LANES)

    s = _dot_nt(q0, kw_ref[h, pl.ds(win_start, band), :]) + wtab_ref[0, hs].reshape(rows, band)
    m = jnp.max(s, axis=-1, keepdims=True)
    m = jnp.where(m == NEG_INF, 0.0, m)
    o_w = _normalize_pv(_exp_pv(s, m, vw_ref[h, pl.ds(win_start, band), :]))

    per_qb = qb // CMP_STRIDE
    ctab = pltpu.roll(ctab_ref[hs].reshape(rows, 2 * n_cmp_pad), i * per_qb, axis=1)[:, n_cmp_pad:]
    s = _dot_nt(q0, kcmp_ref[0, h]) + ctab
    p, l = _softmax_parts(s)
    pn = p / jnp.maximum(l, 1e-30)
    o_c = _dot(pn.astype(BF16), vcmp_ref[0, h])

    psum = pn[0:qb] + pn[qb:2 * qb] + pn[2 * qb:3 * qb] + pn[3 * qb:4 * qb]
    hi, mid, lo = _split3(psum)
    ovl = ovl_ref[...]
    imp_t = _dot_nt(ovl, hi) + _dot_nt(ovl, mid) + _dot_nt(ovl, lo)
    srow = lax.broadcasted_iota(jnp.int32, (n_slc, qb), 0)
    qpos_t = i * qb + lax.broadcasted_iota(jnp.int32, (n_slc, qb), 1)
    sel_t = _select_blocks(imp_t, srow, qpos_t, n_slc)
    sel_t = jnp.concatenate([sel_t, jnp.zeros((LANES - n_slc, qb), F32)], axis=0)
    sel = sel_t.T

    def query_with_mask(keep):
        m = pltpu.roll(jnp.where(keep, 0.0, MASKED), HEAD_DIM, axis=1).astype(BF16)
        m = jnp.concatenate([jnp.where(lane < HEAD_DIM, q0[g * qb:(g + 1) * qb], m) for g in range(GROUP)], axis=0)
        return m

    q_near = query_with_mask(sel > 0.5)
    q_far = query_with_mask((sel > 0.5) & (lane < first_near_block))

    s = _dot_nt(q_near, ks_ref[h, pl.ds(near_start, 2 * qb), :]) + ntab_ref[0, hs].reshape(rows, 2 * qb)
    m0 = jnp.max(s, axis=-1, keepdims=True)
    m0 = jnp.where(m0 == NEG_INF, 0.0, m0)
    a0 = _exp_pv(s, m0, vs_ref[h, pl.ds(near_start, 2 * qb), :])
    return dict(gates=gates, o_w=o_w, o_c=o_c, q_far=q_far, m0=m0, a0=a0)


def _attn_prompt(qp, gates, kcmp, vcmp, ksb, vsb, kwb, vwb, fext, batch, seq):
    qb = Q_BLOCK
    nqb = seq // qb
    n_slc = seq // SLC_BLOCK
    assert n_slc <= LANES - HEAD_DIM
    n_cmp_pad = kcmp.shape[2]
    n_cmp = n_cmp_pad - 1
    kt = min(512, seq)
    assert (seq // kt) % 2 == 0
    band = WINDOW + qb
    iq = np.arange(qb)

    per_qb = qb // CMP_STRIDE
    half = 2 * per_qb
    m = np.arange(-half, half)
    d = iq[:, None] - (m[None, :] * CMP_STRIDE + CMP_BLOCK - 1)
    assert d[:, 0].min() >= MAX_DISTANCE and d[:, -1].max() < 0
    ctab = jnp.concatenate([jnp.zeros((N_HEADS, qb, n_cmp_pad - half), F32), _bias_table(fext, d, d >= 0),
                            jnp.full((N_HEADS, qb, n_cmp_pad - half), NEG_INF, F32)], axis=2)
    assert per_qb * (nqb - 1) < n_cmp_pad
    assert (n_cmp_pad - 1) * CMP_STRIDE + CMP_BLOCK - 1 >= seq and n_cmp == n_cmp_pad - 1

    nv = WINDOW // qb + 1
    kw_ = WINDOW + band
    w = qb + kw_
    assert w % LANES == 0
    m = np.arange(w)
    m = np.where(m < kw_, m, m - w)
    dj = WINDOW - m
    wide = _toeplitz_rows(_bias_table(fext, dj, (dj >= 0) & (dj < WINDOW)), qb)
    wtab = jnp.stack([wide[:, :, WINDOW - qb * v:WINDOW - qb * v + band] for v in range(nv)], axis=0)
    ntab = jnp.stack([wide[:, :, WINDOW - qb * v:WINDOW - qb * v + 2 * qb] for v in range(2)], axis=0)
    assert 2 * qb <= WINDOW

    c_start = np.arange(n_cmp_pad) * CMP_STRIDE
    s_start = np.arange(n_slc) * SLC_BLOCK
    ovl = ((c_start[None, :] < s_start[:, None] + SLC_BLOCK) & (c_start[None, :] + CMP_BLOCK > s_start[:, None])
           & (np.arange(n_cmp_pad) < n_cmp)[None, :])
    ovl = jnp.asarray(ovl, BF16)

    planes = lambda: pl.BlockSpec((N_KV_HEADS, seq, LANES), lambda b, i: (0, b, 0))
    return pl.pallas_call(
        functools.partial(_attn_prompt_body, kt, n_slc),
        grid=(batch, nqb),
        in_specs=[pl.BlockSpec((1, N_HEADS, qb, LANES), lambda b, i: (b, 0, i, 0)),
                  pl.BlockSpec((N_KV_HEADS, qb, LANES), lambda b, i: (0, b * nqb + i, 0)),
                  pl.BlockSpec((1, N_KV_HEADS, n_cmp_pad, LANES), lambda b, i: (b, 0, 0, 0)),
                  pl.BlockSpec((1, N_KV_HEADS, n_cmp_pad, LANES), lambda b, i: (b, 0, 0, 0)),
                  planes(), planes(), planes(), planes(),
                  pl.BlockSpec((N_HEADS, qb, 2 * n_cmp_pad), lambda b, i: (0, 0, 0)),
                  pl.BlockSpec((1, N_HEADS, qb, 2 * qb), lambda b, i: (jnp.minimum(i, 1), 0, 0, 0)),
                  pl.BlockSpec((1, N_HEADS, qb, band), lambda b, i: (jnp.minimum(i, nv - 1), 0, 0, 0)),
                  pl.BlockSpec(ovl.shape, lambda b, i: (0, 0))],
        out_specs=pl.BlockSpec((qb, ATTN_DIM), lambda b, i: (b * nqb + i, 0)),
        out_shape=jax.ShapeDtypeStruct((batch * seq, ATTN_DIM), BF16),
        scratch_shapes=[pltpu.VMEM((GROUP * qb, kt), F32)] * (2 * N_KV_HEADS),
        compiler_params=_cparams("arbitrary", "arbitrary"),
        name="attn_prompt",
    )(qp, gates, kcmp, vcmp, ksb, vsb, kwb, vwb, ctab, ntab, wtab, ovl)


def _attn_sample_body(n_seq, n_pages, ts, past_len, pt_ref, cache_k_ref, cache_v_ref, *refs):
    kbuf, vbuf, ksem, vsem = refs[-4:]
    n_ops = n_seq * n_pages
    slot = _paged_prefetch(pt_ref, [cache_k_ref, cache_v_ref], [kbuf, vbuf], [ksem, vsem], n_ops)
    for b in range(n_seq):
        _attn_sample_one(b, [kbuf.at[slot, b * n_pages + j] for j in range(n_pages)],
                         [vbuf.at[slot, b * n_pages + j] for j in range(n_pages)], refs[:-4], ts, past_len)


def _attn_sample_one(b, kpages, vpages, refs, ts, past_len):
    (q_ref, gt_ref, kcmp_ref, vcmp_ref, ksn_ref, vsn_ref, kwc_ref, vwc_ref, kwn_ref, vwn_ref,
     ctab_ref, stab_ref, sntab_ref, wtab_ref, wntab_ref, ovl_ref, o_ref) = refs
    rows = GROUP * N_KV_HEADS * ts
    rq = N_KV_HEADS * ts
    q = q_ref[b]
    gates = gt_ref[b]

    s = _dot_nt(q, kcmp_ref[b].astype(BF16)) + ctab_ref[...]
    p, l = _softmax_parts(s)
    pn = p / jnp.maximum(l, 1e-30)
    o_c = _dot(pn.astype(BF16), vcmp_ref[b].astype(BF16))

    psum = pn[0:rq]
    for g in range(1, GROUP):
        psum = psum + pn[g * rq:(g + 1) * rq]
    hi, mid, lo = _split3(psum)
    ovl = ovl_ref[...]
    imp = _dot(hi, ovl) + _dot(mid, ovl) + _dot(lo, ovl)
    n_slc = -(-(past_len + ts) // SLC_BLOCK)
    blk = lax.broadcasted_iota(jnp.int32, (rq, LANES), 1)
    qpos = past_len + (lax.broadcasted_iota(jnp.int32, (rq, LANES), 0) & (ts - 1))
    qblk = qpos >> 6
    forced = (blk == 0) | (blk == qblk) | (blk == qblk - 1)
    valid = ((blk << 6) <= qpos) & (blk < n_slc)
    imp = jnp.where(valid, imp + jnp.where(forced, FORCE_SCORE, 0.0), NEG_INF)
    cnt = jnp.zeros((rq, LANES), jnp.int32)
    for sidx in range(n_slc):
        r = imp[:, sidx:sidx + 1]
        beats = (r > imp) | ((r == imp) & (blk > sidx))
        cnt = cnt + jnp.where(beats, 1, 0)
    sel = jnp.where((cnt < N_SEL) & valid, 1.0, 0.0)
    sel = jnp.concatenate([sel] * GROUP, axis=0)

    kc_t = jnp.concatenate([p_[...] for p_ in kpages], axis=1).astype(BF16)
    vc_t = jnp.concatenate([p_[...] for p_ in vpages], axis=1).astype(BF16)
    lane_r = lax.broadcasted_iota(jnp.int32, (rows, LANES), 1)
    per_tile = LANES // SLC_BLOCK
    tiles = []
    for j in range(past_len // LANES):
        m = sel[:, per_tile * j:per_tile * j + 1]
        for c in range(1, per_tile):
            m = jnp.where(lane_r >= c * SLC_BLOCK, sel[:, per_tile * j + c:per_tile * j + c + 1], m)
        tiles.append(m)
    mexp = jnp.concatenate(tiles, axis=1)
    s1 = jnp.where(mexp > 0.5, _dot(q, kc_t) + stab_ref[...], NEG_INF)
    last = sel[:, n_slc - 1:n_slc]
    s2 = jnp.where(last > 0.5, _dot_nt(q, ksn_ref[b].astype(BF16)) + sntab_ref[...], NEG_INF)
    m = jnp.maximum(jnp.max(s1, axis=-1, keepdims=True), jnp.max(s2, axis=-1, keepdims=True))
    m = jnp.where(m == NEG_INF, 0.0, m)
    p1 = jnp.exp(s1 - m)
    p2 = jnp.exp(s2 - m)
    l = jnp.sum(p1, axis=-1, keepdims=True) + jnp.sum(p2, axis=-1, keepdims=True)
    o_s = (_dot_nt(p1.astype(BF16), vc_t) + _dot(p2.astype(BF16), vsn_ref[b].astype(BF16))) / jnp.maximum(l, 1e-30)

    s1 = _dot(q, kwc_ref[b].astype(BF16)) + wtab_ref[...]
    s2 = _dot_nt(q, kwn_ref[b].astype(BF16)) + wntab_ref[...]
    m = jnp.maximum(jnp.max(s1, axis=-1, keepdims=True), jnp.max(s2, axis=-1, keepdims=True))
    m = jnp.where(m == NEG_INF, 0.0, m)
    p1 = jnp.exp(s1 - m)
    p2 = jnp.exp(s2 - m)
    l = jnp.sum(p1, axis=-1, keepdims=True) + jnp.sum(p2, axis=-1, keepdims=True)
    o_w = (_dot_nt(p1.astype(BF16), vwc_ref[b].astype(BF16))
           + _dot(p2.astype(BF16), vwn_ref[b].astype(BF16))) / jnp.maximum(l, 1e-30)

    o_ref[b] = gates[:, 0:1] * o_c + gates[:, 1:2] * o_s + gates[:, 2:3] * o_w


def _attn_sample(q_s, gates_s, kcmp, vcmp, ks_new, vs_new, kw_new, vw_new, cache_ks, cache_vs,
                 cache_kw, cache_vw, pt_flat, fext, n_batch, ts, n_pages, page):
    past_len = n_pages * page
    w_buf = cache_kw.shape[2]
    rows = GROUP * N_KV_HEADS * ts
    n_new = 8
    n_cmp_pad = kcmp.shape[1]
    n_cmp = n_cmp_pad - 1
    n_slc = -(-(past_len + ts) // SLC_BLOCK)

    q5 = q_s.reshape(n_batch, ts, N_KV_HEADS, GROUP, HEAD_DIM).transpose(0, 3, 2, 1, 4)
    eye = jnp.eye(N_KV_HEADS, dtype=q_s.dtype)
    qr = jnp.einsum("bghtd,hk->bghtkd", q5, eye).reshape(n_batch, rows, LANES).astype(BF16)
    g5 = gates_s[:, :N_HEADS * N_BRANCH].reshape(n_batch, ts, N_KV_HEADS, GROUP, N_BRANCH).transpose(0, 3, 2, 1, 4)
    gr = jnp.pad(g5.reshape(n_batch, rows, N_BRANCH), ((0, 0), (0, 0), (0, LANES - N_BRANCH)))
    pad_new = lambda a: jnp.pad(a.reshape(n_batch, ts, KV_DIM), ((0, 0), (0, n_new - ts), (0, 0)))
    ks_new, vs_new, kw_new, vw_new = map(pad_new, (ks_new, vs_new, kw_new, vw_new))

    g_i, h_i, t_i = np.meshgrid(np.arange(GROUP), np.arange(N_KV_HEADS), np.arange(ts), indexing="ij")
    head = (h_i * GROUP + g_i).reshape(rows)
    tq = t_i.reshape(rows)
    pos_q = past_len + tq

    f_rows = fext[jnp.asarray(head)]

    def table(d, valid):
        return jnp.take_along_axis(f_rows, jnp.asarray(_bias_index(d, valid)), axis=1)

    nn = np.arange(n_cmp_pad)
    d = pos_q[:, None] - (nn[None, :] * CMP_STRIDE + CMP_BLOCK - 1)
    ctab = table(d, (d >= 0) & (nn < n_cmp)[None, :])
    near = np.arange(past_len - MAX_DISTANCE, past_len)
    d = pos_q[:, None] - near[None, :]
    assert past_len >= MAX_DISTANCE and d.min() >= 0
    stab = jnp.concatenate([jnp.zeros((rows, past_len - MAX_DISTANCE), F32), table(d, d >= 0)], axis=1)
    jn = np.arange(n_new)
    d = tq[:, None] - jn[None, :]
    sntab = table(d, (d >= 0) & (jn < ts)[None, :])
    pos_w = past_len - w_buf + np.arange(w_buf)
    d = pos_q[:, None] - pos_w[None, :]
    wtab = table(d, (d >= 0) & (d < WINDOW) & (pos_w >= 0)[None, :])
    wntab = table(tq[:, None] - jn[None, :], (tq[:, None] >= jn[None, :]) & (jn < ts)[None, :])

    c_start = nn * CMP_STRIDE
    s_start = np.arange(LANES) * SLC_BLOCK
    ovl = jnp.asarray((c_start[:, None] < s_start[None, :] + SLC_BLOCK) & (c_start[:, None] + CMP_BLOCK > s_start[None, :])
                      & (nn < n_cmp)[:, None] & (np.arange(LANES) < n_slc)[None, :], BF16)

    n_seq = SAMPLE_SEQS if n_batch % SAMPLE_SEQS == 0 else 1
    n_ops = n_seq * n_pages
    const = lambda a: pl.BlockSpec(a.shape, lambda b, pt: (0,) * a.ndim)
    per_b = lambda a: pl.BlockSpec((n_seq,) + a.shape[1:], lambda b, pt: (b,) + (0,) * (a.ndim - 1))
    any_spec = pl.BlockSpec(memory_space=pl.ANY)
    small = [qr, gr, kcmp, vcmp, ks_new, vs_new, cache_kw, cache_vw, kw_new, vw_new]
    consts = [ctab, stab, sntab, wtab, wntab, ovl]
    page_buf = pltpu.VMEM((2, n_ops, KV_DIM, page), F32)
    grid_spec = pltpu.PrefetchScalarGridSpec(
        num_scalar_prefetch=1, grid=(n_batch // n_seq,),
        in_specs=[any_spec, any_spec] + [per_b(a) for a in small] + [const(a) for a in consts],
        out_specs=pl.BlockSpec((n_seq, rows, LANES), lambda b, pt: (b, 0, 0)),
        scratch_shapes=[page_buf, page_buf, pltpu.SemaphoreType.DMA((2,)), pltpu.SemaphoreType.DMA((2,))])
    o = pl.pallas_call(
        functools.partial(_attn_sample_body, n_seq, n_pages, ts, past_len),
        grid_spec=grid_spec,
        out_shape=jax.ShapeDtypeStruct((n_batch, rows, LANES), F32),
        compiler_params=_cparams("arbitrary"),
        name="attn_sample",
    )(pt_flat, cache_ks, cache_vs, *small, *consts)
    o6 = o.reshape(n_batch, GROUP, N_KV_HEADS, ts, N_KV_HEADS, HEAD_DIM)
    o5 = jnp.stack([o6[:, :, h, :, h] for h in range(N_KV_HEADS)], axis=2)
    return o5.transpose(0, 3, 2, 1, 4).reshape(n_batch * ts, ATTN_DIM).astype(BF16)


def _post1_body(tm, x_ref, co_ref, at_ref, wo_ref, nf_ref, wr_ref, br_ref, tri_ref, run0_ref,
                h_ref, hn_ref, rt_ref, cnt_ref, run_ref):
    @pl.when(pl.program_id(0) == 0)
    def _():
        run_ref[...] = run0_ref[...]

    h = x_ref[...] + _dot(co_ref[...], wo_ref[0:CONV_DIM]) + _dot(at_ref[...], wo_ref[CONV_DIM:CONV_DIM + ATTN_DIM])
    hn = _rms(h, nf_ref[...])
    h_ref[...] = h
    _store_token_tiles(hn_ref, hn, tm)

    hi = hn.astype(BF16)
    lo = (hn - hi.astype(F32)).astype(BF16)
    wr = wr_ref[...]
    whi = wr.astype(BF16)
    wlo = (wr - whi.astype(F32)).astype(BF16)
    both = _dot(hi, jnp.concatenate([whi, wlo], axis=1))
    logits = both[:, :LANES] + both[:, LANES:] + _dot(lo, whi) + br_ref[...]

    lane_i = lax.broadcasted_iota(jnp.int32, (tm, LANES), 1)
    lane = lane_i.astype(F32)
    big = float(LANES)
    gmask = (lane_i >= ROUTER_GROUP_LANE) & (lane_i < ROUTER_GROUP_LANE + N_GROUPS)
    lg = jnp.where(gmask, logits, NEG_INF)
    eg = jnp.exp(lg - jnp.max(lg, axis=-1, keepdims=True))
    pg = eg / jnp.sum(eg, axis=-1, keepdims=True)
    gw = jnp.max(pg, axis=-1, keepdims=True)
    grp = jnp.min(jnp.where(gmask & (pg == gw), lane, big), axis=-1, keepdims=True) - ROUTER_GROUP_LANE

    group_of_lane = (lane_i >> 3).astype(F32)
    emask = (lane_i < N_EXPERTS) & (group_of_lane == grp)
    le = jnp.where(emask, logits, NEG_INF)
    ee = jnp.exp(le - jnp.max(le, axis=-1, keepdims=True))
    pe = jnp.where(emask, ee / jnp.sum(ee, axis=-1, keepdims=True), -1.0)
    v1 = jnp.max(pe, axis=-1, keepdims=True)
    i1 = jnp.min(jnp.where(pe == v1, lane, big), axis=-1, keepdims=True)
    pe2 = jnp.where(lane == i1, -1.0, pe)
    v2 = jnp.max(pe2, axis=-1, keepdims=True)
    i2 = jnp.min(jnp.where(pe2 == v2, lane, big), axis=-1, keepdims=True)
    tot = v1 + v2
    w1 = v1 / tot * gw
    w2 = v2 / tot * gw

    oh1 = jnp.where(lane == i1, 1.0, 0.0)
    oh2 = jnp.where(lane == i2, 1.0, 0.0)
    both = oh1 + oh2
    before = _dot(tri_ref[...], both.astype(BF16)) + run_ref[0:1]
    r1 = jnp.sum(oh1 * before, axis=-1, keepdims=True)
    r2 = jnp.sum(oh2 * before, axis=-1, keepdims=True)
    run = run_ref[0:1] + jnp.sum(both, axis=0, keepdims=True)
    run_ref[...] = jnp.broadcast_to(run, run_ref.shape)
    cnt_ref[...] = jnp.broadcast_to(run, cnt_ref.shape)

    rt = jnp.where(lane_i == 0, i1, 0.0)
    rt = jnp.where(lane_i == 1, i2, rt)
    rt = jnp.where(lane_i == 2, r1, rt)
    rt = jnp.where(lane_i == 3, r2, rt)
    rt = jnp.where(lane_i == 4, w1, rt)
    rt = jnp.where(lane_i == 5, w2, rt)
    rt_ref[...] = rt


def _post1(x2d, co, at, wo, nf, wr, br, run0, tm):
    n, d = x2d.shape
    tri = jnp.asarray(np.tril(np.ones((tm, tm), np.float32), -1), BF16)
    rows = lambda w: pl.BlockSpec((tm, w), lambda i: (i, 0))
    const = lambda a: pl.BlockSpec(a.shape, lambda i: (0,) * a.ndim)
    return pl.pallas_call(
        functools.partial(_post1_body, tm),
        grid=(n // tm,),
        in_specs=[rows(d), rows(CONV_DIM), rows(ATTN_DIM), const(wo), const(nf), const(wr), const(br),
                  const(tri), const(run0)],
        out_specs=[rows(d), pl.BlockSpec((tm * TOKEN_TILE_ROWS, LANES), lambda i: (i, 0)), rows(LANES),
                   pl.BlockSpec((8, LANES), lambda i: (0, 0))],
        out_shape=[jax.ShapeDtypeStruct((n, d), F32), jax.ShapeDtypeStruct((n * TOKEN_TILE_ROWS, LANES), F32),
                   jax.ShapeDtypeStruct((n, LANES), F32), jax.ShapeDtypeStruct((8, LANES), F32)],
        scratch_shapes=[pltpu.VMEM((8, LANES), F32)],
        compiler_params=_cparams("arbitrary"),
        name="post1",
    )(x2d, co, at, wo, nf, wr, br, tri, run0)


def _token_copy(src_ref, dst_ref, s, d, sem):
    r = TOKEN_TILE_ROWS
    return pltpu.make_async_copy(src_ref.at[pl.ds(pl.multiple_of(s * r, r), r)],
                                 dst_ref.at[pl.ds(pl.multiple_of(d * r, r), r)], sem)


def _scatter_rows_body(ts, dest_ref, src_ref, init_ref, out_ref, sem):
    del init_ref
    base = pl.program_id(0) * (2 * ts)

    def issue(t, _):
        for k in range(2):
            _token_copy(src_ref, out_ref, t, dest_ref[base + 2 * t + k], sem).start(priority=k)
        return 0

    lax.fori_loop(0, ts, issue, 0, unroll=DMA_ISSUE_UNROLL)
    for _ in range(2):
        pltpu.make_async_copy(src_ref, out_ref.at[pl.ds(0, ts * TOKEN_TILE_ROWS)], sem).wait()


def _scatter_rows(dest, src, slots):
    n_tok = dest.shape[0] // 2
    ts = min(SCATTER_TOKENS, n_tok)
    assert n_tok % ts == 0
    any_spec = pl.BlockSpec(memory_space=pl.ANY)
    return pl.pallas_call(
        functools.partial(_scatter_rows_body, ts),
        grid_spec=pltpu.PrefetchScalarGridSpec(
            num_scalar_prefetch=1, grid=(n_tok // ts,),
            in_specs=[pl.BlockSpec((ts * TOKEN_TILE_ROWS, LANES), lambda i, dest: (i, 0)), any_spec],
            out_specs=any_spec, scratch_shapes=[pltpu.SemaphoreType.DMA(())]),
        out_shape=jax.ShapeDtypeStruct(slots.shape, slots.dtype),
        input_output_aliases={2: 0},
        compiler_params=pltpu.CompilerParams(dimension_semantics=("arbitrary",)),
        name="scatter_rows",
    )(dest, src, slots)


def _experts_body(be_ref, nu_ref, x_ref, wg_ref, wu_ref, wd_ref, o_ref, wg_s, wu_s, wd_s):
    i = pl.program_id(0)

    @pl.when(i < nu_ref[0])
    def _():
        prev = be_ref[jnp.maximum(i - 1, 0)]

        @pl.when((i == 0) | (be_ref[i] != prev))
        def _():
            wg_s[...] = wg_ref[0].astype(BF16)
            wu_s[...] = wu_ref[0].astype(BF16)
            wd_s[...] = wd_ref[0].astype(BF16)

        x = _load_token_tiles(x_ref, EXPERT_ROWS, TOKEN_TILE_ROWS).astype(BF16)
        g = _dot(x, wg_s[...])
        u = _dot(x, wu_s[...])
        a = g * _sigmoid(g) * u
        _store_token_tiles(o_ref, _dot(a.astype(BF16), wd_s[...]), EXPERT_ROWS)

    @pl.when(i >= nu_ref[0])
    def _():
        o_ref[...] = jnp.zeros_like(o_ref)


def _experts(blk_expert, n_used, xs, wg, wu, wd):
    blk_rows = EXPERT_ROWS * TOKEN_TILE_ROWS
    n_blk = xs.shape[0] // blk_rows
    _, d, de = wg.shape
    xmap = lambda i, be, nu: (jnp.minimum(i, jnp.maximum(nu[0] - 1, 0)), 0)
    wmap = lambda i, be, nu: (be[jnp.minimum(i, jnp.maximum(nu[0] - 1, 0))], 0, 0)
    grid_spec = pltpu.PrefetchScalarGridSpec(
        num_scalar_prefetch=2, grid=(n_blk,),
        in_specs=[pl.BlockSpec((blk_rows, LANES), xmap), pl.BlockSpec((1, d, de), wmap),
                  pl.BlockSpec((1, d, de), wmap), pl.BlockSpec((1, de, d), wmap)],
        out_specs=pl.BlockSpec((blk_rows, LANES), lambda i, be, nu: (i, 0)),
        scratch_shapes=[pltpu.VMEM((d, de), BF16), pltpu.VMEM((d, de), BF16), pltpu.VMEM((de, d), BF16)])
    return pl.pallas_call(
        _experts_body, grid_spec=grid_spec,
        out_shape=jax.ShapeDtypeStruct(xs.shape, F32),
        compiler_params=_cparams("arbitrary"),
        name="experts",
    )(blk_expert, n_used, xs, wg, wu, wd)


def _post2_body(tm, dest_ref, h_ref, rt_ref, p_ref, yb_ref, wple_ref, wpg_ref, bpg_ref, np_ref, o_ref, buf, sem):
    i = pl.program_id(0)
    n = pl.num_programs(0)

    def fetch(step, slot):
        base = step * (2 * tm)

        def issue(t, _):
            for k in range(2):
                _token_copy(yb_ref, buf.at[slot], dest_ref[base + 2 * t + k], k * tm + t, sem.at[slot]).start(priority=k)
            return 0

        lax.fori_loop(0, tm, issue, 0, unroll=DMA_ISSUE_UNROLL)

    @pl.when(i == 0)
    def _():
        fetch(0, 0)

    @pl.when(i + 1 < n)
    def _():
        fetch(i + 1, (i + 1) & 1)

    slot = i & 1

    pltpu.make_async_copy(yb_ref.at[pl.ds(0, 2 * tm * TOKEN_TILE_ROWS)], buf.at[slot], sem.at[slot]).wait()
    rt = rt_ref[...]
    y0 = _load_token_tiles(buf.at[slot], tm, TOKEN_TILE_ROWS)
    y1 = _load_token_tiles(buf.at[slot], tm, TOKEN_TILE_ROWS, first=tm * TOKEN_TILE_ROWS)
    h = h_ref[...] + (y0 * rt[:, 4:5] + y1 * rt[:, 5:6])
    gate = _sigmoid(_dot(_rms(h, np_ref[...]).astype(BF16), wpg_ref[...]) + bpg_ref[...])
    o_ref[...] = h + gate * _dot(p_ref[...].astype(BF16), wple_ref[...])


def _post2(dest, h, rt, p2d, yb, wple, wpg, bpg, npl, tm):
    n, d = h.shape
    rows = lambda w: pl.BlockSpec((tm, w), lambda i, dest: (i, 0))
    const = lambda a: pl.BlockSpec(a.shape, lambda i, dest: (0,) * a.ndim)
    grid_spec = pltpu.PrefetchScalarGridSpec(
        num_scalar_prefetch=1, grid=(n // tm,),
        in_specs=[rows(d), rows(LANES), rows(p2d.shape[1]), pl.BlockSpec(memory_space=pl.ANY), const(wple),
                  const(wpg), const(bpg), const(npl)],
        out_specs=rows(d),
        scratch_shapes=[pltpu.VMEM((2, 2 * tm * TOKEN_TILE_ROWS, LANES), F32), pltpu.SemaphoreType.DMA((2,))])
    return pl.pallas_call(
        functools.partial(_post2_body, tm),
        grid_spec=grid_spec,
        out_shape=jax.ShapeDtypeStruct((n, d), F32),
        compiler_params=_cparams("arbitrary"),
        name="post2",
    )(dest, h, rt, p2d, yb, wple, wpg, bpg, npl)


def _row_tile(n, cap=512):
    t = min(cap, n)
    assert n % t == 0 and t % 8 == 0
    return t


def kernel(x_prompt, x_sample, p_prompt, p_sample, cache_k_cmp, cache_v_cmp, cache_k_slc, cache_v_slc, cache_k_win, cache_v_win, state_conv, page_table, w_in, w_out, conv_w, norm_mix, norm_ffn, norm_ple, q_norm, k_norm, cmp_pe_k, cmp_w1_k, cmp_w2_k, cmp_pe_v, cmp_w1_v, cmp_w2_v, rel_bias, w_router_group, b_router_group, w_router_expert, b_router_expert, w_exp_gate, w_exp_up, w_exp_down, w_ple, w_ple_gate, b_ple_gate):
    assert w_in.shape[0] == 1, "single-layer step"
    bp, t, d = x_prompt.shape
    bs, ts, _ = x_sample.shape
    n_pages = page_table.shape[1]
    page = cache_k_cmp.shape[2]
    past_len = n_pages * page
    w_buf = cache_k_win.shape[2]
    n_phys = cache_k_cmp.shape[1]
    assert t % Q_BLOCK == 0 and t >= WINDOW + Q_BLOCK and page % CMP_STRIDE == 0 and ts == 4 and d == D_MODEL
    assert past_len % SLC_BLOCK == 0
    np_rows, ns_rows = bp * t, bs * ts

    row = lambda v: v.reshape(1, -1).astype(F32)
    w_in_b = jnp.pad(w_in[0], ((0, 0), (0, Z_COLS - w_in.shape[2]))).astype(BF16)
    qn = row(jnp.tile(q_norm[0], N_HEADS))
    kn1 = row(jnp.tile(k_norm[0, 1], N_KV_HEADS))
    kn2 = row(jnp.tile(k_norm[0, 2], N_KV_HEADS))
    bd = jnp.asarray(np.kron(np.eye(N_HEADS), np.ones((HEAD_DIM, HEAD_DIM))), BF16)
    pw = (row(norm_mix[0]), w_in_b, qn, kn1, kn2, conv_w[0].astype(F32), bd)
    cw_k = _compress_weights(cmp_pe_k[0], cmp_w1_k[0], cmp_w2_k[0])
    cw_v = _compress_weights(cmp_pe_v[0], cmp_w1_v[0], cmp_w2_v[0])
    kn0 = row(k_norm[0, 0])
    fext = _bias_by_distance(rel_bias)
    pt_flat = page_table.reshape(-1).astype(jnp.int32)
    wr = jnp.zeros((d, LANES), F32).at[:, :N_EXPERTS].set(w_router_expert[0])
    wr = wr.at[:, ROUTER_GROUP_LANE:ROUTER_GROUP_LANE + N_GROUPS].set(w_router_group[0])
    br = jnp.zeros((1, LANES), F32).at[0, :N_EXPERTS].set(b_router_expert[0])
    br = br.at[0, ROUTER_GROUP_LANE:ROUTER_GROUP_LANE + N_GROUPS].set(b_router_group[0])
    wo_b = w_out[0].astype(BF16)
    wple_b = w_ple[0].astype(BF16)
    wpg_b = w_ple_gate[0].astype(BF16)

    tm_p = _row_tile(t, cap=1024)
    (co_p, q_p, kw_p, vw_p, gt_p, cs_p, ksb, vsb, kwb, vwb, kc_t, vc_t, ks_t, vs_t, kc_x, vc_x) = _project(
        x_prompt.reshape(np_rows, d), bp, t, tm_p, pw)
    chunk_w = CMP_STRIDE * KV_DIM
    kcmp_p = _compress_rows(kc_x.reshape(bp, t // CMP_STRIDE, chunk_w), cw_k, kn0, True)
    vcmp_p = _compress_rows(vc_x.reshape(bp, t // CMP_STRIDE, chunk_w), cw_v, kn0, False)
    at_p = _attn_prompt(q_p, gt_p, kcmp_p, vcmp_p, ksb, vsb, kwb, vwb, fext, bp, t)

    st = state_conv[0].astype(F32)
    s0 = jnp.repeat(st[:, 0], ts, axis=0)
    s1 = jnp.repeat(st[:, 1], ts, axis=0)
    (co_s, q_s, kc_s, vc_s, ks_s, vs_s, kw_s, vw_s, gt_s, u_s) = _project(
        x_sample.reshape(ns_rows, d), bs, ts, ns_rows, pw, state=(s0, s1))
    feature_major = lambda c, n, rows_: jnp.transpose(c[0], (0, 2, 3, 1)).reshape(n, KV_DIM, rows_)
    kcmp_s = _compress_pages(feature_major(cache_k_cmp, n_phys, page), pt_flat, bs, n_pages, cw_k, kn0, True)
    vcmp_s = _compress_pages(feature_major(cache_v_cmp, n_phys, page), pt_flat, bs, n_pages, cw_v, kn0, False)
    at_s = _attn_sample(q_s, gt_s, kcmp_s, vcmp_s, ks_s, vs_s, kw_s, vw_s,
                        feature_major(cache_k_slc, n_phys, page), feature_major(cache_v_slc, n_phys, page),
                        feature_major(cache_k_win, bs, w_buf), feature_major(cache_v_win, bs, w_buf),
                        pt_flat, fext, bs, ts, n_pages, page)

    tp1 = _row_tile(np_rows)
    ts1 = _row_tile(ns_rows)
    nf = row(norm_ffn[0])
    h_p, hn_p, rt_p, cnt_p = _post1(x_prompt.reshape(np_rows, d), co_p, at_p, wo_b, nf, wr, br,
                                    jnp.zeros((8, LANES), F32), tp1)
    h_s, hn_s, rt_s, cnt_s = _post1(x_sample.reshape(ns_rows, d), co_s, at_s, wo_b, nf, wr, br, cnt_p, ts1)

    counts = cnt_s[0, :N_EXPERTS].astype(jnp.int32)
    padded = (counts + EXPERT_ROWS - 1) // EXPERT_ROWS * EXPERT_ROWS
    pad_end = jnp.cumsum(padded)
    pad_start = pad_end - padded
    n_assign = 2 * (np_rows + ns_rows)
    n_blk = (n_assign + N_EXPERTS * (EXPERT_ROWS - 1) + EXPERT_ROWS - 1) // EXPERT_ROWS
    blk_first = jnp.arange(n_blk, dtype=jnp.int32) * EXPERT_ROWS
    blk_expert = jnp.minimum(jnp.sum((pad_end[None, :] <= blk_first[:, None]).astype(jnp.int32), axis=1),
                             N_EXPERTS - 1)
    n_used = (pad_end[-1:] // EXPERT_ROWS).astype(jnp.int32)

    def dest_of(rt):
        e = rt[:, 0:2].astype(jnp.int32).reshape(-1)
        return pad_start[e] + rt[:, 2:4].astype(jnp.int32).reshape(-1)

    dest_p = dest_of(rt_p)
    dest_s = dest_of(rt_s)

    xs = jnp.zeros((n_blk * EXPERT_ROWS * TOKEN_TILE_ROWS, LANES), F32)
    xs = _scatter_rows(dest_p, hn_p, xs)
    xs = _scatter_rows(dest_s, hn_s, xs)
    yb = _experts(blk_expert, n_used, xs, w_exp_gate[0], w_exp_up[0], w_exp_down[0])

    bpg = row(b_ple_gate[0])
    npl = row(norm_ple[0])
    y_p = _post2(dest_p, h_p, rt_p, p_prompt[0].reshape(np_rows, -1), yb, wple_b, wpg_b, bpg, npl, tp1)
    y_s = _post2(dest_s, h_s, rt_s, p_sample[0].reshape(ns_rows, -1), yb, wple_b, wpg_b, bpg, npl, ts1)

    kv5 = lambda a, b, s: a.reshape(1, b, s, N_KV_HEADS, HEAD_DIM)
    wp = min(WINDOW, t)
    win_p = lambda a: kv5(a.reshape(bp, t, KV_DIM)[:, t - wp:], bp, wp)
    win_s = lambda c, new: jnp.concatenate([c[0], new.reshape(bs, ts, N_KV_HEADS, HEAD_DIM)], axis=1)[None, :, ts:]
    conv_p = cs_p[:, 8 - (CONV_K - 1):][None]
    conv_s = u_s.reshape(bs, ts, CONV_DIM)[:, ts - (CONV_K - 1):][None]
    from_t = lambda a: jnp.transpose(a.reshape(bp, N_KV_HEADS, HEAD_DIM, t), (0, 3, 1, 2))[None]
    return (y_p.reshape(bp, t, d), y_s.reshape(bs, ts, d),
            from_t(kc_t), from_t(vc_t), from_t(ks_t), from_t(vs_t), win_p(kw_p), win_p(vw_p), conv_p,
            kv5(kc_s, bs, ts), kv5(vc_s, bs, ts), kv5(ks_s, bs, ts), kv5(vs_s, bs, ts),
            win_s(cache_k_win, kw_s), win_s(cache_v_win, vw_s), conv_s)
```
